```python
import math
import jax
import jax.numpy as jnp
from jax import lax
import numpy as np

D_MODEL = 1024
BATCH = 2
SEQ = 8192
DEPTH = 1

GRID_W = 64
CTX_LEN = 256
EPS = 1e-6

MIX_WIDTH = D_MODEL
SSD_INNER = MIX_WIDTH // 2
SSD_HEAD_DIM = 64
SSD_HEADS = SSD_INNER // SSD_HEAD_DIM
SSD_GROUPS = 2
SSD_HEADS_PER_GROUP = SSD_HEADS // SSD_GROUPS
SSD_STATE = 128
CONV_WIDTH = 5
CONV_DIM = SSD_INNER + 2 * SSD_GROUPS * SSD_STATE
CHUNK = 128
DT_MIN = 0.001
DT_MAX = 0.1
ATTN_WIDTH = MIX_WIDTH - SSD_INNER
HEAD_DIM = 64
ATTN_Q_HEADS = ATTN_WIDTH // HEAD_DIM
ATTN_KV_HEADS = 2
Q_PER_KV = ATTN_Q_HEADS // ATTN_KV_HEADS
KV_WIDTH = ATTN_KV_HEADS * HEAD_DIM
WINDOW = 128
BLOCK = 128
ROPE_BASE = 10000.0
IN_COLS = SSD_INNER + CONV_DIM + 2 * SSD_HEADS + ATTN_WIDTH + 2 * KV_WIDTH
N_GROUPS = 4
EXPERTS_PER_GROUP = 8
N_EXPERTS = N_GROUPS * EXPERTS_PER_GROUP
TOP_K = 2
EXPERT_DIM = D_MODEL // 2
MOE_BLOCK = 128

kernel_name = 'hymba_ssd_swa_hmoe_prefix_dit'


def rms_norm(x, gain):
    xf = x.astype(jnp.float32)
    y = xf * lax.rsqrt(jnp.mean(xf * xf, axis=-1, keepdims=True) + EPS)
    return (y * gain).astype(x.dtype)


def _flip(t):
    return jnp.flip(t, axis=1)


def split_projection(p):
    sizes = [SSD_INNER, CONV_DIM, 2 * SSD_HEADS, ATTN_WIDTH, KV_WIDTH, KV_WIDTH]
    return jnp.split(p, np.cumsum(sizes)[:-1].tolist(), axis=-1)


def centred_dwconv(u, w, b):
    out = lax.conv_general_dilated(
        u, w[:, None, :].astype(u.dtype), window_strides=(1,),
        padding=((CONV_WIDTH // 2, CONV_WIDTH // 2),),
        dimension_numbers=('NWC', 'WIO', 'NWC'), feature_group_count=u.shape[-1])
    return out + b.astype(u.dtype)


def ssd_inputs(xbc, dt_raw, conv_w, conv_b, dt_bias):
    b, L = xbc.shape[:2]
    u = jax.nn.silu(centred_dwconv(xbc, conv_w, conv_b))
    xs, bm, cm = jnp.split(u, [SSD_INNER, SSD_INNER + SSD_GROUPS * SSD_STATE], axis=-1)
    xs = xs.reshape(b, L, SSD_HEADS, SSD_HEAD_DIM)
    bm = bm.reshape(b, L, SSD_GROUPS, SSD_STATE)
    cm = cm.reshape(b, L, SSD_GROUPS, SSD_STATE)
    dt = jax.nn.softplus(dt_raw.astype(jnp.float32).reshape(b, L, 2, SSD_HEADS) + dt_bias)
    return xs, bm, cm, dt


def ssd_scan(xs, dt, a, bm, cm, h0, need_y=True):
    f32 = jnp.float32
    b, L = xs.shape[:2]
    nc = L // CHUNK
    G, R, P, N = SSD_GROUPS, SSD_HEADS_PER_GROUP, SSD_HEAD_DIM, SSD_STATE
    xg = xs.astype(f32).reshape(b, nc, CHUNK, G, R, P)
    dtc = dt.astype(f32).reshape(b, nc, CHUNK, G, R)
    bc = bm.astype(f32).reshape(b, nc, CHUNK, G, N)
    a_cs = jnp.cumsum(dtc * a.reshape(G, R), axis=2)
    a_last = a_cs[:, :, -1]
    dtx = dtc[..., None] * xg
    states = jnp.einsum('bcjgn,bcjgr,bcjgrp->bcgrpn', bc, jnp.exp(a_last[:, :, None] - a_cs), dtx)

    def step(h, inp):
        s, al = inp
        return jnp.exp(al)[..., None, None] * h + s, h

    h_final, h_prev = lax.scan(step, h0.astype(f32),
                               (jnp.moveaxis(states, 1, 0), jnp.moveaxis(a_last, 1, 0)))
    if not need_y:
        return None, h_final
    cc = cm.astype(f32).reshape(b, nc, CHUNK, G, N)
    tri = jnp.tril(jnp.ones((CHUNK, CHUNK), dtype=bool))
    seg = a_cs[:, :, :, None] - a_cs[:, :, None, :]
    decay = jnp.exp(jnp.where(tri[:, :, None, None], seg, -jnp.inf))
    cb = jnp.einsum('bcign,bcjgn->bcijg', cc, bc)
    y_diag = jnp.einsum('bcijgr,bcjgrp->bcigrp', cb[..., None] * decay, dtx)
    y_off = jnp.einsum('bcign,cbgrpn->bcigrp', cc, h_prev) * jnp.exp(a_cs)[..., None]
    return (y_diag + y_off).reshape(b, L, SSD_HEADS, P), h_final


def ssd_output(y, xs, z, d_skip, gain):
    b, L = xs.shape[:2]
    y = y + d_skip.astype(jnp.float32)[:, None] * xs.astype(jnp.float32)
    y = y.reshape(b, L, SSD_INNER) * jax.nn.silu(z.astype(jnp.float32))
    return rms_norm(y, gain).astype(z.dtype)


def axial_rope_tables(rows):
    row = jnp.repeat(jnp.arange(rows), GRID_W)
    col = jnp.tile(jnp.arange(GRID_W), rows)
    n_freq = HEAD_DIM // 4
    inv_freq = ROPE_BASE ** (-jnp.arange(n_freq, dtype=jnp.float32) / n_freq)
    ang = jnp.concatenate([row[:, None] * inv_freq, col[:, None] * inv_freq], axis=-1)
    ang = jnp.concatenate([ang, ang], axis=-1)
    return jnp.cos(ang), jnp.sin(ang)


def apply_rope(t, cos, sin):
    half = HEAD_DIM // 2
    rot = jnp.concatenate([-t[..., half:], t[..., :half]], axis=-1)
    return (t * cos[:, None] + rot * sin[:, None]).astype(t.dtype)


def softmax_with_sink(logits, sink):
    full = jnp.concatenate([logits, jnp.broadcast_to(sink, logits.shape[:-1] + (1,))], axis=-1)
    return jax.nn.softmax(full, axis=-1)[..., :-1]


def windowed_attention(q, k, v, k_ctx, v_ctx, sinks):
    f32 = jnp.float32
    b, L = q.shape[:2]
    nb = L // BLOCK
    scale = HEAD_DIM ** -0.5
    qb = q.reshape(b, nb, BLOCK, ATTN_KV_HEADS, Q_PER_KV, HEAD_DIM)

    def band(t):
        tp = jnp.pad(t, ((0, 0), (BLOCK, BLOCK), (0, 0), (0, 0)))
        tp = tp.reshape(b, nb + 2, BLOCK, ATTN_KV_HEADS, HEAD_DIM)
        return jnp.concatenate([tp[:, :-2], tp[:, 1:-1], tp[:, 2:]], axis=2)

    kw, vw = band(k), band(v)
    s_loc = jnp.einsum('bnqhgd,bnkhd->bnhgqk', qb, kw, preferred_element_type=f32) * scale
    blk = jnp.arange(nb)[:, None, None] * BLOCK
    q_pos = blk + jnp.arange(BLOCK)[None, :, None]
    k_pos = blk + jnp.arange(3 * BLOCK)[None, None, :] - BLOCK
    valid = (jnp.abs(k_pos - q_pos) <= WINDOW) & (k_pos >= 0) & (k_pos < L)
    s_loc = jnp.where(valid[None, :, None, None], s_loc, -jnp.inf)
    s_ctx = jnp.einsum('bnqhgd,bkhd->bnhgqk', qb, k_ctx, preferred_element_type=f32) * scale
    sink = sinks.astype(f32).reshape(ATTN_KV_HEADS, Q_PER_KV)[:, :, None, None]
    p = softmax_with_sink(jnp.concatenate([s_loc, s_ctx], axis=-1), sink)
    p_loc = p[..., :3 * BLOCK].astype(v.dtype)
    p_ctx = p[..., 3 * BLOCK:].astype(v.dtype)
    o = (jnp.einsum('bnhgqk,bnkhd->bnqhgd', p_loc, vw)
         + jnp.einsum('bnhgqk,bkhd->bnqhgd', p_ctx, v_ctx))
    return o.reshape(b, L, ATTN_WIDTH)


def context_attention(q_c, k_c, v_c, sinks):
    b, Lc = q_c.shape[:2]
    qg = q_c.reshape(b, Lc, ATTN_KV_HEADS, Q_PER_KV, HEAD_DIM)
    s = jnp.einsum('bqhgd,bkhd->bhgqk', qg, k_c, preferred_element_type=jnp.float32) * HEAD_DIM ** -0.5
    sink = sinks.astype(jnp.float32).reshape(ATTN_KV_HEADS, Q_PER_KV)[:, :, None, None]
    p = softmax_with_sink(s, sink).astype(v_c.dtype)
    return jnp.einsum('bhgqk,bkhd->bqhgd', p, v_c).reshape(b, Lc, ATTN_WIDTH)


def hier_moe(h, w_group, b_group, w_expert, b_expert, w_gate, w_up, w_down):
    f32 = jnp.float32
    n_tok, d = h.shape
    g_prob = jax.nn.softmax((h @ w_group).astype(f32) + b_group, axis=-1)
    g_w, g_idx = lax.top_k(g_prob, 1)
    e_logits = ((h @ w_expert).astype(f32) + b_expert).reshape(n_tok, N_GROUPS, EXPERTS_PER_GROUP)
    e_sel = e_logits[jnp.arange(n_tok), g_idx[:, 0]]
    e_w, e_idx = lax.top_k(jax.nn.softmax(e_sel, axis=-1), TOP_K)
    weights = g_w * e_w / jnp.sum(e_w, axis=-1, keepdims=True)
    expert = g_idx * EXPERTS_PER_GROUP + e_idx
    n_assign = n_tok * TOP_K
    flat_e = expert.reshape(n_assign).astype(jnp.int32)
    flat_tok = jnp.repeat(jnp.arange(n_tok, dtype=jnp.int32), TOP_K)
    flat_w = weights.reshape(n_assign)
    order = jnp.argsort(flat_e)
    sorted_e = flat_e[order]
    counts = jnp.zeros((N_EXPERTS,), jnp.int32).at[flat_e].add(1)
    start = jnp.cumsum(counts) - counts
    padded = (counts + MOE_BLOCK - 1) // MOE_BLOCK * MOE_BLOCK
    ends = jnp.cumsum(padded)
    pstart = ends - padded
    dest = pstart[sorted_e] + jnp.arange(n_assign, dtype=jnp.int32) - start[sorted_e]
    n_blocks = -(-n_assign // MOE_BLOCK) + N_EXPERTS
    n_slots = n_blocks * MOE_BLOCK
    tok_buf = jnp.full((n_slots,), n_tok, jnp.int32).at[dest].set(flat_tok[order])
    w_buf = jnp.zeros((n_slots,), f32).at[dest].set(flat_w[order])
    h_pad = jnp.concatenate([h, jnp.zeros((1, d), h.dtype)], axis=0)
    x_buf = h_pad[tok_buf].reshape(n_blocks, MOE_BLOCK, d)
    blk_e = jnp.searchsorted(ends, jnp.arange(n_blocks, dtype=jnp.int32) * MOE_BLOCK, side='right')
    blk_e = jnp.minimum(blk_e, N_EXPERTS - 1)

    def expert_block(args):
        xb, e = args
        return (jax.nn.silu(xb @ w_gate[e]) * (xb @ w_up[e])) @ w_down[e]

    y_buf = lax.map(expert_block, (x_buf, blk_e)).reshape(n_slots, d)
    out = jax.ops.segment_sum(y_buf.astype(f32) * w_buf[:, None], tok_buf, num_segments=n_tok + 1)
    return out[:n_tok].astype(h.dtype)


def setup_inputs(seed: int = 0) -> dict:
    key = jax.random.key(seed)
    ks = jax.random.split(key, 26)
    f32 = jnp.float32

    def nrm(k, shape, scale):
        return jax.random.normal(k, shape, f32) * scale

    u = jax.random.uniform(ks[10], (DEPTH, 2, SSD_HEADS), f32)
    dt0 = jnp.exp(u * (math.log(DT_MAX) - math.log(DT_MIN)) + math.log(DT_MIN))
    return {
        'x': nrm(ks[0], (BATCH, SEQ, D_MODEL), 1.0),
        'c': nrm(ks[1], (BATCH, D_MODEL), 1.0),
        'ctx': nrm(ks[2], (BATCH, CTX_LEN, D_MODEL), 1.0),
        'c_ctx': nrm(ks[3], (D_MODEL,), 1.0),
        'w_ada': nrm(ks[4], (DEPTH, D_MODEL, 6 * D_MODEL), 0.5 * D_MODEL ** -0.5),
        'b_ada': nrm(ks[5], (DEPTH, 6 * D_MODEL), 0.02),
        'norm1': 1.0 + nrm(ks[6], (DEPTH, D_MODEL), 0.02),
        'w_in': nrm(ks[7], (DEPTH, D_MODEL, IN_COLS), D_MODEL ** -0.5),
        'conv_w': nrm(ks[8], (DEPTH, CONV_WIDTH, CONV_DIM), CONV_WIDTH ** -0.5),
        'conv_b': nrm(ks[9], (DEPTH, CONV_DIM), 0.02),
        'dt_bias': dt0 + jnp.log(-jnp.expm1(-dt0)),
        'a_log': jnp.log(jax.random.uniform(ks[11], (DEPTH, 2, SSD_HEADS), f32, 1.0, 16.0)),
        'd_skip': 1.0 + nrm(ks[12], (DEPTH, SSD_HEADS), 0.1),
        'ssd_norm': 1.0 + nrm(ks[13], (DEPTH, SSD_INNER), 0.02),
        'attn_sinks': nrm(ks[14], (DEPTH, ATTN_Q_HEADS), 1.0),
        'w_out': nrm(ks[15], (DEPTH, MIX_WIDTH, D_MODEL), MIX_WIDTH ** -0.5),
        'norm2': 1.0 + nrm(ks[16], (DEPTH, D_MODEL), 0.02),
        'w_group': nrm(ks[17], (DEPTH, D_MODEL, N_GROUPS), D_MODEL ** -0.5),
        'b_group': nrm(ks[18], (DEPTH, N_GROUPS), 0.01),
        'w_expert': nrm(ks[19], (DEPTH, D_MODEL, N_EXPERTS), D_MODEL ** -0.5),
        'b_expert': nrm(ks[20], (DEPTH, N_EXPERTS), 0.01),
        'w_gate': nrm(ks[21], (DEPTH, N_EXPERTS, D_MODEL, EXPERT_DIM), D_MODEL ** -0.5),
        'w_up': nrm(ks[22], (DEPTH, N_EXPERTS, D_MODEL, EXPERT_DIM), D_MODEL ** -0.5),
        'w_down': nrm(ks[23], (DEPTH, N_EXPERTS, EXPERT_DIM, D_MODEL), EXPERT_DIM ** -0.5),
        'norm_final': 1.0 + nrm(ks[24], (D_MODEL,), 0.02),
    }


def reference(x, c, ctx, c_ctx, w_ada, b_ada, norm1, w_in, conv_w, conv_b, dt_bias, a_log,
              d_skip, ssd_norm, attn_sinks, w_out, norm2, w_group, b_group, w_expert, b_expert,
              w_gate, w_up, w_down, norm_final):
    b, n_tok, d = x.shape
    n_ctx = ctx.shape[1]
    rows = n_tok // GRID_W
    cos, sin = axial_rope_tables(rows)
    h0 = jnp.zeros((b, SSD_GROUPS, SSD_HEADS_PER_GROUP, SSD_HEAD_DIM, SSD_STATE), jnp.float32)
    for layer in range(DEPTH):
        last = layer == DEPTH - 1
        mod_x = jax.nn.silu(c) @ w_ada[layer] + b_ada[layer]
        mod_c = jax.nn.silu(c_ctx) @ w_ada[layer] + b_ada[layer]
        sh1, sc1, g1, sh2, sc2, g2 = jnp.split(mod_x[:, None, :], 6, axis=-1)
        csh1, csc1, cg1, csh2, csc2, cg2 = jnp.split(mod_c, 6, axis=-1)
        hx = rms_norm(x, norm1[layer]) * (1 + sc1) + sh1
        hc = rms_norm(ctx, norm1[layer]) * (1 + csc1) + csh1
        zx, xbc_x, dt_x, qx, kx, vx = split_projection(hx @ w_in[layer])
        zc, xbc_c, dt_c, qc, kc, vc = split_projection(hc @ w_in[layer])
        xs_x, bm_x, cm_x, dtv_x = ssd_inputs(xbc_x, dt_x, conv_w[layer], conv_b[layer], dt_bias[layer])
        xs_c, bm_c, cm_c, dtv_c = ssd_inputs(xbc_c, dt_c, conv_w[layer], conv_b[layer], dt_bias[layer])
        A = -jnp.exp(a_log[layer].astype(jnp.float32))
        y_cf, h_cf = ssd_scan(xs_c, dtv_c[:, :, 0], A[0], bm_c, cm_c, h0, need_y=not last)
        y_cb, h_cb = ssd_scan(_flip(xs_c), _flip(dtv_c[:, :, 1]), A[1], _flip(bm_c), _flip(cm_c), h0,
                              need_y=not last)
        y_xf, _ = ssd_scan(xs_x, dtv_x[:, :, 0], A[0], bm_x, cm_x, h_cf)
        y_xb, _ = ssd_scan(_flip(xs_x), _flip(dtv_x[:, :, 1]), A[1], _flip(bm_x), _flip(cm_x), h_cb)
        ssd_x = ssd_output(y_xf + _flip(y_xb), xs_x, zx, d_skip[layer], ssd_norm[layer])
        qx = apply_rope(qx.reshape(b, n_tok, ATTN_Q_HEADS, HEAD_DIM), cos, sin)
        kx = apply_rope(kx.reshape(b, n_tok, ATTN_KV_HEADS, HEAD_DIM), cos, sin)
        vx = vx.reshape(b, n_tok, ATTN_KV_HEADS, HEAD_DIM)
        kc = kc.reshape(b, n_ctx, ATTN_KV_HEADS, HEAD_DIM)
        vc = vc.reshape(b, n_ctx, ATTN_KV_HEADS, HEAD_DIM)
        attn_x = windowed_attention(qx, kx, vx, kc, vc, attn_sinks[layer])
        x = x + g1 * (jnp.concatenate([ssd_x, attn_x], axis=-1) @ w_out[layer])
        h2 = rms_norm(x, norm2[layer]) * (1 + sc2) + sh2
        moe_x = hier_moe(h2.reshape(b * n_tok, d), w_group[layer], b_group[layer], w_expert[layer],
                         b_expert[layer], w_gate[layer], w_up[layer], w_down[layer])
        x = x + g2 * moe_x.reshape(b, n_tok, d)
        if not last:
            ssd_c = ssd_output(y_cf + _flip(y_cb), xs_c, zc, d_skip[layer], ssd_norm[layer])
            attn_c = context_attention(qc.reshape(b, n_ctx, ATTN_Q_HEADS, HEAD_DIM), kc, vc, attn_sinks[layer])
            ctx = ctx + cg1 * (jnp.concatenate([ssd_c, attn_c], axis=-1) @ w_out[layer])
            h2c = rms_norm(ctx, norm2[layer]) * (1 + csc2) + csh2
            moe_c = hier_moe(h2c.reshape(b * n_ctx, d), w_group[layer], b_group[layer], w_expert[layer],
                             b_expert[layer], w_gate[layer], w_up[layer], w_down[layer])
            ctx = ctx + cg2 * moe_c.reshape(b, n_ctx, d)
    return rms_norm(x, norm_final)
```

```python
import functools
import math

import numpy as np
import jax
import jax.numpy as jnp
from jax import lax
from jax.experimental import pallas as pl
from jax.experimental.pallas import tpu as pltpu

F32 = jnp.float32
BF16 = jnp.bfloat16

D_MODEL = 1024
GRID_W = 64
EPS = 1e-6
SSD_INNER = 512
SSD_HEAD_DIM = 64
SSD_HEADS = 8
SSD_GROUPS = 2
SSD_STATE = 128
CONV_WIDTH = 5
CONV_DIM = SSD_INNER + 2 * SSD_GROUPS * SSD_STATE
CHUNK = 128
ATTN_WIDTH = 512
HEAD_DIM = 64
ATTN_Q_HEADS = 8
ATTN_KV_HEADS = 2
Q_PER_KV = 4
KV_WIDTH = 128
WINDOW = 128
ROPE_BASE = 10000.0
N_GROUPS = 4
EXPERTS_PER_GROUP = 8
N_EXPERTS = 32
EXPERT_DIM = 512

LANES = 128
SUBLANES = 8
NEG = -1e30

ROWS_IN = 512
ROWS_SSD = 512
ROWS_ATT = 256
ROUTE_ROWS = 48
MOE_BLK = 128
ROWS_CMB = 128


def _silu(v):
    return v * (1.0 / (1.0 + jnp.exp(-v)))


def _softplus(v):
    return jnp.maximum(v, 0.0) + jnp.log(1.0 + jnp.exp(-jnp.abs(v)))


def _ada_kernel(c_ref, w_ref, b_ref, o_ref):
    a = _silu(c_ref[...])
    o_ref[...] = jnp.dot(a.astype(BF16), w_ref[...].astype(BF16),
                         preferred_element_type=F32) + b_ref[...]


def _ada(cc, w_ada, b_ada):
    d, n = w_ada.shape
    tn = 1536
    return pl.pallas_call(
        _ada_kernel,
        grid=(n // tn,),
        in_specs=[pl.BlockSpec((SUBLANES, d), lambda j: (0, 0)),
                  pl.BlockSpec((d, tn), lambda j: (0, j)),
                  pl.BlockSpec((1, tn), lambda j: (0, j))],
        out_specs=pl.BlockSpec((SUBLANES, tn), lambda j: (0, j)),
        out_shape=jax.ShapeDtypeStruct((SUBLANES, n), F32),
        name="ada",
    )(cc, w_ada, b_ada)


def _rope(t, cos, sin_signed, first_half):
    rot = jnp.where(first_half, pltpu.roll(t, LANES - HEAD_DIM // 2, 1), pltpu.roll(t, HEAD_DIM // 2, 1))
    return t * cos + rot * sin_signed


def _inproj_kernel(*refs, segs, rope):
    x_ref, sh_ref, sc_ref, g_ref, w_ref, wdt_ref = refs[:6]
    pos = 6
    if rope:
        cos_ref, sin_ref = refs[6:8]
        pos = 8
    outs = refs[pos:]
    x = x_ref[...]
    ms = jnp.mean(x * x, axis=-1, keepdims=True)
    h = (x * lax.rsqrt(ms + EPS)) * (g_ref[...] * (1.0 + sc_ref[0])) + sh_ref[0]
    hb = h.astype(BF16)
    if rope:
        cos = cos_ref[...]
        lane = lax.broadcasted_iota(jnp.int32, cos.shape, 1)
        first_half = (lane & (HEAD_DIM - 1)) < HEAD_DIM // 2
        sin_s = jnp.where(first_half, -sin_ref[...], sin_ref[...])
    for (kind, c0, width), o_ref in zip(segs, outs):
        if kind == "dtT":
            dtt = lax.dot_general(wdt_ref[...], hb, (((1,), (1,)), ((), ())), preferred_element_type=F32)
            for j in range(o_ref.shape[0]):
                o_ref[j] = dtt[:, j * CHUNK:(j + 1) * CHUNK]
            continue
        for j0 in range(0, width, 512):
            wj = min(512, width - j0)
            acc = jnp.dot(hb, w_ref[:, c0 + j0:c0 + j0 + wj], preferred_element_type=F32)
            if rope and kind in ("q", "k"):
                scale = HEAD_DIM ** -0.5 if kind == "q" else 1.0
                for l0 in range(0, wj, LANES):
                    t = _rope(acc[:, l0:l0 + LANES], cos, sin_s, first_half)
                    o_ref[:, j0 + l0:j0 + l0 + LANES] = (t * scale).astype(o_ref.dtype)
            else:
                o_ref[:, j0:j0 + wj] = acc.astype(o_ref.dtype)


_COL_Z, _COL_XBC, _COL_Q, _COL_K, _COL_V = 0, 512, 1536, 2048, 2176
_W_COLS = 2304


def _inproj(xf, mod3, norm1, w_cat, w_dtT, rope_tabs, *, rows, mod_row_fn, segs, seq_blocks):
    t, d = xf.shape
    rope = rope_tabs is not None
    in_specs = [
        pl.BlockSpec((rows, d), lambda i: (i, 0)),
        pl.BlockSpec((1, 1, d), lambda i: (mod_row_fn(i), 0, 0)),
        pl.BlockSpec((1, 1, d), lambda i: (mod_row_fn(i), 0, 1)),
        pl.BlockSpec((1, d), lambda i: (0, 0)),
        pl.BlockSpec(w_cat.shape, lambda i: (0, 0)),
        pl.BlockSpec(w_dtT.shape, lambda i: (0, 0)),
    ]
    args = [xf, mod3, mod3, norm1, w_cat, w_dtT]
    if rope:
        in_specs += [pl.BlockSpec((rows, LANES), lambda i: (i % seq_blocks, 0))] * 2
        args += list(rope_tabs)
    out_specs, out_shapes = [], []
    for kind, _, width in segs:
        if kind == "dtT":
            out_specs.append(pl.BlockSpec((rows // CHUNK, 2 * SSD_HEADS, CHUNK), lambda i: (i, 0, 0)))
            out_shapes.append(jax.ShapeDtypeStruct((t // CHUNK, 2 * SSD_HEADS, CHUNK), F32))
        else:
            dt = BF16 if kind in ("q", "k", "v") else F32
            out_specs.append(pl.BlockSpec((rows, width), lambda i: (i, 0)))
            out_shapes.append(jax.ShapeDtypeStruct((t, width), dt))
    return pl.pallas_call(
        functools.partial(_inproj_kernel, segs=tuple(segs), rope=rope),
        grid=(t // rows,),
        in_specs=in_specs,
        out_specs=out_specs,
        out_shape=out_shapes,
        name="inproj_rope" if rope else "inproj_ctx",
    )(*args)


def _lane_cumsum(v, lane, reverse):
    k = 1
    while k < LANES:
        if reverse:
            v = v + jnp.where(lane < LANES - k, pltpu.roll(v, LANES - k, 1), 0.0)
        else:
            v = v + jnp.where(lane >= k, pltpu.roll(v, k, 1), 0.0)
        k *= 2
    return v


def _ssd_kernel(*refs, need_y, nsteps, cps):
    (xbc_ref, prev_ref, next_ref, dtr_ref, z_ref, cw_ref, cb_ref, bias_ref, alog_ref,
     dskip_ref, gain_ref, exp_ref, h0f_ref, h0b_ref) = refs[:14]
    if need_y:
        y_ref, hf_out, hb_out = refs[14:17]
        rest = refs[17:]
    else:
        y_ref = None
        hf_out, hb_out = refs[14:16]
        rest = refs[16:]
    win, cwin, u_ref, hf, hb, hbs = rest

    sw = pl.program_id(1)
    s = pl.program_id(2)
    bi = jnp.where(sw == 0, nsteps - 1 - s, s)
    rows = xbc_ref.shape[1]
    h = SSD_HEADS

    @pl.when((sw == 0) & (s == 0))
    def _():
        hb[...] = h0b_ref[0]

    @pl.when((sw == 1) & (s == 0))
    def _():
        hf[...] = h0f_ref[0]

    win[0:SUBLANES, :] = jnp.where(bi > 0, prev_ref[0], 0.0)
    win[SUBLANES:SUBLANES + rows, :] = xbc_ref[0]
    win[SUBLANES + rows:, :] = jnp.where(bi < nsteps - 1, next_ref[0], 0.0)

    lane = lax.broadcasted_iota(jnp.int32, (2 * h, LANES), 1)
    ri = lax.broadcasted_iota(jnp.int32, (CHUNK, CHUNK), 0)
    ci = lax.broadcasted_iota(jnp.int32, (CHUNK, CHUNK), 1)
    lane128 = lax.broadcasted_iota(jnp.int32, (CHUNK, LANES), 1)
    lo_mask = lane128 < SSD_HEAD_DIM
    neg_a = -jnp.exp(alog_ref[...])

    def conv_silu(c, ncols):
        r0 = pl.multiple_of(c * CHUNK, CHUNK)
        cwin[...] = win[pl.ds(r0, CHUNK + 2 * SUBLANES), :]
        for j0 in range(0, ncols, LANES):
            acc = jnp.broadcast_to(cb_ref[:, j0:j0 + LANES], (CHUNK, LANES))
            for k in range(CONV_WIDTH):
                off = SUBLANES - CONV_WIDTH // 2 + k
                acc = acc + cwin[off:off + CHUNK, j0:j0 + LANES] * cw_ref[k:k + 1, j0:j0 + LANES]
            u_ref[:, j0:j0 + LANES] = _silu(acc)

    def dt_rows(c):
        dt = _softplus(dtr_ref[c] + bias_ref[...])
        a = dt * neg_a
        pre = _lane_cumsum(a, lane, False)
        suf = _lane_cumsum(a, lane, True)
        return dt, pre, suf

    def state_update(state_ref, w_rows, tot_col):
        dec = jnp.exp(jnp.sum(tot_col * exp_ref[...], axis=0, keepdims=True))
        for g in range(SSD_GROUPS):
            bm = u_ref[:, SSD_INNER + g * SSD_STATE:SSD_INNER + (g + 1) * SSD_STATE]
            bt = bm.T
            for pr in range(2):
                h0 = g * 4 + pr * 2
                c0 = h0 * SSD_HEAD_DIM
                xp = u_ref[:, c0:c0 + LANES]
                rhs = jnp.concatenate([jnp.where(lo_mask, xp, 0.0), jnp.where(lo_mask, 0.0, xp)],
                                      axis=0).astype(BF16)
                lhs = jnp.concatenate([bt * w_rows[h0:h0 + 1, :], bt * w_rows[h0 + 1:h0 + 2, :]],
                                      axis=1).astype(BF16)
                sres = jnp.dot(lhs, rhs, preferred_element_type=F32)
                state_ref[:, c0:c0 + LANES] = state_ref[:, c0:c0 + LANES] * dec[:, c0:c0 + LANES] + sres

    def col_forms(rows16):
        padded = jnp.concatenate([rows16, jnp.zeros((CHUNK - 2 * h, LANES), F32)], axis=0)
        return padded.T

    def backward_chunk(k, carry):
        c = cps - 1 - k
        gc = bi * cps + c
        conv_silu(c, SSD_INNER + SSD_GROUPS * SSD_STATE)
        dt, _, suf = dt_rows(c)
        hbs[gc] = hb[...].astype(BF16)
        s0 = suf[:, 0:1]
        w_rows = jnp.exp(s0 - suf) * dt
        state_update(hb, w_rows[h:2 * h, :], s0[h:2 * h, :])
        return carry

    def forward_chunk(c, carry):
        gc = bi * cps + c
        r0 = pl.multiple_of(c * CHUNK, CHUNK)
        conv_silu(c, CONV_DIM)
        dt, pre, suf = dt_rows(c)
        rowsf = jnp.concatenate([pre[0:h, :], suf[h:2 * h, :]], axis=0)
        colsf = col_forms(rowsf)
        ecol = jnp.exp(colsf)
        for g in range(SSD_GROUPS):
            bm = u_ref[:, SSD_INNER + g * SSD_STATE:SSD_INNER + (g + 1) * SSD_STATE]
            cm = u_ref[:, SSD_INNER + (SSD_GROUPS + g) * SSD_STATE:SSD_INNER + (SSD_GROUPS + g + 1) * SSD_STATE]
            cmb = cm.astype(BF16)
            cbm = lax.dot_general(cmb, bm.astype(BF16), (((1,), (1,)), ((), ())),
                                  preferred_element_type=F32)
            for pr in range(2):
                h0 = g * 4 + pr * 2
                c0 = h0 * SSD_HEAD_DIM
                lhs_parts = []
                for hh in (h0, h0 + 1):
                    segf = colsf[:, hh:hh + 1] - rowsf[hh:hh + 1, :]
                    gf = jnp.exp(jnp.where(ci <= ri, segf, NEG)) * dt[hh:hh + 1, :]
                    segb = colsf[:, h + hh:h + hh + 1] - rowsf[h + hh:h + hh + 1, :]
                    gb = jnp.exp(jnp.where(ci >= ri, segb, NEG)) * dt[h + hh:h + hh + 1, :]
                    lhs_parts.append((cbm * (gf + gb)).astype(BF16))
                for hh in (h0, h0 + 1):
                    lhs_parts.append((cm * ecol[:, hh:hh + 1]).astype(BF16))
                for hh in (h0, h0 + 1):
                    lhs_parts.append((cm * ecol[:, h + hh:h + hh + 1]).astype(BF16))
                lhs = jnp.concatenate(lhs_parts, axis=1)
                xp = u_ref[:, c0:c0 + LANES]
                sf = hf[:, c0:c0 + LANES]
                sb = hbs[gc, :, c0:c0 + LANES].astype(F32)
                rhs = jnp.concatenate(
                    [jnp.where(lo_mask, xp, 0.0), jnp.where(lo_mask, 0.0, xp),
                     jnp.where(lo_mask, sf, 0.0), jnp.where(lo_mask, 0.0, sf),
                     jnp.where(lo_mask, sb, 0.0), jnp.where(lo_mask, 0.0, sb)], axis=0).astype(BF16)
                ypair = jnp.dot(lhs, rhs, preferred_element_type=F32)
                ypair = ypair + dskip_ref[:, c0:c0 + LANES] * xp
                zz = z_ref[0, pl.ds(r0, CHUNK), c0:c0 + LANES]
                u_ref[:, CONV_DIM + c0:CONV_DIM + c0 + LANES] = ypair * _silu(zz)
        last = pre[:, LANES - 1:LANES]
        w_rows = jnp.exp(last - pre) * dt
        state_update(hf, w_rows[0:h, :], last[0:h, :])
        if need_y:
            yv = u_ref[:, CONV_DIM:CONV_DIM + SSD_INNER]
            ms = jnp.mean(yv * yv, axis=-1, keepdims=True)
            y_ref[0, pl.ds(r0, CHUNK), :] = (yv * lax.rsqrt(ms + EPS) * gain_ref[...]).astype(y_ref.dtype)
        return carry

    @pl.when(sw == 0)
    def _():
        lax.fori_loop(0, cps, backward_chunk, 0)

    @pl.when(sw == 1)
    def _():
        lax.fori_loop(0, cps, forward_chunk, 0)

    @pl.when((sw == 0) & (s == nsteps - 1))
    def _():
        hb_out[0] = hb[...]

    @pl.when((sw == 1) & (s == nsteps - 1))
    def _():
        hf_out[0] = hf[...]


def _ssd(xbc, dtr, z, consts, h0f, h0b, *, rows, need_y):
    bt, seq, _ = xbc.shape
    nsteps = seq // rows
    cps = rows // CHUNK
    hb8 = rows // SUBLANES
    nb8 = seq // SUBLANES
    cw, cb, bias, alog, dskip, gain, expm = consts

    def blk(b, sw, s):
        return jnp.where(sw == 0, nsteps - 1 - s, s)

    def full(a):
        return pl.BlockSpec(a.shape, lambda b, sw, s: (0,) * a.ndim)

    in_specs = [
        pl.BlockSpec((1, rows, CONV_DIM), lambda b, sw, s: (b, blk(b, sw, s), 0)),
        pl.BlockSpec((1, SUBLANES, CONV_DIM), lambda b, sw, s: (b, jnp.maximum(blk(b, sw, s) * hb8 - 1, 0), 0)),
        pl.BlockSpec((1, SUBLANES, CONV_DIM),
                     lambda b, sw, s: (b, jnp.minimum((blk(b, sw, s) + 1) * hb8, nb8 - 1), 0)),
        pl.BlockSpec((cps, 2 * SSD_HEADS, CHUNK), lambda b, sw, s: (b * nsteps + blk(b, sw, s), 0, 0)),
        pl.BlockSpec((1, rows, SSD_INNER), lambda b, sw, s: (b, jnp.where(sw == 0, 0, s), 0)),
        full(cw), full(cb), full(bias), full(alog), full(dskip), full(gain), full(expm),
        pl.BlockSpec((1, SSD_STATE, SSD_INNER), lambda b, sw, s: (b, 0, 0)),
        pl.BlockSpec((1, SSD_STATE, SSD_INNER), lambda b, sw, s: (b, 0, 0)),
    ]
    st_spec = pl.BlockSpec((1, SSD_STATE, SSD_INNER), lambda b, sw, s: (b, 0, 0))
    st_shape = jax.ShapeDtypeStruct((bt, SSD_STATE, SSD_INNER), F32)
    out_specs, out_shapes = [st_spec, st_spec], [st_shape, st_shape]
    if need_y:
        out_specs = [pl.BlockSpec((1, rows, SSD_INNER), lambda b, sw, s: (b, jnp.where(sw == 0, 0, s), 0))] + out_specs
        out_shapes = [jax.ShapeDtypeStruct((bt, seq, SSD_INNER), BF16)] + out_shapes
    scratch = [
        pltpu.VMEM((rows + 2 * SUBLANES, CONV_DIM), F32),
        pltpu.VMEM((CHUNK + 2 * SUBLANES, CONV_DIM), F32),
        pltpu.VMEM((CHUNK, CONV_DIM + SSD_INNER), F32),
        pltpu.VMEM((SSD_STATE, SSD_INNER), F32),
        pltpu.VMEM((SSD_STATE, SSD_INNER), F32),
        pltpu.VMEM((seq // CHUNK, SSD_STATE, SSD_INNER), BF16),
    ]
    return pl.pallas_call(
        functools.partial(_ssd_kernel, need_y=need_y, nsteps=nsteps, cps=cps),
        grid=(bt, 2, nsteps),
        in_specs=in_specs,
        out_specs=out_specs,
        out_shape=out_shapes,
        scratch_shapes=scratch,
        compiler_params=pltpu.CompilerParams(dimension_semantics=("arbitrary", "arbitrary", "arbitrary")),
        name="ssd_y" if need_y else "ssd_ctx",
    )(xbc, xbc, xbc, dtr, z, cw, cb, bias, alog, dskip, gain, expm, h0f, h0b)


def _attn_kernel(sink_ref, q_ref, kp_ref, kc_ref, kn_ref, vp_ref, vc_ref, vn_ref, kx_ref, vx_ref, o_ref, *, nsteps):
    m = pl.program_id(1)
    lane = lax.broadcasted_iota(jnp.int32, (1, LANES), 1)
    lo = lane < HEAD_DIM
    rq = Q_PER_KV * WINDOW
    qi = lax.broadcasted_iota(jnp.int32, (rq, WINDOW), 0) & (WINDOW - 1)
    kj = lax.broadcasted_iota(jnp.int32, (rq, WINDOW), 1)
    rblk = lax.shift_right_logical(lax.broadcasted_iota(jnp.int32, (rq, 1), 0), int(math.log2(WINDOW)))
    zero = jnp.zeros((), BF16)

    kpieces = [kp_ref[0], kc_ref[0, 0:WINDOW], kc_ref[0, WINDOW:2 * WINDOW], kn_ref[0]]
    vpieces = [vp_ref[0], vc_ref[0, 0:WINDOW], vc_ref[0, WINDOW:2 * WINDOW], vn_ref[0]]
    kx = kx_ref[0]
    vx = vx_ref[0]
    nt = (((1,), (1,)), ((), ()))

    for sb in range(ROWS_ATT // WINDOW):
        qsb = q_ref[0, sb * WINDOW:(sb + 1) * WINDOW, :]
        prev_ok = jnp.logical_or(m > 0, sb > 0)
        next_ok = jnp.logical_or(m < nsteps - 1, sb < ROWS_ATT // WINDOW - 1)
        outs = [None] * Q_PER_KV
        for kv in range(ATTN_KV_HEADS):
            sel = lo if kv == 0 else jnp.logical_not(lo)
            qs = jnp.concatenate(
                [jnp.where(sel, qsb[:, j * LANES:(j + 1) * LANES], zero) for j in range(Q_PER_KV)], axis=0)
            s_prev = lax.dot_general(qs, kpieces[sb], nt, preferred_element_type=F32)
            s_cur = lax.dot_general(qs, kpieces[sb + 1], nt, preferred_element_type=F32)
            s_next = lax.dot_general(qs, kpieces[sb + 2], nt, preferred_element_type=F32)
            s_ctx = lax.dot_general(qs, kx, nt, preferred_element_type=F32)
            s_prev = jnp.where((kj >= qi) & prev_ok, s_prev, NEG)
            s_next = jnp.where((kj <= qi) & next_ok, s_next, NEG)
            sink = jnp.zeros((rq, 1), F32)
            for j in range(Q_PER_KV):
                sink = jnp.where(rblk == j, sink_ref[kv * Q_PER_KV + j], sink)
            mx = jnp.maximum(
                jnp.maximum(jnp.max(s_prev, axis=-1, keepdims=True), jnp.max(s_cur, axis=-1, keepdims=True)),
                jnp.maximum(jnp.max(s_next, axis=-1, keepdims=True), jnp.max(s_ctx, axis=-1, keepdims=True)))
            mx = jnp.maximum(mx, sink)
            p_prev = jnp.exp(s_prev - mx)
            p_cur = jnp.exp(s_cur - mx)
            p_next = jnp.exp(s_next - mx)
            p_ctx = jnp.exp(s_ctx - mx)
            den = (jnp.sum(p_prev, axis=-1, keepdims=True) + jnp.sum(p_cur, axis=-1, keepdims=True)
                   + jnp.sum(p_next, axis=-1, keepdims=True) + jnp.sum(p_ctx, axis=-1, keepdims=True)
                   + jnp.exp(sink - mx))
            acc = jnp.dot(p_prev.astype(BF16), jnp.where(sel, vpieces[sb], zero), preferred_element_type=F32)
            acc += jnp.dot(p_cur.astype(BF16), jnp.where(sel, vpieces[sb + 1], zero), preferred_element_type=F32)
            acc += jnp.dot(p_next.astype(BF16), jnp.where(sel, vpieces[sb + 2], zero), preferred_element_type=F32)
            acc += jnp.dot(p_ctx.astype(BF16), jnp.where(sel, vx, zero), preferred_element_type=F32)
            acc = acc * (1.0 / den)
            for j in range(Q_PER_KV):
                part = acc[j * WINDOW:(j + 1) * WINDOW, :]
                outs[j] = part if outs[j] is None else outs[j] + part
        for j in range(Q_PER_KV):
            o_ref[0, sb * WINDOW:(sb + 1) * WINDOW, j * LANES:(j + 1) * LANES] = outs[j].astype(o_ref.dtype)


def _attention(sinks, q, k, v, kx, vx):
    bt, seq, _ = q.shape
    nsteps = seq // ROWS_ATT
    per = ROWS_ATT // WINDOW
    nb = seq // WINDOW
    n_ctx = kx.shape[1]

    def prev(b, m, s):
        return (b, jnp.maximum(m * per - 1, 0), 0)

    def cur(b, m, s):
        return (b, m, 0)

    def nxt(b, m, s):
        return (b, jnp.minimum((m + 1) * per, nb - 1), 0)

    small = (1, WINDOW, KV_WIDTH)
    big = (1, ROWS_ATT, KV_WIDTH)
    grid_spec = pltpu.PrefetchScalarGridSpec(
        num_scalar_prefetch=1,
        grid=(bt, nsteps),
        in_specs=[
            pl.BlockSpec((1, ROWS_ATT, ATTN_WIDTH), cur),
            pl.BlockSpec(small, prev), pl.BlockSpec(big, cur), pl.BlockSpec(small, nxt),
            pl.BlockSpec(small, prev), pl.BlockSpec(big, cur), pl.BlockSpec(small, nxt),
            pl.BlockSpec((1, n_ctx, KV_WIDTH), lambda b, m, s: (b, 0, 0)),
            pl.BlockSpec((1, n_ctx, KV_WIDTH), lambda b, m, s: (b, 0, 0)),
        ],
        out_specs=pl.BlockSpec((1, ROWS_ATT, ATTN_WIDTH), cur),
    )
    return pl.pallas_call(
        functools.partial(_attn_kernel, nsteps=nsteps),
        grid_spec=grid_spec,
        out_shape=jax.ShapeDtypeStruct((bt, seq, ATTN_WIDTH), BF16),
        name="attn",
    )(sinks, q, k, k, k, v, v, v, kx, vx)


def _outproj_kernel(ssd_ref, att_ref, x_ref, g1_ref, sh_ref, sc_ref, n2_ref, wa_ref, wb_ref, wr_ref, br_ref,
                    tri_ref, x1_ref, h2_ref, ids_ref, wcol_ref, cnt_ref, carry, *, nsteps):
    i = pl.program_id(0)
    rows = x_ref.shape[0]

    @pl.when(i == 0)
    def _():
        carry[...] = jnp.zeros_like(carry)

    acc = jnp.dot(ssd_ref[...], wa_ref[...], preferred_element_type=F32)
    acc += jnp.dot(att_ref[...], wb_ref[...], preferred_element_type=F32)
    x1 = x_ref[...] + g1_ref[0] * acc
    x1_ref[...] = x1
    ms = jnp.mean(x1 * x1, axis=-1, keepdims=True)
    h2 = (x1 * lax.rsqrt(ms + EPS)) * (n2_ref[...] * (1.0 + sc_ref[0])) + sh_ref[0]
    h2_ref[...] = h2
    lt = lax.dot_general(wr_ref[...], h2.astype(BF16), (((1,), (1,)), ((), ())),
                         preferred_element_type=F32) + br_ref[...]
    row = lax.broadcasted_iota(jnp.int32, lt.shape, 0).astype(F32)
    big_i = float(ROUTE_ROWS)
    gl = jnp.where(row < N_GROUPS, lt, NEG)
    gmax = jnp.max(gl, axis=0, keepdims=True)
    gidx = jnp.min(jnp.where(gl == gmax, row, big_i), axis=0, keepdims=True)
    g_w = 1.0 / jnp.sum(jnp.exp(gl - gmax), axis=0, keepdims=True)
    lo = N_GROUPS + EXPERTS_PER_GROUP * gidx
    el = jnp.where((row >= lo) & (row < lo + EXPERTS_PER_GROUP), lt, NEG)
    m1 = jnp.max(el, axis=0, keepdims=True)
    i1 = jnp.min(jnp.where(el == m1, row, big_i), axis=0, keepdims=True)
    el2 = jnp.where(row == i1, NEG, el)
    m2 = jnp.max(el2, axis=0, keepdims=True)
    i2 = jnp.min(jnp.where(el2 == m2, row, big_i), axis=0, keepdims=True)
    r = jnp.exp(m2 - m1)
    w1 = g_w / (1.0 + r)
    w2 = g_w * r / (1.0 + r)
    oh1 = row == i1
    oh2 = row == i2
    cnt = jnp.where(oh1 | oh2, 1.0, 0.0)
    prefix = jnp.dot(cnt.astype(BF16), tri_ref[...], preferred_element_type=F32)
    base = carry[...] + prefix
    rank1 = jnp.sum(jnp.where(oh1, base, 0.0), axis=0, keepdims=True)
    rank2 = jnp.sum(jnp.where(oh2, base, 0.0), axis=0, keepdims=True)
    carry[...] = carry[...] + jnp.sum(cnt, axis=1, keepdims=True)
    r8 = lax.broadcasted_iota(jnp.int32, (SUBLANES, rows), 0)
    ids_ref[...] = jnp.where(r8 == 0, i1 - N_GROUPS,
                             jnp.where(r8 == 1, i2 - N_GROUPS,
                                       jnp.where(r8 == 2, rank1, jnp.where(r8 == 3, rank2, 0.0)))).astype(jnp.int32)
    r128 = lax.broadcasted_iota(jnp.int32, (LANES, rows), 0)
    wfull = jnp.where(r128 == 0, w1, jnp.where(r128 == 1, w2, 0.0))
    for j in range(rows // LANES):
        wcol_ref[j * LANES:(j + 1) * LANES, :] = wfull[:, j * LANES:(j + 1) * LANES].T

    @pl.when(i == nsteps - 1)
    def _():
        cnt_ref[...] = carry[:, 0:LANES]


def _outproj(ssd, att, xf, mod3, norm2, wa, wb, wr, br, tri, *, rows, seq_blocks):
    t, d = xf.shape
    nsteps = t // rows

    def mod(col):
        return pl.BlockSpec((1, 1, d), lambda i: (i // seq_blocks, 0, col))

    def full(a):
        return pl.BlockSpec(a.shape, lambda i: (0,) * a.ndim)

    return pl.pallas_call(
        functools.partial(_outproj_kernel, nsteps=nsteps),
        grid=(nsteps,),
        in_specs=[pl.BlockSpec((rows, SSD_INNER), lambda i: (i, 0)),
                  pl.BlockSpec((rows, ATTN_WIDTH), lambda i: (i, 0)),
                  pl.BlockSpec((rows, d), lambda i: (i, 0)),
                  mod(2), mod(3), mod(4), full(norm2), full(wa), full(wb), full(wr), full(br), full(tri)],
        out_specs=[pl.BlockSpec((rows, d), lambda i: (i, 0)),
                   pl.BlockSpec((rows, d), lambda i: (i, 0)),
                   pl.BlockSpec((SUBLANES, rows), lambda i: (0, i)),
                   pl.BlockSpec((rows, LANES), lambda i: (i, 0)),
                   pl.BlockSpec((ROUTE_ROWS, LANES), lambda i: (0, 0))],
        out_shape=[jax.ShapeDtypeStruct((t, d), F32),
                   jax.ShapeDtypeStruct((t, d), F32),
                   jax.ShapeDtypeStruct((SUBLANES, t), jnp.int32),
                   jax.ShapeDtypeStruct((t, LANES), F32),
                   jax.ShapeDtypeStruct((ROUTE_ROWS, LANES), F32)],
        scratch_shapes=[pltpu.VMEM((ROUTE_ROWS, rows), F32)],
        compiler_params=pltpu.CompilerParams(dimension_semantics=("arbitrary",)),
        name="outproj_router",
    )(ssd, att, xf, mod3, mod3, mod3, norm2, wa, wb, wr, br, tri)


def _expert_kernel(blk_e_ref, nused_ref, tok_ref, h2_hbm, wg_ref, wu_ref, wd_ref, y_ref,
                   xbuf, wgb, wub, wdb, sem, *, nblocks):
    b = pl.program_id(0)
    nused = nused_ref[0]
    slot = b % 2

    def row_copy(tok, r, sl):
        return pltpu.make_async_copy(h2_hbm.at[pl.ds(tok, 1)], xbuf.at[sl, pl.ds(r, 1)], sem.at[sl])

    def start_gather(blk, sl):
        for r in range(MOE_BLK):
            row_copy(tok_ref[blk * MOE_BLK + r], r, sl).start()

    @pl.when((b == 0) & (nused > 0))
    def _():
        start_gather(0, 0)

    @pl.when(b + 1 < nused)
    def _():
        start_gather(b + 1, 1 - slot)

    e = blk_e_ref[b]
    e_prev = blk_e_ref[jnp.maximum(b - 1, 0)]

    @pl.when((b == 0) | (e != e_prev))
    def _():
        wgb[...] = wg_ref[0].astype(BF16)
        wub[...] = wu_ref[0].astype(BF16)
        wdb[...] = wd_ref[0].astype(BF16)

    @pl.when(b < nused)
    def _():
        for r in range(MOE_BLK):
            row_copy(0, r, slot).wait()
        xb = xbuf[slot].astype(BF16)
        gate = jnp.dot(xb, wgb[...], preferred_element_type=F32)
        up = jnp.dot(xb, wub[...], preferred_element_type=F32)
        hmid = (_silu(gate) * up).astype(BF16)
        y_ref[...] = jnp.dot(hmid, wdb[...], preferred_element_type=F32)

    @pl.when(b >= nused)
    def _():
        y_ref[...] = jnp.zeros_like(y_ref)


def _experts(blk_e, nused, tok_buf, h2, w_gate, w_up, w_down, *, nblocks):
    t, d = h2.shape
    grid_spec = pltpu.PrefetchScalarGridSpec(
        num_scalar_prefetch=3,
        grid=(nblocks,),
        in_specs=[
            pl.BlockSpec(memory_space=pl.ANY),
            pl.BlockSpec((1, d, EXPERT_DIM), lambda b, be, nu, tk: (be[b], 0, 0)),
            pl.BlockSpec((1, d, EXPERT_DIM), lambda b, be, nu, tk: (be[b], 0, 0)),
            pl.BlockSpec((1, EXPERT_DIM, d), lambda b, be, nu, tk: (be[b], 0, 0)),
        ],
        out_specs=pl.BlockSpec((MOE_BLK, d), lambda b, be, nu, tk: (b, 0)),
        scratch_shapes=[
            pltpu.VMEM((2, MOE_BLK, d), F32),
            pltpu.VMEM((d, EXPERT_DIM), BF16),
            pltpu.VMEM((d, EXPERT_DIM), BF16),
            pltpu.VMEM((EXPERT_DIM, d), BF16),
            pltpu.SemaphoreType.DMA((2,)),
        ],
    )
    return pl.pallas_call(
        functools.partial(_expert_kernel, nblocks=nblocks),
        grid_spec=grid_spec,
        out_shape=jax.ShapeDtypeStruct((nblocks * MOE_BLK, d), F32),
        compiler_params=pltpu.CompilerParams(dimension_semantics=("arbitrary",)),
        name="experts",
    )(blk_e, nused, tok_buf, h2, w_gate, w_up, w_down)


def _combine_kernel(dest_ref, y_hbm, x1_ref, wcol_ref, g2_ref, nf_ref, o_ref, ybuf, sem, *, nsteps):
    i = pl.program_id(0)
    rows = x1_ref.shape[0]
    slot = i % 2

    def row_copy(src, r, k, sl):
        return pltpu.make_async_copy(y_hbm.at[pl.ds(src, 1)], ybuf.at[sl, k, pl.ds(r, 1)], sem.at[sl])

    def start_gather(step, sl):
        for r in range(rows):
            for k in range(2):
                row_copy(dest_ref[(step * rows + r) * 2 + k], r, k, sl).start()

    @pl.when(i == 0)
    def _():
        start_gather(0, 0)

    @pl.when(i + 1 < nsteps)
    def _():
        start_gather(i + 1, 1 - slot)

    for r in range(rows):
        for k in range(2):
            row_copy(0, r, k, slot).wait()
    w0 = wcol_ref[:, 0:1]
    w1 = wcol_ref[:, 1:2]
    moe = ybuf[slot, 0] * w0 + ybuf[slot, 1] * w1
    x2 = x1_ref[...] + g2_ref[0] * moe
    ms = jnp.mean(x2 * x2, axis=-1, keepdims=True)
    o_ref[...] = x2 * lax.rsqrt(ms + EPS) * nf_ref[...]


def _combine(dest, y_buf, x1, wcol, mod3, norm_final, *, rows, seq_blocks):
    t, d = x1.shape
    nsteps = t // rows
    grid_spec = pltpu.PrefetchScalarGridSpec(
        num_scalar_prefetch=1,
        grid=(nsteps,),
        in_specs=[
            pl.BlockSpec(memory_space=pl.ANY),
            pl.BlockSpec((rows, d), lambda i, ds: (i, 0)),
            pl.BlockSpec((rows, LANES), lambda i, ds: (i, 0)),
            pl.BlockSpec((1, 1, d), lambda i, ds: (i // seq_blocks, 0, 5)),
            pl.BlockSpec((1, d), lambda i, ds: (0, 0)),
        ],
        out_specs=pl.BlockSpec((rows, d), lambda i, ds: (i, 0)),
        scratch_shapes=[pltpu.VMEM((2, 2, rows, d), F32), pltpu.SemaphoreType.DMA((2,))],
    )
    return pl.pallas_call(
        functools.partial(_combine_kernel, nsteps=nsteps),
        grid_spec=grid_spec,
        out_shape=jax.ShapeDtypeStruct((t, d), F32),
        compiler_params=pltpu.CompilerParams(dimension_semantics=("arbitrary",)),
        name="combine",
    )(dest, y_buf, x1, wcol, mod3, norm_final)


def _rope_tables(seq):
    pos = np.arange(seq)
    n_freq = HEAD_DIM // 4
    inv = ROPE_BASE ** (-np.arange(n_freq, dtype=np.float32) / n_freq)
    ang = np.concatenate([(pos // GRID_W)[:, None] * inv, (pos % GRID_W)[:, None] * inv], axis=-1)
    ang = np.concatenate([ang, ang, ang, ang], axis=-1).astype(np.float32)
    return jnp.asarray(np.cos(ang), F32), jnp.asarray(np.sin(ang), F32)


def _head_expand_mat():
    m = np.zeros((SSD_HEADS, SSD_INNER), np.float32)
    for hh in range(SSD_HEADS):
        m[hh, hh * SSD_HEAD_DIM:(hh + 1) * SSD_HEAD_DIM] = 1.0
    return jnp.asarray(m)


def kernel(x, c, ctx, c_ctx, w_ada, b_ada, norm1, w_in, conv_w, conv_b, dt_bias, a_log, d_skip, ssd_norm,
           attn_sinks, w_out, norm2, w_group, b_group, w_expert, b_expert, w_gate, w_up, w_down, norm_final):
    return _pipeline(x, c, ctx, c_ctx, w_ada, b_ada, norm1, w_in, conv_w, conv_b, dt_bias, a_log, d_skip,
                     ssd_norm, attn_sinks, w_out, norm2, w_group, b_group, w_expert, b_expert, w_gate, w_up,
                     w_down, norm_final)["out"]


def _pipeline(x, c, ctx, c_ctx, w_ada, b_ada, norm1, w_in, conv_w, conv_b, dt_bias, a_log, d_skip, ssd_norm,
              attn_sinks, w_out, norm2, w_group, b_group, w_expert, b_expert, w_gate, w_up, w_down, norm_final):
    bt, seq, d = x.shape
    n_ctx = ctx.shape[1]
    t = bt * seq
    layer = 0
    assert w_ada.shape[0] == 1 and seq % ROWS_IN == 0 and n_ctx % CHUNK == 0

    cc = jnp.zeros((SUBLANES, d), F32).at[:bt].set(c).at[bt].set(c_ctx)
    mod = _ada(cc, w_ada[layer], b_ada[layer][None, :])
    mod3 = mod.reshape(SUBLANES, 1, 6 * d)

    w = w_in[layer]
    o_z, o_xbc, o_dt = 0, SSD_INNER, SSD_INNER + CONV_DIM
    o_q = o_dt + 2 * SSD_HEADS
    o_k, o_v = o_q + ATTN_WIDTH, o_q + ATTN_WIDTH + KV_WIDTH
    q_perm = np.concatenate([np.r_[j * HEAD_DIM:(j + 1) * HEAD_DIM, (j + 4) * HEAD_DIM:(j + 5) * HEAD_DIM]
                             for j in range(Q_PER_KV)])
    w_q = w[:, o_q:o_q + ATTN_WIDTH][:, q_perm]
    w_cat = jnp.concatenate([w[:, o_z:o_z + SSD_INNER], w[:, o_xbc:o_xbc + CONV_DIM], w_q,
                             w[:, o_k:o_k + KV_WIDTH], w[:, o_v:o_v + KV_WIDTH]], axis=1).astype(BF16)
    w_dtT = w[:, o_dt:o_dt + 2 * SSD_HEADS].T.astype(BF16)

    seq_blocks = seq // ROWS_IN
    cos, sin = _rope_tables(seq)
    segs_x = [("z", _COL_Z, SSD_INNER), ("xbc", _COL_XBC, CONV_DIM), ("dtT", 0, 0),
              ("q", _COL_Q, ATTN_WIDTH), ("k", _COL_K, KV_WIDTH), ("v", _COL_V, KV_WIDTH)]
    z_x, xbc_x, dtr_x, q_x, k_x, v_x = _inproj(
        x.reshape(t, d), mod3, norm1[layer][None, :], w_cat, w_dtT, (cos, sin),
        rows=ROWS_IN, mod_row_fn=lambda i: i // seq_blocks, segs=segs_x, seq_blocks=seq_blocks)
    segs_c = [("xbc", _COL_XBC, CONV_DIM), ("dtT", 0, 0), ("k", _COL_K, KV_WIDTH), ("v", _COL_V, KV_WIDTH)]
    xbc_c, dtr_c, k_c, v_c = _inproj(
        ctx.reshape(bt * n_ctx, d), mod3, norm1[layer][None, :], w_cat, w_dtT, None,
        rows=n_ctx, mod_row_fn=lambda i: bt, segs=segs_c, seq_blocks=1)

    cw = jnp.zeros((SUBLANES, CONV_DIM), F32).at[:CONV_WIDTH].set(conv_w[layer])
    cb = conv_b[layer][None, :]
    bias_rows = jnp.broadcast_to(dt_bias[layer].reshape(2 * SSD_HEADS, 1), (2 * SSD_HEADS, LANES))
    alog_rows = jnp.broadcast_to(a_log[layer].reshape(2 * SSD_HEADS, 1), (2 * SSD_HEADS, LANES))
    dskip_e = jnp.repeat(d_skip[layer], SSD_HEAD_DIM)[None, :]
    consts = (cw, cb, bias_rows, alog_rows, dskip_e, ssd_norm[layer][None, :], _head_expand_mat())

    zeros_state = jnp.zeros((bt, SSD_STATE, SSD_INNER), F32)
    z_dummy = jnp.zeros((bt, n_ctx, SSD_INNER), F32)
    h_cf, h_cb = _ssd(xbc_c.reshape(bt, n_ctx, CONV_DIM), dtr_c, z_dummy, consts,
                      zeros_state, zeros_state, rows=n_ctx, need_y=False)
    ssd_x, _, _ = _ssd(xbc_x.reshape(bt, seq, CONV_DIM), dtr_x, z_x.reshape(bt, seq, SSD_INNER),
                       consts, h_cf, h_cb, rows=ROWS_SSD, need_y=True)

    attn_x = _attention(attn_sinks[layer].astype(F32),
                        q_x.reshape(bt, seq, ATTN_WIDTH), k_x.reshape(bt, seq, KV_WIDTH),
                        v_x.reshape(bt, seq, KV_WIDTH), k_c.reshape(bt, n_ctx, KV_WIDTH),
                        v_c.reshape(bt, n_ctx, KV_WIDTH))

    wo = w_out[layer]
    wa = wo[:SSD_INNER].astype(BF16)
    wb = wo[SSD_INNER:][q_perm].astype(BF16)
    wr = jnp.zeros((ROUTE_ROWS, d), F32).at[:N_GROUPS].set(w_group[layer].T)
    wr = wr.at[N_GROUPS:N_GROUPS + N_EXPERTS].set(w_expert[layer].T).astype(BF16)
    br = jnp.zeros((ROUTE_ROWS,), F32).at[:N_GROUPS].set(b_group[layer])
    br = br.at[N_GROUPS:N_GROUPS + N_EXPERTS].set(b_expert[layer])
    br = jnp.broadcast_to(br[:, None], (ROUTE_ROWS, ROWS_IN))
    tri = jnp.asarray(np.triu(np.ones((ROWS_IN, ROWS_IN), np.float32), 1), BF16)
    x1, h2, ids, wcol, cnt = _outproj(
        ssd_x.reshape(t, SSD_INNER), attn_x.reshape(t, ATTN_WIDTH), x.reshape(t, d), mod3, norm2[layer][None, :],
        wa, wb, wr, br, tri, rows=ROWS_IN, seq_blocks=seq_blocks)

    counts = cnt[N_GROUPS:N_GROUPS + N_EXPERTS, 0].astype(jnp.int32)
    padded = (counts + MOE_BLK - 1) // MOE_BLK * MOE_BLK
    ends = jnp.cumsum(padded)
    pstart = ends - padded
    nblocks = (2 * t) // MOE_BLK + N_EXPERTS
    dest = pstart[ids[0:2]] + ids[2:4]
    tok_ids = jnp.broadcast_to(jnp.arange(t, dtype=jnp.int32)[None, :], (2, t))
    tok_buf = jnp.zeros((nblocks * MOE_BLK,), jnp.int32).at[dest.reshape(-1)].set(tok_ids.reshape(-1))
    blk_e = jnp.minimum(jnp.searchsorted(ends, jnp.arange(nblocks, dtype=jnp.int32) * MOE_BLK, side="right"),
                        N_EXPERTS - 1).astype(jnp.int32)
    nused = (ends[-1] // MOE_BLK).astype(jnp.int32).reshape(1)

    y_buf = _experts(blk_e, nused, tok_buf, h2, w_gate[layer], w_up[layer], w_down[layer], nblocks=nblocks)
    dest_flat = dest.T.reshape(-1).astype(jnp.int32)
    out = _combine(dest_flat, y_buf, x1, wcol, mod3, norm_final[None, :], rows=ROWS_CMB,
                   seq_blocks=seq // ROWS_CMB)
    return dict(out=out.reshape(bt, seq, d), mod=mod, z=z_x, xbc=xbc_x, dtr=dtr_x, q=q_x, k=k_x, v=v_x,
                k_c=k_c, v_c=v_c, h_cf=h_cf, h_cb=h_cb, ssd=ssd_x, attn=attn_x, x1=x1, h2=h2, ids=ids,
                wcol=wcol, cnt=cnt, y_buf=y_buf, dest=dest, q_perm=q_perm)
```

```python
import functools
import math

import numpy as np
import jax
import jax.numpy as jnp
from jax import lax
from jax.experimental import pallas as pl
from jax.experimental.pallas import tpu as pltpu

F32 = jnp.float32
BF16 = jnp.bfloat16

D_MODEL = 1024
GRID_W = 64
EPS = 1e-6
SSD_INNER = 512
SSD_HEAD_DIM = 64
SSD_HEADS = 8
SSD_GROUPS = 2
SSD_STATE = 128
CONV_WIDTH = 5
CONV_DIM = SSD_INNER + 2 * SSD_GROUPS * SSD_STATE
CHUNK = 128
ATTN_WIDTH = 512
HEAD_DIM = 64
ATTN_Q_HEADS = 8
ATTN_KV_HEADS = 2
Q_PER_KV = 4
KV_WIDTH = 128
WINDOW = 128
ROPE_BASE = 10000.0
N_GROUPS = 4
EXPERTS_PER_GROUP = 8
N_EXPERTS = 32
EXPERT_DIM = 512

LANES = 128
SUBLANES = 8
NEG = -1e30

ROWS_IN = 512
ROWS_SSD = 512
ROWS_ATT = 256
ROUTE_ROWS = 48
MOE_BLK = 128
RANK_BITS = 16
D_TILES = D_MODEL // LANES
ROWS_CMB = 128


def _silu(v):
    return v * (1.0 / (1.0 + jnp.exp(-v)))


def _softplus(v):
    return jnp.maximum(v, 0.0) + jnp.log(1.0 + jnp.exp(-jnp.abs(v)))


def _ada_kernel(c_ref, w_ref, b_ref, o_ref):
    a = _silu(c_ref[...])
    o_ref[...] = jnp.dot(a.astype(BF16), w_ref[...].astype(BF16),
                         preferred_element_type=F32) + b_ref[...]


def _ada(cc, w_ada, b_ada):
    d, n = w_ada.shape
    tn = 1536
    return pl.pallas_call(
        _ada_kernel,
        grid=(n // tn,),
        in_specs=[pl.BlockSpec((SUBLANES, d), lambda j: (0, 0)),
                  pl.BlockSpec((d, tn), lambda j: (0, j)),
                  pl.BlockSpec((1, tn), lambda j: (0, j))],
        out_specs=pl.BlockSpec((SUBLANES, tn), lambda j: (0, j)),
        out_shape=jax.ShapeDtypeStruct((SUBLANES, n), F32),
        name="ada",
    )(cc, w_ada, b_ada)


def _rope(t, cos, sin_signed, first_half):
    rot = jnp.where(first_half, pltpu.roll(t, LANES - HEAD_DIM // 2, 1), pltpu.roll(t, HEAD_DIM // 2, 1))
    return t * cos + rot * sin_signed


def _inproj_kernel(*refs, segs, rope):
    x_ref, sh_ref, sc_ref, g_ref, w_ref, wdt_ref = refs[:6]
    pos = 6
    if rope:
        cos_ref, sin_ref = refs[6:8]
        pos = 8
    outs = refs[pos:]
    x = x_ref[...]
    ms = jnp.mean(x * x, axis=-1, keepdims=True)
    h = (x * lax.rsqrt(ms + EPS)) * (g_ref[...] * (1.0 + sc_ref[0])) + sh_ref[0]
    hb = h.astype(BF16)
    if rope:
        cos = cos_ref[...]
        lane = lax.broadcasted_iota(jnp.int32, cos.shape, 1)
        first_half = (lane & (HEAD_DIM - 1)) < HEAD_DIM // 2
        sin_s = jnp.where(first_half, -sin_ref[...], sin_ref[...])
    for (kind, c0, width), o_ref in zip(segs, outs):
        if kind == "dtT":
            dtt = lax.dot_general(wdt_ref[...], hb, (((1,), (1,)), ((), ())), preferred_element_type=F32)
            for j in range(o_ref.shape[0]):
                o_ref[j] = dtt[:, j * CHUNK:(j + 1) * CHUNK]
            continue
        for j0 in range(0, width, 512):
            wj = min(512, width - j0)
            acc = jnp.dot(hb, w_ref[:, c0 + j0:c0 + j0 + wj], preferred_element_type=F32)
            if rope and kind in ("q", "k"):
                scale = HEAD_DIM ** -0.5 if kind == "q" else 1.0
                for l0 in range(0, wj, LANES):
                    t = _rope(acc[:, l0:l0 + LANES], cos, sin_s, first_half)
                    o_ref[:, j0 + l0:j0 + l0 + LANES] = (t * scale).astype(o_ref.dtype)
            else:
                o_ref[:, j0:j0 + wj] = acc.astype(o_ref.dtype)


_COL_Z, _COL_XBC, _COL_Q, _COL_K, _COL_V = 0, 512, 1536, 2048, 2176
_W_COLS = 2304


def _inproj(xf, mod3, norm1, w_cat, w_dtT, rope_tabs, *, rows, mod_row_fn, segs, seq_blocks):
    t, d = xf.shape
    rope = rope_tabs is not None
    in_specs = [
        pl.BlockSpec((rows, d), lambda i: (i, 0)),
        pl.BlockSpec((1, 1, d), lambda i: (mod_row_fn(i), 0, 0)),
        pl.BlockSpec((1, 1, d), lambda i: (mod_row_fn(i), 0, 1)),
        pl.BlockSpec((1, d), lambda i: (0, 0)),
        pl.BlockSpec(w_cat.shape, lambda i: (0, 0)),
        pl.BlockSpec(w_dtT.shape, lambda i: (0, 0)),
    ]
    args = [xf, mod3, mod3, norm1, w_cat, w_dtT]
    if rope:
        in_specs += [pl.BlockSpec((rows, LANES), lambda i: (i % seq_blocks, 0))] * 2
        args += list(rope_tabs)
    out_specs, out_shapes = [], []
    for kind, _, width in segs:
        if kind == "dtT":
            out_specs.append(pl.BlockSpec((rows // CHUNK, 2 * SSD_HEADS, CHUNK), lambda i: (i, 0, 0)))
            out_shapes.append(jax.ShapeDtypeStruct((t // CHUNK, 2 * SSD_HEADS, CHUNK), F32))
        else:
            dt = BF16 if kind in ("q", "k", "v") else F32
            out_specs.append(pl.BlockSpec((rows, width), lambda i: (i, 0)))
            out_shapes.append(jax.ShapeDtypeStruct((t, width), dt))
    return pl.pallas_call(
        functools.partial(_inproj_kernel, segs=tuple(segs), rope=rope),
        grid=(t // rows,),
        in_specs=in_specs,
        out_specs=out_specs,
        out_shape=out_shapes,
        name="inproj_rope" if rope else "inproj_ctx",
    )(*args)


def _lane_cumsum(v, lane, reverse):
    k = 1
    while k < LANES:
        if reverse:
            v = v + jnp.where(lane < LANES - k, pltpu.roll(v, LANES - k, 1), 0.0)
        else:
            v = v + jnp.where(lane >= k, pltpu.roll(v, k, 1), 0.0)
        k *= 2
    return v


def _ssd_kernel(*refs, need_y, nsteps, cps):
    (xbc_ref, prev_ref, next_ref, dtr_ref, z_ref, cw_ref, cb_ref, bias_ref, alog_ref,
     dskip_ref, gain_ref, exp_ref, h0f_ref, h0b_ref) = refs[:14]
    if need_y:
        y_ref, hf_out, hb_out = refs[14:17]
        rest = refs[17:]
    else:
        y_ref = None
        hf_out, hb_out = refs[14:16]
        rest = refs[16:]
    win, cwin, u_ref, hf, hb, hbs = rest

    sw = pl.program_id(1)
    s = pl.program_id(2)
    bi = jnp.where(sw == 0, nsteps - 1 - s, s)
    rows = xbc_ref.shape[1]
    h = SSD_HEADS

    @pl.when((sw == 0) & (s == 0))
    def _():
        hb[...] = h0b_ref[0]

    @pl.when((sw == 1) & (s == 0))
    def _():
        hf[...] = h0f_ref[0]

    win[0:SUBLANES, :] = jnp.where(bi > 0, prev_ref[0], 0.0)
    win[SUBLANES:SUBLANES + rows, :] = xbc_ref[0]
    win[SUBLANES + rows:, :] = jnp.where(bi < nsteps - 1, next_ref[0], 0.0)

    lane = lax.broadcasted_iota(jnp.int32, (2 * h, LANES), 1)
    ri = lax.broadcasted_iota(jnp.int32, (CHUNK, CHUNK), 0)
    ci = lax.broadcasted_iota(jnp.int32, (CHUNK, CHUNK), 1)
    lane128 = lax.broadcasted_iota(jnp.int32, (CHUNK, LANES), 1)
    lo_mask = lane128 < SSD_HEAD_DIM
    neg_a = -jnp.exp(alog_ref[...])

    def conv_silu(c, ncols):
        r0 = pl.multiple_of(c * CHUNK, CHUNK)
        cwin[...] = win[pl.ds(r0, CHUNK + 2 * SUBLANES), :]
        for j0 in range(0, ncols, LANES):
            acc = jnp.broadcast_to(cb_ref[:, j0:j0 + LANES], (CHUNK, LANES))
            for k in range(CONV_WIDTH):
                off = SUBLANES - CONV_WIDTH // 2 + k
                acc = acc + cwin[off:off + CHUNK, j0:j0 + LANES] * cw_ref[k:k + 1, j0:j0 + LANES]
            u_ref[:, j0:j0 + LANES] = _silu(acc)

    def dt_rows(c):
        dt = _softplus(dtr_ref[c] + bias_ref[...])
        a = dt * neg_a
        pre = _lane_cumsum(a, lane, False)
        suf = _lane_cumsum(a, lane, True)
        return dt, pre, suf

    def state_update(state_ref, w_rows, tot_col):
        dec = jnp.exp(jnp.sum(tot_col * exp_ref[...], axis=0, keepdims=True))
        for g in range(SSD_GROUPS):
            bm = u_ref[:, SSD_INNER + g * SSD_STATE:SSD_INNER + (g + 1) * SSD_STATE]
            bt = bm.T
            for pr in range(2):
                h0 = g * 4 + pr * 2
                c0 = h0 * SSD_HEAD_DIM
                xp = u_ref[:, c0:c0 + LANES]
                rhs = jnp.concatenate([jnp.where(lo_mask, xp, 0.0), jnp.where(lo_mask, 0.0, xp)],
                                      axis=0).astype(BF16)
                lhs = jnp.concatenate([bt * w_rows[h0:h0 + 1, :], bt * w_rows[h0 + 1:h0 + 2, :]],
                                      axis=1).astype(BF16)
                sres = jnp.dot(lhs, rhs, preferred_element_type=F32)
                state_ref[:, c0:c0 + LANES] = state_ref[:, c0:c0 + LANES] * dec[:, c0:c0 + LANES] + sres

    def col_forms(rows16):
        padded = jnp.concatenate([rows16, jnp.zeros((CHUNK - 2 * h, LANES), F32)], axis=0)
        return padded.T

    def backward_chunk(k, carry):
        c = cps - 1 - k
        gc = bi * cps + c
        conv_silu(c, SSD_INNER + SSD_GROUPS * SSD_STATE)
        dt, _, suf = dt_rows(c)
        hbs[gc] = hb[...].astype(BF16)
        s0 = suf[:, 0:1]
        w_rows = jnp.exp(s0 - suf) * dt
        state_update(hb, w_rows[h:2 * h, :], s0[h:2 * h, :])
        return carry

    def forward_chunk(c, carry):
        gc = bi * cps + c
        r0 = pl.multiple_of(c * CHUNK, CHUNK)
        conv_silu(c, CONV_DIM)
        dt, pre, suf = dt_rows(c)
        rowsf = jnp.concatenate([pre[0:h, :], suf[h:2 * h, :]], axis=0)
        colsf = col_forms(rowsf)
        ecol = jnp.exp(colsf)
        for g in range(SSD_GROUPS):
            bm = u_ref[:, SSD_INNER + g * SSD_STATE:SSD_INNER + (g + 1) * SSD_STATE]
            cm = u_ref[:, SSD_INNER + (SSD_GROUPS + g) * SSD_STATE:SSD_INNER + (SSD_GROUPS + g + 1) * SSD_STATE]
            cmb = cm.astype(BF16)
            cbm = lax.dot_general(cmb, bm.astype(BF16), (((1,), (1,)), ((), ())),
                                  preferred_element_type=F32)
            for pr in range(2):
                h0 = g * 4 + pr * 2
                c0 = h0 * SSD_HEAD_DIM
                lhs_parts = []
                for hh in (h0, h0 + 1):
                    segf = colsf[:, hh:hh + 1] - rowsf[hh:hh + 1, :]
                    gf = jnp.exp(jnp.where(ci <= ri, segf, NEG)) * dt[hh:hh + 1, :]
                    segb = colsf[:, h + hh:h + hh + 1] - rowsf[h + hh:h + hh + 1, :]
                    gb = jnp.exp(jnp.where(ci >= ri, segb, NEG)) * dt[h + hh:h + hh + 1, :]
                    lhs_parts.append((cbm * (gf + gb)).astype(BF16))
                for hh in (h0, h0 + 1):
                    lhs_parts.append((cm * ecol[:, hh:hh + 1]).astype(BF16))
                for hh in (h0, h0 + 1):
                    lhs_parts.append((cm * ecol[:, h + hh:h + hh + 1]).astype(BF16))
                lhs = jnp.concatenate(lhs_parts, axis=1)
                xp = u_ref[:, c0:c0 + LANES]
                sf = hf[:, c0:c0 + LANES]
                sb = hbs[gc, :, c0:c0 + LANES].astype(F32)
                rhs = jnp.concatenate(
                    [jnp.where(lo_mask, xp, 0.0), jnp.where(lo_mask, 0.0, xp),
                     jnp.where(lo_mask, sf, 0.0), jnp.where(lo_mask, 0.0, sf),
                     jnp.where(lo_mask, sb, 0.0), jnp.where(lo_mask, 0.0, sb)], axis=0).astype(BF16)
                ypair = jnp.dot(lhs, rhs, preferred_element_type=F32)
                ypair = ypair + dskip_ref[:, c0:c0 + LANES] * xp
                zz = z_ref[0, pl.ds(r0, CHUNK), c0:c0 + LANES]
                u_ref[:, CONV_DIM + c0:CONV_DIM + c0 + LANES] = ypair * _silu(zz)
        last = pre[:, LANES - 1:LANES]
        w_rows = jnp.exp(last - pre) * dt
        state_update(hf, w_rows[0:h, :], last[0:h, :])
        if need_y:
            yv = u_ref[:, CONV_DIM:CONV_DIM + SSD_INNER]
            ms = jnp.mean(yv * yv, axis=-1, keepdims=True)
            y_ref[0, pl.ds(r0, CHUNK), :] = (yv * lax.rsqrt(ms + EPS) * gain_ref[...]).astype(y_ref.dtype)
        return carry

    @pl.when(sw == 0)
    def _():
        lax.fori_loop(0, cps, backward_chunk, 0)

    @pl.when(sw == 1)
    def _():
        lax.fori_loop(0, cps, forward_chunk, 0)

    @pl.when((sw == 0) & (s == nsteps - 1))
    def _():
        hb_out[0] = hb[...]

    @pl.when((sw == 1) & (s == nsteps - 1))
    def _():
        hf_out[0] = hf[...]


def _ssd(xbc, dtr, z, consts, h0f, h0b, *, rows, need_y):
    bt, seq, _ = xbc.shape
    nsteps = seq // rows
    cps = rows // CHUNK
    hb8 = rows // SUBLANES
    nb8 = seq // SUBLANES
    cw, cb, bias, alog, dskip, gain, expm = consts

    def blk(b, sw, s):
        return jnp.where(sw == 0, nsteps - 1 - s, s)

    def full(a):
        return pl.BlockSpec(a.shape, lambda b, sw, s: (0,) * a.ndim)

    in_specs = [
        pl.BlockSpec((1, rows, CONV_DIM), lambda b, sw, s: (b, blk(b, sw, s), 0)),
        pl.BlockSpec((1, SUBLANES, CONV_DIM), lambda b, sw, s: (b, jnp.maximum(blk(b, sw, s) * hb8 - 1, 0), 0)),
        pl.BlockSpec((1, SUBLANES, CONV_DIM),
                     lambda b, sw, s: (b, jnp.minimum((blk(b, sw, s) + 1) * hb8, nb8 - 1), 0)),
        pl.BlockSpec((cps, 2 * SSD_HEADS, CHUNK), lambda b, sw, s: (b * nsteps + blk(b, sw, s), 0, 0)),
        pl.BlockSpec((1, rows, SSD_INNER), lambda b, sw, s: (b, jnp.where(sw == 0, 0, s), 0)),
        full(cw), full(cb), full(bias), full(alog), full(dskip), full(gain), full(expm),
        pl.BlockSpec((1, SSD_STATE, SSD_INNER), lambda b, sw, s: (b, 0, 0)),
        pl.BlockSpec((1, SSD_STATE, SSD_INNER), lambda b, sw, s: (b, 0, 0)),
    ]
    st_spec = pl.BlockSpec((1, SSD_STATE, SSD_INNER), lambda b, sw, s: (b, 0, 0))
    st_shape = jax.ShapeDtypeStruct((bt, SSD_STATE, SSD_INNER), F32)
    out_specs, out_shapes = [st_spec, st_spec], [st_shape, st_shape]
    if need_y:
        out_specs = [pl.BlockSpec((1, rows, SSD_INNER), lambda b, sw, s: (b, jnp.where(sw == 0, 0, s), 0))] + out_specs
        out_shapes = [jax.ShapeDtypeStruct((bt, seq, SSD_INNER), BF16)] + out_shapes
    scratch = [
        pltpu.VMEM((rows + 2 * SUBLANES, CONV_DIM), F32),
        pltpu.VMEM((CHUNK + 2 * SUBLANES, CONV_DIM), F32),
        pltpu.VMEM((CHUNK, CONV_DIM + SSD_INNER), F32),
        pltpu.VMEM((SSD_STATE, SSD_INNER), F32),
        pltpu.VMEM((SSD_STATE, SSD_INNER), F32),
        pltpu.VMEM((seq // CHUNK, SSD_STATE, SSD_INNER), BF16),
    ]
    return pl.pallas_call(
        functools.partial(_ssd_kernel, need_y=need_y, nsteps=nsteps, cps=cps),
        grid=(bt, 2, nsteps),
        in_specs=in_specs,
        out_specs=out_specs,
        out_shape=out_shapes,
        scratch_shapes=scratch,
        compiler_params=pltpu.CompilerParams(dimension_semantics=("arbitrary", "arbitrary", "arbitrary")),
        name="ssd_y" if need_y else "ssd_ctx",
    )(xbc, xbc, xbc, dtr, z, cw, cb, bias, alog, dskip, gain, expm, h0f, h0b)


def _attn_kernel(sink_ref, q_ref, kp_ref, kc_ref, kn_ref, vp_ref, vc_ref, vn_ref, kx_ref, vx_ref, o_ref, *, nsteps):
    m = pl.program_id(1)
    lane = lax.broadcasted_iota(jnp.int32, (1, LANES), 1)
    lo = lane < HEAD_DIM
    rq = Q_PER_KV * WINDOW
    qi = lax.broadcasted_iota(jnp.int32, (rq, WINDOW), 0) & (WINDOW - 1)
    kj = lax.broadcasted_iota(jnp.int32, (rq, WINDOW), 1)
    rblk = lax.shift_right_logical(lax.broadcasted_iota(jnp.int32, (rq, 1), 0), int(math.log2(WINDOW)))
    zero = jnp.zeros((), BF16)

    kpieces = [kp_ref[0], kc_ref[0, 0:WINDOW], kc_ref[0, WINDOW:2 * WINDOW], kn_ref[0]]
    vpieces = [vp_ref[0], vc_ref[0, 0:WINDOW], vc_ref[0, WINDOW:2 * WINDOW], vn_ref[0]]
    kx = kx_ref[0]
    vx = vx_ref[0]
    nt = (((1,), (1,)), ((), ()))

    for sb in range(ROWS_ATT // WINDOW):
        qsb = q_ref[0, sb * WINDOW:(sb + 1) * WINDOW, :]
        prev_ok = jnp.logical_or(m > 0, sb > 0)
        next_ok = jnp.logical_or(m < nsteps - 1, sb < ROWS_ATT // WINDOW - 1)
        outs = [None] * Q_PER_KV
        for kv in range(ATTN_KV_HEADS):
            sel = lo if kv == 0 else jnp.logical_not(lo)
            qs = jnp.concatenate(
                [jnp.where(sel, qsb[:, j * LANES:(j + 1) * LANES], zero) for j in range(Q_PER_KV)], axis=0)
            s_prev = lax.dot_general(qs, kpieces[sb], nt, preferred_element_type=F32)
            s_cur = lax.dot_general(qs, kpieces[sb + 1], nt, preferred_element_type=F32)
            s_next = lax.dot_general(qs, kpieces[sb + 2], nt, preferred_element_type=F32)
            s_ctx = lax.dot_general(qs, kx, nt, preferred_element_type=F32)
            s_prev = jnp.where((kj >= qi) & prev_ok, s_prev, NEG)
            s_next = jnp.where((kj <= qi) & next_ok, s_next, NEG)
            sink = jnp.zeros((rq, 1), F32)
            for j in range(Q_PER_KV):
                sink = jnp.where(rblk == j, sink_ref[kv * Q_PER_KV + j], sink)
            mx = jnp.maximum(
                jnp.maximum(jnp.max(s_prev, axis=-1, keepdims=True), jnp.max(s_cur, axis=-1, keepdims=True)),
                jnp.maximum(jnp.max(s_next, axis=-1, keepdims=True), jnp.max(s_ctx, axis=-1, keepdims=True)))
            mx = jnp.maximum(mx, sink)
            p_prev = jnp.exp(s_prev - mx)
            p_cur = jnp.exp(s_cur - mx)
            p_next = jnp.exp(s_next - mx)
            p_ctx = jnp.exp(s_ctx - mx)
            den = (jnp.sum(p_prev, axis=-1, keepdims=True) + jnp.sum(p_cur, axis=-1, keepdims=True)
                   + jnp.sum(p_next, axis=-1, keepdims=True) + jnp.sum(p_ctx, axis=-1, keepdims=True)
                   + jnp.exp(sink - mx))
            acc = jnp.dot(p_prev.astype(BF16), jnp.where(sel, vpieces[sb], zero), preferred_element_type=F32)
            acc += jnp.dot(p_cur.astype(BF16), jnp.where(sel, vpieces[sb + 1], zero), preferred_element_type=F32)
            acc += jnp.dot(p_next.astype(BF16), jnp.where(sel, vpieces[sb + 2], zero), preferred_element_type=F32)
            acc += jnp.dot(p_ctx.astype(BF16), jnp.where(sel, vx, zero), preferred_element_type=F32)
            acc = acc * (1.0 / den)
            for j in range(Q_PER_KV):
                part = acc[j * WINDOW:(j + 1) * WINDOW, :]
                outs[j] = part if outs[j] is None else outs[j] + part
        for j in range(Q_PER_KV):
            o_ref[0, sb * WINDOW:(sb + 1) * WINDOW, j * LANES:(j + 1) * LANES] = outs[j].astype(o_ref.dtype)


def _attention(sinks, q, k, v, kx, vx):
    bt, seq, _ = q.shape
    nsteps = seq // ROWS_ATT
    per = ROWS_ATT // WINDOW
    nb = seq // WINDOW
    n_ctx = kx.shape[1]

    def prev(b, m, s):
        return (b, jnp.maximum(m * per - 1, 0), 0)

    def cur(b, m, s):
        return (b, m, 0)

    def nxt(b, m, s):
        return (b, jnp.minimum((m + 1) * per, nb - 1), 0)

    small = (1, WINDOW, KV_WIDTH)
    big = (1, ROWS_ATT, KV_WIDTH)
    grid_spec = pltpu.PrefetchScalarGridSpec(
        num_scalar_prefetch=1,
        grid=(bt, nsteps),
        in_specs=[
            pl.BlockSpec((1, ROWS_ATT, ATTN_WIDTH), cur),
            pl.BlockSpec(small, prev), pl.BlockSpec(big, cur), pl.BlockSpec(small, nxt),
            pl.BlockSpec(small, prev), pl.BlockSpec(big, cur), pl.BlockSpec(small, nxt),
            pl.BlockSpec((1, n_ctx, KV_WIDTH), lambda b, m, s: (b, 0, 0)),
            pl.BlockSpec((1, n_ctx, KV_WIDTH), lambda b, m, s: (b, 0, 0)),
        ],
        out_specs=pl.BlockSpec((1, ROWS_ATT, ATTN_WIDTH), cur),
    )
    return pl.pallas_call(
        functools.partial(_attn_kernel, nsteps=nsteps),
        grid_spec=grid_spec,
        out_shape=jax.ShapeDtypeStruct((bt, seq, ATTN_WIDTH), BF16),
        name="attn",
    )(sinks, q, k, k, k, v, v, v, kx, vx)


def _outproj_kernel(ssd_ref, att_ref, x_ref, g1_ref, sh_ref, sc_ref, n2_ref, wa_ref, wb_ref, wr_ref, br_ref,
                    tri_ref, x1_ref, h2_ref, ids_ref, wcol_ref, cnt_ref, carry, *, nsteps):
    i = pl.program_id(0)
    rows = x_ref.shape[0]

    @pl.when(i == 0)
    def _():
        carry[...] = jnp.zeros_like(carry)

    acc = jnp.dot(ssd_ref[...], wa_ref[...], preferred_element_type=F32)
    acc += jnp.dot(att_ref[...], wb_ref[...], preferred_element_type=F32)
    x1 = x_ref[...] + g1_ref[0] * acc
    x1_ref[...] = x1
    ms = jnp.mean(x1 * x1, axis=-1, keepdims=True)
    h2 = (x1 * lax.rsqrt(ms + EPS)) * (n2_ref[...] * (1.0 + sc_ref[0])) + sh_ref[0]
    for sidx in range(d_tiles := h2.shape[1] // LANES):
        h2_ref[pl.ds(sidx, rows, stride=d_tiles), :] = h2[:, sidx * LANES:(sidx + 1) * LANES]
    lt = lax.dot_general(wr_ref[...], h2.astype(BF16), (((1,), (1,)), ((), ())),
                         preferred_element_type=F32) + br_ref[...]
    row = lax.broadcasted_iota(jnp.int32, lt.shape, 0).astype(F32)
    big_i = float(ROUTE_ROWS)
    gl = jnp.where(row < N_GROUPS, lt, NEG)
    gmax = jnp.max(gl, axis=0, keepdims=True)
    gidx = jnp.min(jnp.where(gl == gmax, row, big_i), axis=0, keepdims=True)
    g_w = 1.0 / jnp.sum(jnp.exp(gl - gmax), axis=0, keepdims=True)
    lo = N_GROUPS + EXPERTS_PER_GROUP * gidx
    el = jnp.where((row >= lo) & (row < lo + EXPERTS_PER_GROUP), lt, NEG)
    m1 = jnp.max(el, axis=0, keepdims=True)
    i1 = jnp.min(jnp.where(el == m1, row, big_i), axis=0, keepdims=True)
    el2 = jnp.where(row == i1, NEG, el)
    m2 = jnp.max(el2, axis=0, keepdims=True)
    i2 = jnp.min(jnp.where(el2 == m2, row, big_i), axis=0, keepdims=True)
    r = jnp.exp(m2 - m1)
    w1 = g_w / (1.0 + r)
    w2 = g_w * r / (1.0 + r)
    oh1 = row == i1
    oh2 = row == i2
    cnt = jnp.where(oh1 | oh2, 1.0, 0.0)
    prefix = jnp.dot(cnt.astype(BF16), tri_ref[...], preferred_element_type=F32)
    base = carry[...] + prefix
    rank1 = jnp.sum(jnp.where(oh1, base, 0.0), axis=0, keepdims=True)
    rank2 = jnp.sum(jnp.where(oh2, base, 0.0), axis=0, keepdims=True)
    carry[...] = carry[...] + jnp.sum(cnt, axis=1, keepdims=True)
    r8 = lax.broadcasted_iota(jnp.int32, (SUBLANES, rows), 0)
    code1 = (i1 - N_GROUPS) * float(1 << RANK_BITS) + rank1
    code2 = (i2 - N_GROUPS) * float(1 << RANK_BITS) + rank2
    ids_ref[...] = jnp.where(r8 == 0, code1, jnp.where(r8 == 1, code2, 0.0)).astype(jnp.int32)
    r128 = lax.broadcasted_iota(jnp.int32, (LANES, rows), 0)
    wfull = jnp.where(r128 == 0, w1, jnp.where(r128 == 1, w2, 0.0))
    for j in range(rows // LANES):
        wcol_ref[j * LANES:(j + 1) * LANES, :] = wfull[:, j * LANES:(j + 1) * LANES].T

    @pl.when(i == nsteps - 1)
    def _():
        cnt_ref[...] = carry[:, 0:LANES]


def _outproj(ssd, att, xf, mod3, norm2, wa, wb, wr, br, tri, *, rows, seq_blocks):
    t, d = xf.shape
    nsteps = t // rows

    def mod(col):
        return pl.BlockSpec((1, 1, d), lambda i: (i // seq_blocks, 0, col))

    def full(a):
        return pl.BlockSpec(a.shape, lambda i: (0,) * a.ndim)

    return pl.pallas_call(
        functools.partial(_outproj_kernel, nsteps=nsteps),
        grid=(nsteps,),
        in_specs=[pl.BlockSpec((rows, SSD_INNER), lambda i: (i, 0)),
                  pl.BlockSpec((rows, ATTN_WIDTH), lambda i: (i, 0)),
                  pl.BlockSpec((rows, d), lambda i: (i, 0)),
                  mod(2), mod(3), mod(4), full(norm2), full(wa), full(wb), full(wr), full(br), full(tri)],
        out_specs=[pl.BlockSpec((rows, d), lambda i: (i, 0)),
                   pl.BlockSpec((rows * (d // LANES), LANES), lambda i: (i, 0)),
                   pl.BlockSpec((SUBLANES, rows), lambda i: (0, i)),
                   pl.BlockSpec((rows, LANES), lambda i: (i, 0)),
                   pl.BlockSpec((ROUTE_ROWS, LANES), lambda i: (0, 0))],
        out_shape=[jax.ShapeDtypeStruct((t, d), F32),
                   jax.ShapeDtypeStruct((t * (d // LANES), LANES), F32),
                   jax.ShapeDtypeStruct((SUBLANES, t), jnp.int32),
                   jax.ShapeDtypeStruct((t, LANES), F32),
                   jax.ShapeDtypeStruct((ROUTE_ROWS, LANES), F32)],
        scratch_shapes=[pltpu.VMEM((ROUTE_ROWS, rows), F32)],
        compiler_params=pltpu.CompilerParams(dimension_semantics=("arbitrary",)),
        name="outproj_router",
    )(ssd, att, xf, mod3, mod3, mod3, norm2, wa, wb, wr, br, tri)


def _expert_kernel(blk_e_ref, nused_ref, pstart_ref, counts_ref, code_ref, h2_hbm, wg_ref, wu_ref, wd_ref, y_ref,
                   xbuf, wgb, wub, wdb, tok_ref, sem, *, n_tok):
    b = pl.program_id(0)
    nused = nused_ref[0]
    slot = b % 2

    def row_copy(tok, r, sl):
        return pltpu.make_async_copy(h2_hbm.at[pl.ds(tok * D_TILES, D_TILES)],
                                     xbuf.at[sl, pl.ds(r * D_TILES, D_TILES)], sem.at[sl])

    def start_gather(blk, sl):
        for r in range(MOE_BLK):
            row_copy(tok_ref[blk * MOE_BLK + r], r, sl).start()

    @pl.when(b == 0)
    def _():
        def pad_expert(e, carry):
            lo = pstart_ref[e] + counts_ref[e]
            hi = pstart_ref[e] + (counts_ref[e] + MOE_BLK - 1) // MOE_BLK * MOE_BLK

            def pad_slot(s, c):
                tok_ref[s] = 0
                return c
            return lax.fori_loop(lo, hi, pad_slot, carry)
        lax.fori_loop(0, N_EXPERTS, pad_expert, 0)

        def claim(i, carry):
            code = code_ref[i]
            dst = pstart_ref[lax.shift_right_logical(code, RANK_BITS)] + (code & ((1 << RANK_BITS) - 1))
            tok_ref[dst] = jnp.where(i >= n_tok, i - n_tok, i)
            return carry
        lax.fori_loop(0, 2 * n_tok, claim, 0, unroll=8)

    @pl.when((b == 0) & (nused > 0))
    def _():
        start_gather(0, 0)

    @pl.when(b + 1 < nused)
    def _():
        start_gather(b + 1, 1 - slot)

    e = blk_e_ref[b]
    e_prev = blk_e_ref[jnp.maximum(b - 1, 0)]

    @pl.when((b == 0) | (e != e_prev))
    def _():
        wgb[...] = wg_ref[0].astype(BF16)
        wub[...] = wu_ref[0].astype(BF16)
        wdb[...] = wd_ref[0].astype(BF16)

    @pl.when(b < nused)
    def _():
        for r in range(MOE_BLK):
            row_copy(0, r, slot).wait()
        xb = jnp.concatenate([xbuf[slot, pl.ds(s, MOE_BLK, stride=D_TILES), :] for s in range(D_TILES)],
                             axis=1).astype(BF16)
        gate = jnp.dot(xb, wgb[...], preferred_element_type=F32)
        up = jnp.dot(xb, wub[...], preferred_element_type=F32)
        hmid = (_silu(gate) * up).astype(BF16)
        y = jnp.dot(hmid, wdb[...], preferred_element_type=F32)
        for s in range(D_TILES):
            y_ref[pl.ds(s, MOE_BLK, stride=D_TILES), :] = y[:, s * LANES:(s + 1) * LANES]

    @pl.when(b >= nused)
    def _():
        y_ref[...] = jnp.zeros_like(y_ref)


def _experts(blk_e, nused, pstart, counts, code, h2t, w_gate, w_up, w_down, *, nblocks, n_tok):
    d = w_gate.shape[1]

    def wspec(shape):
        return pl.BlockSpec(shape, lambda b, be, nu, ps, ct, cd: (be[b], 0, 0))

    grid_spec = pltpu.PrefetchScalarGridSpec(
        num_scalar_prefetch=5,
        grid=(nblocks,),
        in_specs=[pl.BlockSpec(memory_space=pl.ANY),
                  wspec((1, d, EXPERT_DIM)), wspec((1, d, EXPERT_DIM)), wspec((1, EXPERT_DIM, d))],
        out_specs=pl.BlockSpec((MOE_BLK * D_TILES, LANES), lambda b, be, nu, ps, ct, cd: (b, 0)),
        scratch_shapes=[
            pltpu.VMEM((2, MOE_BLK * D_TILES, LANES), F32),
            pltpu.VMEM((d, EXPERT_DIM), BF16),
            pltpu.VMEM((d, EXPERT_DIM), BF16),
            pltpu.VMEM((EXPERT_DIM, d), BF16),
            pltpu.SMEM((nblocks * MOE_BLK,), jnp.int32),
            pltpu.SemaphoreType.DMA((2,)),
        ],
    )
    return pl.pallas_call(
        functools.partial(_expert_kernel, n_tok=n_tok),
        grid_spec=grid_spec,
        out_shape=jax.ShapeDtypeStruct((nblocks * MOE_BLK * D_TILES, LANES), F32),
        compiler_params=pltpu.CompilerParams(dimension_semantics=("arbitrary",)),
        name="experts",
    )(blk_e, nused, pstart, counts, code, h2t, w_gate, w_up, w_down)


def _combine_kernel(pstart_ref, code_ref, y_hbm, x1_ref, wcol_ref, g2_ref, nf_ref, o_ref, ybuf, sem, *, nsteps, n_tok):
    i = pl.program_id(0)
    rows = x1_ref.shape[0]
    slot = i % 2

    def row_copy(src, r, k, sl):
        return pltpu.make_async_copy(y_hbm.at[pl.ds(src * D_TILES, D_TILES)],
                                     ybuf.at[sl, k, pl.ds(r * D_TILES, D_TILES)], sem.at[sl])

    def start_gather(step, sl):
        for r in range(rows):
            for k in range(2):
                code = code_ref[k * n_tok + step * rows + r]
                src = pstart_ref[lax.shift_right_logical(code, RANK_BITS)] + (code & ((1 << RANK_BITS) - 1))
                row_copy(src, r, k, sl).start()

    @pl.when(i == 0)
    def _():
        start_gather(0, 0)

    @pl.when(i + 1 < nsteps)
    def _():
        start_gather(i + 1, 1 - slot)

    for r in range(rows):
        for k in range(2):
            row_copy(0, r, k, slot).wait()
    w0 = wcol_ref[:, 0:1]
    w1 = wcol_ref[:, 1:2]
    ssq = jnp.zeros((rows, 1), F32)
    for s in range(D_TILES):
        cols = slice(s * LANES, (s + 1) * LANES)
        moe = (ybuf[slot, 0, pl.ds(s, rows, stride=D_TILES), :] * w0
               + ybuf[slot, 1, pl.ds(s, rows, stride=D_TILES), :] * w1)
        x2 = x1_ref[:, cols] + g2_ref[0, :, cols] * moe
        ssq = ssq + jnp.sum(x2 * x2, axis=-1, keepdims=True)
        o_ref[:, cols] = x2
    inv = lax.rsqrt(ssq * (1.0 / (D_TILES * LANES)) + EPS)
    o_ref[...] = o_ref[...] * inv * nf_ref[...]


def _combine(pstart, code, y_buf, x1, wcol, mod3, norm_final, *, rows, seq_blocks):
    t, d = x1.shape
    nsteps = t // rows
    grid_spec = pltpu.PrefetchScalarGridSpec(
        num_scalar_prefetch=2,
        grid=(nsteps,),
        in_specs=[
            pl.BlockSpec(memory_space=pl.ANY),
            pl.BlockSpec((rows, d), lambda i, ps, cd: (i, 0)),
            pl.BlockSpec((rows, LANES), lambda i, ps, cd: (i, 0)),
            pl.BlockSpec((1, 1, d), lambda i, ps, cd: (i // seq_blocks, 0, 5)),
            pl.BlockSpec((1, d), lambda i, ps, cd: (0, 0)),
        ],
        out_specs=pl.BlockSpec((rows, d), lambda i, ps, cd: (i, 0)),
        scratch_shapes=[pltpu.VMEM((2, 2, rows * D_TILES, LANES), F32), pltpu.SemaphoreType.DMA((2,))],
    )
    return pl.pallas_call(
        functools.partial(_combine_kernel, nsteps=nsteps, n_tok=t),
        grid_spec=grid_spec,
        out_shape=jax.ShapeDtypeStruct((t, d), F32),
        compiler_params=pltpu.CompilerParams(dimension_semantics=("arbitrary",)),
        name="combine",
    )(pstart, code, y_buf, x1, wcol, mod3, norm_final)


def _rope_tables(seq):
    pos = np.arange(seq)
    n_freq = HEAD_DIM // 4
    inv = ROPE_BASE ** (-np.arange(n_freq, dtype=np.float32) / n_freq)
    ang = np.concatenate([(pos // GRID_W)[:, None] * inv, (pos % GRID_W)[:, None] * inv], axis=-1)
    ang = np.concatenate([ang, ang, ang, ang], axis=-1).astype(np.float32)
    return jnp.asarray(np.cos(ang), F32), jnp.asarray(np.sin(ang), F32)


def _head_expand_mat():
    m = np.zeros((SSD_HEADS, SSD_INNER), np.float32)
    for hh in range(SSD_HEADS):
        m[hh, hh * SSD_HEAD_DIM:(hh + 1) * SSD_HEAD_DIM] = 1.0
    return jnp.asarray(m)


def kernel(x, c, ctx, c_ctx, w_ada, b_ada, norm1, w_in, conv_w, conv_b, dt_bias, a_log, d_skip, ssd_norm,
           attn_sinks, w_out, norm2, w_group, b_group, w_expert, b_expert, w_gate, w_up, w_down, norm_final):
    return _pipeline(x, c, ctx, c_ctx, w_ada, b_ada, norm1, w_in, conv_w, conv_b, dt_bias, a_log, d_skip,
                     ssd_norm, attn_sinks, w_out, norm2, w_group, b_group, w_expert, b_expert, w_gate, w_up,
                     w_down, norm_final)["out"]


def _pipeline(x, c, ctx, c_ctx, w_ada, b_ada, norm1, w_in, conv_w, conv_b, dt_bias, a_log, d_skip, ssd_norm,
              attn_sinks, w_out, norm2, w_group, b_group, w_expert, b_expert, w_gate, w_up, w_down, norm_final):
    bt, seq, d = x.shape
    n_ctx = ctx.shape[1]
    t = bt * seq
    layer = 0
    assert w_ada.shape[0] == 1 and seq % ROWS_IN == 0 and n_ctx % CHUNK == 0

    cc = jnp.zeros((SUBLANES, d), F32).at[:bt].set(c).at[bt].set(c_ctx)
    mod = _ada(cc, w_ada[layer], b_ada[layer][None, :])
    mod3 = mod.reshape(SUBLANES, 1, 6 * d)

    w = w_in[layer]
    o_z, o_xbc, o_dt = 0, SSD_INNER, SSD_INNER + CONV_DIM
    o_q = o_dt + 2 * SSD_HEADS
    o_k, o_v = o_q + ATTN_WIDTH, o_q + ATTN_WIDTH + KV_WIDTH
    q_perm = np.concatenate([np.r_[j * HEAD_DIM:(j + 1) * HEAD_DIM, (j + 4) * HEAD_DIM:(j + 5) * HEAD_DIM]
                             for j in range(Q_PER_KV)])
    w_q = w[:, o_q:o_q + ATTN_WIDTH].reshape(d, ATTN_KV_HEADS, Q_PER_KV, HEAD_DIM).transpose(0, 2, 1, 3)
    w_q = w_q.reshape(d, ATTN_WIDTH)
    w_cat = jnp.concatenate([w[:, o_z:o_z + SSD_INNER], w[:, o_xbc:o_xbc + CONV_DIM], w_q,
                             w[:, o_k:o_k + KV_WIDTH], w[:, o_v:o_v + KV_WIDTH]], axis=1).astype(BF16)
    w_dtT = w[:, o_dt:o_dt + 2 * SSD_HEADS].T.astype(BF16)

    seq_blocks = seq // ROWS_IN
    cos, sin = _rope_tables(seq)
    segs_x = [("z", _COL_Z, SSD_INNER), ("xbc", _COL_XBC, CONV_DIM), ("dtT", 0, 0),
              ("q", _COL_Q, ATTN_WIDTH), ("k", _COL_K, KV_WIDTH), ("v", _COL_V, KV_WIDTH)]
    z_x, xbc_x, dtr_x, q_x, k_x, v_x = _inproj(
        x.reshape(t, d), mod3, norm1[layer][None, :], w_cat, w_dtT, (cos, sin),
        rows=ROWS_IN, mod_row_fn=lambda i: i // seq_blocks, segs=segs_x, seq_blocks=seq_blocks)
    segs_c = [("xbc", _COL_XBC, CONV_DIM), ("dtT", 0, 0), ("k", _COL_K, KV_WIDTH), ("v", _COL_V, KV_WIDTH)]
    xbc_c, dtr_c, k_c, v_c = _inproj(
        ctx.reshape(bt * n_ctx, d), mod3, norm1[layer][None, :], w_cat, w_dtT, None,
        rows=n_ctx, mod_row_fn=lambda i: bt, segs=segs_c, seq_blocks=1)

    cw = jnp.zeros((SUBLANES, CONV_DIM), F32).at[:CONV_WIDTH].set(conv_w[layer])
    cb = conv_b[layer][None, :]
    bias_rows = jnp.broadcast_to(dt_bias[layer].reshape(2 * SSD_HEADS, 1), (2 * SSD_HEADS, LANES))
    alog_rows = jnp.broadcast_to(a_log[layer].reshape(2 * SSD_HEADS, 1), (2 * SSD_HEADS, LANES))
    dskip_e = jnp.repeat(d_skip[layer], SSD_HEAD_DIM)[None, :]
    consts = (cw, cb, bias_rows, alog_rows, dskip_e, ssd_norm[layer][None, :], _head_expand_mat())

    zeros_state = jnp.zeros((bt, SSD_STATE, SSD_INNER), F32)
    z_dummy = jnp.zeros((bt, n_ctx, SSD_INNER), F32)
    h_cf, h_cb = _ssd(xbc_c.reshape(bt, n_ctx, CONV_DIM), dtr_c, z_dummy, consts,
                      zeros_state, zeros_state, rows=n_ctx, need_y=False)
    ssd_x, _, _ = _ssd(xbc_x.reshape(bt, seq, CONV_DIM), dtr_x, z_x.reshape(bt, seq, SSD_INNER),
                       consts, h_cf, h_cb, rows=ROWS_SSD, need_y=True)

    attn_x = _attention(attn_sinks[layer].astype(F32),
                        q_x.reshape(bt, seq, ATTN_WIDTH), k_x.reshape(bt, seq, KV_WIDTH),
                        v_x.reshape(bt, seq, KV_WIDTH), k_c.reshape(bt, n_ctx, KV_WIDTH),
                        v_c.reshape(bt, n_ctx, KV_WIDTH))

    wo = w_out[layer]
    wa = wo[:SSD_INNER].astype(BF16)
    wb = wo[SSD_INNER:].reshape(ATTN_KV_HEADS, Q_PER_KV, HEAD_DIM, d).transpose(1, 0, 2, 3)
    wb = wb.reshape(ATTN_WIDTH, d).astype(BF16)
    wr = jnp.zeros((ROUTE_ROWS, d), F32).at[:N_GROUPS].set(w_group[layer].T)
    wr = wr.at[N_GROUPS:N_GROUPS + N_EXPERTS].set(w_expert[layer].T).astype(BF16)
    br = jnp.zeros((ROUTE_ROWS,), F32).at[:N_GROUPS].set(b_group[layer])
    br = br.at[N_GROUPS:N_GROUPS + N_EXPERTS].set(b_expert[layer])
    br = jnp.broadcast_to(br[:, None], (ROUTE_ROWS, ROWS_IN))
    tri = jnp.asarray(np.triu(np.ones((ROWS_IN, ROWS_IN), np.float32), 1), BF16)
    x1, h2, ids, wcol, cnt = _outproj(
        ssd_x.reshape(t, SSD_INNER), attn_x.reshape(t, ATTN_WIDTH), x.reshape(t, d), mod3, norm2[layer][None, :],
        wa, wb, wr, br, tri, rows=ROWS_IN, seq_blocks=seq_blocks)

    counts = cnt[N_GROUPS:N_GROUPS + N_EXPERTS, 0].astype(jnp.int32)
    padded = (counts + MOE_BLK - 1) // MOE_BLK * MOE_BLK
    ends = jnp.cumsum(padded)
    pstart = ends - padded
    nblocks = (2 * t) // MOE_BLK + N_EXPERTS
    blk_start = jnp.arange(nblocks, dtype=jnp.int32) * MOE_BLK
    blk_e = jnp.minimum(jnp.sum((ends[None, :] <= blk_start[:, None]).astype(jnp.int32), axis=1), N_EXPERTS - 1)
    nused = (ends[-1] // MOE_BLK).astype(jnp.int32).reshape(1)
    code = ids[0:2].reshape(-1)

    y_buf = _experts(blk_e, nused, pstart, counts, code, h2, w_gate[layer], w_up[layer], w_down[layer],
                     nblocks=nblocks, n_tok=t)
    out = _combine(pstart, code, y_buf, x1, wcol, mod3, norm_final[None, :], rows=ROWS_CMB,
                   seq_blocks=seq // ROWS_CMB)
    return dict(out=out.reshape(bt, seq, d), mod=mod, z=z_x, xbc=xbc_x, dtr=dtr_x, q=q_x, k=k_x, v=v_x,
                k_c=k_c, v_c=v_c, h_cf=h_cf, h_cb=h_cb, ssd=ssd_x, attn=attn_x, x1=x1, h2=h2, ids=ids,
                wcol=wcol, cnt=cnt, y_buf=y_buf, pstart=pstart, q_perm=q_perm)
```

```python
import functools
import math

import numpy as np
import jax
import jax.numpy as jnp
from jax import lax
from jax.experimental import pallas as pl
from jax.experimental.pallas import tpu as pltpu

F32 = jnp.float32
BF16 = jnp.bfloat16

D_MODEL = 1024
GRID_W = 64
EPS = 1e-6
SSD_INNER = 512
SSD_HEAD_DIM = 64
SSD_HEADS = 8
SSD_GROUPS = 2
SSD_STATE = 128
CONV_WIDTH = 5
CONV_DIM = SSD_INNER + 2 * SSD_GROUPS * SSD_STATE
CHUNK = 128
ATTN_WIDTH = 512
HEAD_DIM = 64
ATTN_Q_HEADS = 8
ATTN_KV_HEADS = 2
Q_PER_KV = 4
KV_WIDTH = 128
WINDOW = 128
ROPE_BASE = 10000.0
N_GROUPS = 4
EXPERTS_PER_GROUP = 8
N_EXPERTS = 32
EXPERT_DIM = 512

LANES = 128
SUBLANES = 8
NEG = -1e30

ROWS_IN = 512
ROWS_SSD = 512
ROWS_ATT = 256
ROUTE_ROWS = 48
MOE_BLK = 128
RANK_BITS = 16
D_TILES = D_MODEL // LANES
ROWS_CMB = 128


def _silu(v):
    return v * (1.0 / (1.0 + jnp.exp(-v)))


def _softplus(v):
    return jnp.maximum(v, 0.0) + jnp.log(1.0 + jnp.exp(-jnp.abs(v)))


def _ada_kernel(c_ref, w_ref, b_ref, o_ref):
    a = _silu(c_ref[...])
    o_ref[...] = jnp.dot(a.astype(BF16), w_ref[...].astype(BF16),
                         preferred_element_type=F32) + b_ref[...]


def _ada(cc, w_ada, b_ada):
    d, n = w_ada.shape
    tn = 1536
    return pl.pallas_call(
        _ada_kernel,
        grid=(n // tn,),
        in_specs=[pl.BlockSpec((SUBLANES, d), lambda j: (0, 0)),
                  pl.BlockSpec((d, tn), lambda j: (0, j)),
                  pl.BlockSpec((1, tn), lambda j: (0, j))],
        out_specs=pl.BlockSpec((SUBLANES, tn), lambda j: (0, j)),
        out_shape=jax.ShapeDtypeStruct((SUBLANES, n), F32),
        name="ada",
    )(cc, w_ada, b_ada)


def _rope(t, cos, sin_signed, first_half):
    rot = jnp.where(first_half, pltpu.roll(t, LANES - HEAD_DIM // 2, 1), pltpu.roll(t, HEAD_DIM // 2, 1))
    return t * cos + rot * sin_signed


def _inproj_kernel(*refs, segs, rope):
    x_ref, sh_ref, sc_ref, g_ref, w_ref, wdt_ref = refs[:6]
    pos = 6
    if rope:
        cos_ref, sin_ref = refs[6:8]
        pos = 8
    outs = refs[pos:]
    x = x_ref[...]
    ms = jnp.mean(x * x, axis=-1, keepdims=True)
    h = (x * lax.rsqrt(ms + EPS)) * (g_ref[...] * (1.0 + sc_ref[0])) + sh_ref[0]
    hb = h.astype(BF16)
    if rope:
        cos = cos_ref[...]
        lane = lax.broadcasted_iota(jnp.int32, cos.shape, 1)
        first_half = (lane & (HEAD_DIM - 1)) < HEAD_DIM // 2
        sin_s = jnp.where(first_half, -sin_ref[...], sin_ref[...])
    for (kind, c0, width), o_ref in zip(segs, outs):
        if kind == "dtT":
            dtt = lax.dot_general(wdt_ref[...], hb, (((1,), (1,)), ((), ())), preferred_element_type=F32)
            for j in range(o_ref.shape[0]):
                o_ref[j] = dtt[:, j * CHUNK:(j + 1) * CHUNK]
            continue
        for j0 in range(0, width, 512):
            wj = min(512, width - j0)
            acc = jnp.dot(hb, w_ref[:, c0 + j0:c0 + j0 + wj], preferred_element_type=F32)
            if rope and kind in ("q", "k"):
                scale = HEAD_DIM ** -0.5 if kind == "q" else 1.0
                for l0 in range(0, wj, LANES):
                    t = _rope(acc[:, l0:l0 + LANES], cos, sin_s, first_half)
                    o_ref[:, j0 + l0:j0 + l0 + LANES] = (t * scale).astype(o_ref.dtype)
            else:
                o_ref[:, j0:j0 + wj] = acc.astype(o_ref.dtype)


_COL_Z, _COL_XBC, _COL_Q, _COL_K, _COL_V = 0, 512, 1536, 2048, 2176
_W_COLS = 2304


def _inproj(xf, mod3, norm1, w_cat, w_dtT, rope_tabs, *, rows, mod_row_fn, segs, seq_blocks):
    t, d = xf.shape
    rope = rope_tabs is not None
    in_specs = [
        pl.BlockSpec((rows, d), lambda i: (i, 0)),
        pl.BlockSpec((1, 1, d), lambda i: (mod_row_fn(i), 0, 0)),
        pl.BlockSpec((1, 1, d), lambda i: (mod_row_fn(i), 0, 1)),
        pl.BlockSpec((1, d), lambda i: (0, 0)),
        pl.BlockSpec(w_cat.shape, lambda i: (0, 0)),
        pl.BlockSpec(w_dtT.shape, lambda i: (0, 0)),
    ]
    args = [xf, mod3, mod3, norm1, w_cat, w_dtT]
    if rope:
        in_specs += [pl.BlockSpec((rows, LANES), lambda i: (i % seq_blocks, 0))] * 2
        args += list(rope_tabs)
    out_specs, out_shapes = [], []
    for kind, _, width in segs:
        if kind == "dtT":
            out_specs.append(pl.BlockSpec((rows // CHUNK, 2 * SSD_HEADS, CHUNK), lambda i: (i, 0, 0)))
            out_shapes.append(jax.ShapeDtypeStruct((t // CHUNK, 2 * SSD_HEADS, CHUNK), F32))
        else:
            dt = BF16 if kind in ("q", "k", "v") else F32
            out_specs.append(pl.BlockSpec((rows, width), lambda i: (i, 0)))
            out_shapes.append(jax.ShapeDtypeStruct((t, width), dt))
    return pl.pallas_call(
        functools.partial(_inproj_kernel, segs=tuple(segs), rope=rope),
        grid=(t // rows,),
        in_specs=in_specs,
        out_specs=out_specs,
        out_shape=out_shapes,
        name="inproj_rope" if rope else "inproj_ctx",
    )(*args)


def _lane_cumsum(v, lane, reverse):
    k = 1
    while k < LANES:
        if reverse:
            v = v + jnp.where(lane < LANES - k, pltpu.roll(v, LANES - k, 1), 0.0)
        else:
            v = v + jnp.where(lane >= k, pltpu.roll(v, k, 1), 0.0)
        k *= 2
    return v


def _ssd_kernel(*refs, need_y, nsteps, cps):
    (xbc_ref, prev_ref, next_ref, dtr_ref, z_ref, cw_ref, cb_ref, bias_ref, alog_ref,
     dskip_ref, gain_ref, exp_ref, h0f_ref, h0b_ref) = refs[:14]
    if need_y:
        y_ref, hf_out, hb_out = refs[14:17]
        rest = refs[17:]
    else:
        y_ref = None
        hf_out, hb_out = refs[14:16]
        rest = refs[16:]
    win, cwin, u_ref, hf, hb, hbs = rest

    sw = pl.program_id(1)
    s = pl.program_id(2)
    bi = jnp.where(sw == 0, nsteps - 1 - s, s)
    rows = xbc_ref.shape[1]
    h = SSD_HEADS

    @pl.when((sw == 0) & (s == 0))
    def _():
        hb[...] = h0b_ref[0]

    @pl.when((sw == 1) & (s == 0))
    def _():
        hf[...] = h0f_ref[0]

    win[0:SUBLANES, :] = jnp.where(bi > 0, prev_ref[0], 0.0)
    win[SUBLANES:SUBLANES + rows, :] = xbc_ref[0]
    win[SUBLANES + rows:, :] = jnp.where(bi < nsteps - 1, next_ref[0], 0.0)

    lane = lax.broadcasted_iota(jnp.int32, (2 * h, LANES), 1)
    ri = lax.broadcasted_iota(jnp.int32, (CHUNK, CHUNK), 0)
    ci = lax.broadcasted_iota(jnp.int32, (CHUNK, CHUNK), 1)
    lane128 = lax.broadcasted_iota(jnp.int32, (CHUNK, LANES), 1)
    lo_mask = lane128 < SSD_HEAD_DIM
    neg_a = -jnp.exp(alog_ref[...])

    def conv_silu(c, ncols):
        r0 = pl.multiple_of(c * CHUNK, CHUNK)
        cwin[...] = win[pl.ds(r0, CHUNK + 2 * SUBLANES), :]
        for j0 in range(0, ncols, LANES):
            acc = jnp.broadcast_to(cb_ref[:, j0:j0 + LANES], (CHUNK, LANES))
            for k in range(CONV_WIDTH):
                off = SUBLANES - CONV_WIDTH // 2 + k
                acc = acc + cwin[off:off + CHUNK, j0:j0 + LANES] * cw_ref[k:k + 1, j0:j0 + LANES]
            u_ref[:, j0:j0 + LANES] = _silu(acc)

    def dt_rows(c):
        dt = _softplus(dtr_ref[c] + bias_ref[...])
        a = dt * neg_a
        pre = _lane_cumsum(a, lane, False)
        suf = _lane_cumsum(a, lane, True)
        return dt, pre, suf

    def state_update(state_ref, w_rows, tot_col):
        dec = jnp.exp(jnp.sum(tot_col * exp_ref[...], axis=0, keepdims=True))
        for g in range(SSD_GROUPS):
            bm = u_ref[:, SSD_INNER + g * SSD_STATE:SSD_INNER + (g + 1) * SSD_STATE]
            bt = bm.T
            for pr in range(2):
                h0 = g * 4 + pr * 2
                c0 = h0 * SSD_HEAD_DIM
                xp = u_ref[:, c0:c0 + LANES]
                rhs = jnp.concatenate([jnp.where(lo_mask, xp, 0.0), jnp.where(lo_mask, 0.0, xp)],
                                      axis=0).astype(BF16)
                lhs = jnp.concatenate([bt * w_rows[h0:h0 + 1, :], bt * w_rows[h0 + 1:h0 + 2, :]],
                                      axis=1).astype(BF16)
                sres = jnp.dot(lhs, rhs, preferred_element_type=F32)
                state_ref[:, c0:c0 + LANES] = state_ref[:, c0:c0 + LANES] * dec[:, c0:c0 + LANES] + sres

    def col_forms(rows16):
        padded = jnp.concatenate([rows16, jnp.zeros((CHUNK - 2 * h, LANES), F32)], axis=0)
        return padded.T

    def backward_chunk(k, carry):
        c = cps - 1 - k
        gc = bi * cps + c
        conv_silu(c, SSD_INNER + SSD_GROUPS * SSD_STATE)
        dt, _, suf = dt_rows(c)
        hbs[gc] = hb[...].astype(BF16)
        s0 = suf[:, 0:1]
        w_rows = jnp.exp(s0 - suf) * dt
        state_update(hb, w_rows[h:2 * h, :], s0[h:2 * h, :])
        return carry

    def forward_chunk(c, carry):
        gc = bi * cps + c
        r0 = pl.multiple_of(c * CHUNK, CHUNK)
        conv_silu(c, CONV_DIM)
        dt, pre, suf = dt_rows(c)
        rowsf = jnp.concatenate([pre[0:h, :], suf[h:2 * h, :]], axis=0)
        colsf = col_forms(rowsf)
        ecol = jnp.exp(colsf)
        for g in range(SSD_GROUPS):
            bm = u_ref[:, SSD_INNER + g * SSD_STATE:SSD_INNER + (g + 1) * SSD_STATE]
            cm = u_ref[:, SSD_INNER + (SSD_GROUPS + g) * SSD_STATE:SSD_INNER + (SSD_GROUPS + g + 1) * SSD_STATE]
            cmb = cm.astype(BF16)
            cbm = lax.dot_general(cmb, bm.astype(BF16), (((1,), (1,)), ((), ())),
                                  preferred_element_type=F32)
            for pr in range(2):
                h0 = g * 4 + pr * 2
                c0 = h0 * SSD_HEAD_DIM
                lhs_parts = []
                for hh in (h0, h0 + 1):
                    segf = colsf[:, hh:hh + 1] - rowsf[hh:hh + 1, :]
                    gf = jnp.exp(jnp.where(ci <= ri, segf, NEG)) * dt[hh:hh + 1, :]
                    segb = colsf[:, h + hh:h + hh + 1] - rowsf[h + hh:h + hh + 1, :]
                    gb = jnp.exp(jnp.where(ci >= ri, segb, NEG)) * dt[h + hh:h + hh + 1, :]
                    lhs_parts.append((cbm * (gf + gb)).astype(BF16))
                for hh in (h0, h0 + 1):
                    lhs_parts.append((cm * ecol[:, hh:hh + 1]).astype(BF16))
                for hh in (h0, h0 + 1):
                    lhs_parts.append((cm * ecol[:, h + hh:h + hh + 1]).astype(BF16))
                lhs = jnp.concatenate(lhs_parts, axis=1)
                xp = u_ref[:, c0:c0 + LANES]
                sf = hf[:, c0:c0 + LANES]
                sb = hbs[gc, :, c0:c0 + LANES].astype(F32)
                rhs = jnp.concatenate(
                    [jnp.where(lo_mask, xp, 0.0), jnp.where(lo_mask, 0.0, xp),
                     jnp.where(lo_mask, sf, 0.0), jnp.where(lo_mask, 0.0, sf),
                     jnp.where(lo_mask, sb, 0.0), jnp.where(lo_mask, 0.0, sb)], axis=0).astype(BF16)
                ypair = jnp.dot(lhs, rhs, preferred_element_type=F32)
                ypair = ypair + dskip_ref[:, c0:c0 + LANES] * xp
                zz = z_ref[0, pl.ds(r0, CHUNK), c0:c0 + LANES]
                u_ref[:, CONV_DIM + c0:CONV_DIM + c0 + LANES] = ypair * _silu(zz)
        last = pre[:, LANES - 1:LANES]
        w_rows = jnp.exp(last - pre) * dt
        state_update(hf, w_rows[0:h, :], last[0:h, :])
        if need_y:
            yv = u_ref[:, CONV_DIM:CONV_DIM + SSD_INNER]
            ms = jnp.mean(yv * yv, axis=-1, keepdims=True)
            y_ref[0, pl.ds(r0, CHUNK), :] = (yv * lax.rsqrt(ms + EPS) * gain_ref[...]).astype(y_ref.dtype)
        return carry

    @pl.when(sw == 0)
    def _():
        lax.fori_loop(0, cps, backward_chunk, 0)

    @pl.when(sw == 1)
    def _():
        lax.fori_loop(0, cps, forward_chunk, 0)

    @pl.when((sw == 0) & (s == nsteps - 1))
    def _():
        hb_out[0] = hb[...]

    @pl.when((sw == 1) & (s == nsteps - 1))
    def _():
        hf_out[0] = hf[...]


def _ssd(xbc, dtr, z, consts, h0f, h0b, *, rows, need_y):
    bt, seq, _ = xbc.shape
    nsteps = seq // rows
    cps = rows // CHUNK
    hb8 = rows // SUBLANES
    nb8 = seq // SUBLANES
    cw, cb, bias, alog, dskip, gain, expm = consts

    def blk(b, sw, s):
        return jnp.where(sw == 0, nsteps - 1 - s, s)

    def full(a):
        return pl.BlockSpec(a.shape, lambda b, sw, s: (0,) * a.ndim)

    in_specs = [
        pl.BlockSpec((1, rows, CONV_DIM), lambda b, sw, s: (b, blk(b, sw, s), 0)),
        pl.BlockSpec((1, SUBLANES, CONV_DIM), lambda b, sw, s: (b, jnp.maximum(blk(b, sw, s) * hb8 - 1, 0), 0)),
        pl.BlockSpec((1, SUBLANES, CONV_DIM),
                     lambda b, sw, s: (b, jnp.minimum((blk(b, sw, s) + 1) * hb8, nb8 - 1), 0)),
        pl.BlockSpec((cps, 2 * SSD_HEADS, CHUNK), lambda b, sw, s: (b * nsteps + blk(b, sw, s), 0, 0)),
        pl.BlockSpec((1, rows, SSD_INNER), lambda b, sw, s: (b, jnp.where(sw == 0, 0, s), 0)),
        full(cw), full(cb), full(bias), full(alog), full(dskip), full(gain), full(expm),
        pl.BlockSpec((1, SSD_STATE, SSD_INNER), lambda b, sw, s: (b, 0, 0)),
        pl.BlockSpec((1, SSD_STATE, SSD_INNER), lambda b, sw, s: (b, 0, 0)),
    ]
    st_spec = pl.BlockSpec((1, SSD_STATE, SSD_INNER), lambda b, sw, s: (b, 0, 0))
    st_shape = jax.ShapeDtypeStruct((bt, SSD_STATE, SSD_INNER), F32)
    out_specs, out_shapes = [st_spec, st_spec], [st_shape, st_shape]
    if need_y:
        out_specs = [pl.BlockSpec((1, rows, SSD_INNER), lambda b, sw, s: (b, jnp.where(sw == 0, 0, s), 0))] + out_specs
        out_shapes = [jax.ShapeDtypeStruct((bt, seq, SSD_INNER), BF16)] + out_shapes
    scratch = [
        pltpu.VMEM((rows + 2 * SUBLANES, CONV_DIM), F32),
        pltpu.VMEM((CHUNK + 2 * SUBLANES, CONV_DIM), F32),
        pltpu.VMEM((CHUNK, CONV_DIM + SSD_INNER), F32),
        pltpu.VMEM((SSD_STATE, SSD_INNER), F32),
        pltpu.VMEM((SSD_STATE, SSD_INNER), F32),
        pltpu.VMEM((seq // CHUNK, SSD_STATE, SSD_INNER), BF16),
    ]
    return pl.pallas_call(
        functools.partial(_ssd_kernel, need_y=need_y, nsteps=nsteps, cps=cps),
        grid=(bt, 2, nsteps),
        in_specs=in_specs,
        out_specs=out_specs,
        out_shape=out_shapes,
        scratch_shapes=scratch,
        compiler_params=pltpu.CompilerParams(dimension_semantics=("arbitrary", "arbitrary", "arbitrary")),
        name="ssd_y" if need_y else "ssd_ctx",
    )(xbc, xbc, xbc, dtr, z, cw, cb, bias, alog, dskip, gain, expm, h0f, h0b)


def _attn_kernel(sink_ref, q_ref, kp_ref, kc_ref, kn_ref, vp_ref, vc_ref, vn_ref, kx_ref, vx_ref, o_ref, *, nsteps):
    m = pl.program_id(1)
    lane = lax.broadcasted_iota(jnp.int32, (1, LANES), 1)
    lo = lane < HEAD_DIM
    rq = Q_PER_KV * WINDOW
    qi = lax.broadcasted_iota(jnp.int32, (rq, WINDOW), 0) & (WINDOW - 1)
    kj = lax.broadcasted_iota(jnp.int32, (rq, WINDOW), 1)
    rblk = lax.shift_right_logical(lax.broadcasted_iota(jnp.int32, (rq, 1), 0), int(math.log2(WINDOW)))
    zero = jnp.zeros((), BF16)

    kpieces = [kp_ref[0], kc_ref[0, 0:WINDOW], kc_ref[0, WINDOW:2 * WINDOW], kn_ref[0]]
    vpieces = [vp_ref[0], vc_ref[0, 0:WINDOW], vc_ref[0, WINDOW:2 * WINDOW], vn_ref[0]]
    kx = kx_ref[0]
    vx = vx_ref[0]
    nt = (((1,), (1,)), ((), ()))

    for sb in range(ROWS_ATT // WINDOW):
        qsb = q_ref[0, sb * WINDOW:(sb + 1) * WINDOW, :]
        prev_ok = jnp.logical_or(m > 0, sb > 0)
        next_ok = jnp.logical_or(m < nsteps - 1, sb < ROWS_ATT // WINDOW - 1)
        outs = [None] * Q_PER_KV
        for kv in range(ATTN_KV_HEADS):
            sel = lo if kv == 0 else jnp.logical_not(lo)
            qs = jnp.concatenate(
                [jnp.where(sel, qsb[:, j * LANES:(j + 1) * LANES], zero) for j in range(Q_PER_KV)], axis=0)
            s_prev = lax.dot_general(qs, kpieces[sb], nt, preferred_element_type=F32)
            s_cur = lax.dot_general(qs, kpieces[sb + 1], nt, preferred_element_type=F32)
            s_next = lax.dot_general(qs, kpieces[sb + 2], nt, preferred_element_type=F32)
            s_ctx = lax.dot_general(qs, kx, nt, preferred_element_type=F32)
            s_prev = jnp.where((kj >= qi) & prev_ok, s_prev, NEG)
            s_next = jnp.where((kj <= qi) & next_ok, s_next, NEG)
            sink = jnp.zeros((rq, 1), F32)
            for j in range(Q_PER_KV):
                sink = jnp.where(rblk == j, sink_ref[kv * Q_PER_KV + j], sink)
            mx = jnp.maximum(
                jnp.maximum(jnp.max(s_prev, axis=-1, keepdims=True), jnp.max(s_cur, axis=-1, keepdims=True)),
                jnp.maximum(jnp.max(s_next, axis=-1, keepdims=True), jnp.max(s_ctx, axis=-1, keepdims=True)))
            mx = jnp.maximum(mx, sink)
            p_prev = jnp.exp(s_prev - mx)
            p_cur = jnp.exp(s_cur - mx)
            p_next = jnp.exp(s_next - mx)
            p_ctx = jnp.exp(s_ctx - mx)
            den = (jnp.sum(p_prev, axis=-1, keepdims=True) + jnp.sum(p_cur, axis=-1, keepdims=True)
                   + jnp.sum(p_next, axis=-1, keepdims=True) + jnp.sum(p_ctx, axis=-1, keepdims=True)
                   + jnp.exp(sink - mx))
            acc = jnp.dot(p_prev.astype(BF16), jnp.where(sel, vpieces[sb], zero), preferred_element_type=F32)
            acc += jnp.dot(p_cur.astype(BF16), jnp.where(sel, vpieces[sb + 1], zero), preferred_element_type=F32)
            acc += jnp.dot(p_next.astype(BF16), jnp.where(sel, vpieces[sb + 2], zero), preferred_element_type=F32)
            acc += jnp.dot(p_ctx.astype(BF16), jnp.where(sel, vx, zero), preferred_element_type=F32)
            acc = acc * (1.0 / den)
            for j in range(Q_PER_KV):
                part = acc[j * WINDOW:(j + 1) * WINDOW, :]
                outs[j] = part if outs[j] is None else outs[j] + part
        for j in range(Q_PER_KV):
            o_ref[0, sb * WINDOW:(sb + 1) * WINDOW, j * LANES:(j + 1) * LANES] = outs[j].astype(o_ref.dtype)


def _attention(sinks, q, k, v, kx, vx):
    bt, seq, _ = q.shape
    nsteps = seq // ROWS_ATT
    per = ROWS_ATT // WINDOW
    nb = seq // WINDOW
    n_ctx = kx.shape[1]

    def prev(b, m, s):
        return (b, jnp.maximum(m * per - 1, 0), 0)

    def cur(b, m, s):
        return (b, m, 0)

    def nxt(b, m, s):
        return (b, jnp.minimum((m + 1) * per, nb - 1), 0)

    small = (1, WINDOW, KV_WIDTH)
    big = (1, ROWS_ATT, KV_WIDTH)
    grid_spec = pltpu.PrefetchScalarGridSpec(
        num_scalar_prefetch=1,
        grid=(bt, nsteps),
        in_specs=[
            pl.BlockSpec((1, ROWS_ATT, ATTN_WIDTH), cur),
            pl.BlockSpec(small, prev), pl.BlockSpec(big, cur), pl.BlockSpec(small, nxt),
            pl.BlockSpec(small, prev), pl.BlockSpec(big, cur), pl.BlockSpec(small, nxt),
            pl.BlockSpec((1, n_ctx, KV_WIDTH), lambda b, m, s: (b, 0, 0)),
            pl.BlockSpec((1, n_ctx, KV_WIDTH), lambda b, m, s: (b, 0, 0)),
        ],
        out_specs=pl.BlockSpec((1, ROWS_ATT, ATTN_WIDTH), cur),
    )
    return pl.pallas_call(
        functools.partial(_attn_kernel, nsteps=nsteps),
        grid_spec=grid_spec,
        out_shape=jax.ShapeDtypeStruct((bt, seq, ATTN_WIDTH), BF16),
        name="attn",
    )(sinks, q, k, k, k, v, v, v, kx, vx)


def _outproj_kernel(ssd_ref, att_ref, x_ref, g1_ref, sh_ref, sc_ref, n2_ref, wa_ref, wb_ref, wr_ref, br_ref,
                    tri_ref, x1_ref, h2_ref, ids_ref, wcol_ref, cnt_ref, carry, *, nsteps):
    i = pl.program_id(0)
    rows = x_ref.shape[0]

    @pl.when(i == 0)
    def _():
        carry[...] = jnp.zeros_like(carry)

    acc = jnp.dot(ssd_ref[...], wa_ref[...], preferred_element_type=F32)
    acc += jnp.dot(att_ref[...], wb_ref[...], preferred_element_type=F32)
    x1 = x_ref[...] + g1_ref[0] * acc
    x1_ref[...] = x1
    ms = jnp.mean(x1 * x1, axis=-1, keepdims=True)
    h2 = (x1 * lax.rsqrt(ms + EPS)) * (n2_ref[...] * (1.0 + sc_ref[0])) + sh_ref[0]
    for sidx in range(d_tiles := h2.shape[1] // LANES):
        h2_ref[pl.ds(sidx, rows, stride=d_tiles), :] = h2[:, sidx * LANES:(sidx + 1) * LANES]
    lt = lax.dot_general(wr_ref[...], h2.astype(BF16), (((1,), (1,)), ((), ())),
                         preferred_element_type=F32) + br_ref[...]
    row = lax.broadcasted_iota(jnp.int32, lt.shape, 0).astype(F32)
    big_i = float(ROUTE_ROWS)
    gl = jnp.where(row < N_GROUPS, lt, NEG)
    gmax = jnp.max(gl, axis=0, keepdims=True)
    gidx = jnp.min(jnp.where(gl == gmax, row, big_i), axis=0, keepdims=True)
    g_w = 1.0 / jnp.sum(jnp.exp(gl - gmax), axis=0, keepdims=True)
    lo = N_GROUPS + EXPERTS_PER_GROUP * gidx
    el = jnp.where((row >= lo) & (row < lo + EXPERTS_PER_GROUP), lt, NEG)
    m1 = jnp.max(el, axis=0, keepdims=True)
    i1 = jnp.min(jnp.where(el == m1, row, big_i), axis=0, keepdims=True)
    el2 = jnp.where(row == i1, NEG, el)
    m2 = jnp.max(el2, axis=0, keepdims=True)
    i2 = jnp.min(jnp.where(el2 == m2, row, big_i), axis=0, keepdims=True)
    r = jnp.exp(m2 - m1)
    w1 = g_w / (1.0 + r)
    w2 = g_w * r / (1.0 + r)
    oh1 = row == i1
    oh2 = row == i2
    cnt = jnp.where(oh1 | oh2, 1.0, 0.0)
    prefix = jnp.dot(cnt.astype(BF16), tri_ref[...], preferred_element_type=F32)
    base = carry[...] + prefix
    rank1 = jnp.sum(jnp.where(oh1, base, 0.0), axis=0, keepdims=True)
    rank2 = jnp.sum(jnp.where(oh2, base, 0.0), axis=0, keepdims=True)
    carry[...] = carry[...] + jnp.sum(cnt, axis=1, keepdims=True)
    r8 = lax.broadcasted_iota(jnp.int32, (SUBLANES, rows), 0)
    code1 = (i1 - N_GROUPS) * float(1 << RANK_BITS) + rank1
    code2 = (i2 - N_GROUPS) * float(1 << RANK_BITS) + rank2
    ids_ref[...] = jnp.where(r8 == 0, code1, jnp.where(r8 == 1, code2, 0.0)).astype(jnp.int32)
    r128 = lax.broadcasted_iota(jnp.int32, (LANES, rows), 0)
    wfull = jnp.where(r128 == 0, w1, jnp.where(r128 == 1, w2, 0.0))
    for j in range(rows // LANES):
        wcol_ref[j * LANES:(j + 1) * LANES, :] = wfull[:, j * LANES:(j + 1) * LANES].T

    @pl.when(i == nsteps - 1)
    def _():
        cnt_ref[...] = carry[:, 0:LANES]


def _outproj(ssd, att, xf, mod3, norm2, wa, wb, wr, br, tri, *, rows, seq_blocks):
    t, d = xf.shape
    nsteps = t // rows

    def mod(col):
        return pl.BlockSpec((1, 1, d), lambda i: (i // seq_blocks, 0, col))

    def full(a):
        return pl.BlockSpec(a.shape, lambda i: (0,) * a.ndim)

    return pl.pallas_call(
        functools.partial(_outproj_kernel, nsteps=nsteps),
        grid=(nsteps,),
        in_specs=[pl.BlockSpec((rows, SSD_INNER), lambda i: (i, 0)),
                  pl.BlockSpec((rows, ATTN_WIDTH), lambda i: (i, 0)),
                  pl.BlockSpec((rows, d), lambda i: (i, 0)),
                  mod(2), mod(3), mod(4), full(norm2), full(wa), full(wb), full(wr), full(br), full(tri)],
        out_specs=[pl.BlockSpec((rows, d), lambda i: (i, 0)),
                   pl.BlockSpec((rows * (d // LANES), LANES), lambda i: (i, 0)),
                   pl.BlockSpec((SUBLANES, rows), lambda i: (0, i)),
                   pl.BlockSpec((rows, LANES), lambda i: (i, 0)),
                   pl.BlockSpec((ROUTE_ROWS, LANES), lambda i: (0, 0))],
        out_shape=[jax.ShapeDtypeStruct((t, d), F32),
                   jax.ShapeDtypeStruct((t * (d // LANES), LANES), F32),
                   jax.ShapeDtypeStruct((SUBLANES, t), jnp.int32),
                   jax.ShapeDtypeStruct((t, LANES), F32),
                   jax.ShapeDtypeStruct((ROUTE_ROWS, LANES), F32)],
        scratch_shapes=[pltpu.VMEM((ROUTE_ROWS, rows), F32)],
        compiler_params=pltpu.CompilerParams(dimension_semantics=("arbitrary",)),
        name="outproj_router",
    )(ssd, att, xf, mod3, mod3, mod3, norm2, wa, wb, wr, br, tri)


def _expert_kernel(blk_e_ref, nused_ref, pstart_ref, counts_ref, code_ref, h2_hbm, wg_ref, wu_ref, wd_ref, y_ref,
                   xbuf, wgb, wub, wdb, tok_ref, sem, *, n_tok):
    b = pl.program_id(0)
    nused = nused_ref[0]
    slot = b % 2

    def row_copy(tok, r, sl):
        return pltpu.make_async_copy(h2_hbm.at[pl.ds(tok * D_TILES, D_TILES)],
                                     xbuf.at[sl, pl.ds(r * D_TILES, D_TILES)], sem.at[sl])

    def start_gather(blk, sl):
        for r in range(MOE_BLK):
            row_copy(tok_ref[blk * MOE_BLK + r], r, sl).start(priority=r % 2)

    @pl.when(b == 0)
    def _():
        def pad_expert(e, carry):
            lo = pstart_ref[e] + counts_ref[e]
            hi = pstart_ref[e] + (counts_ref[e] + MOE_BLK - 1) // MOE_BLK * MOE_BLK

            def pad_slot(s, c):
                tok_ref[s] = 0
                return c
            return lax.fori_loop(lo, hi, pad_slot, carry)
        lax.fori_loop(0, N_EXPERTS, pad_expert, 0)

        def claim(i, carry):
            code = code_ref[i]
            dst = pstart_ref[lax.shift_right_logical(code, RANK_BITS)] + (code & ((1 << RANK_BITS) - 1))
            tok_ref[dst] = jnp.where(i >= n_tok, i - n_tok, i)
            return carry
        lax.fori_loop(0, 2 * n_tok, claim, 0, unroll=8)

    @pl.when((b == 0) & (nused > 0))
    def _():
        start_gather(0, 0)

    @pl.when(b + 1 < nused)
    def _():
        start_gather(b + 1, 1 - slot)

    e = blk_e_ref[b]
    e_prev = blk_e_ref[jnp.maximum(b - 1, 0)]

    @pl.when((b == 0) | (e != e_prev))
    def _():
        wgb[...] = wg_ref[0].astype(BF16)
        wub[...] = wu_ref[0].astype(BF16)
        wdb[...] = wd_ref[0].astype(BF16)

    @pl.when(b < nused)
    def _():
        for r in range(MOE_BLK):
            row_copy(0, r, slot).wait()
        xb = jnp.concatenate([xbuf[slot, pl.ds(s, MOE_BLK, stride=D_TILES), :] for s in range(D_TILES)],
                             axis=1).astype(BF16)
        gate = jnp.dot(xb, wgb[...], preferred_element_type=F32)
        up = jnp.dot(xb, wub[...], preferred_element_type=F32)
        hmid = (_silu(gate) * up).astype(BF16)
        y = jnp.dot(hmid, wdb[...], preferred_element_type=F32)
        for s in range(D_TILES):
            y_ref[pl.ds(s, MOE_BLK, stride=D_TILES), :] = y[:, s * LANES:(s + 1) * LANES]

    @pl.when(b >= nused)
    def _():
        y_ref[...] = jnp.zeros_like(y_ref)


def _experts(blk_e, nused, pstart, counts, code, h2t, w_gate, w_up, w_down, *, nblocks, n_tok):
    d = w_gate.shape[1]

    def wspec(shape):
        return pl.BlockSpec(shape, lambda b, be, nu, ps, ct, cd: (be[b], 0, 0))

    grid_spec = pltpu.PrefetchScalarGridSpec(
        num_scalar_prefetch=5,
        grid=(nblocks,),
        in_specs=[pl.BlockSpec(memory_space=pl.ANY),
                  wspec((1, d, EXPERT_DIM)), wspec((1, d, EXPERT_DIM)), wspec((1, EXPERT_DIM, d))],
        out_specs=pl.BlockSpec((MOE_BLK * D_TILES, LANES), lambda b, be, nu, ps, ct, cd: (b, 0)),
        scratch_shapes=[
            pltpu.VMEM((2, MOE_BLK * D_TILES, LANES), F32),
            pltpu.VMEM((d, EXPERT_DIM), BF16),
            pltpu.VMEM((d, EXPERT_DIM), BF16),
            pltpu.VMEM((EXPERT_DIM, d), BF16),
            pltpu.SMEM((nblocks * MOE_BLK,), jnp.int32),
            pltpu.SemaphoreType.DMA((2,)),
        ],
    )
    return pl.pallas_call(
        functools.partial(_expert_kernel, n_tok=n_tok),
        grid_spec=grid_spec,
        out_shape=jax.ShapeDtypeStruct((nblocks * MOE_BLK * D_TILES, LANES), F32),
        compiler_params=pltpu.CompilerParams(dimension_semantics=("arbitrary",)),
        name="experts",
    )(blk_e, nused, pstart, counts, code, h2t, w_gate, w_up, w_down)


def _combine_kernel(pstart_ref, code_ref, y_hbm, x1_ref, wcol_ref, g2_ref, nf_ref, o_ref, ybuf, sem, *, nsteps, n_tok):
    i = pl.program_id(0)
    rows = x1_ref.shape[0]
    slot = i % 2

    def row_copy(src, r, k, sl):
        return pltpu.make_async_copy(y_hbm.at[pl.ds(src * D_TILES, D_TILES)],
                                     ybuf.at[sl, k, pl.ds(r * D_TILES, D_TILES)], sem.at[sl])

    def start_gather(step, sl):
        for r in range(rows):
            for k in range(2):
                code = code_ref[k * n_tok + step * rows + r]
                src = pstart_ref[lax.shift_right_logical(code, RANK_BITS)] + (code & ((1 << RANK_BITS) - 1))
                row_copy(src, r, k, sl).start(priority=k)

    @pl.when(i == 0)
    def _():
        start_gather(0, 0)

    @pl.when(i + 1 < nsteps)
    def _():
        start_gather(i + 1, 1 - slot)

    for r in range(rows):
        for k in range(2):
            row_copy(0, r, k, slot).wait()
    w0 = wcol_ref[:, 0:1]
    w1 = wcol_ref[:, 1:2]
    ssq = jnp.zeros((rows, 1), F32)
    for s in range(D_TILES):
        cols = slice(s * LANES, (s + 1) * LANES)
        moe = (ybuf[slot, 0, pl.ds(s, rows, stride=D_TILES), :] * w0
               + ybuf[slot, 1, pl.ds(s, rows, stride=D_TILES), :] * w1)
        x2 = x1_ref[:, cols] + g2_ref[0, :, cols] * moe
        ssq = ssq + jnp.sum(x2 * x2, axis=-1, keepdims=True)
        o_ref[:, cols] = x2
    inv = lax.rsqrt(ssq * (1.0 / (D_TILES * LANES)) + EPS)
    o_ref[...] = o_ref[...] * inv * nf_ref[...]


def _combine(pstart, code, y_buf, x1, wcol, mod3, norm_final, *, rows, seq_blocks):
    t, d = x1.shape
    nsteps = t // rows
    grid_spec = pltpu.PrefetchScalarGridSpec(
        num_scalar_prefetch=2,
        grid=(nsteps,),
        in_specs=[
            pl.BlockSpec(memory_space=pl.ANY),
            pl.BlockSpec((rows, d), lambda i, ps, cd: (i, 0)),
            pl.BlockSpec((rows, LANES), lambda i, ps, cd: (i, 0)),
            pl.BlockSpec((1, 1, d), lambda i, ps, cd: (i // seq_blocks, 0, 5)),
            pl.BlockSpec((1, d), lambda i, ps, cd: (0, 0)),
        ],
        out_specs=pl.BlockSpec((rows, d), lambda i, ps, cd: (i, 0)),
        scratch_shapes=[pltpu.VMEM((2, 2, rows * D_TILES, LANES), F32), pltpu.SemaphoreType.DMA((2,))],
    )
    return pl.pallas_call(
        functools.partial(_combine_kernel, nsteps=nsteps, n_tok=t),
        grid_spec=grid_spec,
        out_shape=jax.ShapeDtypeStruct((t, d), F32),
        compiler_params=pltpu.CompilerParams(dimension_semantics=("arbitrary",)),
        name="combine",
    )(pstart, code, y_buf, x1, wcol, mod3, norm_final)


def _rope_tables(seq):
    pos = np.arange(seq)
    n_freq = HEAD_DIM // 4
    inv = ROPE_BASE ** (-np.arange(n_freq, dtype=np.float32) / n_freq)
    ang = np.concatenate([(pos // GRID_W)[:, None] * inv, (pos % GRID_W)[:, None] * inv], axis=-1)
    ang = np.concatenate([ang, ang, ang, ang], axis=-1).astype(np.float32)
    return jnp.asarray(np.cos(ang), F32), jnp.asarray(np.sin(ang), F32)


def _head_expand_mat():
    m = np.zeros((SSD_HEADS, SSD_INNER), np.float32)
    for hh in range(SSD_HEADS):
        m[hh, hh * SSD_HEAD_DIM:(hh + 1) * SSD_HEAD_DIM] = 1.0
    return jnp.asarray(m)


def kernel(x, c, ctx, c_ctx, w_ada, b_ada, norm1, w_in, conv_w, conv_b, dt_bias, a_log, d_skip, ssd_norm,
           attn_sinks, w_out, norm2, w_group, b_group, w_expert, b_expert, w_gate, w_up, w_down, norm_final):
    return _pipeline(x, c, ctx, c_ctx, w_ada, b_ada, norm1, w_in, conv_w, conv_b, dt_bias, a_log, d_skip,
                     ssd_norm, attn_sinks, w_out, norm2, w_group, b_group, w_expert, b_expert, w_gate, w_up,
                     w_down, norm_final)["out"]


def _pipeline(x, c, ctx, c_ctx, w_ada, b_ada, norm1, w_in, conv_w, conv_b, dt_bias, a_log, d_skip, ssd_norm,
              attn_sinks, w_out, norm2, w_group, b_group, w_expert, b_expert, w_gate, w_up, w_down, norm_final):
    bt, seq, d = x.shape
    n_ctx = ctx.shape[1]
    t = bt * seq
    layer = 0
    assert w_ada.shape[0] == 1 and seq % ROWS_IN == 0 and n_ctx % CHUNK == 0

    cc = jnp.zeros((SUBLANES, d), F32).at[:bt].set(c).at[bt].set(c_ctx)
    mod = _ada(cc, w_ada[layer], b_ada[layer][None, :])
    mod3 = mod.reshape(SUBLANES, 1, 6 * d)

    w = w_in[layer]
    o_z, o_xbc, o_dt = 0, SSD_INNER, SSD_INNER + CONV_DIM
    o_q = o_dt + 2 * SSD_HEADS
    o_k, o_v = o_q + ATTN_WIDTH, o_q + ATTN_WIDTH + KV_WIDTH
    q_perm = np.concatenate([np.r_[j * HEAD_DIM:(j + 1) * HEAD_DIM, (j + 4) * HEAD_DIM:(j + 5) * HEAD_DIM]
                             for j in range(Q_PER_KV)])
    w_q = w[:, o_q:o_q + ATTN_WIDTH].reshape(d, ATTN_KV_HEADS, Q_PER_KV, HEAD_DIM).transpose(0, 2, 1, 3)
    w_q = w_q.reshape(d, ATTN_WIDTH)
    w_cat = jnp.concatenate([w[:, o_z:o_z + SSD_INNER], w[:, o_xbc:o_xbc + CONV_DIM], w_q,
                             w[:, o_k:o_k + KV_WIDTH], w[:, o_v:o_v + KV_WIDTH]], axis=1).astype(BF16)
    w_dtT = w[:, o_dt:o_dt + 2 * SSD_HEADS].T.astype(BF16)

    seq_blocks = seq // ROWS_IN
    cos, sin = _rope_tables(seq)
    segs_x = [("z", _COL_Z, SSD_INNER), ("xbc", _COL_XBC, CONV_DIM), ("dtT", 0, 0),
              ("q", _COL_Q, ATTN_WIDTH), ("k", _COL_K, KV_WIDTH), ("v", _COL_V, KV_WIDTH)]
    z_x, xbc_x, dtr_x, q_x, k_x, v_x = _inproj(
        x.reshape(t, d), mod3, norm1[layer][None, :], w_cat, w_dtT, (cos, sin),
        rows=ROWS_IN, mod_row_fn=lambda i: i // seq_blocks, segs=segs_x, seq_blocks=seq_blocks)
    segs_c = [("xbc", _COL_XBC, CONV_DIM), ("dtT", 0, 0), ("k", _COL_K, KV_WIDTH), ("v", _COL_V, KV_WIDTH)]
    xbc_c, dtr_c, k_c, v_c = _inproj(
        ctx.reshape(bt * n_ctx, d), mod3, norm1[layer][None, :], w_cat, w_dtT, None,
        rows=n_ctx, mod_row_fn=lambda i: bt, segs=segs_c, seq_blocks=1)

    cw = jnp.zeros((SUBLANES, CONV_DIM), F32).at[:CONV_WIDTH].set(conv_w[layer])
    cb = conv_b[layer][None, :]
    bias_rows = jnp.broadcast_to(dt_bias[layer].reshape(2 * SSD_HEADS, 1), (2 * SSD_HEADS, LANES))
    alog_rows = jnp.broadcast_to(a_log[layer].reshape(2 * SSD_HEADS, 1), (2 * SSD_HEADS, LANES))
    dskip_e = jnp.repeat(d_skip[layer], SSD_HEAD_DIM)[None, :]
    consts = (cw, cb, bias_rows, alog_rows, dskip_e, ssd_norm[layer][None, :], _head_expand_mat())

    zeros_state = jnp.zeros((bt, SSD_STATE, SSD_INNER), F32)
    z_dummy = jnp.zeros((bt, n_ctx, SSD_INNER), F32)
    h_cf, h_cb = _ssd(xbc_c.reshape(bt, n_ctx, CONV_DIM), dtr_c, z_dummy, consts,
                      zeros_state, zeros_state, rows=n_ctx, need_y=False)
    ssd_x, _, _ = _ssd(xbc_x.reshape(bt, seq, CONV_DIM), dtr_x, z_x.reshape(bt, seq, SSD_INNER),
                       consts, h_cf, h_cb, rows=ROWS_SSD, need_y=True)

    attn_x = _attention(attn_sinks[layer].astype(F32),
                        q_x.reshape(bt, seq, ATTN_WIDTH), k_x.reshape(bt, seq, KV_WIDTH),
                        v_x.reshape(bt, seq, KV_WIDTH), k_c.reshape(bt, n_ctx, KV_WIDTH),
                        v_c.reshape(bt, n_ctx, KV_WIDTH))

    wo = w_out[layer]
    wa = wo[:SSD_INNER].astype(BF16)
    wb = wo[SSD_INNER:].reshape(ATTN_KV_HEADS, Q_PER_KV, HEAD_DIM, d).transpose(1, 0, 2, 3)
    wb = wb.reshape(ATTN_WIDTH, d).astype(BF16)
    wr = jnp.zeros((ROUTE_ROWS, d), F32).at[:N_GROUPS].set(w_group[layer].T)
    wr = wr.at[N_GROUPS:N_GROUPS + N_EXPERTS].set(w_expert[layer].T).astype(BF16)
    br = jnp.zeros((ROUTE_ROWS,), F32).at[:N_GROUPS].set(b_group[layer])
    br = br.at[N_GROUPS:N_GROUPS + N_EXPERTS].set(b_expert[layer])
    br = jnp.broadcast_to(br[:, None], (ROUTE_ROWS, ROWS_IN))
    tri = jnp.asarray(np.triu(np.ones((ROWS_IN, ROWS_IN), np.float32), 1), BF16)
    x1, h2, ids, wcol, cnt = _outproj(
        ssd_x.reshape(t, SSD_INNER), attn_x.reshape(t, ATTN_WIDTH), x.reshape(t, d), mod3, norm2[layer][None, :],
        wa, wb, wr, br, tri, rows=ROWS_IN, seq_blocks=seq_blocks)

    counts = cnt[N_GROUPS:N_GROUPS + N_EXPERTS, 0].astype(jnp.int32)
    padded = (counts + MOE_BLK - 1) // MOE_BLK * MOE_BLK
    ends = jnp.cumsum(padded)
    pstart = ends - padded
    nblocks = (2 * t) // MOE_BLK + N_EXPERTS
    blk_start = jnp.arange(nblocks, dtype=jnp.int32) * MOE_BLK
    blk_e = jnp.minimum(jnp.sum((ends[None, :] <= blk_start[:, None]).astype(jnp.int32), axis=1), N_EXPERTS - 1)
    nused = (ends[-1] // MOE_BLK).astype(jnp.int32).reshape(1)
    code = ids[0:2].reshape(-1)

    y_buf = _experts(blk_e, nused, pstart, counts, code, h2, w_gate[layer], w_up[layer], w_down[layer],
                     nblocks=nblocks, n_tok=t)
    out = _combine(pstart, code, y_buf, x1, wcol, mod3, norm_final[None, :], rows=ROWS_CMB,
                   seq_blocks=seq // ROWS_CMB)
    return dict(out=out.reshape(bt, seq, d), mod=mod, z=z_x, xbc=xbc_x, dtr=dtr_x, q=q_x, k=k_x, v=v_x,
                k_c=k_c, v_c=v_c, h_cf=h_cf, h_cb=h_cb, ssd=ssd_x, attn=attn_x, x1=x1, h2=h2, ids=ids,
                wcol=wcol, cnt=cnt, y_buf=y_buf, pstart=pstart, q_perm=q_perm)
```

```python
import functools
import math

import numpy as np
import jax
import jax.numpy as jnp
from jax import lax
from jax.experimental import pallas as pl
from jax.experimental.pallas import tpu as pltpu

F32 = jnp.float32
BF16 = jnp.bfloat16

D_MODEL = 1024
GRID_W = 64
EPS = 1e-6
SSD_INNER = 512
SSD_HEAD_DIM = 64
SSD_HEADS = 8
SSD_GROUPS = 2
SSD_STATE = 128
CONV_WIDTH = 5
CONV_DIM = SSD_INNER + 2 * SSD_GROUPS * SSD_STATE
CHUNK = 128
ATTN_WIDTH = 512
HEAD_DIM = 64
ATTN_Q_HEADS = 8
ATTN_KV_HEADS = 2
Q_PER_KV = 4
KV_WIDTH = 128
WINDOW = 128
ROPE_BASE = 10000.0
N_GROUPS = 4
EXPERTS_PER_GROUP = 8
N_EXPERTS = 32
EXPERT_DIM = 512

LANES = 128
SUBLANES = 8
NEG = -1e30

ROWS_IN = 512
ROWS_SSD = 512
ROWS_ATT = 256
ROUTE_ROWS = 48
MOE_BLK = 128
RANK_BITS = 16
D_TILES = D_MODEL // LANES
ROWS_CMB = 128


def _silu(v):
    return v * (1.0 / (1.0 + jnp.exp(-v)))


def _softplus(v):
    return jnp.maximum(v, 0.0) + jnp.log(1.0 + jnp.exp(-jnp.abs(v)))


def _ada_kernel(c_ref, w_ref, b_ref, o_ref):
    a = _silu(c_ref[...])
    o_ref[...] = jnp.dot(a.astype(BF16), w_ref[...].astype(BF16),
                         preferred_element_type=F32) + b_ref[...]


def _ada(cc, w_ada, b_ada):
    d, n = w_ada.shape
    tn = 1536
    return pl.pallas_call(
        _ada_kernel,
        grid=(n // tn,),
        in_specs=[pl.BlockSpec((SUBLANES, d), lambda j: (0, 0)),
                  pl.BlockSpec((d, tn), lambda j: (0, j)),
                  pl.BlockSpec((1, tn), lambda j: (0, j))],
        out_specs=pl.BlockSpec((SUBLANES, tn), lambda j: (0, j)),
        out_shape=jax.ShapeDtypeStruct((SUBLANES, n), F32),
        name="ada",
    )(cc, w_ada, b_ada)


def _rope(t, cos, sin_signed, first_half):
    rot = jnp.where(first_half, pltpu.roll(t, LANES - HEAD_DIM // 2, 1), pltpu.roll(t, HEAD_DIM // 2, 1))
    return t * cos + rot * sin_signed


def _inproj_kernel(*refs, segs, rope):
    x_ref, sh_ref, sc_ref, g_ref, w_ref, wdt_ref = refs[:6]
    pos = 6
    if rope:
        cos_ref, sin_ref = refs[6:8]
        pos = 8
    outs = refs[pos:]
    x = x_ref[...]
    ms = jnp.mean(x * x, axis=-1, keepdims=True)
    h = (x * lax.rsqrt(ms + EPS)) * (g_ref[...] * (1.0 + sc_ref[0])) + sh_ref[0]
    hb = h.astype(BF16)
    if rope:
        cos = cos_ref[...]
        lane = lax.broadcasted_iota(jnp.int32, cos.shape, 1)
        first_half = (lane & (HEAD_DIM - 1)) < HEAD_DIM // 2
        sin_s = jnp.where(first_half, -sin_ref[...], sin_ref[...])
    for (kind, c0, width), o_ref in zip(segs, outs):
        if kind == "dtT":
            dtt = lax.dot_general(wdt_ref[...], hb, (((1,), (1,)), ((), ())), preferred_element_type=F32)
            for j in range(o_ref.shape[0]):
                o_ref[j] = dtt[:, j * CHUNK:(j + 1) * CHUNK]
            continue
        for j0 in range(0, width, 512):
            wj = min(512, width - j0)
            acc = jnp.dot(hb, w_ref[:, c0 + j0:c0 + j0 + wj], preferred_element_type=F32)
            if rope and kind in ("q", "k"):
                scale = HEAD_DIM ** -0.5 if kind == "q" else 1.0
                for l0 in range(0, wj, LANES):
                    t = _rope(acc[:, l0:l0 + LANES], cos, sin_s, first_half)
                    o_ref[:, j0 + l0:j0 + l0 + LANES] = (t * scale).astype(o_ref.dtype)
            else:
                o_ref[:, j0:j0 + wj] = acc.astype(o_ref.dtype)


_COL_Z, _COL_XBC, _COL_Q, _COL_K, _COL_V = 0, 512, 1536, 2048, 2176
_W_COLS = 2304


def _inproj(xf, mod3, norm1, w_cat, w_dtT, rope_tabs, *, rows, mod_row_fn, segs, seq_blocks):
    t, d = xf.shape
    rope = rope_tabs is not None
    in_specs = [
        pl.BlockSpec((rows, d), lambda i: (i, 0)),
        pl.BlockSpec((1, 1, d), lambda i: (mod_row_fn(i), 0, 0)),
        pl.BlockSpec((1, 1, d), lambda i: (mod_row_fn(i), 0, 1)),
        pl.BlockSpec((1, d), lambda i: (0, 0)),
        pl.BlockSpec(w_cat.shape, lambda i: (0, 0)),
        pl.BlockSpec(w_dtT.shape, lambda i: (0, 0)),
    ]
    args = [xf, mod3, mod3, norm1, w_cat, w_dtT]
    if rope:
        in_specs += [pl.BlockSpec((rows, LANES), lambda i: (i % seq_blocks, 0))] * 2
        args += list(rope_tabs)
    out_specs, out_shapes = [], []
    for kind, _, width in segs:
        if kind == "dtT":
            out_specs.append(pl.BlockSpec((rows // CHUNK, 2 * SSD_HEADS, CHUNK), lambda i: (i, 0, 0)))
            out_shapes.append(jax.ShapeDtypeStruct((t // CHUNK, 2 * SSD_HEADS, CHUNK), F32))
        else:
            dt = BF16 if kind in ("q", "k", "v") else F32
            out_specs.append(pl.BlockSpec((rows, width), lambda i: (i, 0)))
            out_shapes.append(jax.ShapeDtypeStruct((t, width), dt))
    return pl.pallas_call(
        functools.partial(_inproj_kernel, segs=tuple(segs), rope=rope),
        grid=(t // rows,),
        in_specs=in_specs,
        out_specs=out_specs,
        out_shape=out_shapes,
        name="inproj_rope" if rope else "inproj_ctx",
    )(*args)


def _lane_cumsum(v, lane, reverse):
    k = 1
    while k < LANES:
        if reverse:
            v = v + jnp.where(lane < LANES - k, pltpu.roll(v, LANES - k, 1), 0.0)
        else:
            v = v + jnp.where(lane >= k, pltpu.roll(v, k, 1), 0.0)
        k *= 2
    return v


def _ssd_kernel(*refs, need_y, nsteps, cps):
    (xbc_ref, prev_ref, next_ref, dtr_ref, z_ref, cw_ref, cb_ref, bias_ref, alog_ref,
     dskip_ref, gain_ref, exp_ref, h0f_ref, h0b_ref) = refs[:14]
    if need_y:
        y_ref, hf_out, hb_out = refs[14:17]
        rest = refs[17:]
    else:
        y_ref = None
        hf_out, hb_out = refs[14:16]
        rest = refs[16:]
    win, cwin, u_ref, hf, hb, hbs = rest

    sw = pl.program_id(1)
    s = pl.program_id(2)
    bi = jnp.where(sw == 0, nsteps - 1 - s, s)
    rows = xbc_ref.shape[1]
    h = SSD_HEADS

    @pl.when((sw == 0) & (s == 0))
    def _():
        hb[...] = h0b_ref[0]

    @pl.when((sw == 1) & (s == 0))
    def _():
        hf[...] = h0f_ref[0]

    win[0:SUBLANES, :] = jnp.where(bi > 0, prev_ref[0], 0.0)
    win[SUBLANES:SUBLANES + rows, :] = xbc_ref[0]
    win[SUBLANES + rows:, :] = jnp.where(bi < nsteps - 1, next_ref[0], 0.0)

    lane = lax.broadcasted_iota(jnp.int32, (2 * h, LANES), 1)
    ri = lax.broadcasted_iota(jnp.int32, (CHUNK, CHUNK), 0)
    ci = lax.broadcasted_iota(jnp.int32, (CHUNK, CHUNK), 1)
    lane128 = lax.broadcasted_iota(jnp.int32, (CHUNK, LANES), 1)
    lo_mask = lane128 < SSD_HEAD_DIM
    neg_a = -jnp.exp(alog_ref[...])

    def conv_silu(c, ncols):
        r0 = pl.multiple_of(c * CHUNK, CHUNK)
        cwin[...] = win[pl.ds(r0, CHUNK + 2 * SUBLANES), :]
        for j0 in range(0, ncols, LANES):
            acc = jnp.broadcast_to(cb_ref[:, j0:j0 + LANES], (CHUNK, LANES))
            for k in range(CONV_WIDTH):
                off = SUBLANES - CONV_WIDTH // 2 + k
                acc = acc + cwin[off:off + CHUNK, j0:j0 + LANES] * cw_ref[k:k + 1, j0:j0 + LANES]
            u_ref[:, j0:j0 + LANES] = _silu(acc)

    def dt_rows(c):
        dt = _softplus(dtr_ref[c] + bias_ref[...])
        a = dt * neg_a
        pre = _lane_cumsum(a, lane, False)
        suf = _lane_cumsum(a, lane, True)
        return dt, pre, suf

    def state_update(state_ref, w_rows, tot_col):
        dec = jnp.exp(jnp.sum(tot_col * exp_ref[...], axis=0, keepdims=True))
        for g in range(SSD_GROUPS):
            bm = u_ref[:, SSD_INNER + g * SSD_STATE:SSD_INNER + (g + 1) * SSD_STATE]
            bt = bm.T
            for pr in range(2):
                h0 = g * 4 + pr * 2
                c0 = h0 * SSD_HEAD_DIM
                xp = u_ref[:, c0:c0 + LANES]
                rhs = jnp.concatenate([jnp.where(lo_mask, xp, 0.0), jnp.where(lo_mask, 0.0, xp)],
                                      axis=0).astype(BF16)
                lhs = jnp.concatenate([bt * w_rows[h0:h0 + 1, :], bt * w_rows[h0 + 1:h0 + 2, :]],
                                      axis=1).astype(BF16)
                sres = jnp.dot(lhs, rhs, preferred_element_type=F32)
                state_ref[:, c0:c0 + LANES] = state_ref[:, c0:c0 + LANES] * dec[:, c0:c0 + LANES] + sres

    def col_forms(rows16):
        padded = jnp.concatenate([rows16, jnp.zeros((CHUNK - 2 * h, LANES), F32)], axis=0)
        return padded.T

    def backward_chunk(k, carry):
        c = cps - 1 - k
        gc = bi * cps + c
        conv_silu(c, SSD_INNER + SSD_GROUPS * SSD_STATE)
        dt, _, suf = dt_rows(c)
        hbs[gc] = hb[...].astype(BF16)
        s0 = suf[:, 0:1]
        w_rows = jnp.exp(s0 - suf) * dt
        state_update(hb, w_rows[h:2 * h, :], s0[h:2 * h, :])
        return carry

    def forward_chunk(c, carry):
        gc = bi * cps + c
        r0 = pl.multiple_of(c * CHUNK, CHUNK)
        conv_silu(c, CONV_DIM)
        dt, pre, suf = dt_rows(c)
        rowsf = jnp.concatenate([pre[0:h, :], suf[h:2 * h, :]], axis=0)
        colsf = col_forms(rowsf)
        ecol = jnp.exp(colsf)
        for g in range(SSD_GROUPS):
            bm = u_ref[:, SSD_INNER + g * SSD_STATE:SSD_INNER + (g + 1) * SSD_STATE]
            cm = u_ref[:, SSD_INNER + (SSD_GROUPS + g) * SSD_STATE:SSD_INNER + (SSD_GROUPS + g + 1) * SSD_STATE]
            cmb = cm.astype(BF16)
            cbm = lax.dot_general(cmb, bm.astype(BF16), (((1,), (1,)), ((), ())),
                                  preferred_element_type=F32)
            for pr in range(2):
                h0 = g * 4 + pr * 2
                c0 = h0 * SSD_HEAD_DIM
                lhs_parts = []
                for hh in (h0, h0 + 1):
                    segf = colsf[:, hh:hh + 1] - rowsf[hh:hh + 1, :]
                    gf = jnp.exp(jnp.where(ci <= ri, segf, NEG)) * dt[hh:hh + 1, :]
                    segb = colsf[:, h + hh:h + hh + 1] - rowsf[h + hh:h + hh + 1, :]
                    gb = jnp.exp(jnp.where(ci >= ri, segb, NEG)) * dt[h + hh:h + hh + 1, :]
                    lhs_parts.append((cbm * (gf + gb)).astype(BF16))
                for hh in (h0, h0 + 1):
                    lhs_parts.append((cm * ecol[:, hh:hh + 1]).astype(BF16))
                for hh in (h0, h0 + 1):
                    lhs_parts.append((cm * ecol[:, h + hh:h + hh + 1]).astype(BF16))
                lhs = jnp.concatenate(lhs_parts, axis=1)
                xp = u_ref[:, c0:c0 + LANES]
                sf = hf[:, c0:c0 + LANES]
                sb = hbs[gc, :, c0:c0 + LANES].astype(F32)
                rhs = jnp.concatenate(
                    [jnp.where(lo_mask, xp, 0.0), jnp.where(lo_mask, 0.0, xp),
                     jnp.where(lo_mask, sf, 0.0), jnp.where(lo_mask, 0.0, sf),
                     jnp.where(lo_mask, sb, 0.0), jnp.where(lo_mask, 0.0, sb)], axis=0).astype(BF16)
                ypair = jnp.dot(lhs, rhs, preferred_element_type=F32)
                ypair = ypair + dskip_ref[:, c0:c0 + LANES] * xp
                zz = z_ref[0, pl.ds(r0, CHUNK), c0:c0 + LANES]
                u_ref[:, CONV_DIM + c0:CONV_DIM + c0 + LANES] = ypair * _silu(zz)
        last = pre[:, LANES - 1:LANES]
        w_rows = jnp.exp(last - pre) * dt
        state_update(hf, w_rows[0:h, :], last[0:h, :])
        if need_y:
            yv = u_ref[:, CONV_DIM:CONV_DIM + SSD_INNER]
            ms = jnp.mean(yv * yv, axis=-1, keepdims=True)
            y_ref[0, pl.ds(r0, CHUNK), :] = (yv * lax.rsqrt(ms + EPS) * gain_ref[...]).astype(y_ref.dtype)
        return carry

    @pl.when(sw == 0)
    def _():
        lax.fori_loop(0, cps, backward_chunk, 0)

    @pl.when(sw == 1)
    def _():
        lax.fori_loop(0, cps, forward_chunk, 0)

    @pl.when((sw == 0) & (s == nsteps - 1))
    def _():
        hb_out[0] = hb[...]

    @pl.when((sw == 1) & (s == nsteps - 1))
    def _():
        hf_out[0] = hf[...]


def _ssd(xbc, dtr, z, consts, h0f, h0b, *, rows, need_y):
    bt, seq, _ = xbc.shape
    nsteps = seq // rows
    cps = rows // CHUNK
    hb8 = rows // SUBLANES
    nb8 = seq // SUBLANES
    cw, cb, bias, alog, dskip, gain, expm = consts

    def blk(b, sw, s):
        return jnp.where(sw == 0, nsteps - 1 - s, s)

    def full(a):
        return pl.BlockSpec(a.shape, lambda b, sw, s: (0,) * a.ndim)

    in_specs = [
        pl.BlockSpec((1, rows, CONV_DIM), lambda b, sw, s: (b, blk(b, sw, s), 0)),
        pl.BlockSpec((1, SUBLANES, CONV_DIM), lambda b, sw, s: (b, jnp.maximum(blk(b, sw, s) * hb8 - 1, 0), 0)),
        pl.BlockSpec((1, SUBLANES, CONV_DIM),
                     lambda b, sw, s: (b, jnp.minimum((blk(b, sw, s) + 1) * hb8, nb8 - 1), 0)),
        pl.BlockSpec((cps, 2 * SSD_HEADS, CHUNK), lambda b, sw, s: (b * nsteps + blk(b, sw, s), 0, 0)),
        pl.BlockSpec((1, rows, SSD_INNER), lambda b, sw, s: (b, jnp.where(sw == 0, 0, s), 0)),
        full(cw), full(cb), full(bias), full(alog), full(dskip), full(gain), full(expm),
        pl.BlockSpec((1, SSD_STATE, SSD_INNER), lambda b, sw, s: (b, 0, 0)),
        pl.BlockSpec((1, SSD_STATE, SSD_INNER), lambda b, sw, s: (b, 0, 0)),
    ]
    st_spec = pl.BlockSpec((1, SSD_STATE, SSD_INNER), lambda b, sw, s: (b, 0, 0))
    st_shape = jax.ShapeDtypeStruct((bt, SSD_STATE, SSD_INNER), F32)
    out_specs, out_shapes = [st_spec, st_spec], [st_shape, st_shape]
    if need_y:
        out_specs = [pl.BlockSpec((1, rows, SSD_INNER), lambda b, sw, s: (b, jnp.where(sw == 0, 0, s), 0))] + out_specs
        out_shapes = [jax.ShapeDtypeStruct((bt, seq, SSD_INNER), BF16)] + out_shapes
    scratch = [
        pltpu.VMEM((rows + 2 * SUBLANES, CONV_DIM), F32),
        pltpu.VMEM((CHUNK + 2 * SUBLANES, CONV_DIM), F32),
        pltpu.VMEM((CHUNK, CONV_DIM + SSD_INNER), F32),
        pltpu.VMEM((SSD_STATE, SSD_INNER), F32),
        pltpu.VMEM((SSD_STATE, SSD_INNER), F32),
        pltpu.VMEM((seq // CHUNK, SSD_STATE, SSD_INNER), BF16),
    ]
    return pl.pallas_call(
        functools.partial(_ssd_kernel, need_y=need_y, nsteps=nsteps, cps=cps),
        grid=(bt, 2, nsteps),
        in_specs=in_specs,
        out_specs=out_specs,
        out_shape=out_shapes,
        scratch_shapes=scratch,
        compiler_params=pltpu.CompilerParams(dimension_semantics=("arbitrary", "arbitrary", "arbitrary")),
        name="ssd_y" if need_y else "ssd_ctx",
    )(xbc, xbc, xbc, dtr, z, cw, cb, bias, alog, dskip, gain, expm, h0f, h0b)


def _attn_kernel(sink_ref, q_ref, kp_ref, kc_ref, kn_ref, vp_ref, vc_ref, vn_ref, kx_ref, vx_ref, o_ref, *, nsteps):
    m = pl.program_id(1)
    lane = lax.broadcasted_iota(jnp.int32, (1, LANES), 1)
    lo = lane < HEAD_DIM
    rq = Q_PER_KV * WINDOW
    qi = lax.broadcasted_iota(jnp.int32, (rq, WINDOW), 0) & (WINDOW - 1)
    kj = lax.broadcasted_iota(jnp.int32, (rq, WINDOW), 1)
    rblk = lax.shift_right_logical(lax.broadcasted_iota(jnp.int32, (rq, 1), 0), int(math.log2(WINDOW)))
    zero = jnp.zeros((), BF16)

    kpieces = [kp_ref[0], kc_ref[0, 0:WINDOW], kc_ref[0, WINDOW:2 * WINDOW], kn_ref[0]]
    vpieces = [vp_ref[0], vc_ref[0, 0:WINDOW], vc_ref[0, WINDOW:2 * WINDOW], vn_ref[0]]
    kx = kx_ref[0]
    vx = vx_ref[0]
    nt = (((1,), (1,)), ((), ()))

    for sb in range(ROWS_ATT // WINDOW):
        qsb = q_ref[0, sb * WINDOW:(sb + 1) * WINDOW, :]
        prev_ok = jnp.logical_or(m > 0, sb > 0)
        next_ok = jnp.logical_or(m < nsteps - 1, sb < ROWS_ATT // WINDOW - 1)
        outs = [None] * Q_PER_KV
        for kv in range(ATTN_KV_HEADS):
            sel = lo if kv == 0 else jnp.logical_not(lo)
            qs = jnp.concatenate(
                [jnp.where(sel, qsb[:, j * LANES:(j + 1) * LANES], zero) for j in range(Q_PER_KV)], axis=0)
            s_prev = lax.dot_general(qs, kpieces[sb], nt, preferred_element_type=F32)
            s_cur = lax.dot_general(qs, kpieces[sb + 1], nt, preferred_element_type=F32)
            s_next = lax.dot_general(qs, kpieces[sb + 2], nt, preferred_element_type=F32)
            s_ctx = lax.dot_general(qs, kx, nt, preferred_element_type=F32)
            s_prev = jnp.where((kj >= qi) & prev_ok, s_prev, NEG)
            s_next = jnp.where((kj <= qi) & next_ok, s_next, NEG)
            sink = jnp.zeros((rq, 1), F32)
            for j in range(Q_PER_KV):
                sink = jnp.where(rblk == j, sink_ref[kv * Q_PER_KV + j], sink)
            mx = jnp.maximum(
                jnp.maximum(jnp.max(s_prev, axis=-1, keepdims=True), jnp.max(s_cur, axis=-1, keepdims=True)),
                jnp.maximum(jnp.max(s_next, axis=-1, keepdims=True), jnp.max(s_ctx, axis=-1, keepdims=True)))
            mx = jnp.maximum(mx, sink)
            p_prev = jnp.exp(s_prev - mx)
            p_cur = jnp.exp(s_cur - mx)
            p_next = jnp.exp(s_next - mx)
            p_ctx = jnp.exp(s_ctx - mx)
            den = (jnp.sum(p_prev, axis=-1, keepdims=True) + jnp.sum(p_cur, axis=-1, keepdims=True)
                   + jnp.sum(p_next, axis=-1, keepdims=True) + jnp.sum(p_ctx, axis=-1, keepdims=True)
                   + jnp.exp(sink - mx))
            acc = jnp.dot(p_prev.astype(BF16), jnp.where(sel, vpieces[sb], zero), preferred_element_type=F32)
            acc += jnp.dot(p_cur.astype(BF16), jnp.where(sel, vpieces[sb + 1], zero), preferred_element_type=F32)
            acc += jnp.dot(p_next.astype(BF16), jnp.where(sel, vpieces[sb + 2], zero), preferred_element_type=F32)
            acc += jnp.dot(p_ctx.astype(BF16), jnp.where(sel, vx, zero), preferred_element_type=F32)
            acc = acc * (1.0 / den)
            for j in range(Q_PER_KV):
                part = acc[j * WINDOW:(j + 1) * WINDOW, :]
                outs[j] = part if outs[j] is None else outs[j] + part
        for j in range(Q_PER_KV):
            o_ref[0, sb * WINDOW:(sb + 1) * WINDOW, j * LANES:(j + 1) * LANES] = outs[j].astype(o_ref.dtype)


def _attention(sinks, q, k, v, kx, vx):
    bt, seq, _ = q.shape
    nsteps = seq // ROWS_ATT
    per = ROWS_ATT // WINDOW
    nb = seq // WINDOW
    n_ctx = kx.shape[1]

    def prev(b, m, s):
        return (b, jnp.maximum(m * per - 1, 0), 0)

    def cur(b, m, s):
        return (b, m, 0)

    def nxt(b, m, s):
        return (b, jnp.minimum((m + 1) * per, nb - 1), 0)

    small = (1, WINDOW, KV_WIDTH)
    big = (1, ROWS_ATT, KV_WIDTH)
    grid_spec = pltpu.PrefetchScalarGridSpec(
        num_scalar_prefetch=1,
        grid=(bt, nsteps),
        in_specs=[
            pl.BlockSpec((1, ROWS_ATT, ATTN_WIDTH), cur),
            pl.BlockSpec(small, prev), pl.BlockSpec(big, cur), pl.BlockSpec(small, nxt),
            pl.BlockSpec(small, prev), pl.BlockSpec(big, cur), pl.BlockSpec(small, nxt),
            pl.BlockSpec((1, n_ctx, KV_WIDTH), lambda b, m, s: (b, 0, 0)),
            pl.BlockSpec((1, n_ctx, KV_WIDTH), lambda b, m, s: (b, 0, 0)),
        ],
        out_specs=pl.BlockSpec((1, ROWS_ATT, ATTN_WIDTH), cur),
    )
    return pl.pallas_call(
        functools.partial(_attn_kernel, nsteps=nsteps),
        grid_spec=grid_spec,
        out_shape=jax.ShapeDtypeStruct((bt, seq, ATTN_WIDTH), BF16),
        name="attn",
    )(sinks, q, k, k, k, v, v, v, kx, vx)


def _outproj_kernel(ssd_ref, att_ref, x_ref, g1_ref, sh_ref, sc_ref, n2_ref, wa_ref, wb_ref, wr_ref, br_ref,
                    tri_ref, x1_ref, h2_ref, ids_ref, wcol_ref, cnt_ref, carry, *, nsteps):
    i = pl.program_id(0)
    rows = x_ref.shape[0]

    @pl.when(i == 0)
    def _():
        carry[...] = jnp.zeros_like(carry)

    acc = jnp.dot(ssd_ref[...], wa_ref[...], preferred_element_type=F32)
    acc += jnp.dot(att_ref[...], wb_ref[...], preferred_element_type=F32)
    x1 = x_ref[...] + g1_ref[0] * acc
    x1_ref[...] = x1
    ms = jnp.mean(x1 * x1, axis=-1, keepdims=True)
    h2 = (x1 * lax.rsqrt(ms + EPS)) * (n2_ref[...] * (1.0 + sc_ref[0])) + sh_ref[0]
    for sidx in range(d_tiles := h2.shape[1] // LANES):
        h2_ref[pl.ds(sidx, rows, stride=d_tiles), :] = h2[:, sidx * LANES:(sidx + 1) * LANES]
    lt = lax.dot_general(wr_ref[...], h2.astype(BF16), (((1,), (1,)), ((), ())),
                         preferred_element_type=F32) + br_ref[...]
    row = lax.broadcasted_iota(jnp.int32, lt.shape, 0).astype(F32)
    big_i = float(ROUTE_ROWS)
    gl = jnp.where(row < N_GROUPS, lt, NEG)
    gmax = jnp.max(gl, axis=0, keepdims=True)
    gidx = jnp.min(jnp.where(gl == gmax, row, big_i), axis=0, keepdims=True)
    g_w = 1.0 / jnp.sum(jnp.exp(gl - gmax), axis=0, keepdims=True)
    lo = N_GROUPS + EXPERTS_PER_GROUP * gidx
    el = jnp.where((row >= lo) & (row < lo + EXPERTS_PER_GROUP), lt, NEG)
    m1 = jnp.max(el, axis=0, keepdims=True)
    i1 = jnp.min(jnp.where(el == m1, row, big_i), axis=0, keepdims=True)
    el2 = jnp.where(row == i1, NEG, el)
    m2 = jnp.max(el2, axis=0, keepdims=True)
    i2 = jnp.min(jnp.where(el2 == m2, row, big_i), axis=0, keepdims=True)
    r = jnp.exp(m2 - m1)
    w1 = g_w / (1.0 + r)
    w2 = g_w * r / (1.0 + r)
    oh1 = row == i1
    oh2 = row == i2
    cnt = jnp.where(oh1 | oh2, 1.0, 0.0)
    prefix = jnp.dot(cnt.astype(BF16), tri_ref[...], preferred_element_type=F32)
    base = carry[...] + prefix
    rank1 = jnp.sum(jnp.where(oh1, base, 0.0), axis=0, keepdims=True)
    rank2 = jnp.sum(jnp.where(oh2, base, 0.0), axis=0, keepdims=True)
    carry[...] = carry[...] + jnp.sum(cnt, axis=1, keepdims=True)
    r8 = lax.broadcasted_iota(jnp.int32, (SUBLANES, rows), 0)
    code1 = (i1 - N_GROUPS) * float(1 << RANK_BITS) + rank1
    code2 = (i2 - N_GROUPS) * float(1 << RANK_BITS) + rank2
    ids_ref[...] = jnp.where(r8 == 0, code1, jnp.where(r8 == 1, code2, 0.0)).astype(jnp.int32)
    r128 = lax.broadcasted_iota(jnp.int32, (LANES, rows), 0)
    wfull = jnp.where(r128 == 0, w1, jnp.where(r128 == 1, w2, 0.0))
    for j in range(rows // LANES):
        wcol_ref[j * LANES:(j + 1) * LANES, :] = wfull[:, j * LANES:(j + 1) * LANES].T

    @pl.when(i == nsteps - 1)
    def _():
        cnt_ref[...] = carry[:, 0:LANES]


def _outproj(ssd, att, xf, mod3, norm2, wa, wb, wr, br, tri, *, rows, seq_blocks):
    t, d = xf.shape
    nsteps = t // rows

    def mod(col):
        return pl.BlockSpec((1, 1, d), lambda i: (i // seq_blocks, 0, col))

    def full(a):
        return pl.BlockSpec(a.shape, lambda i: (0,) * a.ndim)

    return pl.pallas_call(
        functools.partial(_outproj_kernel, nsteps=nsteps),
        grid=(nsteps,),
        in_specs=[pl.BlockSpec((rows, SSD_INNER), lambda i: (i, 0)),
                  pl.BlockSpec((rows, ATTN_WIDTH), lambda i: (i, 0)),
                  pl.BlockSpec((rows, d), lambda i: (i, 0)),
                  mod(2), mod(3), mod(4), full(norm2), full(wa), full(wb), full(wr), full(br), full(tri)],
        out_specs=[pl.BlockSpec((rows, d), lambda i: (i, 0)),
                   pl.BlockSpec((rows * (d // LANES), LANES), lambda i: (i, 0)),
                   pl.BlockSpec((SUBLANES, rows), lambda i: (0, i)),
                   pl.BlockSpec((rows, LANES), lambda i: (i, 0)),
                   pl.BlockSpec((ROUTE_ROWS, LANES), lambda i: (0, 0))],
        out_shape=[jax.ShapeDtypeStruct((t, d), F32),
                   jax.ShapeDtypeStruct((t * (d // LANES), LANES), F32),
                   jax.ShapeDtypeStruct((SUBLANES, t), jnp.int32),
                   jax.ShapeDtypeStruct((t, LANES), F32),
                   jax.ShapeDtypeStruct((ROUTE_ROWS, LANES), F32)],
        scratch_shapes=[pltpu.VMEM((ROUTE_ROWS, rows), F32)],
        compiler_params=pltpu.CompilerParams(dimension_semantics=("arbitrary",)),
        name="outproj_router",
    )(ssd, att, xf, mod3, mod3, mod3, norm2, wa, wb, wr, br, tri)


def _expert_kernel(blk_e_ref, nused_ref, pstart_ref, counts_ref, dst_ref, h2_hbm, wg_ref, wu_ref, wd_ref, y_ref,
                   xbuf, wgb, wub, wdb, tok_ref, sem, *, n_tok, nblocks):
    b = pl.program_id(0)
    nused = nused_ref[0]
    slot = b % 2

    def row_copy(tok, r, sl):
        return pltpu.make_async_copy(h2_hbm.at[pl.ds(tok * D_TILES, D_TILES)],
                                     xbuf.at[sl, pl.ds(r * D_TILES, D_TILES)], sem.at[sl])

    def start_rows(blk, sl, r0, r1):
        for r in range(r0, r1):
            row_copy(tok_ref[blk * MOE_BLK + r], r, sl).start()

    def wait_rows(sl):
        for r in range(MOE_BLK):
            row_copy(0, r, sl).wait()

    @pl.when(b == 0)
    def _():
        def pad_expert(e, carry):
            lo = pstart_ref[e] + counts_ref[e]
            hi = pstart_ref[e] + (counts_ref[e] + MOE_BLK - 1) // MOE_BLK * MOE_BLK

            def pad_slot(s, c):
                tok_ref[s] = 0
                return c
            return lax.fori_loop(lo, hi, pad_slot, carry)
        lax.fori_loop(0, N_EXPERTS, pad_expert, 0)

        def claim(i, carry):
            tok_ref[dst_ref[i]] = jnp.where(i >= n_tok, i - n_tok, i)
            return carry
        lax.fori_loop(0, 2 * n_tok, claim, 0, unroll=16)
        start_rows(0, 0, 0, MOE_BLK)

    e = blk_e_ref[b]
    e_prev = blk_e_ref[jnp.maximum(b - 1, 0)]

    @pl.when((b == 0) | (e != e_prev))
    def _():
        wgb[...] = wg_ref[0].astype(BF16)
        wub[...] = wu_ref[0].astype(BF16)
        wdb[...] = wd_ref[0].astype(BF16)

    nxt = jnp.minimum(b + 1, nused - 1)
    quarter = MOE_BLK // 4
    wait_rows(slot)
    xb = jnp.concatenate([xbuf[slot, pl.ds(s, MOE_BLK, stride=D_TILES), :] for s in range(D_TILES)],
                         axis=1).astype(BF16)
    start_rows(nxt, 1 - slot, 0, quarter)
    gate = jnp.dot(xb, wgb[...], preferred_element_type=F32)
    start_rows(nxt, 1 - slot, quarter, 2 * quarter)
    up = jnp.dot(xb, wub[...], preferred_element_type=F32)
    start_rows(nxt, 1 - slot, 2 * quarter, 3 * quarter)
    hmid = (_silu(gate) * up).astype(BF16)
    start_rows(nxt, 1 - slot, 3 * quarter, MOE_BLK)
    y = jnp.dot(hmid, wdb[...], preferred_element_type=F32)
    for s in range(D_TILES):
        y_ref[pl.ds(s, MOE_BLK, stride=D_TILES), :] = y[:, s * LANES:(s + 1) * LANES]

    @pl.when(b == nblocks - 1)
    def _():
        wait_rows(1 - slot)


def _experts(blk_e, nused, pstart, counts, dst, h2t, w_gate, w_up, w_down, *, nblocks, n_tok):
    d = w_gate.shape[1]

    def wspec(shape):
        return pl.BlockSpec(shape, lambda b, be, nu, ps, ct, cd: (be[b], 0, 0))

    grid_spec = pltpu.PrefetchScalarGridSpec(
        num_scalar_prefetch=5,
        grid=(nblocks,),
        in_specs=[pl.BlockSpec(memory_space=pl.ANY),
                  wspec((1, d, EXPERT_DIM)), wspec((1, d, EXPERT_DIM)), wspec((1, EXPERT_DIM, d))],
        out_specs=pl.BlockSpec((MOE_BLK * D_TILES, LANES), lambda b, be, nu, ps, ct, cd: (b, 0)),
        scratch_shapes=[
            pltpu.VMEM((2, MOE_BLK * D_TILES, LANES), F32),
            pltpu.VMEM((d, EXPERT_DIM), BF16),
            pltpu.VMEM((d, EXPERT_DIM), BF16),
            pltpu.VMEM((EXPERT_DIM, d), BF16),
            pltpu.SMEM((nblocks * MOE_BLK,), jnp.int32),
            pltpu.SemaphoreType.DMA((2,)),
        ],
    )
    return pl.pallas_call(
        functools.partial(_expert_kernel, n_tok=n_tok, nblocks=nblocks),
        grid_spec=grid_spec,
        out_shape=jax.ShapeDtypeStruct((nblocks * MOE_BLK * D_TILES, LANES), F32),
        compiler_params=pltpu.CompilerParams(dimension_semantics=("arbitrary",)),
        name="experts",
    )(blk_e, nused, pstart, counts, dst, h2t, w_gate, w_up, w_down)


def _combine_kernel(dst_ref, y_hbm, x1_ref, wcol_ref, g2_ref, nf_ref, o_ref, ybuf, sem, *, nsteps, n_tok):
    i = pl.program_id(0)
    rows = x1_ref.shape[0]
    slot = i % 2

    def row_copy(src, r, k, sl):
        return pltpu.make_async_copy(y_hbm.at[pl.ds(src * D_TILES, D_TILES)],
                                     ybuf.at[sl, k, pl.ds(r * D_TILES, D_TILES)], sem.at[sl])

    def start_rows(step, sl, r0, r1):
        for r in range(r0, r1):
            for k in range(2):
                row_copy(dst_ref[k * n_tok + step * rows + r], r, k, sl).start()

    def wait_rows(sl):
        for r in range(rows):
            for k in range(2):
                row_copy(0, r, k, sl).wait()

    @pl.when(i == 0)
    def _():
        start_rows(0, 0, 0, rows)

    nxt = jnp.minimum(i + 1, nsteps - 1)
    per = rows // D_TILES
    wait_rows(slot)
    w0 = wcol_ref[:, 0:1]
    w1 = wcol_ref[:, 1:2]
    ssq = jnp.zeros((rows, 1), F32)
    for s in range(D_TILES):
        cols = slice(s * LANES, (s + 1) * LANES)
        moe = (ybuf[slot, 0, pl.ds(s, rows, stride=D_TILES), :] * w0
               + ybuf[slot, 1, pl.ds(s, rows, stride=D_TILES), :] * w1)
        x2 = x1_ref[:, cols] + g2_ref[0, :, cols] * moe
        ssq = ssq + jnp.sum(x2 * x2, axis=-1, keepdims=True)
        o_ref[:, cols] = x2
        start_rows(nxt, 1 - slot, s * per, (s + 1) * per)
    inv = lax.rsqrt(ssq * (1.0 / (D_TILES * LANES)) + EPS)
    o_ref[...] = o_ref[...] * inv * nf_ref[...]

    @pl.when(i == nsteps - 1)
    def _():
        wait_rows(1 - slot)


def _combine(dst, y_buf, x1, wcol, mod3, norm_final, *, rows, seq_blocks):
    t, d = x1.shape
    nsteps = t // rows
    grid_spec = pltpu.PrefetchScalarGridSpec(
        num_scalar_prefetch=1,
        grid=(nsteps,),
        in_specs=[
            pl.BlockSpec(memory_space=pl.ANY),
            pl.BlockSpec((rows, d), lambda i, ds: (i, 0)),
            pl.BlockSpec((rows, LANES), lambda i, ds: (i, 0)),
            pl.BlockSpec((1, 1, d), lambda i, ds: (i // seq_blocks, 0, 5)),
            pl.BlockSpec((1, d), lambda i, ds: (0, 0)),
        ],
        out_specs=pl.BlockSpec((rows, d), lambda i, ds: (i, 0)),
        scratch_shapes=[pltpu.VMEM((2, 2, rows * D_TILES, LANES), F32), pltpu.SemaphoreType.DMA((2,))],
    )
    return pl.pallas_call(
        functools.partial(_combine_kernel, nsteps=nsteps, n_tok=t),
        grid_spec=grid_spec,
        out_shape=jax.ShapeDtypeStruct((t, d), F32),
        compiler_params=pltpu.CompilerParams(dimension_semantics=("arbitrary",)),
        name="combine",
    )(dst, y_buf, x1, wcol, mod3, norm_final)


def _rope_tables(seq):
    pos = np.arange(seq)
    n_freq = HEAD_DIM // 4
    inv = ROPE_BASE ** (-np.arange(n_freq, dtype=np.float32) / n_freq)
    ang = np.concatenate([(pos // GRID_W)[:, None] * inv, (pos % GRID_W)[:, None] * inv], axis=-1)
    ang = np.concatenate([ang, ang, ang, ang], axis=-1).astype(np.float32)
    return jnp.asarray(np.cos(ang), F32), jnp.asarray(np.sin(ang), F32)


def _head_expand_mat():
    m = np.zeros((SSD_HEADS, SSD_INNER), np.float32)
    for hh in range(SSD_HEADS):
        m[hh, hh * SSD_HEAD_DIM:(hh + 1) * SSD_HEAD_DIM] = 1.0
    return jnp.asarray(m)


def kernel(x, c, ctx, c_ctx, w_ada, b_ada, norm1, w_in, conv_w, conv_b, dt_bias, a_log, d_skip, ssd_norm,
           attn_sinks, w_out, norm2, w_group, b_group, w_expert, b_expert, w_gate, w_up, w_down, norm_final):
    return _pipeline(x, c, ctx, c_ctx, w_ada, b_ada, norm1, w_in, conv_w, conv_b, dt_bias, a_log, d_skip,
                     ssd_norm, attn_sinks, w_out, norm2, w_group, b_group, w_expert, b_expert, w_gate, w_up,
                     w_down, norm_final)["out"]


def _pipeline(x, c, ctx, c_ctx, w_ada, b_ada, norm1, w_in, conv_w, conv_b, dt_bias, a_log, d_skip, ssd_norm,
              attn_sinks, w_out, norm2, w_group, b_group, w_expert, b_expert, w_gate, w_up, w_down, norm_final):
    bt, seq, d = x.shape
    n_ctx = ctx.shape[1]
    t = bt * seq
    layer = 0
    assert w_ada.shape[0] == 1 and seq % ROWS_IN == 0 and n_ctx % CHUNK == 0

    cc = jnp.zeros((SUBLANES, d), F32).at[:bt].set(c).at[bt].set(c_ctx)
    mod = _ada(cc, w_ada[layer], b_ada[layer][None, :])
    mod3 = mod.reshape(SUBLANES, 1, 6 * d)

    w = w_in[layer]
    o_z, o_xbc, o_dt = 0, SSD_INNER, SSD_INNER + CONV_DIM
    o_q = o_dt + 2 * SSD_HEADS
    o_k, o_v = o_q + ATTN_WIDTH, o_q + ATTN_WIDTH + KV_WIDTH
    q_perm = np.concatenate([np.r_[j * HEAD_DIM:(j + 1) * HEAD_DIM, (j + 4) * HEAD_DIM:(j + 5) * HEAD_DIM]
                             for j in range(Q_PER_KV)])
    w_q = w[:, o_q:o_q + ATTN_WIDTH].reshape(d, ATTN_KV_HEADS, Q_PER_KV, HEAD_DIM).transpose(0, 2, 1, 3)
    w_q = w_q.reshape(d, ATTN_WIDTH)
    w_cat = jnp.concatenate([w[:, o_z:o_z + SSD_INNER], w[:, o_xbc:o_xbc + CONV_DIM], w_q,
                             w[:, o_k:o_k + KV_WIDTH], w[:, o_v:o_v + KV_WIDTH]], axis=1).astype(BF16)
    w_dtT = w[:, o_dt:o_dt + 2 * SSD_HEADS].T.astype(BF16)

    seq_blocks = seq // ROWS_IN
    cos, sin = _rope_tables(seq)
    segs_x = [("z", _COL_Z, SSD_INNER), ("xbc", _COL_XBC, CONV_DIM), ("dtT", 0, 0),
              ("q", _COL_Q, ATTN_WIDTH), ("k", _COL_K, KV_WIDTH), ("v", _COL_V, KV_WIDTH)]
    z_x, xbc_x, dtr_x, q_x, k_x, v_x = _inproj(
        x.reshape(t, d), mod3, norm1[layer][None, :], w_cat, w_dtT, (cos, sin),
        rows=ROWS_IN, mod_row_fn=lambda i: i // seq_blocks, segs=segs_x, seq_blocks=seq_blocks)
    segs_c = [("xbc", _COL_XBC, CONV_DIM), ("dtT", 0, 0), ("k", _COL_K, KV_WIDTH), ("v", _COL_V, KV_WIDTH)]
    xbc_c, dtr_c, k_c, v_c = _inproj(
        ctx.reshape(bt * n_ctx, d), mod3, norm1[layer][None, :], w_cat, w_dtT, None,
        rows=n_ctx, mod_row_fn=lambda i: bt, segs=segs_c, seq_blocks=1)

    cw = jnp.zeros((SUBLANES, CONV_DIM), F32).at[:CONV_WIDTH].set(conv_w[layer])
    cb = conv_b[layer][None, :]
    bias_rows = jnp.broadcast_to(dt_bias[layer].reshape(2 * SSD_HEADS, 1), (2 * SSD_HEADS, LANES))
    alog_rows = jnp.broadcast_to(a_log[layer].reshape(2 * SSD_HEADS, 1), (2 * SSD_HEADS, LANES))
    dskip_e = jnp.repeat(d_skip[layer], SSD_HEAD_DIM)[None, :]
    consts = (cw, cb, bias_rows, alog_rows, dskip_e, ssd_norm[layer][None, :], _head_expand_mat())

    zeros_state = jnp.zeros((bt, SSD_STATE, SSD_INNER), F32)
    z_dummy = jnp.zeros((bt, n_ctx, SSD_INNER), F32)
    h_cf, h_cb = _ssd(xbc_c.reshape(bt, n_ctx, CONV_DIM), dtr_c, z_dummy, consts,
                      zeros_state, zeros_state, rows=n_ctx, need_y=False)
    ssd_x, _, _ = _ssd(xbc_x.reshape(bt, seq, CONV_DIM), dtr_x, z_x.reshape(bt, seq, SSD_INNER),
                       consts, h_cf, h_cb, rows=ROWS_SSD, need_y=True)

    attn_x = _attention(attn_sinks[layer].astype(F32),
                        q_x.reshape(bt, seq, ATTN_WIDTH), k_x.reshape(bt, seq, KV_WIDTH),
                        v_x.reshape(bt, seq, KV_WIDTH), k_c.reshape(bt, n_ctx, KV_WIDTH),
                        v_c.reshape(bt, n_ctx, KV_WIDTH))

    wo = w_out[layer]
    wa = wo[:SSD_INNER].astype(BF16)
    wb = wo[SSD_INNER:].reshape(ATTN_KV_HEADS, Q_PER_KV, HEAD_DIM, d).transpose(1, 0, 2, 3)
    wb = wb.reshape(ATTN_WIDTH, d).astype(BF16)
    wr = jnp.zeros((ROUTE_ROWS, d), F32).at[:N_GROUPS].set(w_group[layer].T)
    wr = wr.at[N_GROUPS:N_GROUPS + N_EXPERTS].set(w_expert[layer].T).astype(BF16)
    br = jnp.zeros((ROUTE_ROWS,), F32).at[:N_GROUPS].set(b_group[layer])
    br = br.at[N_GROUPS:N_GROUPS + N_EXPERTS].set(b_expert[layer])
    br = jnp.broadcast_to(br[:, None], (ROUTE_ROWS, ROWS_IN))
    tri = jnp.asarray(np.triu(np.ones((ROWS_IN, ROWS_IN), np.float32), 1), BF16)
    x1, h2, ids, wcol, cnt = _outproj(
        ssd_x.reshape(t, SSD_INNER), attn_x.reshape(t, ATTN_WIDTH), x.reshape(t, d), mod3, norm2[layer][None, :],
        wa, wb, wr, br, tri, rows=ROWS_IN, seq_blocks=seq_blocks)

    counts = cnt[N_GROUPS:N_GROUPS + N_EXPERTS, 0].astype(jnp.int32)
    padded = (counts + MOE_BLK - 1) // MOE_BLK * MOE_BLK
    ends = jnp.cumsum(padded)
    pstart = ends - padded
    nblocks = (2 * t) // MOE_BLK + N_EXPERTS
    blk_start = jnp.arange(nblocks, dtype=jnp.int32) * MOE_BLK
    blk_e = jnp.minimum(jnp.sum((ends[None, :] <= blk_start[:, None]).astype(jnp.int32), axis=1), N_EXPERTS - 1)
    nused = (ends[-1] // MOE_BLK).astype(jnp.int32).reshape(1)
    code = ids[0:2].reshape(-1)
    e_id = lax.shift_right_logical(code, RANK_BITS)
    onehot = e_id[:, None] == jnp.arange(N_EXPERTS, dtype=jnp.int32)[None, :]
    dst = jnp.sum(jnp.where(onehot, pstart[None, :], 0), axis=1) + (code & ((1 << RANK_BITS) - 1))

    y_buf = _experts(blk_e, nused, pstart, counts, dst, h2, w_gate[layer], w_up[layer], w_down[layer],
                     nblocks=nblocks, n_tok=t)
    out = _combine(dst, y_buf, x1, wcol, mod3, norm_final[None, :], rows=ROWS_CMB, seq_blocks=seq // ROWS_CMB)
    return dict(out=out.reshape(bt, seq, d), mod=mod, z=z_x, xbc=xbc_x, dtr=dtr_x, q=q_x, k=k_x, v=v_x,
                k_c=k_c, v_c=v_c, h_cf=h_cf, h_cb=h_cb, ssd=ssd_x, attn=attn_x, x1=x1, h2=h2, ids=ids,
                wcol=wcol, cnt=cnt, y_buf=y_buf, pstart=pstart, q_perm=q_perm)
```

```python
import functools
import math

import numpy as np
import jax
import jax.numpy as jnp
from jax import lax
from jax.experimental import pallas as pl
from jax.experimental.pallas import tpu as pltpu

F32 = jnp.float32
BF16 = jnp.bfloat16

D_MODEL = 1024
GRID_W = 64
EPS = 1e-6
SSD_INNER = 512
SSD_HEAD_DIM = 64
SSD_HEADS = 8
SSD_GROUPS = 2
SSD_STATE = 128
CONV_WIDTH = 5
CONV_DIM = SSD_INNER + 2 * SSD_GROUPS * SSD_STATE
CHUNK = 128
ATTN_WIDTH = 512
HEAD_DIM = 64
ATTN_Q_HEADS = 8
ATTN_KV_HEADS = 2
Q_PER_KV = 4
KV_WIDTH = 128
WINDOW = 128
ROPE_BASE = 10000.0
N_GROUPS = 4
EXPERTS_PER_GROUP = 8
N_EXPERTS = 32
EXPERT_DIM = 512

LANES = 128
SUBLANES = 8
NEG = -1e30

ROWS_IN = 512
ROWS_SSD = 512
ROWS_ATT = 256
ROUTE_ROWS = 48
MOE_BLK = 128
RANK_BITS = 16
D_TILES = D_MODEL // LANES
ROWS_CMB = 128


def _silu(v):
    return v * (1.0 / (1.0 + jnp.exp(-v)))


def _softplus(v):
    return jnp.maximum(v, 0.0) + jnp.log(1.0 + jnp.exp(-jnp.abs(v)))


def _ada_kernel(c_ref, w_ref, b_ref, o_ref):
    a = _silu(c_ref[...])
    o_ref[...] = jnp.dot(a.astype(BF16), w_ref[...].astype(BF16),
                         preferred_element_type=F32) + b_ref[...]


def _ada(cc, w_ada, b_ada):
    d, n = w_ada.shape
    tn = 1536
    return pl.pallas_call(
        _ada_kernel,
        grid=(n // tn,),
        in_specs=[pl.BlockSpec((SUBLANES, d), lambda j: (0, 0)),
                  pl.BlockSpec((d, tn), lambda j: (0, j)),
                  pl.BlockSpec((1, tn), lambda j: (0, j))],
        out_specs=pl.BlockSpec((SUBLANES, tn), lambda j: (0, j)),
        out_shape=jax.ShapeDtypeStruct((SUBLANES, n), F32),
        name="ada",
    )(cc, w_ada, b_ada)


def _rope(t, cos, sin_signed, first_half):
    rot = jnp.where(first_half, pltpu.roll(t, LANES - HEAD_DIM // 2, 1), pltpu.roll(t, HEAD_DIM // 2, 1))
    return t * cos + rot * sin_signed


def _inproj_kernel(*refs, segs, rope):
    x_ref, sh_ref, sc_ref, g_ref, w_ref, wdt_ref = refs[:6]
    pos = 6
    if rope:
        cos_ref, sin_ref = refs[6:8]
        pos = 8
    outs = refs[pos:]
    x = x_ref[...]
    ms = jnp.mean(x * x, axis=-1, keepdims=True)
    h = (x * lax.rsqrt(ms + EPS)) * (g_ref[...] * (1.0 + sc_ref[0])) + sh_ref[0]
    hb = h.astype(BF16)
    if rope:
        cos = cos_ref[...]
        lane = lax.broadcasted_iota(jnp.int32, cos.shape, 1)
        first_half = (lane & (HEAD_DIM - 1)) < HEAD_DIM // 2
        sin_s = jnp.where(first_half, -sin_ref[...], sin_ref[...])
    for (kind, c0, width), o_ref in zip(segs, outs):
        if kind == "dtT":
            dtt = lax.dot_general(wdt_ref[...], hb, (((1,), (1,)), ((), ())), preferred_element_type=F32)
            for j in range(o_ref.shape[0]):
                o_ref[j] = dtt[:, j * CHUNK:(j + 1) * CHUNK]
            continue
        for j0 in range(0, width, 512):
            wj = min(512, width - j0)
            acc = jnp.dot(hb, w_ref[:, c0 + j0:c0 + j0 + wj], preferred_element_type=F32)
            if rope and kind in ("q", "k"):
                scale = HEAD_DIM ** -0.5 if kind == "q" else 1.0
                for l0 in range(0, wj, LANES):
                    t = _rope(acc[:, l0:l0 + LANES], cos, sin_s, first_half)
                    o_ref[:, j0 + l0:j0 + l0 + LANES] = (t * scale).astype(o_ref.dtype)
            else:
                o_ref[:, j0:j0 + wj] = acc.astype(o_ref.dtype)


_COL_Z, _COL_XBC, _COL_Q, _COL_K, _COL_V = 0, 512, 1536, 2048, 2176
_W_COLS = 2304


def _inproj(xf, mod3, norm1, w_cat, w_dtT, rope_tabs, *, rows, mod_row_fn, segs, seq_blocks):
    t, d = xf.shape
    rope = rope_tabs is not None
    in_specs = [
        pl.BlockSpec((rows, d), lambda i: (i, 0)),
        pl.BlockSpec((1, 1, d), lambda i: (mod_row_fn(i), 0, 0)),
        pl.BlockSpec((1, 1, d), lambda i: (mod_row_fn(i), 0, 1)),
        pl.BlockSpec((1, d), lambda i: (0, 0)),
        pl.BlockSpec(w_cat.shape, lambda i: (0, 0)),
        pl.BlockSpec(w_dtT.shape, lambda i: (0, 0)),
    ]
    args = [xf, mod3, mod3, norm1, w_cat, w_dtT]
    if rope:
        in_specs += [pl.BlockSpec((rows, LANES), lambda i: (i % seq_blocks, 0))] * 2
        args += list(rope_tabs)
    out_specs, out_shapes = [], []
    for kind, _, width in segs:
        if kind == "dtT":
            out_specs.append(pl.BlockSpec((rows // CHUNK, 2 * SSD_HEADS, CHUNK), lambda i: (i, 0, 0)))
            out_shapes.append(jax.ShapeDtypeStruct((t // CHUNK, 2 * SSD_HEADS, CHUNK), F32))
        else:
            dt = BF16 if kind in ("q", "k", "v") else F32
            out_specs.append(pl.BlockSpec((rows, width), lambda i: (i, 0)))
            out_shapes.append(jax.ShapeDtypeStruct((t, width), dt))
    return pl.pallas_call(
        functools.partial(_inproj_kernel, segs=tuple(segs), rope=rope),
        grid=(t // rows,),
        in_specs=in_specs,
        out_specs=out_specs,
        out_shape=out_shapes,
        name="inproj_rope" if rope else "inproj_ctx",
    )(*args)


def _lane_cumsum(v, lane, reverse):
    k = 1
    while k < LANES:
        if reverse:
            v = v + jnp.where(lane < LANES - k, pltpu.roll(v, LANES - k, 1), 0.0)
        else:
            v = v + jnp.where(lane >= k, pltpu.roll(v, k, 1), 0.0)
        k *= 2
    return v


def _ssd_kernel(*refs, need_y, nsteps, cps):
    (xbc_ref, prev_ref, next_ref, dtr_ref, z_ref, cw_ref, cb_ref, bias_ref, alog_ref,
     dskip_ref, gain_ref, exp_ref, h0f_ref, h0b_ref) = refs[:14]
    if need_y:
        y_ref, hf_out, hb_out = refs[14:17]
        rest = refs[17:]
    else:
        y_ref = None
        hf_out, hb_out = refs[14:16]
        rest = refs[16:]
    win, cwin, u_ref, hf, hb, hbs = rest

    sw = pl.program_id(1)
    s = pl.program_id(2)
    bi = jnp.where(sw == 0, nsteps - 1 - s, s)
    rows = xbc_ref.shape[1]
    h = SSD_HEADS

    @pl.when((sw == 0) & (s == 0))
    def _():
        hb[...] = h0b_ref[0]

    @pl.when((sw == 1) & (s == 0))
    def _():
        hf[...] = h0f_ref[0]

    win[0:SUBLANES, :] = jnp.where(bi > 0, prev_ref[0], 0.0)
    win[SUBLANES:SUBLANES + rows, :] = xbc_ref[0]
    win[SUBLANES + rows:, :] = jnp.where(bi < nsteps - 1, next_ref[0], 0.0)

    lane = lax.broadcasted_iota(jnp.int32, (2 * h, LANES), 1)
    ri = lax.broadcasted_iota(jnp.int32, (CHUNK, CHUNK), 0)
    ci = lax.broadcasted_iota(jnp.int32, (CHUNK, CHUNK), 1)
    lane128 = lax.broadcasted_iota(jnp.int32, (CHUNK, LANES), 1)
    lo_mask = lane128 < SSD_HEAD_DIM
    neg_a = -jnp.exp(alog_ref[...])

    def conv_silu(c, ncols):
        r0 = pl.multiple_of(c * CHUNK, CHUNK)
        cwin[...] = win[pl.ds(r0, CHUNK + 2 * SUBLANES), :]
        for j0 in range(0, ncols, LANES):
            acc = jnp.broadcast_to(cb_ref[:, j0:j0 + LANES], (CHUNK, LANES))
            for k in range(CONV_WIDTH):
                off = SUBLANES - CONV_WIDTH // 2 + k
                acc = acc + cwin[off:off + CHUNK, j0:j0 + LANES] * cw_ref[k:k + 1, j0:j0 + LANES]
            u_ref[:, j0:j0 + LANES] = _silu(acc)

    def dt_rows(c):
        dt = _softplus(dtr_ref[c] + bias_ref[...])
        a = dt * neg_a
        pre = _lane_cumsum(a, lane, False)
        suf = _lane_cumsum(a, lane, True)
        return dt, pre, suf

    def state_update(state_ref, w_rows, tot_col):
        dec = jnp.exp(jnp.sum(tot_col * exp_ref[...], axis=0, keepdims=True))
        for g in range(SSD_GROUPS):
            bm = u_ref[:, SSD_INNER + g * SSD_STATE:SSD_INNER + (g + 1) * SSD_STATE]
            bt = bm.T
            for pr in range(2):
                h0 = g * 4 + pr * 2
                c0 = h0 * SSD_HEAD_DIM
                xp = u_ref[:, c0:c0 + LANES]
                rhs = jnp.concatenate([jnp.where(lo_mask, xp, 0.0), jnp.where(lo_mask, 0.0, xp)],
                                      axis=0).astype(BF16)
                lhs = jnp.concatenate([bt * w_rows[h0:h0 + 1, :], bt * w_rows[h0 + 1:h0 + 2, :]],
                                      axis=1).astype(BF16)
                sres = jnp.dot(lhs, rhs, preferred_element_type=F32)
                state_ref[:, c0:c0 + LANES] = state_ref[:, c0:c0 + LANES] * dec[:, c0:c0 + LANES] + sres

    def col_forms(rows16):
        padded = jnp.concatenate([rows16, jnp.zeros((CHUNK - 2 * h, LANES), F32)], axis=0)
        return padded.T

    def backward_chunk(k, carry):
        c = cps - 1 - k
        gc = bi * cps + c
        conv_silu(c, SSD_INNER + SSD_GROUPS * SSD_STATE)
        dt, _, suf = dt_rows(c)
        hbs[gc] = hb[...].astype(BF16)
        s0 = suf[:, 0:1]
        w_rows = jnp.exp(s0 - suf) * dt
        state_update(hb, w_rows[h:2 * h, :], s0[h:2 * h, :])
        return carry

    def forward_chunk(c, carry):
        gc = bi * cps + c
        r0 = pl.multiple_of(c * CHUNK, CHUNK)
        conv_silu(c, CONV_DIM)
        dt, pre, suf = dt_rows(c)
        rowsf = jnp.concatenate([pre[0:h, :], suf[h:2 * h, :]], axis=0)
        colsf = col_forms(rowsf)
        ecol = jnp.exp(colsf)
        for g in range(SSD_GROUPS):
            bm = u_ref[:, SSD_INNER + g * SSD_STATE:SSD_INNER + (g + 1) * SSD_STATE]
            cm = u_ref[:, SSD_INNER + (SSD_GROUPS + g) * SSD_STATE:SSD_INNER + (SSD_GROUPS + g + 1) * SSD_STATE]
            cmb = cm.astype(BF16)
            cbm = lax.dot_general(cmb, bm.astype(BF16), (((1,), (1,)), ((), ())),
                                  preferred_element_type=F32)
            for pr in range(2):
                h0 = g * 4 + pr * 2
                c0 = h0 * SSD_HEAD_DIM
                lhs_parts = []
                for hh in (h0, h0 + 1):
                    segf = colsf[:, hh:hh + 1] - rowsf[hh:hh + 1, :]
                    gf = jnp.exp(jnp.where(ci <= ri, segf, NEG)) * dt[hh:hh + 1, :]
                    segb = colsf[:, h + hh:h + hh + 1] - rowsf[h + hh:h + hh + 1, :]
                    gb = jnp.exp(jnp.where(ci >= ri, segb, NEG)) * dt[h + hh:h + hh + 1, :]
                    lhs_parts.append((cbm * (gf + gb)).astype(BF16))
                for hh in (h0, h0 + 1):
                    lhs_parts.append((cm * ecol[:, hh:hh + 1]).astype(BF16))
                for hh in (h0, h0 + 1):
                    lhs_parts.append((cm * ecol[:, h + hh:h + hh + 1]).astype(BF16))
                lhs = jnp.concatenate(lhs_parts, axis=1)
                xp = u_ref[:, c0:c0 + LANES]
                sf = hf[:, c0:c0 + LANES]
                sb = hbs[gc, :, c0:c0 + LANES].astype(F32)
                rhs = jnp.concatenate(
                    [jnp.where(lo_mask, xp, 0.0), jnp.where(lo_mask, 0.0, xp),
                     jnp.where(lo_mask, sf, 0.0), jnp.where(lo_mask, 0.0, sf),
                     jnp.where(lo_mask, sb, 0.0), jnp.where(lo_mask, 0.0, sb)], axis=0).astype(BF16)
                ypair = jnp.dot(lhs, rhs, preferred_element_type=F32)
                ypair = ypair + dskip_ref[:, c0:c0 + LANES] * xp
                zz = z_ref[0, pl.ds(r0, CHUNK), c0:c0 + LANES]
                u_ref[:, CONV_DIM + c0:CONV_DIM + c0 + LANES] = ypair * _silu(zz)
        last = pre[:, LANES - 1:LANES]
        w_rows = jnp.exp(last - pre) * dt
        state_update(hf, w_rows[0:h, :], last[0:h, :])
        if need_y:
            yv = u_ref[:, CONV_DIM:CONV_DIM + SSD_INNER]
            ms = jnp.mean(yv * yv, axis=-1, keepdims=True)
            y_ref[0, pl.ds(r0, CHUNK), :] = (yv * lax.rsqrt(ms + EPS) * gain_ref[...]).astype(y_ref.dtype)
        return carry

    @pl.when(sw == 0)
    def _():
        lax.fori_loop(0, cps, backward_chunk, 0)

    @pl.when(sw == 1)
    def _():
        lax.fori_loop(0, cps, forward_chunk, 0)

    @pl.when((sw == 0) & (s == nsteps - 1))
    def _():
        hb_out[0] = hb[...]

    @pl.when((sw == 1) & (s == nsteps - 1))
    def _():
        hf_out[0] = hf[...]


def _ssd(xbc, dtr, z, consts, h0f, h0b, *, rows, need_y):
    bt, seq, _ = xbc.shape
    nsteps = seq // rows
    cps = rows // CHUNK
    hb8 = rows // SUBLANES
    nb8 = seq // SUBLANES
    cw, cb, bias, alog, dskip, gain, expm = consts

    def blk(b, sw, s):
        return jnp.where(sw == 0, nsteps - 1 - s, s)

    def full(a):
        return pl.BlockSpec(a.shape, lambda b, sw, s: (0,) * a.ndim)

    in_specs = [
        pl.BlockSpec((1, rows, CONV_DIM), lambda b, sw, s: (b, blk(b, sw, s), 0)),
        pl.BlockSpec((1, SUBLANES, CONV_DIM), lambda b, sw, s: (b, jnp.maximum(blk(b, sw, s) * hb8 - 1, 0), 0)),
        pl.BlockSpec((1, SUBLANES, CONV_DIM),
                     lambda b, sw, s: (b, jnp.minimum((blk(b, sw, s) + 1) * hb8, nb8 - 1), 0)),
        pl.BlockSpec((cps, 2 * SSD_HEADS, CHUNK), lambda b, sw, s: (b * nsteps + blk(b, sw, s), 0, 0)),
        pl.BlockSpec((1, rows, SSD_INNER), lambda b, sw, s: (b, jnp.where(sw == 0, 0, s), 0)),
        full(cw), full(cb), full(bias), full(alog), full(dskip), full(gain), full(expm),
        pl.BlockSpec((1, SSD_STATE, SSD_INNER), lambda b, sw, s: (b, 0, 0)),
        pl.BlockSpec((1, SSD_STATE, SSD_INNER), lambda b, sw, s: (b, 0, 0)),
    ]
    st_spec = pl.BlockSpec((1, SSD_STATE, SSD_INNER), lambda b, sw, s: (b, 0, 0))
    st_shape = jax.ShapeDtypeStruct((bt, SSD_STATE, SSD_INNER), F32)
    out_specs, out_shapes = [st_spec, st_spec], [st_shape, st_shape]
    if need_y:
        out_specs = [pl.BlockSpec((1, rows, SSD_INNER), lambda b, sw, s: (b, jnp.where(sw == 0, 0, s), 0))] + out_specs
        out_shapes = [jax.ShapeDtypeStruct((bt, seq, SSD_INNER), BF16)] + out_shapes
    scratch = [
        pltpu.VMEM((rows + 2 * SUBLANES, CONV_DIM), F32),
        pltpu.VMEM((CHUNK + 2 * SUBLANES, CONV_DIM), F32),
        pltpu.VMEM((CHUNK, CONV_DIM + SSD_INNER), F32),
        pltpu.VMEM((SSD_STATE, SSD_INNER), F32),
        pltpu.VMEM((SSD_STATE, SSD_INNER), F32),
        pltpu.VMEM((seq // CHUNK, SSD_STATE, SSD_INNER), BF16),
    ]
    return pl.pallas_call(
        functools.partial(_ssd_kernel, need_y=need_y, nsteps=nsteps, cps=cps),
        grid=(bt, 2, nsteps),
        in_specs=in_specs,
        out_specs=out_specs,
        out_shape=out_shapes,
        scratch_shapes=scratch,
        compiler_params=pltpu.CompilerParams(dimension_semantics=("arbitrary", "arbitrary", "arbitrary")),
        name="ssd_y" if need_y else "ssd_ctx",
    )(xbc, xbc, xbc, dtr, z, cw, cb, bias, alog, dskip, gain, expm, h0f, h0b)


def _attn_kernel(sink_ref, q_ref, kp_ref, kc_ref, kn_ref, vp_ref, vc_ref, vn_ref, kx_ref, vx_ref, o_ref, *, nsteps):
    m = pl.program_id(1)
    lane = lax.broadcasted_iota(jnp.int32, (1, LANES), 1)
    lo = lane < HEAD_DIM
    rq = Q_PER_KV * WINDOW
    qi = lax.broadcasted_iota(jnp.int32, (rq, WINDOW), 0) & (WINDOW - 1)
    kj = lax.broadcasted_iota(jnp.int32, (rq, WINDOW), 1)
    rblk = lax.shift_right_logical(lax.broadcasted_iota(jnp.int32, (rq, 1), 0), int(math.log2(WINDOW)))
    zero = jnp.zeros((), BF16)

    kpieces = [kp_ref[0], kc_ref[0, 0:WINDOW], kc_ref[0, WINDOW:2 * WINDOW], kn_ref[0]]
    vpieces = [vp_ref[0], vc_ref[0, 0:WINDOW], vc_ref[0, WINDOW:2 * WINDOW], vn_ref[0]]
    kx = kx_ref[0]
    vx = vx_ref[0]
    nt = (((1,), (1,)), ((), ()))

    for sb in range(ROWS_ATT // WINDOW):
        qsb = q_ref[0, sb * WINDOW:(sb + 1) * WINDOW, :]
        prev_ok = jnp.logical_or(m > 0, sb > 0)
        next_ok = jnp.logical_or(m < nsteps - 1, sb < ROWS_ATT // WINDOW - 1)
        outs = [None] * Q_PER_KV
        for kv in range(ATTN_KV_HEADS):
            sel = lo if kv == 0 else jnp.logical_not(lo)
            qs = jnp.concatenate(
                [jnp.where(sel, qsb[:, j * LANES:(j + 1) * LANES], zero) for j in range(Q_PER_KV)], axis=0)
            s_prev = lax.dot_general(qs, kpieces[sb], nt, preferred_element_type=F32)
            s_cur = lax.dot_general(qs, kpieces[sb + 1], nt, preferred_element_type=F32)
            s_next = lax.dot_general(qs, kpieces[sb + 2], nt, preferred_element_type=F32)
            s_ctx = lax.dot_general(qs, kx, nt, preferred_element_type=F32)
            s_prev = jnp.where((kj >= qi) & prev_ok, s_prev, NEG)
            s_next = jnp.where((kj <= qi) & next_ok, s_next, NEG)
            sink = jnp.zeros((rq, 1), F32)
            for j in range(Q_PER_KV):
                sink = jnp.where(rblk == j, sink_ref[kv * Q_PER_KV + j], sink)
            mx = jnp.maximum(
                jnp.maximum(jnp.max(s_prev, axis=-1, keepdims=True), jnp.max(s_cur, axis=-1, keepdims=True)),
                jnp.maximum(jnp.max(s_next, axis=-1, keepdims=True), jnp.max(s_ctx, axis=-1, keepdims=True)))
            mx = jnp.maximum(mx, sink)
            p_prev = jnp.exp(s_prev - mx)
            p_cur = jnp.exp(s_cur - mx)
            p_next = jnp.exp(s_next - mx)
            p_ctx = jnp.exp(s_ctx - mx)
            den = (jnp.sum(p_prev, axis=-1, keepdims=True) + jnp.sum(p_cur, axis=-1, keepdims=True)
                   + jnp.sum(p_next, axis=-1, keepdims=True) + jnp.sum(p_ctx, axis=-1, keepdims=True)
                   + jnp.exp(sink - mx))
            acc = jnp.dot(p_prev.astype(BF16), jnp.where(sel, vpieces[sb], zero), preferred_element_type=F32)
            acc += jnp.dot(p_cur.astype(BF16), jnp.where(sel, vpieces[sb + 1], zero), preferred_element_type=F32)
            acc += jnp.dot(p_next.astype(BF16), jnp.where(sel, vpieces[sb + 2], zero), preferred_element_type=F32)
            acc += jnp.dot(p_ctx.astype(BF16), jnp.where(sel, vx, zero), preferred_element_type=F32)
            acc = acc * (1.0 / den)
            for j in range(Q_PER_KV):
                part = acc[j * WINDOW:(j + 1) * WINDOW, :]
                outs[j] = part if outs[j] is None else outs[j] + part
        for j in range(Q_PER_KV):
            o_ref[0, sb * WINDOW:(sb + 1) * WINDOW, j * LANES:(j + 1) * LANES] = outs[j].astype(o_ref.dtype)


def _attention(sinks, q, k, v, kx, vx):
    bt, seq, _ = q.shape
    nsteps = seq // ROWS_ATT
    per = ROWS_ATT // WINDOW
    nb = seq // WINDOW
    n_ctx = kx.shape[1]

    def prev(b, m, s):
        return (b, jnp.maximum(m * per - 1, 0), 0)

    def cur(b, m, s):
        return (b, m, 0)

    def nxt(b, m, s):
        return (b, jnp.minimum((m + 1) * per, nb - 1), 0)

    small = (1, WINDOW, KV_WIDTH)
    big = (1, ROWS_ATT, KV_WIDTH)
    grid_spec = pltpu.PrefetchScalarGridSpec(
        num_scalar_prefetch=1,
        grid=(bt, nsteps),
        in_specs=[
            pl.BlockSpec((1, ROWS_ATT, ATTN_WIDTH), cur),
            pl.BlockSpec(small, prev), pl.BlockSpec(big, cur), pl.BlockSpec(small, nxt),
            pl.BlockSpec(small, prev), pl.BlockSpec(big, cur), pl.BlockSpec(small, nxt),
            pl.BlockSpec((1, n_ctx, KV_WIDTH), lambda b, m, s: (b, 0, 0)),
            pl.BlockSpec((1, n_ctx, KV_WIDTH), lambda b, m, s: (b, 0, 0)),
        ],
        out_specs=pl.BlockSpec((1, ROWS_ATT, ATTN_WIDTH), cur),
    )
    return pl.pallas_call(
        functools.partial(_attn_kernel, nsteps=nsteps),
        grid_spec=grid_spec,
        out_shape=jax.ShapeDtypeStruct((bt, seq, ATTN_WIDTH), BF16),
        name="attn",
    )(sinks, q, k, k, k, v, v, v, kx, vx)


def _outproj_kernel(ssd_ref, att_ref, x_ref, g1_ref, sh_ref, sc_ref, n2_ref, wa_ref, wb_ref, wr_ref, br_ref,
                    tri_ref, x1_ref, h2_ref, ids_ref, wcol_ref, cnt_ref, carry, *, nsteps):
    i = pl.program_id(0)
    rows = x_ref.shape[0]

    @pl.when(i == 0)
    def _():
        carry[...] = jnp.zeros_like(carry)

    acc = jnp.dot(ssd_ref[...], wa_ref[...], preferred_element_type=F32)
    acc += jnp.dot(att_ref[...], wb_ref[...], preferred_element_type=F32)
    x1 = x_ref[...] + g1_ref[0] * acc
    x1_ref[...] = x1
    ms = jnp.mean(x1 * x1, axis=-1, keepdims=True)
    h2 = (x1 * lax.rsqrt(ms + EPS)) * (n2_ref[...] * (1.0 + sc_ref[0])) + sh_ref[0]
    for sidx in range(d_tiles := h2.shape[1] // LANES):
        h2_ref[pl.ds(sidx, rows, stride=d_tiles), :] = h2[:, sidx * LANES:(sidx + 1) * LANES]
    lt = lax.dot_general(wr_ref[...], h2.astype(BF16), (((1,), (1,)), ((), ())),
                         preferred_element_type=F32) + br_ref[...]
    row = lax.broadcasted_iota(jnp.int32, lt.shape, 0).astype(F32)
    big_i = float(ROUTE_ROWS)
    gl = jnp.where(row < N_GROUPS, lt, NEG)
    gmax = jnp.max(gl, axis=0, keepdims=True)
    gidx = jnp.min(jnp.where(gl == gmax, row, big_i), axis=0, keepdims=True)
    g_w = 1.0 / jnp.sum(jnp.exp(gl - gmax), axis=0, keepdims=True)
    lo = N_GROUPS + EXPERTS_PER_GROUP * gidx
    el = jnp.where((row >= lo) & (row < lo + EXPERTS_PER_GROUP), lt, NEG)
    m1 = jnp.max(el, axis=0, keepdims=True)
    i1 = jnp.min(jnp.where(el == m1, row, big_i), axis=0, keepdims=True)
    el2 = jnp.where(row == i1, NEG, el)
    m2 = jnp.max(el2, axis=0, keepdims=True)
    i2 = jnp.min(jnp.where(el2 == m2, row, big_i), axis=0, keepdims=True)
    r = jnp.exp(m2 - m1)
    w1 = g_w / (1.0 + r)
    w2 = g_w * r / (1.0 + r)
    oh1 = row == i1
    oh2 = row == i2
    cnt = jnp.where(oh1 | oh2, 1.0, 0.0)
    prefix = jnp.dot(cnt.astype(BF16), tri_ref[...], preferred_element_type=F32)
    base = carry[...] + prefix
    rank1 = jnp.sum(jnp.where(oh1, base, 0.0), axis=0, keepdims=True)
    rank2 = jnp.sum(jnp.where(oh2, base, 0.0), axis=0, keepdims=True)
    carry[...] = carry[...] + jnp.sum(cnt, axis=1, keepdims=True)
    r8 = lax.broadcasted_iota(jnp.int32, (SUBLANES, rows), 0)
    code1 = (i1 - N_GROUPS) * float(1 << RANK_BITS) + rank1
    code2 = (i2 - N_GROUPS) * float(1 << RANK_BITS) + rank2
    ids_ref[...] = jnp.where(r8 == 0, code1, jnp.where(r8 == 1, code2, 0.0)).astype(jnp.int32)
    r128 = lax.broadcasted_iota(jnp.int32, (LANES, rows), 0)
    wfull = jnp.where(r128 == 0, w1, jnp.where(r128 == 1, w2, 0.0))
    for j in range(rows // LANES):
        wcol_ref[j * LANES:(j + 1) * LANES, :] = wfull[:, j * LANES:(j + 1) * LANES].T

    @pl.when(i == nsteps - 1)
    def _():
        cnt_ref[...] = carry[:, 0:LANES]


def _outproj(ssd, att, xf, mod3, norm2, wa, wb, wr, br, tri, *, rows, seq_blocks):
    t, d = xf.shape
    nsteps = t // rows

    def mod(col):
        return pl.BlockSpec((1, 1, d), lambda i: (i // seq_blocks, 0, col))

    def full(a):
        return pl.BlockSpec(a.shape, lambda i: (0,) * a.ndim)

    return pl.pallas_call(
        functools.partial(_outproj_kernel, nsteps=nsteps),
        grid=(nsteps,),
        in_specs=[pl.BlockSpec((rows, SSD_INNER), lambda i: (i, 0)),
                  pl.BlockSpec((rows, ATTN_WIDTH), lambda i: (i, 0)),
                  pl.BlockSpec((rows, d), lambda i: (i, 0)),
                  mod(2), mod(3), mod(4), full(norm2), full(wa), full(wb), full(wr), full(br), full(tri)],
        out_specs=[pl.BlockSpec((rows, d), lambda i: (i, 0)),
                   pl.BlockSpec((rows * (d // LANES), LANES), lambda i: (i, 0)),
                   pl.BlockSpec((SUBLANES, rows), lambda i: (0, i)),
                   pl.BlockSpec((rows, LANES), lambda i: (i, 0)),
                   pl.BlockSpec((ROUTE_ROWS, LANES), lambda i: (0, 0))],
        out_shape=[jax.ShapeDtypeStruct((t, d), F32),
                   jax.ShapeDtypeStruct((t * (d // LANES), LANES), F32),
                   jax.ShapeDtypeStruct((SUBLANES, t), jnp.int32),
                   jax.ShapeDtypeStruct((t, LANES), F32),
                   jax.ShapeDtypeStruct((ROUTE_ROWS, LANES), F32)],
        scratch_shapes=[pltpu.VMEM((ROUTE_ROWS, rows), F32)],
        compiler_params=pltpu.CompilerParams(dimension_semantics=("arbitrary",)),
        name="outproj_router",
    )(ssd, att, xf, mod3, mod3, mod3, norm2, wa, wb, wr, br, tri)


def _expert_kernel(blk_e_ref, nused_ref, pstart_ref, counts_ref, dst_ref, h2_hbm, wg_ref, wu_ref, wd_ref, y_ref,
                   xbuf, wgb, wub, wdb, tok_ref, sem, *, n_tok, nblocks):
    b = pl.program_id(0)
    nused = nused_ref[0]
    slot = b % 2

    def row_copy(tok, r, sl):
        return pltpu.make_async_copy(h2_hbm.at[pl.ds(tok * D_TILES, D_TILES)],
                                     xbuf.at[sl, pl.ds(r * D_TILES, D_TILES)], sem.at[sl])

    def start_rows(blk, sl, r0, r1):
        for r in range(r0, r1):
            row_copy(tok_ref[blk * MOE_BLK + r], r, sl).start()

    def wait_rows(sl):
        for r in range(MOE_BLK):
            row_copy(0, r, sl).wait()

    @pl.when(b == 0)
    def _():
        def pad_expert(e, carry):
            lo = pstart_ref[e] + counts_ref[e]
            hi = pstart_ref[e] + (counts_ref[e] + MOE_BLK - 1) // MOE_BLK * MOE_BLK

            def pad_slot(s, c):
                tok_ref[s] = 0
                return c
            return lax.fori_loop(lo, hi, pad_slot, carry)
        lax.fori_loop(0, N_EXPERTS, pad_expert, 0)

        def claim(i, carry):
            tok_ref[dst_ref[i]] = jnp.where(i >= n_tok, i - n_tok, i)
            return carry
        lax.fori_loop(0, 2 * n_tok, claim, 0, unroll=16)
        start_rows(0, 0, 0, MOE_BLK)

    e = blk_e_ref[b]
    e_prev = blk_e_ref[jnp.maximum(b - 1, 0)]

    @pl.when((b == 0) | (e != e_prev))
    def _():
        wgb[...] = wg_ref[0].astype(BF16)
        wub[...] = wu_ref[0].astype(BF16)
        wdb[...] = wd_ref[0].astype(BF16)

    nxt = jnp.minimum(b + 1, nused - 1)
    wait_rows(slot)
    xb = jnp.concatenate([xbuf[slot, pl.ds(s, MOE_BLK, stride=D_TILES), :] for s in range(D_TILES)],
                         axis=1).astype(BF16)
    gate = jnp.dot(xb, wgb[...], preferred_element_type=F32)
    up = jnp.dot(xb, wub[...], preferred_element_type=F32)
    hmid = (_silu(gate) * up).astype(BF16)
    y = jnp.dot(hmid, wdb[...], preferred_element_type=F32)
    start_rows(nxt, 1 - slot, 0, MOE_BLK)
    for s in range(D_TILES):
        y_ref[pl.ds(s, MOE_BLK, stride=D_TILES), :] = y[:, s * LANES:(s + 1) * LANES]

    @pl.when(b == nblocks - 1)
    def _():
        wait_rows(1 - slot)


def _experts(blk_e, nused, pstart, counts, dst, h2t, w_gate, w_up, w_down, *, nblocks, n_tok):
    d = w_gate.shape[1]

    def wspec(shape):
        return pl.BlockSpec(shape, lambda b, be, nu, ps, ct, cd: (be[b], 0, 0))

    grid_spec = pltpu.PrefetchScalarGridSpec(
        num_scalar_prefetch=5,
        grid=(nblocks,),
        in_specs=[pl.BlockSpec(memory_space=pl.ANY),
                  wspec((1, d, EXPERT_DIM)), wspec((1, d, EXPERT_DIM)), wspec((1, EXPERT_DIM, d))],
        out_specs=pl.BlockSpec((MOE_BLK * D_TILES, LANES), lambda b, be, nu, ps, ct, cd: (b, 0)),
        scratch_shapes=[
            pltpu.VMEM((2, MOE_BLK * D_TILES, LANES), F32),
            pltpu.VMEM((d, EXPERT_DIM), BF16),
            pltpu.VMEM((d, EXPERT_DIM), BF16),
            pltpu.VMEM((EXPERT_DIM, d), BF16),
            pltpu.SMEM((nblocks * MOE_BLK,), jnp.int32),
            pltpu.SemaphoreType.DMA((2,)),
        ],
    )
    return pl.pallas_call(
        functools.partial(_expert_kernel, n_tok=n_tok, nblocks=nblocks),
        grid_spec=grid_spec,
        out_shape=jax.ShapeDtypeStruct((nblocks * MOE_BLK * D_TILES, LANES), F32),
        compiler_params=pltpu.CompilerParams(dimension_semantics=("arbitrary",)),
        name="experts",
    )(blk_e, nused, pstart, counts, dst, h2t, w_gate, w_up, w_down)


def _combine_kernel(dst_ref, y_hbm, x1_ref, wcol_ref, g2_ref, nf_ref, o_ref, ybuf, sem, *, nsteps, n_tok):
    i = pl.program_id(0)
    rows = x1_ref.shape[0]
    slot = i % 2

    def row_copy(src, r, k, sl):
        return pltpu.make_async_copy(y_hbm.at[pl.ds(src * D_TILES, D_TILES)],
                                     ybuf.at[sl, k, pl.ds(r * D_TILES, D_TILES)], sem.at[sl])

    def start_rows(step, sl, r0, r1):
        for r in range(r0, r1):
            for k in range(2):
                row_copy(dst_ref[k * n_tok + step * rows + r], r, k, sl).start()

    def wait_rows(sl):
        for r in range(rows):
            for k in range(2):
                row_copy(0, r, k, sl).wait()

    @pl.when(i == 0)
    def _():
        start_rows(0, 0, 0, rows)

    @pl.when(i + 1 < nsteps)
    def _():
        start_rows(i + 1, 1 - slot, 0, rows)

    wait_rows(slot)
    w0 = wcol_ref[:, 0:1]
    w1 = wcol_ref[:, 1:2]
    ssq = jnp.zeros((rows, 1), F32)
    for s in range(D_TILES):
        cols = slice(s * LANES, (s + 1) * LANES)
        moe = (ybuf[slot, 0, pl.ds(s, rows, stride=D_TILES), :] * w0
               + ybuf[slot, 1, pl.ds(s, rows, stride=D_TILES), :] * w1)
        x2 = x1_ref[:, cols] + g2_ref[0, :, cols] * moe
        ssq = ssq + jnp.sum(x2 * x2, axis=-1, keepdims=True)
        o_ref[:, cols] = x2
    inv = lax.rsqrt(ssq * (1.0 / (D_TILES * LANES)) + EPS)
    o_ref[...] = o_ref[...] * inv * nf_ref[...]


def _combine(dst, y_buf, x1, wcol, mod3, norm_final, *, rows, seq_blocks):
    t, d = x1.shape
    nsteps = t // rows
    grid_spec = pltpu.PrefetchScalarGridSpec(
        num_scalar_prefetch=1,
        grid=(nsteps,),
        in_specs=[
            pl.BlockSpec(memory_space=pl.ANY),
            pl.BlockSpec((rows, d), lambda i, ds: (i, 0)),
            pl.BlockSpec((rows, LANES), lambda i, ds: (i, 0)),
            pl.BlockSpec((1, 1, d), lambda i, ds: (i // seq_blocks, 0, 5)),
            pl.BlockSpec((1, d), lambda i, ds: (0, 0)),
        ],
        out_specs=pl.BlockSpec((rows, d), lambda i, ds: (i, 0)),
        scratch_shapes=[pltpu.VMEM((2, 2, rows * D_TILES, LANES), F32), pltpu.SemaphoreType.DMA((2,))],
    )
    return pl.pallas_call(
        functools.partial(_combine_kernel, nsteps=nsteps, n_tok=t),
        grid_spec=grid_spec,
        out_shape=jax.ShapeDtypeStruct((t, d), F32),
        compiler_params=pltpu.CompilerParams(dimension_semantics=("arbitrary",)),
        name="combine",
    )(dst, y_buf, x1, wcol, mod3, norm_final)


def _rope_tables(seq):
    pos = np.arange(seq)
    n_freq = HEAD_DIM // 4
    inv = ROPE_BASE ** (-np.arange(n_freq, dtype=np.float32) / n_freq)
    ang = np.concatenate([(pos // GRID_W)[:, None] * inv, (pos % GRID_W)[:, None] * inv], axis=-1)
    ang = np.concatenate([ang, ang, ang, ang], axis=-1).astype(np.float32)
    return jnp.asarray(np.cos(ang), F32), jnp.asarray(np.sin(ang), F32)


def _head_expand_mat():
    m = np.zeros((SSD_HEADS, SSD_INNER), np.float32)
    for hh in range(SSD_HEADS):
        m[hh, hh * SSD_HEAD_DIM:(hh + 1) * SSD_HEAD_DIM] = 1.0
    return jnp.asarray(m)


def kernel(x, c, ctx, c_ctx, w_ada, b_ada, norm1, w_in, conv_w, conv_b, dt_bias, a_log, d_skip, ssd_norm,
           attn_sinks, w_out, norm2, w_group, b_group, w_expert, b_expert, w_gate, w_up, w_down, norm_final):
    return _pipeline(x, c, ctx, c_ctx, w_ada, b_ada, norm1, w_in, conv_w, conv_b, dt_bias, a_log, d_skip,
                     ssd_norm, attn_sinks, w_out, norm2, w_group, b_group, w_expert, b_expert, w_gate, w_up,
                     w_down, norm_final)["out"]


def _pipeline(x, c, ctx, c_ctx, w_ada, b_ada, norm1, w_in, conv_w, conv_b, dt_bias, a_log, d_skip, ssd_norm,
              attn_sinks, w_out, norm2, w_group, b_group, w_expert, b_expert, w_gate, w_up, w_down, norm_final):
    bt, seq, d = x.shape
    n_ctx = ctx.shape[1]
    t = bt * seq
    layer = 0
    assert w_ada.shape[0] == 1 and seq % ROWS_IN == 0 and n_ctx % CHUNK == 0

    cc = jnp.zeros((SUBLANES, d), F32).at[:bt].set(c).at[bt].set(c_ctx)
    mod = _ada(cc, w_ada[layer], b_ada[layer][None, :])
    mod3 = mod.reshape(SUBLANES, 1, 6 * d)

    w = w_in[layer]
    o_z, o_xbc, o_dt = 0, SSD_INNER, SSD_INNER + CONV_DIM
    o_q = o_dt + 2 * SSD_HEADS
    o_k, o_v = o_q + ATTN_WIDTH, o_q + ATTN_WIDTH + KV_WIDTH
    q_perm = np.concatenate([np.r_[j * HEAD_DIM:(j + 1) * HEAD_DIM, (j + 4) * HEAD_DIM:(j + 5) * HEAD_DIM]
                             for j in range(Q_PER_KV)])
    w_q = w[:, o_q:o_q + ATTN_WIDTH].reshape(d, ATTN_KV_HEADS, Q_PER_KV, HEAD_DIM).transpose(0, 2, 1, 3)
    w_q = w_q.reshape(d, ATTN_WIDTH)
    w_cat = jnp.concatenate([w[:, o_z:o_z + SSD_INNER], w[:, o_xbc:o_xbc + CONV_DIM], w_q,
                             w[:, o_k:o_k + KV_WIDTH], w[:, o_v:o_v + KV_WIDTH]], axis=1).astype(BF16)
    w_dtT = w[:, o_dt:o_dt + 2 * SSD_HEADS].T.astype(BF16)

    seq_blocks = seq // ROWS_IN
    cos, sin = _rope_tables(seq)
    segs_x = [("z", _COL_Z, SSD_INNER), ("xbc", _COL_XBC, CONV_DIM), ("dtT", 0, 0),
              ("q", _COL_Q, ATTN_WIDTH), ("k", _COL_K, KV_WIDTH), ("v", _COL_V, KV_WIDTH)]
    z_x, xbc_x, dtr_x, q_x, k_x, v_x = _inproj(
        x.reshape(t, d), mod3, norm1[layer][None, :], w_cat, w_dtT, (cos, sin),
        rows=ROWS_IN, mod_row_fn=lambda i: i // seq_blocks, segs=segs_x, seq_blocks=seq_blocks)
    segs_c = [("xbc", _COL_XBC, CONV_DIM), ("dtT", 0, 0), ("k", _COL_K, KV_WIDTH), ("v", _COL_V, KV_WIDTH)]
    xbc_c, dtr_c, k_c, v_c = _inproj(
        ctx.reshape(bt * n_ctx, d), mod3, norm1[layer][None, :], w_cat, w_dtT, None,
        rows=n_ctx, mod_row_fn=lambda i: bt, segs=segs_c, seq_blocks=1)

    cw = jnp.zeros((SUBLANES, CONV_DIM), F32).at[:CONV_WIDTH].set(conv_w[layer])
    cb = conv_b[layer][None, :]
    bias_rows = jnp.broadcast_to(dt_bias[layer].reshape(2 * SSD_HEADS, 1), (2 * SSD_HEADS, LANES))
    alog_rows = jnp.broadcast_to(a_log[layer].reshape(2 * SSD_HEADS, 1), (2 * SSD_HEADS, LANES))
    dskip_e = jnp.repeat(d_skip[layer], SSD_HEAD_DIM)[None, :]
    consts = (cw, cb, bias_rows, alog_rows, dskip_e, ssd_norm[layer][None, :], _head_expand_mat())

    zeros_state = jnp.zeros((bt, SSD_STATE, SSD_INNER), F32)
    z_dummy = jnp.zeros((bt, n_ctx, SSD_INNER), F32)
    h_cf, h_cb = _ssd(xbc_c.reshape(bt, n_ctx, CONV_DIM), dtr_c, z_dummy, consts,
                      zeros_state, zeros_state, rows=n_ctx, need_y=False)
    ssd_x, _, _ = _ssd(xbc_x.reshape(bt, seq, CONV_DIM), dtr_x, z_x.reshape(bt, seq, SSD_INNER),
                       consts, h_cf, h_cb, rows=ROWS_SSD, need_y=True)

    attn_x = _attention(attn_sinks[layer].astype(F32),
                        q_x.reshape(bt, seq, ATTN_WIDTH), k_x.reshape(bt, seq, KV_WIDTH),
                        v_x.reshape(bt, seq, KV_WIDTH), k_c.reshape(bt, n_ctx, KV_WIDTH),
                        v_c.reshape(bt, n_ctx, KV_WIDTH))

    wo = w_out[layer]
    wa = wo[:SSD_INNER].astype(BF16)
    wb = wo[SSD_INNER:].reshape(ATTN_KV_HEADS, Q_PER_KV, HEAD_DIM, d).transpose(1, 0, 2, 3)
    wb = wb.reshape(ATTN_WIDTH, d).astype(BF16)
    wr = jnp.zeros((ROUTE_ROWS, d), F32).at[:N_GROUPS].set(w_group[layer].T)
    wr = wr.at[N_GROUPS:N_GROUPS + N_EXPERTS].set(w_expert[layer].T).astype(BF16)
    br = jnp.zeros((ROUTE_ROWS,), F32).at[:N_GROUPS].set(b_group[layer])
    br = br.at[N_GROUPS:N_GROUPS + N_EXPERTS].set(b_expert[layer])
    br = jnp.broadcast_to(br[:, None], (ROUTE_ROWS, ROWS_IN))
    tri = jnp.asarray(np.triu(np.ones((ROWS_IN, ROWS_IN), np.float32), 1), BF16)
    x1, h2, ids, wcol, cnt = _outproj(
        ssd_x.reshape(t, SSD_INNER), attn_x.reshape(t, ATTN_WIDTH), x.reshape(t, d), mod3, norm2[layer][None, :],
        wa, wb, wr, br, tri, rows=ROWS_IN, seq_blocks=seq_blocks)

    counts = cnt[N_GROUPS:N_GROUPS + N_EXPERTS, 0].astype(jnp.int32)
    padded = (counts + MOE_BLK - 1) // MOE_BLK * MOE_BLK
    ends = jnp.cumsum(padded)
    pstart = ends - padded
    nblocks = (2 * t) // MOE_BLK + N_EXPERTS
    blk_start = jnp.arange(nblocks, dtype=jnp.int32) * MOE_BLK
    blk_e = jnp.minimum(jnp.sum((ends[None, :] <= blk_start[:, None]).astype(jnp.int32), axis=1), N_EXPERTS - 1)
    nused = (ends[-1] // MOE_BLK).astype(jnp.int32).reshape(1)
    code = ids[0:2].reshape(-1)
    e_id = lax.shift_right_logical(code, RANK_BITS)
    onehot = e_id[:, None] == jnp.arange(N_EXPERTS, dtype=jnp.int32)[None, :]
    dst = jnp.sum(jnp.where(onehot, pstart[None, :], 0), axis=1) + (code & ((1 << RANK_BITS) - 1))

    y_buf = _experts(blk_e, nused, pstart, counts, dst, h2, w_gate[layer], w_up[layer], w_down[layer],
                     nblocks=nblocks, n_tok=t)
    out = _combine(dst, y_buf, x1, wcol, mod3, norm_final[None, :], rows=ROWS_CMB, seq_blocks=seq // ROWS_CMB)
    return dict(out=out.reshape(bt, seq, d), mod=mod, z=z_x, xbc=xbc_x, dtr=dtr_x, q=q_x, k=k_x, v=v_x,
                k_c=k_c, v_c=v_c, h_cf=h_cf, h_cb=h_cb, ssd=ssd_x, attn=attn_x, x1=x1, h2=h2, ids=ids,
                wcol=wcol, cnt=cnt, y_buf=y_buf, pstart=pstart, q_perm=q_perm)
```

```python
import functools
import math

import numpy as np
import jax
import jax.numpy as jnp
from jax import lax
from jax.experimental import pallas as pl
from jax.experimental.pallas import tpu as pltpu

F32 = jnp.float32
BF16 = jnp.bfloat16

D_MODEL = 1024
GRID_W = 64
EPS = 1e-6
SSD_INNER = 512
SSD_HEAD_DIM = 64
SSD_HEADS = 8
SSD_GROUPS = 2
SSD_STATE = 128
CONV_WIDTH = 5
CONV_DIM = SSD_INNER + 2 * SSD_GROUPS * SSD_STATE
CHUNK = 128
ATTN_WIDTH = 512
HEAD_DIM = 64
ATTN_Q_HEADS = 8
ATTN_KV_HEADS = 2
Q_PER_KV = 4
KV_WIDTH = 128
WINDOW = 128
ROPE_BASE = 10000.0
N_GROUPS = 4
EXPERTS_PER_GROUP = 8
N_EXPERTS = 32
EXPERT_DIM = 512

LANES = 128
SUBLANES = 8
NEG = -1e30

ROWS_IN = 512
ROWS_SSD = 512
ROWS_ATT = 256
ROUTE_ROWS = 48
MOE_BLK = 128
GATHER_SLOTS = 3
RANK_BITS = 16
D_TILES = D_MODEL // LANES
ROWS_CMB = 128


def _silu(v):
    return v * (1.0 / (1.0 + jnp.exp(-v)))


def _softplus(v):
    return jnp.maximum(v, 0.0) + jnp.log(1.0 + jnp.exp(-jnp.abs(v)))


def _ada_kernel(c_ref, w_ref, b_ref, o_ref):
    a = _silu(c_ref[...])
    o_ref[...] = jnp.dot(a.astype(BF16), w_ref[...].astype(BF16),
                         preferred_element_type=F32) + b_ref[...]


def _ada(cc, w_ada, b_ada):
    d, n = w_ada.shape
    tn = 1536
    return pl.pallas_call(
        _ada_kernel,
        grid=(n // tn,),
        in_specs=[pl.BlockSpec((SUBLANES, d), lambda j: (0, 0)),
                  pl.BlockSpec((d, tn), lambda j: (0, j)),
                  pl.BlockSpec((1, tn), lambda j: (0, j))],
        out_specs=pl.BlockSpec((SUBLANES, tn), lambda j: (0, j)),
        out_shape=jax.ShapeDtypeStruct((SUBLANES, n), F32),
        name="ada",
    )(cc, w_ada, b_ada)


def _rope(t, cos, sin_signed, first_half):
    rot = jnp.where(first_half, pltpu.roll(t, LANES - HEAD_DIM // 2, 1), pltpu.roll(t, HEAD_DIM // 2, 1))
    return t * cos + rot * sin_signed


def _inproj_kernel(*refs, segs, rope):
    x_ref, sh_ref, sc_ref, g_ref, w_ref, wdt_ref = refs[:6]
    pos = 6
    if rope:
        cos_ref, sin_ref = refs[6:8]
        pos = 8
    outs = refs[pos:]
    x = x_ref[...]
    ms = jnp.mean(x * x, axis=-1, keepdims=True)
    h = (x * lax.rsqrt(ms + EPS)) * (g_ref[...] * (1.0 + sc_ref[0])) + sh_ref[0]
    hb = h.astype(BF16)
    if rope:
        cos = cos_ref[...]
        lane = lax.broadcasted_iota(jnp.int32, cos.shape, 1)
        first_half = (lane & (HEAD_DIM - 1)) < HEAD_DIM // 2
        sin_s = jnp.where(first_half, -sin_ref[...], sin_ref[...])
    for (kind, c0, width), o_ref in zip(segs, outs):
        if kind == "dtT":
            dtt = lax.dot_general(wdt_ref[...], hb, (((1,), (1,)), ((), ())), preferred_element_type=F32)
            for j in range(o_ref.shape[0]):
                o_ref[j] = dtt[:, j * CHUNK:(j + 1) * CHUNK]
            continue
        for j0 in range(0, width, 512):
            wj = min(512, width - j0)
            acc = jnp.dot(hb, w_ref[:, c0 + j0:c0 + j0 + wj], preferred_element_type=F32)
            if rope and kind in ("q", "k"):
                scale = HEAD_DIM ** -0.5 if kind == "q" else 1.0
                for l0 in range(0, wj, LANES):
                    t = _rope(acc[:, l0:l0 + LANES], cos, sin_s, first_half)
                    o_ref[:, j0 + l0:j0 + l0 + LANES] = (t * scale).astype(o_ref.dtype)
            else:
                o_ref[:, j0:j0 + wj] = acc.astype(o_ref.dtype)


_COL_Z, _COL_XBC, _COL_Q, _COL_K, _COL_V = 0, 512, 1536, 2048, 2176
_W_COLS = 2304


def _inproj(xf, mod3, norm1, w_cat, w_dtT, rope_tabs, *, rows, mod_row_fn, segs, seq_blocks):
    t, d = xf.shape
    rope = rope_tabs is not None
    in_specs = [
        pl.BlockSpec((rows, d), lambda i: (i, 0)),
        pl.BlockSpec((1, 1, d), lambda i: (mod_row_fn(i), 0, 0)),
        pl.BlockSpec((1, 1, d), lambda i: (mod_row_fn(i), 0, 1)),
        pl.BlockSpec((1, d), lambda i: (0, 0)),
        pl.BlockSpec(w_cat.shape, lambda i: (0, 0)),
        pl.BlockSpec(w_dtT.shape, lambda i: (0, 0)),
    ]
    args = [xf, mod3, mod3, norm1, w_cat, w_dtT]
    if rope:
        in_specs += [pl.BlockSpec((rows, LANES), lambda i: (i % seq_blocks, 0))] * 2
        args += list(rope_tabs)
    out_specs, out_shapes = [], []
    for kind, _, width in segs:
        if kind == "dtT":
            out_specs.append(pl.BlockSpec((rows // CHUNK, 2 * SSD_HEADS, CHUNK), lambda i: (i, 0, 0)))
            out_shapes.append(jax.ShapeDtypeStruct((t // CHUNK, 2 * SSD_HEADS, CHUNK), F32))
        else:
            dt = BF16 if kind in ("q", "k", "v") else F32
            out_specs.append(pl.BlockSpec((rows, width), lambda i: (i, 0)))
            out_shapes.append(jax.ShapeDtypeStruct((t, width), dt))
    return pl.pallas_call(
        functools.partial(_inproj_kernel, segs=tuple(segs), rope=rope),
        grid=(t // rows,),
        in_specs=in_specs,
        out_specs=out_specs,
        out_shape=out_shapes,
        name="inproj_rope" if rope else "inproj_ctx",
    )(*args)


def _lane_cumsum(v, lane, reverse):
    k = 1
    while k < LANES:
        if reverse:
            v = v + jnp.where(lane < LANES - k, pltpu.roll(v, LANES - k, 1), 0.0)
        else:
            v = v + jnp.where(lane >= k, pltpu.roll(v, k, 1), 0.0)
        k *= 2
    return v


def _ssd_kernel(*refs, need_y, nsteps, cps):
    (xbc_ref, prev_ref, next_ref, dtr_ref, z_ref, cw_ref, cb_ref, bias_ref, alog_ref,
     dskip_ref, gain_ref, exp_ref, h0f_ref, h0b_ref) = refs[:14]
    if need_y:
        y_ref, hf_out, hb_out = refs[14:17]
        rest = refs[17:]
    else:
        y_ref = None
        hf_out, hb_out = refs[14:16]
        rest = refs[16:]
    win, cwin, u_ref, hf, hb, hbs = rest

    sw = pl.program_id(1)
    s = pl.program_id(2)
    bi = jnp.where(sw == 0, nsteps - 1 - s, s)
    rows = xbc_ref.shape[1]
    h = SSD_HEADS

    @pl.when((sw == 0) & (s == 0))
    def _():
        hb[...] = h0b_ref[0]

    @pl.when((sw == 1) & (s == 0))
    def _():
        hf[...] = h0f_ref[0]

    win[0:SUBLANES, :] = jnp.where(bi > 0, prev_ref[0], 0.0)
    win[SUBLANES:SUBLANES + rows, :] = xbc_ref[0]
    win[SUBLANES + rows:, :] = jnp.where(bi < nsteps - 1, next_ref[0], 0.0)

    lane = lax.broadcasted_iota(jnp.int32, (2 * h, LANES), 1)
    ri = lax.broadcasted_iota(jnp.int32, (CHUNK, CHUNK), 0)
    ci = lax.broadcasted_iota(jnp.int32, (CHUNK, CHUNK), 1)
    lane128 = lax.broadcasted_iota(jnp.int32, (CHUNK, LANES), 1)
    lo_mask = lane128 < SSD_HEAD_DIM
    neg_a = -jnp.exp(alog_ref[...])

    def conv_silu(c, ncols):
        r0 = pl.multiple_of(c * CHUNK, CHUNK)
        cwin[...] = win[pl.ds(r0, CHUNK + 2 * SUBLANES), :]
        for j0 in range(0, ncols, LANES):
            acc = jnp.broadcast_to(cb_ref[:, j0:j0 + LANES], (CHUNK, LANES))
            for k in range(CONV_WIDTH):
                off = SUBLANES - CONV_WIDTH // 2 + k
                acc = acc + cwin[off:off + CHUNK, j0:j0 + LANES] * cw_ref[k:k + 1, j0:j0 + LANES]
            u_ref[:, j0:j0 + LANES] = _silu(acc)

    def dt_rows(c):
        dt = _softplus(dtr_ref[c] + bias_ref[...])
        a = dt * neg_a
        pre = _lane_cumsum(a, lane, False)
        suf = _lane_cumsum(a, lane, True)
        return dt, pre, suf

    def state_update(state_ref, w_rows, tot_col):
        dec = jnp.exp(jnp.sum(tot_col * exp_ref[...], axis=0, keepdims=True))
        for g in range(SSD_GROUPS):
            bm = u_ref[:, SSD_INNER + g * SSD_STATE:SSD_INNER + (g + 1) * SSD_STATE]
            bt = bm.T
            for pr in range(2):
                h0 = g * 4 + pr * 2
                c0 = h0 * SSD_HEAD_DIM
                xp = u_ref[:, c0:c0 + LANES]
                rhs = jnp.concatenate([jnp.where(lo_mask, xp, 0.0), jnp.where(lo_mask, 0.0, xp)],
                                      axis=0).astype(BF16)
                lhs = jnp.concatenate([bt * w_rows[h0:h0 + 1, :], bt * w_rows[h0 + 1:h0 + 2, :]],
                                      axis=1).astype(BF16)
                sres = jnp.dot(lhs, rhs, preferred_element_type=F32)
                state_ref[:, c0:c0 + LANES] = state_ref[:, c0:c0 + LANES] * dec[:, c0:c0 + LANES] + sres

    def col_forms(rows16):
        padded = jnp.concatenate([rows16, jnp.zeros((CHUNK - 2 * h, LANES), F32)], axis=0)
        return padded.T

    def backward_chunk(k, carry):
        c = cps - 1 - k
        gc = bi * cps + c
        conv_silu(c, SSD_INNER + SSD_GROUPS * SSD_STATE)
        dt, _, suf = dt_rows(c)
        hbs[gc] = hb[...].astype(BF16)
        s0 = suf[:, 0:1]
        w_rows = jnp.exp(s0 - suf) * dt
        state_update(hb, w_rows[h:2 * h, :], s0[h:2 * h, :])
        return carry

    def forward_chunk(c, carry):
        gc = bi * cps + c
        r0 = pl.multiple_of(c * CHUNK, CHUNK)
        conv_silu(c, CONV_DIM)
        dt, pre, suf = dt_rows(c)
        rowsf = jnp.concatenate([pre[0:h, :], suf[h:2 * h, :]], axis=0)
        colsf = col_forms(rowsf)
        ecol = jnp.exp(colsf)
        for g in range(SSD_GROUPS):
            bm = u_ref[:, SSD_INNER + g * SSD_STATE:SSD_INNER + (g + 1) * SSD_STATE]
            cm = u_ref[:, SSD_INNER + (SSD_GROUPS + g) * SSD_STATE:SSD_INNER + (SSD_GROUPS + g + 1) * SSD_STATE]
            cmb = cm.astype(BF16)
            cbm = lax.dot_general(cmb, bm.astype(BF16), (((1,), (1,)), ((), ())),
                                  preferred_element_type=F32)
            for pr in range(2):
                h0 = g * 4 + pr * 2
                c0 = h0 * SSD_HEAD_DIM
                lhs_parts = []
                for hh in (h0, h0 + 1):
                    segf = colsf[:, hh:hh + 1] - rowsf[hh:hh + 1, :]
                    gf = jnp.exp(jnp.where(ci <= ri, segf, NEG)) * dt[hh:hh + 1, :]
                    segb = colsf[:, h + hh:h + hh + 1] - rowsf[h + hh:h + hh + 1, :]
                    gb = jnp.exp(jnp.where(ci >= ri, segb, NEG)) * dt[h + hh:h + hh + 1, :]
                    lhs_parts.append((cbm * (gf + gb)).astype(BF16))
                for hh in (h0, h0 + 1):
                    lhs_parts.append((cm * ecol[:, hh:hh + 1]).astype(BF16))
                for hh in (h0, h0 + 1):
                    lhs_parts.append((cm * ecol[:, h + hh:h + hh + 1]).astype(BF16))
                lhs = jnp.concatenate(lhs_parts, axis=1)
                xp = u_ref[:, c0:c0 + LANES]
                sf = hf[:, c0:c0 + LANES]
                sb = hbs[gc, :, c0:c0 + LANES].astype(F32)
                rhs = jnp.concatenate(
                    [jnp.where(lo_mask, xp, 0.0), jnp.where(lo_mask, 0.0, xp),
                     jnp.where(lo_mask, sf, 0.0), jnp.where(lo_mask, 0.0, sf),
                     jnp.where(lo_mask, sb, 0.0), jnp.where(lo_mask, 0.0, sb)], axis=0).astype(BF16)
                ypair = jnp.dot(lhs, rhs, preferred_element_type=F32)
                ypair = ypair + dskip_ref[:, c0:c0 + LANES] * xp
                zz = z_ref[0, pl.ds(r0, CHUNK), c0:c0 + LANES]
                u_ref[:, CONV_DIM + c0:CONV_DIM + c0 + LANES] = ypair * _silu(zz)
        last = pre[:, LANES - 1:LANES]
        w_rows = jnp.exp(last - pre) * dt
        state_update(hf, w_rows[0:h, :], last[0:h, :])
        if need_y:
            yv = u_ref[:, CONV_DIM:CONV_DIM + SSD_INNER]
            ms = jnp.mean(yv * yv, axis=-1, keepdims=True)
            y_ref[0, pl.ds(r0, CHUNK), :] = (yv * lax.rsqrt(ms + EPS) * gain_ref[...]).astype(y_ref.dtype)
        return carry

    @pl.when(sw == 0)
    def _():
        lax.fori_loop(0, cps, backward_chunk, 0)

    @pl.when(sw == 1)
    def _():
        lax.fori_loop(0, cps, forward_chunk, 0)

    @pl.when((sw == 0) & (s == nsteps - 1))
    def _():
        hb_out[0] = hb[...]

    @pl.when((sw == 1) & (s == nsteps - 1))
    def _():
        hf_out[0] = hf[...]


def _ssd(xbc, dtr, z, consts, h0f, h0b, *, rows, need_y):
    bt, seq, _ = xbc.shape
    nsteps = seq // rows
    cps = rows // CHUNK
    hb8 = rows // SUBLANES
    nb8 = seq // SUBLANES
    cw, cb, bias, alog, dskip, gain, expm = consts

    def blk(b, sw, s):
        return jnp.where(sw == 0, nsteps - 1 - s, s)

    def full(a):
        return pl.BlockSpec(a.shape, lambda b, sw, s: (0,) * a.ndim)

    in_specs = [
        pl.BlockSpec((1, rows, CONV_DIM), lambda b, sw, s: (b, blk(b, sw, s), 0)),
        pl.BlockSpec((1, SUBLANES, CONV_DIM), lambda b, sw, s: (b, jnp.maximum(blk(b, sw, s) * hb8 - 1, 0), 0)),
        pl.BlockSpec((1, SUBLANES, CONV_DIM),
                     lambda b, sw, s: (b, jnp.minimum((blk(b, sw, s) + 1) * hb8, nb8 - 1), 0)),
        pl.BlockSpec((cps, 2 * SSD_HEADS, CHUNK), lambda b, sw, s: (b * nsteps + blk(b, sw, s), 0, 0)),
        pl.BlockSpec((1, rows, SSD_INNER), lambda b, sw, s: (b, jnp.where(sw == 0, 0, s), 0)),
        full(cw), full(cb), full(bias), full(alog), full(dskip), full(gain), full(expm),
        pl.BlockSpec((1, SSD_STATE, SSD_INNER), lambda b, sw, s: (b, 0, 0)),
        pl.BlockSpec((1, SSD_STATE, SSD_INNER), lambda b, sw, s: (b, 0, 0)),
    ]
    st_spec = pl.BlockSpec((1, SSD_STATE, SSD_INNER), lambda b, sw, s: (b, 0, 0))
    st_shape = jax.ShapeDtypeStruct((bt, SSD_STATE, SSD_INNER), F32)
    out_specs, out_shapes = [st_spec, st_spec], [st_shape, st_shape]
    if need_y:
        out_specs = [pl.BlockSpec((1, rows, SSD_INNER), lambda b, sw, s: (b, jnp.where(sw == 0, 0, s), 0))] + out_specs
        out_shapes = [jax.ShapeDtypeStruct((bt, seq, SSD_INNER), BF16)] + out_shapes
    scratch = [
        pltpu.VMEM((rows + 2 * SUBLANES, CONV_DIM), F32),
        pltpu.VMEM((CHUNK + 2 * SUBLANES, CONV_DIM), F32),
        pltpu.VMEM((CHUNK, CONV_DIM + SSD_INNER), F32),
        pltpu.VMEM((SSD_STATE, SSD_INNER), F32),
        pltpu.VMEM((SSD_STATE, SSD_INNER), F32),
        pltpu.VMEM((seq // CHUNK, SSD_STATE, SSD_INNER), BF16),
    ]
    return pl.pallas_call(
        functools.partial(_ssd_kernel, need_y=need_y, nsteps=nsteps, cps=cps),
        grid=(bt, 2, nsteps),
        in_specs=in_specs,
        out_specs=out_specs,
        out_shape=out_shapes,
        scratch_shapes=scratch,
        compiler_params=pltpu.CompilerParams(dimension_semantics=("arbitrary", "arbitrary", "arbitrary")),
        name="ssd_y" if need_y else "ssd_ctx",
    )(xbc, xbc, xbc, dtr, z, cw, cb, bias, alog, dskip, gain, expm, h0f, h0b)


def _attn_kernel(sink_ref, q_ref, kp_ref, kc_ref, kn_ref, vp_ref, vc_ref, vn_ref, kx_ref, vx_ref, o_ref, *, nsteps):
    m = pl.program_id(1)
    lane = lax.broadcasted_iota(jnp.int32, (1, LANES), 1)
    lo = lane < HEAD_DIM
    rq = Q_PER_KV * WINDOW
    qi = lax.broadcasted_iota(jnp.int32, (rq, WINDOW), 0) & (WINDOW - 1)
    kj = lax.broadcasted_iota(jnp.int32, (rq, WINDOW), 1)
    rblk = lax.shift_right_logical(lax.broadcasted_iota(jnp.int32, (rq, 1), 0), int(math.log2(WINDOW)))
    zero = jnp.zeros((), BF16)

    kpieces = [kp_ref[0], kc_ref[0, 0:WINDOW], kc_ref[0, WINDOW:2 * WINDOW], kn_ref[0]]
    vpieces = [vp_ref[0], vc_ref[0, 0:WINDOW], vc_ref[0, WINDOW:2 * WINDOW], vn_ref[0]]
    kx = kx_ref[0]
    vx = vx_ref[0]
    nt = (((1,), (1,)), ((), ()))

    for sb in range(ROWS_ATT // WINDOW):
        qsb = q_ref[0, sb * WINDOW:(sb + 1) * WINDOW, :]
        prev_ok = jnp.logical_or(m > 0, sb > 0)
        next_ok = jnp.logical_or(m < nsteps - 1, sb < ROWS_ATT // WINDOW - 1)
        outs = [None] * Q_PER_KV
        for kv in range(ATTN_KV_HEADS):
            sel = lo if kv == 0 else jnp.logical_not(lo)
            qs = jnp.concatenate(
                [jnp.where(sel, qsb[:, j * LANES:(j + 1) * LANES], zero) for j in range(Q_PER_KV)], axis=0)
            s_prev = lax.dot_general(qs, kpieces[sb], nt, preferred_element_type=F32)
            s_cur = lax.dot_general(qs, kpieces[sb + 1], nt, preferred_element_type=F32)
            s_next = lax.dot_general(qs, kpieces[sb + 2], nt, preferred_element_type=F32)
            s_ctx = lax.dot_general(qs, kx, nt, preferred_element_type=F32)
            s_prev = jnp.where((kj >= qi) & prev_ok, s_prev, NEG)
            s_next = jnp.where((kj <= qi) & next_ok, s_next, NEG)
            sink = jnp.zeros((rq, 1), F32)
            for j in range(Q_PER_KV):
                sink = jnp.where(rblk == j, sink_ref[kv * Q_PER_KV + j], sink)
            mx = jnp.maximum(
                jnp.maximum(jnp.max(s_prev, axis=-1, keepdims=True), jnp.max(s_cur, axis=-1, keepdims=True)),
                jnp.maximum(jnp.max(s_next, axis=-1, keepdims=True), jnp.max(s_ctx, axis=-1, keepdims=True)))
            mx = jnp.maximum(mx, sink)
            p_prev = jnp.exp(s_prev - mx)
            p_cur = jnp.exp(s_cur - mx)
            p_next = jnp.exp(s_next - mx)
            p_ctx = jnp.exp(s_ctx - mx)
            den = (jnp.sum(p_prev, axis=-1, keepdims=True) + jnp.sum(p_cur, axis=-1, keepdims=True)
                   + jnp.sum(p_next, axis=-1, keepdims=True) + jnp.sum(p_ctx, axis=-1, keepdims=True)
                   + jnp.exp(sink - mx))
            acc = jnp.dot(p_prev.astype(BF16), jnp.where(sel, vpieces[sb], zero), preferred_element_type=F32)
            acc += jnp.dot(p_cur.astype(BF16), jnp.where(sel, vpieces[sb + 1], zero), preferred_element_type=F32)
            acc += jnp.dot(p_next.astype(BF16), jnp.where(sel, vpieces[sb + 2], zero), preferred_element_type=F32)
            acc += jnp.dot(p_ctx.astype(BF16), jnp.where(sel, vx, zero), preferred_element_type=F32)
            acc = acc * (1.0 / den)
            for j in range(Q_PER_KV):
                part = acc[j * WINDOW:(j + 1) * WINDOW, :]
                outs[j] = part if outs[j] is None else outs[j] + part
        for j in range(Q_PER_KV):
            o_ref[0, sb * WINDOW:(sb + 1) * WINDOW, j * LANES:(j + 1) * LANES] = outs[j].astype(o_ref.dtype)


def _attention(sinks, q, k, v, kx, vx):
    bt, seq, _ = q.shape
    nsteps = seq // ROWS_ATT
    per = ROWS_ATT // WINDOW
    nb = seq // WINDOW
    n_ctx = kx.shape[1]

    def prev(b, m, s):
        return (b, jnp.maximum(m * per - 1, 0), 0)

    def cur(b, m, s):
        return (b, m, 0)

    def nxt(b, m, s):
        return (b, jnp.minimum((m + 1) * per, nb - 1), 0)

    small = (1, WINDOW, KV_WIDTH)
    big = (1, ROWS_ATT, KV_WIDTH)
    grid_spec = pltpu.PrefetchScalarGridSpec(
        num_scalar_prefetch=1,
        grid=(bt, nsteps),
        in_specs=[
            pl.BlockSpec((1, ROWS_ATT, ATTN_WIDTH), cur),
            pl.BlockSpec(small, prev), pl.BlockSpec(big, cur), pl.BlockSpec(small, nxt),
            pl.BlockSpec(small, prev), pl.BlockSpec(big, cur), pl.BlockSpec(small, nxt),
            pl.BlockSpec((1, n_ctx, KV_WIDTH), lambda b, m, s: (b, 0, 0)),
            pl.BlockSpec((1, n_ctx, KV_WIDTH), lambda b, m, s: (b, 0, 0)),
        ],
        out_specs=pl.BlockSpec((1, ROWS_ATT, ATTN_WIDTH), cur),
    )
    return pl.pallas_call(
        functools.partial(_attn_kernel, nsteps=nsteps),
        grid_spec=grid_spec,
        out_shape=jax.ShapeDtypeStruct((bt, seq, ATTN_WIDTH), BF16),
        name="attn",
    )(sinks, q, k, k, k, v, v, v, kx, vx)


def _outproj_kernel(ssd_ref, att_ref, x_ref, g1_ref, sh_ref, sc_ref, n2_ref, wa_ref, wb_ref, wr_ref, br_ref,
                    tri_ref, x1_ref, h2_ref, ids_ref, wcol_ref, cnt_ref, carry, *, nsteps):
    i = pl.program_id(0)
    rows = x_ref.shape[0]

    @pl.when(i == 0)
    def _():
        carry[...] = jnp.zeros_like(carry)

    acc = jnp.dot(ssd_ref[...], wa_ref[...], preferred_element_type=F32)
    acc += jnp.dot(att_ref[...], wb_ref[...], preferred_element_type=F32)
    x1 = x_ref[...] + g1_ref[0] * acc
    x1_ref[...] = x1
    ms = jnp.mean(x1 * x1, axis=-1, keepdims=True)
    h2 = (x1 * lax.rsqrt(ms + EPS)) * (n2_ref[...] * (1.0 + sc_ref[0])) + sh_ref[0]
    for sidx in range(d_tiles := h2.shape[1] // LANES):
        h2_ref[pl.ds(sidx, rows, stride=d_tiles), :] = h2[:, sidx * LANES:(sidx + 1) * LANES]
    lt = lax.dot_general(wr_ref[...], h2.astype(BF16), (((1,), (1,)), ((), ())),
                         preferred_element_type=F32) + br_ref[...]
    row = lax.broadcasted_iota(jnp.int32, lt.shape, 0).astype(F32)
    big_i = float(ROUTE_ROWS)
    gl = jnp.where(row < N_GROUPS, lt, NEG)
    gmax = jnp.max(gl, axis=0, keepdims=True)
    gidx = jnp.min(jnp.where(gl == gmax, row, big_i), axis=0, keepdims=True)
    g_w = 1.0 / jnp.sum(jnp.exp(gl - gmax), axis=0, keepdims=True)
    lo = N_GROUPS + EXPERTS_PER_GROUP * gidx
    el = jnp.where((row >= lo) & (row < lo + EXPERTS_PER_GROUP), lt, NEG)
    m1 = jnp.max(el, axis=0, keepdims=True)
    i1 = jnp.min(jnp.where(el == m1, row, big_i), axis=0, keepdims=True)
    el2 = jnp.where(row == i1, NEG, el)
    m2 = jnp.max(el2, axis=0, keepdims=True)
    i2 = jnp.min(jnp.where(el2 == m2, row, big_i), axis=0, keepdims=True)
    r = jnp.exp(m2 - m1)
    w1 = g_w / (1.0 + r)
    w2 = g_w * r / (1.0 + r)
    oh1 = row == i1
    oh2 = row == i2
    cnt = jnp.where(oh1 | oh2, 1.0, 0.0)
    prefix = jnp.dot(cnt.astype(BF16), tri_ref[...], preferred_element_type=F32)
    base = carry[...] + prefix
    rank1 = jnp.sum(jnp.where(oh1, base, 0.0), axis=0, keepdims=True)
    rank2 = jnp.sum(jnp.where(oh2, base, 0.0), axis=0, keepdims=True)
    carry[...] = carry[...] + jnp.sum(cnt, axis=1, keepdims=True)
    r8 = lax.broadcasted_iota(jnp.int32, (SUBLANES, rows), 0)
    code1 = (i1 - N_GROUPS) * float(1 << RANK_BITS) + rank1
    code2 = (i2 - N_GROUPS) * float(1 << RANK_BITS) + rank2
    ids_ref[...] = jnp.where(r8 == 0, code1, jnp.where(r8 == 1, code2, 0.0)).astype(jnp.int32)
    r128 = lax.broadcasted_iota(jnp.int32, (LANES, rows), 0)
    wfull = jnp.where(r128 == 0, w1, jnp.where(r128 == 1, w2, 0.0))
    for j in range(rows // LANES):
        wcol_ref[j * LANES:(j + 1) * LANES, :] = wfull[:, j * LANES:(j + 1) * LANES].T

    @pl.when(i == nsteps - 1)
    def _():
        cnt_ref[...] = carry[:, 0:LANES]


def _outproj(ssd, att, xf, mod3, norm2, wa, wb, wr, br, tri, *, rows, seq_blocks):
    t, d = xf.shape
    nsteps = t // rows

    def mod(col):
        return pl.BlockSpec((1, 1, d), lambda i: (i // seq_blocks, 0, col))

    def full(a):
        return pl.BlockSpec(a.shape, lambda i: (0,) * a.ndim)

    return pl.pallas_call(
        functools.partial(_outproj_kernel, nsteps=nsteps),
        grid=(nsteps,),
        in_specs=[pl.BlockSpec((rows, SSD_INNER), lambda i: (i, 0)),
                  pl.BlockSpec((rows, ATTN_WIDTH), lambda i: (i, 0)),
                  pl.BlockSpec((rows, d), lambda i: (i, 0)),
                  mod(2), mod(3), mod(4), full(norm2), full(wa), full(wb), full(wr), full(br), full(tri)],
        out_specs=[pl.BlockSpec((rows, d), lambda i: (i, 0)),
                   pl.BlockSpec((rows * (d // LANES), LANES), lambda i: (i, 0)),
                   pl.BlockSpec((SUBLANES, rows), lambda i: (0, i)),
                   pl.BlockSpec((rows, LANES), lambda i: (i, 0)),
                   pl.BlockSpec((ROUTE_ROWS, LANES), lambda i: (0, 0))],
        out_shape=[jax.ShapeDtypeStruct((t, d), F32),
                   jax.ShapeDtypeStruct((t * (d // LANES), LANES), F32),
                   jax.ShapeDtypeStruct((SUBLANES, t), jnp.int32),
                   jax.ShapeDtypeStruct((t, LANES), F32),
                   jax.ShapeDtypeStruct((ROUTE_ROWS, LANES), F32)],
        scratch_shapes=[pltpu.VMEM((ROUTE_ROWS, rows), F32)],
        compiler_params=pltpu.CompilerParams(dimension_semantics=("arbitrary",)),
        name="outproj_router",
    )(ssd, att, xf, mod3, mod3, mod3, norm2, wa, wb, wr, br, tri)


def _expert_kernel(blk_e_ref, nused_ref, pstart_ref, counts_ref, dst_ref, h2_hbm, wg_ref, wu_ref, wd_ref, y_ref,
                   xbuf, wgb, wub, wdb, tok_ref, sem, *, n_tok, nblocks):
    b = pl.program_id(0)
    nused = nused_ref[0]
    slot = lax.rem(b, GATHER_SLOTS)

    def row_copy(tok, r, sl):
        return pltpu.make_async_copy(h2_hbm.at[pl.ds(tok * D_TILES, D_TILES)],
                                     xbuf.at[sl, pl.ds(r * D_TILES, D_TILES)], sem.at[sl])

    def start_rows(blk, sl, r0, r1):
        for r in range(r0, r1):
            row_copy(tok_ref[blk * MOE_BLK + r], r, sl).start()

    def wait_rows(sl):
        for r in range(MOE_BLK):
            row_copy(0, r, sl).wait()

    @pl.when(b == 0)
    def _():
        def pad_expert(e, carry):
            lo = pstart_ref[e] + counts_ref[e]
            hi = pstart_ref[e] + (counts_ref[e] + MOE_BLK - 1) // MOE_BLK * MOE_BLK

            def pad_slot(s, c):
                tok_ref[s] = 0
                return c
            return lax.fori_loop(lo, hi, pad_slot, carry)
        lax.fori_loop(0, N_EXPERTS, pad_expert, 0)

        def claim(i, carry):
            tok_ref[dst_ref[i]] = jnp.where(i >= n_tok, i - n_tok, i)
            return carry
        lax.fori_loop(0, 2 * n_tok, claim, 0, unroll=16)
        start_rows(0, 0, 0, MOE_BLK)
        start_rows(jnp.minimum(1, nused - 1), 1, 0, MOE_BLK)

    e = blk_e_ref[b]
    e_prev = blk_e_ref[jnp.maximum(b - 1, 0)]

    @pl.when((b == 0) | (e != e_prev))
    def _():
        wgb[...] = wg_ref[0].astype(BF16)
        wub[...] = wu_ref[0].astype(BF16)
        wdb[...] = wd_ref[0].astype(BF16)

    nxt = jnp.minimum(b + 2, nused - 1)
    nxt_slot = lax.rem(b + 2, GATHER_SLOTS)
    wait_rows(slot)
    xb = jnp.concatenate([xbuf[slot, pl.ds(s, MOE_BLK, stride=D_TILES), :] for s in range(D_TILES)],
                         axis=1).astype(BF16)
    gate = jnp.dot(xb, wgb[...], preferred_element_type=F32)
    up = jnp.dot(xb, wub[...], preferred_element_type=F32)
    hmid = (_silu(gate) * up).astype(BF16)
    y = jnp.dot(hmid, wdb[...], preferred_element_type=F32)
    start_rows(nxt, nxt_slot, 0, MOE_BLK)
    for s in range(D_TILES):
        y_ref[pl.ds(s, MOE_BLK, stride=D_TILES), :] = y[:, s * LANES:(s + 1) * LANES]

    @pl.when(b == nblocks - 1)
    def _():
        wait_rows(lax.rem(b + 1, GATHER_SLOTS))
        wait_rows(nxt_slot)


def _experts(blk_e, nused, pstart, counts, dst, h2t, w_gate, w_up, w_down, *, nblocks, n_tok):
    d = w_gate.shape[1]

    def wspec(shape):
        return pl.BlockSpec(shape, lambda b, be, nu, ps, ct, cd: (be[b], 0, 0))

    grid_spec = pltpu.PrefetchScalarGridSpec(
        num_scalar_prefetch=5,
        grid=(nblocks,),
        in_specs=[pl.BlockSpec(memory_space=pl.ANY),
                  wspec((1, d, EXPERT_DIM)), wspec((1, d, EXPERT_DIM)), wspec((1, EXPERT_DIM, d))],
        out_specs=pl.BlockSpec((MOE_BLK * D_TILES, LANES), lambda b, be, nu, ps, ct, cd: (b, 0)),
        scratch_shapes=[
            pltpu.VMEM((GATHER_SLOTS, MOE_BLK * D_TILES, LANES), F32),
            pltpu.VMEM((d, EXPERT_DIM), BF16),
            pltpu.VMEM((d, EXPERT_DIM), BF16),
            pltpu.VMEM((EXPERT_DIM, d), BF16),
            pltpu.SMEM((nblocks * MOE_BLK,), jnp.int32),
            pltpu.SemaphoreType.DMA((GATHER_SLOTS,)),
        ],
    )
    return pl.pallas_call(
        functools.partial(_expert_kernel, n_tok=n_tok, nblocks=nblocks),
        grid_spec=grid_spec,
        out_shape=jax.ShapeDtypeStruct((nblocks * MOE_BLK * D_TILES, LANES), F32),
        compiler_params=pltpu.CompilerParams(dimension_semantics=("arbitrary",)),
        name="experts",
    )(blk_e, nused, pstart, counts, dst, h2t, w_gate, w_up, w_down)


def _combine_kernel(dst_ref, y_hbm, x1_ref, wcol_ref, g2_ref, nf_ref, o_ref, ybuf, sem, *, nsteps, n_tok):
    i = pl.program_id(0)
    rows = x1_ref.shape[0]
    slot = i % 2

    def row_copy(src, r, k, sl):
        return pltpu.make_async_copy(y_hbm.at[pl.ds(src * D_TILES, D_TILES)],
                                     ybuf.at[sl, k, pl.ds(r * D_TILES, D_TILES)], sem.at[sl])

    def start_rows(step, sl, r0, r1):
        for r in range(r0, r1):
            for k in range(2):
                row_copy(dst_ref[k * n_tok + step * rows + r], r, k, sl).start()

    def wait_rows(sl):
        for r in range(rows):
            for k in range(2):
                row_copy(0, r, k, sl).wait()

    @pl.when(i == 0)
    def _():
        start_rows(0, 0, 0, rows)

    @pl.when(i + 1 < nsteps)
    def _():
        start_rows(i + 1, 1 - slot, 0, rows)

    wait_rows(slot)
    w0 = wcol_ref[:, 0:1]
    w1 = wcol_ref[:, 1:2]
    ssq = jnp.zeros((rows, 1), F32)
    for s in range(D_TILES):
        cols = slice(s * LANES, (s + 1) * LANES)
        moe = (ybuf[slot, 0, pl.ds(s, rows, stride=D_TILES), :] * w0
               + ybuf[slot, 1, pl.ds(s, rows, stride=D_TILES), :] * w1)
        x2 = x1_ref[:, cols] + g2_ref[0, :, cols] * moe
        ssq = ssq + jnp.sum(x2 * x2, axis=-1, keepdims=True)
        o_ref[:, cols] = x2
    inv = lax.rsqrt(ssq * (1.0 / (D_TILES * LANES)) + EPS)
    o_ref[...] = o_ref[...] * inv * nf_ref[...]


def _combine(dst, y_buf, x1, wcol, mod3, norm_final, *, rows, seq_blocks):
    t, d = x1.shape
    nsteps = t // rows
    grid_spec = pltpu.PrefetchScalarGridSpec(
        num_scalar_prefetch=1,
        grid=(nsteps,),
        in_specs=[
            pl.BlockSpec(memory_space=pl.ANY),
            pl.BlockSpec((rows, d), lambda i, ds: (i, 0)),
            pl.BlockSpec((rows, LANES), lambda i, ds: (i, 0)),
            pl.BlockSpec((1, 1, d), lambda i, ds: (i // seq_blocks, 0, 5)),
            pl.BlockSpec((1, d), lambda i, ds: (0, 0)),
        ],
        out_specs=pl.BlockSpec((rows, d), lambda i, ds: (i, 0)),
        scratch_shapes=[pltpu.VMEM((2, 2, rows * D_TILES, LANES), F32), pltpu.SemaphoreType.DMA((2,))],
    )
    return pl.pallas_call(
        functools.partial(_combine_kernel, nsteps=nsteps, n_tok=t),
        grid_spec=grid_spec,
        out_shape=jax.ShapeDtypeStruct((t, d), F32),
        compiler_params=pltpu.CompilerParams(dimension_semantics=("arbitrary",)),
        name="combine",
    )(dst, y_buf, x1, wcol, mod3, norm_final)


def _rope_tables(seq):
    pos = np.arange(seq)
    n_freq = HEAD_DIM // 4
    inv = ROPE_BASE ** (-np.arange(n_freq, dtype=np.float32) / n_freq)
    ang = np.concatenate([(pos // GRID_W)[:, None] * inv, (pos % GRID_W)[:, None] * inv], axis=-1)
    ang = np.concatenate([ang, ang, ang, ang], axis=-1).astype(np.float32)
    return jnp.asarray(np.cos(ang), F32), jnp.asarray(np.sin(ang), F32)


def _head_expand_mat():
    m = np.zeros((SSD_HEADS, SSD_INNER), np.float32)
    for hh in range(SSD_HEADS):
        m[hh, hh * SSD_HEAD_DIM:(hh + 1) * SSD_HEAD_DIM] = 1.0
    return jnp.asarray(m)


def kernel(x, c, ctx, c_ctx, w_ada, b_ada, norm1, w_in, conv_w, conv_b, dt_bias, a_log, d_skip, ssd_norm,
           attn_sinks, w_out, norm2, w_group, b_group, w_expert, b_expert, w_gate, w_up, w_down, norm_final):
    return _pipeline(x, c, ctx, c_ctx, w_ada, b_ada, norm1, w_in, conv_w, conv_b, dt_bias, a_log, d_skip,
                     ssd_norm, attn_sinks, w_out, norm2, w_group, b_group, w_expert, b_expert, w_gate, w_up,
                     w_down, norm_final)["out"]


def _pipeline(x, c, ctx, c_ctx, w_ada, b_ada, norm1, w_in, conv_w, conv_b, dt_bias, a_log, d_skip, ssd_norm,
              attn_sinks, w_out, norm2, w_group, b_group, w_expert, b_expert, w_gate, w_up, w_down, norm_final):
    bt, seq, d = x.shape
    n_ctx = ctx.shape[1]
    t = bt * seq
    layer = 0
    assert w_ada.shape[0] == 1 and seq % ROWS_IN == 0 and n_ctx % CHUNK == 0

    cc = jnp.zeros((SUBLANES, d), F32).at[:bt].set(c).at[bt].set(c_ctx)
    mod = _ada(cc, w_ada[layer], b_ada[layer][None, :])
    mod3 = mod.reshape(SUBLANES, 1, 6 * d)

    w = w_in[layer]
    o_z, o_xbc, o_dt = 0, SSD_INNER, SSD_INNER + CONV_DIM
    o_q = o_dt + 2 * SSD_HEADS
    o_k, o_v = o_q + ATTN_WIDTH, o_q + ATTN_WIDTH + KV_WIDTH
    q_perm = np.concatenate([np.r_[j * HEAD_DIM:(j + 1) * HEAD_DIM, (j + 4) * HEAD_DIM:(j + 5) * HEAD_DIM]
                             for j in range(Q_PER_KV)])
    w_q = w[:, o_q:o_q + ATTN_WIDTH].reshape(d, ATTN_KV_HEADS, Q_PER_KV, HEAD_DIM).transpose(0, 2, 1, 3)
    w_q = w_q.reshape(d, ATTN_WIDTH)
    w_cat = jnp.concatenate([w[:, o_z:o_z + SSD_INNER], w[:, o_xbc:o_xbc + CONV_DIM], w_q,
                             w[:, o_k:o_k + KV_WIDTH], w[:, o_v:o_v + KV_WIDTH]], axis=1).astype(BF16)
    w_dtT = w[:, o_dt:o_dt + 2 * SSD_HEADS].T.astype(BF16)

    seq_blocks = seq // ROWS_IN
    cos, sin = _rope_tables(seq)
    segs_x = [("z", _COL_Z, SSD_INNER), ("xbc", _COL_XBC, CONV_DIM), ("dtT", 0, 0),
              ("q", _COL_Q, ATTN_WIDTH), ("k", _COL_K, KV_WIDTH), ("v", _COL_V, KV_WIDTH)]
    z_x, xbc_x, dtr_x, q_x, k_x, v_x = _inproj(
        x.reshape(t, d), mod3, norm1[layer][None, :], w_cat, w_dtT, (cos, sin),
        rows=ROWS_IN, mod_row_fn=lambda i: i // seq_blocks, segs=segs_x, seq_blocks=seq_blocks)
    segs_c = [("xbc", _COL_XBC, CONV_DIM), ("dtT", 0, 0), ("k", _COL_K, KV_WIDTH), ("v", _COL_V, KV_WIDTH)]
    xbc_c, dtr_c, k_c, v_c = _inproj(
        ctx.reshape(bt * n_ctx, d), mod3, norm1[layer][None, :], w_cat, w_dtT, None,
        rows=n_ctx, mod_row_fn=lambda i: bt, segs=segs_c, seq_blocks=1)

    cw = jnp.zeros((SUBLANES, CONV_DIM), F32).at[:CONV_WIDTH].set(conv_w[layer])
    cb = conv_b[layer][None, :]
    bias_rows = jnp.broadcast_to(dt_bias[layer].reshape(2 * SSD_HEADS, 1), (2 * SSD_HEADS, LANES))
    alog_rows = jnp.broadcast_to(a_log[layer].reshape(2 * SSD_HEADS, 1), (2 * SSD_HEADS, LANES))
    dskip_e = jnp.repeat(d_skip[layer], SSD_HEAD_DIM)[None, :]
    consts = (cw, cb, bias_rows, alog_rows, dskip_e, ssd_norm[layer][None, :], _head_expand_mat())

    zeros_state = jnp.zeros((bt, SSD_STATE, SSD_INNER), F32)
    z_dummy = jnp.zeros((bt, n_ctx, SSD_INNER), F32)
    h_cf, h_cb = _ssd(xbc_c.reshape(bt, n_ctx, CONV_DIM), dtr_c, z_dummy, consts,
                      zeros_state, zeros_state, rows=n_ctx, need_y=False)
    ssd_x, _, _ = _ssd(xbc_x.reshape(bt, seq, CONV_DIM), dtr_x, z_x.reshape(bt, seq, SSD_INNER),
                       consts, h_cf, h_cb, rows=ROWS_SSD, need_y=True)

    attn_x = _attention(attn_sinks[layer].astype(F32),
                        q_x.reshape(bt, seq, ATTN_WIDTH), k_x.reshape(bt, seq, KV_WIDTH),
                        v_x.reshape(bt, seq, KV_WIDTH), k_c.reshape(bt, n_ctx, KV_WIDTH),
                        v_c.reshape(bt, n_ctx, KV_WIDTH))

    wo = w_out[layer]
    wa = wo[:SSD_INNER].astype(BF16)
    wb = wo[SSD_INNER:].reshape(ATTN_KV_HEADS, Q_PER_KV, HEAD_DIM, d).transpose(1, 0, 2, 3)
    wb = wb.reshape(ATTN_WIDTH, d).astype(BF16)
    wr = jnp.zeros((ROUTE_ROWS, d), F32).at[:N_GROUPS].set(w_group[layer].T)
    wr = wr.at[N_GROUPS:N_GROUPS + N_EXPERTS].set(w_expert[layer].T).astype(BF16)
    br = jnp.zeros((ROUTE_ROWS,), F32).at[:N_GROUPS].set(b_group[layer])
    br = br.at[N_GROUPS:N_GROUPS + N_EXPERTS].set(b_expert[layer])
    br = jnp.broadcast_to(br[:, None], (ROUTE_ROWS, ROWS_IN))
    tri = jnp.asarray(np.triu(np.ones((ROWS_IN, ROWS_IN), np.float32), 1), BF16)
    x1, h2, ids, wcol, cnt = _outproj(
        ssd_x.reshape(t, SSD_INNER), attn_x.reshape(t, ATTN_WIDTH), x.reshape(t, d), mod3, norm2[layer][None, :],
        wa, wb, wr, br, tri, rows=ROWS_IN, seq_blocks=seq_blocks)

    counts = cnt[N_GROUPS:N_GROUPS + N_EXPERTS, 0].astype(jnp.int32)
    padded = (counts + MOE_BLK - 1) // MOE_BLK * MOE_BLK
    ends = jnp.cumsum(padded)
    pstart = ends - padded
    nblocks = (2 * t) // MOE_BLK + N_EXPERTS
    blk_start = jnp.arange(nblocks, dtype=jnp.int32) * MOE_BLK
    blk_e = jnp.minimum(jnp.sum((ends[None, :] <= blk_start[:, None]).astype(jnp.int32), axis=1), N_EXPERTS - 1)
    nused = (ends[-1] // MOE_BLK).astype(jnp.int32).reshape(1)
    code = ids[0:2].reshape(-1)
    e_id = lax.shift_right_logical(code, RANK_BITS)
    onehot = e_id[:, None] == jnp.arange(N_EXPERTS, dtype=jnp.int32)[None, :]
    dst = jnp.sum(jnp.where(onehot, pstart[None, :], 0), axis=1) + (code & ((1 << RANK_BITS) - 1))

    y_buf = _experts(blk_e, nused, pstart, counts, dst, h2, w_gate[layer], w_up[layer], w_down[layer],
                     nblocks=nblocks, n_tok=t)
    out = _combine(dst, y_buf, x1, wcol, mod3, norm_final[None, :], rows=ROWS_CMB, seq_blocks=seq // ROWS_CMB)
    return dict(out=out.reshape(bt, seq, d), mod=mod, z=z_x, xbc=xbc_x, dtr=dtr_x, q=q_x, k=k_x, v=v_x,
                k_c=k_c, v_c=v_c, h_cf=h_cf, h_cb=h_cb, ssd=ssd_x, attn=attn_x, x1=x1, h2=h2, ids=ids,
                wcol=wcol, cnt=cnt, y_buf=y_buf, pstart=pstart, q_perm=q_perm)
```

```python
import functools
import math

import numpy as np
import jax
import jax.numpy as jnp
from jax import lax
from jax.experimental import pallas as pl
from jax.experimental.pallas import tpu as pltpu

F32 = jnp.float32
BF16 = jnp.bfloat16

D_MODEL = 1024
GRID_W = 64
EPS = 1e-6
SSD_INNER = 512
SSD_HEAD_DIM = 64
SSD_HEADS = 8
SSD_GROUPS = 2
SSD_STATE = 128
CONV_WIDTH = 5
CONV_DIM = SSD_INNER + 2 * SSD_GROUPS * SSD_STATE
CHUNK = 128
ATTN_WIDTH = 512
HEAD_DIM = 64
ATTN_Q_HEADS = 8
ATTN_KV_HEADS = 2
Q_PER_KV = 4
KV_WIDTH = 128
WINDOW = 128
ROPE_BASE = 10000.0
N_GROUPS = 4
EXPERTS_PER_GROUP = 8
N_EXPERTS = 32
EXPERT_DIM = 512

LANES = 128
SUBLANES = 8
NEG = -1e30

ROWS_IN = 512
ROWS_SSD = 512
ROWS_ATT = 256
ROUTE_ROWS = 48
MOE_BLK = 128
GATHER_SLOTS = 3
RANK_BITS = 16
D_TILES = D_MODEL // LANES
ROWS_CMB = 128


def _silu(v):
    return v * (1.0 / (1.0 + jnp.exp(-v)))


def _softplus(v):
    return jnp.maximum(v, 0.0) + jnp.log(1.0 + jnp.exp(-jnp.abs(v)))


def _ada_kernel(c_ref, w_ref, b_ref, o_ref):
    a = _silu(c_ref[...])
    o_ref[...] = jnp.dot(a.astype(BF16), w_ref[...].astype(BF16),
                         preferred_element_type=F32) + b_ref[...]


def _ada(cc, w_ada, b_ada):
    d, n = w_ada.shape
    tn = 1536
    return pl.pallas_call(
        _ada_kernel,
        grid=(n // tn,),
        in_specs=[pl.BlockSpec((SUBLANES, d), lambda j: (0, 0)),
                  pl.BlockSpec((d, tn), lambda j: (0, j)),
                  pl.BlockSpec((1, tn), lambda j: (0, j))],
        out_specs=pl.BlockSpec((SUBLANES, tn), lambda j: (0, j)),
        out_shape=jax.ShapeDtypeStruct((SUBLANES, n), F32),
        name="ada",
    )(cc, w_ada, b_ada)


def _rope(t, cos, sin_signed, first_half):
    rot = jnp.where(first_half, pltpu.roll(t, LANES - HEAD_DIM // 2, 1), pltpu.roll(t, HEAD_DIM // 2, 1))
    return t * cos + rot * sin_signed


def _lane_cumsum(v, lane, reverse):
    k = 1
    while k < LANES:
        if reverse:
            v = v + jnp.where(lane < LANES - k, pltpu.roll(v, LANES - k, 1), 0.0)
        else:
            v = v + jnp.where(lane >= k, pltpu.roll(v, k, 1), 0.0)
        k *= 2
    return v


def _inproj_kernel(*refs, segs, rope):
    x_ref, sh_ref, sc_ref, g_ref, w_ref, wdt_ref, bias_ref, alog_ref = refs[:8]
    pos = 8
    if rope:
        cos_ref, sin_ref = refs[8:10]
        pos = 10
    outs = iter(refs[pos:])
    x = x_ref[...]
    ms = jnp.mean(x * x, axis=-1, keepdims=True)
    h = (x * lax.rsqrt(ms + EPS)) * (g_ref[...] * (1.0 + sc_ref[0])) + sh_ref[0]
    hb = h.astype(BF16)
    if rope:
        cos = cos_ref[...]
        lane = lax.broadcasted_iota(jnp.int32, cos.shape, 1)
        first_half = (lane & (HEAD_DIM - 1)) < HEAD_DIM // 2
        sin_s = jnp.where(first_half, -sin_ref[...], sin_ref[...])
    for kind, c0, width in segs:
        if kind == "dtT":
            dt_ref, pre_ref, suf_ref = next(outs), next(outs), next(outs)
            dtt = lax.dot_general(wdt_ref[...], hb, (((1,), (1,)), ((), ())), preferred_element_type=F32)
            lane16 = lax.broadcasted_iota(jnp.int32, (2 * SSD_HEADS, CHUNK), 1)
            neg_a = -jnp.exp(alog_ref[...])
            for j in range(dt_ref.shape[0]):
                dt = _softplus(dtt[:, j * CHUNK:(j + 1) * CHUNK] + bias_ref[...])
                a = dt * neg_a
                dt_ref[j] = dt
                pre_ref[j] = _lane_cumsum(a, lane16, False)
                suf_ref[j] = _lane_cumsum(a, lane16, True)
            continue
        o_ref = next(outs)
        for j0 in range(0, width, 512):
            wj = min(512, width - j0)
            acc = jnp.dot(hb, w_ref[:, c0 + j0:c0 + j0 + wj], preferred_element_type=F32)
            if rope and kind in ("q", "k"):
                scale = HEAD_DIM ** -0.5 if kind == "q" else 1.0
                for l0 in range(0, wj, LANES):
                    t = _rope(acc[:, l0:l0 + LANES], cos, sin_s, first_half)
                    o_ref[:, j0 + l0:j0 + l0 + LANES] = (t * scale).astype(o_ref.dtype)
            else:
                o_ref[:, j0:j0 + wj] = acc.astype(o_ref.dtype)


_COL_Z, _COL_XBC, _COL_Q, _COL_K, _COL_V = 0, 512, 1536, 2048, 2176
_W_COLS = 2304


def _inproj(xf, mod3, norm1, w_cat, w_dtT, bias_rows, alog_rows, rope_tabs, *, rows, mod_row_fn, segs, seq_blocks):
    t, d = xf.shape
    rope = rope_tabs is not None
    in_specs = [
        pl.BlockSpec((rows, d), lambda i: (i, 0)),
        pl.BlockSpec((1, 1, d), lambda i: (mod_row_fn(i), 0, 0)),
        pl.BlockSpec((1, 1, d), lambda i: (mod_row_fn(i), 0, 1)),
        pl.BlockSpec((1, d), lambda i: (0, 0)),
        pl.BlockSpec(w_cat.shape, lambda i: (0, 0)),
        pl.BlockSpec(w_dtT.shape, lambda i: (0, 0)),
        pl.BlockSpec(bias_rows.shape, lambda i: (0, 0)),
        pl.BlockSpec(alog_rows.shape, lambda i: (0, 0)),
    ]
    args = [xf, mod3, mod3, norm1, w_cat, w_dtT, bias_rows, alog_rows]
    if rope:
        in_specs += [pl.BlockSpec((rows, LANES), lambda i: (i % seq_blocks, 0))] * 2
        args += list(rope_tabs)
    out_specs, out_shapes = [], []
    for kind, _, width in segs:
        if kind == "dtT":
            for _ in range(3):
                out_specs.append(pl.BlockSpec((rows // CHUNK, 2 * SSD_HEADS, CHUNK), lambda i: (i, 0, 0)))
                out_shapes.append(jax.ShapeDtypeStruct((t // CHUNK, 2 * SSD_HEADS, CHUNK), F32))
        else:
            dt = BF16 if kind in ("q", "k", "v") else F32
            out_specs.append(pl.BlockSpec((rows, width), lambda i: (i, 0)))
            out_shapes.append(jax.ShapeDtypeStruct((t, width), dt))
    return pl.pallas_call(
        functools.partial(_inproj_kernel, segs=tuple(segs), rope=rope),
        grid=(t // rows,),
        in_specs=in_specs,
        out_specs=out_specs,
        out_shape=out_shapes,
        name="inproj_rope" if rope else "inproj_ctx",
    )(*args)


_XB_COLS = SSD_INNER + SSD_GROUPS * SSD_STATE
_Y_OFF = CONV_DIM - _XB_COLS


def _ssd_kernel(*refs, need_y, nsteps, cps):
    (xbc_ref, prev_ref, next_ref, dt_ref, pre_ref, suf_ref, z_ref, cw_ref, cb_ref,
     dskip_ref, gain_ref, exp_ref, h0f_ref, h0b_ref) = refs[:14]
    if need_y:
        y_ref, hf_out, hb_out = refs[14:17]
        rest = refs[17:]
    else:
        y_ref = None
        hf_out, hb_out = refs[14:16]
        rest = refs[16:]
    cwin, u_ref, hf, hb, hbs, ust = rest

    sw = pl.program_id(1)
    s = pl.program_id(2)
    bi = jnp.where(sw == 0, nsteps - 1 - s, s)
    rows = xbc_ref.shape[1]
    h = SSD_HEADS

    @pl.when((sw == 0) & (s == 0))
    def _():
        hb[...] = h0b_ref[0]

    @pl.when((sw == 1) & (s == 0))
    def _():
        hf[...] = h0f_ref[0]

    ri = lax.broadcasted_iota(jnp.int32, (CHUNK, CHUNK), 0)
    ci = lax.broadcasted_iota(jnp.int32, (CHUNK, CHUNK), 1)
    lane128 = lax.broadcasted_iota(jnp.int32, (CHUNK, LANES), 1)
    lo_mask = lane128 < SSD_HEAD_DIM

    def conv_silu(c, c0, c1, out_ref):
        r0 = pl.multiple_of(c * CHUNK, CHUNK)
        above = xbc_ref[0, pl.ds(pl.multiple_of(jnp.maximum(r0 - SUBLANES, 0), SUBLANES), SUBLANES), c0:c1]
        halo_a = jnp.where(bi > 0, prev_ref[0, :, c0:c1], 0.0)
        cwin[0:SUBLANES, c0:c1] = jnp.where(c > 0, above, halo_a)
        cwin[SUBLANES:SUBLANES + CHUNK, c0:c1] = xbc_ref[0, pl.ds(r0, CHUNK), c0:c1]
        below = xbc_ref[0, pl.ds(pl.multiple_of(jnp.minimum(r0 + CHUNK, rows - SUBLANES), SUBLANES), SUBLANES), c0:c1]
        halo_b = jnp.where(bi < nsteps - 1, next_ref[0, :, c0:c1], 0.0)
        cwin[SUBLANES + CHUNK:, c0:c1] = jnp.where(c < cps - 1, below, halo_b)
        for j0 in range(c0, c1, LANES):
            acc = jnp.broadcast_to(cb_ref[:, j0:j0 + LANES], (CHUNK, LANES))
            for k in range(CONV_WIDTH):
                off = SUBLANES - CONV_WIDTH // 2 + k
                acc = acc + cwin[off:off + CHUNK, j0:j0 + LANES] * cw_ref[k:k + 1, j0:j0 + LANES]
            out_ref[:, j0 - c0:j0 - c0 + LANES] = _silu(acc)

    def dt_rows(c):
        return dt_ref[c], pre_ref[c], suf_ref[c]

    def state_update(state_ref, xb, w_rows, tot_col):
        dec = jnp.exp(jnp.sum(tot_col * exp_ref[...], axis=0, keepdims=True))
        for g in range(SSD_GROUPS):
            bm = xb[:, SSD_INNER + g * SSD_STATE:SSD_INNER + (g + 1) * SSD_STATE]
            bt = bm.T
            for pr in range(2):
                h0 = g * 4 + pr * 2
                c0 = h0 * SSD_HEAD_DIM
                xp = xb[:, c0:c0 + LANES]
                rhs = jnp.concatenate([jnp.where(lo_mask, xp, 0.0), jnp.where(lo_mask, 0.0, xp)],
                                      axis=0).astype(BF16)
                lhs = jnp.concatenate([bt * w_rows[h0:h0 + 1, :], bt * w_rows[h0 + 1:h0 + 2, :]],
                                      axis=1).astype(BF16)
                sres = jnp.dot(lhs, rhs, preferred_element_type=F32)
                state_ref[:, c0:c0 + LANES] = state_ref[:, c0:c0 + LANES] * dec[:, c0:c0 + LANES] + sres

    def col_forms(rows16):
        padded = jnp.concatenate([rows16, jnp.zeros((CHUNK - 2 * h, LANES), F32)], axis=0)
        return padded.T

    def backward_chunk(k, carry):
        c = cps - 1 - k
        gc = bi * cps + c
        xb = ust.at[gc]
        conv_silu(c, 0, _XB_COLS, xb)
        dt, _, suf = dt_rows(c)
        hbs[gc] = hb[...].astype(BF16)
        s0 = suf[:, 0:1]
        w_rows = jnp.exp(s0 - suf) * dt
        state_update(hb, xb, w_rows[h:2 * h, :], s0[h:2 * h, :])
        return carry

    def forward_chunk(c, carry):
        gc = bi * cps + c
        r0 = pl.multiple_of(c * CHUNK, CHUNK)
        xb = ust.at[gc]
        conv_silu(c, _XB_COLS, CONV_DIM, u_ref)
        dt, pre, suf = dt_rows(c)
        rowsf = jnp.concatenate([pre[0:h, :], suf[h:2 * h, :]], axis=0)
        colsf = col_forms(rowsf)
        ecol = jnp.exp(colsf)
        for g in range(SSD_GROUPS):
            bm = xb[:, SSD_INNER + g * SSD_STATE:SSD_INNER + (g + 1) * SSD_STATE]
            cm = u_ref[:, g * SSD_STATE:(g + 1) * SSD_STATE]
            cmb = cm.astype(BF16)
            cbm = lax.dot_general(cmb, bm.astype(BF16), (((1,), (1,)), ((), ())),
                                  preferred_element_type=F32)
            for pr in range(2):
                h0 = g * 4 + pr * 2
                c0 = h0 * SSD_HEAD_DIM
                lhs_parts = []
                for hh in (h0, h0 + 1):
                    segf = colsf[:, hh:hh + 1] - rowsf[hh:hh + 1, :]
                    gf = jnp.exp(jnp.where(ci <= ri, segf, NEG)) * dt[hh:hh + 1, :]
                    segb = colsf[:, h + hh:h + hh + 1] - rowsf[h + hh:h + hh + 1, :]
                    gb = jnp.exp(jnp.where(ci >= ri, segb, NEG)) * dt[h + hh:h + hh + 1, :]
                    lhs_parts.append((cbm * (gf + gb)).astype(BF16))
                for hh in (h0, h0 + 1):
                    lhs_parts.append((cm * ecol[:, hh:hh + 1]).astype(BF16))
                for hh in (h0, h0 + 1):
                    lhs_parts.append((cm * ecol[:, h + hh:h + hh + 1]).astype(BF16))
                lhs = jnp.concatenate(lhs_parts, axis=1)
                xp = xb[:, c0:c0 + LANES]
                sf = hf[:, c0:c0 + LANES]
                sb = hbs[gc, :, c0:c0 + LANES].astype(F32)
                rhs = jnp.concatenate(
                    [jnp.where(lo_mask, xp, 0.0), jnp.where(lo_mask, 0.0, xp),
                     jnp.where(lo_mask, sf, 0.0), jnp.where(lo_mask, 0.0, sf),
                     jnp.where(lo_mask, sb, 0.0), jnp.where(lo_mask, 0.0, sb)], axis=0).astype(BF16)
                ypair = jnp.dot(lhs, rhs, preferred_element_type=F32)
                ypair = ypair + dskip_ref[:, c0:c0 + LANES] * xp
                zz = z_ref[0, pl.ds(r0, CHUNK), c0:c0 + LANES]
                u_ref[:, _Y_OFF + c0:_Y_OFF + c0 + LANES] = ypair * _silu(zz)
        last = pre[:, LANES - 1:LANES]
        w_rows = jnp.exp(last - pre) * dt
        state_update(hf, xb, w_rows[0:h, :], last[0:h, :])
        if need_y:
            yv = u_ref[:, _Y_OFF:_Y_OFF + SSD_INNER]
            ms = jnp.mean(yv * yv, axis=-1, keepdims=True)
            y_ref[0, pl.ds(r0, CHUNK), :] = (yv * lax.rsqrt(ms + EPS) * gain_ref[...]).astype(y_ref.dtype)
        return carry

    @pl.when(sw == 0)
    def _():
        lax.fori_loop(0, cps, backward_chunk, 0)

    @pl.when(sw == 1)
    def _():
        lax.fori_loop(0, cps, forward_chunk, 0)

    @pl.when((sw == 0) & (s == nsteps - 1))
    def _():
        hb_out[0] = hb[...]

    @pl.when((sw == 1) & (s == nsteps - 1))
    def _():
        hf_out[0] = hf[...]


def _ssd(xbc, dts, z, consts, h0f, h0b, *, rows, need_y):
    bt, seq, _ = xbc.shape
    nsteps = seq // rows
    cps = rows // CHUNK
    hb8 = rows // SUBLANES
    nb8 = seq // SUBLANES
    cw, cb, dskip, gain, expm = consts
    dt_spec = pl.BlockSpec((cps, 2 * SSD_HEADS, CHUNK), lambda b, sw, s: (b * nsteps + blk(b, sw, s), 0, 0))

    def blk(b, sw, s):
        return jnp.where(sw == 0, nsteps - 1 - s, s)

    def full(a):
        return pl.BlockSpec(a.shape, lambda b, sw, s: (0,) * a.ndim)

    in_specs = [
        pl.BlockSpec((1, rows, CONV_DIM), lambda b, sw, s: (b, blk(b, sw, s), 0)),
        pl.BlockSpec((1, SUBLANES, CONV_DIM), lambda b, sw, s: (b, jnp.maximum(blk(b, sw, s) * hb8 - 1, 0), 0)),
        pl.BlockSpec((1, SUBLANES, CONV_DIM),
                     lambda b, sw, s: (b, jnp.minimum((blk(b, sw, s) + 1) * hb8, nb8 - 1), 0)),
        dt_spec, dt_spec, dt_spec,
        pl.BlockSpec((1, rows, SSD_INNER), lambda b, sw, s: (b, jnp.where(sw == 0, 0, s), 0)),
        full(cw), full(cb), full(dskip), full(gain), full(expm),
        pl.BlockSpec((1, SSD_STATE, SSD_INNER), lambda b, sw, s: (b, 0, 0)),
        pl.BlockSpec((1, SSD_STATE, SSD_INNER), lambda b, sw, s: (b, 0, 0)),
    ]
    st_spec = pl.BlockSpec((1, SSD_STATE, SSD_INNER), lambda b, sw, s: (b, 0, 0))
    st_shape = jax.ShapeDtypeStruct((bt, SSD_STATE, SSD_INNER), F32)
    out_specs, out_shapes = [st_spec, st_spec], [st_shape, st_shape]
    if need_y:
        out_specs = [pl.BlockSpec((1, rows, SSD_INNER), lambda b, sw, s: (b, jnp.where(sw == 0, 0, s), 0))] + out_specs
        out_shapes = [jax.ShapeDtypeStruct((bt, seq, SSD_INNER), BF16)] + out_shapes
    scratch = [
        pltpu.VMEM((CHUNK + 2 * SUBLANES, CONV_DIM), F32),
        pltpu.VMEM((CHUNK, _Y_OFF + SSD_INNER), F32),
        pltpu.VMEM((SSD_STATE, SSD_INNER), F32),
        pltpu.VMEM((SSD_STATE, SSD_INNER), F32),
        pltpu.VMEM((seq // CHUNK, SSD_STATE, SSD_INNER), BF16),
        pltpu.VMEM((seq // CHUNK, CHUNK, _XB_COLS), F32),
    ]
    return pl.pallas_call(
        functools.partial(_ssd_kernel, need_y=need_y, nsteps=nsteps, cps=cps),
        grid=(bt, 2, nsteps),
        in_specs=in_specs,
        out_specs=out_specs,
        out_shape=out_shapes,
        scratch_shapes=scratch,
        compiler_params=pltpu.CompilerParams(dimension_semantics=("arbitrary", "arbitrary", "arbitrary")),
        name="ssd_y" if need_y else "ssd_ctx",
    )(xbc, xbc, xbc, *dts, z, cw, cb, dskip, gain, expm, h0f, h0b)


def _attn_kernel(sink_ref, q_ref, kp_ref, kc_ref, kn_ref, vp_ref, vc_ref, vn_ref, kx_ref, vx_ref, o_ref, *, nsteps):
    m = pl.program_id(1)
    lane = lax.broadcasted_iota(jnp.int32, (1, LANES), 1)
    lo = lane < HEAD_DIM
    rq = Q_PER_KV * WINDOW
    qi = lax.broadcasted_iota(jnp.int32, (rq, WINDOW), 0) & (WINDOW - 1)
    kj = lax.broadcasted_iota(jnp.int32, (rq, WINDOW), 1)
    rblk = lax.shift_right_logical(lax.broadcasted_iota(jnp.int32, (rq, 1), 0), int(math.log2(WINDOW)))
    zero = jnp.zeros((), BF16)

    kpieces = [kp_ref[0], kc_ref[0, 0:WINDOW], kc_ref[0, WINDOW:2 * WINDOW], kn_ref[0]]
    vpieces = [vp_ref[0], vc_ref[0, 0:WINDOW], vc_ref[0, WINDOW:2 * WINDOW], vn_ref[0]]
    kx = kx_ref[0]
    vx = vx_ref[0]
    nt = (((1,), (1,)), ((), ()))

    for sb in range(ROWS_ATT // WINDOW):
        qsb = q_ref[0, sb * WINDOW:(sb + 1) * WINDOW, :]
        prev_ok = jnp.logical_or(m > 0, sb > 0)
        next_ok = jnp.logical_or(m < nsteps - 1, sb < ROWS_ATT // WINDOW - 1)
        outs = [None] * Q_PER_KV
        for kv in range(ATTN_KV_HEADS):
            sel = lo if kv == 0 else jnp.logical_not(lo)
            qs = jnp.concatenate(
                [jnp.where(sel, qsb[:, j * LANES:(j + 1) * LANES], zero) for j in range(Q_PER_KV)], axis=0)
            s_prev = lax.dot_general(qs, kpieces[sb], nt, preferred_element_type=F32)
            s_cur = lax.dot_general(qs, kpieces[sb + 1], nt, preferred_element_type=F32)
            s_next = lax.dot_general(qs, kpieces[sb + 2], nt, preferred_element_type=F32)
            s_ctx = lax.dot_general(qs, kx, nt, preferred_element_type=F32)
            s_prev = jnp.where((kj >= qi) & prev_ok, s_prev, NEG)
            s_next = jnp.where((kj <= qi) & next_ok, s_next, NEG)
            sink = jnp.zeros((rq, 1), F32)
            for j in range(Q_PER_KV):
                sink = jnp.where(rblk == j, sink_ref[kv * Q_PER_KV + j], sink)
            mx = jnp.maximum(
                jnp.maximum(jnp.max(s_prev, axis=-1, keepdims=True), jnp.max(s_cur, axis=-1, keepdims=True)),
                jnp.maximum(jnp.max(s_next, axis=-1, keepdims=True), jnp.max(s_ctx, axis=-1, keepdims=True)))
            mx = jnp.maximum(mx, sink)
            p_prev = jnp.exp(s_prev - mx)
            p_cur = jnp.exp(s_cur - mx)
            p_next = jnp.exp(s_next - mx)
            p_ctx = jnp.exp(s_ctx - mx)
            den = (jnp.sum(p_prev, axis=-1, keepdims=True) + jnp.sum(p_cur, axis=-1, keepdims=True)
                   + jnp.sum(p_next, axis=-1, keepdims=True) + jnp.sum(p_ctx, axis=-1, keepdims=True)
                   + jnp.exp(sink - mx))
            acc = jnp.dot(p_prev.astype(BF16), jnp.where(sel, vpieces[sb], zero), preferred_element_type=F32)
            acc += jnp.dot(p_cur.astype(BF16), jnp.where(sel, vpieces[sb + 1], zero), preferred_element_type=F32)
            acc += jnp.dot(p_next.astype(BF16), jnp.where(sel, vpieces[sb + 2], zero), preferred_element_type=F32)
            acc += jnp.dot(p_ctx.astype(BF16), jnp.where(sel, vx, zero), preferred_element_type=F32)
            acc = acc * (1.0 / den)
            for j in range(Q_PER_KV):
                part = acc[j * WINDOW:(j + 1) * WINDOW, :]
                outs[j] = part if outs[j] is None else outs[j] + part
        for j in range(Q_PER_KV):
            o_ref[0, sb * WINDOW:(sb + 1) * WINDOW, j * LANES:(j + 1) * LANES] = outs[j].astype(o_ref.dtype)


def _attention(sinks, q, k, v, kx, vx):
    bt, seq, _ = q.shape
    nsteps = seq // ROWS_ATT
    per = ROWS_ATT // WINDOW
    nb = seq // WINDOW
    n_ctx = kx.shape[1]

    def prev(b, m, s):
        return (b, jnp.maximum(m * per - 1, 0), 0)

    def cur(b, m, s):
        return (b, m, 0)

    def nxt(b, m, s):
        return (b, jnp.minimum((m + 1) * per, nb - 1), 0)

    small = (1, WINDOW, KV_WIDTH)
    big = (1, ROWS_ATT, KV_WIDTH)
    grid_spec = pltpu.PrefetchScalarGridSpec(
        num_scalar_prefetch=1,
        grid=(bt, nsteps),
        in_specs=[
            pl.BlockSpec((1, ROWS_ATT, ATTN_WIDTH), cur),
            pl.BlockSpec(small, prev), pl.BlockSpec(big, cur), pl.BlockSpec(small, nxt),
            pl.BlockSpec(small, prev), pl.BlockSpec(big, cur), pl.BlockSpec(small, nxt),
            pl.BlockSpec((1, n_ctx, KV_WIDTH), lambda b, m, s: (b, 0, 0)),
            pl.BlockSpec((1, n_ctx, KV_WIDTH), lambda b, m, s: (b, 0, 0)),
        ],
        out_specs=pl.BlockSpec((1, ROWS_ATT, ATTN_WIDTH), cur),
    )
    return pl.pallas_call(
        functools.partial(_attn_kernel, nsteps=nsteps),
        grid_spec=grid_spec,
        out_shape=jax.ShapeDtypeStruct((bt, seq, ATTN_WIDTH), BF16),
        name="attn",
    )(sinks, q, k, k, k, v, v, v, kx, vx)


def _outproj_kernel(ssd_ref, att_ref, x_ref, g1_ref, sh_ref, sc_ref, n2_ref, wa_ref, wb_ref, wr_ref, br_ref,
                    tri_ref, x1_ref, h2_ref, ids_ref, wcol_ref, cnt_ref, carry, *, nsteps):
    i = pl.program_id(0)
    rows = x_ref.shape[0]

    @pl.when(i == 0)
    def _():
        carry[...] = jnp.zeros_like(carry)

    acc = jnp.dot(ssd_ref[...], wa_ref[...], preferred_element_type=F32)
    acc += jnp.dot(att_ref[...], wb_ref[...], preferred_element_type=F32)
    x1 = x_ref[...] + g1_ref[0] * acc
    x1_ref[...] = x1
    ms = jnp.mean(x1 * x1, axis=-1, keepdims=True)
    h2 = (x1 * lax.rsqrt(ms + EPS)) * (n2_ref[...] * (1.0 + sc_ref[0])) + sh_ref[0]
    for sidx in range(d_tiles := h2.shape[1] // LANES):
        h2_ref[pl.ds(sidx, rows, stride=d_tiles), :] = h2[:, sidx * LANES:(sidx + 1) * LANES]
    lt = lax.dot_general(wr_ref[...], h2.astype(BF16), (((1,), (1,)), ((), ())),
                         preferred_element_type=F32) + br_ref[...]
    row = lax.broadcasted_iota(jnp.int32, lt.shape, 0).astype(F32)
    big_i = float(ROUTE_ROWS)
    gl = jnp.where(row < N_GROUPS, lt, NEG)
    gmax = jnp.max(gl, axis=0, keepdims=True)
    gidx = jnp.min(jnp.where(gl == gmax, row, big_i), axis=0, keepdims=True)
    g_w = 1.0 / jnp.sum(jnp.exp(gl - gmax), axis=0, keepdims=True)
    lo = N_GROUPS + EXPERTS_PER_GROUP * gidx
    el = jnp.where((row >= lo) & (row < lo + EXPERTS_PER_GROUP), lt, NEG)
    m1 = jnp.max(el, axis=0, keepdims=True)
    i1 = jnp.min(jnp.where(el == m1, row, big_i), axis=0, keepdims=True)
    el2 = jnp.where(row == i1, NEG, el)
    m2 = jnp.max(el2, axis=0, keepdims=True)
    i2 = jnp.min(jnp.where(el2 == m2, row, big_i), axis=0, keepdims=True)
    r = jnp.exp(m2 - m1)
    w1 = g_w / (1.0 + r)
    w2 = g_w * r / (1.0 + r)
    oh1 = row == i1
    oh2 = row == i2
    cnt = jnp.where(oh1 | oh2, 1.0, 0.0)
    prefix = jnp.dot(cnt.astype(BF16), tri_ref[...], preferred_element_type=F32)
    base = carry[...] + prefix
    rank1 = jnp.sum(jnp.where(oh1, base, 0.0), axis=0, keepdims=True)
    rank2 = jnp.sum(jnp.where(oh2, base, 0.0), axis=0, keepdims=True)
    carry[...] = carry[...] + jnp.sum(cnt, axis=1, keepdims=True)
    r8 = lax.broadcasted_iota(jnp.int32, (SUBLANES, rows), 0)
    code1 = (i1 - N_GROUPS) * float(1 << RANK_BITS) + rank1
    code2 = (i2 - N_GROUPS) * float(1 << RANK_BITS) + rank2
    ids_ref[...] = jnp.where(r8 == 0, code1, jnp.where(r8 == 1, code2, 0.0)).astype(jnp.int32)
    r128 = lax.broadcasted_iota(jnp.int32, (LANES, rows), 0)
    wfull = jnp.where(r128 == 0, w1, jnp.where(r128 == 1, w2, 0.0))
    for j in range(rows // LANES):
        wcol_ref[j * LANES:(j + 1) * LANES, :] = wfull[:, j * LANES:(j + 1) * LANES].T

    @pl.when(i == nsteps - 1)
    def _():
        cnt_ref[...] = carry[:, 0:LANES]


def _outproj(ssd, att, xf, mod3, norm2, wa, wb, wr, br, tri, *, rows, seq_blocks):
    t, d = xf.shape
    nsteps = t // rows

    def mod(col):
        return pl.BlockSpec((1, 1, d), lambda i: (i // seq_blocks, 0, col))

    def full(a):
        return pl.BlockSpec(a.shape, lambda i: (0,) * a.ndim)

    return pl.pallas_call(
        functools.partial(_outproj_kernel, nsteps=nsteps),
        grid=(nsteps,),
        in_specs=[pl.BlockSpec((rows, SSD_INNER), lambda i: (i, 0)),
                  pl.BlockSpec((rows, ATTN_WIDTH), lambda i: (i, 0)),
                  pl.BlockSpec((rows, d), lambda i: (i, 0)),
                  mod(2), mod(3), mod(4), full(norm2), full(wa), full(wb), full(wr), full(br), full(tri)],
        out_specs=[pl.BlockSpec((rows, d), lambda i: (i, 0)),
                   pl.BlockSpec((rows * (d // LANES), LANES), lambda i: (i, 0)),
                   pl.BlockSpec((SUBLANES, rows), lambda i: (0, i)),
                   pl.BlockSpec((rows, LANES), lambda i: (i, 0)),
                   pl.BlockSpec((ROUTE_ROWS, LANES), lambda i: (0, 0))],
        out_shape=[jax.ShapeDtypeStruct((t, d), F32),
                   jax.ShapeDtypeStruct((t * (d // LANES), LANES), F32),
                   jax.ShapeDtypeStruct((SUBLANES, t), jnp.int32),
                   jax.ShapeDtypeStruct((t, LANES), F32),
                   jax.ShapeDtypeStruct((ROUTE_ROWS, LANES), F32)],
        scratch_shapes=[pltpu.VMEM((ROUTE_ROWS, rows), F32)],
        compiler_params=pltpu.CompilerParams(dimension_semantics=("arbitrary",)),
        name="outproj_router",
    )(ssd, att, xf, mod3, mod3, mod3, norm2, wa, wb, wr, br, tri)


def _expert_kernel(blk_e_ref, nused_ref, pstart_ref, counts_ref, dst_ref, h2_hbm, wg_ref, wu_ref, wd_ref, y_ref,
                   xbuf, wgb, wub, wdb, tok_ref, sem, *, n_tok, nblocks):
    b = pl.program_id(0)
    nused = nused_ref[0]
    slot = lax.rem(b, GATHER_SLOTS)

    def row_copy(tok, r, sl):
        return pltpu.make_async_copy(h2_hbm.at[pl.ds(tok * D_TILES, D_TILES)],
                                     xbuf.at[sl, pl.ds(r * D_TILES, D_TILES)], sem.at[sl])

    def start_rows(blk, sl, r0, r1):
        for r in range(r0, r1):
            row_copy(tok_ref[blk * MOE_BLK + r], r, sl).start()

    def wait_rows(sl):
        for r in range(MOE_BLK):
            row_copy(0, r, sl).wait()

    @pl.when(b == 0)
    def _():
        def pad_expert(e, carry):
            lo = pstart_ref[e] + counts_ref[e]
            hi = pstart_ref[e] + (counts_ref[e] + MOE_BLK - 1) // MOE_BLK * MOE_BLK

            def pad_slot(s, c):
                tok_ref[s] = 0
                return c
            return lax.fori_loop(lo, hi, pad_slot, carry)
        lax.fori_loop(0, N_EXPERTS, pad_expert, 0)

        def claim(i, carry):
            tok_ref[dst_ref[i]] = jnp.where(i >= n_tok, i - n_tok, i)
            return carry
        lax.fori_loop(0, 2 * n_tok, claim, 0, unroll=16)
        start_rows(0, 0, 0, MOE_BLK)
        start_rows(jnp.minimum(1, nused - 1), 1, 0, MOE_BLK)

    e = blk_e_ref[b]
    e_prev = blk_e_ref[jnp.maximum(b - 1, 0)]

    @pl.when((b == 0) | (e != e_prev))
    def _():
        wgb[...] = wg_ref[0].astype(BF16)
        wub[...] = wu_ref[0].astype(BF16)
        wdb[...] = wd_ref[0].astype(BF16)

    nxt = jnp.minimum(b + 2, nused - 1)
    nxt_slot = lax.rem(b + 2, GATHER_SLOTS)
    wait_rows(slot)
    xb = jnp.concatenate([xbuf[slot, pl.ds(s, MOE_BLK, stride=D_TILES), :] for s in range(D_TILES)],
                         axis=1).astype(BF16)
    gate = jnp.dot(xb, wgb[...], preferred_element_type=F32)
    up = jnp.dot(xb, wub[...], preferred_element_type=F32)
    hmid = (_silu(gate) * up).astype(BF16)
    y = jnp.dot(hmid, wdb[...], preferred_element_type=F32)
    start_rows(nxt, nxt_slot, 0, MOE_BLK)
    for s in range(D_TILES):
        y_ref[pl.ds(s, MOE_BLK, stride=D_TILES), :] = y[:, s * LANES:(s + 1) * LANES]

    @pl.when(b == nblocks - 1)
    def _():
        wait_rows(lax.rem(b + 1, GATHER_SLOTS))
        wait_rows(nxt_slot)


def _experts(blk_e, nused, pstart, counts, dst, h2t, w_gate, w_up, w_down, *, nblocks, n_tok):
    d = w_gate.shape[1]

    def wspec(shape):
        return pl.BlockSpec(shape, lambda b, be, nu, ps, ct, cd: (be[b], 0, 0))

    grid_spec = pltpu.PrefetchScalarGridSpec(
        num_scalar_prefetch=5,
        grid=(nblocks,),
        in_specs=[pl.BlockSpec(memory_space=pl.ANY),
                  wspec((1, d, EXPERT_DIM)), wspec((1, d, EXPERT_DIM)), wspec((1, EXPERT_DIM, d))],
        out_specs=pl.BlockSpec((MOE_BLK * D_TILES, LANES), lambda b, be, nu, ps, ct, cd: (b, 0)),
        scratch_shapes=[
            pltpu.VMEM((GATHER_SLOTS, MOE_BLK * D_TILES, LANES), F32),
            pltpu.VMEM((d, EXPERT_DIM), BF16),
            pltpu.VMEM((d, EXPERT_DIM), BF16),
            pltpu.VMEM((EXPERT_DIM, d), BF16),
            pltpu.SMEM((nblocks * MOE_BLK,), jnp.int32),
            pltpu.SemaphoreType.DMA((GATHER_SLOTS,)),
        ],
    )
    return pl.pallas_call(
        functools.partial(_expert_kernel, n_tok=n_tok, nblocks=nblocks),
        grid_spec=grid_spec,
        out_shape=jax.ShapeDtypeStruct((nblocks * MOE_BLK * D_TILES, LANES), F32),
        compiler_params=pltpu.CompilerParams(dimension_semantics=("arbitrary",)),
        name="experts",
    )(blk_e, nused, pstart, counts, dst, h2t, w_gate, w_up, w_down)


def _combine_kernel(dst_ref, y_hbm, x1_ref, wcol_ref, g2_ref, nf_ref, o_ref, ybuf, sem, *, nsteps, n_tok):
    i = pl.program_id(0)
    rows = x1_ref.shape[0]
    slot = i % 2

    def row_copy(src, r, k, sl):
        return pltpu.make_async_copy(y_hbm.at[pl.ds(src * D_TILES, D_TILES)],
                                     ybuf.at[sl, k, pl.ds(r * D_TILES, D_TILES)], sem.at[sl])

    def start_rows(step, sl, r0, r1):
        for r in range(r0, r1):
            for k in range(2):
                row_copy(dst_ref[k * n_tok + step * rows + r], r, k, sl).start()

    def wait_rows(sl):
        for r in range(rows):
            for k in range(2):
                row_copy(0, r, k, sl).wait()

    @pl.when(i == 0)
    def _():
        start_rows(0, 0, 0, rows)

    @pl.when(i + 1 < nsteps)
    def _():
        start_rows(i + 1, 1 - slot, 0, rows)

    wait_rows(slot)
    w0 = wcol_ref[:, 0:1]
    w1 = wcol_ref[:, 1:2]
    ssq = jnp.zeros((rows, 1), F32)
    for s in range(D_TILES):
        cols = slice(s * LANES, (s + 1) * LANES)
        moe = (ybuf[slot, 0, pl.ds(s, rows, stride=D_TILES), :] * w0
               + ybuf[slot, 1, pl.ds(s, rows, stride=D_TILES), :] * w1)
        x2 = x1_ref[:, cols] + g2_ref[0, :, cols] * moe
        ssq = ssq + jnp.sum(x2 * x2, axis=-1, keepdims=True)
        o_ref[:, cols] = x2
    inv = lax.rsqrt(ssq * (1.0 / (D_TILES * LANES)) + EPS)
    o_ref[...] = o_ref[...] * inv * nf_ref[...]


def _combine(dst, y_buf, x1, wcol, mod3, norm_final, *, rows, seq_blocks):
    t, d = x1.shape
    nsteps = t // rows
    grid_spec = pltpu.PrefetchScalarGridSpec(
        num_scalar_prefetch=1,
        grid=(nsteps,),
        in_specs=[
            pl.BlockSpec(memory_space=pl.ANY),
            pl.BlockSpec((rows, d), lambda i, ds: (i, 0)),
            pl.BlockSpec((rows, LANES), lambda i, ds: (i, 0)),
            pl.BlockSpec((1, 1, d), lambda i, ds: (i // seq_blocks, 0, 5)),
            pl.BlockSpec((1, d), lambda i, ds: (0, 0)),
        ],
        out_specs=pl.BlockSpec((rows, d), lambda i, ds: (i, 0)),
        scratch_shapes=[pltpu.VMEM((2, 2, rows * D_TILES, LANES), F32), pltpu.SemaphoreType.DMA((2,))],
    )
    return pl.pallas_call(
        functools.partial(_combine_kernel, nsteps=nsteps, n_tok=t),
        grid_spec=grid_spec,
        out_shape=jax.ShapeDtypeStruct((t, d), F32),
        compiler_params=pltpu.CompilerParams(dimension_semantics=("arbitrary",)),
        name="combine",
    )(dst, y_buf, x1, wcol, mod3, norm_final)


def _rope_tables(seq):
    pos = np.arange(seq)
    n_freq = HEAD_DIM // 4
    inv = ROPE_BASE ** (-np.arange(n_freq, dtype=np.float32) / n_freq)
    ang = np.concatenate([(pos // GRID_W)[:, None] * inv, (pos % GRID_W)[:, None] * inv], axis=-1)
    ang = np.concatenate([ang, ang, ang, ang], axis=-1).astype(np.float32)
    return jnp.asarray(np.cos(ang), F32), jnp.asarray(np.sin(ang), F32)


def _head_expand_mat():
    m = np.zeros((SSD_HEADS, SSD_INNER), np.float32)
    for hh in range(SSD_HEADS):
        m[hh, hh * SSD_HEAD_DIM:(hh + 1) * SSD_HEAD_DIM] = 1.0
    return jnp.asarray(m)


def kernel(x, c, ctx, c_ctx, w_ada, b_ada, norm1, w_in, conv_w, conv_b, dt_bias, a_log, d_skip, ssd_norm,
           attn_sinks, w_out, norm2, w_group, b_group, w_expert, b_expert, w_gate, w_up, w_down, norm_final):
    return _pipeline(x, c, ctx, c_ctx, w_ada, b_ada, norm1, w_in, conv_w, conv_b, dt_bias, a_log, d_skip,
                     ssd_norm, attn_sinks, w_out, norm2, w_group, b_group, w_expert, b_expert, w_gate, w_up,
                     w_down, norm_final)["out"]


def _pipeline(x, c, ctx, c_ctx, w_ada, b_ada, norm1, w_in, conv_w, conv_b, dt_bias, a_log, d_skip, ssd_norm,
              attn_sinks, w_out, norm2, w_group, b_group, w_expert, b_expert, w_gate, w_up, w_down, norm_final):
    bt, seq, d = x.shape
    n_ctx = ctx.shape[1]
    t = bt * seq
    layer = 0
    assert w_ada.shape[0] == 1 and seq % ROWS_IN == 0 and n_ctx % CHUNK == 0

    cc = jnp.zeros((SUBLANES, d), F32).at[:bt].set(c).at[bt].set(c_ctx)
    mod = _ada(cc, w_ada[layer], b_ada[layer][None, :])
    mod3 = mod.reshape(SUBLANES, 1, 6 * d)

    w = w_in[layer]
    o_z, o_xbc, o_dt = 0, SSD_INNER, SSD_INNER + CONV_DIM
    o_q = o_dt + 2 * SSD_HEADS
    o_k, o_v = o_q + ATTN_WIDTH, o_q + ATTN_WIDTH + KV_WIDTH
    q_perm = np.concatenate([np.r_[j * HEAD_DIM:(j + 1) * HEAD_DIM, (j + 4) * HEAD_DIM:(j + 5) * HEAD_DIM]
                             for j in range(Q_PER_KV)])
    w_q = w[:, o_q:o_q + ATTN_WIDTH].reshape(d, ATTN_KV_HEADS, Q_PER_KV, HEAD_DIM).transpose(0, 2, 1, 3)
    w_q = w_q.reshape(d, ATTN_WIDTH)
    w_cat = jnp.concatenate([w[:, o_z:o_z + SSD_INNER], w[:, o_xbc:o_xbc + CONV_DIM], w_q,
                             w[:, o_k:o_k + KV_WIDTH], w[:, o_v:o_v + KV_WIDTH]], axis=1).astype(BF16)
    w_dtT = w[:, o_dt:o_dt + 2 * SSD_HEADS].T.astype(BF16)

    seq_blocks = seq // ROWS_IN
    cos, sin = _rope_tables(seq)
    bias_rows = jnp.broadcast_to(dt_bias[layer].reshape(2 * SSD_HEADS, 1), (2 * SSD_HEADS, LANES))
    alog_rows = jnp.broadcast_to(a_log[layer].reshape(2 * SSD_HEADS, 1), (2 * SSD_HEADS, LANES))
    segs_x = [("z", _COL_Z, SSD_INNER), ("xbc", _COL_XBC, CONV_DIM), ("dtT", 0, 0),
              ("q", _COL_Q, ATTN_WIDTH), ("k", _COL_K, KV_WIDTH), ("v", _COL_V, KV_WIDTH)]
    z_x, xbc_x, dt_x, pre_x, suf_x, q_x, k_x, v_x = _inproj(
        x.reshape(t, d), mod3, norm1[layer][None, :], w_cat, w_dtT, bias_rows, alog_rows, (cos, sin),
        rows=ROWS_IN, mod_row_fn=lambda i: i // seq_blocks, segs=segs_x, seq_blocks=seq_blocks)
    segs_c = [("xbc", _COL_XBC, CONV_DIM), ("dtT", 0, 0), ("k", _COL_K, KV_WIDTH), ("v", _COL_V, KV_WIDTH)]
    xbc_c, dt_c, pre_c, suf_c, k_c, v_c = _inproj(
        ctx.reshape(bt * n_ctx, d), mod3, norm1[layer][None, :], w_cat, w_dtT, bias_rows, alog_rows, None,
        rows=n_ctx, mod_row_fn=lambda i: bt, segs=segs_c, seq_blocks=1)

    cw = jnp.zeros((SUBLANES, CONV_DIM), F32).at[:CONV_WIDTH].set(conv_w[layer])
    cb = conv_b[layer][None, :]
    dskip_e = jnp.repeat(d_skip[layer], SSD_HEAD_DIM)[None, :]
    consts = (cw, cb, dskip_e, ssd_norm[layer][None, :], _head_expand_mat())

    zeros_state = jnp.zeros((bt, SSD_STATE, SSD_INNER), F32)
    z_dummy = jnp.zeros((bt, n_ctx, SSD_INNER), F32)
    h_cf, h_cb = _ssd(xbc_c.reshape(bt, n_ctx, CONV_DIM), (dt_c, pre_c, suf_c), z_dummy, consts,
                      zeros_state, zeros_state, rows=n_ctx, need_y=False)
    ssd_x, _, _ = _ssd(xbc_x.reshape(bt, seq, CONV_DIM), (dt_x, pre_x, suf_x), z_x.reshape(bt, seq, SSD_INNER),
                       consts, h_cf, h_cb, rows=ROWS_SSD, need_y=True)

    attn_x = _attention(attn_sinks[layer].astype(F32),
                        q_x.reshape(bt, seq, ATTN_WIDTH), k_x.reshape(bt, seq, KV_WIDTH),
                        v_x.reshape(bt, seq, KV_WIDTH), k_c.reshape(bt, n_ctx, KV_WIDTH),
                        v_c.reshape(bt, n_ctx, KV_WIDTH))

    wo = w_out[layer]
    wa = wo[:SSD_INNER].astype(BF16)
    wb = wo[SSD_INNER:].reshape(ATTN_KV_HEADS, Q_PER_KV, HEAD_DIM, d).transpose(1, 0, 2, 3)
    wb = wb.reshape(ATTN_WIDTH, d).astype(BF16)
    wr = jnp.zeros((ROUTE_ROWS, d), F32).at[:N_GROUPS].set(w_group[layer].T)
    wr = wr.at[N_GROUPS:N_GROUPS + N_EXPERTS].set(w_expert[layer].T).astype(BF16)
    br = jnp.zeros((ROUTE_ROWS,), F32).at[:N_GROUPS].set(b_group[layer])
    br = br.at[N_GROUPS:N_GROUPS + N_EXPERTS].set(b_expert[layer])
    br = jnp.broadcast_to(br[:, None], (ROUTE_ROWS, ROWS_IN))
    tri = jnp.asarray(np.triu(np.ones((ROWS_IN, ROWS_IN), np.float32), 1), BF16)
    x1, h2, ids, wcol, cnt = _outproj(
        ssd_x.reshape(t, SSD_INNER), attn_x.reshape(t, ATTN_WIDTH), x.reshape(t, d), mod3, norm2[layer][None, :],
        wa, wb, wr, br, tri, rows=ROWS_IN, seq_blocks=seq_blocks)

    counts = cnt[N_GROUPS:N_GROUPS + N_EXPERTS, 0].astype(jnp.int32)
    padded = (counts + MOE_BLK - 1) // MOE_BLK * MOE_BLK
    ends = jnp.cumsum(padded)
    pstart = ends - padded
    nblocks = (2 * t) // MOE_BLK + N_EXPERTS
    blk_start = jnp.arange(nblocks, dtype=jnp.int32) * MOE_BLK
    blk_e = jnp.minimum(jnp.sum((ends[None, :] <= blk_start[:, None]).astype(jnp.int32), axis=1), N_EXPERTS - 1)
    nused = (ends[-1] // MOE_BLK).astype(jnp.int32).reshape(1)
    code = ids[0:2].reshape(-1)
    e_id = lax.shift_right_logical(code, RANK_BITS)
    onehot = e_id[:, None] == jnp.arange(N_EXPERTS, dtype=jnp.int32)[None, :]
    dst = jnp.sum(jnp.where(onehot, pstart[None, :], 0), axis=1) + (code & ((1 << RANK_BITS) - 1))

    y_buf = _experts(blk_e, nused, pstart, counts, dst, h2, w_gate[layer], w_up[layer], w_down[layer],
                     nblocks=nblocks, n_tok=t)
    out = _combine(dst, y_buf, x1, wcol, mod3, norm_final[None, :], rows=ROWS_CMB, seq_blocks=seq // ROWS_CMB)
    return dict(out=out.reshape(bt, seq, d), mod=mod, z=z_x, xbc=xbc_x, dt=dt_x, q=q_x, k=k_x, v=v_x,
                k_c=k_c, v_c=v_c, h_cf=h_cf, h_cb=h_cb, ssd=ssd_x, attn=attn_x, x1=x1, h2=h2, ids=ids,
                wcol=wcol, cnt=cnt, y_buf=y_buf, pstart=pstart, q_perm=q_perm)
```

```python
import functools
import math

import numpy as np
import jax
import jax.numpy as jnp
from jax import lax
from jax.experimental import pallas as pl
from jax.experimental.pallas import tpu as pltpu

F32 = jnp.float32
BF16 = jnp.bfloat16

D_MODEL = 1024
GRID_W = 64
EPS = 1e-6
SSD_INNER = 512
SSD_HEAD_DIM = 64
SSD_HEADS = 8
SSD_GROUPS = 2
SSD_STATE = 128
CONV_WIDTH = 5
CONV_DIM = SSD_INNER + 2 * SSD_GROUPS * SSD_STATE
CHUNK = 128
ATTN_WIDTH = 512
HEAD_DIM = 64
ATTN_Q_HEADS = 8
ATTN_KV_HEADS = 2
Q_PER_KV = 4
KV_WIDTH = 128
WINDOW = 128
ROPE_BASE = 10000.0
N_GROUPS = 4
EXPERTS_PER_GROUP = 8
N_EXPERTS = 32
EXPERT_DIM = 512

LANES = 128
SUBLANES = 8
NEG = -1e30

ROWS_IN = 512
ROWS_SSD = 512
ROWS_ATT = 256
ROUTE_ROWS = 48
MOE_BLK = 256
GATHER_SLOTS = 3
RANK_BITS = 16
D_TILES = D_MODEL // LANES
ROWS_CMB = 128


def _silu(v):
    return v * (1.0 / (1.0 + jnp.exp(-v)))


def _softplus(v):
    return jnp.maximum(v, 0.0) + jnp.log(1.0 + jnp.exp(-jnp.abs(v)))


def _ada_kernel(c_ref, w_ref, b_ref, o_ref):
    a = _silu(c_ref[...])
    o_ref[...] = jnp.dot(a.astype(BF16), w_ref[...].astype(BF16),
                         preferred_element_type=F32) + b_ref[...]


def _ada(cc, w_ada, b_ada):
    d, n = w_ada.shape
    tn = 1536
    return pl.pallas_call(
        _ada_kernel,
        grid=(n // tn,),
        in_specs=[pl.BlockSpec((SUBLANES, d), lambda j: (0, 0)),
                  pl.BlockSpec((d, tn), lambda j: (0, j)),
                  pl.BlockSpec((1, tn), lambda j: (0, j))],
        out_specs=pl.BlockSpec((SUBLANES, tn), lambda j: (0, j)),
        out_shape=jax.ShapeDtypeStruct((SUBLANES, n), F32),
        name="ada",
    )(cc, w_ada, b_ada)


def _rope(t, cos, sin_signed, first_half):
    rot = jnp.where(first_half, pltpu.roll(t, LANES - HEAD_DIM // 2, 1), pltpu.roll(t, HEAD_DIM // 2, 1))
    return t * cos + rot * sin_signed


def _lane_cumsum(v, lane, reverse):
    k = 1
    while k < LANES:
        if reverse:
            v = v + jnp.where(lane < LANES - k, pltpu.roll(v, LANES - k, 1), 0.0)
        else:
            v = v + jnp.where(lane >= k, pltpu.roll(v, k, 1), 0.0)
        k *= 2
    return v


def _inproj_kernel(*refs, segs, rope):
    x_ref, sh_ref, sc_ref, g_ref, w_ref, wdt_ref, bias_ref, alog_ref = refs[:8]
    pos = 8
    if rope:
        cos_ref, sin_ref = refs[8:10]
        pos = 10
    outs = iter(refs[pos:])
    x = x_ref[...]
    ms = jnp.mean(x * x, axis=-1, keepdims=True)
    h = (x * lax.rsqrt(ms + EPS)) * (g_ref[...] * (1.0 + sc_ref[0])) + sh_ref[0]
    hb = h.astype(BF16)
    if rope:
        cos = cos_ref[...]
        lane = lax.broadcasted_iota(jnp.int32, cos.shape, 1)
        first_half = (lane & (HEAD_DIM - 1)) < HEAD_DIM // 2
        sin_s = jnp.where(first_half, -sin_ref[...], sin_ref[...])
    for kind, c0, width in segs:
        if kind == "dtT":
            dt_ref, pre_ref, suf_ref = next(outs), next(outs), next(outs)
            dtt = lax.dot_general(wdt_ref[...], hb, (((1,), (1,)), ((), ())), preferred_element_type=F32)
            lane16 = lax.broadcasted_iota(jnp.int32, (2 * SSD_HEADS, CHUNK), 1)
            neg_a = -jnp.exp(alog_ref[...])
            for j in range(dt_ref.shape[0]):
                dt = _softplus(dtt[:, j * CHUNK:(j + 1) * CHUNK] + bias_ref[...])
                a = dt * neg_a
                dt_ref[j] = dt
                pre_ref[j] = _lane_cumsum(a, lane16, False)
                suf_ref[j] = _lane_cumsum(a, lane16, True)
            continue
        o_ref = next(outs)
        for j0 in range(0, width, 512):
            wj = min(512, width - j0)
            acc = jnp.dot(hb, w_ref[:, c0 + j0:c0 + j0 + wj], preferred_element_type=F32)
            if rope and kind in ("q", "k"):
                scale = HEAD_DIM ** -0.5 if kind == "q" else 1.0
                for l0 in range(0, wj, LANES):
                    t = _rope(acc[:, l0:l0 + LANES], cos, sin_s, first_half)
                    o_ref[:, j0 + l0:j0 + l0 + LANES] = (t * scale).astype(o_ref.dtype)
            else:
                o_ref[:, j0:j0 + wj] = acc.astype(o_ref.dtype)


_COL_Z, _COL_XBC, _COL_Q, _COL_K, _COL_V = 0, 512, 1536, 2048, 2176
_W_COLS = 2304


def _inproj(xf, mod3, norm1, w_cat, w_dtT, bias_rows, alog_rows, rope_tabs, *, rows, mod_row_fn, segs, seq_blocks):
    t, d = xf.shape
    rope = rope_tabs is not None
    in_specs = [
        pl.BlockSpec((rows, d), lambda i: (i, 0)),
        pl.BlockSpec((1, 1, d), lambda i: (mod_row_fn(i), 0, 0)),
        pl.BlockSpec((1, 1, d), lambda i: (mod_row_fn(i), 0, 1)),
        pl.BlockSpec((1, d), lambda i: (0, 0)),
        pl.BlockSpec(w_cat.shape, lambda i: (0, 0)),
        pl.BlockSpec(w_dtT.shape, lambda i: (0, 0)),
        pl.BlockSpec(bias_rows.shape, lambda i: (0, 0)),
        pl.BlockSpec(alog_rows.shape, lambda i: (0, 0)),
    ]
    args = [xf, mod3, mod3, norm1, w_cat, w_dtT, bias_rows, alog_rows]
    if rope:
        in_specs += [pl.BlockSpec((rows, LANES), lambda i: (i % seq_blocks, 0))] * 2
        args += list(rope_tabs)
    out_specs, out_shapes = [], []
    for kind, _, width in segs:
        if kind == "dtT":
            for _ in range(3):
                out_specs.append(pl.BlockSpec((rows // CHUNK, 2 * SSD_HEADS, CHUNK), lambda i: (i, 0, 0)))
                out_shapes.append(jax.ShapeDtypeStruct((t // CHUNK, 2 * SSD_HEADS, CHUNK), F32))
        else:
            dt = BF16 if kind in ("q", "k", "v") else F32
            out_specs.append(pl.BlockSpec((rows, width), lambda i: (i, 0)))
            out_shapes.append(jax.ShapeDtypeStruct((t, width), dt))
    return pl.pallas_call(
        functools.partial(_inproj_kernel, segs=tuple(segs), rope=rope),
        grid=(t // rows,),
        in_specs=in_specs,
        out_specs=out_specs,
        out_shape=out_shapes,
        name="inproj_rope" if rope else "inproj_ctx",
    )(*args)


_XB_COLS = SSD_INNER + SSD_GROUPS * SSD_STATE
_Y_OFF = CONV_DIM - _XB_COLS


def _ssd_kernel(*refs, need_y, nsteps, cps):
    (xbc_ref, prev_ref, next_ref, dt_ref, pre_ref, suf_ref, z_ref, cw_ref, cb_ref,
     dskip_ref, gain_ref, exp_ref, h0f_ref, h0b_ref) = refs[:14]
    if need_y:
        y_ref, hf_out, hb_out = refs[14:17]
        rest = refs[17:]
    else:
        y_ref = None
        hf_out, hb_out = refs[14:16]
        rest = refs[16:]
    cwin, u_ref, hf, hb, hbs, ust = rest

    sw = pl.program_id(1)
    s = pl.program_id(2)
    bi = jnp.where(sw == 0, nsteps - 1 - s, s)
    rows = xbc_ref.shape[1]
    h = SSD_HEADS

    @pl.when((sw == 0) & (s == 0))
    def _():
        hb[...] = h0b_ref[0]

    @pl.when((sw == 1) & (s == 0))
    def _():
        hf[...] = h0f_ref[0]

    ri = lax.broadcasted_iota(jnp.int32, (CHUNK, CHUNK), 0)
    ci = lax.broadcasted_iota(jnp.int32, (CHUNK, CHUNK), 1)
    lane128 = lax.broadcasted_iota(jnp.int32, (CHUNK, LANES), 1)
    lo_mask = lane128 < SSD_HEAD_DIM

    def conv_silu(c, c0, c1, out_ref):
        r0 = pl.multiple_of(c * CHUNK, CHUNK)
        above = xbc_ref[0, pl.ds(pl.multiple_of(jnp.maximum(r0 - SUBLANES, 0), SUBLANES), SUBLANES), c0:c1]
        halo_a = jnp.where(bi > 0, prev_ref[0, :, c0:c1], 0.0)
        cwin[0:SUBLANES, c0:c1] = jnp.where(c > 0, above, halo_a)
        cwin[SUBLANES:SUBLANES + CHUNK, c0:c1] = xbc_ref[0, pl.ds(r0, CHUNK), c0:c1]
        below = xbc_ref[0, pl.ds(pl.multiple_of(jnp.minimum(r0 + CHUNK, rows - SUBLANES), SUBLANES), SUBLANES), c0:c1]
        halo_b = jnp.where(bi < nsteps - 1, next_ref[0, :, c0:c1], 0.0)
        cwin[SUBLANES + CHUNK:, c0:c1] = jnp.where(c < cps - 1, below, halo_b)
        for j0 in range(c0, c1, LANES):
            acc = jnp.broadcast_to(cb_ref[:, j0:j0 + LANES], (CHUNK, LANES))
            for k in range(CONV_WIDTH):
                off = SUBLANES - CONV_WIDTH // 2 + k
                acc = acc + cwin[off:off + CHUNK, j0:j0 + LANES] * cw_ref[k:k + 1, j0:j0 + LANES]
            out_ref[:, j0 - c0:j0 - c0 + LANES] = _silu(acc)

    def dt_rows(c):
        return dt_ref[c], pre_ref[c], suf_ref[c]

    def state_update(state_ref, xb, w_rows, tot_col):
        dec = jnp.exp(jnp.sum(tot_col * exp_ref[...], axis=0, keepdims=True))
        for g in range(SSD_GROUPS):
            bm = xb[:, SSD_INNER + g * SSD_STATE:SSD_INNER + (g + 1) * SSD_STATE]
            bt = bm.T
            for pr in range(2):
                h0 = g * 4 + pr * 2
                c0 = h0 * SSD_HEAD_DIM
                xp = xb[:, c0:c0 + LANES]
                rhs = jnp.concatenate([jnp.where(lo_mask, xp, 0.0), jnp.where(lo_mask, 0.0, xp)],
                                      axis=0).astype(BF16)
                lhs = jnp.concatenate([bt * w_rows[h0:h0 + 1, :], bt * w_rows[h0 + 1:h0 + 2, :]],
                                      axis=1).astype(BF16)
                sres = jnp.dot(lhs, rhs, preferred_element_type=F32)
                state_ref[:, c0:c0 + LANES] = state_ref[:, c0:c0 + LANES] * dec[:, c0:c0 + LANES] + sres

    def col_forms(rows16):
        padded = jnp.concatenate([rows16, jnp.zeros((CHUNK - 2 * h, LANES), F32)], axis=0)
        return padded.T

    def backward_chunk(k, carry):
        c = cps - 1 - k
        gc = bi * cps + c
        xb = ust.at[gc]
        conv_silu(c, 0, _XB_COLS, xb)
        dt, _, suf = dt_rows(c)
        hbs[gc] = hb[...].astype(BF16)
        s0 = suf[:, 0:1]
        w_rows = jnp.exp(s0 - suf) * dt
        state_update(hb, xb, w_rows[h:2 * h, :], s0[h:2 * h, :])
        return carry

    def forward_chunk(c, carry):
        gc = bi * cps + c
        r0 = pl.multiple_of(c * CHUNK, CHUNK)
        xb = ust.at[gc]
        conv_silu(c, _XB_COLS, CONV_DIM, u_ref)
        dt, pre, suf = dt_rows(c)
        rowsf = jnp.concatenate([pre[0:h, :], suf[h:2 * h, :]], axis=0)
        colsf = col_forms(rowsf)
        ecol = jnp.exp(colsf)
        for g in range(SSD_GROUPS):
            bm = xb[:, SSD_INNER + g * SSD_STATE:SSD_INNER + (g + 1) * SSD_STATE]
            cm = u_ref[:, g * SSD_STATE:(g + 1) * SSD_STATE]
            cmb = cm.astype(BF16)
            cbm = lax.dot_general(cmb, bm.astype(BF16), (((1,), (1,)), ((), ())),
                                  preferred_element_type=F32)
            for pr in range(2):
                h0 = g * 4 + pr * 2
                c0 = h0 * SSD_HEAD_DIM
                lhs_parts = []
                for hh in (h0, h0 + 1):
                    segf = colsf[:, hh:hh + 1] - rowsf[hh:hh + 1, :]
                    gf = jnp.exp(jnp.where(ci <= ri, segf, NEG)) * dt[hh:hh + 1, :]
                    segb = colsf[:, h + hh:h + hh + 1] - rowsf[h + hh:h + hh + 1, :]
                    gb = jnp.exp(jnp.where(ci >= ri, segb, NEG)) * dt[h + hh:h + hh + 1, :]
                    lhs_parts.append((cbm * (gf + gb)).astype(BF16))
                for hh in (h0, h0 + 1):
                    lhs_parts.append((cm * ecol[:, hh:hh + 1]).astype(BF16))
                for hh in (h0, h0 + 1):
                    lhs_parts.append((cm * ecol[:, h + hh:h + hh + 1]).astype(BF16))
                lhs = jnp.concatenate(lhs_parts, axis=1)
                xp = xb[:, c0:c0 + LANES]
                sf = hf[:, c0:c0 + LANES]
                sb = hbs[gc, :, c0:c0 + LANES].astype(F32)
                rhs = jnp.concatenate(
                    [jnp.where(lo_mask, xp, 0.0), jnp.where(lo_mask, 0.0, xp),
                     jnp.where(lo_mask, sf, 0.0), jnp.where(lo_mask, 0.0, sf),
                     jnp.where(lo_mask, sb, 0.0), jnp.where(lo_mask, 0.0, sb)], axis=0).astype(BF16)
                ypair = jnp.dot(lhs, rhs, preferred_element_type=F32)
                ypair = ypair + dskip_ref[:, c0:c0 + LANES] * xp
                zz = z_ref[0, pl.ds(r0, CHUNK), c0:c0 + LANES]
                u_ref[:, _Y_OFF + c0:_Y_OFF + c0 + LANES] = ypair * _silu(zz)
        last = pre[:, LANES - 1:LANES]
        w_rows = jnp.exp(last - pre) * dt
        state_update(hf, xb, w_rows[0:h, :], last[0:h, :])
        if need_y:
            yv = u_ref[:, _Y_OFF:_Y_OFF + SSD_INNER]
            ms = jnp.mean(yv * yv, axis=-1, keepdims=True)
            y_ref[0, pl.ds(r0, CHUNK), :] = (yv * lax.rsqrt(ms + EPS) * gain_ref[...]).astype(y_ref.dtype)
        return carry

    @pl.when(sw == 0)
    def _():
        lax.fori_loop(0, cps, backward_chunk, 0)

    @pl.when(sw == 1)
    def _():
        lax.fori_loop(0, cps, forward_chunk, 0)

    @pl.when((sw == 0) & (s == nsteps - 1))
    def _():
        hb_out[0] = hb[...]

    @pl.when((sw == 1) & (s == nsteps - 1))
    def _():
        hf_out[0] = hf[...]


def _ssd(xbc, dts, z, consts, h0f, h0b, *, rows, need_y):
    bt, seq, _ = xbc.shape
    nsteps = seq // rows
    cps = rows // CHUNK
    hb8 = rows // SUBLANES
    nb8 = seq // SUBLANES
    cw, cb, dskip, gain, expm = consts
    dt_spec = pl.BlockSpec((cps, 2 * SSD_HEADS, CHUNK), lambda b, sw, s: (b * nsteps + blk(b, sw, s), 0, 0))

    def blk(b, sw, s):
        return jnp.where(sw == 0, nsteps - 1 - s, s)

    def full(a):
        return pl.BlockSpec(a.shape, lambda b, sw, s: (0,) * a.ndim)

    in_specs = [
        pl.BlockSpec((1, rows, CONV_DIM), lambda b, sw, s: (b, blk(b, sw, s), 0)),
        pl.BlockSpec((1, SUBLANES, CONV_DIM), lambda b, sw, s: (b, jnp.maximum(blk(b, sw, s) * hb8 - 1, 0), 0)),
        pl.BlockSpec((1, SUBLANES, CONV_DIM),
                     lambda b, sw, s: (b, jnp.minimum((blk(b, sw, s) + 1) * hb8, nb8 - 1), 0)),
        dt_spec, dt_spec, dt_spec,
        pl.BlockSpec((1, rows, SSD_INNER), lambda b, sw, s: (b, jnp.where(sw == 0, 0, s), 0)),
        full(cw), full(cb), full(dskip), full(gain), full(expm),
        pl.BlockSpec((1, SSD_STATE, SSD_INNER), lambda b, sw, s: (b, 0, 0)),
        pl.BlockSpec((1, SSD_STATE, SSD_INNER), lambda b, sw, s: (b, 0, 0)),
    ]
    st_spec = pl.BlockSpec((1, SSD_STATE, SSD_INNER), lambda b, sw, s: (b, 0, 0))
    st_shape = jax.ShapeDtypeStruct((bt, SSD_STATE, SSD_INNER), F32)
    out_specs, out_shapes = [st_spec, st_spec], [st_shape, st_shape]
    if need_y:
        out_specs = [pl.BlockSpec((1, rows, SSD_INNER), lambda b, sw, s: (b, jnp.where(sw == 0, 0, s), 0))] + out_specs
        out_shapes = [jax.ShapeDtypeStruct((bt, seq, SSD_INNER), BF16)] + out_shapes
    scratch = [
        pltpu.VMEM((CHUNK + 2 * SUBLANES, CONV_DIM), F32),
        pltpu.VMEM((CHUNK, _Y_OFF + SSD_INNER), F32),
        pltpu.VMEM((SSD_STATE, SSD_INNER), F32),
        pltpu.VMEM((SSD_STATE, SSD_INNER), F32),
        pltpu.VMEM((seq // CHUNK, SSD_STATE, SSD_INNER), BF16),
        pltpu.VMEM((seq // CHUNK, CHUNK, _XB_COLS), F32),
    ]
    return pl.pallas_call(
        functools.partial(_ssd_kernel, need_y=need_y, nsteps=nsteps, cps=cps),
        grid=(bt, 2, nsteps),
        in_specs=in_specs,
        out_specs=out_specs,
        out_shape=out_shapes,
        scratch_shapes=scratch,
        compiler_params=pltpu.CompilerParams(dimension_semantics=("arbitrary", "arbitrary", "arbitrary")),
        name="ssd_y" if need_y else "ssd_ctx",
    )(xbc, xbc, xbc, *dts, z, cw, cb, dskip, gain, expm, h0f, h0b)


def _attn_kernel(sink_ref, q_ref, kp_ref, kc_ref, kn_ref, vp_ref, vc_ref, vn_ref, kx_ref, vx_ref, o_ref, *, nsteps):
    m = pl.program_id(1)
    lane = lax.broadcasted_iota(jnp.int32, (1, LANES), 1)
    lo = lane < HEAD_DIM
    rq = Q_PER_KV * WINDOW
    qi = lax.broadcasted_iota(jnp.int32, (rq, WINDOW), 0) & (WINDOW - 1)
    kj = lax.broadcasted_iota(jnp.int32, (rq, WINDOW), 1)
    rblk = lax.shift_right_logical(lax.broadcasted_iota(jnp.int32, (rq, 1), 0), int(math.log2(WINDOW)))
    zero = jnp.zeros((), BF16)

    kpieces = [kp_ref[0], kc_ref[0, 0:WINDOW], kc_ref[0, WINDOW:2 * WINDOW], kn_ref[0]]
    vpieces = [vp_ref[0], vc_ref[0, 0:WINDOW], vc_ref[0, WINDOW:2 * WINDOW], vn_ref[0]]
    kx = kx_ref[0]
    vx = vx_ref[0]
    nt = (((1,), (1,)), ((), ()))

    for sb in range(ROWS_ATT // WINDOW):
        qsb = q_ref[0, sb * WINDOW:(sb + 1) * WINDOW, :]
        prev_ok = jnp.logical_or(m > 0, sb > 0)
        next_ok = jnp.logical_or(m < nsteps - 1, sb < ROWS_ATT // WINDOW - 1)
        outs = [None] * Q_PER_KV
        for kv in range(ATTN_KV_HEADS):
            sel = lo if kv == 0 else jnp.logical_not(lo)
            qs = jnp.concatenate(
                [jnp.where(sel, qsb[:, j * LANES:(j + 1) * LANES], zero) for j in range(Q_PER_KV)], axis=0)
            s_prev = lax.dot_general(qs, kpieces[sb], nt, preferred_element_type=F32)
            s_cur = lax.dot_general(qs, kpieces[sb + 1], nt, preferred_element_type=F32)
            s_next = lax.dot_general(qs, kpieces[sb + 2], nt, preferred_element_type=F32)
            s_ctx = lax.dot_general(qs, kx, nt, preferred_element_type=F32)
            s_prev = jnp.where((kj >= qi) & prev_ok, s_prev, NEG)
            s_next = jnp.where((kj <= qi) & next_ok, s_next, NEG)
            sink = jnp.zeros((rq, 1), F32)
            for j in range(Q_PER_KV):
                sink = jnp.where(rblk == j, sink_ref[kv * Q_PER_KV + j], sink)
            mx = jnp.maximum(
                jnp.maximum(jnp.max(s_prev, axis=-1, keepdims=True), jnp.max(s_cur, axis=-1, keepdims=True)),
                jnp.maximum(jnp.max(s_next, axis=-1, keepdims=True), jnp.max(s_ctx, axis=-1, keepdims=True)))
            mx = jnp.maximum(mx, sink)
            p_prev = jnp.exp(s_prev - mx)
            p_cur = jnp.exp(s_cur - mx)
            p_next = jnp.exp(s_next - mx)
            p_ctx = jnp.exp(s_ctx - mx)
            den = (jnp.sum(p_prev, axis=-1, keepdims=True) + jnp.sum(p_cur, axis=-1, keepdims=True)
                   + jnp.sum(p_next, axis=-1, keepdims=True) + jnp.sum(p_ctx, axis=-1, keepdims=True)
                   + jnp.exp(sink - mx))
            acc = jnp.dot(p_prev.astype(BF16), jnp.where(sel, vpieces[sb], zero), preferred_element_type=F32)
            acc += jnp.dot(p_cur.astype(BF16), jnp.where(sel, vpieces[sb + 1], zero), preferred_element_type=F32)
            acc += jnp.dot(p_next.astype(BF16), jnp.where(sel, vpieces[sb + 2], zero), preferred_element_type=F32)
            acc += jnp.dot(p_ctx.astype(BF16), jnp.where(sel, vx, zero), preferred_element_type=F32)
            acc = acc * (1.0 / den)
            for j in range(Q_PER_KV):
                part = acc[j * WINDOW:(j + 1) * WINDOW, :]
                outs[j] = part if outs[j] is None else outs[j] + part
        for j in range(Q_PER_KV):
            o_ref[0, sb * WINDOW:(sb + 1) * WINDOW, j * LANES:(j + 1) * LANES] = outs[j].astype(o_ref.dtype)


def _attention(sinks, q, k, v, kx, vx):
    bt, seq, _ = q.shape
    nsteps = seq // ROWS_ATT
    per = ROWS_ATT // WINDOW
    nb = seq // WINDOW
    n_ctx = kx.shape[1]

    def prev(b, m, s):
        return (b, jnp.maximum(m * per - 1, 0), 0)

    def cur(b, m, s):
        return (b, m, 0)

    def nxt(b, m, s):
        return (b, jnp.minimum((m + 1) * per, nb - 1), 0)

    small = (1, WINDOW, KV_WIDTH)
    big = (1, ROWS_ATT, KV_WIDTH)
    grid_spec = pltpu.PrefetchScalarGridSpec(
        num_scalar_prefetch=1,
        grid=(bt, nsteps),
        in_specs=[
            pl.BlockSpec((1, ROWS_ATT, ATTN_WIDTH), cur),
            pl.BlockSpec(small, prev), pl.BlockSpec(big, cur), pl.BlockSpec(small, nxt),
            pl.BlockSpec(small, prev), pl.BlockSpec(big, cur), pl.BlockSpec(small, nxt),
            pl.BlockSpec((1, n_ctx, KV_WIDTH), lambda b, m, s: (b, 0, 0)),
            pl.BlockSpec((1, n_ctx, KV_WIDTH), lambda b, m, s: (b, 0, 0)),
        ],
        out_specs=pl.BlockSpec((1, ROWS_ATT, ATTN_WIDTH), cur),
    )
    return pl.pallas_call(
        functools.partial(_attn_kernel, nsteps=nsteps),
        grid_spec=grid_spec,
        out_shape=jax.ShapeDtypeStruct((bt, seq, ATTN_WIDTH), BF16),
        name="attn",
    )(sinks, q, k, k, k, v, v, v, kx, vx)


def _outproj_kernel(ssd_ref, att_ref, x_ref, g1_ref, sh_ref, sc_ref, n2_ref, wa_ref, wb_ref, wr_ref, br_ref,
                    tri_ref, x1_ref, h2_ref, ids_ref, wcol_ref, cnt_ref, carry, *, nsteps):
    i = pl.program_id(0)
    rows = x_ref.shape[0]

    @pl.when(i == 0)
    def _():
        carry[...] = jnp.zeros_like(carry)

    acc = jnp.dot(ssd_ref[...], wa_ref[...], preferred_element_type=F32)
    acc += jnp.dot(att_ref[...], wb_ref[...], preferred_element_type=F32)
    x1 = x_ref[...] + g1_ref[0] * acc
    x1_ref[...] = x1
    ms = jnp.mean(x1 * x1, axis=-1, keepdims=True)
    h2 = (x1 * lax.rsqrt(ms + EPS)) * (n2_ref[...] * (1.0 + sc_ref[0])) + sh_ref[0]
    for sidx in range(d_tiles := h2.shape[1] // LANES):
        h2_ref[pl.ds(sidx, rows, stride=d_tiles), :] = h2[:, sidx * LANES:(sidx + 1) * LANES]
    lt = lax.dot_general(wr_ref[...], h2.astype(BF16), (((1,), (1,)), ((), ())),
                         preferred_element_type=F32) + br_ref[...]
    row = lax.broadcasted_iota(jnp.int32, lt.shape, 0).astype(F32)
    big_i = float(ROUTE_ROWS)
    gl = jnp.where(row < N_GROUPS, lt, NEG)
    gmax = jnp.max(gl, axis=0, keepdims=True)
    gidx = jnp.min(jnp.where(gl == gmax, row, big_i), axis=0, keepdims=True)
    g_w = 1.0 / jnp.sum(jnp.exp(gl - gmax), axis=0, keepdims=True)
    lo = N_GROUPS + EXPERTS_PER_GROUP * gidx
    el = jnp.where((row >= lo) & (row < lo + EXPERTS_PER_GROUP), lt, NEG)
    m1 = jnp.max(el, axis=0, keepdims=True)
    i1 = jnp.min(jnp.where(el == m1, row, big_i), axis=0, keepdims=True)
    el2 = jnp.where(row == i1, NEG, el)
    m2 = jnp.max(el2, axis=0, keepdims=True)
    i2 = jnp.min(jnp.where(el2 == m2, row, big_i), axis=0, keepdims=True)
    r = jnp.exp(m2 - m1)
    w1 = g_w / (1.0 + r)
    w2 = g_w * r / (1.0 + r)
    oh1 = row == i1
    oh2 = row == i2
    cnt = jnp.where(oh1 | oh2, 1.0, 0.0)
    prefix = jnp.dot(cnt.astype(BF16), tri_ref[...], preferred_element_type=F32)
    base = carry[...] + prefix
    rank1 = jnp.sum(jnp.where(oh1, base, 0.0), axis=0, keepdims=True)
    rank2 = jnp.sum(jnp.where(oh2, base, 0.0), axis=0, keepdims=True)
    carry[...] = carry[...] + jnp.sum(cnt, axis=1, keepdims=True)
    r8 = lax.broadcasted_iota(jnp.int32, (SUBLANES, rows), 0)
    code1 = (i1 - N_GROUPS) * float(1 << RANK_BITS) + rank1
    code2 = (i2 - N_GROUPS) * float(1 << RANK_BITS) + rank2
    ids_ref[...] = jnp.where(r8 == 0, code1, jnp.where(r8 == 1, code2, 0.0)).astype(jnp.int32)
    r128 = lax.broadcasted_iota(jnp.int32, (LANES, rows), 0)
    wfull = jnp.where(r128 == 0, w1, jnp.where(r128 == 1, w2, 0.0))
    for j in range(rows // LANES):
        wcol_ref[j * LANES:(j + 1) * LANES, :] = wfull[:, j * LANES:(j + 1) * LANES].T

    @pl.when(i == nsteps - 1)
    def _():
        cnt_ref[...] = carry[:, 0:LANES]


def _outproj(ssd, att, xf, mod3, norm2, wa, wb, wr, br, tri, *, rows, seq_blocks):
    t, d = xf.shape
    nsteps = t // rows

    def mod(col):
        return pl.BlockSpec((1, 1, d), lambda i: (i // seq_blocks, 0, col))

    def full(a):
        return pl.BlockSpec(a.shape, lambda i: (0,) * a.ndim)

    return pl.pallas_call(
        functools.partial(_outproj_kernel, nsteps=nsteps),
        grid=(nsteps,),
        in_specs=[pl.BlockSpec((rows, SSD_INNER), lambda i: (i, 0)),
                  pl.BlockSpec((rows, ATTN_WIDTH), lambda i: (i, 0)),
                  pl.BlockSpec((rows, d), lambda i: (i, 0)),
                  mod(2), mod(3), mod(4), full(norm2), full(wa), full(wb), full(wr), full(br), full(tri)],
        out_specs=[pl.BlockSpec((rows, d), lambda i: (i, 0)),
                   pl.BlockSpec((rows * (d // LANES), LANES), lambda i: (i, 0)),
                   pl.BlockSpec((SUBLANES, rows), lambda i: (0, i)),
                   pl.BlockSpec((rows, LANES), lambda i: (i, 0)),
                   pl.BlockSpec((ROUTE_ROWS, LANES), lambda i: (0, 0))],
        out_shape=[jax.ShapeDtypeStruct((t, d), F32),
                   jax.ShapeDtypeStruct((t * (d // LANES), LANES), F32),
                   jax.ShapeDtypeStruct((SUBLANES, t), jnp.int32),
                   jax.ShapeDtypeStruct((t, LANES), F32),
                   jax.ShapeDtypeStruct((ROUTE_ROWS, LANES), F32)],
        scratch_shapes=[pltpu.VMEM((ROUTE_ROWS, rows), F32)],
        compiler_params=pltpu.CompilerParams(dimension_semantics=("arbitrary",)),
        name="outproj_router",
    )(ssd, att, xf, mod3, mod3, mod3, norm2, wa, wb, wr, br, tri)


def _expert_kernel(blk_e_ref, nused_ref, pstart_ref, counts_ref, dst_ref, h2_hbm, wg_ref, wu_ref, wd_ref, y_ref,
                   xbuf, wgb, wub, wdb, tok_ref, sem, *, n_tok, nblocks):
    b = pl.program_id(0)
    nused = nused_ref[0]
    slot = lax.rem(b, GATHER_SLOTS)

    def row_copy(tok, r, sl):
        return pltpu.make_async_copy(h2_hbm.at[pl.ds(tok * D_TILES, D_TILES)],
                                     xbuf.at[sl, pl.ds(r * D_TILES, D_TILES)], sem.at[sl])

    def start_rows(blk, sl, r0, r1):
        for r in range(r0, r1):
            row_copy(tok_ref[blk * MOE_BLK + r], r, sl).start()

    def wait_rows(sl):
        for r in range(MOE_BLK):
            row_copy(0, r, sl).wait()

    @pl.when(b == 0)
    def _():
        def pad_expert(e, carry):
            lo = pstart_ref[e] + counts_ref[e]
            hi = pstart_ref[e] + (counts_ref[e] + MOE_BLK - 1) // MOE_BLK * MOE_BLK

            def pad_slot(s, c):
                tok_ref[s] = 0
                return c
            return lax.fori_loop(lo, hi, pad_slot, carry)
        lax.fori_loop(0, N_EXPERTS, pad_expert, 0)

        def claim(i, carry):
            tok_ref[dst_ref[i]] = jnp.where(i >= n_tok, i - n_tok, i)
            return carry
        lax.fori_loop(0, 2 * n_tok, claim, 0, unroll=16)
        start_rows(0, 0, 0, MOE_BLK)
        start_rows(jnp.minimum(1, nused - 1), 1, 0, MOE_BLK)

    e = blk_e_ref[b]
    e_prev = blk_e_ref[jnp.maximum(b - 1, 0)]

    @pl.when((b == 0) | (e != e_prev))
    def _():
        wgb[...] = wg_ref[0].astype(BF16)
        wub[...] = wu_ref[0].astype(BF16)
        wdb[...] = wd_ref[0].astype(BF16)

    nxt = jnp.minimum(b + 2, nused - 1)
    nxt_slot = lax.rem(b + 2, GATHER_SLOTS)
    wait_rows(slot)
    xb = jnp.concatenate([xbuf[slot, pl.ds(s, MOE_BLK, stride=D_TILES), :] for s in range(D_TILES)],
                         axis=1).astype(BF16)
    gate = jnp.dot(xb, wgb[...], preferred_element_type=F32)
    up = jnp.dot(xb, wub[...], preferred_element_type=F32)
    hmid = (_silu(gate) * up).astype(BF16)
    y = jnp.dot(hmid, wdb[...], preferred_element_type=F32)
    start_rows(nxt, nxt_slot, 0, MOE_BLK)
    for s in range(D_TILES):
        y_ref[pl.ds(s, MOE_BLK, stride=D_TILES), :] = y[:, s * LANES:(s + 1) * LANES]

    @pl.when(b == nblocks - 1)
    def _():
        wait_rows(lax.rem(b + 1, GATHER_SLOTS))
        wait_rows(nxt_slot)


def _experts(blk_e, nused, pstart, counts, dst, h2t, w_gate, w_up, w_down, *, nblocks, n_tok):
    d = w_gate.shape[1]

    def wspec(shape):
        return pl.BlockSpec(shape, lambda b, be, nu, ps, ct, cd: (be[b], 0, 0))

    grid_spec = pltpu.PrefetchScalarGridSpec(
        num_scalar_prefetch=5,
        grid=(nblocks,),
        in_specs=[pl.BlockSpec(memory_space=pl.ANY),
                  wspec((1, d, EXPERT_DIM)), wspec((1, d, EXPERT_DIM)), wspec((1, EXPERT_DIM, d))],
        out_specs=pl.BlockSpec((MOE_BLK * D_TILES, LANES), lambda b, be, nu, ps, ct, cd: (b, 0)),
        scratch_shapes=[
            pltpu.VMEM((GATHER_SLOTS, MOE_BLK * D_TILES, LANES), F32),
            pltpu.VMEM((d, EXPERT_DIM), BF16),
            pltpu.VMEM((d, EXPERT_DIM), BF16),
            pltpu.VMEM((EXPERT_DIM, d), BF16),
            pltpu.SMEM((nblocks * MOE_BLK,), jnp.int32),
            pltpu.SemaphoreType.DMA((GATHER_SLOTS,)),
        ],
    )
    return pl.pallas_call(
        functools.partial(_expert_kernel, n_tok=n_tok, nblocks=nblocks),
        grid_spec=grid_spec,
        out_shape=jax.ShapeDtypeStruct((nblocks * MOE_BLK * D_TILES, LANES), F32),
        compiler_params=pltpu.CompilerParams(dimension_semantics=("arbitrary",)),
        name="experts",
    )(blk_e, nused, pstart, counts, dst, h2t, w_gate, w_up, w_down)


def _combine_kernel(dst_ref, y_hbm, x1_ref, wcol_ref, g2_ref, nf_ref, o_ref, ybuf, sem, *, nsteps, n_tok):
    i = pl.program_id(0)
    rows = x1_ref.shape[0]
    slot = i % 2

    def row_copy(src, r, k, sl):
        return pltpu.make_async_copy(y_hbm.at[pl.ds(src * D_TILES, D_TILES)],
                                     ybuf.at[sl, k, pl.ds(r * D_TILES, D_TILES)], sem.at[sl])

    def start_rows(step, sl, r0, r1):
        for r in range(r0, r1):
            for k in range(2):
                row_copy(dst_ref[k * n_tok + step * rows + r], r, k, sl).start()

    def wait_rows(sl):
        for r in range(rows):
            for k in range(2):
                row_copy(0, r, k, sl).wait()

    @pl.when(i == 0)
    def _():
        start_rows(0, 0, 0, rows)

    @pl.when(i + 1 < nsteps)
    def _():
        start_rows(i + 1, 1 - slot, 0, rows)

    wait_rows(slot)
    w0 = wcol_ref[:, 0:1]
    w1 = wcol_ref[:, 1:2]
    ssq = jnp.zeros((rows, 1), F32)
    for s in range(D_TILES):
        cols = slice(s * LANES, (s + 1) * LANES)
        moe = (ybuf[slot, 0, pl.ds(s, rows, stride=D_TILES), :] * w0
               + ybuf[slot, 1, pl.ds(s, rows, stride=D_TILES), :] * w1)
        x2 = x1_ref[:, cols] + g2_ref[0, :, cols] * moe
        ssq = ssq + jnp.sum(x2 * x2, axis=-1, keepdims=True)
        o_ref[:, cols] = x2
    inv = lax.rsqrt(ssq * (1.0 / (D_TILES * LANES)) + EPS)
    o_ref[...] = o_ref[...] * inv * nf_ref[...]


def _combine(dst, y_buf, x1, wcol, mod3, norm_final, *, rows, seq_blocks):
    t, d = x1.shape
    nsteps = t // rows
    grid_spec = pltpu.PrefetchScalarGridSpec(
        num_scalar_prefetch=1,
        grid=(nsteps,),
        in_specs=[
            pl.BlockSpec(memory_space=pl.ANY),
            pl.BlockSpec((rows, d), lambda i, ds: (i, 0)),
            pl.BlockSpec((rows, LANES), lambda i, ds: (i, 0)),
            pl.BlockSpec((1, 1, d), lambda i, ds: (i // seq_blocks, 0, 5)),
            pl.BlockSpec((1, d), lambda i, ds: (0, 0)),
        ],
        out_specs=pl.BlockSpec((rows, d), lambda i, ds: (i, 0)),
        scratch_shapes=[pltpu.VMEM((2, 2, rows * D_TILES, LANES), F32), pltpu.SemaphoreType.DMA((2,))],
    )
    return pl.pallas_call(
        functools.partial(_combine_kernel, nsteps=nsteps, n_tok=t),
        grid_spec=grid_spec,
        out_shape=jax.ShapeDtypeStruct((t, d), F32),
        compiler_params=pltpu.CompilerParams(dimension_semantics=("arbitrary",)),
        name="combine",
    )(dst, y_buf, x1, wcol, mod3, norm_final)


def _rope_tables(seq):
    pos = np.arange(seq)
    n_freq = HEAD_DIM // 4
    inv = ROPE_BASE ** (-np.arange(n_freq, dtype=np.float32) / n_freq)
    ang = np.concatenate([(pos // GRID_W)[:, None] * inv, (pos % GRID_W)[:, None] * inv], axis=-1)
    ang = np.concatenate([ang, ang, ang, ang], axis=-1).astype(np.float32)
    return jnp.asarray(np.cos(ang), F32), jnp.asarray(np.sin(ang), F32)


def _head_expand_mat():
    m = np.zeros((SSD_HEADS, SSD_INNER), np.float32)
    for hh in range(SSD_HEADS):
        m[hh, hh * SSD_HEAD_DIM:(hh + 1) * SSD_HEAD_DIM] = 1.0
    return jnp.asarray(m)


def kernel(x, c, ctx, c_ctx, w_ada, b_ada, norm1, w_in, conv_w, conv_b, dt_bias, a_log, d_skip, ssd_norm,
           attn_sinks, w_out, norm2, w_group, b_group, w_expert, b_expert, w_gate, w_up, w_down, norm_final):
    return _pipeline(x, c, ctx, c_ctx, w_ada, b_ada, norm1, w_in, conv_w, conv_b, dt_bias, a_log, d_skip,
                     ssd_norm, attn_sinks, w_out, norm2, w_group, b_group, w_expert, b_expert, w_gate, w_up,
                     w_down, norm_final)["out"]


def _pipeline(x, c, ctx, c_ctx, w_ada, b_ada, norm1, w_in, conv_w, conv_b, dt_bias, a_log, d_skip, ssd_norm,
              attn_sinks, w_out, norm2, w_group, b_group, w_expert, b_expert, w_gate, w_up, w_down, norm_final):
    bt, seq, d = x.shape
    n_ctx = ctx.shape[1]
    t = bt * seq
    layer = 0
    assert w_ada.shape[0] == 1 and seq % ROWS_IN == 0 and n_ctx % CHUNK == 0

    cc = jnp.zeros((SUBLANES, d), F32).at[:bt].set(c).at[bt].set(c_ctx)
    mod = _ada(cc, w_ada[layer], b_ada[layer][None, :])
    mod3 = mod.reshape(SUBLANES, 1, 6 * d)

    w = w_in[layer]
    o_z, o_xbc, o_dt = 0, SSD_INNER, SSD_INNER + CONV_DIM
    o_q = o_dt + 2 * SSD_HEADS
    o_k, o_v = o_q + ATTN_WIDTH, o_q + ATTN_WIDTH + KV_WIDTH
    q_perm = np.concatenate([np.r_[j * HEAD_DIM:(j + 1) * HEAD_DIM, (j + 4) * HEAD_DIM:(j + 5) * HEAD_DIM]
                             for j in range(Q_PER_KV)])
    w_q = w[:, o_q:o_q + ATTN_WIDTH].reshape(d, ATTN_KV_HEADS, Q_PER_KV, HEAD_DIM).transpose(0, 2, 1, 3)
    w_q = w_q.reshape(d, ATTN_WIDTH)
    w_cat = jnp.concatenate([w[:, o_z:o_z + SSD_INNER], w[:, o_xbc:o_xbc + CONV_DIM], w_q,
                             w[:, o_k:o_k + KV_WIDTH], w[:, o_v:o_v + KV_WIDTH]], axis=1).astype(BF16)
    w_dtT = w[:, o_dt:o_dt + 2 * SSD_HEADS].T.astype(BF16)

    seq_blocks = seq // ROWS_IN
    cos, sin = _rope_tables(seq)
    bias_rows = jnp.broadcast_to(dt_bias[layer].reshape(2 * SSD_HEADS, 1), (2 * SSD_HEADS, LANES))
    alog_rows = jnp.broadcast_to(a_log[layer].reshape(2 * SSD_HEADS, 1), (2 * SSD_HEADS, LANES))
    segs_x = [("z", _COL_Z, SSD_INNER), ("xbc", _COL_XBC, CONV_DIM), ("dtT", 0, 0),
              ("q", _COL_Q, ATTN_WIDTH), ("k", _COL_K, KV_WIDTH), ("v", _COL_V, KV_WIDTH)]
    z_x, xbc_x, dt_x, pre_x, suf_x, q_x, k_x, v_x = _inproj(
        x.reshape(t, d), mod3, norm1[layer][None, :], w_cat, w_dtT, bias_rows, alog_rows, (cos, sin),
        rows=ROWS_IN, mod_row_fn=lambda i: i // seq_blocks, segs=segs_x, seq_blocks=seq_blocks)
    segs_c = [("xbc", _COL_XBC, CONV_DIM), ("dtT", 0, 0), ("k", _COL_K, KV_WIDTH), ("v", _COL_V, KV_WIDTH)]
    xbc_c, dt_c, pre_c, suf_c, k_c, v_c = _inproj(
        ctx.reshape(bt * n_ctx, d), mod3, norm1[layer][None, :], w_cat, w_dtT, bias_rows, alog_rows, None,
        rows=n_ctx, mod_row_fn=lambda i: bt, segs=segs_c, seq_blocks=1)

    cw = jnp.zeros((SUBLANES, CONV_DIM), F32).at[:CONV_WIDTH].set(conv_w[layer])
    cb = conv_b[layer][None, :]
    dskip_e = jnp.repeat(d_skip[layer], SSD_HEAD_DIM)[None, :]
    consts = (cw, cb, dskip_e, ssd_norm[layer][None, :], _head_expand_mat())

    zeros_state = jnp.zeros((bt, SSD_STATE, SSD_INNER), F32)
    z_dummy = jnp.zeros((bt, n_ctx, SSD_INNER), F32)
    h_cf, h_cb = _ssd(xbc_c.reshape(bt, n_ctx, CONV_DIM), (dt_c, pre_c, suf_c), z_dummy, consts,
                      zeros_state, zeros_state, rows=n_ctx, need_y=False)
    ssd_x, _, _ = _ssd(xbc_x.reshape(bt, seq, CONV_DIM), (dt_x, pre_x, suf_x), z_x.reshape(bt, seq, SSD_INNER),
                       consts, h_cf, h_cb, rows=ROWS_SSD, need_y=True)

    attn_x = _attention(attn_sinks[layer].astype(F32),
                        q_x.reshape(bt, seq, ATTN_WIDTH), k_x.reshape(bt, seq, KV_WIDTH),
                        v_x.reshape(bt, seq, KV_WIDTH), k_c.reshape(bt, n_ctx, KV_WIDTH),
                        v_c.reshape(bt, n_ctx, KV_WIDTH))

    wo = w_out[layer]
    wa = wo[:SSD_INNER].astype(BF16)
    wb = wo[SSD_INNER:].reshape(ATTN_KV_HEADS, Q_PER_KV, HEAD_DIM, d).transpose(1, 0, 2, 3)
    wb = wb.reshape(ATTN_WIDTH, d).astype(BF16)
    wr = jnp.zeros((ROUTE_ROWS, d), F32).at[:N_GROUPS].set(w_group[layer].T)
    wr = wr.at[N_GROUPS:N_GROUPS + N_EXPERTS].set(w_expert[layer].T).astype(BF16)
    br = jnp.zeros((ROUTE_ROWS,), F32).at[:N_GROUPS].set(b_group[layer])
    br = br.at[N_GROUPS:N_GROUPS + N_EXPERTS].set(b_expert[layer])
    br = jnp.broadcast_to(br[:, None], (ROUTE_ROWS, ROWS_IN))
    tri = jnp.asarray(np.triu(np.ones((ROWS_IN, ROWS_IN), np.float32), 1), BF16)
    x1, h2, ids, wcol, cnt = _outproj(
        ssd_x.reshape(t, SSD_INNER), attn_x.reshape(t, ATTN_WIDTH), x.reshape(t, d), mod3, norm2[layer][None, :],
        wa, wb, wr, br, tri, rows=ROWS_IN, seq_blocks=seq_blocks)

    counts = cnt[N_GROUPS:N_GROUPS + N_EXPERTS, 0].astype(jnp.int32)
    padded = (counts + MOE_BLK - 1) // MOE_BLK * MOE_BLK
    ends = jnp.cumsum(padded)
    pstart = ends - padded
    nblocks = (2 * t) // MOE_BLK + N_EXPERTS
    blk_start = jnp.arange(nblocks, dtype=jnp.int32) * MOE_BLK
    blk_e = jnp.minimum(jnp.sum((ends[None, :] <= blk_start[:, None]).astype(jnp.int32), axis=1), N_EXPERTS - 1)
    nused = (ends[-1] // MOE_BLK).astype(jnp.int32).reshape(1)
    code = ids[0:2].reshape(-1)
    e_id = lax.shift_right_logical(code, RANK_BITS)
    onehot = e_id[:, None] == jnp.arange(N_EXPERTS, dtype=jnp.int32)[None, :]
    dst = jnp.sum(jnp.where(onehot, pstart[None, :], 0), axis=1) + (code & ((1 << RANK_BITS) - 1))

    y_buf = _experts(blk_e, nused, pstart, counts, dst, h2, w_gate[layer], w_up[layer], w_down[layer],
                     nblocks=nblocks, n_tok=t)
    out = _combine(dst, y_buf, x1, wcol, mod3, norm_final[None, :], rows=ROWS_CMB, seq_blocks=seq // ROWS_CMB)
    return dict(out=out.reshape(bt, seq, d), mod=mod, z=z_x, xbc=xbc_x, dt=dt_x, q=q_x, k=k_x, v=v_x,
                k_c=k_c, v_c=v_c, h_cf=h_cf, h_cb=h_cb, ssd=ssd_x, attn=attn_x, x1=x1, h2=h2, ids=ids,
                wcol=wcol, cnt=cnt, y_buf=y_buf, pstart=pstart, q_perm=q_perm)
```

```python
import functools
import math

import numpy as np
import jax
import jax.numpy as jnp
from jax import lax
from jax.experimental import pallas as pl
from jax.experimental.pallas import tpu as pltpu

F32 = jnp.float32
BF16 = jnp.bfloat16

D_MODEL = 1024
GRID_W = 64
EPS = 1e-6
SSD_INNER = 512
SSD_HEAD_DIM = 64
SSD_HEADS = 8
SSD_GROUPS = 2
SSD_STATE = 128
CONV_WIDTH = 5
CONV_DIM = SSD_INNER + 2 * SSD_GROUPS * SSD_STATE
CHUNK = 128
ATTN_WIDTH = 512
HEAD_DIM = 64
ATTN_Q_HEADS = 8
ATTN_KV_HEADS = 2
Q_PER_KV = 4
KV_WIDTH = 128
WINDOW = 128
ROPE_BASE = 10000.0
N_GROUPS = 4
EXPERTS_PER_GROUP = 8
N_EXPERTS = 32
EXPERT_DIM = 512

LANES = 128
SUBLANES = 8
NEG = -1e30

ROWS_IN = 512
ROWS_SSD = 512
ROWS_ATT = 256
ROUTE_ROWS = 48
MOE_BLK = 128
GATHER_SLOTS = 3
RANK_BITS = 16
D_TILES = D_MODEL // LANES
ROWS_CMB = 128


def _silu(v):
    half = 0.5 * v
    return half + half * jnp.tanh(half)


def _softplus(v):
    return jnp.maximum(v, 0.0) + jnp.log(1.0 + jnp.exp(-jnp.abs(v)))


def _ada_kernel(c_ref, w_ref, b_ref, o_ref):
    a = _silu(c_ref[...])
    o_ref[...] = jnp.dot(a.astype(BF16), w_ref[...].astype(BF16),
                         preferred_element_type=F32) + b_ref[...]


def _ada(cc, w_ada, b_ada):
    d, n = w_ada.shape
    tn = 1536
    return pl.pallas_call(
        _ada_kernel,
        grid=(n // tn,),
        in_specs=[pl.BlockSpec((SUBLANES, d), lambda j: (0, 0)),
                  pl.BlockSpec((d, tn), lambda j: (0, j)),
                  pl.BlockSpec((1, tn), lambda j: (0, j))],
        out_specs=pl.BlockSpec((SUBLANES, tn), lambda j: (0, j)),
        out_shape=jax.ShapeDtypeStruct((SUBLANES, n), F32),
        name="ada",
    )(cc, w_ada, b_ada)


def _rope(t, cos, sin_signed, first_half):
    rot = jnp.where(first_half, pltpu.roll(t, LANES - HEAD_DIM // 2, 1), pltpu.roll(t, HEAD_DIM // 2, 1))
    return t * cos + rot * sin_signed


def _lane_cumsum(v, lane, reverse):
    k = 1
    while k < LANES:
        if reverse:
            v = v + jnp.where(lane < LANES - k, pltpu.roll(v, LANES - k, 1), 0.0)
        else:
            v = v + jnp.where(lane >= k, pltpu.roll(v, k, 1), 0.0)
        k *= 2
    return v


def _inproj_kernel(*refs, segs, rope):
    x_ref, sh_ref, sc_ref, g_ref, w_ref, wdt_ref, bias_ref, alog_ref = refs[:8]
    pos = 8
    if rope:
        cos_ref, sin_ref = refs[8:10]
        pos = 10
    outs = iter(refs[pos:])
    x = x_ref[...]
    ms = jnp.mean(x * x, axis=-1, keepdims=True)
    h = (x * lax.rsqrt(ms + EPS)) * (g_ref[...] * (1.0 + sc_ref[0])) + sh_ref[0]
    hb = h.astype(BF16)
    if rope:
        cos = cos_ref[...]
        lane = lax.broadcasted_iota(jnp.int32, cos.shape, 1)
        first_half = (lane & (HEAD_DIM - 1)) < HEAD_DIM // 2
        sin_s = jnp.where(first_half, -sin_ref[...], sin_ref[...])
    for kind, c0, width in segs:
        if kind == "dtT":
            dt_ref, pre_ref, suf_ref = next(outs), next(outs), next(outs)
            dtt = lax.dot_general(wdt_ref[...], hb, (((1,), (1,)), ((), ())), preferred_element_type=F32)
            lane16 = lax.broadcasted_iota(jnp.int32, (2 * SSD_HEADS, CHUNK), 1)
            neg_a = -jnp.exp(alog_ref[...])
            for j in range(dt_ref.shape[0]):
                dt = _softplus(dtt[:, j * CHUNK:(j + 1) * CHUNK] + bias_ref[...])
                a = dt * neg_a
                dt_ref[j] = dt
                pre_ref[j] = _lane_cumsum(a, lane16, False)
                suf_ref[j] = _lane_cumsum(a, lane16, True)
            continue
        o_ref = next(outs)
        for j0 in range(0, width, 512):
            wj = min(512, width - j0)
            acc = jnp.dot(hb, w_ref[:, c0 + j0:c0 + j0 + wj], preferred_element_type=F32)
            if rope and kind in ("q", "k"):
                scale = HEAD_DIM ** -0.5 if kind == "q" else 1.0
                for l0 in range(0, wj, LANES):
                    t = _rope(acc[:, l0:l0 + LANES], cos, sin_s, first_half)
                    o_ref[:, j0 + l0:j0 + l0 + LANES] = (t * scale).astype(o_ref.dtype)
            else:
                o_ref[:, j0:j0 + wj] = acc.astype(o_ref.dtype)


_COL_Z, _COL_XBC, _COL_Q, _COL_K, _COL_V = 0, 512, 1536, 2048, 2176
_W_COLS = 2304


def _inproj(xf, mod3, norm1, w_cat, w_dtT, bias_rows, alog_rows, rope_tabs, *, rows, mod_row_fn, segs, seq_blocks):
    t, d = xf.shape
    rope = rope_tabs is not None
    in_specs = [
        pl.BlockSpec((rows, d), lambda i: (i, 0)),
        pl.BlockSpec((1, 1, d), lambda i: (mod_row_fn(i), 0, 0)),
        pl.BlockSpec((1, 1, d), lambda i: (mod_row_fn(i), 0, 1)),
        pl.BlockSpec((1, d), lambda i: (0, 0)),
        pl.BlockSpec(w_cat.shape, lambda i: (0, 0)),
        pl.BlockSpec(w_dtT.shape, lambda i: (0, 0)),
        pl.BlockSpec(bias_rows.shape, lambda i: (0, 0)),
        pl.BlockSpec(alog_rows.shape, lambda i: (0, 0)),
    ]
    args = [xf, mod3, mod3, norm1, w_cat, w_dtT, bias_rows, alog_rows]
    if rope:
        in_specs += [pl.BlockSpec((rows, LANES), lambda i: (i % seq_blocks, 0))] * 2
        args += list(rope_tabs)
    out_specs, out_shapes = [], []
    for kind, _, width in segs:
        if kind == "dtT":
            for _ in range(3):
                out_specs.append(pl.BlockSpec((rows // CHUNK, 2 * SSD_HEADS, CHUNK), lambda i: (i, 0, 0)))
                out_shapes.append(jax.ShapeDtypeStruct((t // CHUNK, 2 * SSD_HEADS, CHUNK), F32))
        else:
            dt = BF16 if kind in ("q", "k", "v") else F32
            out_specs.append(pl.BlockSpec((rows, width), lambda i: (i, 0)))
            out_shapes.append(jax.ShapeDtypeStruct((t, width), dt))
    return pl.pallas_call(
        functools.partial(_inproj_kernel, segs=tuple(segs), rope=rope),
        grid=(t // rows,),
        in_specs=in_specs,
        out_specs=out_specs,
        out_shape=out_shapes,
        name="inproj_rope" if rope else "inproj_ctx",
    )(*args)


_XB_COLS = SSD_INNER + SSD_GROUPS * SSD_STATE
_Y_OFF = CONV_DIM - _XB_COLS


def _ssd_kernel(*refs, need_y, nsteps, cps):
    (xbc_ref, prev_ref, next_ref, dt_ref, pre_ref, suf_ref, z_ref, cw_ref, cb_ref,
     dskip_ref, gain_ref, exp_ref, h0f_ref, h0b_ref) = refs[:14]
    if need_y:
        y_ref, hf_out, hb_out = refs[14:17]
        rest = refs[17:]
    else:
        y_ref = None
        hf_out, hb_out = refs[14:16]
        rest = refs[16:]
    cwin, u_ref, hf, hb, hbs, ust = rest

    sw = pl.program_id(1)
    s = pl.program_id(2)
    bi = jnp.where(sw == 0, nsteps - 1 - s, s)
    rows = xbc_ref.shape[1]
    h = SSD_HEADS

    @pl.when((sw == 0) & (s == 0))
    def _():
        hb[...] = h0b_ref[0]

    @pl.when((sw == 1) & (s == 0))
    def _():
        hf[...] = h0f_ref[0]

    ri = lax.broadcasted_iota(jnp.int32, (CHUNK, CHUNK), 0)
    ci = lax.broadcasted_iota(jnp.int32, (CHUNK, CHUNK), 1)
    lane128 = lax.broadcasted_iota(jnp.int32, (CHUNK, LANES), 1)
    lo_mask = lane128 < SSD_HEAD_DIM

    def conv_silu(c, c0, c1, out_ref):
        r0 = pl.multiple_of(c * CHUNK, CHUNK)
        above = xbc_ref[0, pl.ds(pl.multiple_of(jnp.maximum(r0 - SUBLANES, 0), SUBLANES), SUBLANES), c0:c1]
        halo_a = jnp.where(bi > 0, prev_ref[0, :, c0:c1], 0.0)
        cwin[0:SUBLANES, c0:c1] = jnp.where(c > 0, above, halo_a)
        cwin[SUBLANES:SUBLANES + CHUNK, c0:c1] = xbc_ref[0, pl.ds(r0, CHUNK), c0:c1]
        below = xbc_ref[0, pl.ds(pl.multiple_of(jnp.minimum(r0 + CHUNK, rows - SUBLANES), SUBLANES), SUBLANES), c0:c1]
        halo_b = jnp.where(bi < nsteps - 1, next_ref[0, :, c0:c1], 0.0)
        cwin[SUBLANES + CHUNK:, c0:c1] = jnp.where(c < cps - 1, below, halo_b)
        for j0 in range(c0, c1, LANES):
            acc = jnp.broadcast_to(cb_ref[:, j0:j0 + LANES], (CHUNK, LANES))
            for k in range(CONV_WIDTH):
                off = SUBLANES - CONV_WIDTH // 2 + k
                acc = acc + cwin[off:off + CHUNK, j0:j0 + LANES] * cw_ref[k:k + 1, j0:j0 + LANES]
            out_ref[:, j0 - c0:j0 - c0 + LANES] = _silu(acc)

    def dt_rows(c):
        return dt_ref[c], pre_ref[c], suf_ref[c]

    def state_update(state_ref, xb, w_rows, tot_col):
        dec = jnp.exp(jnp.sum(tot_col * exp_ref[...], axis=0, keepdims=True))
        for g in range(SSD_GROUPS):
            bm = xb[:, SSD_INNER + g * SSD_STATE:SSD_INNER + (g + 1) * SSD_STATE]
            bt = bm.T
            for pr in range(2):
                h0 = g * 4 + pr * 2
                c0 = h0 * SSD_HEAD_DIM
                xp = xb[:, c0:c0 + LANES]
                rhs = jnp.concatenate([jnp.where(lo_mask, xp, 0.0), jnp.where(lo_mask, 0.0, xp)],
                                      axis=0).astype(BF16)
                lhs = jnp.concatenate([bt * w_rows[h0:h0 + 1, :], bt * w_rows[h0 + 1:h0 + 2, :]],
                                      axis=1).astype(BF16)
                sres = jnp.dot(lhs, rhs, preferred_element_type=F32)
                state_ref[:, c0:c0 + LANES] = state_ref[:, c0:c0 + LANES] * dec[:, c0:c0 + LANES] + sres

    def col_forms(rows16):
        padded = jnp.concatenate([rows16, jnp.zeros((CHUNK - 2 * h, LANES), F32)], axis=0)
        return padded.T

    def backward_chunk(k, carry):
        c = cps - 1 - k
        gc = bi * cps + c
        xb = ust.at[gc]
        conv_silu(c, 0, _XB_COLS, xb)
        dt, _, suf = dt_rows(c)
        hbs[gc] = hb[...].astype(BF16)
        s0 = suf[:, 0:1]
        w_rows = jnp.exp(s0 - suf) * dt
        state_update(hb, xb, w_rows[h:2 * h, :], s0[h:2 * h, :])
        return carry

    def forward_chunk(c, carry):
        gc = bi * cps + c
        r0 = pl.multiple_of(c * CHUNK, CHUNK)
        xb = ust.at[gc]
        conv_silu(c, _XB_COLS, CONV_DIM, u_ref)
        dt, pre, suf = dt_rows(c)
        rowsf = jnp.concatenate([pre[0:h, :], suf[h:2 * h, :]], axis=0)
        colsf = col_forms(rowsf)
        ecol = jnp.exp(colsf)
        for g in range(SSD_GROUPS):
            bm = xb[:, SSD_INNER + g * SSD_STATE:SSD_INNER + (g + 1) * SSD_STATE]
            cm = u_ref[:, g * SSD_STATE:(g + 1) * SSD_STATE]
            cmb = cm.astype(BF16)
            cbm = lax.dot_general(cmb, bm.astype(BF16), (((1,), (1,)), ((), ())),
                                  preferred_element_type=F32)
            for pr in range(2):
                h0 = g * 4 + pr * 2
                c0 = h0 * SSD_HEAD_DIM
                lhs_parts = []
                for hh in (h0, h0 + 1):
                    segf = colsf[:, hh:hh + 1] - rowsf[hh:hh + 1, :]
                    gf = jnp.exp(jnp.where(ci <= ri, segf, NEG)) * dt[hh:hh + 1, :]
                    segb = colsf[:, h + hh:h + hh + 1] - rowsf[h + hh:h + hh + 1, :]
                    gb = jnp.exp(jnp.where(ci >= ri, segb, NEG)) * dt[h + hh:h + hh + 1, :]
                    lhs_parts.append((cbm * (gf + gb)).astype(BF16))
                for hh in (h0, h0 + 1):
                    lhs_parts.append((cm * ecol[:, hh:hh + 1]).astype(BF16))
                for hh in (h0, h0 + 1):
                    lhs_parts.append((cm * ecol[:, h + hh:h + hh + 1]).astype(BF16))
                lhs = jnp.concatenate(lhs_parts, axis=1)
                xp = xb[:, c0:c0 + LANES]
                sf = hf[:, c0:c0 + LANES]
                sb = hbs[gc, :, c0:c0 + LANES].astype(F32)
                rhs = jnp.concatenate(
                    [jnp.where(lo_mask, xp, 0.0), jnp.where(lo_mask, 0.0, xp),
                     jnp.where(lo_mask, sf, 0.0), jnp.where(lo_mask, 0.0, sf),
                     jnp.where(lo_mask, sb, 0.0), jnp.where(lo_mask, 0.0, sb)], axis=0).astype(BF16)
                ypair = jnp.dot(lhs, rhs, preferred_element_type=F32)
                ypair = ypair + dskip_ref[:, c0:c0 + LANES] * xp
                zz = z_ref[0, pl.ds(r0, CHUNK), c0:c0 + LANES]
                u_ref[:, _Y_OFF + c0:_Y_OFF + c0 + LANES] = ypair * _silu(zz)
        last = pre[:, LANES - 1:LANES]
        w_rows = jnp.exp(last - pre) * dt
        state_update(hf, xb, w_rows[0:h, :], last[0:h, :])
        if need_y:
            yv = u_ref[:, _Y_OFF:_Y_OFF + SSD_INNER]
            ms = jnp.mean(yv * yv, axis=-1, keepdims=True)
            y_ref[0, pl.ds(r0, CHUNK), :] = (yv * lax.rsqrt(ms + EPS) * gain_ref[...]).astype(y_ref.dtype)
        return carry

    @pl.when(sw == 0)
    def _():
        lax.fori_loop(0, cps, backward_chunk, 0)

    @pl.when(sw == 1)
    def _():
        lax.fori_loop(0, cps, forward_chunk, 0)

    @pl.when((sw == 0) & (s == nsteps - 1))
    def _():
        hb_out[0] = hb[...]

    @pl.when((sw == 1) & (s == nsteps - 1))
    def _():
        hf_out[0] = hf[...]


def _ssd(xbc, dts, z, consts, h0f, h0b, *, rows, need_y):
    bt, seq, _ = xbc.shape
    nsteps = seq // rows
    cps = rows // CHUNK
    hb8 = rows // SUBLANES
    nb8 = seq // SUBLANES
    cw, cb, dskip, gain, expm = consts
    dt_spec = pl.BlockSpec((cps, 2 * SSD_HEADS, CHUNK), lambda b, sw, s: (b * nsteps + blk(b, sw, s), 0, 0))

    def blk(b, sw, s):
        return jnp.where(sw == 0, nsteps - 1 - s, s)

    def full(a):
        return pl.BlockSpec(a.shape, lambda b, sw, s: (0,) * a.ndim)

    in_specs = [
        pl.BlockSpec((1, rows, CONV_DIM), lambda b, sw, s: (b, blk(b, sw, s), 0)),
        pl.BlockSpec((1, SUBLANES, CONV_DIM), lambda b, sw, s: (b, jnp.maximum(blk(b, sw, s) * hb8 - 1, 0), 0)),
        pl.BlockSpec((1, SUBLANES, CONV_DIM),
                     lambda b, sw, s: (b, jnp.minimum((blk(b, sw, s) + 1) * hb8, nb8 - 1), 0)),
        dt_spec, dt_spec, dt_spec,
        pl.BlockSpec((1, rows, SSD_INNER), lambda b, sw, s: (b, jnp.where(sw == 0, 0, s), 0)),
        full(cw), full(cb), full(dskip), full(gain), full(expm),
        pl.BlockSpec((1, SSD_STATE, SSD_INNER), lambda b, sw, s: (b, 0, 0)),
        pl.BlockSpec((1, SSD_STATE, SSD_INNER), lambda b, sw, s: (b, 0, 0)),
    ]
    st_spec = pl.BlockSpec((1, SSD_STATE, SSD_INNER), lambda b, sw, s: (b, 0, 0))
    st_shape = jax.ShapeDtypeStruct((bt, SSD_STATE, SSD_INNER), F32)
    out_specs, out_shapes = [st_spec, st_spec], [st_shape, st_shape]
    if need_y:
        out_specs = [pl.BlockSpec((1, rows, SSD_INNER), lambda b, sw, s: (b, jnp.where(sw == 0, 0, s), 0))] + out_specs
        out_shapes = [jax.ShapeDtypeStruct((bt, seq, SSD_INNER), BF16)] + out_shapes
    scratch = [
        pltpu.VMEM((CHUNK + 2 * SUBLANES, CONV_DIM), F32),
        pltpu.VMEM((CHUNK, _Y_OFF + SSD_INNER), F32),
        pltpu.VMEM((SSD_STATE, SSD_INNER), F32),
        pltpu.VMEM((SSD_STATE, SSD_INNER), F32),
        pltpu.VMEM((seq // CHUNK, SSD_STATE, SSD_INNER), BF16),
        pltpu.VMEM((seq // CHUNK, CHUNK, _XB_COLS), F32),
    ]
    return pl.pallas_call(
        functools.partial(_ssd_kernel, need_y=need_y, nsteps=nsteps, cps=cps),
        grid=(bt, 2, nsteps),
        in_specs=in_specs,
        out_specs=out_specs,
        out_shape=out_shapes,
        scratch_shapes=scratch,
        compiler_params=pltpu.CompilerParams(dimension_semantics=("arbitrary", "arbitrary", "arbitrary")),
        name="ssd_y" if need_y else "ssd_ctx",
    )(xbc, xbc, xbc, *dts, z, cw, cb, dskip, gain, expm, h0f, h0b)


def _attn_kernel(sink_ref, q_ref, kp_ref, kc_ref, kn_ref, vp_ref, vc_ref, vn_ref, kx_ref, vx_ref, o_ref, *, nsteps):
    m = pl.program_id(1)
    lane = lax.broadcasted_iota(jnp.int32, (1, LANES), 1)
    lo = lane < HEAD_DIM
    rq = Q_PER_KV * WINDOW
    qi = lax.broadcasted_iota(jnp.int32, (rq, WINDOW), 0) & (WINDOW - 1)
    kj = lax.broadcasted_iota(jnp.int32, (rq, WINDOW), 1)
    rblk = lax.shift_right_logical(lax.broadcasted_iota(jnp.int32, (rq, 1), 0), int(math.log2(WINDOW)))
    zero = jnp.zeros((), BF16)

    kpieces = [kp_ref[0], kc_ref[0, 0:WINDOW], kc_ref[0, WINDOW:2 * WINDOW], kn_ref[0]]
    vpieces = [vp_ref[0], vc_ref[0, 0:WINDOW], vc_ref[0, WINDOW:2 * WINDOW], vn_ref[0]]
    kx = kx_ref[0]
    vx = vx_ref[0]
    nt = (((1,), (1,)), ((), ()))

    for sb in range(ROWS_ATT // WINDOW):
        qsb = q_ref[0, sb * WINDOW:(sb + 1) * WINDOW, :]
        prev_ok = jnp.logical_or(m > 0, sb > 0)
        next_ok = jnp.logical_or(m < nsteps - 1, sb < ROWS_ATT // WINDOW - 1)
        outs = [None] * Q_PER_KV
        for kv in range(ATTN_KV_HEADS):
            sel = lo if kv == 0 else jnp.logical_not(lo)
            qs = jnp.concatenate(
                [jnp.where(sel, qsb[:, j * LANES:(j + 1) * LANES], zero) for j in range(Q_PER_KV)], axis=0)
            s_prev = lax.dot_general(qs, kpieces[sb], nt, preferred_element_type=F32)
            s_cur = lax.dot_general(qs, kpieces[sb + 1], nt, preferred_element_type=F32)
            s_next = lax.dot_general(qs, kpieces[sb + 2], nt, preferred_element_type=F32)
            s_ctx = lax.dot_general(qs, kx, nt, preferred_element_type=F32)
            s_prev = jnp.where((kj >= qi) & prev_ok, s_prev, NEG)
            s_next = jnp.where((kj <= qi) & next_ok, s_next, NEG)
            sink = jnp.zeros((rq, 1), F32)
            for j in range(Q_PER_KV):
                sink = jnp.where(rblk == j, sink_ref[kv * Q_PER_KV + j], sink)
            n_ct = s_ctx.shape[1] // LANES
            tile_max = jnp.maximum(jnp.maximum(s_prev, s_cur), s_next)
            for t in range(n_ct):
                tile_max = jnp.maximum(tile_max, s_ctx[:, t * LANES:(t + 1) * LANES])
            mx = jnp.maximum(jnp.max(tile_max, axis=-1, keepdims=True), sink)
            p_prev = jnp.exp(s_prev - mx)
            p_cur = jnp.exp(s_cur - mx)
            p_next = jnp.exp(s_next - mx)
            p_ctx = jnp.exp(s_ctx - mx)
            tile_sum = p_prev + p_cur + p_next
            for t in range(n_ct):
                tile_sum = tile_sum + p_ctx[:, t * LANES:(t + 1) * LANES]
            den = jnp.sum(tile_sum, axis=-1, keepdims=True) + jnp.exp(sink - mx)
            acc = jnp.dot(p_prev.astype(BF16), jnp.where(sel, vpieces[sb], zero), preferred_element_type=F32)
            acc += jnp.dot(p_cur.astype(BF16), jnp.where(sel, vpieces[sb + 1], zero), preferred_element_type=F32)
            acc += jnp.dot(p_next.astype(BF16), jnp.where(sel, vpieces[sb + 2], zero), preferred_element_type=F32)
            acc += jnp.dot(p_ctx.astype(BF16), jnp.where(sel, vx, zero), preferred_element_type=F32)
            acc = acc * (1.0 / den)
            for j in range(Q_PER_KV):
                part = acc[j * WINDOW:(j + 1) * WINDOW, :]
                outs[j] = part if outs[j] is None else outs[j] + part
        for j in range(Q_PER_KV):
            o_ref[0, sb * WINDOW:(sb + 1) * WINDOW, j * LANES:(j + 1) * LANES] = outs[j].astype(o_ref.dtype)


def _attention(sinks, q, k, v, kx, vx):
    bt, seq, _ = q.shape
    nsteps = seq // ROWS_ATT
    per = ROWS_ATT // WINDOW
    nb = seq // WINDOW
    n_ctx = kx.shape[1]

    def prev(b, m, s):
        return (b, jnp.maximum(m * per - 1, 0), 0)

    def cur(b, m, s):
        return (b, m, 0)

    def nxt(b, m, s):
        return (b, jnp.minimum((m + 1) * per, nb - 1), 0)

    small = (1, WINDOW, KV_WIDTH)
    big = (1, ROWS_ATT, KV_WIDTH)
    grid_spec = pltpu.PrefetchScalarGridSpec(
        num_scalar_prefetch=1,
        grid=(bt, nsteps),
        in_specs=[
            pl.BlockSpec((1, ROWS_ATT, ATTN_WIDTH), cur),
            pl.BlockSpec(small, prev), pl.BlockSpec(big, cur), pl.BlockSpec(small, nxt),
            pl.BlockSpec(small, prev), pl.BlockSpec(big, cur), pl.BlockSpec(small, nxt),
            pl.BlockSpec((1, n_ctx, KV_WIDTH), lambda b, m, s: (b, 0, 0)),
            pl.BlockSpec((1, n_ctx, KV_WIDTH), lambda b, m, s: (b, 0, 0)),
        ],
        out_specs=pl.BlockSpec((1, ROWS_ATT, ATTN_WIDTH), cur),
    )
    return pl.pallas_call(
        functools.partial(_attn_kernel, nsteps=nsteps),
        grid_spec=grid_spec,
        out_shape=jax.ShapeDtypeStruct((bt, seq, ATTN_WIDTH), BF16),
        name="attn",
    )(sinks, q, k, k, k, v, v, v, kx, vx)


def _outproj_kernel(ssd_ref, att_ref, x_ref, g1_ref, sh_ref, sc_ref, n2_ref, wa_ref, wb_ref, wr_ref, br_ref,
                    tri_ref, x1_ref, h2_ref, ids_ref, wcol_ref, cnt_ref, carry, *, nsteps):
    i = pl.program_id(0)
    rows = x_ref.shape[0]

    @pl.when(i == 0)
    def _():
        carry[...] = jnp.zeros_like(carry)

    acc = jnp.dot(ssd_ref[...], wa_ref[...], preferred_element_type=F32)
    acc += jnp.dot(att_ref[...], wb_ref[...], preferred_element_type=F32)
    x1 = x_ref[...] + g1_ref[0] * acc
    x1_ref[...] = x1
    ms = jnp.mean(x1 * x1, axis=-1, keepdims=True)
    h2 = (x1 * lax.rsqrt(ms + EPS)) * (n2_ref[...] * (1.0 + sc_ref[0])) + sh_ref[0]
    for sidx in range(d_tiles := h2.shape[1] // LANES):
        h2_ref[pl.ds(sidx, rows, stride=d_tiles), :] = h2[:, sidx * LANES:(sidx + 1) * LANES]
    lt = lax.dot_general(wr_ref[...], h2.astype(BF16), (((1,), (1,)), ((), ())),
                         preferred_element_type=F32) + br_ref[...]
    row = lax.broadcasted_iota(jnp.int32, lt.shape, 0).astype(F32)
    big_i = float(ROUTE_ROWS)
    gl = jnp.where(row < N_GROUPS, lt, NEG)
    gmax = jnp.max(gl, axis=0, keepdims=True)
    gidx = jnp.min(jnp.where(gl == gmax, row, big_i), axis=0, keepdims=True)
    g_w = 1.0 / jnp.sum(jnp.exp(gl - gmax), axis=0, keepdims=True)
    lo = N_GROUPS + EXPERTS_PER_GROUP * gidx
    el = jnp.where((row >= lo) & (row < lo + EXPERTS_PER_GROUP), lt, NEG)
    m1 = jnp.max(el, axis=0, keepdims=True)
    i1 = jnp.min(jnp.where(el == m1, row, big_i), axis=0, keepdims=True)
    el2 = jnp.where(row == i1, NEG, el)
    m2 = jnp.max(el2, axis=0, keepdims=True)
    i2 = jnp.min(jnp.where(el2 == m2, row, big_i), axis=0, keepdims=True)
    r = jnp.exp(m2 - m1)
    w1 = g_w / (1.0 + r)
    w2 = g_w * r / (1.0 + r)
    oh1 = row == i1
    oh2 = row == i2
    cnt = jnp.where(oh1 | oh2, 1.0, 0.0)
    prefix = jnp.dot(cnt.astype(BF16), tri_ref[...], preferred_element_type=F32)
    base = carry[...] + prefix
    rank1 = jnp.sum(jnp.where(oh1, base, 0.0), axis=0, keepdims=True)
    rank2 = jnp.sum(jnp.where(oh2, base, 0.0), axis=0, keepdims=True)
    carry[...] = carry[...] + jnp.sum(cnt, axis=1, keepdims=True)
    r8 = lax.broadcasted_iota(jnp.int32, (SUBLANES, rows), 0)
    code1 = (i1 - N_GROUPS) * float(1 << RANK_BITS) + rank1
    code2 = (i2 - N_GROUPS) * float(1 << RANK_BITS) + rank2
    ids_ref[...] = jnp.where(r8 == 0, code1, jnp.where(r8 == 1, code2, 0.0)).astype(jnp.int32)
    r128 = lax.broadcasted_iota(jnp.int32, (LANES, rows), 0)
    wfull = jnp.where(r128 == 0, w1, jnp.where(r128 == 1, w2, 0.0))
    for j in range(rows // LANES):
        wcol_ref[j * LANES:(j + 1) * LANES, :] = wfull[:, j * LANES:(j + 1) * LANES].T

    @pl.when(i == nsteps - 1)
    def _():
        cnt_ref[...] = carry[:, 0:LANES]


def _outproj(ssd, att, xf, mod3, norm2, wa, wb, wr, br, tri, *, rows, seq_blocks):
    t, d = xf.shape
    nsteps = t // rows

    def mod(col):
        return pl.BlockSpec((1, 1, d), lambda i: (i // seq_blocks, 0, col))

    def full(a):
        return pl.BlockSpec(a.shape, lambda i: (0,) * a.ndim)

    return pl.pallas_call(
        functools.partial(_outproj_kernel, nsteps=nsteps),
        grid=(nsteps,),
        in_specs=[pl.BlockSpec((rows, SSD_INNER), lambda i: (i, 0)),
                  pl.BlockSpec((rows, ATTN_WIDTH), lambda i: (i, 0)),
                  pl.BlockSpec((rows, d), lambda i: (i, 0)),
                  mod(2), mod(3), mod(4), full(norm2), full(wa), full(wb), full(wr), full(br), full(tri)],
        out_specs=[pl.BlockSpec((rows, d), lambda i: (i, 0)),
                   pl.BlockSpec((rows * (d // LANES), LANES), lambda i: (i, 0)),
                   pl.BlockSpec((SUBLANES, rows), lambda i: (0, i)),
                   pl.BlockSpec((rows, LANES), lambda i: (i, 0)),
                   pl.BlockSpec((ROUTE_ROWS, LANES), lambda i: (0, 0))],
        out_shape=[jax.ShapeDtypeStruct((t, d), F32),
                   jax.ShapeDtypeStruct((t * (d // LANES), LANES), F32),
                   jax.ShapeDtypeStruct((SUBLANES, t), jnp.int32),
                   jax.ShapeDtypeStruct((t, LANES), F32),
                   jax.ShapeDtypeStruct((ROUTE_ROWS, LANES), F32)],
        scratch_shapes=[pltpu.VMEM((ROUTE_ROWS, rows), F32)],
        compiler_params=pltpu.CompilerParams(dimension_semantics=("arbitrary",)),
        name="outproj_router",
    )(ssd, att, xf, mod3, mod3, mod3, norm2, wa, wb, wr, br, tri)


def _expert_kernel(blk_e_ref, nused_ref, pstart_ref, counts_ref, dst_ref, h2_hbm, wg_ref, wu_ref, wd_ref, y_ref,
                   xbuf, wgb, wub, wdb, tok_ref, sem, *, n_tok, nblocks):
    b = pl.program_id(0)
    nused = nused_ref[0]
    slot = lax.rem(b, GATHER_SLOTS)

    def row_copy(tok, r, sl):
        return pltpu.make_async_copy(h2_hbm.at[pl.ds(tok * D_TILES, D_TILES)],
                                     xbuf.at[sl, pl.ds(r * D_TILES, D_TILES)], sem.at[sl])

    def start_rows(blk, sl, r0, r1):
        for r in range(r0, r1):
            row_copy(tok_ref[blk * MOE_BLK + r], r, sl).start()

    def wait_rows(sl):
        for r in range(MOE_BLK):
            row_copy(0, r, sl).wait()

    @pl.when(b == 0)
    def _():
        def pad_expert(e, carry):
            lo = pstart_ref[e] + counts_ref[e]
            hi = pstart_ref[e] + (counts_ref[e] + MOE_BLK - 1) // MOE_BLK * MOE_BLK

            def pad_slot(s, c):
                tok_ref[s] = 0
                return c
            return lax.fori_loop(lo, hi, pad_slot, carry)
        lax.fori_loop(0, N_EXPERTS, pad_expert, 0)

        def claim(i, carry):
            tok_ref[dst_ref[i]] = jnp.where(i >= n_tok, i - n_tok, i)
            return carry
        lax.fori_loop(0, 2 * n_tok, claim, 0, unroll=16)
        start_rows(0, 0, 0, MOE_BLK)
        start_rows(jnp.minimum(1, nused - 1), 1, 0, MOE_BLK)

    e = blk_e_ref[b]
    e_prev = blk_e_ref[jnp.maximum(b - 1, 0)]

    @pl.when((b == 0) | (e != e_prev))
    def _():
        wgb[...] = wg_ref[0].astype(BF16)
        wub[...] = wu_ref[0].astype(BF16)
        wdb[...] = wd_ref[0].astype(BF16)

    nxt = jnp.minimum(b + 2, nused - 1)
    nxt_slot = lax.rem(b + 2, GATHER_SLOTS)
    wait_rows(slot)
    xb = jnp.concatenate([xbuf[slot, pl.ds(s, MOE_BLK, stride=D_TILES), :] for s in range(D_TILES)],
                         axis=1).astype(BF16)
    gate = jnp.dot(xb, wgb[...], preferred_element_type=F32)
    up = jnp.dot(xb, wub[...], preferred_element_type=F32)
    hmid = (_silu(gate) * up).astype(BF16)
    y = jnp.dot(hmid, wdb[...], preferred_element_type=F32)
    start_rows(nxt, nxt_slot, 0, MOE_BLK)
    for s in range(D_TILES):
        y_ref[pl.ds(s, MOE_BLK, stride=D_TILES), :] = y[:, s * LANES:(s + 1) * LANES]

    @pl.when(b == nblocks - 1)
    def _():
        wait_rows(lax.rem(b + 1, GATHER_SLOTS))
        wait_rows(nxt_slot)


def _experts(blk_e, nused, pstart, counts, dst, h2t, w_gate, w_up, w_down, *, nblocks, n_tok):
    d = w_gate.shape[1]

    def wspec(shape):
        return pl.BlockSpec(shape, lambda b, be, nu, ps, ct, cd: (be[b], 0, 0))

    grid_spec = pltpu.PrefetchScalarGridSpec(
        num_scalar_prefetch=5,
        grid=(nblocks,),
        in_specs=[pl.BlockSpec(memory_space=pl.ANY),
                  wspec((1, d, EXPERT_DIM)), wspec((1, d, EXPERT_DIM)), wspec((1, EXPERT_DIM, d))],
        out_specs=pl.BlockSpec((MOE_BLK * D_TILES, LANES), lambda b, be, nu, ps, ct, cd: (b, 0)),
        scratch_shapes=[
            pltpu.VMEM((GATHER_SLOTS, MOE_BLK * D_TILES, LANES), F32),
            pltpu.VMEM((d, EXPERT_DIM), BF16),
            pltpu.VMEM((d, EXPERT_DIM), BF16),
            pltpu.VMEM((EXPERT_DIM, d), BF16),
            pltpu.SMEM((nblocks * MOE_BLK,), jnp.int32),
            pltpu.SemaphoreType.DMA((GATHER_SLOTS,)),
        ],
    )
    return pl.pallas_call(
        functools.partial(_expert_kernel, n_tok=n_tok, nblocks=nblocks),
        grid_spec=grid_spec,
        out_shape=jax.ShapeDtypeStruct((nblocks * MOE_BLK * D_TILES, LANES), F32),
        compiler_params=pltpu.CompilerParams(dimension_semantics=("arbitrary",)),
        name="experts",
    )(blk_e, nused, pstart, counts, dst, h2t, w_gate, w_up, w_down)


def _combine_kernel(dst_ref, y_hbm, x1_ref, wcol_ref, g2_ref, nf_ref, o_ref, ybuf, sem, *, nsteps, n_tok):
    i = pl.program_id(0)
    rows = x1_ref.shape[0]
    slot = i % 2

    def row_copy(src, r, k, sl):
        return pltpu.make_async_copy(y_hbm.at[pl.ds(src * D_TILES, D_TILES)],
                                     ybuf.at[sl, k, pl.ds(r * D_TILES, D_TILES)], sem.at[sl])

    def start_rows(step, sl, r0, r1):
        for r in range(r0, r1):
            for k in range(2):
                row_copy(dst_ref[k * n_tok + step * rows + r], r, k, sl).start()

    def wait_rows(sl):
        for r in range(rows):
            for k in range(2):
                row_copy(0, r, k, sl).wait()

    @pl.when(i == 0)
    def _():
        start_rows(0, 0, 0, rows)

    @pl.when(i + 1 < nsteps)
    def _():
        start_rows(i + 1, 1 - slot, 0, rows)

    wait_rows(slot)
    w0 = wcol_ref[:, 0:1]
    w1 = wcol_ref[:, 1:2]
    ssq = jnp.zeros((rows, 1), F32)
    for s in range(D_TILES):
        cols = slice(s * LANES, (s + 1) * LANES)
        moe = (ybuf[slot, 0, pl.ds(s, rows, stride=D_TILES), :] * w0
               + ybuf[slot, 1, pl.ds(s, rows, stride=D_TILES), :] * w1)
        x2 = x1_ref[:, cols] + g2_ref[0, :, cols] * moe
        ssq = ssq + jnp.sum(x2 * x2, axis=-1, keepdims=True)
        o_ref[:, cols] = x2
    inv = lax.rsqrt(ssq * (1.0 / (D_TILES * LANES)) + EPS)
    o_ref[...] = o_ref[...] * inv * nf_ref[...]


def _combine(dst, y_buf, x1, wcol, mod3, norm_final, *, rows, seq_blocks):
    t, d = x1.shape
    nsteps = t // rows
    grid_spec = pltpu.PrefetchScalarGridSpec(
        num_scalar_prefetch=1,
        grid=(nsteps,),
        in_specs=[
            pl.BlockSpec(memory_space=pl.ANY),
            pl.BlockSpec((rows, d), lambda i, ds: (i, 0)),
            pl.BlockSpec((rows, LANES), lambda i, ds: (i, 0)),
            pl.BlockSpec((1, 1, d), lambda i, ds: (i // seq_blocks, 0, 5)),
            pl.BlockSpec((1, d), lambda i, ds: (0, 0)),
        ],
        out_specs=pl.BlockSpec((rows, d), lambda i, ds: (i, 0)),
        scratch_shapes=[pltpu.VMEM((2, 2, rows * D_TILES, LANES), F32), pltpu.SemaphoreType.DMA((2,))],
    )
    return pl.pallas_call(
        functools.partial(_combine_kernel, nsteps=nsteps, n_tok=t),
        grid_spec=grid_spec,
        out_shape=jax.ShapeDtypeStruct((t, d), F32),
        compiler_params=pltpu.CompilerParams(dimension_semantics=("arbitrary",)),
        name="combine",
    )(dst, y_buf, x1, wcol, mod3, norm_final)


def _rope_tables(seq):
    pos = np.arange(seq)
    n_freq = HEAD_DIM // 4
    inv = ROPE_BASE ** (-np.arange(n_freq, dtype=np.float32) / n_freq)
    ang = np.concatenate([(pos // GRID_W)[:, None] * inv, (pos % GRID_W)[:, None] * inv], axis=-1)
    ang = np.concatenate([ang, ang, ang, ang], axis=-1).astype(np.float32)
    return jnp.asarray(np.cos(ang), F32), jnp.asarray(np.sin(ang), F32)


def _head_expand_mat():
    m = np.zeros((SSD_HEADS, SSD_INNER), np.float32)
    for hh in range(SSD_HEADS):
        m[hh, hh * SSD_HEAD_DIM:(hh + 1) * SSD_HEAD_DIM] = 1.0
    return jnp.asarray(m)


def kernel(x, c, ctx, c_ctx, w_ada, b_ada, norm1, w_in, conv_w, conv_b, dt_bias, a_log, d_skip, ssd_norm,
           attn_sinks, w_out, norm2, w_group, b_group, w_expert, b_expert, w_gate, w_up, w_down, norm_final):
    return _pipeline(x, c, ctx, c_ctx, w_ada, b_ada, norm1, w_in, conv_w, conv_b, dt_bias, a_log, d_skip,
                     ssd_norm, attn_sinks, w_out, norm2, w_group, b_group, w_expert, b_expert, w_gate, w_up,
                     w_down, norm_final)["out"]


def _pipeline(x, c, ctx, c_ctx, w_ada, b_ada, norm1, w_in, conv_w, conv_b, dt_bias, a_log, d_skip, ssd_norm,
              attn_sinks, w_out, norm2, w_group, b_group, w_expert, b_expert, w_gate, w_up, w_down, norm_final):
    bt, seq, d = x.shape
    n_ctx = ctx.shape[1]
    t = bt * seq
    layer = 0
    assert w_ada.shape[0] == 1 and seq % ROWS_IN == 0 and n_ctx % CHUNK == 0

    cc = jnp.zeros((SUBLANES, d), F32).at[:bt].set(c).at[bt].set(c_ctx)
    mod = _ada(cc, w_ada[layer], b_ada[layer][None, :])
    mod3 = mod.reshape(SUBLANES, 1, 6 * d)

    w = w_in[layer]
    o_z, o_xbc, o_dt = 0, SSD_INNER, SSD_INNER + CONV_DIM
    o_q = o_dt + 2 * SSD_HEADS
    o_k, o_v = o_q + ATTN_WIDTH, o_q + ATTN_WIDTH + KV_WIDTH
    q_perm = np.concatenate([np.r_[j * HEAD_DIM:(j + 1) * HEAD_DIM, (j + 4) * HEAD_DIM:(j + 5) * HEAD_DIM]
                             for j in range(Q_PER_KV)])
    w_q = w[:, o_q:o_q + ATTN_WIDTH].reshape(d, ATTN_KV_HEADS, Q_PER_KV, HEAD_DIM).transpose(0, 2, 1, 3)
    w_q = w_q.reshape(d, ATTN_WIDTH)
    w_cat = jnp.concatenate([w[:, o_z:o_z + SSD_INNER], w[:, o_xbc:o_xbc + CONV_DIM], w_q,
                             w[:, o_k:o_k + KV_WIDTH], w[:, o_v:o_v + KV_WIDTH]], axis=1).astype(BF16)
    w_dtT = w[:, o_dt:o_dt + 2 * SSD_HEADS].T.astype(BF16)

    seq_blocks = seq // ROWS_IN
    cos, sin = _rope_tables(seq)
    bias_rows = jnp.broadcast_to(dt_bias[layer].reshape(2 * SSD_HEADS, 1), (2 * SSD_HEADS, LANES))
    alog_rows = jnp.broadcast_to(a_log[layer].reshape(2 * SSD_HEADS, 1), (2 * SSD_HEADS, LANES))
    segs_x = [("z", _COL_Z, SSD_INNER), ("xbc", _COL_XBC, CONV_DIM), ("dtT", 0, 0),
              ("q", _COL_Q, ATTN_WIDTH), ("k", _COL_K, KV_WIDTH), ("v", _COL_V, KV_WIDTH)]
    z_x, xbc_x, dt_x, pre_x, suf_x, q_x, k_x, v_x = _inproj(
        x.reshape(t, d), mod3, norm1[layer][None, :], w_cat, w_dtT, bias_rows, alog_rows, (cos, sin),
        rows=ROWS_IN, mod_row_fn=lambda i: i // seq_blocks, segs=segs_x, seq_blocks=seq_blocks)
    segs_c = [("xbc", _COL_XBC, CONV_DIM), ("dtT", 0, 0), ("k", _COL_K, KV_WIDTH), ("v", _COL_V, KV_WIDTH)]
    xbc_c, dt_c, pre_c, suf_c, k_c, v_c = _inproj(
        ctx.reshape(bt * n_ctx, d), mod3, norm1[layer][None, :], w_cat, w_dtT, bias_rows, alog_rows, None,
        rows=n_ctx, mod_row_fn=lambda i: bt, segs=segs_c, seq_blocks=1)

    cw = jnp.zeros((SUBLANES, CONV_DIM), F32).at[:CONV_WIDTH].set(conv_w[layer])
    cb = conv_b[layer][None, :]
    dskip_e = jnp.repeat(d_skip[layer], SSD_HEAD_DIM)[None, :]
    consts = (cw, cb, dskip_e, ssd_norm[layer][None, :], _head_expand_mat())

    zeros_state = jnp.zeros((bt, SSD_STATE, SSD_INNER), F32)
    z_dummy = jnp.zeros((bt, n_ctx, SSD_INNER), F32)
    h_cf, h_cb = _ssd(xbc_c.reshape(bt, n_ctx, CONV_DIM), (dt_c, pre_c, suf_c), z_dummy, consts,
                      zeros_state, zeros_state, rows=n_ctx, need_y=False)
    ssd_x, _, _ = _ssd(xbc_x.reshape(bt, seq, CONV_DIM), (dt_x, pre_x, suf_x), z_x.reshape(bt, seq, SSD_INNER),
                       consts, h_cf, h_cb, rows=ROWS_SSD, need_y=True)

    attn_x = _attention(attn_sinks[layer].astype(F32),
                        q_x.reshape(bt, seq, ATTN_WIDTH), k_x.reshape(bt, seq, KV_WIDTH),
                        v_x.reshape(bt, seq, KV_WIDTH), k_c.reshape(bt, n_ctx, KV_WIDTH),
                        v_c.reshape(bt, n_ctx, KV_WIDTH))

    wo = w_out[layer]
    wa = wo[:SSD_INNER].astype(BF16)
    wb = wo[SSD_INNER:].reshape(ATTN_KV_HEADS, Q_PER_KV, HEAD_DIM, d).transpose(1, 0, 2, 3)
    wb = wb.reshape(ATTN_WIDTH, d).astype(BF16)
    wr = jnp.zeros((ROUTE_ROWS, d), F32).at[:N_GROUPS].set(w_group[layer].T)
    wr = wr.at[N_GROUPS:N_GROUPS + N_EXPERTS].set(w_expert[layer].T).astype(BF16)
    br = jnp.zeros((ROUTE_ROWS,), F32).at[:N_GROUPS].set(b_group[layer])
    br = br.at[N_GROUPS:N_GROUPS + N_EXPERTS].set(b_expert[layer])
    br = jnp.broadcast_to(br[:, None], (ROUTE_ROWS, ROWS_IN))
    tri = jnp.asarray(np.triu(np.ones((ROWS_IN, ROWS_IN), np.float32), 1), BF16)
    x1, h2, ids, wcol, cnt = _outproj(
        ssd_x.reshape(t, SSD_INNER), attn_x.reshape(t, ATTN_WIDTH), x.reshape(t, d), mod3, norm2[layer][None, :],
        wa, wb, wr, br, tri, rows=ROWS_IN, seq_blocks=seq_blocks)

    counts = cnt[N_GROUPS:N_GROUPS + N_EXPERTS, 0].astype(jnp.int32)
    padded = (counts + MOE_BLK - 1) // MOE_BLK * MOE_BLK
    ends = jnp.cumsum(padded)
    pstart = ends - padded
    nblocks = (2 * t) // MOE_BLK + N_EXPERTS
    blk_start = jnp.arange(nblocks, dtype=jnp.int32) * MOE_BLK
    blk_e = jnp.minimum(jnp.sum((ends[None, :] <= blk_start[:, None]).astype(jnp.int32), axis=1), N_EXPERTS - 1)
    nused = (ends[-1] // MOE_BLK).astype(jnp.int32).reshape(1)
    code = ids[0:2].reshape(-1)
    e_id = lax.shift_right_logical(code, RANK_BITS)
    onehot = e_id[:, None] == jnp.arange(N_EXPERTS, dtype=jnp.int32)[None, :]
    dst = jnp.sum(jnp.where(onehot, pstart[None, :], 0), axis=1) + (code & ((1 << RANK_BITS) - 1))

    y_buf = _experts(blk_e, nused, pstart, counts, dst, h2, w_gate[layer], w_up[layer], w_down[layer],
                     nblocks=nblocks, n_tok=t)
    out = _combine(dst, y_buf, x1, wcol, mod3, norm_final[None, :], rows=ROWS_CMB, seq_blocks=seq // ROWS_CMB)
    return dict(out=out.reshape(bt, seq, d), mod=mod, z=z_x, xbc=xbc_x, dt=dt_x, q=q_x, k=k_x, v=v_x,
                k_c=k_c, v_c=v_c, h_cf=h_cf, h_cb=h_cb, ssd=ssd_x, attn=attn_x, x1=x1, h2=h2, ids=ids,
                wcol=wcol, cnt=cnt, y_buf=y_buf, pstart=pstart, q_perm=q_perm)
```

```python
import functools
import math

import numpy as np
import jax
import jax.numpy as jnp
from jax import lax
from jax.experimental import pallas as pl
from jax.experimental.pallas import tpu as pltpu

F32 = jnp.float32
BF16 = jnp.bfloat16

D_MODEL = 1024
GRID_W = 64
EPS = 1e-6
SSD_INNER = 512
SSD_HEAD_DIM = 64
SSD_HEADS = 8
SSD_GROUPS = 2
SSD_STATE = 128
CONV_WIDTH = 5
CONV_DIM = SSD_INNER + 2 * SSD_GROUPS * SSD_STATE
CHUNK = 128
ATTN_WIDTH = 512
HEAD_DIM = 64
ATTN_Q_HEADS = 8
ATTN_KV_HEADS = 2
Q_PER_KV = 4
KV_WIDTH = 128
WINDOW = 128
ROPE_BASE = 10000.0
N_GROUPS = 4
EXPERTS_PER_GROUP = 8
N_EXPERTS = 32
EXPERT_DIM = 512

LANES = 128
SUBLANES = 8
NEG = -1e30

ROWS_IN = 512
ROWS_SSD = 512
ROWS_ATT = 256
ROUTE_ROWS = 48
MOE_BLK = 128
GATHER_SLOTS = 3
RANK_BITS = 16
D_TILES = D_MODEL // LANES
ROWS_CMB = 128


def _silu(v):
    half = 0.5 * v
    return half + half * jnp.tanh(half)


def _softplus(v):
    return jnp.maximum(v, 0.0) + jnp.log(1.0 + jnp.exp(-jnp.abs(v)))


def _ada_kernel(c_ref, w_ref, b_ref, o_ref):
    a = _silu(c_ref[...])
    o_ref[...] = jnp.dot(a.astype(BF16), w_ref[...].astype(BF16),
                         preferred_element_type=F32) + b_ref[...]


def _ada(cc, w_ada, b_ada):
    d, n = w_ada.shape
    tn = 1536
    return pl.pallas_call(
        _ada_kernel,
        grid=(n // tn,),
        in_specs=[pl.BlockSpec((SUBLANES, d), lambda j: (0, 0)),
                  pl.BlockSpec((d, tn), lambda j: (0, j)),
                  pl.BlockSpec((1, tn), lambda j: (0, j))],
        out_specs=pl.BlockSpec((SUBLANES, tn), lambda j: (0, j)),
        out_shape=jax.ShapeDtypeStruct((SUBLANES, n), F32),
        name="ada",
    )(cc, w_ada, b_ada)


def _rope(t, cos, sin_signed, first_half):
    rot = jnp.where(first_half, pltpu.roll(t, LANES - HEAD_DIM // 2, 1), pltpu.roll(t, HEAD_DIM // 2, 1))
    return t * cos + rot * sin_signed


def _lane_cumsum(v, lane, reverse):
    k = 1
    while k < LANES:
        if reverse:
            v = v + jnp.where(lane < LANES - k, pltpu.roll(v, LANES - k, 1), 0.0)
        else:
            v = v + jnp.where(lane >= k, pltpu.roll(v, k, 1), 0.0)
        k *= 2
    return v


def _inproj_kernel(*refs, segs, rope):
    x_ref, sh_ref, sc_ref, g_ref, w_ref, wdt_ref, bias_ref, alog_ref = refs[:8]
    pos = 8
    if rope:
        cos_ref, sin_ref = refs[8:10]
        pos = 10
    outs = iter(refs[pos:])
    x = x_ref[...]
    ms = jnp.mean(x * x, axis=-1, keepdims=True)
    h = (x * lax.rsqrt(ms + EPS)) * (g_ref[...] * (1.0 + sc_ref[0])) + sh_ref[0]
    hb = h.astype(BF16)
    if rope:
        cos = cos_ref[...]
        lane = lax.broadcasted_iota(jnp.int32, cos.shape, 1)
        first_half = (lane & (HEAD_DIM - 1)) < HEAD_DIM // 2
        sin_s = jnp.where(first_half, -sin_ref[...], sin_ref[...])
    for kind, c0, width in segs:
        if kind == "dtT":
            dt_ref, pre_ref, suf_ref = next(outs), next(outs), next(outs)
            dtt = lax.dot_general(wdt_ref[...], hb, (((1,), (1,)), ((), ())), preferred_element_type=F32)
            lane16 = lax.broadcasted_iota(jnp.int32, (2 * SSD_HEADS, CHUNK), 1)
            neg_a = -jnp.exp(alog_ref[...])
            for j in range(dt_ref.shape[0]):
                dt = _softplus(dtt[:, j * CHUNK:(j + 1) * CHUNK] + bias_ref[...])
                a = dt * neg_a
                dt_ref[j] = dt
                pre_ref[j] = _lane_cumsum(a, lane16, False)
                suf_ref[j] = _lane_cumsum(a, lane16, True)
            continue
        o_ref = next(outs)
        for j0 in range(0, width, 512):
            wj = min(512, width - j0)
            acc = jnp.dot(hb, w_ref[:, c0 + j0:c0 + j0 + wj], preferred_element_type=F32)
            if rope and kind in ("q", "k"):
                scale = HEAD_DIM ** -0.5 if kind == "q" else 1.0
                for l0 in range(0, wj, LANES):
                    t = _rope(acc[:, l0:l0 + LANES], cos, sin_s, first_half)
                    o_ref[:, j0 + l0:j0 + l0 + LANES] = (t * scale).astype(o_ref.dtype)
            else:
                o_ref[:, j0:j0 + wj] = acc.astype(o_ref.dtype)


_COL_Z, _COL_XBC, _COL_Q, _COL_K, _COL_V = 0, 512, 1536, 2048, 2176
_W_COLS = 2304


def _inproj(xf, mod3, norm1, w_cat, w_dtT, bias_rows, alog_rows, rope_tabs, *, rows, mod_row_fn, segs, seq_blocks):
    t, d = xf.shape
    rope = rope_tabs is not None
    in_specs = [
        pl.BlockSpec((rows, d), lambda i: (i, 0)),
        pl.BlockSpec((1, 1, d), lambda i: (mod_row_fn(i), 0, 0)),
        pl.BlockSpec((1, 1, d), lambda i: (mod_row_fn(i), 0, 1)),
        pl.BlockSpec((1, d), lambda i: (0, 0)),
        pl.BlockSpec(w_cat.shape, lambda i: (0, 0)),
        pl.BlockSpec(w_dtT.shape, lambda i: (0, 0)),
        pl.BlockSpec(bias_rows.shape, lambda i: (0, 0)),
        pl.BlockSpec(alog_rows.shape, lambda i: (0, 0)),
    ]
    args = [xf, mod3, mod3, norm1, w_cat, w_dtT, bias_rows, alog_rows]
    if rope:
        in_specs += [pl.BlockSpec((rows, LANES), lambda i: (i % seq_blocks, 0))] * 2
        args += list(rope_tabs)
    out_specs, out_shapes = [], []
    for kind, _, width in segs:
        if kind == "dtT":
            for _ in range(3):
                out_specs.append(pl.BlockSpec((rows // CHUNK, 2 * SSD_HEADS, CHUNK), lambda i: (i, 0, 0)))
                out_shapes.append(jax.ShapeDtypeStruct((t // CHUNK, 2 * SSD_HEADS, CHUNK), F32))
        else:
            dt = BF16 if kind in ("q", "k", "v") else F32
            out_specs.append(pl.BlockSpec((rows, width), lambda i: (i, 0)))
            out_shapes.append(jax.ShapeDtypeStruct((t, width), dt))
    return pl.pallas_call(
        functools.partial(_inproj_kernel, segs=tuple(segs), rope=rope),
        grid=(t // rows,),
        in_specs=in_specs,
        out_specs=out_specs,
        out_shape=out_shapes,
        name="inproj_rope" if rope else "inproj_ctx",
    )(*args)


_XB_COLS = SSD_INNER + SSD_GROUPS * SSD_STATE
_Y_OFF = CONV_DIM - _XB_COLS


def _ssd_kernel(*refs, need_y, nsteps, cps):
    (xbc_ref, prev_ref, next_ref, dt_ref, pre_ref, suf_ref, z_ref, cw_ref, cb_ref,
     dskip_ref, gain_ref, exp_ref, h0f_ref, h0b_ref) = refs[:14]
    if need_y:
        y_ref, hf_out, hb_out = refs[14:17]
        rest = refs[17:]
    else:
        y_ref = None
        hf_out, hb_out = refs[14:16]
        rest = refs[16:]
    cwin, u_ref, hf, hb, hbs, ust = rest

    sw = pl.program_id(1)
    s = pl.program_id(2)
    bi = jnp.where(sw == 0, nsteps - 1 - s, s)
    rows = xbc_ref.shape[1]
    h = SSD_HEADS

    @pl.when((sw == 0) & (s == 0))
    def _():
        hb[...] = h0b_ref[0]

    @pl.when((sw == 1) & (s == 0))
    def _():
        hf[...] = h0f_ref[0]

    ri = lax.broadcasted_iota(jnp.int32, (CHUNK, CHUNK), 0)
    ci = lax.broadcasted_iota(jnp.int32, (CHUNK, CHUNK), 1)
    lane128 = lax.broadcasted_iota(jnp.int32, (CHUNK, LANES), 1)
    lo_mask = lane128 < SSD_HEAD_DIM

    def conv_silu(c, c0, c1, out_ref):
        r0 = pl.multiple_of(c * CHUNK, CHUNK)
        above = xbc_ref[0, pl.ds(pl.multiple_of(jnp.maximum(r0 - SUBLANES, 0), SUBLANES), SUBLANES), c0:c1]
        halo_a = jnp.where(bi > 0, prev_ref[0, :, c0:c1], 0.0)
        cwin[0:SUBLANES, c0:c1] = jnp.where(c > 0, above, halo_a)
        cwin[SUBLANES:SUBLANES + CHUNK, c0:c1] = xbc_ref[0, pl.ds(r0, CHUNK), c0:c1]
        below = xbc_ref[0, pl.ds(pl.multiple_of(jnp.minimum(r0 + CHUNK, rows - SUBLANES), SUBLANES), SUBLANES), c0:c1]
        halo_b = jnp.where(bi < nsteps - 1, next_ref[0, :, c0:c1], 0.0)
        cwin[SUBLANES + CHUNK:, c0:c1] = jnp.where(c < cps - 1, below, halo_b)
        for j0 in range(c0, c1, LANES):
            acc = jnp.broadcast_to(cb_ref[:, j0:j0 + LANES], (CHUNK, LANES))
            for k in range(CONV_WIDTH):
                off = SUBLANES - CONV_WIDTH // 2 + k
                acc = acc + cwin[off:off + CHUNK, j0:j0 + LANES] * cw_ref[k:k + 1, j0:j0 + LANES]
            out_ref[:, j0 - c0:j0 - c0 + LANES] = _silu(acc)

    def dt_rows(c):
        return dt_ref[c], pre_ref[c], suf_ref[c]

    def state_update(state_ref, xb, w_rows, tot_col):
        dec = jnp.exp(jnp.sum(tot_col * exp_ref[...], axis=0, keepdims=True))
        for g in range(SSD_GROUPS):
            bm = xb[:, SSD_INNER + g * SSD_STATE:SSD_INNER + (g + 1) * SSD_STATE]
            bt = bm.T
            for pr in range(2):
                h0 = g * 4 + pr * 2
                c0 = h0 * SSD_HEAD_DIM
                xp = xb[:, c0:c0 + LANES]
                rhs = jnp.concatenate([jnp.where(lo_mask, xp, 0.0), jnp.where(lo_mask, 0.0, xp)],
                                      axis=0).astype(BF16)
                lhs = jnp.concatenate([bt * w_rows[h0:h0 + 1, :], bt * w_rows[h0 + 1:h0 + 2, :]],
                                      axis=1).astype(BF16)
                sres = jnp.dot(lhs, rhs, preferred_element_type=F32)
                state_ref[:, c0:c0 + LANES] = state_ref[:, c0:c0 + LANES] * dec[:, c0:c0 + LANES] + sres

    def col_forms(rows16):
        padded = jnp.concatenate([rows16, jnp.zeros((CHUNK - 2 * h, LANES), F32)], axis=0)
        return padded.T

    def backward_chunk(k, carry):
        c = cps - 1 - k
        gc = bi * cps + c
        xb = ust.at[gc]
        conv_silu(c, 0, _XB_COLS, xb)
        dt, _, suf = dt_rows(c)
        hbs[gc] = hb[...].astype(BF16)
        s0 = suf[:, 0:1]
        w_rows = jnp.exp(s0 - suf) * dt
        state_update(hb, xb, w_rows[h:2 * h, :], s0[h:2 * h, :])
        return carry

    def forward_chunk(c, carry):
        gc = bi * cps + c
        r0 = pl.multiple_of(c * CHUNK, CHUNK)
        xb = ust.at[gc]
        conv_silu(c, _XB_COLS, CONV_DIM, u_ref)
        dt, pre, suf = dt_rows(c)
        rowsf = jnp.concatenate([pre[0:h, :], suf[h:2 * h, :]], axis=0)
        colsf = col_forms(rowsf)
        ecol = jnp.exp(colsf)
        for g in range(SSD_GROUPS):
            bm = xb[:, SSD_INNER + g * SSD_STATE:SSD_INNER + (g + 1) * SSD_STATE]
            cm = u_ref[:, g * SSD_STATE:(g + 1) * SSD_STATE]
            cmb = cm.astype(BF16)
            cbm = lax.dot_general(cmb, bm.astype(BF16), (((1,), (1,)), ((), ())),
                                  preferred_element_type=F32)
            for pr in range(2):
                h0 = g * 4 + pr * 2
                c0 = h0 * SSD_HEAD_DIM
                lhs_parts = []
                for hh in (h0, h0 + 1):
                    segf = colsf[:, hh:hh + 1] - rowsf[hh:hh + 1, :]
                    gf = jnp.exp(jnp.where(ci <= ri, segf, NEG)) * dt[hh:hh + 1, :]
                    segb = colsf[:, h + hh:h + hh + 1] - rowsf[h + hh:h + hh + 1, :]
                    gb = jnp.exp(jnp.where(ci >= ri, segb, NEG)) * dt[h + hh:h + hh + 1, :]
                    lhs_parts.append((cbm * (gf + gb)).astype(BF16))
                for hh in (h0, h0 + 1):
                    lhs_parts.append((cm * ecol[:, hh:hh + 1]).astype(BF16))
                for hh in (h0, h0 + 1):
                    lhs_parts.append((cm * ecol[:, h + hh:h + hh + 1]).astype(BF16))
                lhs = jnp.concatenate(lhs_parts, axis=1)
                xp = xb[:, c0:c0 + LANES]
                sf = hf[:, c0:c0 + LANES]
                sb = hbs[gc, :, c0:c0 + LANES].astype(F32)
                rhs = jnp.concatenate(
                    [jnp.where(lo_mask, xp, 0.0), jnp.where(lo_mask, 0.0, xp),
                     jnp.where(lo_mask, sf, 0.0), jnp.where(lo_mask, 0.0, sf),
                     jnp.where(lo_mask, sb, 0.0), jnp.where(lo_mask, 0.0, sb)], axis=0).astype(BF16)
                ypair = jnp.dot(lhs, rhs, preferred_element_type=F32)
                ypair = ypair + dskip_ref[:, c0:c0 + LANES] * xp
                zz = z_ref[0, pl.ds(r0, CHUNK), c0:c0 + LANES]
                u_ref[:, _Y_OFF + c0:_Y_OFF + c0 + LANES] = ypair * _silu(zz)
        last = pre[:, LANES - 1:LANES]
        w_rows = jnp.exp(last - pre) * dt
        state_update(hf, xb, w_rows[0:h, :], last[0:h, :])
        if need_y:
            yv = u_ref[:, _Y_OFF:_Y_OFF + SSD_INNER]
            ms = jnp.mean(yv * yv, axis=-1, keepdims=True)
            y_ref[0, pl.ds(r0, CHUNK), :] = (yv * lax.rsqrt(ms + EPS) * gain_ref[...]).astype(y_ref.dtype)
        return carry

    @pl.when(sw == 0)
    def _():
        lax.fori_loop(0, cps, backward_chunk, 0)

    @pl.when(sw == 1)
    def _():
        lax.fori_loop(0, cps, forward_chunk, 0)

    @pl.when((sw == 0) & (s == nsteps - 1))
    def _():
        hb_out[0] = hb[...]

    @pl.when((sw == 1) & (s == nsteps - 1))
    def _():
        hf_out[0] = hf[...]


def _ssd(xbc, dts, z, consts, h0f, h0b, *, rows, need_y):
    bt, seq, _ = xbc.shape
    nsteps = seq // rows
    cps = rows // CHUNK
    hb8 = rows // SUBLANES
    nb8 = seq // SUBLANES
    cw, cb, dskip, gain, expm = consts
    dt_spec = pl.BlockSpec((cps, 2 * SSD_HEADS, CHUNK), lambda b, sw, s: (b * nsteps + blk(b, sw, s), 0, 0))

    def blk(b, sw, s):
        return jnp.where(sw == 0, nsteps - 1 - s, s)

    def full(a):
        return pl.BlockSpec(a.shape, lambda b, sw, s: (0,) * a.ndim)

    in_specs = [
        pl.BlockSpec((1, rows, CONV_DIM), lambda b, sw, s: (b, blk(b, sw, s), 0)),
        pl.BlockSpec((1, SUBLANES, CONV_DIM), lambda b, sw, s: (b, jnp.maximum(blk(b, sw, s) * hb8 - 1, 0), 0)),
        pl.BlockSpec((1, SUBLANES, CONV_DIM),
                     lambda b, sw, s: (b, jnp.minimum((blk(b, sw, s) + 1) * hb8, nb8 - 1), 0)),
        dt_spec, dt_spec, dt_spec,
        pl.BlockSpec((1, rows, SSD_INNER), lambda b, sw, s: (b, jnp.where(sw == 0, 0, s), 0)),
        full(cw), full(cb), full(dskip), full(gain), full(expm),
        pl.BlockSpec((1, SSD_STATE, SSD_INNER), lambda b, sw, s: (b, 0, 0)),
        pl.BlockSpec((1, SSD_STATE, SSD_INNER), lambda b, sw, s: (b, 0, 0)),
    ]
    st_spec = pl.BlockSpec((1, SSD_STATE, SSD_INNER), lambda b, sw, s: (b, 0, 0))
    st_shape = jax.ShapeDtypeStruct((bt, SSD_STATE, SSD_INNER), F32)
    out_specs, out_shapes = [st_spec, st_spec], [st_shape, st_shape]
    if need_y:
        out_specs = [pl.BlockSpec((1, rows, SSD_INNER), lambda b, sw, s: (b, jnp.where(sw == 0, 0, s), 0))] + out_specs
        out_shapes = [jax.ShapeDtypeStruct((bt, seq, SSD_INNER), BF16)] + out_shapes
    scratch = [
        pltpu.VMEM((CHUNK + 2 * SUBLANES, CONV_DIM), F32),
        pltpu.VMEM((CHUNK, _Y_OFF + SSD_INNER), F32),
        pltpu.VMEM((SSD_STATE, SSD_INNER), F32),
        pltpu.VMEM((SSD_STATE, SSD_INNER), F32),
        pltpu.VMEM((seq // CHUNK, SSD_STATE, SSD_INNER), BF16),
        pltpu.VMEM((seq // CHUNK, CHUNK, _XB_COLS), F32),
    ]
    return pl.pallas_call(
        functools.partial(_ssd_kernel, need_y=need_y, nsteps=nsteps, cps=cps),
        grid=(bt, 2, nsteps),
        in_specs=in_specs,
        out_specs=out_specs,
        out_shape=out_shapes,
        scratch_shapes=scratch,
        compiler_params=pltpu.CompilerParams(dimension_semantics=("arbitrary", "arbitrary", "arbitrary")),
        name="ssd_y" if need_y else "ssd_ctx",
    )(xbc, xbc, xbc, *dts, z, cw, cb, dskip, gain, expm, h0f, h0b)


def _attn_kernel(sink_ref, q_ref, kp_ref, kc_ref, kn_ref, vp_ref, vc_ref, vn_ref, kx_ref, vx_ref, o_ref, *, nsteps):
    m = pl.program_id(1)
    lane = lax.broadcasted_iota(jnp.int32, (1, LANES), 1)
    lo = lane < HEAD_DIM
    rq = Q_PER_KV * WINDOW
    qi = lax.broadcasted_iota(jnp.int32, (rq, WINDOW), 0) & (WINDOW - 1)
    kj = lax.broadcasted_iota(jnp.int32, (rq, WINDOW), 1)
    rblk = lax.shift_right_logical(lax.broadcasted_iota(jnp.int32, (rq, 1), 0), int(math.log2(WINDOW)))
    zero = jnp.zeros((), BF16)

    kpieces = [kp_ref[0], kc_ref[0, 0:WINDOW], kc_ref[0, WINDOW:2 * WINDOW], kn_ref[0]]
    vpieces = [vp_ref[0], vc_ref[0, 0:WINDOW], vc_ref[0, WINDOW:2 * WINDOW], vn_ref[0]]
    kx = kx_ref[0]
    vx = vx_ref[0]
    nt = (((1,), (1,)), ((), ()))

    for sb in range(ROWS_ATT // WINDOW):
        qsb = q_ref[0, sb * WINDOW:(sb + 1) * WINDOW, :]
        prev_ok = jnp.logical_or(m > 0, sb > 0)
        next_ok = jnp.logical_or(m < nsteps - 1, sb < ROWS_ATT // WINDOW - 1)
        outs = [None] * Q_PER_KV
        for kv in range(ATTN_KV_HEADS):
            sel = lo if kv == 0 else jnp.logical_not(lo)
            qs = jnp.concatenate(
                [jnp.where(sel, qsb[:, j * LANES:(j + 1) * LANES], zero) for j in range(Q_PER_KV)], axis=0)
            s_prev = lax.dot_general(qs, kpieces[sb], nt, preferred_element_type=F32)
            s_cur = lax.dot_general(qs, kpieces[sb + 1], nt, preferred_element_type=F32)
            s_next = lax.dot_general(qs, kpieces[sb + 2], nt, preferred_element_type=F32)
            s_ctx = lax.dot_general(qs, kx, nt, preferred_element_type=F32)
            s_prev = jnp.where((kj >= qi) & prev_ok, s_prev, NEG)
            s_next = jnp.where((kj <= qi) & next_ok, s_next, NEG)
            sink = jnp.zeros((rq, 1), F32)
            for j in range(Q_PER_KV):
                sink = jnp.where(rblk == j, sink_ref[kv * Q_PER_KV + j], sink)
            n_ct = s_ctx.shape[1] // LANES
            tile_max = jnp.maximum(jnp.maximum(s_prev, s_cur), s_next)
            for t in range(n_ct):
                tile_max = jnp.maximum(tile_max, s_ctx[:, t * LANES:(t + 1) * LANES])
            mx = jnp.maximum(jnp.max(tile_max, axis=-1, keepdims=True), sink)
            p_prev = jnp.exp(s_prev - mx)
            p_cur = jnp.exp(s_cur - mx)
            p_next = jnp.exp(s_next - mx)
            p_ctx = jnp.exp(s_ctx - mx)
            tile_sum = p_prev + p_cur + p_next
            for t in range(n_ct):
                tile_sum = tile_sum + p_ctx[:, t * LANES:(t + 1) * LANES]
            den = jnp.sum(tile_sum, axis=-1, keepdims=True) + jnp.exp(sink - mx)
            acc = jnp.dot(p_prev.astype(BF16), jnp.where(sel, vpieces[sb], zero), preferred_element_type=F32)
            acc += jnp.dot(p_cur.astype(BF16), jnp.where(sel, vpieces[sb + 1], zero), preferred_element_type=F32)
            acc += jnp.dot(p_next.astype(BF16), jnp.where(sel, vpieces[sb + 2], zero), preferred_element_type=F32)
            acc += jnp.dot(p_ctx.astype(BF16), jnp.where(sel, vx, zero), preferred_element_type=F32)
            acc = acc * (1.0 / den)
            for j in range(Q_PER_KV):
                part = acc[j * WINDOW:(j + 1) * WINDOW, :]
                outs[j] = part if outs[j] is None else outs[j] + part
        for j in range(Q_PER_KV):
            o_ref[0, sb * WINDOW:(sb + 1) * WINDOW, j * LANES:(j + 1) * LANES] = outs[j].astype(o_ref.dtype)


def _attention(sinks, q, k, v, kx, vx):
    bt, seq, _ = q.shape
    nsteps = seq // ROWS_ATT
    per = ROWS_ATT // WINDOW
    nb = seq // WINDOW
    n_ctx = kx.shape[1]

    def prev(b, m, s):
        return (b, jnp.maximum(m * per - 1, 0), 0)

    def cur(b, m, s):
        return (b, m, 0)

    def nxt(b, m, s):
        return (b, jnp.minimum((m + 1) * per, nb - 1), 0)

    small = (1, WINDOW, KV_WIDTH)
    big = (1, ROWS_ATT, KV_WIDTH)
    grid_spec = pltpu.PrefetchScalarGridSpec(
        num_scalar_prefetch=1,
        grid=(bt, nsteps),
        in_specs=[
            pl.BlockSpec((1, ROWS_ATT, ATTN_WIDTH), cur),
            pl.BlockSpec(small, prev), pl.BlockSpec(big, cur), pl.BlockSpec(small, nxt),
            pl.BlockSpec(small, prev), pl.BlockSpec(big, cur), pl.BlockSpec(small, nxt),
            pl.BlockSpec((1, n_ctx, KV_WIDTH), lambda b, m, s: (b, 0, 0)),
            pl.BlockSpec((1, n_ctx, KV_WIDTH), lambda b, m, s: (b, 0, 0)),
        ],
        out_specs=pl.BlockSpec((1, ROWS_ATT, ATTN_WIDTH), cur),
    )
    return pl.pallas_call(
        functools.partial(_attn_kernel, nsteps=nsteps),
        grid_spec=grid_spec,
        out_shape=jax.ShapeDtypeStruct((bt, seq, ATTN_WIDTH), BF16),
        name="attn",
    )(sinks, q, k, k, k, v, v, v, kx, vx)


def _outproj_kernel(ssd_ref, att_ref, x_ref, g1_ref, sh_ref, sc_ref, n2_ref, wa_ref, wb_ref, wr_ref, br_ref,
                    tri_ref, x1_ref, h2_ref, ids_ref, wcol_ref, cnt_ref, carry, *, nsteps):
    i = pl.program_id(0)
    rows = x_ref.shape[0]

    @pl.when(i == 0)
    def _():
        carry[...] = jnp.zeros_like(carry)

    acc = jnp.dot(ssd_ref[...], wa_ref[...], preferred_element_type=F32)
    acc += jnp.dot(att_ref[...], wb_ref[...], preferred_element_type=F32)
    x1 = x_ref[...] + g1_ref[0] * acc
    x1_ref[...] = x1
    ms = jnp.mean(x1 * x1, axis=-1, keepdims=True)
    h2 = (x1 * lax.rsqrt(ms + EPS)) * (n2_ref[...] * (1.0 + sc_ref[0])) + sh_ref[0]
    h2b = h2.astype(BF16)
    half = h2.shape[1] // 2
    lo_bits = lax.shift_right_logical(pltpu.bitcast(h2b[:, :half].astype(F32), jnp.uint32), jnp.uint32(16))
    hi_bits = pltpu.bitcast(h2b[:, half:].astype(F32), jnp.uint32) & jnp.uint32(0xFFFF0000)
    h2_ref[...] = hi_bits | lo_bits
    lt = lax.dot_general(wr_ref[...], h2b, (((1,), (1,)), ((), ())),
                         preferred_element_type=F32) + br_ref[...]
    row = lax.broadcasted_iota(jnp.int32, lt.shape, 0).astype(F32)
    big_i = float(ROUTE_ROWS)
    gl = jnp.where(row < N_GROUPS, lt, NEG)
    gmax = jnp.max(gl, axis=0, keepdims=True)
    gidx = jnp.min(jnp.where(gl == gmax, row, big_i), axis=0, keepdims=True)
    g_w = 1.0 / jnp.sum(jnp.exp(gl - gmax), axis=0, keepdims=True)
    lo = N_GROUPS + EXPERTS_PER_GROUP * gidx
    el = jnp.where((row >= lo) & (row < lo + EXPERTS_PER_GROUP), lt, NEG)
    m1 = jnp.max(el, axis=0, keepdims=True)
    i1 = jnp.min(jnp.where(el == m1, row, big_i), axis=0, keepdims=True)
    el2 = jnp.where(row == i1, NEG, el)
    m2 = jnp.max(el2, axis=0, keepdims=True)
    i2 = jnp.min(jnp.where(el2 == m2, row, big_i), axis=0, keepdims=True)
    r = jnp.exp(m2 - m1)
    w1 = g_w / (1.0 + r)
    w2 = g_w * r / (1.0 + r)
    oh1 = row == i1
    oh2 = row == i2
    cnt = jnp.where(oh1 | oh2, 1.0, 0.0)
    prefix = jnp.dot(cnt.astype(BF16), tri_ref[...], preferred_element_type=F32)
    base = carry[...] + prefix
    rank1 = jnp.sum(jnp.where(oh1, base, 0.0), axis=0, keepdims=True)
    rank2 = jnp.sum(jnp.where(oh2, base, 0.0), axis=0, keepdims=True)
    carry[...] = carry[...] + jnp.sum(cnt, axis=1, keepdims=True)
    r8 = lax.broadcasted_iota(jnp.int32, (SUBLANES, rows), 0)
    code1 = (i1 - N_GROUPS) * float(1 << RANK_BITS) + rank1
    code2 = (i2 - N_GROUPS) * float(1 << RANK_BITS) + rank2
    ids_ref[...] = jnp.where(r8 == 0, code1, jnp.where(r8 == 1, code2, 0.0)).astype(jnp.int32)
    r128 = lax.broadcasted_iota(jnp.int32, (LANES, rows), 0)
    wfull = jnp.where(r128 == 0, w1, jnp.where(r128 == 1, w2, 0.0))
    for j in range(rows // LANES):
        wcol_ref[j * LANES:(j + 1) * LANES, :] = wfull[:, j * LANES:(j + 1) * LANES].T

    @pl.when(i == nsteps - 1)
    def _():
        cnt_ref[...] = carry[:, 0:LANES]


def _outproj(ssd, att, xf, mod3, norm2, wa, wb, wr, br, tri, *, rows, seq_blocks):
    t, d = xf.shape
    nsteps = t // rows

    def mod(col):
        return pl.BlockSpec((1, 1, d), lambda i: (i // seq_blocks, 0, col))

    def full(a):
        return pl.BlockSpec(a.shape, lambda i: (0,) * a.ndim)

    return pl.pallas_call(
        functools.partial(_outproj_kernel, nsteps=nsteps),
        grid=(nsteps,),
        in_specs=[pl.BlockSpec((rows, SSD_INNER), lambda i: (i, 0)),
                  pl.BlockSpec((rows, ATTN_WIDTH), lambda i: (i, 0)),
                  pl.BlockSpec((rows, d), lambda i: (i, 0)),
                  mod(2), mod(3), mod(4), full(norm2), full(wa), full(wb), full(wr), full(br), full(tri)],
        out_specs=[pl.BlockSpec((rows, d), lambda i: (i, 0)),
                   pl.BlockSpec((rows, d // 2), lambda i: (i, 0)),
                   pl.BlockSpec((SUBLANES, rows), lambda i: (0, i)),
                   pl.BlockSpec((rows, LANES), lambda i: (i, 0)),
                   pl.BlockSpec((ROUTE_ROWS, LANES), lambda i: (0, 0))],
        out_shape=[jax.ShapeDtypeStruct((t, d), F32),
                   jax.ShapeDtypeStruct((t, d // 2), jnp.uint32),
                   jax.ShapeDtypeStruct((SUBLANES, t), jnp.int32),
                   jax.ShapeDtypeStruct((t, LANES), F32),
                   jax.ShapeDtypeStruct((ROUTE_ROWS, LANES), F32)],
        scratch_shapes=[pltpu.VMEM((ROUTE_ROWS, rows), F32)],
        compiler_params=pltpu.CompilerParams(dimension_semantics=("arbitrary",)),
        name="outproj_router",
    )(ssd, att, xf, mod3, mod3, mod3, norm2, wa, wb, wr, br, tri)


def _expert_kernel(blk_e_ref, nused_ref, pstart_ref, counts_ref, dst_ref, h2_hbm, wg_ref, wu_ref, wd_ref, y_ref,
                   h2v, xg, wgb, wub, wdb, tok_ref, sem, *, n_tok, nblocks):
    b = pl.program_id(0)
    nused = nused_ref[0]
    slot = b % 2
    half = h2v.shape[1]

    def h2_copy():
        return pltpu.make_async_copy(h2_hbm, h2v, sem.at[0])

    def gather_rows(blk, sl):
        for r in range(MOE_BLK):
            xg[sl, pl.ds(r, 1), :] = h2v[pl.ds(tok_ref[blk * MOE_BLK + r], 1), :]

    @pl.when(b == 0)
    def _():
        h2_copy().start()
        def pad_expert(e, carry):
            lo = pstart_ref[e] + counts_ref[e]
            hi = pstart_ref[e] + (counts_ref[e] + MOE_BLK - 1) // MOE_BLK * MOE_BLK

            def pad_slot(s, c):
                tok_ref[s] = 0
                return c
            return lax.fori_loop(lo, hi, pad_slot, carry)
        lax.fori_loop(0, N_EXPERTS, pad_expert, 0)

        def claim(i, carry):
            tok_ref[dst_ref[i]] = jnp.where(i >= n_tok, i - n_tok, i)
            return carry
        lax.fori_loop(0, 2 * n_tok, claim, 0, unroll=16)
        h2_copy().wait()
        gather_rows(0, 0)

    e = blk_e_ref[b]
    e_prev = blk_e_ref[jnp.maximum(b - 1, 0)]

    @pl.when((b == 0) | (e != e_prev))
    def _():
        wgb[...] = wg_ref[0].astype(BF16)
        wub[...] = wu_ref[0].astype(BF16)
        wdb[...] = wd_ref[0].astype(BF16)

    @pl.when(b < nused)
    def _():
        words = xg[slot]
        x_lo = pltpu.bitcast(words << jnp.uint32(16), F32).astype(BF16)
        x_hi = pltpu.bitcast(words & jnp.uint32(0xFFFF0000), F32).astype(BF16)
        gate = (jnp.dot(x_lo, wgb[0:half, :], preferred_element_type=F32)
                + jnp.dot(x_hi, wgb[half:, :], preferred_element_type=F32))
        up = (jnp.dot(x_lo, wub[0:half, :], preferred_element_type=F32)
              + jnp.dot(x_hi, wub[half:, :], preferred_element_type=F32))
        hmid = (_silu(gate) * up).astype(BF16)
        y = jnp.dot(hmid, wdb[...], preferred_element_type=F32)
        for s in range(D_TILES):
            y_ref[pl.ds(s, MOE_BLK, stride=D_TILES), :] = y[:, s * LANES:(s + 1) * LANES]
        gather_rows(jnp.minimum(b + 1, nused - 1), 1 - slot)

    @pl.when(b >= nused)
    def _():
        y_ref[...] = jnp.zeros_like(y_ref)


def _experts(blk_e, nused, pstart, counts, dst, h2p, w_gate, w_up, w_down, *, nblocks, n_tok):
    d = w_gate.shape[1]

    def wspec(shape):
        return pl.BlockSpec(shape, lambda b, be, nu, ps, ct, cd: (be[b], 0, 0))

    grid_spec = pltpu.PrefetchScalarGridSpec(
        num_scalar_prefetch=5,
        grid=(nblocks,),
        in_specs=[pl.BlockSpec(memory_space=pl.ANY),
                  wspec((1, d, EXPERT_DIM)), wspec((1, d, EXPERT_DIM)), wspec((1, EXPERT_DIM, d))],
        out_specs=pl.BlockSpec((MOE_BLK * D_TILES, LANES), lambda b, be, nu, ps, ct, cd: (b, 0)),
        scratch_shapes=[
            pltpu.VMEM(h2p.shape, jnp.uint32),
            pltpu.VMEM((2, MOE_BLK, d // 2), jnp.uint32),
            pltpu.VMEM((d, EXPERT_DIM), BF16),
            pltpu.VMEM((d, EXPERT_DIM), BF16),
            pltpu.VMEM((EXPERT_DIM, d), BF16),
            pltpu.SMEM((nblocks * MOE_BLK,), jnp.int32),
            pltpu.SemaphoreType.DMA((1,)),
        ],
    )
    return pl.pallas_call(
        functools.partial(_expert_kernel, n_tok=n_tok, nblocks=nblocks),
        grid_spec=grid_spec,
        out_shape=jax.ShapeDtypeStruct((nblocks * MOE_BLK * D_TILES, LANES), F32),
        compiler_params=pltpu.CompilerParams(dimension_semantics=("arbitrary",)),
        name="experts",
    )(blk_e, nused, pstart, counts, dst, h2p, w_gate, w_up, w_down)


def _combine_kernel(dst_ref, y_hbm, x1_ref, wcol_ref, g2_ref, nf_ref, o_ref, ybuf, sem, *, nsteps, n_tok):
    i = pl.program_id(0)
    rows = x1_ref.shape[0]
    slot = i % 2

    def row_copy(src, r, k, sl):
        return pltpu.make_async_copy(y_hbm.at[pl.ds(src * D_TILES, D_TILES)],
                                     ybuf.at[sl, k, pl.ds(r * D_TILES, D_TILES)], sem.at[sl])

    def start_rows(step, sl, r0, r1):
        for r in range(r0, r1):
            for k in range(2):
                row_copy(dst_ref[k * n_tok + step * rows + r], r, k, sl).start()

    def wait_rows(sl):
        for r in range(rows):
            for k in range(2):
                row_copy(0, r, k, sl).wait()

    @pl.when(i == 0)
    def _():
        start_rows(0, 0, 0, rows)

    @pl.when(i + 1 < nsteps)
    def _():
        start_rows(i + 1, 1 - slot, 0, rows)

    wait_rows(slot)
    w0 = wcol_ref[:, 0:1]
    w1 = wcol_ref[:, 1:2]
    ssq = jnp.zeros((rows, 1), F32)
    for s in range(D_TILES):
        cols = slice(s * LANES, (s + 1) * LANES)
        moe = (ybuf[slot, 0, pl.ds(s, rows, stride=D_TILES), :] * w0
               + ybuf[slot, 1, pl.ds(s, rows, stride=D_TILES), :] * w1)
        x2 = x1_ref[:, cols] + g2_ref[0, :, cols] * moe
        ssq = ssq + jnp.sum(x2 * x2, axis=-1, keepdims=True)
        o_ref[:, cols] = x2
    inv = lax.rsqrt(ssq * (1.0 / (D_TILES * LANES)) + EPS)
    o_ref[...] = o_ref[...] * inv * nf_ref[...]


def _combine(dst, y_buf, x1, wcol, mod3, norm_final, *, rows, seq_blocks):
    t, d = x1.shape
    nsteps = t // rows
    grid_spec = pltpu.PrefetchScalarGridSpec(
        num_scalar_prefetch=1,
        grid=(nsteps,),
        in_specs=[
            pl.BlockSpec(memory_space=pl.ANY),
            pl.BlockSpec((rows, d), lambda i, ds: (i, 0)),
            pl.BlockSpec((rows, LANES), lambda i, ds: (i, 0)),
            pl.BlockSpec((1, 1, d), lambda i, ds: (i // seq_blocks, 0, 5)),
            pl.BlockSpec((1, d), lambda i, ds: (0, 0)),
        ],
        out_specs=pl.BlockSpec((rows, d), lambda i, ds: (i, 0)),
        scratch_shapes=[pltpu.VMEM((2, 2, rows * D_TILES, LANES), F32), pltpu.SemaphoreType.DMA((2,))],
    )
    return pl.pallas_call(
        functools.partial(_combine_kernel, nsteps=nsteps, n_tok=t),
        grid_spec=grid_spec,
        out_shape=jax.ShapeDtypeStruct((t, d), F32),
        compiler_params=pltpu.CompilerParams(dimension_semantics=("arbitrary",)),
        name="combine",
    )(dst, y_buf, x1, wcol, mod3, norm_final)


def _rope_tables(seq):
    pos = np.arange(seq)
    n_freq = HEAD_DIM // 4
    inv = ROPE_BASE ** (-np.arange(n_freq, dtype=np.float32) / n_freq)
    ang = np.concatenate([(pos // GRID_W)[:, None] * inv, (pos % GRID_W)[:, None] * inv], axis=-1)
    ang = np.concatenate([ang, ang, ang, ang], axis=-1).astype(np.float32)
    return jnp.asarray(np.cos(ang), F32), jnp.asarray(np.sin(ang), F32)


def _head_expand_mat():
    m = np.zeros((SSD_HEADS, SSD_INNER), np.float32)
    for hh in range(SSD_HEADS):
        m[hh, hh * SSD_HEAD_DIM:(hh + 1) * SSD_HEAD_DIM] = 1.0
    return jnp.asarray(m)


def kernel(x, c, ctx, c_ctx, w_ada, b_ada, norm1, w_in, conv_w, conv_b, dt_bias, a_log, d_skip, ssd_norm,
           attn_sinks, w_out, norm2, w_group, b_group, w_expert, b_expert, w_gate, w_up, w_down, norm_final):
    return _pipeline(x, c, ctx, c_ctx, w_ada, b_ada, norm1, w_in, conv_w, conv_b, dt_bias, a_log, d_skip,
                     ssd_norm, attn_sinks, w_out, norm2, w_group, b_group, w_expert, b_expert, w_gate, w_up,
                     w_down, norm_final)["out"]


def _pipeline(x, c, ctx, c_ctx, w_ada, b_ada, norm1, w_in, conv_w, conv_b, dt_bias, a_log, d_skip, ssd_norm,
              attn_sinks, w_out, norm2, w_group, b_group, w_expert, b_expert, w_gate, w_up, w_down, norm_final):
    bt, seq, d = x.shape
    n_ctx = ctx.shape[1]
    t = bt * seq
    layer = 0
    assert w_ada.shape[0] == 1 and seq % ROWS_IN == 0 and n_ctx % CHUNK == 0

    cc = jnp.zeros((SUBLANES, d), F32).at[:bt].set(c).at[bt].set(c_ctx)
    mod = _ada(cc, w_ada[layer], b_ada[layer][None, :])
    mod3 = mod.reshape(SUBLANES, 1, 6 * d)

    w = w_in[layer]
    o_z, o_xbc, o_dt = 0, SSD_INNER, SSD_INNER + CONV_DIM
    o_q = o_dt + 2 * SSD_HEADS
    o_k, o_v = o_q + ATTN_WIDTH, o_q + ATTN_WIDTH + KV_WIDTH
    q_perm = np.concatenate([np.r_[j * HEAD_DIM:(j + 1) * HEAD_DIM, (j + 4) * HEAD_DIM:(j + 5) * HEAD_DIM]
                             for j in range(Q_PER_KV)])
    w_q = w[:, o_q:o_q + ATTN_WIDTH].reshape(d, ATTN_KV_HEADS, Q_PER_KV, HEAD_DIM).transpose(0, 2, 1, 3)
    w_q = w_q.reshape(d, ATTN_WIDTH)
    w_cat = jnp.concatenate([w[:, o_z:o_z + SSD_INNER], w[:, o_xbc:o_xbc + CONV_DIM], w_q,
                             w[:, o_k:o_k + KV_WIDTH], w[:, o_v:o_v + KV_WIDTH]], axis=1).astype(BF16)
    w_dtT = w[:, o_dt:o_dt + 2 * SSD_HEADS].T.astype(BF16)

    seq_blocks = seq // ROWS_IN
    cos, sin = _rope_tables(seq)
    bias_rows = jnp.broadcast_to(dt_bias[layer].reshape(2 * SSD_HEADS, 1), (2 * SSD_HEADS, LANES))
    alog_rows = jnp.broadcast_to(a_log[layer].reshape(2 * SSD_HEADS, 1), (2 * SSD_HEADS, LANES))
    segs_x = [("z", _COL_Z, SSD_INNER), ("xbc", _COL_XBC, CONV_DIM), ("dtT", 0, 0),
              ("q", _COL_Q, ATTN_WIDTH), ("k", _COL_K, KV_WIDTH), ("v", _COL_V, KV_WIDTH)]
    z_x, xbc_x, dt_x, pre_x, suf_x, q_x, k_x, v_x = _inproj(
        x.reshape(t, d), mod3, norm1[layer][None, :], w_cat, w_dtT, bias_rows, alog_rows, (cos, sin),
        rows=ROWS_IN, mod_row_fn=lambda i: i // seq_blocks, segs=segs_x, seq_blocks=seq_blocks)
    segs_c = [("xbc", _COL_XBC, CONV_DIM), ("dtT", 0, 0), ("k", _COL_K, KV_WIDTH), ("v", _COL_V, KV_WIDTH)]
    xbc_c, dt_c, pre_c, suf_c, k_c, v_c = _inproj(
        ctx.reshape(bt * n_ctx, d), mod3, norm1[layer][None, :], w_cat, w_dtT, bias_rows, alog_rows, None,
        rows=n_ctx, mod_row_fn=lambda i: bt, segs=segs_c, seq_blocks=1)

    cw = jnp.zeros((SUBLANES, CONV_DIM), F32).at[:CONV_WIDTH].set(conv_w[layer])
    cb = conv_b[layer][None, :]
    dskip_e = jnp.repeat(d_skip[layer], SSD_HEAD_DIM)[None, :]
    consts = (cw, cb, dskip_e, ssd_norm[layer][None, :], _head_expand_mat())

    zeros_state = jnp.zeros((bt, SSD_STATE, SSD_INNER), F32)
    z_dummy = jnp.zeros((bt, n_ctx, SSD_INNER), F32)
    h_cf, h_cb = _ssd(xbc_c.reshape(bt, n_ctx, CONV_DIM), (dt_c, pre_c, suf_c), z_dummy, consts,
                      zeros_state, zeros_state, rows=n_ctx, need_y=False)
    ssd_x, _, _ = _ssd(xbc_x.reshape(bt, seq, CONV_DIM), (dt_x, pre_x, suf_x), z_x.reshape(bt, seq, SSD_INNER),
                       consts, h_cf, h_cb, rows=ROWS_SSD, need_y=True)

    attn_x = _attention(attn_sinks[layer].astype(F32),
                        q_x.reshape(bt, seq, ATTN_WIDTH), k_x.reshape(bt, seq, KV_WIDTH),
                        v_x.reshape(bt, seq, KV_WIDTH), k_c.reshape(bt, n_ctx, KV_WIDTH),
                        v_c.reshape(bt, n_ctx, KV_WIDTH))

    wo = w_out[layer]
    wa = wo[:SSD_INNER].astype(BF16)
    wb = wo[SSD_INNER:].reshape(ATTN_KV_HEADS, Q_PER_KV, HEAD_DIM, d).transpose(1, 0, 2, 3)
    wb = wb.reshape(ATTN_WIDTH, d).astype(BF16)
    wr = jnp.zeros((ROUTE_ROWS, d), F32).at[:N_GROUPS].set(w_group[layer].T)
    wr = wr.at[N_GROUPS:N_GROUPS + N_EXPERTS].set(w_expert[layer].T).astype(BF16)
    br = jnp.zeros((ROUTE_ROWS,), F32).at[:N_GROUPS].set(b_group[layer])
    br = br.at[N_GROUPS:N_GROUPS + N_EXPERTS].set(b_expert[layer])
    br = jnp.broadcast_to(br[:, None], (ROUTE_ROWS, ROWS_IN))
    tri = jnp.asarray(np.triu(np.ones((ROWS_IN, ROWS_IN), np.float32), 1), BF16)
    x1, h2, ids, wcol, cnt = _outproj(
        ssd_x.reshape(t, SSD_INNER), attn_x.reshape(t, ATTN_WIDTH), x.reshape(t, d), mod3, norm2[layer][None, :],
        wa, wb, wr, br, tri, rows=ROWS_IN, seq_blocks=seq_blocks)

    counts = cnt[N_GROUPS:N_GROUPS + N_EXPERTS, 0].astype(jnp.int32)
    padded = (counts + MOE_BLK - 1) // MOE_BLK * MOE_BLK
    ends = jnp.cumsum(padded)
    pstart = ends - padded
    nblocks = (2 * t) // MOE_BLK + N_EXPERTS
    blk_start = jnp.arange(nblocks, dtype=jnp.int32) * MOE_BLK
    blk_e = jnp.minimum(jnp.sum((ends[None, :] <= blk_start[:, None]).astype(jnp.int32), axis=1), N_EXPERTS - 1)
    nused = (ends[-1] // MOE_BLK).astype(jnp.int32).reshape(1)
    code = ids[0:2].reshape(-1)
    e_id = lax.shift_right_logical(code, RANK_BITS)
    onehot = e_id[:, None] == jnp.arange(N_EXPERTS, dtype=jnp.int32)[None, :]
    dst = jnp.sum(jnp.where(onehot, pstart[None, :], 0), axis=1) + (code & ((1 << RANK_BITS) - 1))

    y_buf = _experts(blk_e, nused, pstart, counts, dst, h2, w_gate[layer], w_up[layer], w_down[layer],
                     nblocks=nblocks, n_tok=t)
    out = _combine(dst, y_buf, x1, wcol, mod3, norm_final[None, :], rows=ROWS_CMB, seq_blocks=seq // ROWS_CMB)
    return dict(out=out.reshape(bt, seq, d), mod=mod, z=z_x, xbc=xbc_x, dt=dt_x, q=q_x, k=k_x, v=v_x,
                k_c=k_c, v_c=v_c, h_cf=h_cf, h_cb=h_cb, ssd=ssd_x, attn=attn_x, x1=x1, h2=h2, ids=ids,
                wcol=wcol, cnt=cnt, y_buf=y_buf, pstart=pstart, q_perm=q_perm)
```

```python
import functools
import math

import numpy as np
import jax
import jax.numpy as jnp
from jax import lax
from jax.experimental import pallas as pl
from jax.experimental.pallas import tpu as pltpu

F32 = jnp.float32
BF16 = jnp.bfloat16

D_MODEL = 1024
GRID_W = 64
EPS = 1e-6
SSD_INNER = 512
SSD_HEAD_DIM = 64
SSD_HEADS = 8
SSD_GROUPS = 2
SSD_STATE = 128
CONV_WIDTH = 5
CONV_DIM = SSD_INNER + 2 * SSD_GROUPS * SSD_STATE
CHUNK = 128
ATTN_WIDTH = 512
HEAD_DIM = 64
ATTN_Q_HEADS = 8
ATTN_KV_HEADS = 2
Q_PER_KV = 4
KV_WIDTH = 128
WINDOW = 128
ROPE_BASE = 10000.0
N_GROUPS = 4
EXPERTS_PER_GROUP = 8
N_EXPERTS = 32
EXPERT_DIM = 512

LANES = 128
SUBLANES = 8
NEG = -1e30

ROWS_IN = 512
ROWS_SSD = 512
ROWS_ATT = 256
ROUTE_ROWS = 48
MOE_BLK = 256
GATHER_SLOTS = 3
RANK_BITS = 16
D_TILES = D_MODEL // LANES
ROWS_CMB = 128


def _silu(v):
    half = 0.5 * v
    return half + half * jnp.tanh(half)


def _softplus(v):
    return jnp.maximum(v, 0.0) + jnp.log(1.0 + jnp.exp(-jnp.abs(v)))


def _ada_kernel(c_ref, w_ref, b_ref, o_ref):
    a = _silu(c_ref[...])
    o_ref[...] = jnp.dot(a.astype(BF16), w_ref[...].astype(BF16),
                         preferred_element_type=F32) + b_ref[...]


def _ada(cc, w_ada, b_ada):
    d, n = w_ada.shape
    tn = 1536
    return pl.pallas_call(
        _ada_kernel,
        grid=(n // tn,),
        in_specs=[pl.BlockSpec((SUBLANES, d), lambda j: (0, 0)),
                  pl.BlockSpec((d, tn), lambda j: (0, j)),
                  pl.BlockSpec((1, tn), lambda j: (0, j))],
        out_specs=pl.BlockSpec((SUBLANES, tn), lambda j: (0, j)),
        out_shape=jax.ShapeDtypeStruct((SUBLANES, n), F32),
        name="ada",
    )(cc, w_ada, b_ada)


def _rope(t, cos, sin_signed, first_half):
    rot = jnp.where(first_half, pltpu.roll(t, LANES - HEAD_DIM // 2, 1), pltpu.roll(t, HEAD_DIM // 2, 1))
    return t * cos + rot * sin_signed


def _lane_cumsum(v, lane, reverse):
    k = 1
    while k < LANES:
        if reverse:
            v = v + jnp.where(lane < LANES - k, pltpu.roll(v, LANES - k, 1), 0.0)
        else:
            v = v + jnp.where(lane >= k, pltpu.roll(v, k, 1), 0.0)
        k *= 2
    return v


def _inproj_kernel(*refs, segs, rope):
    x_ref, sh_ref, sc_ref, g_ref, w_ref, wdt_ref, bias_ref, alog_ref = refs[:8]
    pos = 8
    if rope:
        cos_ref, sin_ref = refs[8:10]
        pos = 10
    outs = iter(refs[pos:])
    x = x_ref[...]
    ms = jnp.mean(x * x, axis=-1, keepdims=True)
    h = (x * lax.rsqrt(ms + EPS)) * (g_ref[...] * (1.0 + sc_ref[0])) + sh_ref[0]
    hb = h.astype(BF16)
    if rope:
        cos = cos_ref[...]
        lane = lax.broadcasted_iota(jnp.int32, cos.shape, 1)
        first_half = (lane & (HEAD_DIM - 1)) < HEAD_DIM // 2
        sin_s = jnp.where(first_half, -sin_ref[...], sin_ref[...])
    for kind, c0, width in segs:
        if kind == "dtT":
            dt_ref, pre_ref, suf_ref = next(outs), next(outs), next(outs)
            dtt = lax.dot_general(wdt_ref[...], hb, (((1,), (1,)), ((), ())), preferred_element_type=F32)
            lane16 = lax.broadcasted_iota(jnp.int32, (2 * SSD_HEADS, CHUNK), 1)
            neg_a = -jnp.exp(alog_ref[...])
            for j in range(dt_ref.shape[0]):
                dt = _softplus(dtt[:, j * CHUNK:(j + 1) * CHUNK] + bias_ref[...])
                a = dt * neg_a
                dt_ref[j] = dt
                pre_ref[j] = _lane_cumsum(a, lane16, False)
                suf_ref[j] = _lane_cumsum(a, lane16, True)
            continue
        o_ref = next(outs)
        for j0 in range(0, width, 512):
            wj = min(512, width - j0)
            acc = jnp.dot(hb, w_ref[:, c0 + j0:c0 + j0 + wj], preferred_element_type=F32)
            if rope and kind in ("q", "k"):
                scale = HEAD_DIM ** -0.5 if kind == "q" else 1.0
                for l0 in range(0, wj, LANES):
                    t = _rope(acc[:, l0:l0 + LANES], cos, sin_s, first_half)
                    o_ref[:, j0 + l0:j0 + l0 + LANES] = (t * scale).astype(o_ref.dtype)
            else:
                o_ref[:, j0:j0 + wj] = acc.astype(o_ref.dtype)


_COL_Z, _COL_XBC, _COL_Q, _COL_K, _COL_V = 0, 512, 1536, 2048, 2176
_W_COLS = 2304


def _inproj(xf, mod3, norm1, w_cat, w_dtT, bias_rows, alog_rows, rope_tabs, *, rows, mod_row_fn, segs, seq_blocks):
    t, d = xf.shape
    rope = rope_tabs is not None
    in_specs = [
        pl.BlockSpec((rows, d), lambda i: (i, 0)),
        pl.BlockSpec((1, 1, d), lambda i: (mod_row_fn(i), 0, 0)),
        pl.BlockSpec((1, 1, d), lambda i: (mod_row_fn(i), 0, 1)),
        pl.BlockSpec((1, d), lambda i: (0, 0)),
        pl.BlockSpec(w_cat.shape, lambda i: (0, 0)),
        pl.BlockSpec(w_dtT.shape, lambda i: (0, 0)),
        pl.BlockSpec(bias_rows.shape, lambda i: (0, 0)),
        pl.BlockSpec(alog_rows.shape, lambda i: (0, 0)),
    ]
    args = [xf, mod3, mod3, norm1, w_cat, w_dtT, bias_rows, alog_rows]
    if rope:
        in_specs += [pl.BlockSpec((rows, LANES), lambda i: (i % seq_blocks, 0))] * 2
        args += list(rope_tabs)
    out_specs, out_shapes = [], []
    for kind, _, width in segs:
        if kind == "dtT":
            for _ in range(3):
                out_specs.append(pl.BlockSpec((rows // CHUNK, 2 * SSD_HEADS, CHUNK), lambda i: (i, 0, 0)))
                out_shapes.append(jax.ShapeDtypeStruct((t // CHUNK, 2 * SSD_HEADS, CHUNK), F32))
        else:
            dt = BF16 if kind in ("q", "k", "v") else F32
            out_specs.append(pl.BlockSpec((rows, width), lambda i: (i, 0)))
            out_shapes.append(jax.ShapeDtypeStruct((t, width), dt))
    return pl.pallas_call(
        functools.partial(_inproj_kernel, segs=tuple(segs), rope=rope),
        grid=(t // rows,),
        in_specs=in_specs,
        out_specs=out_specs,
        out_shape=out_shapes,
        name="inproj_rope" if rope else "inproj_ctx",
    )(*args)


_XB_COLS = SSD_INNER + SSD_GROUPS * SSD_STATE
_Y_OFF = CONV_DIM - _XB_COLS


def _ssd_kernel(*refs, need_y, nsteps, cps):
    (xbc_ref, prev_ref, next_ref, dt_ref, pre_ref, suf_ref, z_ref, cw_ref, cb_ref,
     dskip_ref, gain_ref, exp_ref, h0f_ref, h0b_ref) = refs[:14]
    if need_y:
        y_ref, hf_out, hb_out = refs[14:17]
        rest = refs[17:]
    else:
        y_ref = None
        hf_out, hb_out = refs[14:16]
        rest = refs[16:]
    cwin, u_ref, hf, hb, hbs, ust = rest

    sw = pl.program_id(1)
    s = pl.program_id(2)
    bi = jnp.where(sw == 0, nsteps - 1 - s, s)
    rows = xbc_ref.shape[1]
    h = SSD_HEADS

    @pl.when((sw == 0) & (s == 0))
    def _():
        hb[...] = h0b_ref[0]

    @pl.when((sw == 1) & (s == 0))
    def _():
        hf[...] = h0f_ref[0]

    ri = lax.broadcasted_iota(jnp.int32, (CHUNK, CHUNK), 0)
    ci = lax.broadcasted_iota(jnp.int32, (CHUNK, CHUNK), 1)
    lane128 = lax.broadcasted_iota(jnp.int32, (CHUNK, LANES), 1)
    lo_mask = lane128 < SSD_HEAD_DIM

    def conv_silu(c, c0, c1, out_ref):
        r0 = pl.multiple_of(c * CHUNK, CHUNK)
        above = xbc_ref[0, pl.ds(pl.multiple_of(jnp.maximum(r0 - SUBLANES, 0), SUBLANES), SUBLANES), c0:c1]
        halo_a = jnp.where(bi > 0, prev_ref[0, :, c0:c1], 0.0)
        cwin[0:SUBLANES, c0:c1] = jnp.where(c > 0, above, halo_a)
        cwin[SUBLANES:SUBLANES + CHUNK, c0:c1] = xbc_ref[0, pl.ds(r0, CHUNK), c0:c1]
        below = xbc_ref[0, pl.ds(pl.multiple_of(jnp.minimum(r0 + CHUNK, rows - SUBLANES), SUBLANES), SUBLANES), c0:c1]
        halo_b = jnp.where(bi < nsteps - 1, next_ref[0, :, c0:c1], 0.0)
        cwin[SUBLANES + CHUNK:, c0:c1] = jnp.where(c < cps - 1, below, halo_b)
        for j0 in range(c0, c1, LANES):
            acc = jnp.broadcast_to(cb_ref[:, j0:j0 + LANES], (CHUNK, LANES))
            for k in range(CONV_WIDTH):
                off = SUBLANES - CONV_WIDTH // 2 + k
                acc = acc + cwin[off:off + CHUNK, j0:j0 + LANES] * cw_ref[k:k + 1, j0:j0 + LANES]
            out_ref[:, j0 - c0:j0 - c0 + LANES] = _silu(acc)

    def dt_rows(c):
        return dt_ref[c], pre_ref[c], suf_ref[c]

    def state_update(state_ref, xb, w_rows, tot_col):
        dec = jnp.exp(jnp.sum(tot_col * exp_ref[...], axis=0, keepdims=True))
        for g in range(SSD_GROUPS):
            bm = xb[:, SSD_INNER + g * SSD_STATE:SSD_INNER + (g + 1) * SSD_STATE]
            bt = bm.T
            for pr in range(2):
                h0 = g * 4 + pr * 2
                c0 = h0 * SSD_HEAD_DIM
                xp = xb[:, c0:c0 + LANES]
                rhs = jnp.concatenate([jnp.where(lo_mask, xp, 0.0), jnp.where(lo_mask, 0.0, xp)],
                                      axis=0).astype(BF16)
                lhs = jnp.concatenate([bt * w_rows[h0:h0 + 1, :], bt * w_rows[h0 + 1:h0 + 2, :]],
                                      axis=1).astype(BF16)
                sres = jnp.dot(lhs, rhs, preferred_element_type=F32)
                state_ref[:, c0:c0 + LANES] = state_ref[:, c0:c0 + LANES] * dec[:, c0:c0 + LANES] + sres

    def col_forms(rows16):
        padded = jnp.concatenate([rows16, jnp.zeros((CHUNK - 2 * h, LANES), F32)], axis=0)
        return padded.T

    def backward_chunk(k, carry):
        c = cps - 1 - k
        gc = bi * cps + c
        xb = ust.at[gc]
        conv_silu(c, 0, _XB_COLS, xb)
        dt, _, suf = dt_rows(c)
        hbs[gc] = hb[...].astype(BF16)
        s0 = suf[:, 0:1]
        w_rows = jnp.exp(s0 - suf) * dt
        state_update(hb, xb, w_rows[h:2 * h, :], s0[h:2 * h, :])
        return carry

    def forward_chunk(c, carry):
        gc = bi * cps + c
        r0 = pl.multiple_of(c * CHUNK, CHUNK)
        xb = ust.at[gc]
        conv_silu(c, _XB_COLS, CONV_DIM, u_ref)
        dt, pre, suf = dt_rows(c)
        rowsf = jnp.concatenate([pre[0:h, :], suf[h:2 * h, :]], axis=0)
        colsf = col_forms(rowsf)
        ecol = jnp.exp(colsf)
        for g in range(SSD_GROUPS):
            bm = xb[:, SSD_INNER + g * SSD_STATE:SSD_INNER + (g + 1) * SSD_STATE]
            cm = u_ref[:, g * SSD_STATE:(g + 1) * SSD_STATE]
            cmb = cm.astype(BF16)
            cbm = lax.dot_general(cmb, bm.astype(BF16), (((1,), (1,)), ((), ())),
                                  preferred_element_type=F32)
            for pr in range(2):
                h0 = g * 4 + pr * 2
                c0 = h0 * SSD_HEAD_DIM
                lhs_parts = []
                for hh in (h0, h0 + 1):
                    segf = colsf[:, hh:hh + 1] - rowsf[hh:hh + 1, :]
                    gf = jnp.exp(jnp.where(ci <= ri, segf, NEG)) * dt[hh:hh + 1, :]
                    segb = colsf[:, h + hh:h + hh + 1] - rowsf[h + hh:h + hh + 1, :]
                    gb = jnp.exp(jnp.where(ci >= ri, segb, NEG)) * dt[h + hh:h + hh + 1, :]
                    lhs_parts.append((cbm * (gf + gb)).astype(BF16))
                for hh in (h0, h0 + 1):
                    lhs_parts.append((cm * ecol[:, hh:hh + 1]).astype(BF16))
                for hh in (h0, h0 + 1):
                    lhs_parts.append((cm * ecol[:, h + hh:h + hh + 1]).astype(BF16))
                lhs = jnp.concatenate(lhs_parts, axis=1)
                xp = xb[:, c0:c0 + LANES]
                sf = hf[:, c0:c0 + LANES]
                sb = hbs[gc, :, c0:c0 + LANES].astype(F32)
                rhs = jnp.concatenate(
                    [jnp.where(lo_mask, xp, 0.0), jnp.where(lo_mask, 0.0, xp),
                     jnp.where(lo_mask, sf, 0.0), jnp.where(lo_mask, 0.0, sf),
                     jnp.where(lo_mask, sb, 0.0), jnp.where(lo_mask, 0.0, sb)], axis=0).astype(BF16)
                ypair = jnp.dot(lhs, rhs, preferred_element_type=F32)
                ypair = ypair + dskip_ref[:, c0:c0 + LANES] * xp
                zz = z_ref[0, pl.ds(r0, CHUNK), c0:c0 + LANES]
                u_ref[:, _Y_OFF + c0:_Y_OFF + c0 + LANES] = ypair * _silu(zz)
        last = pre[:, LANES - 1:LANES]
        w_rows = jnp.exp(last - pre) * dt
        state_update(hf, xb, w_rows[0:h, :], last[0:h, :])
        if need_y:
            yv = u_ref[:, _Y_OFF:_Y_OFF + SSD_INNER]
            ms = jnp.mean(yv * yv, axis=-1, keepdims=True)
            y_ref[0, pl.ds(r0, CHUNK), :] = (yv * lax.rsqrt(ms + EPS) * gain_ref[...]).astype(y_ref.dtype)
        return carry

    @pl.when(sw == 0)
    def _():
        lax.fori_loop(0, cps, backward_chunk, 0)

    @pl.when(sw == 1)
    def _():
        lax.fori_loop(0, cps, forward_chunk, 0)

    @pl.when((sw == 0) & (s == nsteps - 1))
    def _():
        hb_out[0] = hb[...]

    @pl.when((sw == 1) & (s == nsteps - 1))
    def _():
        hf_out[0] = hf[...]


def _ssd(xbc, dts, z, consts, h0f, h0b, *, rows, need_y):
    bt, seq, _ = xbc.shape
    nsteps = seq // rows
    cps = rows // CHUNK
    hb8 = rows // SUBLANES
    nb8 = seq // SUBLANES
    cw, cb, dskip, gain, expm = consts
    dt_spec = pl.BlockSpec((cps, 2 * SSD_HEADS, CHUNK), lambda b, sw, s: (b * nsteps + blk(b, sw, s), 0, 0))

    def blk(b, sw, s):
        return jnp.where(sw == 0, nsteps - 1 - s, s)

    def full(a):
        return pl.BlockSpec(a.shape, lambda b, sw, s: (0,) * a.ndim)

    in_specs = [
        pl.BlockSpec((1, rows, CONV_DIM), lambda b, sw, s: (b, blk(b, sw, s), 0)),
        pl.BlockSpec((1, SUBLANES, CONV_DIM), lambda b, sw, s: (b, jnp.maximum(blk(b, sw, s) * hb8 - 1, 0), 0)),
        pl.BlockSpec((1, SUBLANES, CONV_DIM),
                     lambda b, sw, s: (b, jnp.minimum((blk(b, sw, s) + 1) * hb8, nb8 - 1), 0)),
        dt_spec, dt_spec, dt_spec,
        pl.BlockSpec((1, rows, SSD_INNER), lambda b, sw, s: (b, jnp.where(sw == 0, 0, s), 0)),
        full(cw), full(cb), full(dskip), full(gain), full(expm),
        pl.BlockSpec((1, SSD_STATE, SSD_INNER), lambda b, sw, s: (b, 0, 0)),
        pl.BlockSpec((1, SSD_STATE, SSD_INNER), lambda b, sw, s: (b, 0, 0)),
    ]
    st_spec = pl.BlockSpec((1, SSD_STATE, SSD_INNER), lambda b, sw, s: (b, 0, 0))
    st_shape = jax.ShapeDtypeStruct((bt, SSD_STATE, SSD_INNER), F32)
    out_specs, out_shapes = [st_spec, st_spec], [st_shape, st_shape]
    if need_y:
        out_specs = [pl.BlockSpec((1, rows, SSD_INNER), lambda b, sw, s: (b, jnp.where(sw == 0, 0, s), 0))] + out_specs
        out_shapes = [jax.ShapeDtypeStruct((bt, seq, SSD_INNER), BF16)] + out_shapes
    scratch = [
        pltpu.VMEM((CHUNK + 2 * SUBLANES, CONV_DIM), F32),
        pltpu.VMEM((CHUNK, _Y_OFF + SSD_INNER), F32),
        pltpu.VMEM((SSD_STATE, SSD_INNER), F32),
        pltpu.VMEM((SSD_STATE, SSD_INNER), F32),
        pltpu.VMEM((seq // CHUNK, SSD_STATE, SSD_INNER), BF16),
        pltpu.VMEM((seq // CHUNK, CHUNK, _XB_COLS), F32),
    ]
    return pl.pallas_call(
        functools.partial(_ssd_kernel, need_y=need_y, nsteps=nsteps, cps=cps),
        grid=(bt, 2, nsteps),
        in_specs=in_specs,
        out_specs=out_specs,
        out_shape=out_shapes,
        scratch_shapes=scratch,
        compiler_params=pltpu.CompilerParams(dimension_semantics=("arbitrary", "arbitrary", "arbitrary")),
        name="ssd_y" if need_y else "ssd_ctx",
    )(xbc, xbc, xbc, *dts, z, cw, cb, dskip, gain, expm, h0f, h0b)


def _attn_kernel(sink_ref, q_ref, kp_ref, kc_ref, kn_ref, vp_ref, vc_ref, vn_ref, kx_ref, vx_ref, o_ref, *, nsteps):
    m = pl.program_id(1)
    lane = lax.broadcasted_iota(jnp.int32, (1, LANES), 1)
    lo = lane < HEAD_DIM
    rq = Q_PER_KV * WINDOW
    qi = lax.broadcasted_iota(jnp.int32, (rq, WINDOW), 0) & (WINDOW - 1)
    kj = lax.broadcasted_iota(jnp.int32, (rq, WINDOW), 1)
    rblk = lax.shift_right_logical(lax.broadcasted_iota(jnp.int32, (rq, 1), 0), int(math.log2(WINDOW)))
    zero = jnp.zeros((), BF16)

    kpieces = [kp_ref[0], kc_ref[0, 0:WINDOW], kc_ref[0, WINDOW:2 * WINDOW], kn_ref[0]]
    vpieces = [vp_ref[0], vc_ref[0, 0:WINDOW], vc_ref[0, WINDOW:2 * WINDOW], vn_ref[0]]
    kx = kx_ref[0]
    vx = vx_ref[0]
    nt = (((1,), (1,)), ((), ()))

    for sb in range(ROWS_ATT // WINDOW):
        qsb = q_ref[0, sb * WINDOW:(sb + 1) * WINDOW, :]
        prev_ok = jnp.logical_or(m > 0, sb > 0)
        next_ok = jnp.logical_or(m < nsteps - 1, sb < ROWS_ATT // WINDOW - 1)
        outs = [None] * Q_PER_KV
        for kv in range(ATTN_KV_HEADS):
            sel = lo if kv == 0 else jnp.logical_not(lo)
            qs = jnp.concatenate(
                [jnp.where(sel, qsb[:, j * LANES:(j + 1) * LANES], zero) for j in range(Q_PER_KV)], axis=0)
            s_prev = lax.dot_general(qs, kpieces[sb], nt, preferred_element_type=F32)
            s_cur = lax.dot_general(qs, kpieces[sb + 1], nt, preferred_element_type=F32)
            s_next = lax.dot_general(qs, kpieces[sb + 2], nt, preferred_element_type=F32)
            s_ctx = lax.dot_general(qs, kx, nt, preferred_element_type=F32)
            s_prev = jnp.where((kj >= qi) & prev_ok, s_prev, NEG)
            s_next = jnp.where((kj <= qi) & next_ok, s_next, NEG)
            sink = jnp.zeros((rq, 1), F32)
            for j in range(Q_PER_KV):
                sink = jnp.where(rblk == j, sink_ref[kv * Q_PER_KV + j], sink)
            n_ct = s_ctx.shape[1] // LANES
            tile_max = jnp.maximum(jnp.maximum(s_prev, s_cur), s_next)
            for t in range(n_ct):
                tile_max = jnp.maximum(tile_max, s_ctx[:, t * LANES:(t + 1) * LANES])
            mx = jnp.maximum(jnp.max(tile_max, axis=-1, keepdims=True), sink)
            p_prev = jnp.exp(s_prev - mx)
            p_cur = jnp.exp(s_cur - mx)
            p_next = jnp.exp(s_next - mx)
            p_ctx = jnp.exp(s_ctx - mx)
            tile_sum = p_prev + p_cur + p_next
            for t in range(n_ct):
                tile_sum = tile_sum + p_ctx[:, t * LANES:(t + 1) * LANES]
            den = jnp.sum(tile_sum, axis=-1, keepdims=True) + jnp.exp(sink - mx)
            acc = jnp.dot(p_prev.astype(BF16), jnp.where(sel, vpieces[sb], zero), preferred_element_type=F32)
            acc += jnp.dot(p_cur.astype(BF16), jnp.where(sel, vpieces[sb + 1], zero), preferred_element_type=F32)
            acc += jnp.dot(p_next.astype(BF16), jnp.where(sel, vpieces[sb + 2], zero), preferred_element_type=F32)
            acc += jnp.dot(p_ctx.astype(BF16), jnp.where(sel, vx, zero), preferred_element_type=F32)
            acc = acc * (1.0 / den)
            for j in range(Q_PER_KV):
                part = acc[j * WINDOW:(j + 1) * WINDOW, :]
                outs[j] = part if outs[j] is None else outs[j] + part
        for j in range(Q_PER_KV):
            o_ref[0, sb * WINDOW:(sb + 1) * WINDOW, j * LANES:(j + 1) * LANES] = outs[j].astype(o_ref.dtype)


def _attention(sinks, q, k, v, kx, vx):
    bt, seq, _ = q.shape
    nsteps = seq // ROWS_ATT
    per = ROWS_ATT // WINDOW
    nb = seq // WINDOW
    n_ctx = kx.shape[1]

    def prev(b, m, s):
        return (b, jnp.maximum(m * per - 1, 0), 0)

    def cur(b, m, s):
        return (b, m, 0)

    def nxt(b, m, s):
        return (b, jnp.minimum((m + 1) * per, nb - 1), 0)

    small = (1, WINDOW, KV_WIDTH)
    big = (1, ROWS_ATT, KV_WIDTH)
    grid_spec = pltpu.PrefetchScalarGridSpec(
        num_scalar_prefetch=1,
        grid=(bt, nsteps),
        in_specs=[
            pl.BlockSpec((1, ROWS_ATT, ATTN_WIDTH), cur),
            pl.BlockSpec(small, prev), pl.BlockSpec(big, cur), pl.BlockSpec(small, nxt),
            pl.BlockSpec(small, prev), pl.BlockSpec(big, cur), pl.BlockSpec(small, nxt),
            pl.BlockSpec((1, n_ctx, KV_WIDTH), lambda b, m, s: (b, 0, 0)),
            pl.BlockSpec((1, n_ctx, KV_WIDTH), lambda b, m, s: (b, 0, 0)),
        ],
        out_specs=pl.BlockSpec((1, ROWS_ATT, ATTN_WIDTH), cur),
    )
    return pl.pallas_call(
        functools.partial(_attn_kernel, nsteps=nsteps),
        grid_spec=grid_spec,
        out_shape=jax.ShapeDtypeStruct((bt, seq, ATTN_WIDTH), BF16),
        name="attn",
    )(sinks, q, k, k, k, v, v, v, kx, vx)


def _outproj_kernel(ssd_ref, att_ref, x_ref, g1_ref, sh_ref, sc_ref, n2_ref, wa_ref, wb_ref, wr_ref, br_ref,
                    tri_ref, x1_ref, h2_ref, ids_ref, wcol_ref, cnt_ref, carry, *, nsteps):
    i = pl.program_id(0)
    rows = x_ref.shape[0]

    @pl.when(i == 0)
    def _():
        carry[...] = jnp.zeros_like(carry)

    acc = jnp.dot(ssd_ref[...], wa_ref[...], preferred_element_type=F32)
    acc += jnp.dot(att_ref[...], wb_ref[...], preferred_element_type=F32)
    x1 = x_ref[...] + g1_ref[0] * acc
    x1_ref[...] = x1
    ms = jnp.mean(x1 * x1, axis=-1, keepdims=True)
    h2 = (x1 * lax.rsqrt(ms + EPS)) * (n2_ref[...] * (1.0 + sc_ref[0])) + sh_ref[0]
    h2b = h2.astype(BF16)
    half = h2.shape[1] // 2
    lo_bits = lax.shift_right_logical(pltpu.bitcast(h2b[:, :half].astype(F32), jnp.uint32), jnp.uint32(16))
    hi_bits = pltpu.bitcast(h2b[:, half:].astype(F32), jnp.uint32) & jnp.uint32(0xFFFF0000)
    h2_ref[...] = hi_bits | lo_bits
    lt = lax.dot_general(wr_ref[...], h2b, (((1,), (1,)), ((), ())),
                         preferred_element_type=F32) + br_ref[...]
    row = lax.broadcasted_iota(jnp.int32, lt.shape, 0).astype(F32)
    big_i = float(ROUTE_ROWS)
    gl = jnp.where(row < N_GROUPS, lt, NEG)
    gmax = jnp.max(gl, axis=0, keepdims=True)
    gidx = jnp.min(jnp.where(gl == gmax, row, big_i), axis=0, keepdims=True)
    g_w = 1.0 / jnp.sum(jnp.exp(gl - gmax), axis=0, keepdims=True)
    lo = N_GROUPS + EXPERTS_PER_GROUP * gidx
    el = jnp.where((row >= lo) & (row < lo + EXPERTS_PER_GROUP), lt, NEG)
    m1 = jnp.max(el, axis=0, keepdims=True)
    i1 = jnp.min(jnp.where(el == m1, row, big_i), axis=0, keepdims=True)
    el2 = jnp.where(row == i1, NEG, el)
    m2 = jnp.max(el2, axis=0, keepdims=True)
    i2 = jnp.min(jnp.where(el2 == m2, row, big_i), axis=0, keepdims=True)
    r = jnp.exp(m2 - m1)
    w1 = g_w / (1.0 + r)
    w2 = g_w * r / (1.0 + r)
    oh1 = row == i1
    oh2 = row == i2
    cnt = jnp.where(oh1 | oh2, 1.0, 0.0)
    prefix = jnp.dot(cnt.astype(BF16), tri_ref[...], preferred_element_type=F32)
    base = carry[...] + prefix
    rank1 = jnp.sum(jnp.where(oh1, base, 0.0), axis=0, keepdims=True)
    rank2 = jnp.sum(jnp.where(oh2, base, 0.0), axis=0, keepdims=True)
    carry[...] = carry[...] + jnp.sum(cnt, axis=1, keepdims=True)
    r8 = lax.broadcasted_iota(jnp.int32, (SUBLANES, rows), 0)
    code1 = (i1 - N_GROUPS) * float(1 << RANK_BITS) + rank1
    code2 = (i2 - N_GROUPS) * float(1 << RANK_BITS) + rank2
    ids_ref[...] = jnp.where(r8 == 0, code1, jnp.where(r8 == 1, code2, 0.0)).astype(jnp.int32)
    r128 = lax.broadcasted_iota(jnp.int32, (LANES, rows), 0)
    wfull = jnp.where(r128 == 0, w1, jnp.where(r128 == 1, w2, 0.0))
    for j in range(rows // LANES):
        wcol_ref[j * LANES:(j + 1) * LANES, :] = wfull[:, j * LANES:(j + 1) * LANES].T

    @pl.when(i == nsteps - 1)
    def _():
        cnt_ref[...] = carry[:, 0:LANES]


def _outproj(ssd, att, xf, mod3, norm2, wa, wb, wr, br, tri, *, rows, seq_blocks):
    t, d = xf.shape
    nsteps = t // rows

    def mod(col):
        return pl.BlockSpec((1, 1, d), lambda i: (i // seq_blocks, 0, col))

    def full(a):
        return pl.BlockSpec(a.shape, lambda i: (0,) * a.ndim)

    return pl.pallas_call(
        functools.partial(_outproj_kernel, nsteps=nsteps),
        grid=(nsteps,),
        in_specs=[pl.BlockSpec((rows, SSD_INNER), lambda i: (i, 0)),
                  pl.BlockSpec((rows, ATTN_WIDTH), lambda i: (i, 0)),
                  pl.BlockSpec((rows, d), lambda i: (i, 0)),
                  mod(2), mod(3), mod(4), full(norm2), full(wa), full(wb), full(wr), full(br), full(tri)],
        out_specs=[pl.BlockSpec((rows, d), lambda i: (i, 0)),
                   pl.BlockSpec((rows, d // 2), lambda i: (i, 0)),
                   pl.BlockSpec((SUBLANES, rows), lambda i: (0, i)),
                   pl.BlockSpec((rows, LANES), lambda i: (i, 0)),
                   pl.BlockSpec((ROUTE_ROWS, LANES), lambda i: (0, 0))],
        out_shape=[jax.ShapeDtypeStruct((t, d), F32),
                   jax.ShapeDtypeStruct((t, d // 2), jnp.uint32),
                   jax.ShapeDtypeStruct((SUBLANES, t), jnp.int32),
                   jax.ShapeDtypeStruct((t, LANES), F32),
                   jax.ShapeDtypeStruct((ROUTE_ROWS, LANES), F32)],
        scratch_shapes=[pltpu.VMEM((ROUTE_ROWS, rows), F32)],
        compiler_params=pltpu.CompilerParams(dimension_semantics=("arbitrary",)),
        name="outproj_router",
    )(ssd, att, xf, mod3, mod3, mod3, norm2, wa, wb, wr, br, tri)


def _expert_kernel(blk_e_ref, nused_ref, pstart_ref, counts_ref, dst_ref, h2_hbm, wg_ref, wu_ref, wd_ref, y_ref,
                   h2v, xg, wgb, wub, wdb, tok_ref, sem, *, n_tok, nblocks):
    b = pl.program_id(0)
    nused = nused_ref[0]
    slot = b % 2
    half = h2v.shape[1]

    def h2_copy():
        return pltpu.make_async_copy(h2_hbm, h2v, sem.at[0])

    def gather_rows(blk, sl):
        for r in range(MOE_BLK):
            xg[sl, pl.ds(r, 1), :] = h2v[pl.ds(tok_ref[blk * MOE_BLK + r], 1), :]

    @pl.when(b == 0)
    def _():
        h2_copy().start()
        def pad_expert(e, carry):
            lo = pstart_ref[e] + counts_ref[e]
            hi = pstart_ref[e] + (counts_ref[e] + MOE_BLK - 1) // MOE_BLK * MOE_BLK

            def pad_slot(s, c):
                tok_ref[s] = 0
                return c
            return lax.fori_loop(lo, hi, pad_slot, carry)
        lax.fori_loop(0, N_EXPERTS, pad_expert, 0)

        def claim(i, carry):
            tok_ref[dst_ref[i]] = jnp.where(i >= n_tok, i - n_tok, i)
            return carry
        lax.fori_loop(0, 2 * n_tok, claim, 0, unroll=16)
        h2_copy().wait()
        gather_rows(0, 0)

    e = blk_e_ref[b]
    e_prev = blk_e_ref[jnp.maximum(b - 1, 0)]

    @pl.when((b == 0) | (e != e_prev))
    def _():
        wgb[...] = wg_ref[0].astype(BF16)
        wub[...] = wu_ref[0].astype(BF16)
        wdb[...] = wd_ref[0].astype(BF16)

    @pl.when(b < nused)
    def _():
        words = xg[slot]
        x_lo = pltpu.bitcast(words << jnp.uint32(16), F32).astype(BF16)
        x_hi = pltpu.bitcast(words & jnp.uint32(0xFFFF0000), F32).astype(BF16)
        gate = (jnp.dot(x_lo, wgb[0:half, :], preferred_element_type=F32)
                + jnp.dot(x_hi, wgb[half:, :], preferred_element_type=F32))
        up = (jnp.dot(x_lo, wub[0:half, :], preferred_element_type=F32)
              + jnp.dot(x_hi, wub[half:, :], preferred_element_type=F32))
        hmid = (_silu(gate) * up).astype(BF16)
        y = jnp.dot(hmid, wdb[...], preferred_element_type=F32)
        for s in range(D_TILES):
            y_ref[pl.ds(s, MOE_BLK, stride=D_TILES), :] = y[:, s * LANES:(s + 1) * LANES]
        gather_rows(jnp.minimum(b + 1, nused - 1), 1 - slot)

    @pl.when(b >= nused)
    def _():
        y_ref[...] = jnp.zeros_like(y_ref)


def _experts(blk_e, nused, pstart, counts, dst, h2p, w_gate, w_up, w_down, *, nblocks, n_tok):
    d = w_gate.shape[1]

    def wspec(shape):
        return pl.BlockSpec(shape, lambda b, be, nu, ps, ct, cd: (be[b], 0, 0))

    grid_spec = pltpu.PrefetchScalarGridSpec(
        num_scalar_prefetch=5,
        grid=(nblocks,),
        in_specs=[pl.BlockSpec(memory_space=pl.ANY),
                  wspec((1, d, EXPERT_DIM)), wspec((1, d, EXPERT_DIM)), wspec((1, EXPERT_DIM, d))],
        out_specs=pl.BlockSpec((MOE_BLK * D_TILES, LANES), lambda b, be, nu, ps, ct, cd: (b, 0)),
        scratch_shapes=[
            pltpu.VMEM(h2p.shape, jnp.uint32),
            pltpu.VMEM((2, MOE_BLK, d // 2), jnp.uint32),
            pltpu.VMEM((d, EXPERT_DIM), BF16),
            pltpu.VMEM((d, EXPERT_DIM), BF16),
            pltpu.VMEM((EXPERT_DIM, d), BF16),
            pltpu.SMEM((nblocks * MOE_BLK,), jnp.int32),
            pltpu.SemaphoreType.DMA((1,)),
        ],
    )
    return pl.pallas_call(
        functools.partial(_expert_kernel, n_tok=n_tok, nblocks=nblocks),
        grid_spec=grid_spec,
        out_shape=jax.ShapeDtypeStruct((nblocks * MOE_BLK * D_TILES, LANES), F32),
        compiler_params=pltpu.CompilerParams(dimension_semantics=("arbitrary",)),
        name="experts",
    )(blk_e, nused, pstart, counts, dst, h2p, w_gate, w_up, w_down)


def _combine_kernel(dst_ref, y_hbm, x1_ref, wcol_ref, g2_ref, nf_ref, o_ref, ybuf, sem, *, nsteps, n_tok):
    i = pl.program_id(0)
    rows = x1_ref.shape[0]
    slot = i % 2

    def row_copy(src, r, k, sl):
        return pltpu.make_async_copy(y_hbm.at[pl.ds(src * D_TILES, D_TILES)],
                                     ybuf.at[sl, k, pl.ds(r * D_TILES, D_TILES)], sem.at[sl])

    def start_rows(step, sl, r0, r1):
        for r in range(r0, r1):
            for k in range(2):
                row_copy(dst_ref[k * n_tok + step * rows + r], r, k, sl).start()

    def wait_rows(sl):
        for r in range(rows):
            for k in range(2):
                row_copy(0, r, k, sl).wait()

    @pl.when(i == 0)
    def _():
        start_rows(0, 0, 0, rows)

    @pl.when(i + 1 < nsteps)
    def _():
        start_rows(i + 1, 1 - slot, 0, rows)

    wait_rows(slot)
    w0 = wcol_ref[:, 0:1]
    w1 = wcol_ref[:, 1:2]
    ssq = jnp.zeros((rows, 1), F32)
    for s in range(D_TILES):
        cols = slice(s * LANES, (s + 1) * LANES)
        moe = (ybuf[slot, 0, pl.ds(s, rows, stride=D_TILES), :] * w0
               + ybuf[slot, 1, pl.ds(s, rows, stride=D_TILES), :] * w1)
        x2 = x1_ref[:, cols] + g2_ref[0, :, cols] * moe
        ssq = ssq + jnp.sum(x2 * x2, axis=-1, keepdims=True)
        o_ref[:, cols] = x2
    inv = lax.rsqrt(ssq * (1.0 / (D_TILES * LANES)) + EPS)
    o_ref[...] = o_ref[...] * inv * nf_ref[...]


def _combine(dst, y_buf, x1, wcol, mod3, norm_final, *, rows, seq_blocks):
    t, d = x1.shape
    nsteps = t // rows
    grid_spec = pltpu.PrefetchScalarGridSpec(
        num_scalar_prefetch=1,
        grid=(nsteps,),
        in_specs=[
            pl.BlockSpec(memory_space=pl.ANY),
            pl.BlockSpec((rows, d), lambda i, ds: (i, 0)),
            pl.BlockSpec((rows, LANES), lambda i, ds: (i, 0)),
            pl.BlockSpec((1, 1, d), lambda i, ds: (i // seq_blocks, 0, 5)),
            pl.BlockSpec((1, d), lambda i, ds: (0, 0)),
        ],
        out_specs=pl.BlockSpec((rows, d), lambda i, ds: (i, 0)),
        scratch_shapes=[pltpu.VMEM((2, 2, rows * D_TILES, LANES), F32), pltpu.SemaphoreType.DMA((2,))],
    )
    return pl.pallas_call(
        functools.partial(_combine_kernel, nsteps=nsteps, n_tok=t),
        grid_spec=grid_spec,
        out_shape=jax.ShapeDtypeStruct((t, d), F32),
        compiler_params=pltpu.CompilerParams(dimension_semantics=("arbitrary",)),
        name="combine",
    )(dst, y_buf, x1, wcol, mod3, norm_final)


def _rope_tables(seq):
    pos = np.arange(seq)
    n_freq = HEAD_DIM // 4
    inv = ROPE_BASE ** (-np.arange(n_freq, dtype=np.float32) / n_freq)
    ang = np.concatenate([(pos // GRID_W)[:, None] * inv, (pos % GRID_W)[:, None] * inv], axis=-1)
    ang = np.concatenate([ang, ang, ang, ang], axis=-1).astype(np.float32)
    return jnp.asarray(np.cos(ang), F32), jnp.asarray(np.sin(ang), F32)


def _head_expand_mat():
    m = np.zeros((SSD_HEADS, SSD_INNER), np.float32)
    for hh in range(SSD_HEADS):
        m[hh, hh * SSD_HEAD_DIM:(hh + 1) * SSD_HEAD_DIM] = 1.0
    return jnp.asarray(m)


def kernel(x, c, ctx, c_ctx, w_ada, b_ada, norm1, w_in, conv_w, conv_b, dt_bias, a_log, d_skip, ssd_norm,
           attn_sinks, w_out, norm2, w_group, b_group, w_expert, b_expert, w_gate, w_up, w_down, norm_final):
    return _pipeline(x, c, ctx, c_ctx, w_ada, b_ada, norm1, w_in, conv_w, conv_b, dt_bias, a_log, d_skip,
                     ssd_norm, attn_sinks, w_out, norm2, w_group, b_group, w_expert, b_expert, w_gate, w_up,
                     w_down, norm_final)["out"]


def _pipeline(x, c, ctx, c_ctx, w_ada, b_ada, norm1, w_in, conv_w, conv_b, dt_bias, a_log, d_skip, ssd_norm,
              attn_sinks, w_out, norm2, w_group, b_group, w_expert, b_expert, w_gate, w_up, w_down, norm_final):
    bt, seq, d = x.shape
    n_ctx = ctx.shape[1]
    t = bt * seq
    layer = 0
    assert w_ada.shape[0] == 1 and seq % ROWS_IN == 0 and n_ctx % CHUNK == 0

    cc = jnp.zeros((SUBLANES, d), F32).at[:bt].set(c).at[bt].set(c_ctx)
    mod = _ada(cc, w_ada[layer], b_ada[layer][None, :])
    mod3 = mod.reshape(SUBLANES, 1, 6 * d)

    w = w_in[layer]
    o_z, o_xbc, o_dt = 0, SSD_INNER, SSD_INNER + CONV_DIM
    o_q = o_dt + 2 * SSD_HEADS
    o_k, o_v = o_q + ATTN_WIDTH, o_q + ATTN_WIDTH + KV_WIDTH
    q_perm = np.concatenate([np.r_[j * HEAD_DIM:(j + 1) * HEAD_DIM, (j + 4) * HEAD_DIM:(j + 5) * HEAD_DIM]
                             for j in range(Q_PER_KV)])
    w_q = w[:, o_q:o_q + ATTN_WIDTH].reshape(d, ATTN_KV_HEADS, Q_PER_KV, HEAD_DIM).transpose(0, 2, 1, 3)
    w_q = w_q.reshape(d, ATTN_WIDTH)
    w_cat = jnp.concatenate([w[:, o_z:o_z + SSD_INNER], w[:, o_xbc:o_xbc + CONV_DIM], w_q,
                             w[:, o_k:o_k + KV_WIDTH], w[:, o_v:o_v + KV_WIDTH]], axis=1).astype(BF16)
    w_dtT = w[:, o_dt:o_dt + 2 * SSD_HEADS].T.astype(BF16)

    seq_blocks = seq // ROWS_IN
    cos, sin = _rope_tables(seq)
    bias_rows = jnp.broadcast_to(dt_bias[layer].reshape(2 * SSD_HEADS, 1), (2 * SSD_HEADS, LANES))
    alog_rows = jnp.broadcast_to(a_log[layer].reshape(2 * SSD_HEADS, 1), (2 * SSD_HEADS, LANES))
    segs_x = [("z", _COL_Z, SSD_INNER), ("xbc", _COL_XBC, CONV_DIM), ("dtT", 0, 0),
              ("q", _COL_Q, ATTN_WIDTH), ("k", _COL_K, KV_WIDTH), ("v", _COL_V, KV_WIDTH)]
    z_x, xbc_x, dt_x, pre_x, suf_x, q_x, k_x, v_x = _inproj(
        x.reshape(t, d), mod3, norm1[layer][None, :], w_cat, w_dtT, bias_rows, alog_rows, (cos, sin),
        rows=ROWS_IN, mod_row_fn=lambda i: i // seq_blocks, segs=segs_x, seq_blocks=seq_blocks)
    segs_c = [("xbc", _COL_XBC, CONV_DIM), ("dtT", 0, 0), ("k", _COL_K, KV_WIDTH), ("v", _COL_V, KV_WIDTH)]
    xbc_c, dt_c, pre_c, suf_c, k_c, v_c = _inproj(
        ctx.reshape(bt * n_ctx, d), mod3, norm1[layer][None, :], w_cat, w_dtT, bias_rows, alog_rows, None,
        rows=n_ctx, mod_row_fn=lambda i: bt, segs=segs_c, seq_blocks=1)

    cw = jnp.zeros((SUBLANES, CONV_DIM), F32).at[:CONV_WIDTH].set(conv_w[layer])
    cb = conv_b[layer][None, :]
    dskip_e = jnp.repeat(d_skip[layer], SSD_HEAD_DIM)[None, :]
    consts = (cw, cb, dskip_e, ssd_norm[layer][None, :], _head_expand_mat())

    zeros_state = jnp.zeros((bt, SSD_STATE, SSD_INNER), F32)
    z_dummy = jnp.zeros((bt, n_ctx, SSD_INNER), F32)
    h_cf, h_cb = _ssd(xbc_c.reshape(bt, n_ctx, CONV_DIM), (dt_c, pre_c, suf_c), z_dummy, consts,
                      zeros_state, zeros_state, rows=n_ctx, need_y=False)
    ssd_x, _, _ = _ssd(xbc_x.reshape(bt, seq, CONV_DIM), (dt_x, pre_x, suf_x), z_x.reshape(bt, seq, SSD_INNER),
                       consts, h_cf, h_cb, rows=ROWS_SSD, need_y=True)

    attn_x = _attention(attn_sinks[layer].astype(F32),
                        q_x.reshape(bt, seq, ATTN_WIDTH), k_x.reshape(bt, seq, KV_WIDTH),
                        v_x.reshape(bt, seq, KV_WIDTH), k_c.reshape(bt, n_ctx, KV_WIDTH),
                        v_c.reshape(bt, n_ctx, KV_WIDTH))

    wo = w_out[layer]
    wa = wo[:SSD_INNER].astype(BF16)
    wb = wo[SSD_INNER:].reshape(ATTN_KV_HEADS, Q_PER_KV, HEAD_DIM, d).transpose(1, 0, 2, 3)
    wb = wb.reshape(ATTN_WIDTH, d).astype(BF16)
    wr = jnp.zeros((ROUTE_ROWS, d), F32).at[:N_GROUPS].set(w_group[layer].T)
    wr = wr.at[N_GROUPS:N_GROUPS + N_EXPERTS].set(w_expert[layer].T).astype(BF16)
    br = jnp.zeros((ROUTE_ROWS,), F32).at[:N_GROUPS].set(b_group[layer])
    br = br.at[N_GROUPS:N_GROUPS + N_EXPERTS].set(b_expert[layer])
    br = jnp.broadcast_to(br[:, None], (ROUTE_ROWS, ROWS_IN))
    tri = jnp.asarray(np.triu(np.ones((ROWS_IN, ROWS_IN), np.float32), 1), BF16)
    x1, h2, ids, wcol, cnt = _outproj(
        ssd_x.reshape(t, SSD_INNER), attn_x.reshape(t, ATTN_WIDTH), x.reshape(t, d), mod3, norm2[layer][None, :],
        wa, wb, wr, br, tri, rows=ROWS_IN, seq_blocks=seq_blocks)

    counts = cnt[N_GROUPS:N_GROUPS + N_EXPERTS, 0].astype(jnp.int32)
    padded = (counts + MOE_BLK - 1) // MOE_BLK * MOE_BLK
    ends = jnp.cumsum(padded)
    pstart = ends - padded
    nblocks = (2 * t) // MOE_BLK + N_EXPERTS
    blk_start = jnp.arange(nblocks, dtype=jnp.int32) * MOE_BLK
    blk_e = jnp.minimum(jnp.sum((ends[None, :] <= blk_start[:, None]).astype(jnp.int32), axis=1), N_EXPERTS - 1)
    nused = (ends[-1] // MOE_BLK).astype(jnp.int32).reshape(1)
    code = ids[0:2].reshape(-1)
    e_id = lax.shift_right_logical(code, RANK_BITS)
    onehot = e_id[:, None] == jnp.arange(N_EXPERTS, dtype=jnp.int32)[None, :]
    dst = jnp.sum(jnp.where(onehot, pstart[None, :], 0), axis=1) + (code & ((1 << RANK_BITS) - 1))

    y_buf = _experts(blk_e, nused, pstart, counts, dst, h2, w_gate[layer], w_up[layer], w_down[layer],
                     nblocks=nblocks, n_tok=t)
    out = _combine(dst, y_buf, x1, wcol, mod3, norm_final[None, :], rows=ROWS_CMB, seq_blocks=seq // ROWS_CMB)
    return dict(out=out.reshape(bt, seq, d), mod=mod, z=z_x, xbc=xbc_x, dt=dt_x, q=q_x, k=k_x, v=v_x,
                k_c=k_c, v_c=v_c, h_cf=h_cf, h_cb=h_cb, ssd=ssd_x, attn=attn_x, x1=x1, h2=h2, ids=ids,
                wcol=wcol, cnt=cnt, y_buf=y_buf, pstart=pstart, q_perm=q_perm)
```

```python
import functools
import math

import numpy as np
import jax
import jax.numpy as jnp
from jax import lax
from jax.experimental import pallas as pl
from jax.experimental.pallas import tpu as pltpu

F32 = jnp.float32
BF16 = jnp.bfloat16

D_MODEL = 1024
GRID_W = 64
EPS = 1e-6
SSD_INNER = 512
SSD_HEAD_DIM = 64
SSD_HEADS = 8
SSD_GROUPS = 2
SSD_STATE = 128
CONV_WIDTH = 5
CONV_DIM = SSD_INNER + 2 * SSD_GROUPS * SSD_STATE
CHUNK = 128
ATTN_WIDTH = 512
HEAD_DIM = 64
ATTN_Q_HEADS = 8
ATTN_KV_HEADS = 2
Q_PER_KV = 4
KV_WIDTH = 128
WINDOW = 128
ROPE_BASE = 10000.0
N_GROUPS = 4
EXPERTS_PER_GROUP = 8
N_EXPERTS = 32
EXPERT_DIM = 512

LANES = 128
SUBLANES = 8
NEG = -1e30

ROWS_IN = 512
ROWS_SSD = 512
ROWS_ATT = 256
ROUTE_ROWS = 48
MOE_BLK = 256
GATHER_SLOTS = 3
RANK_BITS = 16
D_TILES = D_MODEL // LANES
ROWS_CMB = 128


def _silu(v):
    half = 0.5 * v
    return half + half * jnp.tanh(half)


def _softplus(v):
    return jnp.maximum(v, 0.0) + jnp.log(1.0 + jnp.exp(-jnp.abs(v)))


def _ada_kernel(c_ref, w_ref, b_ref, o_ref):
    a = _silu(c_ref[...])
    o_ref[...] = jnp.dot(a.astype(BF16), w_ref[...].astype(BF16),
                         preferred_element_type=F32) + b_ref[...]


def _ada(cc, w_ada, b_ada):
    d, n = w_ada.shape
    tn = 1536
    return pl.pallas_call(
        _ada_kernel,
        grid=(n // tn,),
        in_specs=[pl.BlockSpec((SUBLANES, d), lambda j: (0, 0)),
                  pl.BlockSpec((d, tn), lambda j: (0, j)),
                  pl.BlockSpec((1, tn), lambda j: (0, j))],
        out_specs=pl.BlockSpec((SUBLANES, tn), lambda j: (0, j)),
        out_shape=jax.ShapeDtypeStruct((SUBLANES, n), F32),
        name="ada",
    )(cc, w_ada, b_ada)


def _rope(t, cos, sin_signed, first_half):
    rot = jnp.where(first_half, pltpu.roll(t, LANES - HEAD_DIM // 2, 1), pltpu.roll(t, HEAD_DIM // 2, 1))
    return t * cos + rot * sin_signed


def _lane_cumsum(v, lane, reverse):
    k = 1
    while k < LANES:
        if reverse:
            v = v + jnp.where(lane < LANES - k, pltpu.roll(v, LANES - k, 1), 0.0)
        else:
            v = v + jnp.where(lane >= k, pltpu.roll(v, k, 1), 0.0)
        k *= 2
    return v


def _inproj_kernel(*refs, segs, rope):
    x_ref, sh_ref, sc_ref, g_ref, w_ref, wdt_ref, bias_ref, alog_ref = refs[:8]
    pos = 8
    if rope:
        cos_ref, sin_ref = refs[8:10]
        pos = 10
    outs = iter(refs[pos:])
    x = x_ref[...]
    ms = jnp.mean(x * x, axis=-1, keepdims=True)
    h = (x * lax.rsqrt(ms + EPS)) * (g_ref[...] * (1.0 + sc_ref[0])) + sh_ref[0]
    hb = h.astype(BF16)
    if rope:
        cos = cos_ref[...]
        lane = lax.broadcasted_iota(jnp.int32, cos.shape, 1)
        first_half = (lane & (HEAD_DIM - 1)) < HEAD_DIM // 2
        sin_s = jnp.where(first_half, -sin_ref[...], sin_ref[...])
    for kind, c0, width in segs:
        if kind == "dtT":
            dt_ref, pre_ref, suf_ref = next(outs), next(outs), next(outs)
            dtt = lax.dot_general(wdt_ref[...], hb, (((1,), (1,)), ((), ())), preferred_element_type=F32)
            lane16 = lax.broadcasted_iota(jnp.int32, (2 * SSD_HEADS, CHUNK), 1)
            neg_a = -jnp.exp(alog_ref[...])
            for j in range(dt_ref.shape[0]):
                dt = _softplus(dtt[:, j * CHUNK:(j + 1) * CHUNK] + bias_ref[...])
                a = dt * neg_a
                dt_ref[j] = dt
                pre_ref[j] = _lane_cumsum(a, lane16, False)
                suf_ref[j] = _lane_cumsum(a, lane16, True)
            continue
        o_ref = next(outs)
        for j0 in range(0, width, 512):
            wj = min(512, width - j0)
            acc = jnp.dot(hb, w_ref[:, c0 + j0:c0 + j0 + wj], preferred_element_type=F32)
            if rope and kind in ("q", "k"):
                scale = HEAD_DIM ** -0.5 if kind == "q" else 1.0
                for l0 in range(0, wj, LANES):
                    t = _rope(acc[:, l0:l0 + LANES], cos, sin_s, first_half)
                    o_ref[:, j0 + l0:j0 + l0 + LANES] = (t * scale).astype(o_ref.dtype)
            else:
                o_ref[:, j0:j0 + wj] = acc.astype(o_ref.dtype)


_COL_Z, _COL_XBC, _COL_Q, _COL_K, _COL_V = 0, 512, 1536, 2048, 2176
_W_COLS = 2304


def _inproj(xf, mod3, norm1, w_cat, w_dtT, bias_rows, alog_rows, rope_tabs, *, rows, mod_row_fn, segs, seq_blocks):
    t, d = xf.shape
    rope = rope_tabs is not None
    in_specs = [
        pl.BlockSpec((rows, d), lambda i: (i, 0)),
        pl.BlockSpec((1, 1, d), lambda i: (mod_row_fn(i), 0, 0)),
        pl.BlockSpec((1, 1, d), lambda i: (mod_row_fn(i), 0, 1)),
        pl.BlockSpec((1, d), lambda i: (0, 0)),
        pl.BlockSpec(w_cat.shape, lambda i: (0, 0)),
        pl.BlockSpec(w_dtT.shape, lambda i: (0, 0)),
        pl.BlockSpec(bias_rows.shape, lambda i: (0, 0)),
        pl.BlockSpec(alog_rows.shape, lambda i: (0, 0)),
    ]
    args = [xf, mod3, mod3, norm1, w_cat, w_dtT, bias_rows, alog_rows]
    if rope:
        in_specs += [pl.BlockSpec((rows, LANES), lambda i: (i % seq_blocks, 0))] * 2
        args += list(rope_tabs)
    out_specs, out_shapes = [], []
    for kind, _, width in segs:
        if kind == "dtT":
            for _ in range(3):
                out_specs.append(pl.BlockSpec((rows // CHUNK, 2 * SSD_HEADS, CHUNK), lambda i: (i, 0, 0)))
                out_shapes.append(jax.ShapeDtypeStruct((t // CHUNK, 2 * SSD_HEADS, CHUNK), F32))
        else:
            dt = BF16 if kind in ("q", "k", "v") else F32
            out_specs.append(pl.BlockSpec((rows, width), lambda i: (i, 0)))
            out_shapes.append(jax.ShapeDtypeStruct((t, width), dt))
    return pl.pallas_call(
        functools.partial(_inproj_kernel, segs=tuple(segs), rope=rope),
        grid=(t // rows,),
        in_specs=in_specs,
        out_specs=out_specs,
        out_shape=out_shapes,
        name="inproj_rope" if rope else "inproj_ctx",
    )(*args)


_XB_COLS = SSD_INNER + SSD_GROUPS * SSD_STATE
_Y_OFF = CONV_DIM - _XB_COLS


def _ssd_kernel(*refs, need_y, nsteps, cps):
    (xbc_ref, prev_ref, next_ref, dt_ref, pre_ref, suf_ref, z_ref, cw_ref, cb_ref,
     dskip_ref, gain_ref, exp_ref, h0f_ref, h0b_ref) = refs[:14]
    if need_y:
        y_ref, hf_out, hb_out = refs[14:17]
        rest = refs[17:]
    else:
        y_ref = None
        hf_out, hb_out = refs[14:16]
        rest = refs[16:]
    cwin, u_ref, hf, hb, hbs, ust = rest

    sw = pl.program_id(1)
    s = pl.program_id(2)
    bi = jnp.where(sw == 0, nsteps - 1 - s, s)
    rows = xbc_ref.shape[1]
    h = SSD_HEADS

    @pl.when((sw == 0) & (s == 0))
    def _():
        hb[...] = h0b_ref[0]

    @pl.when((sw == 1) & (s == 0))
    def _():
        hf[...] = h0f_ref[0]

    ri = lax.broadcasted_iota(jnp.int32, (CHUNK, CHUNK), 0)
    ci = lax.broadcasted_iota(jnp.int32, (CHUNK, CHUNK), 1)
    lane128 = lax.broadcasted_iota(jnp.int32, (CHUNK, LANES), 1)
    lo_mask = lane128 < SSD_HEAD_DIM

    def conv_silu(c, c0, c1, out_ref):
        r0 = pl.multiple_of(c * CHUNK, CHUNK)
        above = xbc_ref[0, pl.ds(pl.multiple_of(jnp.maximum(r0 - SUBLANES, 0), SUBLANES), SUBLANES), c0:c1]
        halo_a = jnp.where(bi > 0, prev_ref[0, :, c0:c1], 0.0)
        cwin[0:SUBLANES, c0:c1] = jnp.where(c > 0, above, halo_a)
        cwin[SUBLANES:SUBLANES + CHUNK, c0:c1] = xbc_ref[0, pl.ds(r0, CHUNK), c0:c1]
        below = xbc_ref[0, pl.ds(pl.multiple_of(jnp.minimum(r0 + CHUNK, rows - SUBLANES), SUBLANES), SUBLANES), c0:c1]
        halo_b = jnp.where(bi < nsteps - 1, next_ref[0, :, c0:c1], 0.0)
        cwin[SUBLANES + CHUNK:, c0:c1] = jnp.where(c < cps - 1, below, halo_b)
        for j0 in range(c0, c1, LANES):
            acc = jnp.broadcast_to(cb_ref[:, j0:j0 + LANES], (CHUNK, LANES))
            for k in range(CONV_WIDTH):
                off = SUBLANES - CONV_WIDTH // 2 + k
                acc = acc + cwin[off:off + CHUNK, j0:j0 + LANES] * cw_ref[k:k + 1, j0:j0 + LANES]
            out_ref[:, j0 - c0:j0 - c0 + LANES] = _silu(acc)

    def dt_rows(c):
        return dt_ref[c], pre_ref[c], suf_ref[c]

    def state_update(state_ref, xb, w_rows, tot_col):
        dec = jnp.exp(jnp.sum(tot_col * exp_ref[...], axis=0, keepdims=True))
        for g in range(SSD_GROUPS):
            bm = xb[:, SSD_INNER + g * SSD_STATE:SSD_INNER + (g + 1) * SSD_STATE]
            bt = bm.T
            for pr in range(2):
                h0 = g * 4 + pr * 2
                c0 = h0 * SSD_HEAD_DIM
                xp = xb[:, c0:c0 + LANES]
                rhs = jnp.concatenate([jnp.where(lo_mask, xp, 0.0), jnp.where(lo_mask, 0.0, xp)],
                                      axis=0).astype(BF16)
                lhs = jnp.concatenate([bt * w_rows[h0:h0 + 1, :], bt * w_rows[h0 + 1:h0 + 2, :]],
                                      axis=1).astype(BF16)
                sres = jnp.dot(lhs, rhs, preferred_element_type=F32)
                state_ref[:, c0:c0 + LANES] = state_ref[:, c0:c0 + LANES] * dec[:, c0:c0 + LANES] + sres

    def col_forms(rows16):
        padded = jnp.concatenate([rows16, jnp.zeros((CHUNK - 2 * h, LANES), F32)], axis=0)
        return padded.T

    def backward_chunk(k, carry):
        c = cps - 1 - k
        gc = bi * cps + c
        xb = ust.at[gc]
        conv_silu(c, 0, _XB_COLS, xb)
        dt, _, suf = dt_rows(c)
        hbs[gc] = hb[...].astype(BF16)
        s0 = suf[:, 0:1]
        w_rows = jnp.exp(s0 - suf) * dt
        state_update(hb, xb, w_rows[h:2 * h, :], s0[h:2 * h, :])
        return carry

    def forward_chunk(c, carry):
        gc = bi * cps + c
        r0 = pl.multiple_of(c * CHUNK, CHUNK)
        xb = ust.at[gc]
        conv_silu(c, _XB_COLS, CONV_DIM, u_ref)
        dt, pre, suf = dt_rows(c)
        rowsf = jnp.concatenate([pre[0:h, :], suf[h:2 * h, :]], axis=0)
        colsf = col_forms(rowsf)
        ecol = jnp.exp(colsf)
        for g in range(SSD_GROUPS):
            bm = xb[:, SSD_INNER + g * SSD_STATE:SSD_INNER + (g + 1) * SSD_STATE]
            cm = u_ref[:, g * SSD_STATE:(g + 1) * SSD_STATE]
            cmb = cm.astype(BF16)
            cbm = lax.dot_general(cmb, bm.astype(BF16), (((1,), (1,)), ((), ())),
                                  preferred_element_type=F32)
            for pr in range(2):
                h0 = g * 4 + pr * 2
                c0 = h0 * SSD_HEAD_DIM
                lhs_parts = []
                for hh in (h0, h0 + 1):
                    segf = colsf[:, hh:hh + 1] - rowsf[hh:hh + 1, :]
                    gf = jnp.exp(jnp.where(ci <= ri, segf, NEG)) * dt[hh:hh + 1, :]
                    segb = colsf[:, h + hh:h + hh + 1] - rowsf[h + hh:h + hh + 1, :]
                    gb = jnp.exp(jnp.where(ci >= ri, segb, NEG)) * dt[h + hh:h + hh + 1, :]
                    lhs_parts.append((cbm * (gf + gb)).astype(BF16))
                for hh in (h0, h0 + 1):
                    lhs_parts.append((cm * ecol[:, hh:hh + 1]).astype(BF16))
                for hh in (h0, h0 + 1):
                    lhs_parts.append((cm * ecol[:, h + hh:h + hh + 1]).astype(BF16))
                lhs = jnp.concatenate(lhs_parts, axis=1)
                xp = xb[:, c0:c0 + LANES]
                sf = hf[:, c0:c0 + LANES]
                sb = hbs[gc, :, c0:c0 + LANES].astype(F32)
                rhs = jnp.concatenate(
                    [jnp.where(lo_mask, xp, 0.0), jnp.where(lo_mask, 0.0, xp),
                     jnp.where(lo_mask, sf, 0.0), jnp.where(lo_mask, 0.0, sf),
                     jnp.where(lo_mask, sb, 0.0), jnp.where(lo_mask, 0.0, sb)], axis=0).astype(BF16)
                ypair = jnp.dot(lhs, rhs, preferred_element_type=F32)
                ypair = ypair + dskip_ref[:, c0:c0 + LANES] * xp
                zz = z_ref[0, pl.ds(r0, CHUNK), c0:c0 + LANES]
                u_ref[:, _Y_OFF + c0:_Y_OFF + c0 + LANES] = ypair * _silu(zz)
        last = pre[:, LANES - 1:LANES]
        w_rows = jnp.exp(last - pre) * dt
        state_update(hf, xb, w_rows[0:h, :], last[0:h, :])
        if need_y:
            yv = u_ref[:, _Y_OFF:_Y_OFF + SSD_INNER]
            ms = jnp.mean(yv * yv, axis=-1, keepdims=True)
            y_ref[0, pl.ds(r0, CHUNK), :] = (yv * lax.rsqrt(ms + EPS) * gain_ref[...]).astype(y_ref.dtype)
        return carry

    @pl.when(sw == 0)
    def _():
        lax.fori_loop(0, cps, backward_chunk, 0)

    @pl.when(sw == 1)
    def _():
        lax.fori_loop(0, cps, forward_chunk, 0)

    @pl.when((sw == 0) & (s == nsteps - 1))
    def _():
        hb_out[0] = hb[...]

    @pl.when((sw == 1) & (s == nsteps - 1))
    def _():
        hf_out[0] = hf[...]


def _ssd(xbc, dts, z, consts, h0f, h0b, *, rows, need_y):
    bt, seq, _ = xbc.shape
    nsteps = seq // rows
    cps = rows // CHUNK
    hb8 = rows // SUBLANES
    nb8 = seq // SUBLANES
    cw, cb, dskip, gain, expm = consts
    dt_spec = pl.BlockSpec((cps, 2 * SSD_HEADS, CHUNK), lambda b, sw, s: (b * nsteps + blk(b, sw, s), 0, 0))

    def blk(b, sw, s):
        return jnp.where(sw == 0, nsteps - 1 - s, s)

    def full(a):
        return pl.BlockSpec(a.shape, lambda b, sw, s: (0,) * a.ndim)

    in_specs = [
        pl.BlockSpec((1, rows, CONV_DIM), lambda b, sw, s: (b, blk(b, sw, s), 0)),
        pl.BlockSpec((1, SUBLANES, CONV_DIM), lambda b, sw, s: (b, jnp.maximum(blk(b, sw, s) * hb8 - 1, 0), 0)),
        pl.BlockSpec((1, SUBLANES, CONV_DIM),
                     lambda b, sw, s: (b, jnp.minimum((blk(b, sw, s) + 1) * hb8, nb8 - 1), 0)),
        dt_spec, dt_spec, dt_spec,
        pl.BlockSpec((1, rows, SSD_INNER), lambda b, sw, s: (b, jnp.where(sw == 0, 0, s), 0)),
        full(cw), full(cb), full(dskip), full(gain), full(expm),
        pl.BlockSpec((1, SSD_STATE, SSD_INNER), lambda b, sw, s: (b, 0, 0)),
        pl.BlockSpec((1, SSD_STATE, SSD_INNER), lambda b, sw, s: (b, 0, 0)),
    ]
    st_spec = pl.BlockSpec((1, SSD_STATE, SSD_INNER), lambda b, sw, s: (b, 0, 0))
    st_shape = jax.ShapeDtypeStruct((bt, SSD_STATE, SSD_INNER), F32)
    out_specs, out_shapes = [st_spec, st_spec], [st_shape, st_shape]
    if need_y:
        out_specs = [pl.BlockSpec((1, rows, SSD_INNER), lambda b, sw, s: (b, jnp.where(sw == 0, 0, s), 0))] + out_specs
        out_shapes = [jax.ShapeDtypeStruct((bt, seq, SSD_INNER), BF16)] + out_shapes
    scratch = [
        pltpu.VMEM((CHUNK + 2 * SUBLANES, CONV_DIM), F32),
        pltpu.VMEM((CHUNK, _Y_OFF + SSD_INNER), F32),
        pltpu.VMEM((SSD_STATE, SSD_INNER), F32),
        pltpu.VMEM((SSD_STATE, SSD_INNER), F32),
        pltpu.VMEM((seq // CHUNK, SSD_STATE, SSD_INNER), BF16),
        pltpu.VMEM((seq // CHUNK, CHUNK, _XB_COLS), F32),
    ]
    return pl.pallas_call(
        functools.partial(_ssd_kernel, need_y=need_y, nsteps=nsteps, cps=cps),
        grid=(bt, 2, nsteps),
        in_specs=in_specs,
        out_specs=out_specs,
        out_shape=out_shapes,
        scratch_shapes=scratch,
        compiler_params=pltpu.CompilerParams(dimension_semantics=("arbitrary", "arbitrary", "arbitrary")),
        name="ssd_y" if need_y else "ssd_ctx",
    )(xbc, xbc, xbc, *dts, z, cw, cb, dskip, gain, expm, h0f, h0b)


def _attn_kernel(sink_ref, q_ref, kp_ref, kc_ref, kn_ref, vp_ref, vc_ref, vn_ref, kx_ref, vx_ref, o_ref, *, nsteps):
    m = pl.program_id(1)
    lane = lax.broadcasted_iota(jnp.int32, (1, LANES), 1)
    lo = lane < HEAD_DIM
    rq = Q_PER_KV * WINDOW
    qi = lax.broadcasted_iota(jnp.int32, (rq, WINDOW), 0) & (WINDOW - 1)
    kj = lax.broadcasted_iota(jnp.int32, (rq, WINDOW), 1)
    rblk = lax.shift_right_logical(lax.broadcasted_iota(jnp.int32, (rq, 1), 0), int(math.log2(WINDOW)))
    zero = jnp.zeros((), BF16)

    kpieces = [kp_ref[0], kc_ref[0, 0:WINDOW], kc_ref[0, WINDOW:2 * WINDOW], kn_ref[0]]
    vpieces = [vp_ref[0], vc_ref[0, 0:WINDOW], vc_ref[0, WINDOW:2 * WINDOW], vn_ref[0]]
    kx = kx_ref[0]
    vx = vx_ref[0]
    nt = (((1,), (1,)), ((), ()))

    for sb in range(ROWS_ATT // WINDOW):
        qsb = q_ref[0, sb * WINDOW:(sb + 1) * WINDOW, :]
        prev_ok = jnp.logical_or(m > 0, sb > 0)
        next_ok = jnp.logical_or(m < nsteps - 1, sb < ROWS_ATT // WINDOW - 1)
        outs = [None] * Q_PER_KV
        for kv in range(ATTN_KV_HEADS):
            sel = lo if kv == 0 else jnp.logical_not(lo)
            qs = jnp.concatenate(
                [jnp.where(sel, qsb[:, j * LANES:(j + 1) * LANES], zero) for j in range(Q_PER_KV)], axis=0)
            s_prev = lax.dot_general(qs, kpieces[sb], nt, preferred_element_type=F32)
            s_cur = lax.dot_general(qs, kpieces[sb + 1], nt, preferred_element_type=F32)
            s_next = lax.dot_general(qs, kpieces[sb + 2], nt, preferred_element_type=F32)
            s_ctx = lax.dot_general(qs, kx, nt, preferred_element_type=F32)
            s_prev = jnp.where((kj >= qi) & prev_ok, s_prev, NEG)
            s_next = jnp.where((kj <= qi) & next_ok, s_next, NEG)
            sink = jnp.zeros((rq, 1), F32)
            for j in range(Q_PER_KV):
                sink = jnp.where(rblk == j, sink_ref[kv * Q_PER_KV + j], sink)
            n_ct = s_ctx.shape[1] // LANES
            tile_max = jnp.maximum(jnp.maximum(s_prev, s_cur), s_next)
            for t in range(n_ct):
                tile_max = jnp.maximum(tile_max, s_ctx[:, t * LANES:(t + 1) * LANES])
            mx = jnp.maximum(jnp.max(tile_max, axis=-1, keepdims=True), sink)
            p_prev = jnp.exp(s_prev - mx)
            p_cur = jnp.exp(s_cur - mx)
            p_next = jnp.exp(s_next - mx)
            p_ctx = jnp.exp(s_ctx - mx)
            tile_sum = p_prev + p_cur + p_next
            for t in range(n_ct):
                tile_sum = tile_sum + p_ctx[:, t * LANES:(t + 1) * LANES]
            den = jnp.sum(tile_sum, axis=-1, keepdims=True) + jnp.exp(sink - mx)
            acc = jnp.dot(p_prev.astype(BF16), jnp.where(sel, vpieces[sb], zero), preferred_element_type=F32)
            acc += jnp.dot(p_cur.astype(BF16), jnp.where(sel, vpieces[sb + 1], zero), preferred_element_type=F32)
            acc += jnp.dot(p_next.astype(BF16), jnp.where(sel, vpieces[sb + 2], zero), preferred_element_type=F32)
            acc += jnp.dot(p_ctx.astype(BF16), jnp.where(sel, vx, zero), preferred_element_type=F32)
            acc = acc * (1.0 / den)
            for j in range(Q_PER_KV):
                part = acc[j * WINDOW:(j + 1) * WINDOW, :]
                outs[j] = part if outs[j] is None else outs[j] + part
        for j in range(Q_PER_KV):
            o_ref[0, sb * WINDOW:(sb + 1) * WINDOW, j * LANES:(j + 1) * LANES] = outs[j].astype(o_ref.dtype)


def _attention(sinks, q, k, v, kx, vx):
    bt, seq, _ = q.shape
    nsteps = seq // ROWS_ATT
    per = ROWS_ATT // WINDOW
    nb = seq // WINDOW
    n_ctx = kx.shape[1]

    def prev(b, m, s):
        return (b, jnp.maximum(m * per - 1, 0), 0)

    def cur(b, m, s):
        return (b, m, 0)

    def nxt(b, m, s):
        return (b, jnp.minimum((m + 1) * per, nb - 1), 0)

    small = (1, WINDOW, KV_WIDTH)
    big = (1, ROWS_ATT, KV_WIDTH)
    grid_spec = pltpu.PrefetchScalarGridSpec(
        num_scalar_prefetch=1,
        grid=(bt, nsteps),
        in_specs=[
            pl.BlockSpec((1, ROWS_ATT, ATTN_WIDTH), cur),
            pl.BlockSpec(small, prev), pl.BlockSpec(big, cur), pl.BlockSpec(small, nxt),
            pl.BlockSpec(small, prev), pl.BlockSpec(big, cur), pl.BlockSpec(small, nxt),
            pl.BlockSpec((1, n_ctx, KV_WIDTH), lambda b, m, s: (b, 0, 0)),
            pl.BlockSpec((1, n_ctx, KV_WIDTH), lambda b, m, s: (b, 0, 0)),
        ],
        out_specs=pl.BlockSpec((1, ROWS_ATT, ATTN_WIDTH), cur),
    )
    return pl.pallas_call(
        functools.partial(_attn_kernel, nsteps=nsteps),
        grid_spec=grid_spec,
        out_shape=jax.ShapeDtypeStruct((bt, seq, ATTN_WIDTH), BF16),
        name="attn",
    )(sinks, q, k, k, k, v, v, v, kx, vx)


def _outproj_kernel(ssd_ref, att_ref, x_ref, g1_ref, sh_ref, sc_ref, n2_ref, wa_ref, wb_ref, wr_ref, br_ref,
                    tri_ref, x1_ref, h2_ref, ids_ref, wcol_ref, cnt_ref, carry, *, nsteps):
    i = pl.program_id(0)
    rows = x_ref.shape[0]

    @pl.when(i == 0)
    def _():
        carry[...] = jnp.zeros_like(carry)

    acc = jnp.dot(ssd_ref[...], wa_ref[...], preferred_element_type=F32)
    acc += jnp.dot(att_ref[...], wb_ref[...], preferred_element_type=F32)
    x1 = x_ref[...] + g1_ref[0] * acc
    x1_ref[...] = x1
    ms = jnp.mean(x1 * x1, axis=-1, keepdims=True)
    h2 = (x1 * lax.rsqrt(ms + EPS)) * (n2_ref[...] * (1.0 + sc_ref[0])) + sh_ref[0]
    h2b = h2.astype(BF16)
    half = h2.shape[1] // 2
    lo_bits = lax.shift_right_logical(pltpu.bitcast(h2b[:, :half].astype(F32), jnp.uint32), jnp.uint32(16))
    hi_bits = pltpu.bitcast(h2b[:, half:].astype(F32), jnp.uint32) & jnp.uint32(0xFFFF0000)
    h2_ref[...] = hi_bits | lo_bits
    lt = lax.dot_general(wr_ref[...], h2b, (((1,), (1,)), ((), ())),
                         preferred_element_type=F32) + br_ref[...]
    row = lax.broadcasted_iota(jnp.int32, lt.shape, 0).astype(F32)
    big_i = float(ROUTE_ROWS)
    gl = jnp.where(row < N_GROUPS, lt, NEG)
    gmax = jnp.max(gl, axis=0, keepdims=True)
    gidx = jnp.min(jnp.where(gl == gmax, row, big_i), axis=0, keepdims=True)
    g_w = 1.0 / jnp.sum(jnp.exp(gl - gmax), axis=0, keepdims=True)
    lo = N_GROUPS + EXPERTS_PER_GROUP * gidx
    el = jnp.where((row >= lo) & (row < lo + EXPERTS_PER_GROUP), lt, NEG)
    m1 = jnp.max(el, axis=0, keepdims=True)
    i1 = jnp.min(jnp.where(el == m1, row, big_i), axis=0, keepdims=True)
    el2 = jnp.where(row == i1, NEG, el)
    m2 = jnp.max(el2, axis=0, keepdims=True)
    i2 = jnp.min(jnp.where(el2 == m2, row, big_i), axis=0, keepdims=True)
    r = jnp.exp(m2 - m1)
    w1 = g_w / (1.0 + r)
    w2 = g_w * r / (1.0 + r)
    oh1 = row == i1
    oh2 = row == i2
    cnt = jnp.where(oh1 | oh2, 1.0, 0.0)
    prefix = jnp.dot(cnt.astype(BF16), tri_ref[...], preferred_element_type=F32)
    base = carry[...] + prefix
    rank1 = jnp.sum(jnp.where(oh1, base, 0.0), axis=0, keepdims=True)
    rank2 = jnp.sum(jnp.where(oh2, base, 0.0), axis=0, keepdims=True)
    carry[...] = carry[...] + jnp.sum(cnt, axis=1, keepdims=True)
    r8 = lax.broadcasted_iota(jnp.int32, (SUBLANES, rows), 0)
    code1 = (i1 - N_GROUPS) * float(1 << RANK_BITS) + rank1
    code2 = (i2 - N_GROUPS) * float(1 << RANK_BITS) + rank2
    ids_ref[...] = jnp.where(r8 == 0, code1, jnp.where(r8 == 1, code2, 0.0)).astype(jnp.int32)
    r128 = lax.broadcasted_iota(jnp.int32, (LANES, rows), 0)
    wfull = jnp.where(r128 == 0, w1, jnp.where(r128 == 1, w2, 0.0))
    for j in range(rows // LANES):
        wcol_ref[j * LANES:(j + 1) * LANES, :] = wfull[:, j * LANES:(j + 1) * LANES].T

    @pl.when(i == nsteps - 1)
    def _():
        cnt_ref[...] = carry[:, 0:LANES]


def _outproj(ssd, att, xf, mod3, norm2, wa, wb, wr, br, tri, *, rows, seq_blocks):
    t, d = xf.shape
    nsteps = t // rows

    def mod(col):
        return pl.BlockSpec((1, 1, d), lambda i: (i // seq_blocks, 0, col))

    def full(a):
        return pl.BlockSpec(a.shape, lambda i: (0,) * a.ndim)

    return pl.pallas_call(
        functools.partial(_outproj_kernel, nsteps=nsteps),
        grid=(nsteps,),
        in_specs=[pl.BlockSpec((rows, SSD_INNER), lambda i: (i, 0)),
                  pl.BlockSpec((rows, ATTN_WIDTH), lambda i: (i, 0)),
                  pl.BlockSpec((rows, d), lambda i: (i, 0)),
                  mod(2), mod(3), mod(4), full(norm2), full(wa), full(wb), full(wr), full(br), full(tri)],
        out_specs=[pl.BlockSpec((rows, d), lambda i: (i, 0)),
                   pl.BlockSpec((rows, d // 2), lambda i: (i, 0)),
                   pl.BlockSpec((SUBLANES, rows), lambda i: (0, i)),
                   pl.BlockSpec((rows, LANES), lambda i: (i, 0)),
                   pl.BlockSpec((ROUTE_ROWS, LANES), lambda i: (0, 0))],
        out_shape=[jax.ShapeDtypeStruct((t, d), F32),
                   jax.ShapeDtypeStruct((t, d // 2), jnp.uint32),
                   jax.ShapeDtypeStruct((SUBLANES, t), jnp.int32),
                   jax.ShapeDtypeStruct((t, LANES), F32),
                   jax.ShapeDtypeStruct((ROUTE_ROWS, LANES), F32)],
        scratch_shapes=[pltpu.VMEM((ROUTE_ROWS, rows), F32)],
        compiler_params=pltpu.CompilerParams(dimension_semantics=("arbitrary",)),
        name="outproj_router",
    )(ssd, att, xf, mod3, mod3, mod3, norm2, wa, wb, wr, br, tri)


def _expert_kernel(blk_e_ref, nused_ref, pstart_ref, counts_ref, eord_ref, enext_ref, dst_ref,
                   h2_hbm, wg_hbm, wu_hbm, wd_hbm, y_ref,
                   h2v, xg, wgs, wus, wds, wgb, wub, wdb, tok_ref, sem, wsem, *, n_tok, nblocks):
    b = pl.program_id(0)
    nused = nused_ref[0]
    slot = b % 2
    half = h2v.shape[1]

    def h2_copy():
        return pltpu.make_async_copy(h2_hbm, h2v, sem.at[0])

    def weight_copies(e, sl):
        return (pltpu.make_async_copy(wg_hbm.at[e], wgs.at[sl], wsem.at[sl, 0]),
                pltpu.make_async_copy(wu_hbm.at[e], wus.at[sl], wsem.at[sl, 1]),
                pltpu.make_async_copy(wd_hbm.at[e], wds.at[sl], wsem.at[sl, 2]))

    def gather_rows(blk, sl):
        for r in range(MOE_BLK):
            xg[sl, pl.ds(r, 1), :] = h2v[pl.ds(tok_ref[blk * MOE_BLK + r], 1), :]

    @pl.when(b == 0)
    def _():
        h2_copy().start()
        for cp in weight_copies(blk_e_ref[0], 0):
            cp.start()
        def pad_expert(e, carry):
            lo = pstart_ref[e] + counts_ref[e]
            hi = pstart_ref[e] + (counts_ref[e] + MOE_BLK - 1) // MOE_BLK * MOE_BLK

            def pad_slot(s, c):
                tok_ref[s] = 0
                return c
            return lax.fori_loop(lo, hi, pad_slot, carry)
        lax.fori_loop(0, N_EXPERTS, pad_expert, 0)

        for k in range(2):
            def claim(t, carry, k=k):
                tok_ref[dst_ref[k * n_tok + t]] = t
                return carry
            lax.fori_loop(0, n_tok, claim, 0, unroll=16)
        h2_copy().wait()
        gather_rows(0, 0)

    e = blk_e_ref[b]
    e_prev = blk_e_ref[jnp.maximum(b - 1, 0)]

    @pl.when((b == 0) | (e != e_prev))
    def _():
        sl = eord_ref[e] % 2
        for cp in weight_copies(e, sl):
            cp.wait()
        wgb[...] = wgs[sl].astype(BF16)
        wub[...] = wus[sl].astype(BF16)
        wdb[...] = wds[sl].astype(BF16)
        e_next = enext_ref[e]

        @pl.when(e_next >= 0)
        def _():
            for cp in weight_copies(e_next, 1 - sl):
                cp.start()

    @pl.when(b < nused)
    def _():
        words = xg[slot]
        x_lo = pltpu.bitcast(words << jnp.uint32(16), F32).astype(BF16)
        x_hi = pltpu.bitcast(words & jnp.uint32(0xFFFF0000), F32).astype(BF16)
        gate = (jnp.dot(x_lo, wgb[0:half, :], preferred_element_type=F32)
                + jnp.dot(x_hi, wgb[half:, :], preferred_element_type=F32))
        up = (jnp.dot(x_lo, wub[0:half, :], preferred_element_type=F32)
              + jnp.dot(x_hi, wub[half:, :], preferred_element_type=F32))
        hmid = (_silu(gate) * up).astype(BF16)
        y = jnp.dot(hmid, wdb[...], preferred_element_type=F32)
        for s in range(D_TILES):
            y_ref[pl.ds(s, MOE_BLK, stride=D_TILES), :] = y[:, s * LANES:(s + 1) * LANES]
        gather_rows(jnp.minimum(b + 1, nused - 1), 1 - slot)

    @pl.when(b >= nused)
    def _():
        y_ref[...] = jnp.zeros_like(y_ref)


def _experts(blk_e, nused, pstart, counts, eord, enext, dst, h2p, w_gate, w_up, w_down, *, nblocks, n_tok):
    d = w_gate.shape[1]
    hbm = pl.BlockSpec(memory_space=pl.ANY)
    grid_spec = pltpu.PrefetchScalarGridSpec(
        num_scalar_prefetch=7,
        grid=(nblocks,),
        in_specs=[hbm, hbm, hbm, hbm],
        out_specs=pl.BlockSpec((MOE_BLK * D_TILES, LANES), lambda b, *_: (b, 0)),
        scratch_shapes=[
            pltpu.VMEM(h2p.shape, jnp.uint32),
            pltpu.VMEM((2, MOE_BLK, d // 2), jnp.uint32),
            pltpu.VMEM((2, d, EXPERT_DIM), F32),
            pltpu.VMEM((2, d, EXPERT_DIM), F32),
            pltpu.VMEM((2, EXPERT_DIM, d), F32),
            pltpu.VMEM((d, EXPERT_DIM), BF16),
            pltpu.VMEM((d, EXPERT_DIM), BF16),
            pltpu.VMEM((EXPERT_DIM, d), BF16),
            pltpu.SMEM((nblocks * MOE_BLK,), jnp.int32),
            pltpu.SemaphoreType.DMA((1,)),
            pltpu.SemaphoreType.DMA((2, 3)),
        ],
    )
    return pl.pallas_call(
        functools.partial(_expert_kernel, n_tok=n_tok, nblocks=nblocks),
        grid_spec=grid_spec,
        out_shape=jax.ShapeDtypeStruct((nblocks * MOE_BLK * D_TILES, LANES), F32),
        compiler_params=pltpu.CompilerParams(dimension_semantics=("arbitrary",)),
        name="experts",
    )(blk_e, nused, pstart, counts, eord, enext, dst, h2p, w_gate, w_up, w_down)


def _combine_kernel(dst_ref, y_hbm, x1_ref, wcol_ref, g2_ref, nf_ref, o_ref, ybuf, sem, *, nsteps, n_tok):
    i = pl.program_id(0)
    rows = x1_ref.shape[0]
    slot = i % 2

    def row_copy(src, r, k, sl):
        return pltpu.make_async_copy(y_hbm.at[pl.ds(src * D_TILES, D_TILES)],
                                     ybuf.at[sl, k, pl.ds(r * D_TILES, D_TILES)], sem.at[sl])

    def start_rows(step, sl, r0, r1):
        for r in range(r0, r1):
            for k in range(2):
                row_copy(dst_ref[k * n_tok + step * rows + r], r, k, sl).start()

    def wait_rows(sl):
        for r in range(rows):
            for k in range(2):
                row_copy(0, r, k, sl).wait()

    @pl.when(i == 0)
    def _():
        start_rows(0, 0, 0, rows)

    @pl.when(i + 1 < nsteps)
    def _():
        start_rows(i + 1, 1 - slot, 0, rows)

    wait_rows(slot)
    w0 = wcol_ref[:, 0:1]
    w1 = wcol_ref[:, 1:2]
    ssq = jnp.zeros((rows, 1), F32)
    for s in range(D_TILES):
        cols = slice(s * LANES, (s + 1) * LANES)
        moe = (ybuf[slot, 0, pl.ds(s, rows, stride=D_TILES), :] * w0
               + ybuf[slot, 1, pl.ds(s, rows, stride=D_TILES), :] * w1)
        x2 = x1_ref[:, cols] + g2_ref[0, :, cols] * moe
        ssq = ssq + jnp.sum(x2 * x2, axis=-1, keepdims=True)
        o_ref[:, cols] = x2
    inv = lax.rsqrt(ssq * (1.0 / (D_TILES * LANES)) + EPS)
    o_ref[...] = o_ref[...] * inv * nf_ref[...]


def _combine(dst, y_buf, x1, wcol, mod3, norm_final, *, rows, seq_blocks):
    t, d = x1.shape
    nsteps = t // rows
    grid_spec = pltpu.PrefetchScalarGridSpec(
        num_scalar_prefetch=1,
        grid=(nsteps,),
        in_specs=[
            pl.BlockSpec(memory_space=pl.ANY),
            pl.BlockSpec((rows, d), lambda i, ds: (i, 0)),
            pl.BlockSpec((rows, LANES), lambda i, ds: (i, 0)),
            pl.BlockSpec((1, 1, d), lambda i, ds: (i // seq_blocks, 0, 5)),
            pl.BlockSpec((1, d), lambda i, ds: (0, 0)),
        ],
        out_specs=pl.BlockSpec((rows, d), lambda i, ds: (i, 0)),
        scratch_shapes=[pltpu.VMEM((2, 2, rows * D_TILES, LANES), F32), pltpu.SemaphoreType.DMA((2,))],
    )
    return pl.pallas_call(
        functools.partial(_combine_kernel, nsteps=nsteps, n_tok=t),
        grid_spec=grid_spec,
        out_shape=jax.ShapeDtypeStruct((t, d), F32),
        compiler_params=pltpu.CompilerParams(dimension_semantics=("arbitrary",)),
        name="combine",
    )(dst, y_buf, x1, wcol, mod3, norm_final)


def _rope_tables(seq):
    pos = np.arange(seq)
    n_freq = HEAD_DIM // 4
    inv = ROPE_BASE ** (-np.arange(n_freq, dtype=np.float32) / n_freq)
    ang = np.concatenate([(pos // GRID_W)[:, None] * inv, (pos % GRID_W)[:, None] * inv], axis=-1)
    ang = np.concatenate([ang, ang, ang, ang], axis=-1).astype(np.float32)
    return jnp.asarray(np.cos(ang), F32), jnp.asarray(np.sin(ang), F32)


def _head_expand_mat():
    m = np.zeros((SSD_HEADS, SSD_INNER), np.float32)
    for hh in range(SSD_HEADS):
        m[hh, hh * SSD_HEAD_DIM:(hh + 1) * SSD_HEAD_DIM] = 1.0
    return jnp.asarray(m)


def kernel(x, c, ctx, c_ctx, w_ada, b_ada, norm1, w_in, conv_w, conv_b, dt_bias, a_log, d_skip, ssd_norm,
           attn_sinks, w_out, norm2, w_group, b_group, w_expert, b_expert, w_gate, w_up, w_down, norm_final):
    return _pipeline(x, c, ctx, c_ctx, w_ada, b_ada, norm1, w_in, conv_w, conv_b, dt_bias, a_log, d_skip,
                     ssd_norm, attn_sinks, w_out, norm2, w_group, b_group, w_expert, b_expert, w_gate, w_up,
                     w_down, norm_final)["out"]


def _pipeline(x, c, ctx, c_ctx, w_ada, b_ada, norm1, w_in, conv_w, conv_b, dt_bias, a_log, d_skip, ssd_norm,
              attn_sinks, w_out, norm2, w_group, b_group, w_expert, b_expert, w_gate, w_up, w_down, norm_final):
    bt, seq, d = x.shape
    n_ctx = ctx.shape[1]
    t = bt * seq
    layer = 0
    assert w_ada.shape[0] == 1 and seq % ROWS_IN == 0 and n_ctx % CHUNK == 0

    cc = jnp.zeros((SUBLANES, d), F32).at[:bt].set(c).at[bt].set(c_ctx)
    mod = _ada(cc, w_ada[layer], b_ada[layer][None, :])
    mod3 = mod.reshape(SUBLANES, 1, 6 * d)

    w = w_in[layer]
    o_z, o_xbc, o_dt = 0, SSD_INNER, SSD_INNER + CONV_DIM
    o_q = o_dt + 2 * SSD_HEADS
    o_k, o_v = o_q + ATTN_WIDTH, o_q + ATTN_WIDTH + KV_WIDTH
    q_perm = np.concatenate([np.r_[j * HEAD_DIM:(j + 1) * HEAD_DIM, (j + 4) * HEAD_DIM:(j + 5) * HEAD_DIM]
                             for j in range(Q_PER_KV)])
    w_q = w[:, o_q:o_q + ATTN_WIDTH].reshape(d, ATTN_KV_HEADS, Q_PER_KV, HEAD_DIM).transpose(0, 2, 1, 3)
    w_q = w_q.reshape(d, ATTN_WIDTH)
    w_cat = jnp.concatenate([w[:, o_z:o_z + SSD_INNER], w[:, o_xbc:o_xbc + CONV_DIM], w_q,
                             w[:, o_k:o_k + KV_WIDTH], w[:, o_v:o_v + KV_WIDTH]], axis=1).astype(BF16)
    w_dtT = w[:, o_dt:o_dt + 2 * SSD_HEADS].T.astype(BF16)

    seq_blocks = seq // ROWS_IN
    cos, sin = _rope_tables(seq)
    bias_rows = jnp.broadcast_to(dt_bias[layer].reshape(2 * SSD_HEADS, 1), (2 * SSD_HEADS, LANES))
    alog_rows = jnp.broadcast_to(a_log[layer].reshape(2 * SSD_HEADS, 1), (2 * SSD_HEADS, LANES))
    segs_x = [("z", _COL_Z, SSD_INNER), ("xbc", _COL_XBC, CONV_DIM), ("dtT", 0, 0),
              ("q", _COL_Q, ATTN_WIDTH), ("k", _COL_K, KV_WIDTH), ("v", _COL_V, KV_WIDTH)]
    z_x, xbc_x, dt_x, pre_x, suf_x, q_x, k_x, v_x = _inproj(
        x.reshape(t, d), mod3, norm1[layer][None, :], w_cat, w_dtT, bias_rows, alog_rows, (cos, sin),
        rows=ROWS_IN, mod_row_fn=lambda i: i // seq_blocks, segs=segs_x, seq_blocks=seq_blocks)
    segs_c = [("xbc", _COL_XBC, CONV_DIM), ("dtT", 0, 0), ("k", _COL_K, KV_WIDTH), ("v", _COL_V, KV_WIDTH)]
    xbc_c, dt_c, pre_c, suf_c, k_c, v_c = _inproj(
        ctx.reshape(bt * n_ctx, d), mod3, norm1[layer][None, :], w_cat, w_dtT, bias_rows, alog_rows, None,
        rows=n_ctx, mod_row_fn=lambda i: bt, segs=segs_c, seq_blocks=1)

    cw = jnp.zeros((SUBLANES, CONV_DIM), F32).at[:CONV_WIDTH].set(conv_w[layer])
    cb = conv_b[layer][None, :]
    dskip_e = jnp.repeat(d_skip[layer], SSD_HEAD_DIM)[None, :]
    consts = (cw, cb, dskip_e, ssd_norm[layer][None, :], _head_expand_mat())

    zeros_state = jnp.zeros((bt, SSD_STATE, SSD_INNER), F32)
    z_dummy = jnp.zeros((bt, n_ctx, SSD_INNER), F32)
    h_cf, h_cb = _ssd(xbc_c.reshape(bt, n_ctx, CONV_DIM), (dt_c, pre_c, suf_c), z_dummy, consts,
                      zeros_state, zeros_state, rows=n_ctx, need_y=False)
    ssd_x, _, _ = _ssd(xbc_x.reshape(bt, seq, CONV_DIM), (dt_x, pre_x, suf_x), z_x.reshape(bt, seq, SSD_INNER),
                       consts, h_cf, h_cb, rows=ROWS_SSD, need_y=True)

    attn_x = _attention(attn_sinks[layer].astype(F32),
                        q_x.reshape(bt, seq, ATTN_WIDTH), k_x.reshape(bt, seq, KV_WIDTH),
                        v_x.reshape(bt, seq, KV_WIDTH), k_c.reshape(bt, n_ctx, KV_WIDTH),
                        v_c.reshape(bt, n_ctx, KV_WIDTH))

    wo = w_out[layer]
    wa = wo[:SSD_INNER].astype(BF16)
    wb = wo[SSD_INNER:].reshape(ATTN_KV_HEADS, Q_PER_KV, HEAD_DIM, d).transpose(1, 0, 2, 3)
    wb = wb.reshape(ATTN_WIDTH, d).astype(BF16)
    wr = jnp.zeros((ROUTE_ROWS, d), F32).at[:N_GROUPS].set(w_group[layer].T)
    wr = wr.at[N_GROUPS:N_GROUPS + N_EXPERTS].set(w_expert[layer].T).astype(BF16)
    br = jnp.zeros((ROUTE_ROWS,), F32).at[:N_GROUPS].set(b_group[layer])
    br = br.at[N_GROUPS:N_GROUPS + N_EXPERTS].set(b_expert[layer])
    br = jnp.broadcast_to(br[:, None], (ROUTE_ROWS, ROWS_IN))
    tri = jnp.asarray(np.triu(np.ones((ROWS_IN, ROWS_IN), np.float32), 1), BF16)
    x1, h2, ids, wcol, cnt = _outproj(
        ssd_x.reshape(t, SSD_INNER), attn_x.reshape(t, ATTN_WIDTH), x.reshape(t, d), mod3, norm2[layer][None, :],
        wa, wb, wr, br, tri, rows=ROWS_IN, seq_blocks=seq_blocks)

    counts = cnt[N_GROUPS:N_GROUPS + N_EXPERTS, 0].astype(jnp.int32)
    padded = (counts + MOE_BLK - 1) // MOE_BLK * MOE_BLK
    ends = jnp.cumsum(padded)
    pstart = ends - padded
    nblocks = (2 * t) // MOE_BLK + N_EXPERTS
    blk_start = jnp.arange(nblocks, dtype=jnp.int32) * MOE_BLK
    eids = jnp.arange(N_EXPERTS, dtype=jnp.int32)
    active = counts > 0
    last_active = jnp.max(jnp.where(active, eids, 0))
    blk_e = jnp.minimum(jnp.sum((ends[None, :] <= blk_start[:, None]).astype(jnp.int32), axis=1), last_active)
    nused = (ends[-1] // MOE_BLK).astype(jnp.int32).reshape(1)
    eord = jnp.cumsum(active.astype(jnp.int32)) - active.astype(jnp.int32)
    later = active[None, :] & (eids[None, :] > eids[:, None])
    enext = jnp.min(jnp.where(later, eids[None, :], N_EXPERTS), axis=1)
    enext = jnp.where(enext == N_EXPERTS, -1, enext).astype(jnp.int32)
    code = ids[0:2].reshape(-1)
    e_id = lax.shift_right_logical(code, RANK_BITS)
    onehot = e_id[:, None] == jnp.arange(N_EXPERTS, dtype=jnp.int32)[None, :]
    dst = jnp.sum(jnp.where(onehot, pstart[None, :], 0), axis=1) + (code & ((1 << RANK_BITS) - 1))

    y_buf = _experts(blk_e, nused, pstart, counts, eord, enext, dst, h2, w_gate[layer], w_up[layer],
                     w_down[layer], nblocks=nblocks, n_tok=t)
    out = _combine(dst, y_buf, x1, wcol, mod3, norm_final[None, :], rows=ROWS_CMB, seq_blocks=seq // ROWS_CMB)
    return dict(out=out.reshape(bt, seq, d), mod=mod, z=z_x, xbc=xbc_x, dt=dt_x, q=q_x, k=k_x, v=v_x,
                k_c=k_c, v_c=v_c, h_cf=h_cf, h_cb=h_cb, ssd=ssd_x, attn=attn_x, x1=x1, h2=h2, ids=ids,
                wcol=wcol, cnt=cnt, y_buf=y_buf, pstart=pstart, q_perm=q_perm)
```

```python
import functools
import math

import numpy as np
import jax
import jax.numpy as jnp
from jax import lax
from jax.experimental import pallas as pl
from jax.experimental.pallas import tpu as pltpu

F32 = jnp.float32
BF16 = jnp.bfloat16

D_MODEL = 1024
GRID_W = 64
EPS = 1e-6
SSD_INNER = 512
SSD_HEAD_DIM = 64
SSD_HEADS = 8
SSD_GROUPS = 2
SSD_STATE = 128
CONV_WIDTH = 5
CONV_DIM = SSD_INNER + 2 * SSD_GROUPS * SSD_STATE
CHUNK = 128
ATTN_WIDTH = 512
HEAD_DIM = 64
ATTN_Q_HEADS = 8
ATTN_KV_HEADS = 2
Q_PER_KV = 4
KV_WIDTH = 128
WINDOW = 128
ROPE_BASE = 10000.0
N_GROUPS = 4
EXPERTS_PER_GROUP = 8
N_EXPERTS = 32
EXPERT_DIM = 512

LANES = 128
SUBLANES = 8
NEG = -1e30

ROWS_IN = 512
ROWS_SSD = 512
ROWS_ATT = 256
ROUTE_ROWS = 48
MOE_BLK = 256
GATHER_SLOTS = 3
RANK_BITS = 16
D_TILES = D_MODEL // LANES
ROWS_CMB = 128


def _silu(v):
    half = 0.5 * v
    return half + half * jnp.tanh(half)


def _softplus(v):
    return jnp.maximum(v, 0.0) + jnp.log(1.0 + jnp.exp(-jnp.abs(v)))


def _ada_kernel(c_ref, w_ref, b_ref, o_ref):
    a = _silu(c_ref[...])
    o_ref[...] = jnp.dot(a.astype(BF16), w_ref[...].astype(BF16),
                         preferred_element_type=F32) + b_ref[...]


def _ada(cc, w_ada, b_ada):
    d, n = w_ada.shape
    tn = 1536
    return pl.pallas_call(
        _ada_kernel,
        grid=(n // tn,),
        in_specs=[pl.BlockSpec((SUBLANES, d), lambda j: (0, 0)),
                  pl.BlockSpec((d, tn), lambda j: (0, j)),
                  pl.BlockSpec((1, tn), lambda j: (0, j))],
        out_specs=pl.BlockSpec((SUBLANES, tn), lambda j: (0, j)),
        out_shape=jax.ShapeDtypeStruct((SUBLANES, n), F32),
        name="ada",
    )(cc, w_ada, b_ada)


def _rope(t, cos, sin_signed, first_half):
    rot = jnp.where(first_half, pltpu.roll(t, LANES - HEAD_DIM // 2, 1), pltpu.roll(t, HEAD_DIM // 2, 1))
    return t * cos + rot * sin_signed


def _lane_cumsum(v, lane, reverse):
    k = 1
    while k < LANES:
        if reverse:
            v = v + jnp.where(lane < LANES - k, pltpu.roll(v, LANES - k, 1), 0.0)
        else:
            v = v + jnp.where(lane >= k, pltpu.roll(v, k, 1), 0.0)
        k *= 2
    return v


def _inproj_kernel(*refs, segs, rope, nsteps, seq_blocks):
    x_ref, sh_ref, sc_ref, g_ref, w_ref, wdt_ref, bias_ref, alog_ref, cw_ref, cb_ref = refs[:10]
    pos = 10
    if rope:
        cos_ref, sin_ref = refs[10:12]
        pos = 12
    n_out, u_idx = 0, None
    for kind, _, _ in segs:
        if kind == "xbc":
            u_idx = n_out
        n_out += 3 if kind == "dtT" else 1
    out_refs = refs[pos:pos + n_out]
    xs, win, tail = refs[pos + n_out:]
    i = pl.program_id(0)

    @pl.when(i == 0)
    def _():
        tail[...] = jnp.zeros_like(tail)

    @pl.when(i < nsteps)
    def _():
        _inproj_tile(x_ref, sh_ref, sc_ref, g_ref, w_ref, wdt_ref, bias_ref, alog_ref,
                     cos_ref if rope else None, sin_ref if rope else None, out_refs, xs.at[i % 2], segs, rope)

    @pl.when(i >= 1)
    def _():
        u_ref = out_refs[u_idx]
        rows = x_ref.shape[0]
        j = i - 1
        first_of_seq = (j % seq_blocks) == 0
        last_of_seq = (j % seq_blocks) == seq_blocks - 1
        prev_tile = xs.at[j % 2]
        win[0:SUBLANES, :] = jnp.where(first_of_seq, 0.0, tail[...])
        win[SUBLANES:SUBLANES + rows, :] = prev_tile[...]
        win[SUBLANES + rows:, :] = jnp.where(last_of_seq, 0.0, xs[i % 2, 0:SUBLANES, :])
        tail[...] = prev_tile[rows - SUBLANES:rows, :]
        for r0 in range(0, rows, CHUNK):
            for j0 in range(0, CONV_DIM, LANES):
                acc = jnp.broadcast_to(cb_ref[:, j0:j0 + LANES], (CHUNK, LANES))
                for k in range(CONV_WIDTH):
                    off = r0 + SUBLANES - CONV_WIDTH // 2 + k
                    acc = acc + win[off:off + CHUNK, j0:j0 + LANES] * cw_ref[k:k + 1, j0:j0 + LANES]
                u_ref[r0:r0 + CHUNK, j0:j0 + LANES] = _silu(acc)


def _inproj_tile(x_ref, sh_ref, sc_ref, g_ref, w_ref, wdt_ref, bias_ref, alog_ref, cos_ref, sin_ref, out_refs,
                 xbc_out, segs, rope):
    outs = iter(out_refs)
    x = x_ref[...]
    ms = jnp.mean(x * x, axis=-1, keepdims=True)
    h = (x * lax.rsqrt(ms + EPS)) * (g_ref[...] * (1.0 + sc_ref[0])) + sh_ref[0]
    hb = h.astype(BF16)
    if rope:
        cos = cos_ref[...]
        lane = lax.broadcasted_iota(jnp.int32, cos.shape, 1)
        first_half = (lane & (HEAD_DIM - 1)) < HEAD_DIM // 2
        sin_s = jnp.where(first_half, -sin_ref[...], sin_ref[...])
    for kind, c0, width in segs:
        if kind == "dtT":
            dt_ref, pre_ref, suf_ref = next(outs), next(outs), next(outs)
            dtt = lax.dot_general(wdt_ref[...], hb, (((1,), (1,)), ((), ())), preferred_element_type=F32)
            lane16 = lax.broadcasted_iota(jnp.int32, (2 * SSD_HEADS, CHUNK), 1)
            neg_a = -jnp.exp(alog_ref[...])
            for j in range(dt_ref.shape[0]):
                dt = _softplus(dtt[:, j * CHUNK:(j + 1) * CHUNK] + bias_ref[...])
                a = dt * neg_a
                dt_ref[j] = dt
                pre_ref[j] = _lane_cumsum(a, lane16, False)
                suf_ref[j] = _lane_cumsum(a, lane16, True)
            continue
        o_ref = next(outs)
        if kind == "xbc":
            o_ref = xbc_out
        for j0 in range(0, width, 512):
            wj = min(512, width - j0)
            acc = jnp.dot(hb, w_ref[:, c0 + j0:c0 + j0 + wj], preferred_element_type=F32)
            if rope and kind in ("q", "k"):
                scale = HEAD_DIM ** -0.5 if kind == "q" else 1.0
                for l0 in range(0, wj, LANES):
                    t = _rope(acc[:, l0:l0 + LANES], cos, sin_s, first_half)
                    o_ref[:, j0 + l0:j0 + l0 + LANES] = (t * scale).astype(o_ref.dtype)
            elif kind == "z":
                o_ref[:, j0:j0 + wj] = _silu(acc)
            else:
                o_ref[:, j0:j0 + wj] = acc.astype(o_ref.dtype)


_COL_Z, _COL_XBC, _COL_Q, _COL_K, _COL_V = 0, 512, 1536, 2048, 2176
_W_COLS = 2304


def _inproj(xf, mod3, norm1, w_cat, w_dtT, bias_rows, alog_rows, cw, cb, rope_tabs, *, rows, mod_row_fn, segs,
            seq_blocks):
    t, d = xf.shape
    rope = rope_tabs is not None
    nsteps = t // rows

    def cur(i):
        return jnp.minimum(i, nsteps - 1)

    def full(a):
        return pl.BlockSpec(a.shape, lambda i: (0,) * a.ndim)

    in_specs = [
        pl.BlockSpec((rows, d), lambda i: (cur(i), 0)),
        pl.BlockSpec((1, 1, d), lambda i: (mod_row_fn(cur(i)), 0, 0)),
        pl.BlockSpec((1, 1, d), lambda i: (mod_row_fn(cur(i)), 0, 1)),
        full(norm1), full(w_cat), full(w_dtT), full(bias_rows), full(alog_rows), full(cw), full(cb),
    ]
    args = [xf, mod3, mod3, norm1, w_cat, w_dtT, bias_rows, alog_rows, cw, cb]
    if rope:
        in_specs += [pl.BlockSpec((rows, LANES), lambda i: (cur(i) % seq_blocks, 0))] * 2
        args += list(rope_tabs)
    out_specs, out_shapes = [], []
    for kind, _, width in segs:
        if kind == "dtT":
            for _ in range(3):
                out_specs.append(pl.BlockSpec((rows // CHUNK, 2 * SSD_HEADS, CHUNK), lambda i: (cur(i), 0, 0)))
                out_shapes.append(jax.ShapeDtypeStruct((t // CHUNK, 2 * SSD_HEADS, CHUNK), F32))
        elif kind == "xbc":
            out_specs.append(pl.BlockSpec((rows, width), lambda i: (jnp.maximum(i - 1, 0), 0)))
            out_shapes.append(jax.ShapeDtypeStruct((t, width), F32))
        else:
            dt = BF16 if kind in ("q", "k", "v") else F32
            out_specs.append(pl.BlockSpec((rows, width), lambda i: (cur(i), 0)))
            out_shapes.append(jax.ShapeDtypeStruct((t, width), dt))
    scratch = [
        pltpu.VMEM((2, rows, CONV_DIM), F32),
        pltpu.VMEM((rows + 2 * SUBLANES, CONV_DIM), F32),
        pltpu.VMEM((SUBLANES, CONV_DIM), F32),
    ]
    return pl.pallas_call(
        functools.partial(_inproj_kernel, segs=tuple(segs), rope=rope, nsteps=nsteps, seq_blocks=seq_blocks),
        grid=(nsteps + 1,),
        in_specs=in_specs,
        out_specs=out_specs,
        out_shape=out_shapes,
        scratch_shapes=scratch,
        compiler_params=pltpu.CompilerParams(dimension_semantics=("arbitrary",)),
        name="inproj_rope" if rope else "inproj_ctx",
    )(*args)


_XB_COLS = SSD_INNER + SSD_GROUPS * SSD_STATE


def _ssd_kernel(*refs, need_y, nsteps, cps):
    (u_ref, dt_ref, pre_ref, suf_ref, z_ref, dskip_ref, gain_ref, exp_ref, h0f_ref, h0b_ref) = refs[:10]
    if need_y:
        y_ref, hf_out, hb_out = refs[10:13]
        rest = refs[13:]
    else:
        y_ref = None
        hf_out, hb_out = refs[10:12]
        rest = refs[12:]
    ybuf, hf, hb, hbs = rest

    sw = pl.program_id(1)
    s = pl.program_id(2)
    bi = jnp.where(sw == 0, nsteps - 1 - s, s)
    h = SSD_HEADS

    @pl.when((sw == 0) & (s == 0))
    def _():
        hb[...] = h0b_ref[0]

    @pl.when((sw == 1) & (s == 0))
    def _():
        hf[...] = h0f_ref[0]

    ri = lax.broadcasted_iota(jnp.int32, (CHUNK, CHUNK), 0)
    ci = lax.broadcasted_iota(jnp.int32, (CHUNK, CHUNK), 1)
    lane128 = lax.broadcasted_iota(jnp.int32, (CHUNK, LANES), 1)
    lo_mask = lane128 < SSD_HEAD_DIM

    def dt_rows(c):
        return dt_ref[c], pre_ref[c], suf_ref[c]

    def state_update(state_ref, xb, w_rows, tot_col):
        dec = jnp.exp(jnp.sum(tot_col * exp_ref[...], axis=0, keepdims=True))
        for g in range(SSD_GROUPS):
            bm = xb[:, SSD_INNER + g * SSD_STATE:SSD_INNER + (g + 1) * SSD_STATE]
            bt = bm.T
            for pr in range(2):
                h0 = g * 4 + pr * 2
                c0 = h0 * SSD_HEAD_DIM
                xp = xb[:, c0:c0 + LANES]
                rhs = jnp.concatenate([jnp.where(lo_mask, xp, 0.0), jnp.where(lo_mask, 0.0, xp)],
                                      axis=0).astype(BF16)
                lhs = jnp.concatenate([bt * w_rows[h0:h0 + 1, :], bt * w_rows[h0 + 1:h0 + 2, :]],
                                      axis=1).astype(BF16)
                sres = jnp.dot(lhs, rhs, preferred_element_type=F32)
                state_ref[:, c0:c0 + LANES] = state_ref[:, c0:c0 + LANES] * dec[:, c0:c0 + LANES] + sres

    def col_forms(rows16):
        padded = jnp.concatenate([rows16, jnp.zeros((CHUNK - 2 * h, LANES), F32)], axis=0)
        return padded.T

    def backward_chunk(k, carry):
        c = cps - 1 - k
        gc = bi * cps + c
        xb = u_ref.at[0, pl.ds(pl.multiple_of(c * CHUNK, CHUNK), CHUNK)]
        dt, _, suf = dt_rows(c)
        hbs[gc] = hb[...].astype(BF16)
        s0 = suf[:, 0:1]
        w_rows = jnp.exp(s0 - suf) * dt
        state_update(hb, xb, w_rows[h:2 * h, :], s0[h:2 * h, :])
        return carry

    def forward_chunk(c, carry):
        gc = bi * cps + c
        r0 = pl.multiple_of(c * CHUNK, CHUNK)
        xb = u_ref.at[0, pl.ds(r0, CHUNK)]
        dt, pre, suf = dt_rows(c)
        rowsf = jnp.concatenate([pre[0:h, :], suf[h:2 * h, :]], axis=0)
        colsf = col_forms(rowsf)
        ecol = jnp.exp(colsf)
        for g in range(SSD_GROUPS):
            bm = xb[:, SSD_INNER + g * SSD_STATE:SSD_INNER + (g + 1) * SSD_STATE]
            cm = xb[:, _XB_COLS + g * SSD_STATE:_XB_COLS + (g + 1) * SSD_STATE]
            cmb = cm.astype(BF16)
            cbm = lax.dot_general(cmb, bm.astype(BF16), (((1,), (1,)), ((), ())),
                                  preferred_element_type=F32)
            for pr in range(2):
                h0 = g * 4 + pr * 2
                c0 = h0 * SSD_HEAD_DIM
                lhs_parts = []
                for hh in (h0, h0 + 1):
                    segf = colsf[:, hh:hh + 1] - rowsf[hh:hh + 1, :]
                    gf = jnp.exp(jnp.where(ci <= ri, segf, NEG)) * dt[hh:hh + 1, :]
                    segb = colsf[:, h + hh:h + hh + 1] - rowsf[h + hh:h + hh + 1, :]
                    gb = jnp.exp(jnp.where(ci >= ri, segb, NEG)) * dt[h + hh:h + hh + 1, :]
                    lhs_parts.append((cbm * (gf + gb)).astype(BF16))
                for hh in (h0, h0 + 1):
                    lhs_parts.append((cm * ecol[:, hh:hh + 1]).astype(BF16))
                for hh in (h0, h0 + 1):
                    lhs_parts.append((cm * ecol[:, h + hh:h + hh + 1]).astype(BF16))
                lhs = jnp.concatenate(lhs_parts, axis=1)
                xp = xb[:, c0:c0 + LANES]
                sf = hf[:, c0:c0 + LANES]
                sb = hbs[gc, :, c0:c0 + LANES].astype(F32)
                rhs = jnp.concatenate(
                    [jnp.where(lo_mask, xp, 0.0), jnp.where(lo_mask, 0.0, xp),
                     jnp.where(lo_mask, sf, 0.0), jnp.where(lo_mask, 0.0, sf),
                     jnp.where(lo_mask, sb, 0.0), jnp.where(lo_mask, 0.0, sb)], axis=0).astype(BF16)
                ypair = jnp.dot(lhs, rhs, preferred_element_type=F32)
                ypair = ypair + dskip_ref[:, c0:c0 + LANES] * xp
                ybuf[:, c0:c0 + LANES] = ypair * z_ref[0, pl.ds(r0, CHUNK), c0:c0 + LANES]
        last = pre[:, LANES - 1:LANES]
        w_rows = jnp.exp(last - pre) * dt
        state_update(hf, xb, w_rows[0:h, :], last[0:h, :])
        if need_y:
            yv = ybuf[...]
            ms = jnp.mean(yv * yv, axis=-1, keepdims=True)
            y_ref[0, pl.ds(r0, CHUNK), :] = (yv * lax.rsqrt(ms + EPS) * gain_ref[...]).astype(y_ref.dtype)
        return carry

    @pl.when(sw == 0)
    def _():
        lax.fori_loop(0, cps, backward_chunk, 0)

    @pl.when(sw == 1)
    def _():
        lax.fori_loop(0, cps, forward_chunk, 0)

    @pl.when((sw == 0) & (s == nsteps - 1))
    def _():
        hb_out[0] = hb[...]

    @pl.when((sw == 1) & (s == nsteps - 1))
    def _():
        hf_out[0] = hf[...]


def _ssd(u, dts, z, consts, h0f, h0b, *, rows, need_y):
    bt, seq, _ = u.shape
    nsteps = seq // rows
    cps = rows // CHUNK
    dskip, gain, expm = consts
    dt_spec = pl.BlockSpec((cps, 2 * SSD_HEADS, CHUNK), lambda b, sw, s: (b * nsteps + blk(b, sw, s), 0, 0))

    def blk(b, sw, s):
        return jnp.where(sw == 0, nsteps - 1 - s, s)

    def full(a):
        return pl.BlockSpec(a.shape, lambda b, sw, s: (0,) * a.ndim)

    in_specs = [
        pl.BlockSpec((1, rows, CONV_DIM), lambda b, sw, s: (b, blk(b, sw, s), 0)),
        dt_spec, dt_spec, dt_spec,
        pl.BlockSpec((1, rows, SSD_INNER), lambda b, sw, s: (b, jnp.where(sw == 0, 0, s), 0)),
        full(dskip), full(gain), full(expm),
        pl.BlockSpec((1, SSD_STATE, SSD_INNER), lambda b, sw, s: (b, 0, 0)),
        pl.BlockSpec((1, SSD_STATE, SSD_INNER), lambda b, sw, s: (b, 0, 0)),
    ]
    st_spec = pl.BlockSpec((1, SSD_STATE, SSD_INNER), lambda b, sw, s: (b, 0, 0))
    st_shape = jax.ShapeDtypeStruct((bt, SSD_STATE, SSD_INNER), F32)
    out_specs, out_shapes = [st_spec, st_spec], [st_shape, st_shape]
    if need_y:
        out_specs = [pl.BlockSpec((1, rows, SSD_INNER), lambda b, sw, s: (b, jnp.where(sw == 0, 0, s), 0))] + out_specs
        out_shapes = [jax.ShapeDtypeStruct((bt, seq, SSD_INNER), BF16)] + out_shapes
    scratch = [
        pltpu.VMEM((CHUNK, SSD_INNER), F32),
        pltpu.VMEM((SSD_STATE, SSD_INNER), F32),
        pltpu.VMEM((SSD_STATE, SSD_INNER), F32),
        pltpu.VMEM((seq // CHUNK, SSD_STATE, SSD_INNER), BF16),
    ]
    return pl.pallas_call(
        functools.partial(_ssd_kernel, need_y=need_y, nsteps=nsteps, cps=cps),
        grid=(bt, 2, nsteps),
        in_specs=in_specs,
        out_specs=out_specs,
        out_shape=out_shapes,
        scratch_shapes=scratch,
        compiler_params=pltpu.CompilerParams(dimension_semantics=("arbitrary", "arbitrary", "arbitrary")),
        name="ssd_y" if need_y else "ssd_ctx",
    )(u, *dts, z, dskip, gain, expm, h0f, h0b)


def _attn_kernel(sink_ref, q_ref, kp_ref, kc_ref, kn_ref, vp_ref, vc_ref, vn_ref, kx_ref, vx_ref, o_ref, *, nsteps):
    m = pl.program_id(1)
    lane = lax.broadcasted_iota(jnp.int32, (1, LANES), 1)
    lo = lane < HEAD_DIM
    rq = Q_PER_KV * WINDOW
    qi = lax.broadcasted_iota(jnp.int32, (rq, WINDOW), 0) & (WINDOW - 1)
    kj = lax.broadcasted_iota(jnp.int32, (rq, WINDOW), 1)
    rblk = lax.shift_right_logical(lax.broadcasted_iota(jnp.int32, (rq, 1), 0), int(math.log2(WINDOW)))
    zero = jnp.zeros((), BF16)

    kpieces = [kp_ref[0], kc_ref[0, 0:WINDOW], kc_ref[0, WINDOW:2 * WINDOW], kn_ref[0]]
    vpieces = [vp_ref[0], vc_ref[0, 0:WINDOW], vc_ref[0, WINDOW:2 * WINDOW], vn_ref[0]]
    kx = kx_ref[0]
    vx = vx_ref[0]
    nt = (((1,), (1,)), ((), ()))

    for sb in range(ROWS_ATT // WINDOW):
        qsb = q_ref[0, sb * WINDOW:(sb + 1) * WINDOW, :]
        prev_ok = jnp.logical_or(m > 0, sb > 0)
        next_ok = jnp.logical_or(m < nsteps - 1, sb < ROWS_ATT // WINDOW - 1)
        outs = [None] * Q_PER_KV
        for kv in range(ATTN_KV_HEADS):
            sel = lo if kv == 0 else jnp.logical_not(lo)
            qs = jnp.concatenate(
                [jnp.where(sel, qsb[:, j * LANES:(j + 1) * LANES], zero) for j in range(Q_PER_KV)], axis=0)
            s_prev = lax.dot_general(qs, kpieces[sb], nt, preferred_element_type=F32)
            s_cur = lax.dot_general(qs, kpieces[sb + 1], nt, preferred_element_type=F32)
            s_next = lax.dot_general(qs, kpieces[sb + 2], nt, preferred_element_type=F32)
            s_ctx = lax.dot_general(qs, kx, nt, preferred_element_type=F32)
            s_prev = jnp.where((kj >= qi) & prev_ok, s_prev, NEG)
            s_next = jnp.where((kj <= qi) & next_ok, s_next, NEG)
            sink = jnp.zeros((rq, 1), F32)
            for j in range(Q_PER_KV):
                sink = jnp.where(rblk == j, sink_ref[kv * Q_PER_KV + j], sink)
            n_ct = s_ctx.shape[1] // LANES
            tile_max = jnp.maximum(jnp.maximum(s_prev, s_cur), s_next)
            for t in range(n_ct):
                tile_max = jnp.maximum(tile_max, s_ctx[:, t * LANES:(t + 1) * LANES])
            mx = jnp.maximum(jnp.max(tile_max, axis=-1, keepdims=True), sink)
            p_prev = jnp.exp(s_prev - mx)
            p_cur = jnp.exp(s_cur - mx)
            p_next = jnp.exp(s_next - mx)
            p_ctx = jnp.exp(s_ctx - mx)
            tile_sum = p_prev + p_cur + p_next
            for t in range(n_ct):
                tile_sum = tile_sum + p_ctx[:, t * LANES:(t + 1) * LANES]
            den = jnp.sum(tile_sum, axis=-1, keepdims=True) + jnp.exp(sink - mx)
            acc = jnp.dot(p_prev.astype(BF16), jnp.where(sel, vpieces[sb], zero), preferred_element_type=F32)
            acc += jnp.dot(p_cur.astype(BF16), jnp.where(sel, vpieces[sb + 1], zero), preferred_element_type=F32)
            acc += jnp.dot(p_next.astype(BF16), jnp.where(sel, vpieces[sb + 2], zero), preferred_element_type=F32)
            acc += jnp.dot(p_ctx.astype(BF16), jnp.where(sel, vx, zero), preferred_element_type=F32)
            acc = acc * (1.0 / den)
            for j in range(Q_PER_KV):
                part = acc[j * WINDOW:(j + 1) * WINDOW, :]
                outs[j] = part if outs[j] is None else outs[j] + part
        for j in range(Q_PER_KV):
            o_ref[0, sb * WINDOW:(sb + 1) * WINDOW, j * LANES:(j + 1) * LANES] = outs[j].astype(o_ref.dtype)


def _attention(sinks, q, k, v, kx, vx):
    bt, seq, _ = q.shape
    nsteps = seq // ROWS_ATT
    per = ROWS_ATT // WINDOW
    nb = seq // WINDOW
    n_ctx = kx.shape[1]

    def prev(b, m, s):
        return (b, jnp.maximum(m * per - 1, 0), 0)

    def cur(b, m, s):
        return (b, m, 0)

    def nxt(b, m, s):
        return (b, jnp.minimum((m + 1) * per, nb - 1), 0)

    small = (1, WINDOW, KV_WIDTH)
    big = (1, ROWS_ATT, KV_WIDTH)
    grid_spec = pltpu.PrefetchScalarGridSpec(
        num_scalar_prefetch=1,
        grid=(bt, nsteps),
        in_specs=[
            pl.BlockSpec((1, ROWS_ATT, ATTN_WIDTH), cur),
            pl.BlockSpec(small, prev), pl.BlockSpec(big, cur), pl.BlockSpec(small, nxt),
            pl.BlockSpec(small, prev), pl.BlockSpec(big, cur), pl.BlockSpec(small, nxt),
            pl.BlockSpec((1, n_ctx, KV_WIDTH), lambda b, m, s: (b, 0, 0)),
            pl.BlockSpec((1, n_ctx, KV_WIDTH), lambda b, m, s: (b, 0, 0)),
        ],
        out_specs=pl.BlockSpec((1, ROWS_ATT, ATTN_WIDTH), cur),
    )
    return pl.pallas_call(
        functools.partial(_attn_kernel, nsteps=nsteps),
        grid_spec=grid_spec,
        out_shape=jax.ShapeDtypeStruct((bt, seq, ATTN_WIDTH), BF16),
        name="attn",
    )(sinks, q, k, k, k, v, v, v, kx, vx)


def _outproj_kernel(ssd_ref, att_ref, x_ref, g1_ref, sh_ref, sc_ref, n2_ref, wa_ref, wb_ref, wr_ref, br_ref,
                    tri_ref, x1_ref, h2_ref, ids_ref, wcol_ref, cnt_ref, carry, *, nsteps):
    i = pl.program_id(0)
    rows = x_ref.shape[0]

    @pl.when(i == 0)
    def _():
        carry[...] = jnp.zeros_like(carry)

    acc = jnp.dot(ssd_ref[...], wa_ref[...], preferred_element_type=F32)
    acc += jnp.dot(att_ref[...], wb_ref[...], preferred_element_type=F32)
    x1 = x_ref[...] + g1_ref[0] * acc
    x1_ref[...] = x1
    ms = jnp.mean(x1 * x1, axis=-1, keepdims=True)
    h2 = (x1 * lax.rsqrt(ms + EPS)) * (n2_ref[...] * (1.0 + sc_ref[0])) + sh_ref[0]
    h2b = h2.astype(BF16)
    half = h2.shape[1] // 2
    lo_bits = lax.shift_right_logical(pltpu.bitcast(h2b[:, :half].astype(F32), jnp.uint32), jnp.uint32(16))
    hi_bits = pltpu.bitcast(h2b[:, half:].astype(F32), jnp.uint32) & jnp.uint32(0xFFFF0000)
    h2_ref[...] = hi_bits | lo_bits
    lt = lax.dot_general(wr_ref[...], h2b, (((1,), (1,)), ((), ())),
                         preferred_element_type=F32) + br_ref[...]
    row = lax.broadcasted_iota(jnp.int32, lt.shape, 0).astype(F32)
    big_i = float(ROUTE_ROWS)
    gl = jnp.where(row < N_GROUPS, lt, NEG)
    gmax = jnp.max(gl, axis=0, keepdims=True)
    gidx = jnp.min(jnp.where(gl == gmax, row, big_i), axis=0, keepdims=True)
    g_w = 1.0 / jnp.sum(jnp.exp(gl - gmax), axis=0, keepdims=True)
    lo = N_GROUPS + EXPERTS_PER_GROUP * gidx
    el = jnp.where((row >= lo) & (row < lo + EXPERTS_PER_GROUP), lt, NEG)
    m1 = jnp.max(el, axis=0, keepdims=True)
    i1 = jnp.min(jnp.where(el == m1, row, big_i), axis=0, keepdims=True)
    el2 = jnp.where(row == i1, NEG, el)
    m2 = jnp.max(el2, axis=0, keepdims=True)
    i2 = jnp.min(jnp.where(el2 == m2, row, big_i), axis=0, keepdims=True)
    r = jnp.exp(m2 - m1)
    w1 = g_w / (1.0 + r)
    w2 = g_w * r / (1.0 + r)
    oh1 = row == i1
    oh2 = row == i2
    cnt = jnp.where(oh1 | oh2, 1.0, 0.0)
    prefix = jnp.dot(cnt.astype(BF16), tri_ref[...], preferred_element_type=F32)
    base = carry[...] + prefix
    rank1 = jnp.sum(jnp.where(oh1, base, 0.0), axis=0, keepdims=True)
    rank2 = jnp.sum(jnp.where(oh2, base, 0.0), axis=0, keepdims=True)
    carry[...] = carry[...] + jnp.sum(cnt, axis=1, keepdims=True)
    r8 = lax.broadcasted_iota(jnp.int32, (SUBLANES, rows), 0)
    code1 = (i1 - N_GROUPS) * float(1 << RANK_BITS) + rank1
    code2 = (i2 - N_GROUPS) * float(1 << RANK_BITS) + rank2
    ids_ref[...] = jnp.where(r8 == 0, code1, jnp.where(r8 == 1, code2, 0.0)).astype(jnp.int32)
    r128 = lax.broadcasted_iota(jnp.int32, (LANES, rows), 0)
    wfull = jnp.where(r128 == 0, w1, jnp.where(r128 == 1, w2, 0.0))
    for j in range(rows // LANES):
        wcol_ref[j * LANES:(j + 1) * LANES, :] = wfull[:, j * LANES:(j + 1) * LANES].T

    @pl.when(i == nsteps - 1)
    def _():
        cnt_ref[...] = carry[:, 0:LANES]


def _outproj(ssd, att, xf, mod3, norm2, wa, wb, wr, br, tri, *, rows, seq_blocks):
    t, d = xf.shape
    nsteps = t // rows

    def mod(col):
        return pl.BlockSpec((1, 1, d), lambda i: (i // seq_blocks, 0, col))

    def full(a):
        return pl.BlockSpec(a.shape, lambda i: (0,) * a.ndim)

    return pl.pallas_call(
        functools.partial(_outproj_kernel, nsteps=nsteps),
        grid=(nsteps,),
        in_specs=[pl.BlockSpec((rows, SSD_INNER), lambda i: (i, 0)),
                  pl.BlockSpec((rows, ATTN_WIDTH), lambda i: (i, 0)),
                  pl.BlockSpec((rows, d), lambda i: (i, 0)),
                  mod(2), mod(3), mod(4), full(norm2), full(wa), full(wb), full(wr), full(br), full(tri)],
        out_specs=[pl.BlockSpec((rows, d), lambda i: (i, 0)),
                   pl.BlockSpec((rows, d // 2), lambda i: (i, 0)),
                   pl.BlockSpec((SUBLANES, rows), lambda i: (0, i)),
                   pl.BlockSpec((rows, LANES), lambda i: (i, 0)),
                   pl.BlockSpec((ROUTE_ROWS, LANES), lambda i: (0, 0))],
        out_shape=[jax.ShapeDtypeStruct((t, d), F32),
                   jax.ShapeDtypeStruct((t, d // 2), jnp.uint32),
                   jax.ShapeDtypeStruct((SUBLANES, t), jnp.int32),
                   jax.ShapeDtypeStruct((t, LANES), F32),
                   jax.ShapeDtypeStruct((ROUTE_ROWS, LANES), F32)],
        scratch_shapes=[pltpu.VMEM((ROUTE_ROWS, rows), F32)],
        compiler_params=pltpu.CompilerParams(dimension_semantics=("arbitrary",)),
        name="outproj_router",
    )(ssd, att, xf, mod3, mod3, mod3, norm2, wa, wb, wr, br, tri)


def _expert_kernel(blk_e_ref, nused_ref, pstart_ref, counts_ref, eord_ref, enext_ref, dst_ref,
                   h2_hbm, wg_hbm, wu_hbm, wd_hbm, y_ref,
                   h2v, xg, wgs, wus, wds, wgb, wub, wdb, tok_ref, sem, wsem, *, n_tok, nblocks):
    b = pl.program_id(0)
    nused = nused_ref[0]
    slot = b % 2
    half = h2v.shape[1]

    def h2_copy():
        return pltpu.make_async_copy(h2_hbm, h2v, sem.at[0])

    def weight_copies(e, sl):
        return (pltpu.make_async_copy(wg_hbm.at[e], wgs.at[sl], wsem.at[sl, 0]),
                pltpu.make_async_copy(wu_hbm.at[e], wus.at[sl], wsem.at[sl, 1]),
                pltpu.make_async_copy(wd_hbm.at[e], wds.at[sl], wsem.at[sl, 2]))

    def gather_rows(blk, sl):
        for r in range(MOE_BLK):
            xg[sl, pl.ds(r, 1), :] = h2v[pl.ds(tok_ref[blk * MOE_BLK + r], 1), :]

    @pl.when(b == 0)
    def _():
        h2_copy().start()
        for cp in weight_copies(blk_e_ref[0], 0):
            cp.start()
        def pad_expert(e, carry):
            lo = pstart_ref[e] + counts_ref[e]
            hi = pstart_ref[e] + (counts_ref[e] + MOE_BLK - 1) // MOE_BLK * MOE_BLK

            def pad_slot(s, c):
                tok_ref[s] = 0
                return c
            return lax.fori_loop(lo, hi, pad_slot, carry)
        lax.fori_loop(0, N_EXPERTS, pad_expert, 0)

        for k in range(2):
            def claim(t, carry, k=k):
                tok_ref[dst_ref[k * n_tok + t]] = t
                return carry
            lax.fori_loop(0, n_tok, claim, 0, unroll=16)
        h2_copy().wait()
        gather_rows(0, 0)

    e = blk_e_ref[b]
    e_prev = blk_e_ref[jnp.maximum(b - 1, 0)]

    @pl.when((b == 0) | (e != e_prev))
    def _():
        sl = eord_ref[e] % 2
        for cp in weight_copies(e, sl):
            cp.wait()
        wgb[...] = wgs[sl].astype(BF16)
        wub[...] = wus[sl].astype(BF16)
        wdb[...] = wds[sl].astype(BF16)
        e_next = enext_ref[e]

        @pl.when(e_next >= 0)
        def _():
            for cp in weight_copies(e_next, 1 - sl):
                cp.start()

    @pl.when(b < nused)
    def _():
        words = xg[slot]
        x_lo = pltpu.bitcast(words << jnp.uint32(16), F32).astype(BF16)
        x_hi = pltpu.bitcast(words & jnp.uint32(0xFFFF0000), F32).astype(BF16)
        gate = (jnp.dot(x_lo, wgb[0:half, :], preferred_element_type=F32)
                + jnp.dot(x_hi, wgb[half:, :], preferred_element_type=F32))
        up = (jnp.dot(x_lo, wub[0:half, :], preferred_element_type=F32)
              + jnp.dot(x_hi, wub[half:, :], preferred_element_type=F32))
        hmid = (_silu(gate) * up).astype(BF16)
        y = jnp.dot(hmid, wdb[...], preferred_element_type=F32)
        for s in range(D_TILES):
            y_ref[pl.ds(s, MOE_BLK, stride=D_TILES), :] = y[:, s * LANES:(s + 1) * LANES]
        gather_rows(jnp.minimum(b + 1, nused - 1), 1 - slot)

    @pl.when(b >= nused)
    def _():
        y_ref[...] = jnp.zeros_like(y_ref)


def _experts(blk_e, nused, pstart, counts, eord, enext, dst, h2p, w_gate, w_up, w_down, *, nblocks, n_tok):
    d = w_gate.shape[1]
    hbm = pl.BlockSpec(memory_space=pl.ANY)
    grid_spec = pltpu.PrefetchScalarGridSpec(
        num_scalar_prefetch=7,
        grid=(nblocks,),
        in_specs=[hbm, hbm, hbm, hbm],
        out_specs=pl.BlockSpec((MOE_BLK * D_TILES, LANES), lambda b, *_: (b, 0)),
        scratch_shapes=[
            pltpu.VMEM(h2p.shape, jnp.uint32),
            pltpu.VMEM((2, MOE_BLK, d // 2), jnp.uint32),
            pltpu.VMEM((2, d, EXPERT_DIM), F32),
            pltpu.VMEM((2, d, EXPERT_DIM), F32),
            pltpu.VMEM((2, EXPERT_DIM, d), F32),
            pltpu.VMEM((d, EXPERT_DIM), BF16),
            pltpu.VMEM((d, EXPERT_DIM), BF16),
            pltpu.VMEM((EXPERT_DIM, d), BF16),
            pltpu.SMEM((nblocks * MOE_BLK,), jnp.int32),
            pltpu.SemaphoreType.DMA((1,)),
            pltpu.SemaphoreType.DMA((2, 3)),
        ],
    )
    return pl.pallas_call(
        functools.partial(_expert_kernel, n_tok=n_tok, nblocks=nblocks),
        grid_spec=grid_spec,
        out_shape=jax.ShapeDtypeStruct((nblocks * MOE_BLK * D_TILES, LANES), F32),
        compiler_params=pltpu.CompilerParams(dimension_semantics=("arbitrary",)),
        name="experts",
    )(blk_e, nused, pstart, counts, eord, enext, dst, h2p, w_gate, w_up, w_down)


def _combine_kernel(dst_ref, y_hbm, x1_ref, wcol_ref, g2_ref, nf_ref, o_ref, ybuf, sem, *, nsteps, n_tok):
    i = pl.program_id(0)
    rows = x1_ref.shape[0]
    slot = i % 2

    def row_copy(src, r, k, sl):
        return pltpu.make_async_copy(y_hbm.at[pl.ds(src * D_TILES, D_TILES)],
                                     ybuf.at[sl, k, pl.ds(r * D_TILES, D_TILES)], sem.at[sl])

    def start_rows(step, sl, r0, r1):
        for r in range(r0, r1):
            for k in range(2):
                row_copy(dst_ref[k * n_tok + step * rows + r], r, k, sl).start()

    def wait_rows(sl):
        for r in range(rows):
            for k in range(2):
                row_copy(0, r, k, sl).wait()

    @pl.when(i == 0)
    def _():
        start_rows(0, 0, 0, rows)

    @pl.when(i + 1 < nsteps)
    def _():
        start_rows(i + 1, 1 - slot, 0, rows)

    wait_rows(slot)
    w0 = wcol_ref[:, 0:1]
    w1 = wcol_ref[:, 1:2]
    ssq = jnp.zeros((rows, 1), F32)
    for s in range(D_TILES):
        cols = slice(s * LANES, (s + 1) * LANES)
        moe = (ybuf[slot, 0, pl.ds(s, rows, stride=D_TILES), :] * w0
               + ybuf[slot, 1, pl.ds(s, rows, stride=D_TILES), :] * w1)
        x2 = x1_ref[:, cols] + g2_ref[0, :, cols] * moe
        ssq = ssq + jnp.sum(x2 * x2, axis=-1, keepdims=True)
        o_ref[:, cols] = x2
    inv = lax.rsqrt(ssq * (1.0 / (D_TILES * LANES)) + EPS)
    o_ref[...] = o_ref[...] * inv * nf_ref[...]


def _combine(dst, y_buf, x1, wcol, mod3, norm_final, *, rows, seq_blocks):
    t, d = x1.shape
    nsteps = t // rows
    grid_spec = pltpu.PrefetchScalarGridSpec(
        num_scalar_prefetch=1,
        grid=(nsteps,),
        in_specs=[
            pl.BlockSpec(memory_space=pl.ANY),
            pl.BlockSpec((rows, d), lambda i, ds: (i, 0)),
            pl.BlockSpec((rows, LANES), lambda i, ds: (i, 0)),
            pl.BlockSpec((1, 1, d), lambda i, ds: (i // seq_blocks, 0, 5)),
            pl.BlockSpec((1, d), lambda i, ds: (0, 0)),
        ],
        out_specs=pl.BlockSpec((rows, d), lambda i, ds: (i, 0)),
        scratch_shapes=[pltpu.VMEM((2, 2, rows * D_TILES, LANES), F32), pltpu.SemaphoreType.DMA((2,))],
    )
    return pl.pallas_call(
        functools.partial(_combine_kernel, nsteps=nsteps, n_tok=t),
        grid_spec=grid_spec,
        out_shape=jax.ShapeDtypeStruct((t, d), F32),
        compiler_params=pltpu.CompilerParams(dimension_semantics=("arbitrary",)),
        name="combine",
    )(dst, y_buf, x1, wcol, mod3, norm_final)


def _rope_tables(seq):
    pos = np.arange(seq)
    n_freq = HEAD_DIM // 4
    inv = ROPE_BASE ** (-np.arange(n_freq, dtype=np.float32) / n_freq)
    ang = np.concatenate([(pos // GRID_W)[:, None] * inv, (pos % GRID_W)[:, None] * inv], axis=-1)
    ang = np.concatenate([ang, ang, ang, ang], axis=-1).astype(np.float32)
    return jnp.asarray(np.cos(ang), F32), jnp.asarray(np.sin(ang), F32)


def _head_expand_mat():
    m = np.zeros((SSD_HEADS, SSD_INNER), np.float32)
    for hh in range(SSD_HEADS):
        m[hh, hh * SSD_HEAD_DIM:(hh + 1) * SSD_HEAD_DIM] = 1.0
    return jnp.asarray(m)


def kernel(x, c, ctx, c_ctx, w_ada, b_ada, norm1, w_in, conv_w, conv_b, dt_bias, a_log, d_skip, ssd_norm,
           attn_sinks, w_out, norm2, w_group, b_group, w_expert, b_expert, w_gate, w_up, w_down, norm_final):
    return _pipeline(x, c, ctx, c_ctx, w_ada, b_ada, norm1, w_in, conv_w, conv_b, dt_bias, a_log, d_skip,
                     ssd_norm, attn_sinks, w_out, norm2, w_group, b_group, w_expert, b_expert, w_gate, w_up,
                     w_down, norm_final)["out"]


def _pipeline(x, c, ctx, c_ctx, w_ada, b_ada, norm1, w_in, conv_w, conv_b, dt_bias, a_log, d_skip, ssd_norm,
              attn_sinks, w_out, norm2, w_group, b_group, w_expert, b_expert, w_gate, w_up, w_down, norm_final):
    bt, seq, d = x.shape
    n_ctx = ctx.shape[1]
    t = bt * seq
    layer = 0
    assert w_ada.shape[0] == 1 and seq % ROWS_IN == 0 and n_ctx % CHUNK == 0

    cc = jnp.zeros((SUBLANES, d), F32).at[:bt].set(c).at[bt].set(c_ctx)
    mod = _ada(cc, w_ada[layer], b_ada[layer][None, :])
    mod3 = mod.reshape(SUBLANES, 1, 6 * d)

    w = w_in[layer]
    o_z, o_xbc, o_dt = 0, SSD_INNER, SSD_INNER + CONV_DIM
    o_q = o_dt + 2 * SSD_HEADS
    o_k, o_v = o_q + ATTN_WIDTH, o_q + ATTN_WIDTH + KV_WIDTH
    q_perm = np.concatenate([np.r_[j * HEAD_DIM:(j + 1) * HEAD_DIM, (j + 4) * HEAD_DIM:(j + 5) * HEAD_DIM]
                             for j in range(Q_PER_KV)])
    w_q = w[:, o_q:o_q + ATTN_WIDTH].reshape(d, ATTN_KV_HEADS, Q_PER_KV, HEAD_DIM).transpose(0, 2, 1, 3)
    w_q = w_q.reshape(d, ATTN_WIDTH)
    w_cat = jnp.concatenate([w[:, o_z:o_z + SSD_INNER], w[:, o_xbc:o_xbc + CONV_DIM], w_q,
                             w[:, o_k:o_k + KV_WIDTH], w[:, o_v:o_v + KV_WIDTH]], axis=1).astype(BF16)
    w_dtT = w[:, o_dt:o_dt + 2 * SSD_HEADS].T.astype(BF16)

    seq_blocks = seq // ROWS_IN
    cos, sin = _rope_tables(seq)
    bias_rows = jnp.broadcast_to(dt_bias[layer].reshape(2 * SSD_HEADS, 1), (2 * SSD_HEADS, LANES))
    alog_rows = jnp.broadcast_to(a_log[layer].reshape(2 * SSD_HEADS, 1), (2 * SSD_HEADS, LANES))
    cw = jnp.zeros((SUBLANES, CONV_DIM), F32).at[:CONV_WIDTH].set(conv_w[layer])
    cb = conv_b[layer][None, :]
    segs_x = [("z", _COL_Z, SSD_INNER), ("xbc", _COL_XBC, CONV_DIM), ("dtT", 0, 0),
              ("q", _COL_Q, ATTN_WIDTH), ("k", _COL_K, KV_WIDTH), ("v", _COL_V, KV_WIDTH)]
    z_x, xbc_x, dt_x, pre_x, suf_x, q_x, k_x, v_x = _inproj(
        x.reshape(t, d), mod3, norm1[layer][None, :], w_cat, w_dtT, bias_rows, alog_rows, cw, cb, (cos, sin),
        rows=ROWS_IN, mod_row_fn=lambda i: i // seq_blocks, segs=segs_x, seq_blocks=seq_blocks)
    segs_c = [("xbc", _COL_XBC, CONV_DIM), ("dtT", 0, 0), ("k", _COL_K, KV_WIDTH), ("v", _COL_V, KV_WIDTH)]
    xbc_c, dt_c, pre_c, suf_c, k_c, v_c = _inproj(
        ctx.reshape(bt * n_ctx, d), mod3, norm1[layer][None, :], w_cat, w_dtT, bias_rows, alog_rows, cw, cb, None,
        rows=n_ctx, mod_row_fn=lambda i: bt, segs=segs_c, seq_blocks=1)

    dskip_e = jnp.repeat(d_skip[layer], SSD_HEAD_DIM)[None, :]
    consts = (dskip_e, ssd_norm[layer][None, :], _head_expand_mat())

    zeros_state = jnp.zeros((bt, SSD_STATE, SSD_INNER), F32)
    z_dummy = jnp.zeros((bt, n_ctx, SSD_INNER), F32)
    h_cf, h_cb = _ssd(xbc_c.reshape(bt, n_ctx, CONV_DIM), (dt_c, pre_c, suf_c), z_dummy, consts,
                      zeros_state, zeros_state, rows=n_ctx, need_y=False)
    ssd_x, _, _ = _ssd(xbc_x.reshape(bt, seq, CONV_DIM), (dt_x, pre_x, suf_x), z_x.reshape(bt, seq, SSD_INNER),
                       consts, h_cf, h_cb, rows=ROWS_SSD, need_y=True)

    attn_x = _attention(attn_sinks[layer].astype(F32),
                        q_x.reshape(bt, seq, ATTN_WIDTH), k_x.reshape(bt, seq, KV_WIDTH),
                        v_x.reshape(bt, seq, KV_WIDTH), k_c.reshape(bt, n_ctx, KV_WIDTH),
                        v_c.reshape(bt, n_ctx, KV_WIDTH))

    wo = w_out[layer]
    wa = wo[:SSD_INNER].astype(BF16)
    wb = wo[SSD_INNER:].reshape(ATTN_KV_HEADS, Q_PER_KV, HEAD_DIM, d).transpose(1, 0, 2, 3)
    wb = wb.reshape(ATTN_WIDTH, d).astype(BF16)
    wr = jnp.zeros((ROUTE_ROWS, d), F32).at[:N_GROUPS].set(w_group[layer].T)
    wr = wr.at[N_GROUPS:N_GROUPS + N_EXPERTS].set(w_expert[layer].T).astype(BF16)
    br = jnp.zeros((ROUTE_ROWS,), F32).at[:N_GROUPS].set(b_group[layer])
    br = br.at[N_GROUPS:N_GROUPS + N_EXPERTS].set(b_expert[layer])
    br = jnp.broadcast_to(br[:, None], (ROUTE_ROWS, ROWS_IN))
    tri = jnp.asarray(np.triu(np.ones((ROWS_IN, ROWS_IN), np.float32), 1), BF16)
    x1, h2, ids, wcol, cnt = _outproj(
        ssd_x.reshape(t, SSD_INNER), attn_x.reshape(t, ATTN_WIDTH), x.reshape(t, d), mod3, norm2[layer][None, :],
        wa, wb, wr, br, tri, rows=ROWS_IN, seq_blocks=seq_blocks)

    counts = cnt[N_GROUPS:N_GROUPS + N_EXPERTS, 0].astype(jnp.int32)
    padded = (counts + MOE_BLK - 1) // MOE_BLK * MOE_BLK
    ends = jnp.cumsum(padded)
    pstart = ends - padded
    nblocks = (2 * t) // MOE_BLK + N_EXPERTS
    blk_start = jnp.arange(nblocks, dtype=jnp.int32) * MOE_BLK
    eids = jnp.arange(N_EXPERTS, dtype=jnp.int32)
    active = counts > 0
    last_active = jnp.max(jnp.where(active, eids, 0))
    blk_e = jnp.minimum(jnp.sum((ends[None, :] <= blk_start[:, None]).astype(jnp.int32), axis=1), last_active)
    nused = (ends[-1] // MOE_BLK).astype(jnp.int32).reshape(1)
    eord = jnp.cumsum(active.astype(jnp.int32)) - active.astype(jnp.int32)
    later = active[None, :] & (eids[None, :] > eids[:, None])
    enext = jnp.min(jnp.where(later, eids[None, :], N_EXPERTS), axis=1)
    enext = jnp.where(enext == N_EXPERTS, -1, enext).astype(jnp.int32)
    code = ids[0:2].reshape(-1)
    e_id = lax.shift_right_logical(code, RANK_BITS)
    onehot = e_id[:, None] == jnp.arange(N_EXPERTS, dtype=jnp.int32)[None, :]
    dst = jnp.sum(jnp.where(onehot, pstart[None, :], 0), axis=1) + (code & ((1 << RANK_BITS) - 1))

    y_buf = _experts(blk_e, nused, pstart, counts, eord, enext, dst, h2, w_gate[layer], w_up[layer],
                     w_down[layer], nblocks=nblocks, n_tok=t)
    out = _combine(dst, y_buf, x1, wcol, mod3, norm_final[None, :], rows=ROWS_CMB, seq_blocks=seq // ROWS_CMB)
    return dict(out=out.reshape(bt, seq, d), mod=mod, z=z_x, xbc=xbc_x, dt=dt_x, q=q_x, k=k_x, v=v_x,
                k_c=k_c, v_c=v_c, h_cf=h_cf, h_cb=h_cb, ssd=ssd_x, attn=attn_x, x1=x1, h2=h2, ids=ids,
                wcol=wcol, cnt=cnt, y_buf=y_buf, pstart=pstart, q_perm=q_perm)
```

```python
import functools
import math

import numpy as np
import jax
import jax.numpy as jnp
from jax import lax
from jax.experimental import pallas as pl
from jax.experimental.pallas import tpu as pltpu

F32 = jnp.float32
BF16 = jnp.bfloat16

D_MODEL = 1024
GRID_W = 64
EPS = 1e-6
SSD_INNER = 512
SSD_HEAD_DIM = 64
SSD_HEADS = 8
SSD_GROUPS = 2
SSD_STATE = 128
CONV_WIDTH = 5
CONV_DIM = SSD_INNER + 2 * SSD_GROUPS * SSD_STATE
CHUNK = 128
ATTN_WIDTH = 512
HEAD_DIM = 64
ATTN_Q_HEADS = 8
ATTN_KV_HEADS = 2
Q_PER_KV = 4
KV_WIDTH = 128
WINDOW = 128
ROPE_BASE = 10000.0
N_GROUPS = 4
EXPERTS_PER_GROUP = 8
N_EXPERTS = 32
EXPERT_DIM = 512

LANES = 128
SUBLANES = 8
NEG = -1e30
LOG2E = math.log2(math.e)

ROWS_IN = 512
ROWS_SSD = 512
ROWS_ATT = 512
ROUTE_ROWS = 48
MOE_BLK = 256
RANK_BITS = 16
D_TILES = D_MODEL // LANES
ROWS_CMB = 128


def _silu(v):
    half = 0.5 * v
    return half + half * jnp.tanh(half)


def _softplus(v):
    return jnp.maximum(v, 0.0) + jnp.log(1.0 + jnp.exp(-jnp.abs(v)))


def _ada_kernel(c_ref, w_ref, b_ref, o_ref):
    a = _silu(c_ref[...])
    o_ref[...] = jnp.dot(a.astype(BF16), w_ref[...].astype(BF16),
                         preferred_element_type=F32) + b_ref[...]


def _ada(cc, w_ada, b_ada):
    d, n = w_ada.shape
    tn = 1536
    return pl.pallas_call(
        _ada_kernel,
        grid=(n // tn,),
        in_specs=[pl.BlockSpec((SUBLANES, d), lambda j: (0, 0)),
                  pl.BlockSpec((d, tn), lambda j: (0, j)),
                  pl.BlockSpec((1, tn), lambda j: (0, j))],
        out_specs=pl.BlockSpec((SUBLANES, tn), lambda j: (0, j)),
        out_shape=jax.ShapeDtypeStruct((SUBLANES, n), F32),
        name="ada",
    )(cc, w_ada, b_ada)


def _rope(t, cos, sin_signed, first_half):
    rot = jnp.where(first_half, pltpu.roll(t, LANES - HEAD_DIM // 2, 1), pltpu.roll(t, HEAD_DIM // 2, 1))
    return t * cos + rot * sin_signed


def _lane_cumsum(v, lane, reverse):
    k = 1
    while k < LANES:
        if reverse:
            v = v + jnp.where(lane < LANES - k, pltpu.roll(v, LANES - k, 1), 0.0)
        else:
            v = v + jnp.where(lane >= k, pltpu.roll(v, k, 1), 0.0)
        k *= 2
    return v


def _inproj_kernel(*refs, segs, rope):
    x_ref, sh_ref, sc_ref, g_ref, w_ref, wdt_ref, bias_ref, alog_ref = refs[:8]
    pos = 8
    if rope:
        cos_ref, sin_ref = refs[8:10]
        pos = 10
    outs = iter(refs[pos:])
    x = x_ref[...]
    ms = jnp.mean(x * x, axis=-1, keepdims=True)
    h = (x * lax.rsqrt(ms + EPS)) * (g_ref[...] * (1.0 + sc_ref[0])) + sh_ref[0]
    hb = h.astype(BF16)
    if rope:
        cos = cos_ref[...]
        lane = lax.broadcasted_iota(jnp.int32, cos.shape, 1)
        first_half = (lane & (HEAD_DIM - 1)) < HEAD_DIM // 2
        sin_s = jnp.where(first_half, -sin_ref[...], sin_ref[...])
    for kind, c0, width in segs:
        if kind == "dtT":
            dt_ref, pre_ref, suf_ref = next(outs), next(outs), next(outs)
            dtt = lax.dot_general(wdt_ref[...], hb, (((1,), (1,)), ((), ())), preferred_element_type=F32)
            lane16 = lax.broadcasted_iota(jnp.int32, (2 * SSD_HEADS, CHUNK), 1)
            neg_a = -jnp.exp(alog_ref[...])
            for j in range(dt_ref.shape[0]):
                dt = _softplus(dtt[:, j * CHUNK:(j + 1) * CHUNK] + bias_ref[...])
                a = dt * neg_a
                dt_ref[j] = dt
                pre_ref[j] = _lane_cumsum(a, lane16, False)
                suf_ref[j] = _lane_cumsum(a, lane16, True)
            continue
        o_ref = next(outs)
        for j0 in range(0, width, 512):
            wj = min(512, width - j0)
            acc = jnp.dot(hb, w_ref[:, c0 + j0:c0 + j0 + wj], preferred_element_type=F32)
            if rope and kind in ("q", "k"):
                scale = HEAD_DIM ** -0.5 * LOG2E if kind == "q" else 1.0
                for l0 in range(0, wj, LANES):
                    t = _rope(acc[:, l0:l0 + LANES], cos, sin_s, first_half)
                    o_ref[:, j0 + l0:j0 + l0 + LANES] = (t * scale).astype(o_ref.dtype)
            else:
                o_ref[:, j0:j0 + wj] = acc.astype(o_ref.dtype)


_COL_Z, _COL_XBC, _COL_Q, _COL_K, _COL_V = 0, 512, 1536, 2048, 2176
_W_COLS = 2304


def _inproj(xf, mod3, norm1, w_cat, w_dtT, bias_rows, alog_rows, rope_tabs, *, rows, mod_row_fn, segs, seq_blocks):
    t, d = xf.shape
    rope = rope_tabs is not None
    in_specs = [
        pl.BlockSpec((rows, d), lambda i: (i, 0)),
        pl.BlockSpec((1, 1, d), lambda i: (mod_row_fn(i), 0, 0)),
        pl.BlockSpec((1, 1, d), lambda i: (mod_row_fn(i), 0, 1)),
        pl.BlockSpec((1, d), lambda i: (0, 0)),
        pl.BlockSpec(w_cat.shape, lambda i: (0, 0)),
        pl.BlockSpec(w_dtT.shape, lambda i: (0, 0)),
        pl.BlockSpec(bias_rows.shape, lambda i: (0, 0)),
        pl.BlockSpec(alog_rows.shape, lambda i: (0, 0)),
    ]
    args = [xf, mod3, mod3, norm1, w_cat, w_dtT, bias_rows, alog_rows]
    if rope:
        in_specs += [pl.BlockSpec((rows, LANES), lambda i: (i % seq_blocks, 0))] * 2
        args += list(rope_tabs)
    out_specs, out_shapes = [], []
    for kind, _, width in segs:
        if kind == "dtT":
            for _ in range(3):
                out_specs.append(pl.BlockSpec((rows // CHUNK, 2 * SSD_HEADS, CHUNK), lambda i: (i, 0, 0)))
                out_shapes.append(jax.ShapeDtypeStruct((t // CHUNK, 2 * SSD_HEADS, CHUNK), F32))
        else:
            dt = BF16 if kind in ("q", "k", "v") else F32
            out_specs.append(pl.BlockSpec((rows, width), lambda i: (i, 0)))
            out_shapes.append(jax.ShapeDtypeStruct((t, width), dt))
    return pl.pallas_call(
        functools.partial(_inproj_kernel, segs=tuple(segs), rope=rope),
        grid=(t // rows,),
        in_specs=in_specs,
        out_specs=out_specs,
        out_shape=out_shapes,
        name="inproj_rope" if rope else "inproj_ctx",
    )(*args)


_XB_COLS = SSD_INNER + SSD_GROUPS * SSD_STATE
_Y_OFF = CONV_DIM - _XB_COLS


def _ssd_kernel(*refs, need_y, nsteps, cps):
    (xbc_ref, prev_ref, next_ref, dt_ref, pre_ref, suf_ref, z_ref, cw_ref, cb_ref,
     dskip_ref, gain_ref, exp_ref, h0f_ref, h0b_ref) = refs[:14]
    if need_y:
        y_ref, hf_out, hb_out = refs[14:17]
        rest = refs[17:]
    else:
        y_ref = None
        hf_out, hb_out = refs[14:16]
        rest = refs[16:]
    cwin, u_ref, hf, hb, hbs, ust = rest

    sw = pl.program_id(1)
    s = pl.program_id(2)
    bi = jnp.where(sw == 0, nsteps - 1 - s, s)
    rows = xbc_ref.shape[1]
    h = SSD_HEADS

    @pl.when((sw == 0) & (s == 0))
    def _():
        hb[...] = h0b_ref[0]

    @pl.when((sw == 1) & (s == 0))
    def _():
        hf[...] = h0f_ref[0]

    ri = lax.broadcasted_iota(jnp.int32, (CHUNK, CHUNK), 0)
    ci = lax.broadcasted_iota(jnp.int32, (CHUNK, CHUNK), 1)
    lane128 = lax.broadcasted_iota(jnp.int32, (CHUNK, LANES), 1)
    lo_mask = lane128 < SSD_HEAD_DIM

    def conv_silu(c, c0, c1, out_ref):
        r0 = pl.multiple_of(c * CHUNK, CHUNK)
        above = xbc_ref[0, pl.ds(pl.multiple_of(jnp.maximum(r0 - SUBLANES, 0), SUBLANES), SUBLANES), c0:c1]
        halo_a = jnp.where(bi > 0, prev_ref[0, :, c0:c1], 0.0)
        cwin[0:SUBLANES, c0:c1] = jnp.where(c > 0, above, halo_a)
        cwin[SUBLANES:SUBLANES + CHUNK, c0:c1] = xbc_ref[0, pl.ds(r0, CHUNK), c0:c1]
        below = xbc_ref[0, pl.ds(pl.multiple_of(jnp.minimum(r0 + CHUNK, rows - SUBLANES), SUBLANES), SUBLANES), c0:c1]
        halo_b = jnp.where(bi < nsteps - 1, next_ref[0, :, c0:c1], 0.0)
        cwin[SUBLANES + CHUNK:, c0:c1] = jnp.where(c < cps - 1, below, halo_b)
        for j0 in range(c0, c1, LANES):
            acc = jnp.broadcast_to(cb_ref[:, j0:j0 + LANES], (CHUNK, LANES))
            for k in range(CONV_WIDTH):
                off = SUBLANES - CONV_WIDTH // 2 + k
                acc = acc + cwin[off:off + CHUNK, j0:j0 + LANES] * cw_ref[k:k + 1, j0:j0 + LANES]
            out_ref[:, j0 - c0:j0 - c0 + LANES] = _silu(acc)

    def dt_rows(c):
        return dt_ref[c], pre_ref[c], suf_ref[c]

    def state_update(state_ref, xb, w_rows, tot_col):
        dec = jnp.exp(jnp.sum(tot_col * exp_ref[...], axis=0, keepdims=True))
        for g in range(SSD_GROUPS):
            bm = xb[:, SSD_INNER + g * SSD_STATE:SSD_INNER + (g + 1) * SSD_STATE]
            bt = bm.T
            for pr in range(2):
                h0 = g * 4 + pr * 2
                c0 = h0 * SSD_HEAD_DIM
                xp = xb[:, c0:c0 + LANES]
                rhs = jnp.concatenate([jnp.where(lo_mask, xp, 0.0), jnp.where(lo_mask, 0.0, xp)],
                                      axis=0).astype(BF16)
                lhs = jnp.concatenate([bt * w_rows[h0:h0 + 1, :], bt * w_rows[h0 + 1:h0 + 2, :]],
                                      axis=1).astype(BF16)
                sres = jnp.dot(lhs, rhs, preferred_element_type=F32)
                state_ref[:, c0:c0 + LANES] = state_ref[:, c0:c0 + LANES] * dec[:, c0:c0 + LANES] + sres

    def col_forms(rows16):
        padded = jnp.concatenate([rows16, jnp.zeros((CHUNK - 2 * h, LANES), F32)], axis=0)
        return padded.T

    def backward_chunk(k, carry):
        c = cps - 1 - k
        gc = bi * cps + c
        xb = ust.at[gc]
        conv_silu(c, 0, _XB_COLS, xb)
        dt, _, suf = dt_rows(c)
        hbs[gc] = hb[...].astype(BF16)
        s0 = suf[:, 0:1]
        w_rows = jnp.exp(s0 - suf) * dt
        state_update(hb, xb, w_rows[h:2 * h, :], s0[h:2 * h, :])
        return carry

    def forward_chunk(c, carry):
        gc = bi * cps + c
        r0 = pl.multiple_of(c * CHUNK, CHUNK)
        xb = ust.at[gc]
        conv_silu(c, _XB_COLS, CONV_DIM, u_ref)
        dt, pre, suf = dt_rows(c)
        rowsf = jnp.concatenate([pre[0:h, :], suf[h:2 * h, :]], axis=0)
        colsf = col_forms(rowsf)
        ecol = jnp.exp(colsf)
        for g in range(SSD_GROUPS):
            bm = xb[:, SSD_INNER + g * SSD_STATE:SSD_INNER + (g + 1) * SSD_STATE]
            cm = u_ref[:, g * SSD_STATE:(g + 1) * SSD_STATE]
            cmb = cm.astype(BF16)
            cbm = lax.dot_general(cmb, bm.astype(BF16), (((1,), (1,)), ((), ())),
                                  preferred_element_type=F32)
            for pr in range(2):
                h0 = g * 4 + pr * 2
                c0 = h0 * SSD_HEAD_DIM
                lhs_parts = []
                for hh in (h0, h0 + 1):
                    segf = colsf[:, hh:hh + 1] - rowsf[hh:hh + 1, :]
                    gf = jnp.exp(jnp.where(ci <= ri, segf, NEG)) * dt[hh:hh + 1, :]
                    segb = colsf[:, h + hh:h + hh + 1] - rowsf[h + hh:h + hh + 1, :]
                    gb = jnp.exp(jnp.where(ci >= ri, segb, NEG)) * dt[h + hh:h + hh + 1, :]
                    lhs_parts.append((cbm * (gf + gb)).astype(BF16))
                for hh in (h0, h0 + 1):
                    lhs_parts.append((cm * ecol[:, hh:hh + 1]).astype(BF16))
                for hh in (h0, h0 + 1):
                    lhs_parts.append((cm * ecol[:, h + hh:h + hh + 1]).astype(BF16))
                lhs = jnp.concatenate(lhs_parts, axis=1)
                xp = xb[:, c0:c0 + LANES]
                sf = hf[:, c0:c0 + LANES]
                sb = hbs[gc, :, c0:c0 + LANES].astype(F32)
                rhs = jnp.concatenate(
                    [jnp.where(lo_mask, xp, 0.0), jnp.where(lo_mask, 0.0, xp),
                     jnp.where(lo_mask, sf, 0.0), jnp.where(lo_mask, 0.0, sf),
                     jnp.where(lo_mask, sb, 0.0), jnp.where(lo_mask, 0.0, sb)], axis=0).astype(BF16)
                ypair = jnp.dot(lhs, rhs, preferred_element_type=F32)
                ypair = ypair + dskip_ref[:, c0:c0 + LANES] * xp
                zz = z_ref[0, pl.ds(r0, CHUNK), c0:c0 + LANES]
                u_ref[:, _Y_OFF + c0:_Y_OFF + c0 + LANES] = ypair * _silu(zz)
        last = pre[:, LANES - 1:LANES]
        w_rows = jnp.exp(last - pre) * dt
        state_update(hf, xb, w_rows[0:h, :], last[0:h, :])
        if need_y:
            yv = u_ref[:, _Y_OFF:_Y_OFF + SSD_INNER]
            ms = jnp.mean(yv * yv, axis=-1, keepdims=True)
            y_ref[0, pl.ds(r0, CHUNK), :] = (yv * lax.rsqrt(ms + EPS) * gain_ref[...]).astype(y_ref.dtype)
        return carry

    @pl.when(sw == 0)
    def _():
        lax.fori_loop(0, cps, backward_chunk, 0)

    @pl.when(sw == 1)
    def _():
        lax.fori_loop(0, cps, forward_chunk, 0)

    @pl.when((sw == 0) & (s == nsteps - 1))
    def _():
        hb_out[0] = hb[...]

    @pl.when((sw == 1) & (s == nsteps - 1))
    def _():
        hf_out[0] = hf[...]


def _ssd(xbc, dts, z, consts, h0f, h0b, *, rows, need_y):
    bt, seq, _ = xbc.shape
    nsteps = seq // rows
    cps = rows // CHUNK
    hb8 = rows // SUBLANES
    nb8 = seq // SUBLANES
    cw, cb, dskip, gain, expm = consts
    dt_spec = pl.BlockSpec((cps, 2 * SSD_HEADS, CHUNK), lambda b, sw, s: (b * nsteps + blk(b, sw, s), 0, 0))

    def blk(b, sw, s):
        return jnp.where(sw == 0, nsteps - 1 - s, s)

    def full(a):
        return pl.BlockSpec(a.shape, lambda b, sw, s: (0,) * a.ndim)

    in_specs = [
        pl.BlockSpec((1, rows, CONV_DIM), lambda b, sw, s: (b, blk(b, sw, s), 0)),
        pl.BlockSpec((1, SUBLANES, CONV_DIM), lambda b, sw, s: (b, jnp.maximum(blk(b, sw, s) * hb8 - 1, 0), 0)),
        pl.BlockSpec((1, SUBLANES, CONV_DIM),
                     lambda b, sw, s: (b, jnp.minimum((blk(b, sw, s) + 1) * hb8, nb8 - 1), 0)),
        dt_spec, dt_spec, dt_spec,
        pl.BlockSpec((1, rows, SSD_INNER), lambda b, sw, s: (b, jnp.where(sw == 0, 0, s), 0)),
        full(cw), full(cb), full(dskip), full(gain), full(expm),
        pl.BlockSpec((1, SSD_STATE, SSD_INNER), lambda b, sw, s: (b, 0, 0)),
        pl.BlockSpec((1, SSD_STATE, SSD_INNER), lambda b, sw, s: (b, 0, 0)),
    ]
    st_spec = pl.BlockSpec((1, SSD_STATE, SSD_INNER), lambda b, sw, s: (b, 0, 0))
    st_shape = jax.ShapeDtypeStruct((bt, SSD_STATE, SSD_INNER), F32)
    out_specs, out_shapes = [st_spec, st_spec], [st_shape, st_shape]
    if need_y:
        out_specs = [pl.BlockSpec((1, rows, SSD_INNER), lambda b, sw, s: (b, jnp.where(sw == 0, 0, s), 0))] + out_specs
        out_shapes = [jax.ShapeDtypeStruct((bt, seq, SSD_INNER), BF16)] + out_shapes
    scratch = [
        pltpu.VMEM((CHUNK + 2 * SUBLANES, CONV_DIM), F32),
        pltpu.VMEM((CHUNK, _Y_OFF + SSD_INNER), F32),
        pltpu.VMEM((SSD_STATE, SSD_INNER), F32),
        pltpu.VMEM((SSD_STATE, SSD_INNER), F32),
        pltpu.VMEM((seq // CHUNK, SSD_STATE, SSD_INNER), BF16),
        pltpu.VMEM((seq // CHUNK, CHUNK, _XB_COLS), F32),
    ]
    return pl.pallas_call(
        functools.partial(_ssd_kernel, need_y=need_y, nsteps=nsteps, cps=cps),
        grid=(bt, 2, nsteps),
        in_specs=in_specs,
        out_specs=out_specs,
        out_shape=out_shapes,
        scratch_shapes=scratch,
        compiler_params=pltpu.CompilerParams(dimension_semantics=("arbitrary", "arbitrary", "arbitrary")),
        name="ssd_y" if need_y else "ssd_ctx",
    )(xbc, xbc, xbc, *dts, z, cw, cb, dskip, gain, expm, h0f, h0b)


def _attn_kernel(sink_ref, q_ref, kp_ref, kc_ref, kn_ref, vp_ref, vc_ref, vn_ref, kx_ref, vx_ref, o_ref, *, nsteps):
    m = pl.program_id(1)
    lane = lax.broadcasted_iota(jnp.int32, (1, LANES), 1)
    lo = lane < HEAD_DIM
    rq = Q_PER_KV * WINDOW
    per = ROWS_ATT // WINDOW
    qi = lax.broadcasted_iota(jnp.int32, (rq, WINDOW), 0) & (WINDOW - 1)
    kj = lax.broadcasted_iota(jnp.int32, (rq, WINDOW), 1)
    rblk = lax.shift_right_logical(lax.broadcasted_iota(jnp.int32, (rq, 1), 0), int(math.log2(WINDOW)))
    zero = jnp.zeros((), BF16)
    band_prev = jnp.where(kj >= qi, 0.0, NEG)
    band_next = jnp.where(kj <= qi, 0.0, NEG)

    kpieces = [kp_ref[0]] + [kc_ref[0, w * WINDOW:(w + 1) * WINDOW] for w in range(per)] + [kn_ref[0]]
    vpieces = [vp_ref[0]] + [vc_ref[0, w * WINDOW:(w + 1) * WINDOW] for w in range(per)] + [vn_ref[0]]
    kx = kx_ref[0]
    vx = vx_ref[0]
    nt = (((1,), (1,)), ((), ()))

    for sb in range(per):
        qsb = q_ref[0, sb * WINDOW:(sb + 1) * WINDOW, :]
        mask_prev = band_prev if sb > 0 else jnp.where(m > 0, band_prev, NEG)
        mask_next = band_next if sb < per - 1 else jnp.where(m < nsteps - 1, band_next, NEG)
        outs = [None] * Q_PER_KV
        for kv in range(ATTN_KV_HEADS):
            sel = lo if kv == 0 else jnp.logical_not(lo)
            qs = jnp.concatenate(
                [jnp.where(sel, qsb[:, j * LANES:(j + 1) * LANES], zero) for j in range(Q_PER_KV)], axis=0)
            s_prev = lax.dot_general(qs, kpieces[sb], nt, preferred_element_type=F32) + mask_prev
            s_cur = lax.dot_general(qs, kpieces[sb + 1], nt, preferred_element_type=F32)
            s_next = lax.dot_general(qs, kpieces[sb + 2], nt, preferred_element_type=F32) + mask_next
            s_ctx = lax.dot_general(qs, kx, nt, preferred_element_type=F32)
            sink = jnp.zeros((rq, 1), F32)
            for j in range(Q_PER_KV):
                sink = jnp.where(rblk == j, sink_ref[kv * Q_PER_KV + j], sink)
            n_ct = s_ctx.shape[1] // LANES
            tile_max = jnp.maximum(jnp.maximum(s_prev, s_cur), s_next)
            for t in range(n_ct):
                tile_max = jnp.maximum(tile_max, s_ctx[:, t * LANES:(t + 1) * LANES])
            mx = jnp.maximum(jnp.max(tile_max, axis=-1, keepdims=True), sink)
            p_prev = jnp.exp2(s_prev - mx)
            p_cur = jnp.exp2(s_cur - mx)
            p_next = jnp.exp2(s_next - mx)
            p_ctx = jnp.exp2(s_ctx - mx)
            tile_sum = p_prev + p_cur + p_next
            for t in range(n_ct):
                tile_sum = tile_sum + p_ctx[:, t * LANES:(t + 1) * LANES]
            den = jnp.sum(tile_sum, axis=-1, keepdims=True) + jnp.exp2(sink - mx)
            acc = jnp.dot(p_prev.astype(BF16), jnp.where(sel, vpieces[sb], zero), preferred_element_type=F32)
            acc += jnp.dot(p_cur.astype(BF16), jnp.where(sel, vpieces[sb + 1], zero), preferred_element_type=F32)
            acc += jnp.dot(p_next.astype(BF16), jnp.where(sel, vpieces[sb + 2], zero), preferred_element_type=F32)
            acc += jnp.dot(p_ctx.astype(BF16), jnp.where(sel, vx, zero), preferred_element_type=F32)
            acc = acc * (1.0 / den)
            for j in range(Q_PER_KV):
                part = acc[j * WINDOW:(j + 1) * WINDOW, :]
                outs[j] = part if outs[j] is None else outs[j] + part
        for j in range(Q_PER_KV):
            o_ref[0, sb * WINDOW:(sb + 1) * WINDOW, j * LANES:(j + 1) * LANES] = outs[j].astype(o_ref.dtype)


def _attention(sinks, q, k, v, kx, vx):
    bt, seq, _ = q.shape
    nsteps = seq // ROWS_ATT
    per = ROWS_ATT // WINDOW
    nb = seq // WINDOW
    n_ctx = kx.shape[1]

    def prev(b, m, s):
        return (b, jnp.maximum(m * per - 1, 0), 0)

    def cur(b, m, s):
        return (b, m, 0)

    def nxt(b, m, s):
        return (b, jnp.minimum((m + 1) * per, nb - 1), 0)

    small = (1, WINDOW, KV_WIDTH)
    big = (1, ROWS_ATT, KV_WIDTH)
    grid_spec = pltpu.PrefetchScalarGridSpec(
        num_scalar_prefetch=1,
        grid=(bt, nsteps),
        in_specs=[
            pl.BlockSpec((1, ROWS_ATT, ATTN_WIDTH), cur),
            pl.BlockSpec(small, prev), pl.BlockSpec(big, cur), pl.BlockSpec(small, nxt),
            pl.BlockSpec(small, prev), pl.BlockSpec(big, cur), pl.BlockSpec(small, nxt),
            pl.BlockSpec((1, n_ctx, KV_WIDTH), lambda b, m, s: (b, 0, 0)),
            pl.BlockSpec((1, n_ctx, KV_WIDTH), lambda b, m, s: (b, 0, 0)),
        ],
        out_specs=pl.BlockSpec((1, ROWS_ATT, ATTN_WIDTH), cur),
    )
    return pl.pallas_call(
        functools.partial(_attn_kernel, nsteps=nsteps),
        grid_spec=grid_spec,
        out_shape=jax.ShapeDtypeStruct((bt, seq, ATTN_WIDTH), BF16),
        name="attn",
    )(sinks, q, k, k, k, v, v, v, kx, vx)


def _outproj_kernel(ssd_ref, att_ref, x_ref, g1_ref, sh_ref, sc_ref, n2_ref, wa_ref, wb_ref, wr_ref, br_ref,
                    tri_ref, x1_ref, h2_ref, ids_ref, wcol_ref, cnt_ref, carry, *, nsteps):
    i = pl.program_id(0)
    rows = x_ref.shape[0]

    @pl.when(i == 0)
    def _():
        carry[...] = jnp.zeros_like(carry)

    acc = jnp.dot(ssd_ref[...], wa_ref[...], preferred_element_type=F32)
    acc += jnp.dot(att_ref[...], wb_ref[...], preferred_element_type=F32)
    x1 = x_ref[...] + g1_ref[0] * acc
    x1_ref[...] = x1
    ms = jnp.mean(x1 * x1, axis=-1, keepdims=True)
    h2 = (x1 * lax.rsqrt(ms + EPS)) * (n2_ref[...] * (1.0 + sc_ref[0])) + sh_ref[0]
    h2b = h2.astype(BF16)
    half = h2.shape[1] // 2
    lo_bits = lax.shift_right_logical(pltpu.bitcast(h2b[:, :half].astype(F32), jnp.uint32), jnp.uint32(16))
    hi_bits = pltpu.bitcast(h2b[:, half:].astype(F32), jnp.uint32) & jnp.uint32(0xFFFF0000)
    h2_ref[...] = hi_bits | lo_bits
    lt = lax.dot_general(wr_ref[...], h2b, (((1,), (1,)), ((), ())),
                         preferred_element_type=F32) + br_ref[...]
    row = lax.broadcasted_iota(jnp.int32, lt.shape, 0).astype(F32)
    big_i = float(ROUTE_ROWS)
    gl = jnp.where(row < N_GROUPS, lt, NEG)
    gmax = jnp.max(gl, axis=0, keepdims=True)
    gidx = jnp.min(jnp.where(gl == gmax, row, big_i), axis=0, keepdims=True)
    g_w = 1.0 / jnp.sum(jnp.exp(gl - gmax), axis=0, keepdims=True)
    lo = N_GROUPS + EXPERTS_PER_GROUP * gidx
    el = jnp.where((row >= lo) & (row < lo + EXPERTS_PER_GROUP), lt, NEG)
    m1 = jnp.max(el, axis=0, keepdims=True)
    i1 = jnp.min(jnp.where(el == m1, row, big_i), axis=0, keepdims=True)
    el2 = jnp.where(row == i1, NEG, el)
    m2 = jnp.max(el2, axis=0, keepdims=True)
    i2 = jnp.min(jnp.where(el2 == m2, row, big_i), axis=0, keepdims=True)
    r = jnp.exp(m2 - m1)
    w1 = g_w / (1.0 + r)
    w2 = g_w * r / (1.0 + r)
    oh1 = row == i1
    oh2 = row == i2
    cnt = jnp.where(oh1 | oh2, 1.0, 0.0)
    prefix = jnp.dot(cnt.astype(BF16), tri_ref[...], preferred_element_type=F32)
    base = carry[...] + prefix
    rank1 = jnp.sum(jnp.where(oh1, base, 0.0), axis=0, keepdims=True)
    rank2 = jnp.sum(jnp.where(oh2, base, 0.0), axis=0, keepdims=True)
    carry[...] = carry[...] + jnp.sum(cnt, axis=1, keepdims=True)
    r8 = lax.broadcasted_iota(jnp.int32, (SUBLANES, rows), 0)
    code1 = (i1 - N_GROUPS) * float(1 << RANK_BITS) + rank1
    code2 = (i2 - N_GROUPS) * float(1 << RANK_BITS) + rank2
    ids_ref[...] = jnp.where(r8 == 0, code1, jnp.where(r8 == 1, code2, 0.0)).astype(jnp.int32)
    r128 = lax.broadcasted_iota(jnp.int32, (LANES, rows), 0)
    wfull = jnp.where(r128 == 0, w1, jnp.where(r128 == 1, w2, 0.0))
    for j in range(rows // LANES):
        wcol_ref[j * LANES:(j + 1) * LANES, :] = wfull[:, j * LANES:(j + 1) * LANES].T

    @pl.when(i == nsteps - 1)
    def _():
        cnt_ref[...] = carry[:, 0:LANES]


def _outproj(ssd, att, xf, mod3, norm2, wa, wb, wr, br, tri, *, rows, seq_blocks):
    t, d = xf.shape
    nsteps = t // rows

    def mod(col):
        return pl.BlockSpec((1, 1, d), lambda i: (i // seq_blocks, 0, col))

    def full(a):
        return pl.BlockSpec(a.shape, lambda i: (0,) * a.ndim)

    return pl.pallas_call(
        functools.partial(_outproj_kernel, nsteps=nsteps),
        grid=(nsteps,),
        in_specs=[pl.BlockSpec((rows, SSD_INNER), lambda i: (i, 0)),
                  pl.BlockSpec((rows, ATTN_WIDTH), lambda i: (i, 0)),
                  pl.BlockSpec((rows, d), lambda i: (i, 0)),
                  mod(2), mod(3), mod(4), full(norm2), full(wa), full(wb), full(wr), full(br), full(tri)],
        out_specs=[pl.BlockSpec((rows, d), lambda i: (i, 0)),
                   pl.BlockSpec((rows, d // 2), lambda i: (i, 0)),
                   pl.BlockSpec((SUBLANES, rows), lambda i: (0, i)),
                   pl.BlockSpec((rows, LANES), lambda i: (i, 0)),
                   pl.BlockSpec((ROUTE_ROWS, LANES), lambda i: (0, 0))],
        out_shape=[jax.ShapeDtypeStruct((t, d), F32),
                   jax.ShapeDtypeStruct((t, d // 2), jnp.uint32),
                   jax.ShapeDtypeStruct((SUBLANES, t), jnp.int32),
                   jax.ShapeDtypeStruct((t, LANES), F32),
                   jax.ShapeDtypeStruct((ROUTE_ROWS, LANES), F32)],
        scratch_shapes=[pltpu.VMEM((ROUTE_ROWS, rows), F32)],
        compiler_params=pltpu.CompilerParams(dimension_semantics=("arbitrary",)),
        name="outproj_router",
    )(ssd, att, xf, mod3, mod3, mod3, norm2, wa, wb, wr, br, tri)


def _expert_kernel(blk_e_ref, nused_ref, pstart_ref, counts_ref, eord_ref, enext_ref, dst_ref,
                   h2_hbm, wg_hbm, wu_hbm, wd_hbm, y_ref,
                   h2v, xg, wgs, wus, wds, wgb, wub, wdb, tok_ref, sem, wsem, *, n_tok, nblocks):
    b = pl.program_id(0)
    nused = nused_ref[0]
    slot = b % 2
    half = h2v.shape[1]

    def h2_copy():
        return pltpu.make_async_copy(h2_hbm, h2v, sem.at[0])

    def weight_copies(e, sl):
        return (pltpu.make_async_copy(wg_hbm.at[e], wgs.at[sl], wsem.at[sl, 0]),
                pltpu.make_async_copy(wu_hbm.at[e], wus.at[sl], wsem.at[sl, 1]),
                pltpu.make_async_copy(wd_hbm.at[e], wds.at[sl], wsem.at[sl, 2]))

    def gather_rows(blk, sl):
        for r in range(MOE_BLK):
            xg[sl, pl.ds(r, 1), :] = h2v[pl.ds(tok_ref[blk * MOE_BLK + r], 1), :]

    @pl.when(b == 0)
    def _():
        h2_copy().start()
        for cp in weight_copies(blk_e_ref[0], 0):
            cp.start()
        def pad_expert(e, carry):
            lo = pstart_ref[e] + counts_ref[e]
            hi = pstart_ref[e] + (counts_ref[e] + MOE_BLK - 1) // MOE_BLK * MOE_BLK

            def pad_slot(s, c):
                tok_ref[s] = 0
                return c
            return lax.fori_loop(lo, hi, pad_slot, carry)
        lax.fori_loop(0, N_EXPERTS, pad_expert, 0)

        for k in range(2):
            def claim(t, carry, k=k):
                tok_ref[dst_ref[k * n_tok + t]] = t
                return carry
            lax.fori_loop(0, n_tok, claim, 0, unroll=16)
        h2_copy().wait()
        gather_rows(0, 0)

    e = blk_e_ref[b]
    e_prev = blk_e_ref[jnp.maximum(b - 1, 0)]

    @pl.when((b == 0) | (e != e_prev))
    def _():
        sl = eord_ref[e] % 2
        for cp in weight_copies(e, sl):
            cp.wait()
        wgb[...] = wgs[sl].astype(BF16)
        wub[...] = wus[sl].astype(BF16)
        wdb[...] = wds[sl].astype(BF16)
        e_next = enext_ref[e]

        @pl.when(e_next >= 0)
        def _():
            for cp in weight_copies(e_next, 1 - sl):
                cp.start()

    @pl.when(b < nused)
    def _():
        words = xg[slot]
        x_lo = pltpu.bitcast(words << jnp.uint32(16), F32).astype(BF16)
        x_hi = pltpu.bitcast(words & jnp.uint32(0xFFFF0000), F32).astype(BF16)
        gate = (jnp.dot(x_lo, wgb[0:half, :], preferred_element_type=F32)
                + jnp.dot(x_hi, wgb[half:, :], preferred_element_type=F32))
        up = (jnp.dot(x_lo, wub[0:half, :], preferred_element_type=F32)
              + jnp.dot(x_hi, wub[half:, :], preferred_element_type=F32))
        hmid = (_silu(gate) * up).astype(BF16)
        y = jnp.dot(hmid, wdb[...], preferred_element_type=F32)
        for s in range(D_TILES):
            y_ref[pl.ds(s, MOE_BLK, stride=D_TILES), :] = y[:, s * LANES:(s + 1) * LANES]
        gather_rows(jnp.minimum(b + 1, nused - 1), 1 - slot)

    @pl.when(b >= nused)
    def _():
        y_ref[...] = jnp.zeros_like(y_ref)


def _experts(blk_e, nused, pstart, counts, eord, enext, dst, h2p, w_gate, w_up, w_down, *, nblocks, n_tok):
    d = w_gate.shape[1]
    hbm = pl.BlockSpec(memory_space=pl.ANY)
    grid_spec = pltpu.PrefetchScalarGridSpec(
        num_scalar_prefetch=7,
        grid=(nblocks,),
        in_specs=[hbm, hbm, hbm, hbm],
        out_specs=pl.BlockSpec((MOE_BLK * D_TILES, LANES), lambda b, *_: (b, 0)),
        scratch_shapes=[
            pltpu.VMEM(h2p.shape, jnp.uint32),
            pltpu.VMEM((2, MOE_BLK, d // 2), jnp.uint32),
            pltpu.VMEM((2, d, EXPERT_DIM), F32),
            pltpu.VMEM((2, d, EXPERT_DIM), F32),
            pltpu.VMEM((2, EXPERT_DIM, d), F32),
            pltpu.VMEM((d, EXPERT_DIM), BF16),
            pltpu.VMEM((d, EXPERT_DIM), BF16),
            pltpu.VMEM((EXPERT_DIM, d), BF16),
            pltpu.SMEM((nblocks * MOE_BLK,), jnp.int32),
            pltpu.SemaphoreType.DMA((1,)),
            pltpu.SemaphoreType.DMA((2, 3)),
        ],
    )
    return pl.pallas_call(
        functools.partial(_expert_kernel, n_tok=n_tok, nblocks=nblocks),
        grid_spec=grid_spec,
        out_shape=jax.ShapeDtypeStruct((nblocks * MOE_BLK * D_TILES, LANES), F32),
        compiler_params=pltpu.CompilerParams(dimension_semantics=("arbitrary",)),
        name="experts",
    )(blk_e, nused, pstart, counts, eord, enext, dst, h2p, w_gate, w_up, w_down)


def _combine_kernel(dst_ref, y_hbm, x1_ref, wcol_ref, g2_ref, nf_ref, o_ref, ybuf, sem, *, nsteps, n_tok):
    i = pl.program_id(0)
    rows = x1_ref.shape[0]
    slot = i % 2

    def row_copy(src, r, k, sl):
        return pltpu.make_async_copy(y_hbm.at[pl.ds(src * D_TILES, D_TILES)],
                                     ybuf.at[sl, k, pl.ds(r * D_TILES, D_TILES)], sem.at[sl])

    def start_rows(step, sl, r0, r1):
        for r in range(r0, r1):
            for k in range(2):
                row_copy(dst_ref[k * n_tok + step * rows + r], r, k, sl).start()

    def wait_rows(sl):
        for r in range(rows):
            for k in range(2):
                row_copy(0, r, k, sl).wait()

    @pl.when(i == 0)
    def _():
        start_rows(0, 0, 0, rows)

    @pl.when(i + 1 < nsteps)
    def _():
        start_rows(i + 1, 1 - slot, 0, rows)

    wait_rows(slot)
    w0 = wcol_ref[:, 0:1]
    w1 = wcol_ref[:, 1:2]
    ssq = jnp.zeros((rows, 1), F32)
    for s in range(D_TILES):
        cols = slice(s * LANES, (s + 1) * LANES)
        moe = (ybuf[slot, 0, pl.ds(s, rows, stride=D_TILES), :] * w0
               + ybuf[slot, 1, pl.ds(s, rows, stride=D_TILES), :] * w1)
        x2 = x1_ref[:, cols] + g2_ref[0, :, cols] * moe
        ssq = ssq + jnp.sum(x2 * x2, axis=-1, keepdims=True)
        o_ref[:, cols] = x2
    inv = lax.rsqrt(ssq * (1.0 / (D_TILES * LANES)) + EPS)
    o_ref[...] = o_ref[...] * inv * nf_ref[...]


def _combine(dst, y_buf, x1, wcol, mod3, norm_final, *, rows, seq_blocks):
    t, d = x1.shape
    nsteps = t // rows
    grid_spec = pltpu.PrefetchScalarGridSpec(
        num_scalar_prefetch=1,
        grid=(nsteps,),
        in_specs=[
            pl.BlockSpec(memory_space=pl.ANY),
            pl.BlockSpec((rows, d), lambda i, ds: (i, 0)),
            pl.BlockSpec((rows, LANES), lambda i, ds: (i, 0)),
            pl.BlockSpec((1, 1, d), lambda i, ds: (i // seq_blocks, 0, 5)),
            pl.BlockSpec((1, d), lambda i, ds: (0, 0)),
        ],
        out_specs=pl.BlockSpec((rows, d), lambda i, ds: (i, 0)),
        scratch_shapes=[pltpu.VMEM((2, 2, rows * D_TILES, LANES), F32), pltpu.SemaphoreType.DMA((2,))],
    )
    return pl.pallas_call(
        functools.partial(_combine_kernel, nsteps=nsteps, n_tok=t),
        grid_spec=grid_spec,
        out_shape=jax.ShapeDtypeStruct((t, d), F32),
        compiler_params=pltpu.CompilerParams(dimension_semantics=("arbitrary",)),
        name="combine",
    )(dst, y_buf, x1, wcol, mod3, norm_final)


def _rope_tables(seq):
    pos = np.arange(seq)
    n_freq = HEAD_DIM // 4
    inv = ROPE_BASE ** (-np.arange(n_freq, dtype=np.float32) / n_freq)
    ang = np.concatenate([(pos // GRID_W)[:, None] * inv, (pos % GRID_W)[:, None] * inv], axis=-1)
    ang = np.concatenate([ang, ang, ang, ang], axis=-1).astype(np.float32)
    return jnp.asarray(np.cos(ang), F32), jnp.asarray(np.sin(ang), F32)


def _head_expand_mat():
    m = np.zeros((SSD_HEADS, SSD_INNER), np.float32)
    for hh in range(SSD_HEADS):
        m[hh, hh * SSD_HEAD_DIM:(hh + 1) * SSD_HEAD_DIM] = 1.0
    return jnp.asarray(m)


def kernel(x, c, ctx, c_ctx, w_ada, b_ada, norm1, w_in, conv_w, conv_b, dt_bias, a_log, d_skip, ssd_norm,
           attn_sinks, w_out, norm2, w_group, b_group, w_expert, b_expert, w_gate, w_up, w_down, norm_final):
    return _pipeline(x, c, ctx, c_ctx, w_ada, b_ada, norm1, w_in, conv_w, conv_b, dt_bias, a_log, d_skip,
                     ssd_norm, attn_sinks, w_out, norm2, w_group, b_group, w_expert, b_expert, w_gate, w_up,
                     w_down, norm_final)["out"]


def _pipeline(x, c, ctx, c_ctx, w_ada, b_ada, norm1, w_in, conv_w, conv_b, dt_bias, a_log, d_skip, ssd_norm,
              attn_sinks, w_out, norm2, w_group, b_group, w_expert, b_expert, w_gate, w_up, w_down, norm_final):
    bt, seq, d = x.shape
    n_ctx = ctx.shape[1]
    t = bt * seq
    layer = 0
    assert w_ada.shape[0] == 1 and seq % ROWS_IN == 0 and n_ctx % CHUNK == 0

    cc = jnp.zeros((SUBLANES, d), F32).at[:bt].set(c).at[bt].set(c_ctx)
    mod = _ada(cc, w_ada[layer], b_ada[layer][None, :])
    mod3 = mod.reshape(SUBLANES, 1, 6 * d)

    w = w_in[layer]
    o_z, o_xbc, o_dt = 0, SSD_INNER, SSD_INNER + CONV_DIM
    o_q = o_dt + 2 * SSD_HEADS
    o_k, o_v = o_q + ATTN_WIDTH, o_q + ATTN_WIDTH + KV_WIDTH
    q_perm = np.concatenate([np.r_[j * HEAD_DIM:(j + 1) * HEAD_DIM, (j + 4) * HEAD_DIM:(j + 5) * HEAD_DIM]
                             for j in range(Q_PER_KV)])
    w_q = w[:, o_q:o_q + ATTN_WIDTH].reshape(d, ATTN_KV_HEADS, Q_PER_KV, HEAD_DIM).transpose(0, 2, 1, 3)
    w_q = w_q.reshape(d, ATTN_WIDTH)
    w_cat = jnp.concatenate([w[:, o_z:o_z + SSD_INNER], w[:, o_xbc:o_xbc + CONV_DIM], w_q,
                             w[:, o_k:o_k + KV_WIDTH], w[:, o_v:o_v + KV_WIDTH]], axis=1).astype(BF16)
    w_dtT = w[:, o_dt:o_dt + 2 * SSD_HEADS].T.astype(BF16)

    seq_blocks = seq // ROWS_IN
    cos, sin = _rope_tables(seq)
    bias_rows = jnp.broadcast_to(dt_bias[layer].reshape(2 * SSD_HEADS, 1), (2 * SSD_HEADS, LANES))
    alog_rows = jnp.broadcast_to(a_log[layer].reshape(2 * SSD_HEADS, 1), (2 * SSD_HEADS, LANES))
    segs_x = [("z", _COL_Z, SSD_INNER), ("xbc", _COL_XBC, CONV_DIM), ("dtT", 0, 0),
              ("q", _COL_Q, ATTN_WIDTH), ("k", _COL_K, KV_WIDTH), ("v", _COL_V, KV_WIDTH)]
    z_x, xbc_x, dt_x, pre_x, suf_x, q_x, k_x, v_x = _inproj(
        x.reshape(t, d), mod3, norm1[layer][None, :], w_cat, w_dtT, bias_rows, alog_rows, (cos, sin),
        rows=ROWS_IN, mod_row_fn=lambda i: i // seq_blocks, segs=segs_x, seq_blocks=seq_blocks)
    segs_c = [("xbc", _COL_XBC, CONV_DIM), ("dtT", 0, 0), ("k", _COL_K, KV_WIDTH), ("v", _COL_V, KV_WIDTH)]
    xbc_c, dt_c, pre_c, suf_c, k_c, v_c = _inproj(
        ctx.reshape(bt * n_ctx, d), mod3, norm1[layer][None, :], w_cat, w_dtT, bias_rows, alog_rows, None,
        rows=n_ctx, mod_row_fn=lambda i: bt, segs=segs_c, seq_blocks=1)

    cw = jnp.zeros((SUBLANES, CONV_DIM), F32).at[:CONV_WIDTH].set(conv_w[layer])
    cb = conv_b[layer][None, :]
    dskip_e = jnp.repeat(d_skip[layer], SSD_HEAD_DIM)[None, :]
    consts = (cw, cb, dskip_e, ssd_norm[layer][None, :], _head_expand_mat())

    zeros_state = jnp.zeros((bt, SSD_STATE, SSD_INNER), F32)
    z_dummy = jnp.zeros((bt, n_ctx, SSD_INNER), F32)
    h_cf, h_cb = _ssd(xbc_c.reshape(bt, n_ctx, CONV_DIM), (dt_c, pre_c, suf_c), z_dummy, consts,
                      zeros_state, zeros_state, rows=n_ctx, need_y=False)
    ssd_x, _, _ = _ssd(xbc_x.reshape(bt, seq, CONV_DIM), (dt_x, pre_x, suf_x), z_x.reshape(bt, seq, SSD_INNER),
                       consts, h_cf, h_cb, rows=ROWS_SSD, need_y=True)

    attn_x = _attention(attn_sinks[layer].astype(F32) * LOG2E,
                        q_x.reshape(bt, seq, ATTN_WIDTH), k_x.reshape(bt, seq, KV_WIDTH),
                        v_x.reshape(bt, seq, KV_WIDTH), k_c.reshape(bt, n_ctx, KV_WIDTH),
                        v_c.reshape(bt, n_ctx, KV_WIDTH))

    wo = w_out[layer]
    wa = wo[:SSD_INNER].astype(BF16)
    wb = wo[SSD_INNER:].reshape(ATTN_KV_HEADS, Q_PER_KV, HEAD_DIM, d).transpose(1, 0, 2, 3)
    wb = wb.reshape(ATTN_WIDTH, d).astype(BF16)
    wr = jnp.zeros((ROUTE_ROWS, d), F32).at[:N_GROUPS].set(w_group[layer].T)
    wr = wr.at[N_GROUPS:N_GROUPS + N_EXPERTS].set(w_expert[layer].T).astype(BF16)
    br = jnp.zeros((ROUTE_ROWS,), F32).at[:N_GROUPS].set(b_group[layer])
    br = br.at[N_GROUPS:N_GROUPS + N_EXPERTS].set(b_expert[layer])
    br = jnp.broadcast_to(br[:, None], (ROUTE_ROWS, ROWS_IN))
    tri = jnp.asarray(np.triu(np.ones((ROWS_IN, ROWS_IN), np.float32), 1), BF16)
    x1, h2, ids, wcol, cnt = _outproj(
        ssd_x.reshape(t, SSD_INNER), attn_x.reshape(t, ATTN_WIDTH), x.reshape(t, d), mod3, norm2[layer][None, :],
        wa, wb, wr, br, tri, rows=ROWS_IN, seq_blocks=seq_blocks)

    counts = cnt[N_GROUPS:N_GROUPS + N_EXPERTS, 0].astype(jnp.int32)
    padded = (counts + MOE_BLK - 1) // MOE_BLK * MOE_BLK
    ends = jnp.cumsum(padded)
    pstart = ends - padded
    nblocks = (2 * t) // MOE_BLK + N_EXPERTS
    blk_start = jnp.arange(nblocks, dtype=jnp.int32) * MOE_BLK
    eids = jnp.arange(N_EXPERTS, dtype=jnp.int32)
    active = counts > 0
    last_active = jnp.max(jnp.where(active, eids, 0))
    blk_e = jnp.minimum(jnp.sum((ends[None, :] <= blk_start[:, None]).astype(jnp.int32), axis=1), last_active)
    nused = (ends[-1] // MOE_BLK).astype(jnp.int32).reshape(1)
    eord = jnp.cumsum(active.astype(jnp.int32)) - active.astype(jnp.int32)
    later = active[None, :] & (eids[None, :] > eids[:, None])
    enext = jnp.min(jnp.where(later, eids[None, :], N_EXPERTS), axis=1)
    enext = jnp.where(enext == N_EXPERTS, -1, enext).astype(jnp.int32)
    code = ids[0:2].reshape(-1)
    e_id = lax.shift_right_logical(code, RANK_BITS)
    onehot = e_id[:, None] == jnp.arange(N_EXPERTS, dtype=jnp.int32)[None, :]
    dst = jnp.sum(jnp.where(onehot, pstart[None, :], 0), axis=1) + (code & ((1 << RANK_BITS) - 1))

    y_buf = _experts(blk_e, nused, pstart, counts, eord, enext, dst, h2, w_gate[layer], w_up[layer],
                     w_down[layer], nblocks=nblocks, n_tok=t)
    out = _combine(dst, y_buf, x1, wcol, mod3, norm_final[None, :], rows=ROWS_CMB, seq_blocks=seq // ROWS_CMB)
    return dict(out=out.reshape(bt, seq, d), mod=mod, z=z_x, xbc=xbc_x, dt=dt_x, q=q_x, k=k_x, v=v_x,
                k_c=k_c, v_c=v_c, h_cf=h_cf, h_cb=h_cb, ssd=ssd_x, attn=attn_x, x1=x1, h2=h2, ids=ids,
                wcol=wcol, cnt=cnt, y_buf=y_buf, pstart=pstart, q_perm=q_perm)
```

```python
import functools
import math

import numpy as np
import jax
import jax.numpy as jnp
from jax import lax
from jax.experimental import pallas as pl
from jax.experimental.pallas import tpu as pltpu

F32 = jnp.float32
BF16 = jnp.bfloat16

D_MODEL = 1024
GRID_W = 64
EPS = 1e-6
SSD_INNER = 512
SSD_HEAD_DIM = 64
SSD_HEADS = 8
SSD_GROUPS = 2
SSD_STATE = 128
CONV_WIDTH = 5
CONV_DIM = SSD_INNER + 2 * SSD_GROUPS * SSD_STATE
CHUNK = 128
ATTN_WIDTH = 512
HEAD_DIM = 64
ATTN_Q_HEADS = 8
ATTN_KV_HEADS = 2
Q_PER_KV = 4
KV_WIDTH = 128
WINDOW = 128
ROPE_BASE = 10000.0
N_GROUPS = 4
EXPERTS_PER_GROUP = 8
N_EXPERTS = 32
EXPERT_DIM = 512

LANES = 128
SUBLANES = 8
NEG = -1e30
LOG2E = math.log2(math.e)

ROWS_IN = 512
ROWS_SSD = 512
ROWS_ATT = 512
ROUTE_ROWS = 48
MOE_BLK = 256
RANK_BITS = 16
D_TILES = D_MODEL // LANES
ROWS_CMB = 256


def _silu(v):
    half = 0.5 * v
    return half + half * jnp.tanh(half)


def _softplus(v):
    return jnp.maximum(v, 0.0) + jnp.log(1.0 + jnp.exp(-jnp.abs(v)))


def _ada_kernel(c_ref, w_ref, b_ref, o_ref):
    a = _silu(c_ref[...])
    o_ref[...] = jnp.dot(a.astype(BF16), w_ref[...].astype(BF16),
                         preferred_element_type=F32) + b_ref[...]


def _ada(cc, w_ada, b_ada):
    d, n = w_ada.shape
    tn = 1536
    return pl.pallas_call(
        _ada_kernel,
        grid=(n // tn,),
        in_specs=[pl.BlockSpec((SUBLANES, d), lambda j: (0, 0)),
                  pl.BlockSpec((d, tn), lambda j: (0, j)),
                  pl.BlockSpec((1, tn), lambda j: (0, j))],
        out_specs=pl.BlockSpec((SUBLANES, tn), lambda j: (0, j)),
        out_shape=jax.ShapeDtypeStruct((SUBLANES, n), F32),
        name="ada",
    )(cc, w_ada, b_ada)


def _rope(t, cos, sin_signed, first_half):
    rot = jnp.where(first_half, pltpu.roll(t, LANES - HEAD_DIM // 2, 1), pltpu.roll(t, HEAD_DIM // 2, 1))
    return t * cos + rot * sin_signed


def _lane_cumsum(v, lane, reverse):
    k = 1
    while k < LANES:
        if reverse:
            v = v + jnp.where(lane < LANES - k, pltpu.roll(v, LANES - k, 1), 0.0)
        else:
            v = v + jnp.where(lane >= k, pltpu.roll(v, k, 1), 0.0)
        k *= 2
    return v


def _inproj_kernel(*refs, segs, rope):
    x_ref, sh_ref, sc_ref, g_ref, w_ref, wdt_ref, bias_ref, alog_ref = refs[:8]
    pos = 8
    if rope:
        cos_ref, sin_ref = refs[8:10]
        pos = 10
    outs = iter(refs[pos:])
    x = x_ref[...]
    ms = jnp.mean(x * x, axis=-1, keepdims=True)
    h = (x * lax.rsqrt(ms + EPS)) * (g_ref[...] * (1.0 + sc_ref[0])) + sh_ref[0]
    hb = h.astype(BF16)
    if rope:
        cos = cos_ref[...]
        lane = lax.broadcasted_iota(jnp.int32, cos.shape, 1)
        first_half = (lane & (HEAD_DIM - 1)) < HEAD_DIM // 2
        sin_s = jnp.where(first_half, -sin_ref[...], sin_ref[...])
    for kind, c0, width in segs:
        if kind == "dtT":
            dt_ref, pre_ref, suf_ref = next(outs), next(outs), next(outs)
            dtt = lax.dot_general(wdt_ref[...], hb, (((1,), (1,)), ((), ())), preferred_element_type=F32)
            lane16 = lax.broadcasted_iota(jnp.int32, (2 * SSD_HEADS, CHUNK), 1)
            neg_a = -jnp.exp(alog_ref[...])
            for j in range(dt_ref.shape[0]):
                dt = _softplus(dtt[:, j * CHUNK:(j + 1) * CHUNK] + bias_ref[...])
                a = dt * neg_a
                dt_ref[j] = dt
                pre_ref[j] = _lane_cumsum(a, lane16, False)
                suf_ref[j] = _lane_cumsum(a, lane16, True)
            continue
        o_ref = next(outs)
        for j0 in range(0, width, 512):
            wj = min(512, width - j0)
            acc = jnp.dot(hb, w_ref[:, c0 + j0:c0 + j0 + wj], preferred_element_type=F32)
            if rope and kind in ("q", "k"):
                scale = HEAD_DIM ** -0.5 * LOG2E if kind == "q" else 1.0
                for l0 in range(0, wj, LANES):
                    t = _rope(acc[:, l0:l0 + LANES], cos, sin_s, first_half)
                    o_ref[:, j0 + l0:j0 + l0 + LANES] = (t * scale).astype(o_ref.dtype)
            else:
                o_ref[:, j0:j0 + wj] = acc.astype(o_ref.dtype)


_COL_Z, _COL_XBC, _COL_Q, _COL_K, _COL_V = 0, 512, 1536, 2048, 2176
_W_COLS = 2304


def _inproj(xf, mod3, norm1, w_cat, w_dtT, bias_rows, alog_rows, rope_tabs, *, rows, mod_row_fn, segs, seq_blocks):
    t, d = xf.shape
    rope = rope_tabs is not None
    in_specs = [
        pl.BlockSpec((rows, d), lambda i: (i, 0)),
        pl.BlockSpec((1, 1, d), lambda i: (mod_row_fn(i), 0, 0)),
        pl.BlockSpec((1, 1, d), lambda i: (mod_row_fn(i), 0, 1)),
        pl.BlockSpec((1, d), lambda i: (0, 0)),
        pl.BlockSpec(w_cat.shape, lambda i: (0, 0)),
        pl.BlockSpec(w_dtT.shape, lambda i: (0, 0)),
        pl.BlockSpec(bias_rows.shape, lambda i: (0, 0)),
        pl.BlockSpec(alog_rows.shape, lambda i: (0, 0)),
    ]
    args = [xf, mod3, mod3, norm1, w_cat, w_dtT, bias_rows, alog_rows]
    if rope:
        in_specs += [pl.BlockSpec((rows, LANES), lambda i: (i % seq_blocks, 0))] * 2
        args += list(rope_tabs)
    out_specs, out_shapes = [], []
    for kind, _, width in segs:
        if kind == "dtT":
            for _ in range(3):
                out_specs.append(pl.BlockSpec((rows // CHUNK, 2 * SSD_HEADS, CHUNK), lambda i: (i, 0, 0)))
                out_shapes.append(jax.ShapeDtypeStruct((t // CHUNK, 2 * SSD_HEADS, CHUNK), F32))
        else:
            dt = BF16 if kind in ("q", "k", "v") else F32
            out_specs.append(pl.BlockSpec((rows, width), lambda i: (i, 0)))
            out_shapes.append(jax.ShapeDtypeStruct((t, width), dt))
    return pl.pallas_call(
        functools.partial(_inproj_kernel, segs=tuple(segs), rope=rope),
        grid=(t // rows,),
        in_specs=in_specs,
        out_specs=out_specs,
        out_shape=out_shapes,
        name="inproj_rope" if rope else "inproj_ctx",
    )(*args)


_XB_COLS = SSD_INNER + SSD_GROUPS * SSD_STATE
_Y_OFF = CONV_DIM - _XB_COLS


def _ssd_kernel(*refs, need_y, nsteps, cps):
    (xbc_ref, prev_ref, next_ref, dt_ref, pre_ref, suf_ref, z_ref, cw_ref, cb_ref,
     dskip_ref, gain_ref, exp_ref, h0f_ref, h0b_ref) = refs[:14]
    if need_y:
        y_ref, hf_out, hb_out = refs[14:17]
        rest = refs[17:]
    else:
        y_ref = None
        hf_out, hb_out = refs[14:16]
        rest = refs[16:]
    cwin, u_ref, hf, hb, hbs, ust = rest

    sw = pl.program_id(1)
    s = pl.program_id(2)
    bi = jnp.where(sw == 0, nsteps - 1 - s, s)
    rows = xbc_ref.shape[1]
    h = SSD_HEADS

    @pl.when((sw == 0) & (s == 0))
    def _():
        hb[...] = h0b_ref[0]

    @pl.when((sw == 1) & (s == 0))
    def _():
        hf[...] = h0f_ref[0]

    ri = lax.broadcasted_iota(jnp.int32, (CHUNK, CHUNK), 0)
    ci = lax.broadcasted_iota(jnp.int32, (CHUNK, CHUNK), 1)
    lane128 = lax.broadcasted_iota(jnp.int32, (CHUNK, LANES), 1)
    lo_mask = lane128 < SSD_HEAD_DIM

    def conv_silu(c, c0, c1, out_ref):
        r0 = pl.multiple_of(c * CHUNK, CHUNK)
        above = xbc_ref[0, pl.ds(pl.multiple_of(jnp.maximum(r0 - SUBLANES, 0), SUBLANES), SUBLANES), c0:c1]
        halo_a = jnp.where(bi > 0, prev_ref[0, :, c0:c1], 0.0)
        cwin[0:SUBLANES, c0:c1] = jnp.where(c > 0, above, halo_a)
        cwin[SUBLANES:SUBLANES + CHUNK, c0:c1] = xbc_ref[0, pl.ds(r0, CHUNK), c0:c1]
        below = xbc_ref[0, pl.ds(pl.multiple_of(jnp.minimum(r0 + CHUNK, rows - SUBLANES), SUBLANES), SUBLANES), c0:c1]
        halo_b = jnp.where(bi < nsteps - 1, next_ref[0, :, c0:c1], 0.0)
        cwin[SUBLANES + CHUNK:, c0:c1] = jnp.where(c < cps - 1, below, halo_b)
        for j0 in range(c0, c1, LANES):
            acc = jnp.broadcast_to(cb_ref[:, j0:j0 + LANES], (CHUNK, LANES))
            for k in range(CONV_WIDTH):
                off = SUBLANES - CONV_WIDTH // 2 + k
                acc = acc + cwin[off:off + CHUNK, j0:j0 + LANES] * cw_ref[k:k + 1, j0:j0 + LANES]
            out_ref[:, j0 - c0:j0 - c0 + LANES] = _silu(acc)

    def dt_rows(c):
        return dt_ref[c], pre_ref[c], suf_ref[c]

    def state_update(state_ref, xb, w_rows, tot_col):
        dec = jnp.exp(jnp.sum(tot_col * exp_ref[...], axis=0, keepdims=True))
        for g in range(SSD_GROUPS):
            bm = xb[:, SSD_INNER + g * SSD_STATE:SSD_INNER + (g + 1) * SSD_STATE]
            bt = bm.T
            for pr in range(2):
                h0 = g * 4 + pr * 2
                c0 = h0 * SSD_HEAD_DIM
                xp = xb[:, c0:c0 + LANES]
                rhs = jnp.concatenate([jnp.where(lo_mask, xp, 0.0), jnp.where(lo_mask, 0.0, xp)],
                                      axis=0).astype(BF16)
                lhs = jnp.concatenate([bt * w_rows[h0:h0 + 1, :], bt * w_rows[h0 + 1:h0 + 2, :]],
                                      axis=1).astype(BF16)
                sres = jnp.dot(lhs, rhs, preferred_element_type=F32)
                state_ref[:, c0:c0 + LANES] = state_ref[:, c0:c0 + LANES] * dec[:, c0:c0 + LANES] + sres

    def col_forms(rows16):
        padded = jnp.concatenate([rows16, jnp.zeros((CHUNK - 2 * h, LANES), F32)], axis=0)
        return padded.T

    def backward_chunk(k, carry):
        c = cps - 1 - k
        gc = bi * cps + c
        xb = ust.at[gc]
        conv_silu(c, 0, _XB_COLS, xb)
        dt, _, suf = dt_rows(c)
        hbs[gc] = hb[...].astype(BF16)
        s0 = suf[:, 0:1]
        w_rows = jnp.exp(s0 - suf) * dt
        state_update(hb, xb, w_rows[h:2 * h, :], s0[h:2 * h, :])
        return carry

    def forward_chunk(c, carry):
        gc = bi * cps + c
        r0 = pl.multiple_of(c * CHUNK, CHUNK)
        xb = ust.at[gc]
        conv_silu(c, _XB_COLS, CONV_DIM, u_ref)
        dt, pre, suf = dt_rows(c)
        rowsf = jnp.concatenate([pre[0:h, :], suf[h:2 * h, :]], axis=0)
        colsf = col_forms(rowsf)
        ecol = jnp.exp(colsf)
        for g in range(SSD_GROUPS):
            bm = xb[:, SSD_INNER + g * SSD_STATE:SSD_INNER + (g + 1) * SSD_STATE]
            cm = u_ref[:, g * SSD_STATE:(g + 1) * SSD_STATE]
            cmb = cm.astype(BF16)
            cbm = lax.dot_general(cmb, bm.astype(BF16), (((1,), (1,)), ((), ())),
                                  preferred_element_type=F32)
            for pr in range(2):
                h0 = g * 4 + pr * 2
                c0 = h0 * SSD_HEAD_DIM
                lhs_parts = []
                for hh in (h0, h0 + 1):
                    segf = colsf[:, hh:hh + 1] - rowsf[hh:hh + 1, :]
                    gf = jnp.exp(jnp.where(ci <= ri, segf, NEG)) * dt[hh:hh + 1, :]
                    segb = colsf[:, h + hh:h + hh + 1] - rowsf[h + hh:h + hh + 1, :]
                    gb = jnp.exp(jnp.where(ci >= ri, segb, NEG)) * dt[h + hh:h + hh + 1, :]
                    lhs_parts.append((cbm * (gf + gb)).astype(BF16))
                for hh in (h0, h0 + 1):
                    lhs_parts.append((cm * ecol[:, hh:hh + 1]).astype(BF16))
                for hh in (h0, h0 + 1):
                    lhs_parts.append((cm * ecol[:, h + hh:h + hh + 1]).astype(BF16))
                lhs = jnp.concatenate(lhs_parts, axis=1)
                xp = xb[:, c0:c0 + LANES]
                sf = hf[:, c0:c0 + LANES]
                sb = hbs[gc, :, c0:c0 + LANES].astype(F32)
                rhs = jnp.concatenate(
                    [jnp.where(lo_mask, xp, 0.0), jnp.where(lo_mask, 0.0, xp),
                     jnp.where(lo_mask, sf, 0.0), jnp.where(lo_mask, 0.0, sf),
                     jnp.where(lo_mask, sb, 0.0), jnp.where(lo_mask, 0.0, sb)], axis=0).astype(BF16)
                ypair = jnp.dot(lhs, rhs, preferred_element_type=F32)
                ypair = ypair + dskip_ref[:, c0:c0 + LANES] * xp
                zz = z_ref[0, pl.ds(r0, CHUNK), c0:c0 + LANES]
                u_ref[:, _Y_OFF + c0:_Y_OFF + c0 + LANES] = ypair * _silu(zz)
        last = pre[:, LANES - 1:LANES]
        w_rows = jnp.exp(last - pre) * dt
        state_update(hf, xb, w_rows[0:h, :], last[0:h, :])
        if need_y:
            yv = u_ref[:, _Y_OFF:_Y_OFF + SSD_INNER]
            ms = jnp.mean(yv * yv, axis=-1, keepdims=True)
            y_ref[0, pl.ds(r0, CHUNK), :] = (yv * lax.rsqrt(ms + EPS) * gain_ref[...]).astype(y_ref.dtype)
        return carry

    @pl.when(sw == 0)
    def _():
        lax.fori_loop(0, cps, backward_chunk, 0)

    @pl.when(sw == 1)
    def _():
        lax.fori_loop(0, cps, forward_chunk, 0)

    @pl.when((sw == 0) & (s == nsteps - 1))
    def _():
        hb_out[0] = hb[...]

    @pl.when((sw == 1) & (s == nsteps - 1))
    def _():
        hf_out[0] = hf[...]


def _ssd(xbc, dts, z, consts, h0f, h0b, *, rows, need_y):
    bt, seq, _ = xbc.shape
    nsteps = seq // rows
    cps = rows // CHUNK
    hb8 = rows // SUBLANES
    nb8 = seq // SUBLANES
    cw, cb, dskip, gain, expm = consts
    dt_spec = pl.BlockSpec((cps, 2 * SSD_HEADS, CHUNK), lambda b, sw, s: (b * nsteps + blk(b, sw, s), 0, 0))

    def blk(b, sw, s):
        return jnp.where(sw == 0, nsteps - 1 - s, s)

    def full(a):
        return pl.BlockSpec(a.shape, lambda b, sw, s: (0,) * a.ndim)

    in_specs = [
        pl.BlockSpec((1, rows, CONV_DIM), lambda b, sw, s: (b, blk(b, sw, s), 0)),
        pl.BlockSpec((1, SUBLANES, CONV_DIM), lambda b, sw, s: (b, jnp.maximum(blk(b, sw, s) * hb8 - 1, 0), 0)),
        pl.BlockSpec((1, SUBLANES, CONV_DIM),
                     lambda b, sw, s: (b, jnp.minimum((blk(b, sw, s) + 1) * hb8, nb8 - 1), 0)),
        dt_spec, dt_spec, dt_spec,
        pl.BlockSpec((1, rows, SSD_INNER), lambda b, sw, s: (b, jnp.where(sw == 0, 0, s), 0)),
        full(cw), full(cb), full(dskip), full(gain), full(expm),
        pl.BlockSpec((1, SSD_STATE, SSD_INNER), lambda b, sw, s: (b, 0, 0)),
        pl.BlockSpec((1, SSD_STATE, SSD_INNER), lambda b, sw, s: (b, 0, 0)),
    ]
    st_spec = pl.BlockSpec((1, SSD_STATE, SSD_INNER), lambda b, sw, s: (b, 0, 0))
    st_shape = jax.ShapeDtypeStruct((bt, SSD_STATE, SSD_INNER), F32)
    out_specs, out_shapes = [st_spec, st_spec], [st_shape, st_shape]
    if need_y:
        out_specs = [pl.BlockSpec((1, rows, SSD_INNER), lambda b, sw, s: (b, jnp.where(sw == 0, 0, s), 0))] + out_specs
        out_shapes = [jax.ShapeDtypeStruct((bt, seq, SSD_INNER), BF16)] + out_shapes
    scratch = [
        pltpu.VMEM((CHUNK + 2 * SUBLANES, CONV_DIM), F32),
        pltpu.VMEM((CHUNK, _Y_OFF + SSD_INNER), F32),
        pltpu.VMEM((SSD_STATE, SSD_INNER), F32),
        pltpu.VMEM((SSD_STATE, SSD_INNER), F32),
        pltpu.VMEM((seq // CHUNK, SSD_STATE, SSD_INNER), BF16),
        pltpu.VMEM((seq // CHUNK, CHUNK, _XB_COLS), F32),
    ]
    return pl.pallas_call(
        functools.partial(_ssd_kernel, need_y=need_y, nsteps=nsteps, cps=cps),
        grid=(bt, 2, nsteps),
        in_specs=in_specs,
        out_specs=out_specs,
        out_shape=out_shapes,
        scratch_shapes=scratch,
        compiler_params=pltpu.CompilerParams(dimension_semantics=("arbitrary", "arbitrary", "arbitrary")),
        name="ssd_y" if need_y else "ssd_ctx",
    )(xbc, xbc, xbc, *dts, z, cw, cb, dskip, gain, expm, h0f, h0b)


def _attn_kernel(sink_ref, q_ref, kp_ref, kc_ref, kn_ref, vp_ref, vc_ref, vn_ref, kx_ref, vx_ref, o_ref, *, nsteps):
    m = pl.program_id(1)
    lane = lax.broadcasted_iota(jnp.int32, (1, LANES), 1)
    lo = lane < HEAD_DIM
    rq = Q_PER_KV * WINDOW
    per = ROWS_ATT // WINDOW
    qi = lax.broadcasted_iota(jnp.int32, (rq, WINDOW), 0) & (WINDOW - 1)
    kj = lax.broadcasted_iota(jnp.int32, (rq, WINDOW), 1)
    rblk = lax.shift_right_logical(lax.broadcasted_iota(jnp.int32, (rq, 1), 0), int(math.log2(WINDOW)))
    zero = jnp.zeros((), BF16)
    band_prev = jnp.where(kj >= qi, 0.0, NEG)
    band_next = jnp.where(kj <= qi, 0.0, NEG)

    kpieces = [kp_ref[0]] + [kc_ref[0, w * WINDOW:(w + 1) * WINDOW] for w in range(per)] + [kn_ref[0]]
    vpieces = [vp_ref[0]] + [vc_ref[0, w * WINDOW:(w + 1) * WINDOW] for w in range(per)] + [vn_ref[0]]
    kx = kx_ref[0]
    vx = vx_ref[0]
    nt = (((1,), (1,)), ((), ()))

    for sb in range(per):
        qsb = q_ref[0, sb * WINDOW:(sb + 1) * WINDOW, :]
        mask_prev = band_prev if sb > 0 else jnp.where(m > 0, band_prev, NEG)
        mask_next = band_next if sb < per - 1 else jnp.where(m < nsteps - 1, band_next, NEG)
        outs = [None] * Q_PER_KV
        for kv in range(ATTN_KV_HEADS):
            sel = lo if kv == 0 else jnp.logical_not(lo)
            qs = jnp.concatenate(
                [jnp.where(sel, qsb[:, j * LANES:(j + 1) * LANES], zero) for j in range(Q_PER_KV)], axis=0)
            s_prev = lax.dot_general(qs, kpieces[sb], nt, preferred_element_type=F32) + mask_prev
            s_cur = lax.dot_general(qs, kpieces[sb + 1], nt, preferred_element_type=F32)
            s_next = lax.dot_general(qs, kpieces[sb + 2], nt, preferred_element_type=F32) + mask_next
            s_ctx = lax.dot_general(qs, kx, nt, preferred_element_type=F32)
            sink = jnp.zeros((rq, 1), F32)
            for j in range(Q_PER_KV):
                sink = jnp.where(rblk == j, sink_ref[kv * Q_PER_KV + j], sink)
            n_ct = s_ctx.shape[1] // LANES
            tile_max = jnp.maximum(jnp.maximum(s_prev, s_cur), s_next)
            for t in range(n_ct):
                tile_max = jnp.maximum(tile_max, s_ctx[:, t * LANES:(t + 1) * LANES])
            mx = jnp.maximum(jnp.max(tile_max, axis=-1, keepdims=True), sink)
            p_prev = jnp.exp2(s_prev - mx)
            p_cur = jnp.exp2(s_cur - mx)
            p_next = jnp.exp2(s_next - mx)
            p_ctx = jnp.exp2(s_ctx - mx)
            tile_sum = p_prev + p_cur + p_next
            for t in range(n_ct):
                tile_sum = tile_sum + p_ctx[:, t * LANES:(t + 1) * LANES]
            den = jnp.sum(tile_sum, axis=-1, keepdims=True) + jnp.exp2(sink - mx)
            acc = jnp.dot(p_prev.astype(BF16), jnp.where(sel, vpieces[sb], zero), preferred_element_type=F32)
            acc += jnp.dot(p_cur.astype(BF16), jnp.where(sel, vpieces[sb + 1], zero), preferred_element_type=F32)
            acc += jnp.dot(p_next.astype(BF16), jnp.where(sel, vpieces[sb + 2], zero), preferred_element_type=F32)
            acc += jnp.dot(p_ctx.astype(BF16), jnp.where(sel, vx, zero), preferred_element_type=F32)
            acc = acc * (1.0 / den)
            for j in range(Q_PER_KV):
                part = acc[j * WINDOW:(j + 1) * WINDOW, :]
                outs[j] = part if outs[j] is None else outs[j] + part
        for j in range(Q_PER_KV):
            o_ref[0, sb * WINDOW:(sb + 1) * WINDOW, j * LANES:(j + 1) * LANES] = outs[j].astype(o_ref.dtype)


def _attention(sinks, q, k, v, kx, vx):
    bt, seq, _ = q.shape
    nsteps = seq // ROWS_ATT
    per = ROWS_ATT // WINDOW
    nb = seq // WINDOW
    n_ctx = kx.shape[1]

    def prev(b, m, s):
        return (b, jnp.maximum(m * per - 1, 0), 0)

    def cur(b, m, s):
        return (b, m, 0)

    def nxt(b, m, s):
        return (b, jnp.minimum((m + 1) * per, nb - 1), 0)

    small = (1, WINDOW, KV_WIDTH)
    big = (1, ROWS_ATT, KV_WIDTH)
    grid_spec = pltpu.PrefetchScalarGridSpec(
        num_scalar_prefetch=1,
        grid=(bt, nsteps),
        in_specs=[
            pl.BlockSpec((1, ROWS_ATT, ATTN_WIDTH), cur),
            pl.BlockSpec(small, prev), pl.BlockSpec(big, cur), pl.BlockSpec(small, nxt),
            pl.BlockSpec(small, prev), pl.BlockSpec(big, cur), pl.BlockSpec(small, nxt),
            pl.BlockSpec((1, n_ctx, KV_WIDTH), lambda b, m, s: (b, 0, 0)),
            pl.BlockSpec((1, n_ctx, KV_WIDTH), lambda b, m, s: (b, 0, 0)),
        ],
        out_specs=pl.BlockSpec((1, ROWS_ATT, ATTN_WIDTH), cur),
    )
    return pl.pallas_call(
        functools.partial(_attn_kernel, nsteps=nsteps),
        grid_spec=grid_spec,
        out_shape=jax.ShapeDtypeStruct((bt, seq, ATTN_WIDTH), BF16),
        name="attn",
    )(sinks, q, k, k, k, v, v, v, kx, vx)


def _outproj_kernel(ssd_ref, att_ref, x_ref, g1_ref, sh_ref, sc_ref, n2_ref, wa_ref, wb_ref, wr_ref, br_ref,
                    tri_ref, x1_ref, h2_ref, ids_ref, wcol_ref, cnt_ref, carry, *, nsteps):
    i = pl.program_id(0)
    rows = x_ref.shape[0]

    @pl.when(i == 0)
    def _():
        carry[...] = jnp.zeros_like(carry)

    acc = jnp.dot(ssd_ref[...], wa_ref[...], preferred_element_type=F32)
    acc += jnp.dot(att_ref[...], wb_ref[...], preferred_element_type=F32)
    x1 = x_ref[...] + g1_ref[0] * acc
    x1_ref[...] = x1
    ms = jnp.mean(x1 * x1, axis=-1, keepdims=True)
    h2 = (x1 * lax.rsqrt(ms + EPS)) * (n2_ref[...] * (1.0 + sc_ref[0])) + sh_ref[0]
    h2b = h2.astype(BF16)
    half = h2.shape[1] // 2
    lo_bits = lax.shift_right_logical(pltpu.bitcast(h2b[:, :half].astype(F32), jnp.uint32), jnp.uint32(16))
    hi_bits = pltpu.bitcast(h2b[:, half:].astype(F32), jnp.uint32) & jnp.uint32(0xFFFF0000)
    h2_ref[...] = hi_bits | lo_bits
    lt = lax.dot_general(wr_ref[...], h2b, (((1,), (1,)), ((), ())),
                         preferred_element_type=F32) + br_ref[...]
    row = lax.broadcasted_iota(jnp.int32, lt.shape, 0).astype(F32)
    big_i = float(ROUTE_ROWS)
    gl = jnp.where(row < N_GROUPS, lt, NEG)
    gmax = jnp.max(gl, axis=0, keepdims=True)
    gidx = jnp.min(jnp.where(gl == gmax, row, big_i), axis=0, keepdims=True)
    g_w = 1.0 / jnp.sum(jnp.exp(gl - gmax), axis=0, keepdims=True)
    lo = N_GROUPS + EXPERTS_PER_GROUP * gidx
    el = jnp.where((row >= lo) & (row < lo + EXPERTS_PER_GROUP), lt, NEG)
    m1 = jnp.max(el, axis=0, keepdims=True)
    i1 = jnp.min(jnp.where(el == m1, row, big_i), axis=0, keepdims=True)
    el2 = jnp.where(row == i1, NEG, el)
    m2 = jnp.max(el2, axis=0, keepdims=True)
    i2 = jnp.min(jnp.where(el2 == m2, row, big_i), axis=0, keepdims=True)
    r = jnp.exp(m2 - m1)
    w1 = g_w / (1.0 + r)
    w2 = g_w * r / (1.0 + r)
    oh1 = row == i1
    oh2 = row == i2
    cnt = jnp.where(oh1 | oh2, 1.0, 0.0)
    prefix = jnp.dot(cnt.astype(BF16), tri_ref[...], preferred_element_type=F32)
    base = carry[...] + prefix
    rank1 = jnp.sum(jnp.where(oh1, base, 0.0), axis=0, keepdims=True)
    rank2 = jnp.sum(jnp.where(oh2, base, 0.0), axis=0, keepdims=True)
    carry[...] = carry[...] + jnp.sum(cnt, axis=1, keepdims=True)
    r8 = lax.broadcasted_iota(jnp.int32, (SUBLANES, rows), 0)
    code1 = (i1 - N_GROUPS) * float(1 << RANK_BITS) + rank1
    code2 = (i2 - N_GROUPS) * float(1 << RANK_BITS) + rank2
    ids_ref[...] = jnp.where(r8 == 0, code1, jnp.where(r8 == 1, code2, 0.0)).astype(jnp.int32)
    r128 = lax.broadcasted_iota(jnp.int32, (LANES, rows), 0)
    wfull = jnp.where(r128 == 0, w1, jnp.where(r128 == 1, w2, 0.0))
    for j in range(rows // LANES):
        wcol_ref[j * LANES:(j + 1) * LANES, :] = wfull[:, j * LANES:(j + 1) * LANES].T

    @pl.when(i == nsteps - 1)
    def _():
        cnt_ref[...] = carry[:, 0:LANES]


def _outproj(ssd, att, xf, mod3, norm2, wa, wb, wr, br, tri, *, rows, seq_blocks):
    t, d = xf.shape
    nsteps = t // rows

    def mod(col):
        return pl.BlockSpec((1, 1, d), lambda i: (i // seq_blocks, 0, col))

    def full(a):
        return pl.BlockSpec(a.shape, lambda i: (0,) * a.ndim)

    return pl.pallas_call(
        functools.partial(_outproj_kernel, nsteps=nsteps),
        grid=(nsteps,),
        in_specs=[pl.BlockSpec((rows, SSD_INNER), lambda i: (i, 0)),
                  pl.BlockSpec((rows, ATTN_WIDTH), lambda i: (i, 0)),
                  pl.BlockSpec((rows, d), lambda i: (i, 0)),
                  mod(2), mod(3), mod(4), full(norm2), full(wa), full(wb), full(wr), full(br), full(tri)],
        out_specs=[pl.BlockSpec((rows, d), lambda i: (i, 0)),
                   pl.BlockSpec((rows, d // 2), lambda i: (i, 0)),
                   pl.BlockSpec((SUBLANES, rows), lambda i: (0, i)),
                   pl.BlockSpec((rows, LANES), lambda i: (i, 0)),
                   pl.BlockSpec((ROUTE_ROWS, LANES), lambda i: (0, 0))],
        out_shape=[jax.ShapeDtypeStruct((t, d), F32),
                   jax.ShapeDtypeStruct((t, d // 2), jnp.uint32),
                   jax.ShapeDtypeStruct((SUBLANES, t), jnp.int32),
                   jax.ShapeDtypeStruct((t, LANES), F32),
                   jax.ShapeDtypeStruct((ROUTE_ROWS, LANES), F32)],
        scratch_shapes=[pltpu.VMEM((ROUTE_ROWS, rows), F32)],
        compiler_params=pltpu.CompilerParams(dimension_semantics=("arbitrary",)),
        name="outproj_router",
    )(ssd, att, xf, mod3, mod3, mod3, norm2, wa, wb, wr, br, tri)


def _expert_kernel(blk_e_ref, nused_ref, pstart_ref, counts_ref, eord_ref, enext_ref, dst_ref,
                   h2_hbm, wg_hbm, wu_hbm, wd_hbm, y_hbm,
                   h2v, xg, ystage, wgs, wus, wds, wgb, wub, wdb, tok_ref, sem, wsem, ysem, *, n_tok, nblocks):
    b = pl.program_id(0)
    nused = nused_ref[0]
    slot = b % 2
    half = h2v.shape[1]

    def out_copy(aid, r, sl):
        return pltpu.make_async_copy(ystage.at[sl, pl.ds(r * D_TILES, D_TILES)],
                                     y_hbm.at[pl.ds(aid * D_TILES, D_TILES)], ysem.at[sl])

    def wait_out(sl):
        for r in range(MOE_BLK):
            out_copy(0, r, sl).wait()

    def h2_copy():
        return pltpu.make_async_copy(h2_hbm, h2v, sem.at[0])

    def weight_copies(e, sl):
        return (pltpu.make_async_copy(wg_hbm.at[e], wgs.at[sl], wsem.at[sl, 0]),
                pltpu.make_async_copy(wu_hbm.at[e], wus.at[sl], wsem.at[sl, 1]),
                pltpu.make_async_copy(wd_hbm.at[e], wds.at[sl], wsem.at[sl, 2]))

    def gather_rows(blk, sl):
        for r in range(MOE_BLK):
            tok = tok_ref[blk * MOE_BLK + r] & (n_tok - 1)
            xg[sl, pl.ds(r, 1), :] = h2v[pl.ds(tok, 1), :]

    def scratch_fill(sl):
        rows_sl = MOE_BLK * D_TILES
        return pltpu.make_async_copy(ystage.at[sl], y_hbm.at[pl.ds((2 * n_tok + sl * MOE_BLK) * D_TILES, rows_sl)],
                                     ysem.at[sl])

    @pl.when(b == 0)
    def _():
        h2_copy().start()
        for cp in weight_copies(blk_e_ref[0], 0):
            cp.start()
        ystage[...] = jnp.zeros_like(ystage)
        scratch_fill(0).start()
        scratch_fill(1).start()
        def pad_expert(e, carry):
            lo = pstart_ref[e] + counts_ref[e]
            hi = pstart_ref[e] + (counts_ref[e] + MOE_BLK - 1) // MOE_BLK * MOE_BLK

            def pad_slot(s, c):
                tok_ref[s] = 2 * n_tok + (s & (2 * MOE_BLK - 1))
                return c
            return lax.fori_loop(lo, hi, pad_slot, carry)
        lax.fori_loop(0, N_EXPERTS, pad_expert, 0)

        def claim(i, carry):
            tok_ref[dst_ref[i]] = i
            return carry
        lax.fori_loop(0, 2 * n_tok, claim, 0, unroll=16)
        h2_copy().wait()
        scratch_fill(0).wait()
        scratch_fill(1).wait()
        gather_rows(0, 0)

    e = blk_e_ref[b]
    e_prev = blk_e_ref[jnp.maximum(b - 1, 0)]

    @pl.when((b == 0) | (e != e_prev))
    def _():
        sl = eord_ref[e] % 2
        for cp in weight_copies(e, sl):
            cp.wait()
        wgb[...] = wgs[sl].astype(BF16)
        wub[...] = wus[sl].astype(BF16)
        wdb[...] = wds[sl].astype(BF16)
        e_next = enext_ref[e]

        @pl.when(e_next >= 0)
        def _():
            for cp in weight_copies(e_next, 1 - sl):
                cp.start()

    @pl.when(b < nused)
    def _():
        words = xg[slot]
        x_lo = pltpu.bitcast(words << jnp.uint32(16), F32).astype(BF16)
        x_hi = pltpu.bitcast(words & jnp.uint32(0xFFFF0000), F32).astype(BF16)
        gate = (jnp.dot(x_lo, wgb[0:half, :], preferred_element_type=F32)
                + jnp.dot(x_hi, wgb[half:, :], preferred_element_type=F32))
        up = (jnp.dot(x_lo, wub[0:half, :], preferred_element_type=F32)
              + jnp.dot(x_hi, wub[half:, :], preferred_element_type=F32))
        hmid = (_silu(gate) * up).astype(BF16)
        y = jnp.dot(hmid, wdb[...], preferred_element_type=F32)
        gather_rows(jnp.minimum(b + 1, nused - 1), 1 - slot)

        @pl.when(b >= 2)
        def _():
            wait_out(slot)
        for s in range(D_TILES):
            ystage[slot, pl.ds(s, MOE_BLK, stride=D_TILES), :] = y[:, s * LANES:(s + 1) * LANES]
        for r in range(MOE_BLK):
            out_copy(tok_ref[b * MOE_BLK + r], r, slot).start()

    @pl.when(b == nblocks - 1)
    def _():
        @pl.when(nused >= 2)
        def _():
            wait_out(nused % 2)

        wait_out((nused - 1) % 2)


def _experts(blk_e, nused, pstart, counts, eord, enext, dst, h2p, w_gate, w_up, w_down, *, nblocks, n_tok):
    d = w_gate.shape[1]
    hbm = pl.BlockSpec(memory_space=pl.ANY)
    grid_spec = pltpu.PrefetchScalarGridSpec(
        num_scalar_prefetch=7,
        grid=(nblocks,),
        in_specs=[hbm, hbm, hbm, hbm],
        out_specs=hbm,
        scratch_shapes=[
            pltpu.VMEM(h2p.shape, jnp.uint32),
            pltpu.VMEM((2, MOE_BLK, d // 2), jnp.uint32),
            pltpu.VMEM((2, MOE_BLK * D_TILES, LANES), F32),
            pltpu.VMEM((2, d, EXPERT_DIM), F32),
            pltpu.VMEM((2, d, EXPERT_DIM), F32),
            pltpu.VMEM((2, EXPERT_DIM, d), F32),
            pltpu.VMEM((d, EXPERT_DIM), BF16),
            pltpu.VMEM((d, EXPERT_DIM), BF16),
            pltpu.VMEM((EXPERT_DIM, d), BF16),
            pltpu.SMEM((nblocks * MOE_BLK,), jnp.int32),
            pltpu.SemaphoreType.DMA((1,)),
            pltpu.SemaphoreType.DMA((2, 3)),
            pltpu.SemaphoreType.DMA((2,)),
        ],
    )
    assert n_tok & (n_tok - 1) == 0 and MOE_BLK & (MOE_BLK - 1) == 0
    n_tiles = 2 * n_tok + 2 * MOE_BLK
    return pl.pallas_call(
        functools.partial(_expert_kernel, n_tok=n_tok, nblocks=nblocks),
        grid_spec=grid_spec,
        out_shape=jax.ShapeDtypeStruct((n_tiles * D_TILES, LANES), F32),
        compiler_params=pltpu.CompilerParams(dimension_semantics=("arbitrary",)),
        name="experts",
    )(blk_e, nused, pstart, counts, eord, enext, dst, h2p, w_gate, w_up, w_down)


def _combine_kernel(y0_ref, y1_ref, x1_ref, wcol_ref, g2_ref, nf_ref, o_ref):
    rows = x1_ref.shape[0]
    w0 = wcol_ref[:, 0:1]
    w1 = wcol_ref[:, 1:2]
    ssq = jnp.zeros((rows, 1), F32)
    for s in range(D_TILES):
        cols = slice(s * LANES, (s + 1) * LANES)
        moe = (y0_ref[pl.ds(s, rows, stride=D_TILES), :] * w0
               + y1_ref[pl.ds(s, rows, stride=D_TILES), :] * w1)
        x2 = x1_ref[:, cols] + g2_ref[0, :, cols] * moe
        ssq = ssq + jnp.sum(x2 * x2, axis=-1, keepdims=True)
        o_ref[:, cols] = x2
    inv = lax.rsqrt(ssq * (1.0 / (D_TILES * LANES)) + EPS)
    o_ref[...] = o_ref[...] * inv * nf_ref[...]


def _combine(y_tok, x1, wcol, mod3, norm_final, *, rows, seq_blocks):
    t, d = x1.shape
    nsteps = t // rows
    return pl.pallas_call(
        _combine_kernel,
        grid=(nsteps,),
        in_specs=[
            pl.BlockSpec((rows * D_TILES, LANES), lambda i: (i, 0)),
            pl.BlockSpec((rows * D_TILES, LANES), lambda i: (nsteps + i, 0)),
            pl.BlockSpec((rows, d), lambda i: (i, 0)),
            pl.BlockSpec((rows, LANES), lambda i: (i, 0)),
            pl.BlockSpec((1, 1, d), lambda i: (i // seq_blocks, 0, 5)),
            pl.BlockSpec((1, d), lambda i: (0, 0)),
        ],
        out_specs=pl.BlockSpec((rows, d), lambda i: (i, 0)),
        out_shape=jax.ShapeDtypeStruct((t, d), F32),
        name="combine",
    )(y_tok, y_tok, x1, wcol, mod3, norm_final)


def _rope_tables(seq):
    pos = np.arange(seq)
    n_freq = HEAD_DIM // 4
    inv = ROPE_BASE ** (-np.arange(n_freq, dtype=np.float32) / n_freq)
    ang = np.concatenate([(pos // GRID_W)[:, None] * inv, (pos % GRID_W)[:, None] * inv], axis=-1)
    ang = np.concatenate([ang, ang, ang, ang], axis=-1).astype(np.float32)
    return jnp.asarray(np.cos(ang), F32), jnp.asarray(np.sin(ang), F32)


def _head_expand_mat():
    m = np.zeros((SSD_HEADS, SSD_INNER), np.float32)
    for hh in range(SSD_HEADS):
        m[hh, hh * SSD_HEAD_DIM:(hh + 1) * SSD_HEAD_DIM] = 1.0
    return jnp.asarray(m)


def kernel(x, c, ctx, c_ctx, w_ada, b_ada, norm1, w_in, conv_w, conv_b, dt_bias, a_log, d_skip, ssd_norm,
           attn_sinks, w_out, norm2, w_group, b_group, w_expert, b_expert, w_gate, w_up, w_down, norm_final):
    return _pipeline(x, c, ctx, c_ctx, w_ada, b_ada, norm1, w_in, conv_w, conv_b, dt_bias, a_log, d_skip,
                     ssd_norm, attn_sinks, w_out, norm2, w_group, b_group, w_expert, b_expert, w_gate, w_up,
                     w_down, norm_final)["out"]


def _pipeline(x, c, ctx, c_ctx, w_ada, b_ada, norm1, w_in, conv_w, conv_b, dt_bias, a_log, d_skip, ssd_norm,
              attn_sinks, w_out, norm2, w_group, b_group, w_expert, b_expert, w_gate, w_up, w_down, norm_final):
    bt, seq, d = x.shape
    n_ctx = ctx.shape[1]
    t = bt * seq
    layer = 0
    assert w_ada.shape[0] == 1 and seq % ROWS_IN == 0 and n_ctx % CHUNK == 0

    cc = jnp.zeros((SUBLANES, d), F32).at[:bt].set(c).at[bt].set(c_ctx)
    mod = _ada(cc, w_ada[layer], b_ada[layer][None, :])
    mod3 = mod.reshape(SUBLANES, 1, 6 * d)

    w = w_in[layer]
    o_z, o_xbc, o_dt = 0, SSD_INNER, SSD_INNER + CONV_DIM
    o_q = o_dt + 2 * SSD_HEADS
    o_k, o_v = o_q + ATTN_WIDTH, o_q + ATTN_WIDTH + KV_WIDTH
    q_perm = np.concatenate([np.r_[j * HEAD_DIM:(j + 1) * HEAD_DIM, (j + 4) * HEAD_DIM:(j + 5) * HEAD_DIM]
                             for j in range(Q_PER_KV)])
    w_q = w[:, o_q:o_q + ATTN_WIDTH].reshape(d, ATTN_KV_HEADS, Q_PER_KV, HEAD_DIM).transpose(0, 2, 1, 3)
    w_q = w_q.reshape(d, ATTN_WIDTH)
    w_cat = jnp.concatenate([w[:, o_z:o_z + SSD_INNER], w[:, o_xbc:o_xbc + CONV_DIM], w_q,
                             w[:, o_k:o_k + KV_WIDTH], w[:, o_v:o_v + KV_WIDTH]], axis=1).astype(BF16)
    w_dtT = w[:, o_dt:o_dt + 2 * SSD_HEADS].T.astype(BF16)

    seq_blocks = seq // ROWS_IN
    cos, sin = _rope_tables(seq)
    bias_rows = jnp.broadcast_to(dt_bias[layer].reshape(2 * SSD_HEADS, 1), (2 * SSD_HEADS, LANES))
    alog_rows = jnp.broadcast_to(a_log[layer].reshape(2 * SSD_HEADS, 1), (2 * SSD_HEADS, LANES))
    segs_x = [("z", _COL_Z, SSD_INNER), ("xbc", _COL_XBC, CONV_DIM), ("dtT", 0, 0),
              ("q", _COL_Q, ATTN_WIDTH), ("k", _COL_K, KV_WIDTH), ("v", _COL_V, KV_WIDTH)]
    z_x, xbc_x, dt_x, pre_x, suf_x, q_x, k_x, v_x = _inproj(
        x.reshape(t, d), mod3, norm1[layer][None, :], w_cat, w_dtT, bias_rows, alog_rows, (cos, sin),
        rows=ROWS_IN, mod_row_fn=lambda i: i // seq_blocks, segs=segs_x, seq_blocks=seq_blocks)
    segs_c = [("xbc", _COL_XBC, CONV_DIM), ("dtT", 0, 0), ("k", _COL_K, KV_WIDTH), ("v", _COL_V, KV_WIDTH)]
    xbc_c, dt_c, pre_c, suf_c, k_c, v_c = _inproj(
        ctx.reshape(bt * n_ctx, d), mod3, norm1[layer][None, :], w_cat, w_dtT, bias_rows, alog_rows, None,
        rows=n_ctx, mod_row_fn=lambda i: bt, segs=segs_c, seq_blocks=1)

    cw = jnp.zeros((SUBLANES, CONV_DIM), F32).at[:CONV_WIDTH].set(conv_w[layer])
    cb = conv_b[layer][None, :]
    dskip_e = jnp.repeat(d_skip[layer], SSD_HEAD_DIM)[None, :]
    consts = (cw, cb, dskip_e, ssd_norm[layer][None, :], _head_expand_mat())

    zeros_state = jnp.zeros((bt, SSD_STATE, SSD_INNER), F32)
    z_dummy = jnp.zeros((bt, n_ctx, SSD_INNER), F32)
    h_cf, h_cb = _ssd(xbc_c.reshape(bt, n_ctx, CONV_DIM), (dt_c, pre_c, suf_c), z_dummy, consts,
                      zeros_state, zeros_state, rows=n_ctx, need_y=False)
    ssd_x, _, _ = _ssd(xbc_x.reshape(bt, seq, CONV_DIM), (dt_x, pre_x, suf_x), z_x.reshape(bt, seq, SSD_INNER),
                       consts, h_cf, h_cb, rows=ROWS_SSD, need_y=True)

    attn_x = _attention(attn_sinks[layer].astype(F32) * LOG2E,
                        q_x.reshape(bt, seq, ATTN_WIDTH), k_x.reshape(bt, seq, KV_WIDTH),
                        v_x.reshape(bt, seq, KV_WIDTH), k_c.reshape(bt, n_ctx, KV_WIDTH),
                        v_c.reshape(bt, n_ctx, KV_WIDTH))

    wo = w_out[layer]
    wa = wo[:SSD_INNER].astype(BF16)
    wb = wo[SSD_INNER:].reshape(ATTN_KV_HEADS, Q_PER_KV, HEAD_DIM, d).transpose(1, 0, 2, 3)
    wb = wb.reshape(ATTN_WIDTH, d).astype(BF16)
    wr = jnp.zeros((ROUTE_ROWS, d), F32).at[:N_GROUPS].set(w_group[layer].T)
    wr = wr.at[N_GROUPS:N_GROUPS + N_EXPERTS].set(w_expert[layer].T).astype(BF16)
    br = jnp.zeros((ROUTE_ROWS,), F32).at[:N_GROUPS].set(b_group[layer])
    br = br.at[N_GROUPS:N_GROUPS + N_EXPERTS].set(b_expert[layer])
    br = jnp.broadcast_to(br[:, None], (ROUTE_ROWS, ROWS_IN))
    tri = jnp.asarray(np.triu(np.ones((ROWS_IN, ROWS_IN), np.float32), 1), BF16)
    x1, h2, ids, wcol, cnt = _outproj(
        ssd_x.reshape(t, SSD_INNER), attn_x.reshape(t, ATTN_WIDTH), x.reshape(t, d), mod3, norm2[layer][None, :],
        wa, wb, wr, br, tri, rows=ROWS_IN, seq_blocks=seq_blocks)

    counts = cnt[N_GROUPS:N_GROUPS + N_EXPERTS, 0].astype(jnp.int32)
    padded = (counts + MOE_BLK - 1) // MOE_BLK * MOE_BLK
    ends = jnp.cumsum(padded)
    pstart = ends - padded
    nblocks = (2 * t) // MOE_BLK + N_EXPERTS
    blk_start = jnp.arange(nblocks, dtype=jnp.int32) * MOE_BLK
    eids = jnp.arange(N_EXPERTS, dtype=jnp.int32)
    active = counts > 0
    last_active = jnp.max(jnp.where(active, eids, 0))
    blk_e = jnp.minimum(jnp.sum((ends[None, :] <= blk_start[:, None]).astype(jnp.int32), axis=1), last_active)
    nused = (ends[-1] // MOE_BLK).astype(jnp.int32).reshape(1)
    eord = jnp.cumsum(active.astype(jnp.int32)) - active.astype(jnp.int32)
    later = active[None, :] & (eids[None, :] > eids[:, None])
    enext = jnp.min(jnp.where(later, eids[None, :], N_EXPERTS), axis=1)
    enext = jnp.where(enext == N_EXPERTS, -1, enext).astype(jnp.int32)
    code = ids[0:2].reshape(-1)
    e_id = lax.shift_right_logical(code, RANK_BITS)
    onehot = e_id[:, None] == jnp.arange(N_EXPERTS, dtype=jnp.int32)[None, :]
    dst = jnp.sum(jnp.where(onehot, pstart[None, :], 0), axis=1) + (code & ((1 << RANK_BITS) - 1))

    y_buf = _experts(blk_e, nused, pstart, counts, eord, enext, dst, h2, w_gate[layer], w_up[layer],
                     w_down[layer], nblocks=nblocks, n_tok=t)
    out = _combine(y_buf, x1, wcol, mod3, norm_final[None, :], rows=ROWS_CMB, seq_blocks=seq // ROWS_CMB)
    return dict(out=out.reshape(bt, seq, d), mod=mod, z=z_x, xbc=xbc_x, dt=dt_x, q=q_x, k=k_x, v=v_x,
                k_c=k_c, v_c=v_c, h_cf=h_cf, h_cb=h_cb, ssd=ssd_x, attn=attn_x, x1=x1, h2=h2, ids=ids,
                wcol=wcol, cnt=cnt, y_buf=y_buf, pstart=pstart, q_perm=q_perm)
```

```python
import functools
import math

import numpy as np
import jax
import jax.numpy as jnp
from jax import lax
from jax.experimental import pallas as pl
from jax.experimental.pallas import tpu as pltpu

F32 = jnp.float32
BF16 = jnp.bfloat16

D_MODEL = 1024
GRID_W = 64
EPS = 1e-6
SSD_INNER = 512
SSD_HEAD_DIM = 64
SSD_HEADS = 8
SSD_GROUPS = 2
SSD_STATE = 128
CONV_WIDTH = 5
CONV_DIM = SSD_INNER + 2 * SSD_GROUPS * SSD_STATE
CHUNK = 128
ATTN_WIDTH = 512
HEAD_DIM = 64
ATTN_Q_HEADS = 8
ATTN_KV_HEADS = 2
Q_PER_KV = 4
KV_WIDTH = 128
WINDOW = 128
ROPE_BASE = 10000.0
N_GROUPS = 4
EXPERTS_PER_GROUP = 8
N_EXPERTS = 32
EXPERT_DIM = 512

LANES = 128
SUBLANES = 8
NEG = -1e30
LOG2E = math.log2(math.e)

ROWS_IN = 512
ROWS_SSD = 512
ROWS_ATT = 512
ROUTE_ROWS = 48
MOE_BLK = 256
RANK_BITS = 16
D_TILES = D_MODEL // LANES
ROWS_CMB = 128


def _silu(v):
    half = 0.5 * v
    return half + half * jnp.tanh(half)


def _softplus(v):
    return jnp.maximum(v, 0.0) + jnp.log(1.0 + jnp.exp(-jnp.abs(v)))


def _ada_kernel(c_ref, w_ref, b_ref, o_ref):
    a = _silu(c_ref[...])
    o_ref[...] = jnp.dot(a.astype(BF16), w_ref[...].astype(BF16),
                         preferred_element_type=F32) + b_ref[...]


def _ada(cc, w_ada, b_ada):
    d, n = w_ada.shape
    tn = 1536
    return pl.pallas_call(
        _ada_kernel,
        grid=(n // tn,),
        in_specs=[pl.BlockSpec((SUBLANES, d), lambda j: (0, 0)),
                  pl.BlockSpec((d, tn), lambda j: (0, j)),
                  pl.BlockSpec((1, tn), lambda j: (0, j))],
        out_specs=pl.BlockSpec((SUBLANES, tn), lambda j: (0, j)),
        out_shape=jax.ShapeDtypeStruct((SUBLANES, n), F32),
        name="ada",
    )(cc, w_ada, b_ada)


def _rope(t, cos, sin_signed, first_half):
    rot = jnp.where(first_half, pltpu.roll(t, LANES - HEAD_DIM // 2, 1), pltpu.roll(t, HEAD_DIM // 2, 1))
    return t * cos + rot * sin_signed


def _lane_cumsum(v, lane, reverse):
    k = 1
    while k < LANES:
        if reverse:
            v = v + jnp.where(lane < LANES - k, pltpu.roll(v, LANES - k, 1), 0.0)
        else:
            v = v + jnp.where(lane >= k, pltpu.roll(v, k, 1), 0.0)
        k *= 2
    return v


def _inproj_kernel(*refs, segs, rope):
    x_ref, sh_ref, sc_ref, g_ref, w_ref, wdt_ref, bias_ref, alog_ref = refs[:8]
    pos = 8
    if rope:
        cos_ref, sin_ref = refs[8:10]
        pos = 10
    outs = iter(refs[pos:])
    x = x_ref[...]
    ms = jnp.mean(x * x, axis=-1, keepdims=True)
    h = (x * lax.rsqrt(ms + EPS)) * (g_ref[...] * (1.0 + sc_ref[0])) + sh_ref[0]
    hb = h.astype(BF16)
    if rope:
        cos = cos_ref[...]
        lane = lax.broadcasted_iota(jnp.int32, cos.shape, 1)
        first_half = (lane & (HEAD_DIM - 1)) < HEAD_DIM // 2
        sin_s = jnp.where(first_half, -sin_ref[...], sin_ref[...])
    for kind, c0, width in segs:
        if kind == "dtT":
            dt_ref, pre_ref, suf_ref = next(outs), next(outs), next(outs)
            dtt = lax.dot_general(wdt_ref[...], hb, (((1,), (1,)), ((), ())), preferred_element_type=F32)
            lane16 = lax.broadcasted_iota(jnp.int32, (2 * SSD_HEADS, CHUNK), 1)
            neg_a = -jnp.exp(alog_ref[...])
            for j in range(dt_ref.shape[0]):
                dt = _softplus(dtt[:, j * CHUNK:(j + 1) * CHUNK] + bias_ref[...])
                a = dt * neg_a
                dt_ref[j] = dt
                pre_ref[j] = _lane_cumsum(a, lane16, False)
                suf_ref[j] = _lane_cumsum(a, lane16, True)
            continue
        o_ref = next(outs)
        for j0 in range(0, width, 512):
            wj = min(512, width - j0)
            acc = jnp.dot(hb, w_ref[:, c0 + j0:c0 + j0 + wj], preferred_element_type=F32)
            if rope and kind in ("q", "k"):
                scale = HEAD_DIM ** -0.5 * LOG2E if kind == "q" else 1.0
                for l0 in range(0, wj, LANES):
                    t = _rope(acc[:, l0:l0 + LANES], cos, sin_s, first_half)
                    o_ref[:, j0 + l0:j0 + l0 + LANES] = (t * scale).astype(o_ref.dtype)
            else:
                o_ref[:, j0:j0 + wj] = acc.astype(o_ref.dtype)


_COL_Z, _COL_XBC, _COL_Q, _COL_K, _COL_V = 0, 512, 1536, 2048, 2176
_W_COLS = 2304


def _inproj(xf, mod3, norm1, w_cat, w_dtT, bias_rows, alog_rows, rope_tabs, *, rows, mod_row_fn, segs, seq_blocks):
    t, d = xf.shape
    rope = rope_tabs is not None
    in_specs = [
        pl.BlockSpec((rows, d), lambda i: (i, 0)),
        pl.BlockSpec((1, 1, d), lambda i: (mod_row_fn(i), 0, 0)),
        pl.BlockSpec((1, 1, d), lambda i: (mod_row_fn(i), 0, 1)),
        pl.BlockSpec((1, d), lambda i: (0, 0)),
        pl.BlockSpec(w_cat.shape, lambda i: (0, 0)),
        pl.BlockSpec(w_dtT.shape, lambda i: (0, 0)),
        pl.BlockSpec(bias_rows.shape, lambda i: (0, 0)),
        pl.BlockSpec(alog_rows.shape, lambda i: (0, 0)),
    ]
    args = [xf, mod3, mod3, norm1, w_cat, w_dtT, bias_rows, alog_rows]
    if rope:
        in_specs += [pl.BlockSpec((rows, LANES), lambda i: (i % seq_blocks, 0))] * 2
        args += list(rope_tabs)
    out_specs, out_shapes = [], []
    for kind, _, width in segs:
        if kind == "dtT":
            for _ in range(3):
                out_specs.append(pl.BlockSpec((rows // CHUNK, 2 * SSD_HEADS, CHUNK), lambda i: (i, 0, 0)))
                out_shapes.append(jax.ShapeDtypeStruct((t // CHUNK, 2 * SSD_HEADS, CHUNK), F32))
        else:
            dt = BF16 if kind in ("q", "k", "v") else F32
            out_specs.append(pl.BlockSpec((rows, width), lambda i: (i, 0)))
            out_shapes.append(jax.ShapeDtypeStruct((t, width), dt))
    return pl.pallas_call(
        functools.partial(_inproj_kernel, segs=tuple(segs), rope=rope),
        grid=(t // rows,),
        in_specs=in_specs,
        out_specs=out_specs,
        out_shape=out_shapes,
        name="inproj_rope" if rope else "inproj_ctx",
    )(*args)


_XB_COLS = SSD_INNER + SSD_GROUPS * SSD_STATE
_Y_OFF = CONV_DIM - _XB_COLS


def _ssd_kernel(*refs, need_y, nsteps, cps):
    (xbc_ref, prev_ref, next_ref, dt_ref, pre_ref, suf_ref, z_ref, cw_ref, cb_ref,
     dskip_ref, gain_ref, exp_ref, h0f_ref, h0b_ref) = refs[:14]
    if need_y:
        y_ref, hf_out, hb_out = refs[14:17]
        rest = refs[17:]
    else:
        y_ref = None
        hf_out, hb_out = refs[14:16]
        rest = refs[16:]
    cwin, u_ref, hf, hb, hbs, ust = rest

    sw = pl.program_id(1)
    s = pl.program_id(2)
    bi = jnp.where(sw == 0, nsteps - 1 - s, s)
    rows = xbc_ref.shape[1]
    h = SSD_HEADS

    @pl.when((sw == 0) & (s == 0))
    def _():
        hb[...] = h0b_ref[0]

    @pl.when((sw == 1) & (s == 0))
    def _():
        hf[...] = h0f_ref[0]

    ri = lax.broadcasted_iota(jnp.int32, (CHUNK, CHUNK), 0)
    ci = lax.broadcasted_iota(jnp.int32, (CHUNK, CHUNK), 1)
    lane128 = lax.broadcasted_iota(jnp.int32, (CHUNK, LANES), 1)
    lo_mask = lane128 < SSD_HEAD_DIM

    def conv_silu(c, c0, c1, out_ref):
        r0 = pl.multiple_of(c * CHUNK, CHUNK)
        above = xbc_ref[0, pl.ds(pl.multiple_of(jnp.maximum(r0 - SUBLANES, 0), SUBLANES), SUBLANES), c0:c1]
        halo_a = jnp.where(bi > 0, prev_ref[0, :, c0:c1], 0.0)
        cwin[0:SUBLANES, c0:c1] = jnp.where(c > 0, above, halo_a)
        cwin[SUBLANES:SUBLANES + CHUNK, c0:c1] = xbc_ref[0, pl.ds(r0, CHUNK), c0:c1]
        below = xbc_ref[0, pl.ds(pl.multiple_of(jnp.minimum(r0 + CHUNK, rows - SUBLANES), SUBLANES), SUBLANES), c0:c1]
        halo_b = jnp.where(bi < nsteps - 1, next_ref[0, :, c0:c1], 0.0)
        cwin[SUBLANES + CHUNK:, c0:c1] = jnp.where(c < cps - 1, below, halo_b)
        for j0 in range(c0, c1, LANES):
            acc = jnp.broadcast_to(cb_ref[:, j0:j0 + LANES], (CHUNK, LANES))
            for k in range(CONV_WIDTH):
                off = SUBLANES - CONV_WIDTH // 2 + k
                acc = acc + cwin[off:off + CHUNK, j0:j0 + LANES] * cw_ref[k:k + 1, j0:j0 + LANES]
            out_ref[:, j0 - c0:j0 - c0 + LANES] = _silu(acc)

    def dt_rows(c):
        return dt_ref[c], pre_ref[c], suf_ref[c]

    def state_update(state_ref, xb, w_rows, tot_col):
        dec = jnp.exp(jnp.sum(tot_col * exp_ref[...], axis=0, keepdims=True))
        for g in range(SSD_GROUPS):
            bm = xb[:, SSD_INNER + g * SSD_STATE:SSD_INNER + (g + 1) * SSD_STATE]
            bt = bm.T
            for pr in range(2):
                h0 = g * 4 + pr * 2
                c0 = h0 * SSD_HEAD_DIM
                xp = xb[:, c0:c0 + LANES]
                rhs = jnp.concatenate([jnp.where(lo_mask, xp, 0.0), jnp.where(lo_mask, 0.0, xp)],
                                      axis=0).astype(BF16)
                lhs = jnp.concatenate([bt * w_rows[h0:h0 + 1, :], bt * w_rows[h0 + 1:h0 + 2, :]],
                                      axis=1).astype(BF16)
                sres = jnp.dot(lhs, rhs, preferred_element_type=F32)
                state_ref[:, c0:c0 + LANES] = state_ref[:, c0:c0 + LANES] * dec[:, c0:c0 + LANES] + sres

    def col_forms(rows16):
        padded = jnp.concatenate([rows16, jnp.zeros((CHUNK - 2 * h, LANES), F32)], axis=0)
        return padded.T

    def backward_chunk(k, carry):
        c = cps - 1 - k
        gc = bi * cps + c
        xb = ust.at[gc]
        conv_silu(c, 0, _XB_COLS, xb)
        dt, _, suf = dt_rows(c)
        hbs[gc] = hb[...].astype(BF16)
        s0 = suf[:, 0:1]
        w_rows = jnp.exp(s0 - suf) * dt
        state_update(hb, xb, w_rows[h:2 * h, :], s0[h:2 * h, :])
        return carry

    def forward_chunk(c, carry):
        gc = bi * cps + c
        r0 = pl.multiple_of(c * CHUNK, CHUNK)
        xb = ust.at[gc]
        conv_silu(c, _XB_COLS, CONV_DIM, u_ref)
        dt, pre, suf = dt_rows(c)
        rowsf = jnp.concatenate([pre[0:h, :], suf[h:2 * h, :]], axis=0)
        colsf = col_forms(rowsf)
        ecol = jnp.exp(colsf)
        for g in range(SSD_GROUPS):
            bm = xb[:, SSD_INNER + g * SSD_STATE:SSD_INNER + (g + 1) * SSD_STATE]
            cm = u_ref[:, g * SSD_STATE:(g + 1) * SSD_STATE]
            cmb = cm.astype(BF16)
            cbm = lax.dot_general(cmb, bm.astype(BF16), (((1,), (1,)), ((), ())),
                                  preferred_element_type=F32)
            for pr in range(2):
                h0 = g * 4 + pr * 2
                c0 = h0 * SSD_HEAD_DIM
                lhs_parts = []
                for hh in (h0, h0 + 1):
                    segf = colsf[:, hh:hh + 1] - rowsf[hh:hh + 1, :]
                    gf = jnp.exp(jnp.where(ci <= ri, segf, NEG)) * dt[hh:hh + 1, :]
                    segb = colsf[:, h + hh:h + hh + 1] - rowsf[h + hh:h + hh + 1, :]
                    gb = jnp.exp(jnp.where(ci >= ri, segb, NEG)) * dt[h + hh:h + hh + 1, :]
                    lhs_parts.append((cbm * (gf + gb)).astype(BF16))
                for hh in (h0, h0 + 1):
                    lhs_parts.append((cm * ecol[:, hh:hh + 1]).astype(BF16))
                for hh in (h0, h0 + 1):
                    lhs_parts.append((cm * ecol[:, h + hh:h + hh + 1]).astype(BF16))
                lhs = jnp.concatenate(lhs_parts, axis=1)
                xp = xb[:, c0:c0 + LANES]
                sf = hf[:, c0:c0 + LANES]
                sb = hbs[gc, :, c0:c0 + LANES].astype(F32)
                rhs = jnp.concatenate(
                    [jnp.where(lo_mask, xp, 0.0), jnp.where(lo_mask, 0.0, xp),
                     jnp.where(lo_mask, sf, 0.0), jnp.where(lo_mask, 0.0, sf),
                     jnp.where(lo_mask, sb, 0.0), jnp.where(lo_mask, 0.0, sb)], axis=0).astype(BF16)
                ypair = jnp.dot(lhs, rhs, preferred_element_type=F32)
                ypair = ypair + dskip_ref[:, c0:c0 + LANES] * xp
                zz = z_ref[0, pl.ds(r0, CHUNK), c0:c0 + LANES]
                u_ref[:, _Y_OFF + c0:_Y_OFF + c0 + LANES] = ypair * _silu(zz)
        last = pre[:, LANES - 1:LANES]
        w_rows = jnp.exp(last - pre) * dt
        state_update(hf, xb, w_rows[0:h, :], last[0:h, :])
        if need_y:
            yv = u_ref[:, _Y_OFF:_Y_OFF + SSD_INNER]
            ms = jnp.mean(yv * yv, axis=-1, keepdims=True)
            y_ref[0, pl.ds(r0, CHUNK), :] = (yv * lax.rsqrt(ms + EPS) * gain_ref[...]).astype(y_ref.dtype)
        return carry

    @pl.when(sw == 0)
    def _():
        lax.fori_loop(0, cps, backward_chunk, 0)

    @pl.when(sw == 1)
    def _():
        lax.fori_loop(0, cps, forward_chunk, 0)

    @pl.when((sw == 0) & (s == nsteps - 1))
    def _():
        hb_out[0] = hb[...]

    @pl.when((sw == 1) & (s == nsteps - 1))
    def _():
        hf_out[0] = hf[...]


def _ssd(xbc, dts, z, consts, h0f, h0b, *, rows, need_y):
    bt, seq, _ = xbc.shape
    nsteps = seq // rows
    cps = rows // CHUNK
    hb8 = rows // SUBLANES
    nb8 = seq // SUBLANES
    cw, cb, dskip, gain, expm = consts
    dt_spec = pl.BlockSpec((cps, 2 * SSD_HEADS, CHUNK), lambda b, sw, s: (b * nsteps + blk(b, sw, s), 0, 0))

    def blk(b, sw, s):
        return jnp.where(sw == 0, nsteps - 1 - s, s)

    def full(a):
        return pl.BlockSpec(a.shape, lambda b, sw, s: (0,) * a.ndim)

    in_specs = [
        pl.BlockSpec((1, rows, CONV_DIM), lambda b, sw, s: (b, blk(b, sw, s), 0)),
        pl.BlockSpec((1, SUBLANES, CONV_DIM), lambda b, sw, s: (b, jnp.maximum(blk(b, sw, s) * hb8 - 1, 0), 0)),
        pl.BlockSpec((1, SUBLANES, CONV_DIM),
                     lambda b, sw, s: (b, jnp.minimum((blk(b, sw, s) + 1) * hb8, nb8 - 1), 0)),
        dt_spec, dt_spec, dt_spec,
        pl.BlockSpec((1, rows, SSD_INNER), lambda b, sw, s: (b, jnp.where(sw == 0, 0, s), 0)),
        full(cw), full(cb), full(dskip), full(gain), full(expm),
        pl.BlockSpec((1, SSD_STATE, SSD_INNER), lambda b, sw, s: (b, 0, 0)),
        pl.BlockSpec((1, SSD_STATE, SSD_INNER), lambda b, sw, s: (b, 0, 0)),
    ]
    st_spec = pl.BlockSpec((1, SSD_STATE, SSD_INNER), lambda b, sw, s: (b, 0, 0))
    st_shape = jax.ShapeDtypeStruct((bt, SSD_STATE, SSD_INNER), F32)
    out_specs, out_shapes = [st_spec, st_spec], [st_shape, st_shape]
    if need_y:
        out_specs = [pl.BlockSpec((1, rows, SSD_INNER), lambda b, sw, s: (b, jnp.where(sw == 0, 0, s), 0))] + out_specs
        out_shapes = [jax.ShapeDtypeStruct((bt, seq, SSD_INNER), BF16)] + out_shapes
    scratch = [
        pltpu.VMEM((CHUNK + 2 * SUBLANES, CONV_DIM), F32),
        pltpu.VMEM((CHUNK, _Y_OFF + SSD_INNER), F32),
        pltpu.VMEM((SSD_STATE, SSD_INNER), F32),
        pltpu.VMEM((SSD_STATE, SSD_INNER), F32),
        pltpu.VMEM((seq // CHUNK, SSD_STATE, SSD_INNER), BF16),
        pltpu.VMEM((seq // CHUNK, CHUNK, _XB_COLS), F32),
    ]
    return pl.pallas_call(
        functools.partial(_ssd_kernel, need_y=need_y, nsteps=nsteps, cps=cps),
        grid=(bt, 2, nsteps),
        in_specs=in_specs,
        out_specs=out_specs,
        out_shape=out_shapes,
        scratch_shapes=scratch,
        compiler_params=pltpu.CompilerParams(dimension_semantics=("arbitrary", "arbitrary", "arbitrary")),
        name="ssd_y" if need_y else "ssd_ctx",
    )(xbc, xbc, xbc, *dts, z, cw, cb, dskip, gain, expm, h0f, h0b)


def _attn_kernel(sink_ref, q_ref, kp_ref, kc_ref, kn_ref, vp_ref, vc_ref, vn_ref, kx_ref, vx_ref, o_ref, *, nsteps):
    m = pl.program_id(1)
    lane = lax.broadcasted_iota(jnp.int32, (1, LANES), 1)
    lo = lane < HEAD_DIM
    rq = Q_PER_KV * WINDOW
    per = ROWS_ATT // WINDOW
    qi = lax.broadcasted_iota(jnp.int32, (rq, WINDOW), 0) & (WINDOW - 1)
    kj = lax.broadcasted_iota(jnp.int32, (rq, WINDOW), 1)
    rblk = lax.shift_right_logical(lax.broadcasted_iota(jnp.int32, (rq, 1), 0), int(math.log2(WINDOW)))
    zero = jnp.zeros((), BF16)
    band_prev = jnp.where(kj >= qi, 0.0, NEG)
    band_next = jnp.where(kj <= qi, 0.0, NEG)

    kpieces = [kp_ref[0]] + [kc_ref[0, w * WINDOW:(w + 1) * WINDOW] for w in range(per)] + [kn_ref[0]]
    vpieces = [vp_ref[0]] + [vc_ref[0, w * WINDOW:(w + 1) * WINDOW] for w in range(per)] + [vn_ref[0]]
    kx = kx_ref[0]
    vx = vx_ref[0]
    nt = (((1,), (1,)), ((), ()))

    for sb in range(per):
        qsb = q_ref[0, sb * WINDOW:(sb + 1) * WINDOW, :]
        mask_prev = band_prev if sb > 0 else jnp.where(m > 0, band_prev, NEG)
        mask_next = band_next if sb < per - 1 else jnp.where(m < nsteps - 1, band_next, NEG)
        outs = [None] * Q_PER_KV
        for kv in range(ATTN_KV_HEADS):
            sel = lo if kv == 0 else jnp.logical_not(lo)
            qs = jnp.concatenate(
                [jnp.where(sel, qsb[:, j * LANES:(j + 1) * LANES], zero) for j in range(Q_PER_KV)], axis=0)
            s_prev = lax.dot_general(qs, kpieces[sb], nt, preferred_element_type=F32) + mask_prev
            s_cur = lax.dot_general(qs, kpieces[sb + 1], nt, preferred_element_type=F32)
            s_next = lax.dot_general(qs, kpieces[sb + 2], nt, preferred_element_type=F32) + mask_next
            s_ctx = lax.dot_general(qs, kx, nt, preferred_element_type=F32)
            sink = jnp.zeros((rq, 1), F32)
            for j in range(Q_PER_KV):
                sink = jnp.where(rblk == j, sink_ref[kv * Q_PER_KV + j], sink)
            n_ct = s_ctx.shape[1] // LANES
            tile_max = jnp.maximum(jnp.maximum(s_prev, s_cur), s_next)
            for t in range(n_ct):
                tile_max = jnp.maximum(tile_max, s_ctx[:, t * LANES:(t + 1) * LANES])
            mx = jnp.maximum(jnp.max(tile_max, axis=-1, keepdims=True), sink)
            p_prev = jnp.exp2(s_prev - mx)
            p_cur = jnp.exp2(s_cur - mx)
            p_next = jnp.exp2(s_next - mx)
            p_ctx = jnp.exp2(s_ctx - mx)
            tile_sum = p_prev + p_cur + p_next
            for t in range(n_ct):
                tile_sum = tile_sum + p_ctx[:, t * LANES:(t + 1) * LANES]
            den = jnp.sum(tile_sum, axis=-1, keepdims=True) + jnp.exp2(sink - mx)
            acc = jnp.dot(p_prev.astype(BF16), jnp.where(sel, vpieces[sb], zero), preferred_element_type=F32)
            acc += jnp.dot(p_cur.astype(BF16), jnp.where(sel, vpieces[sb + 1], zero), preferred_element_type=F32)
            acc += jnp.dot(p_next.astype(BF16), jnp.where(sel, vpieces[sb + 2], zero), preferred_element_type=F32)
            acc += jnp.dot(p_ctx.astype(BF16), jnp.where(sel, vx, zero), preferred_element_type=F32)
            acc = acc * (1.0 / den)
            for j in range(Q_PER_KV):
                part = acc[j * WINDOW:(j + 1) * WINDOW, :]
                outs[j] = part if outs[j] is None else outs[j] + part
        for j in range(Q_PER_KV):
            o_ref[0, sb * WINDOW:(sb + 1) * WINDOW, j * LANES:(j + 1) * LANES] = outs[j].astype(o_ref.dtype)


def _attention(sinks, q, k, v, kx, vx):
    bt, seq, _ = q.shape
    nsteps = seq // ROWS_ATT
    per = ROWS_ATT // WINDOW
    nb = seq // WINDOW
    n_ctx = kx.shape[1]

    def prev(b, m, s):
        return (b, jnp.maximum(m * per - 1, 0), 0)

    def cur(b, m, s):
        return (b, m, 0)

    def nxt(b, m, s):
        return (b, jnp.minimum((m + 1) * per, nb - 1), 0)

    small = (1, WINDOW, KV_WIDTH)
    big = (1, ROWS_ATT, KV_WIDTH)
    grid_spec = pltpu.PrefetchScalarGridSpec(
        num_scalar_prefetch=1,
        grid=(bt, nsteps),
        in_specs=[
            pl.BlockSpec((1, ROWS_ATT, ATTN_WIDTH), cur),
            pl.BlockSpec(small, prev), pl.BlockSpec(big, cur), pl.BlockSpec(small, nxt),
            pl.BlockSpec(small, prev), pl.BlockSpec(big, cur), pl.BlockSpec(small, nxt),
            pl.BlockSpec((1, n_ctx, KV_WIDTH), lambda b, m, s: (b, 0, 0)),
            pl.BlockSpec((1, n_ctx, KV_WIDTH), lambda b, m, s: (b, 0, 0)),
        ],
        out_specs=pl.BlockSpec((1, ROWS_ATT, ATTN_WIDTH), cur),
    )
    return pl.pallas_call(
        functools.partial(_attn_kernel, nsteps=nsteps),
        grid_spec=grid_spec,
        out_shape=jax.ShapeDtypeStruct((bt, seq, ATTN_WIDTH), BF16),
        name="attn",
    )(sinks, q, k, k, k, v, v, v, kx, vx)


def _outproj_kernel(ssd_ref, att_ref, x_ref, g1_ref, sh_ref, sc_ref, n2_ref, wa_ref, wb_ref, wr_ref, br_ref,
                    tri_ref, x1_ref, h2_ref, ids_ref, wcol_ref, cnt_ref, carry, *, nsteps):
    i = pl.program_id(0)
    rows = x_ref.shape[0]

    @pl.when(i == 0)
    def _():
        carry[...] = jnp.zeros_like(carry)

    acc = jnp.dot(ssd_ref[...], wa_ref[...], preferred_element_type=F32)
    acc += jnp.dot(att_ref[...], wb_ref[...], preferred_element_type=F32)
    x1 = x_ref[...] + g1_ref[0] * acc
    x1_ref[...] = x1
    ms = jnp.mean(x1 * x1, axis=-1, keepdims=True)
    h2 = (x1 * lax.rsqrt(ms + EPS)) * (n2_ref[...] * (1.0 + sc_ref[0])) + sh_ref[0]
    h2b = h2.astype(BF16)
    half = h2.shape[1] // 2
    lo_bits = lax.shift_right_logical(pltpu.bitcast(h2b[:, :half].astype(F32), jnp.uint32), jnp.uint32(16))
    hi_bits = pltpu.bitcast(h2b[:, half:].astype(F32), jnp.uint32) & jnp.uint32(0xFFFF0000)
    h2_ref[...] = hi_bits | lo_bits
    lt = lax.dot_general(wr_ref[...], h2b, (((1,), (1,)), ((), ())),
                         preferred_element_type=F32) + br_ref[...]
    row = lax.broadcasted_iota(jnp.int32, lt.shape, 0).astype(F32)
    big_i = float(ROUTE_ROWS)
    gl = jnp.where(row < N_GROUPS, lt, NEG)
    gmax = jnp.max(gl, axis=0, keepdims=True)
    gidx = jnp.min(jnp.where(gl == gmax, row, big_i), axis=0, keepdims=True)
    g_w = 1.0 / jnp.sum(jnp.exp(gl - gmax), axis=0, keepdims=True)
    lo = N_GROUPS + EXPERTS_PER_GROUP * gidx
    el = jnp.where((row >= lo) & (row < lo + EXPERTS_PER_GROUP), lt, NEG)
    m1 = jnp.max(el, axis=0, keepdims=True)
    i1 = jnp.min(jnp.where(el == m1, row, big_i), axis=0, keepdims=True)
    el2 = jnp.where(row == i1, NEG, el)
    m2 = jnp.max(el2, axis=0, keepdims=True)
    i2 = jnp.min(jnp.where(el2 == m2, row, big_i), axis=0, keepdims=True)
    r = jnp.exp(m2 - m1)
    w1 = g_w / (1.0 + r)
    w2 = g_w * r / (1.0 + r)
    oh1 = row == i1
    oh2 = row == i2
    cnt = jnp.where(oh1 | oh2, 1.0, 0.0)
    prefix = jnp.dot(cnt.astype(BF16), tri_ref[...], preferred_element_type=F32)
    base = carry[...] + prefix
    rank1 = jnp.sum(jnp.where(oh1, base, 0.0), axis=0, keepdims=True)
    rank2 = jnp.sum(jnp.where(oh2, base, 0.0), axis=0, keepdims=True)
    carry[...] = carry[...] + jnp.sum(cnt, axis=1, keepdims=True)
    r8 = lax.broadcasted_iota(jnp.int32, (SUBLANES, rows), 0)
    code1 = (i1 - N_GROUPS) * float(1 << RANK_BITS) + rank1
    code2 = (i2 - N_GROUPS) * float(1 << RANK_BITS) + rank2
    ids_ref[...] = jnp.where(r8 == 0, code1, jnp.where(r8 == 1, code2, 0.0)).astype(jnp.int32)
    r128 = lax.broadcasted_iota(jnp.int32, (LANES, rows), 0)
    wfull = jnp.where(r128 == 0, w1, jnp.where(r128 == 1, w2, 0.0))
    for j in range(rows // LANES):
        wcol_ref[j * LANES:(j + 1) * LANES, :] = wfull[:, j * LANES:(j + 1) * LANES].T

    @pl.when(i == nsteps - 1)
    def _():
        cnt_ref[...] = carry[:, 0:LANES]


def _outproj(ssd, att, xf, mod3, norm2, wa, wb, wr, br, tri, *, rows, seq_blocks):
    t, d = xf.shape
    nsteps = t // rows

    def mod(col):
        return pl.BlockSpec((1, 1, d), lambda i: (i // seq_blocks, 0, col))

    def full(a):
        return pl.BlockSpec(a.shape, lambda i: (0,) * a.ndim)

    return pl.pallas_call(
        functools.partial(_outproj_kernel, nsteps=nsteps),
        grid=(nsteps,),
        in_specs=[pl.BlockSpec((rows, SSD_INNER), lambda i: (i, 0)),
                  pl.BlockSpec((rows, ATTN_WIDTH), lambda i: (i, 0)),
                  pl.BlockSpec((rows, d), lambda i: (i, 0)),
                  mod(2), mod(3), mod(4), full(norm2), full(wa), full(wb), full(wr), full(br), full(tri)],
        out_specs=[pl.BlockSpec((rows, d), lambda i: (i, 0)),
                   pl.BlockSpec((rows, d // 2), lambda i: (i, 0)),
                   pl.BlockSpec((SUBLANES, rows), lambda i: (0, i)),
                   pl.BlockSpec((rows, LANES), lambda i: (i, 0)),
                   pl.BlockSpec((ROUTE_ROWS, LANES), lambda i: (0, 0))],
        out_shape=[jax.ShapeDtypeStruct((t, d), F32),
                   jax.ShapeDtypeStruct((t, d // 2), jnp.uint32),
                   jax.ShapeDtypeStruct((SUBLANES, t), jnp.int32),
                   jax.ShapeDtypeStruct((t, LANES), F32),
                   jax.ShapeDtypeStruct((ROUTE_ROWS, LANES), F32)],
        scratch_shapes=[pltpu.VMEM((ROUTE_ROWS, rows), F32)],
        compiler_params=pltpu.CompilerParams(dimension_semantics=("arbitrary",)),
        name="outproj_router",
    )(ssd, att, xf, mod3, mod3, mod3, norm2, wa, wb, wr, br, tri)


def _expert_kernel(blk_e_ref, nused_ref, pstart_ref, counts_ref, eord_ref, enext_ref, dst_ref,
                   h2_hbm, wg_hbm, wu_hbm, wd_hbm, y_ref,
                   h2v, xg, wgs, wus, wds, wgb, wub, wdb, tok_ref, sem, wsem, *, n_tok, nblocks):
    b = pl.program_id(0)
    nused = nused_ref[0]
    slot = b % 2
    half = h2v.shape[1]

    def h2_copy():
        return pltpu.make_async_copy(h2_hbm, h2v, sem.at[0])

    def weight_copies(e, sl):
        return (pltpu.make_async_copy(wg_hbm.at[e], wgs.at[sl], wsem.at[sl, 0]),
                pltpu.make_async_copy(wu_hbm.at[e], wus.at[sl], wsem.at[sl, 1]),
                pltpu.make_async_copy(wd_hbm.at[e], wds.at[sl], wsem.at[sl, 2]))

    def gather_rows(blk, sl):
        for r in range(MOE_BLK):
            xg[sl, pl.ds(r, 1), :] = h2v[pl.ds(tok_ref[blk * MOE_BLK + r], 1), :]

    @pl.when(b == 0)
    def _():
        h2_copy().start()
        for cp in weight_copies(blk_e_ref[0], 0):
            cp.start()
        def pad_expert(e, carry):
            lo = pstart_ref[e] + counts_ref[e]
            hi = pstart_ref[e] + (counts_ref[e] + MOE_BLK - 1) // MOE_BLK * MOE_BLK

            def pad_slot(s, c):
                tok_ref[s] = 0
                return c
            return lax.fori_loop(lo, hi, pad_slot, carry)
        lax.fori_loop(0, N_EXPERTS, pad_expert, 0)

        for k in range(2):
            def claim(t, carry, k=k):
                tok_ref[dst_ref[k * n_tok + t]] = t
                return carry
            lax.fori_loop(0, n_tok, claim, 0, unroll=16)
        h2_copy().wait()
        gather_rows(0, 0)

    e = blk_e_ref[b]
    e_prev = blk_e_ref[jnp.maximum(b - 1, 0)]

    @pl.when((b == 0) | (e != e_prev))
    def _():
        sl = eord_ref[e] % 2
        for cp in weight_copies(e, sl):
            cp.wait()
        wgb[...] = wgs[sl].astype(BF16)
        wub[...] = wus[sl].astype(BF16)
        wdb[...] = wds[sl].astype(BF16)
        e_next = enext_ref[e]

        @pl.when(e_next >= 0)
        def _():
            for cp in weight_copies(e_next, 1 - sl):
                cp.start()

    @pl.when(b < nused)
    def _():
        words = xg[slot]
        x_lo = pltpu.bitcast(words << jnp.uint32(16), F32).astype(BF16)
        x_hi = pltpu.bitcast(words & jnp.uint32(0xFFFF0000), F32).astype(BF16)
        gate = (jnp.dot(x_lo, wgb[0:half, :], preferred_element_type=F32)
                + jnp.dot(x_hi, wgb[half:, :], preferred_element_type=F32))
        up = (jnp.dot(x_lo, wub[0:half, :], preferred_element_type=F32)
              + jnp.dot(x_hi, wub[half:, :], preferred_element_type=F32))
        hmid = (_silu(gate) * up).astype(BF16)
        y_ref[...] = jnp.dot(hmid, wdb[...], preferred_element_type=F32)
        gather_rows(jnp.minimum(b + 1, nused - 1), 1 - slot)

    @pl.when(b >= nused)
    def _():
        y_ref[...] = jnp.zeros_like(y_ref)


def _experts(blk_e, nused, pstart, counts, eord, enext, dst, h2p, w_gate, w_up, w_down, *, nblocks, n_tok):
    d = w_gate.shape[1]
    hbm = pl.BlockSpec(memory_space=pl.ANY)
    grid_spec = pltpu.PrefetchScalarGridSpec(
        num_scalar_prefetch=7,
        grid=(nblocks,),
        in_specs=[hbm, hbm, hbm, hbm],
        out_specs=pl.BlockSpec((MOE_BLK, d), lambda b, *_: (b, 0)),
        scratch_shapes=[
            pltpu.VMEM(h2p.shape, jnp.uint32),
            pltpu.VMEM((2, MOE_BLK, d // 2), jnp.uint32),
            pltpu.VMEM((2, d, EXPERT_DIM), F32),
            pltpu.VMEM((2, d, EXPERT_DIM), F32),
            pltpu.VMEM((2, EXPERT_DIM, d), F32),
            pltpu.VMEM((d, EXPERT_DIM), BF16),
            pltpu.VMEM((d, EXPERT_DIM), BF16),
            pltpu.VMEM((EXPERT_DIM, d), BF16),
            pltpu.SMEM((nblocks * MOE_BLK,), jnp.int32),
            pltpu.SemaphoreType.DMA((1,)),
            pltpu.SemaphoreType.DMA((2, 3)),
        ],
    )
    return pl.pallas_call(
        functools.partial(_expert_kernel, n_tok=n_tok, nblocks=nblocks),
        grid_spec=grid_spec,
        out_shape=jax.ShapeDtypeStruct((nblocks * MOE_BLK, d), F32),
        compiler_params=pltpu.CompilerParams(dimension_semantics=("arbitrary",)),
        name="experts",
    )(blk_e, nused, pstart, counts, eord, enext, dst, h2p, w_gate, w_up, w_down)


def _combine_kernel(dst_ref, y_hbm, x1_ref, wcol_ref, g2_ref, nf_ref, o_ref, ybuf, sem, *, nsteps, n_tok):
    i = pl.program_id(0)
    rows = x1_ref.shape[0]
    slot = i % 2

    def row_copy(src, r, k, sl):
        return pltpu.make_async_copy(y_hbm.at[pl.ds(src, 1)], ybuf.at[sl, k, pl.ds(r, 1)], sem.at[sl])

    def start_rows(step, sl):
        for r in range(rows):
            for k in range(2):
                row_copy(dst_ref[k * n_tok + step * rows + r], r, k, sl).start()

    def wait_rows(sl):
        for r in range(rows):
            for k in range(2):
                row_copy(0, r, k, sl).wait()

    @pl.when(i == 0)
    def _():
        start_rows(0, 0)

    @pl.when(i + 1 < nsteps)
    def _():
        start_rows(i + 1, 1 - slot)

    wait_rows(slot)
    moe = ybuf[slot, 0] * wcol_ref[:, 0:1] + ybuf[slot, 1] * wcol_ref[:, 1:2]
    x2 = x1_ref[...] + g2_ref[0] * moe
    ms = jnp.mean(x2 * x2, axis=-1, keepdims=True)
    o_ref[...] = x2 * lax.rsqrt(ms + EPS) * nf_ref[...]


def _combine(dst, y_buf, x1, wcol, mod3, norm_final, *, rows, seq_blocks):
    t, d = x1.shape
    nsteps = t // rows
    grid_spec = pltpu.PrefetchScalarGridSpec(
        num_scalar_prefetch=1,
        grid=(nsteps,),
        in_specs=[
            pl.BlockSpec(memory_space=pl.ANY),
            pl.BlockSpec((rows, d), lambda i, ds: (i, 0)),
            pl.BlockSpec((rows, LANES), lambda i, ds: (i, 0)),
            pl.BlockSpec((1, 1, d), lambda i, ds: (i // seq_blocks, 0, 5)),
            pl.BlockSpec((1, d), lambda i, ds: (0, 0)),
        ],
        out_specs=pl.BlockSpec((rows, d), lambda i, ds: (i, 0)),
        scratch_shapes=[pltpu.VMEM((2, 2, rows, d), F32), pltpu.SemaphoreType.DMA((2,))],
    )
    return pl.pallas_call(
        functools.partial(_combine_kernel, nsteps=nsteps, n_tok=t),
        grid_spec=grid_spec,
        out_shape=jax.ShapeDtypeStruct((t, d), F32),
        compiler_params=pltpu.CompilerParams(dimension_semantics=("arbitrary",)),
        name="combine",
    )(dst, y_buf, x1, wcol, mod3, norm_final)


def _rope_tables(seq):
    pos = np.arange(seq)
    n_freq = HEAD_DIM // 4
    inv = ROPE_BASE ** (-np.arange(n_freq, dtype=np.float32) / n_freq)
    ang = np.concatenate([(pos // GRID_W)[:, None] * inv, (pos % GRID_W)[:, None] * inv], axis=-1)
    ang = np.concatenate([ang, ang, ang, ang], axis=-1).astype(np.float32)
    return jnp.asarray(np.cos(ang), F32), jnp.asarray(np.sin(ang), F32)


def _head_expand_mat():
    m = np.zeros((SSD_HEADS, SSD_INNER), np.float32)
    for hh in range(SSD_HEADS):
        m[hh, hh * SSD_HEAD_DIM:(hh + 1) * SSD_HEAD_DIM] = 1.0
    return jnp.asarray(m)


def kernel(x, c, ctx, c_ctx, w_ada, b_ada, norm1, w_in, conv_w, conv_b, dt_bias, a_log, d_skip, ssd_norm,
           attn_sinks, w_out, norm2, w_group, b_group, w_expert, b_expert, w_gate, w_up, w_down, norm_final):
    return _pipeline(x, c, ctx, c_ctx, w_ada, b_ada, norm1, w_in, conv_w, conv_b, dt_bias, a_log, d_skip,
                     ssd_norm, attn_sinks, w_out, norm2, w_group, b_group, w_expert, b_expert, w_gate, w_up,
                     w_down, norm_final)["out"]


def _pipeline(x, c, ctx, c_ctx, w_ada, b_ada, norm1, w_in, conv_w, conv_b, dt_bias, a_log, d_skip, ssd_norm,
              attn_sinks, w_out, norm2, w_group, b_group, w_expert, b_expert, w_gate, w_up, w_down, norm_final):
    bt, seq, d = x.shape
    n_ctx = ctx.shape[1]
    t = bt * seq
    layer = 0
    assert w_ada.shape[0] == 1 and seq % ROWS_IN == 0 and n_ctx % CHUNK == 0

    cc = jnp.zeros((SUBLANES, d), F32).at[:bt].set(c).at[bt].set(c_ctx)
    mod = _ada(cc, w_ada[layer], b_ada[layer][None, :])
    mod3 = mod.reshape(SUBLANES, 1, 6 * d)

    w = w_in[layer]
    o_z, o_xbc, o_dt = 0, SSD_INNER, SSD_INNER + CONV_DIM
    o_q = o_dt + 2 * SSD_HEADS
    o_k, o_v = o_q + ATTN_WIDTH, o_q + ATTN_WIDTH + KV_WIDTH
    q_perm = np.concatenate([np.r_[j * HEAD_DIM:(j + 1) * HEAD_DIM, (j + 4) * HEAD_DIM:(j + 5) * HEAD_DIM]
                             for j in range(Q_PER_KV)])
    w_q = w[:, o_q:o_q + ATTN_WIDTH].reshape(d, ATTN_KV_HEADS, Q_PER_KV, HEAD_DIM).transpose(0, 2, 1, 3)
    w_q = w_q.reshape(d, ATTN_WIDTH)
    w_cat = jnp.concatenate([w[:, o_z:o_z + SSD_INNER], w[:, o_xbc:o_xbc + CONV_DIM], w_q,
                             w[:, o_k:o_k + KV_WIDTH], w[:, o_v:o_v + KV_WIDTH]], axis=1).astype(BF16)
    w_dtT = w[:, o_dt:o_dt + 2 * SSD_HEADS].T.astype(BF16)

    seq_blocks = seq // ROWS_IN
    cos, sin = _rope_tables(seq)
    bias_rows = jnp.broadcast_to(dt_bias[layer].reshape(2 * SSD_HEADS, 1), (2 * SSD_HEADS, LANES))
    alog_rows = jnp.broadcast_to(a_log[layer].reshape(2 * SSD_HEADS, 1), (2 * SSD_HEADS, LANES))
    segs_x = [("z", _COL_Z, SSD_INNER), ("xbc", _COL_XBC, CONV_DIM), ("dtT", 0, 0),
              ("q", _COL_Q, ATTN_WIDTH), ("k", _COL_K, KV_WIDTH), ("v", _COL_V, KV_WIDTH)]
    z_x, xbc_x, dt_x, pre_x, suf_x, q_x, k_x, v_x = _inproj(
        x.reshape(t, d), mod3, norm1[layer][None, :], w_cat, w_dtT, bias_rows, alog_rows, (cos, sin),
        rows=ROWS_IN, mod_row_fn=lambda i: i // seq_blocks, segs=segs_x, seq_blocks=seq_blocks)
    segs_c = [("xbc", _COL_XBC, CONV_DIM), ("dtT", 0, 0), ("k", _COL_K, KV_WIDTH), ("v", _COL_V, KV_WIDTH)]
    xbc_c, dt_c, pre_c, suf_c, k_c, v_c = _inproj(
        ctx.reshape(bt * n_ctx, d), mod3, norm1[layer][None, :], w_cat, w_dtT, bias_rows, alog_rows, None,
        rows=n_ctx, mod_row_fn=lambda i: bt, segs=segs_c, seq_blocks=1)

    cw = jnp.zeros((SUBLANES, CONV_DIM), F32).at[:CONV_WIDTH].set(conv_w[layer])
    cb = conv_b[layer][None, :]
    dskip_e = jnp.repeat(d_skip[layer], SSD_HEAD_DIM)[None, :]
    consts = (cw, cb, dskip_e, ssd_norm[layer][None, :], _head_expand_mat())

    zeros_state = jnp.zeros((bt, SSD_STATE, SSD_INNER), F32)
    z_dummy = jnp.zeros((bt, n_ctx, SSD_INNER), F32)
    h_cf, h_cb = _ssd(xbc_c.reshape(bt, n_ctx, CONV_DIM), (dt_c, pre_c, suf_c), z_dummy, consts,
                      zeros_state, zeros_state, rows=n_ctx, need_y=False)
    ssd_x, _, _ = _ssd(xbc_x.reshape(bt, seq, CONV_DIM), (dt_x, pre_x, suf_x), z_x.reshape(bt, seq, SSD_INNER),
                       consts, h_cf, h_cb, rows=ROWS_SSD, need_y=True)

    attn_x = _attention(attn_sinks[layer].astype(F32) * LOG2E,
                        q_x.reshape(bt, seq, ATTN_WIDTH), k_x.reshape(bt, seq, KV_WIDTH),
                        v_x.reshape(bt, seq, KV_WIDTH), k_c.reshape(bt, n_ctx, KV_WIDTH),
                        v_c.reshape(bt, n_ctx, KV_WIDTH))

    wo = w_out[layer]
    wa = wo[:SSD_INNER].astype(BF16)
    wb = wo[SSD_INNER:].reshape(ATTN_KV_HEADS, Q_PER_KV, HEAD_DIM, d).transpose(1, 0, 2, 3)
    wb = wb.reshape(ATTN_WIDTH, d).astype(BF16)
    wr = jnp.zeros((ROUTE_ROWS, d), F32).at[:N_GROUPS].set(w_group[layer].T)
    wr = wr.at[N_GROUPS:N_GROUPS + N_EXPERTS].set(w_expert[layer].T).astype(BF16)
    br = jnp.zeros((ROUTE_ROWS,), F32).at[:N_GROUPS].set(b_group[layer])
    br = br.at[N_GROUPS:N_GROUPS + N_EXPERTS].set(b_expert[layer])
    br = jnp.broadcast_to(br[:, None], (ROUTE_ROWS, ROWS_IN))
    tri = jnp.asarray(np.triu(np.ones((ROWS_IN, ROWS_IN), np.float32), 1), BF16)
    x1, h2, ids, wcol, cnt = _outproj(
        ssd_x.reshape(t, SSD_INNER), attn_x.reshape(t, ATTN_WIDTH), x.reshape(t, d), mod3, norm2[layer][None, :],
        wa, wb, wr, br, tri, rows=ROWS_IN, seq_blocks=seq_blocks)

    counts = cnt[N_GROUPS:N_GROUPS + N_EXPERTS, 0].astype(jnp.int32)
    padded = (counts + MOE_BLK - 1) // MOE_BLK * MOE_BLK
    ends = jnp.cumsum(padded)
    pstart = ends - padded
    nblocks = (2 * t) // MOE_BLK + N_EXPERTS
    blk_start = jnp.arange(nblocks, dtype=jnp.int32) * MOE_BLK
    eids = jnp.arange(N_EXPERTS, dtype=jnp.int32)
    active = counts > 0
    last_active = jnp.max(jnp.where(active, eids, 0))
    blk_e = jnp.minimum(jnp.sum((ends[None, :] <= blk_start[:, None]).astype(jnp.int32), axis=1), last_active)
    nused = (ends[-1] // MOE_BLK).astype(jnp.int32).reshape(1)
    eord = jnp.cumsum(active.astype(jnp.int32)) - active.astype(jnp.int32)
    later = active[None, :] & (eids[None, :] > eids[:, None])
    enext = jnp.min(jnp.where(later, eids[None, :], N_EXPERTS), axis=1)
    enext = jnp.where(enext == N_EXPERTS, -1, enext).astype(jnp.int32)
    code = ids[0:2].reshape(-1)
    e_id = lax.shift_right_logical(code, RANK_BITS)
    onehot = e_id[:, None] == jnp.arange(N_EXPERTS, dtype=jnp.int32)[None, :]
    dst = jnp.sum(jnp.where(onehot, pstart[None, :], 0), axis=1) + (code & ((1 << RANK_BITS) - 1))

    y_buf = _experts(blk_e, nused, pstart, counts, eord, enext, dst, h2, w_gate[layer], w_up[layer],
                     w_down[layer], nblocks=nblocks, n_tok=t)
    out = _combine(dst, y_buf, x1, wcol, mod3, norm_final[None, :], rows=ROWS_CMB, seq_blocks=seq // ROWS_CMB)
    return dict(out=out.reshape(bt, seq, d), mod=mod, z=z_x, xbc=xbc_x, dt=dt_x, q=q_x, k=k_x, v=v_x,
                k_c=k_c, v_c=v_c, h_cf=h_cf, h_cb=h_cb, ssd=ssd_x, attn=attn_x, x1=x1, h2=h2, ids=ids,
                wcol=wcol, cnt=cnt, y_buf=y_buf, pstart=pstart, q_perm=q_perm)
```

```python
import functools
import math

import numpy as np
import jax
import jax.numpy as jnp
from jax import lax
from jax.experimental import pallas as pl
from jax.experimental.pallas import tpu as pltpu

F32 = jnp.float32
BF16 = jnp.bfloat16

D_MODEL = 1024
GRID_W = 64
EPS = 1e-6
SSD_INNER = 512
SSD_HEAD_DIM = 64
SSD_HEADS = 8
SSD_GROUPS = 2
SSD_STATE = 128
CONV_WIDTH = 5
CONV_DIM = SSD_INNER + 2 * SSD_GROUPS * SSD_STATE
CHUNK = 128
ATTN_WIDTH = 512
HEAD_DIM = 64
ATTN_Q_HEADS = 8
ATTN_KV_HEADS = 2
Q_PER_KV = 4
KV_WIDTH = 128
WINDOW = 128
ROPE_BASE = 10000.0
N_GROUPS = 4
EXPERTS_PER_GROUP = 8
N_EXPERTS = 32
EXPERT_DIM = 512

LANES = 128
SUBLANES = 8
NEG = -1e30
LOG2E = math.log2(math.e)

ROWS_IN = 512
ROWS_SSD = 1024
ROWS_ATT = 512
ROUTE_ROWS = 48
MOE_BLK = 256
RANK_BITS = 16
D_TILES = D_MODEL // LANES
ROWS_CMB = 128


def _silu(v):
    half = 0.5 * v
    return half + half * jnp.tanh(half)


def _softplus(v):
    return jnp.maximum(v, 0.0) + jnp.log(1.0 + jnp.exp(-jnp.abs(v)))


def _ada_kernel(c_ref, w_ref, b_ref, o_ref):
    a = _silu(c_ref[...])
    o_ref[...] = jnp.dot(a.astype(BF16), w_ref[...].astype(BF16),
                         preferred_element_type=F32) + b_ref[...]


def _ada(cc, w_ada, b_ada):
    d, n = w_ada.shape
    tn = 1536
    return pl.pallas_call(
        _ada_kernel,
        grid=(n // tn,),
        in_specs=[pl.BlockSpec((SUBLANES, d), lambda j: (0, 0)),
                  pl.BlockSpec((d, tn), lambda j: (0, j)),
                  pl.BlockSpec((1, tn), lambda j: (0, j))],
        out_specs=pl.BlockSpec((SUBLANES, tn), lambda j: (0, j)),
        out_shape=jax.ShapeDtypeStruct((SUBLANES, n), F32),
        name="ada",
    )(cc, w_ada, b_ada)


def _rope(t, cos, sin_signed, first_half):
    rot = jnp.where(first_half, pltpu.roll(t, LANES - HEAD_DIM // 2, 1), pltpu.roll(t, HEAD_DIM // 2, 1))
    return t * cos + rot * sin_signed


def _lane_cumsum(v, lane, reverse):
    k = 1
    while k < LANES:
        if reverse:
            v = v + jnp.where(lane < LANES - k, pltpu.roll(v, LANES - k, 1), 0.0)
        else:
            v = v + jnp.where(lane >= k, pltpu.roll(v, k, 1), 0.0)
        k *= 2
    return v


def _inproj_kernel(*refs, segs, rope):
    x_ref, sh_ref, sc_ref, g_ref, w_ref, wdt_ref, bias_ref, alog_ref = refs[:8]
    pos = 8
    if rope:
        cos_ref, sin_ref = refs[8:10]
        pos = 10
    outs = iter(refs[pos:])
    x = x_ref[...]
    ms = jnp.mean(x * x, axis=-1, keepdims=True)
    h = (x * lax.rsqrt(ms + EPS)) * (g_ref[...] * (1.0 + sc_ref[0])) + sh_ref[0]
    hb = h.astype(BF16)
    if rope:
        cos = cos_ref[...]
        lane = lax.broadcasted_iota(jnp.int32, cos.shape, 1)
        first_half = (lane & (HEAD_DIM - 1)) < HEAD_DIM // 2
        sin_s = jnp.where(first_half, -sin_ref[...], sin_ref[...])
    for kind, c0, width in segs:
        if kind == "dtT":
            dt_ref, pre_ref, suf_ref = next(outs), next(outs), next(outs)
            dtt = lax.dot_general(wdt_ref[...], hb, (((1,), (1,)), ((), ())), preferred_element_type=F32)
            lane16 = lax.broadcasted_iota(jnp.int32, (2 * SSD_HEADS, CHUNK), 1)
            neg_a = -jnp.exp(alog_ref[...])
            for j in range(dt_ref.shape[0]):
                dt = _softplus(dtt[:, j * CHUNK:(j + 1) * CHUNK] + bias_ref[...])
                a = dt * neg_a
                dt_ref[j] = dt
                pre_ref[j] = _lane_cumsum(a, lane16, False)
                suf_ref[j] = _lane_cumsum(a, lane16, True)
            continue
        o_ref = next(outs)
        for j0 in range(0, width, 512):
            wj = min(512, width - j0)
            acc = jnp.dot(hb, w_ref[:, c0 + j0:c0 + j0 + wj], preferred_element_type=F32)
            if rope and kind in ("q", "k"):
                scale = HEAD_DIM ** -0.5 * LOG2E if kind == "q" else 1.0
                for l0 in range(0, wj, LANES):
                    t = _rope(acc[:, l0:l0 + LANES], cos, sin_s, first_half)
                    o_ref[:, j0 + l0:j0 + l0 + LANES] = (t * scale).astype(o_ref.dtype)
            else:
                o_ref[:, j0:j0 + wj] = acc.astype(o_ref.dtype)


_COL_Z, _COL_XBC, _COL_Q, _COL_K, _COL_V = 0, 512, 1536, 2048, 2176
_W_COLS = 2304


def _inproj(xf, mod3, norm1, w_cat, w_dtT, bias_rows, alog_rows, rope_tabs, *, rows, mod_row_fn, segs, seq_blocks):
    t, d = xf.shape
    rope = rope_tabs is not None
    in_specs = [
        pl.BlockSpec((rows, d), lambda i: (i, 0)),
        pl.BlockSpec((1, 1, d), lambda i: (mod_row_fn(i), 0, 0)),
        pl.BlockSpec((1, 1, d), lambda i: (mod_row_fn(i), 0, 1)),
        pl.BlockSpec((1, d), lambda i: (0, 0)),
        pl.BlockSpec(w_cat.shape, lambda i: (0, 0)),
        pl.BlockSpec(w_dtT.shape, lambda i: (0, 0)),
        pl.BlockSpec(bias_rows.shape, lambda i: (0, 0)),
        pl.BlockSpec(alog_rows.shape, lambda i: (0, 0)),
    ]
    args = [xf, mod3, mod3, norm1, w_cat, w_dtT, bias_rows, alog_rows]
    if rope:
        in_specs += [pl.BlockSpec((rows, LANES), lambda i: (i % seq_blocks, 0))] * 2
        args += list(rope_tabs)
    out_specs, out_shapes = [], []
    for kind, _, width in segs:
        if kind == "dtT":
            for _ in range(3):
                out_specs.append(pl.BlockSpec((rows // CHUNK, 2 * SSD_HEADS, CHUNK), lambda i: (i, 0, 0)))
                out_shapes.append(jax.ShapeDtypeStruct((t // CHUNK, 2 * SSD_HEADS, CHUNK), F32))
        else:
            dt = BF16 if kind in ("q", "k", "v") else F32
            out_specs.append(pl.BlockSpec((rows, width), lambda i: (i, 0)))
            out_shapes.append(jax.ShapeDtypeStruct((t, width), dt))
    return pl.pallas_call(
        functools.partial(_inproj_kernel, segs=tuple(segs), rope=rope),
        grid=(t // rows,),
        in_specs=in_specs,
        out_specs=out_specs,
        out_shape=out_shapes,
        name="inproj_rope" if rope else "inproj_ctx",
    )(*args)


_XB_COLS = SSD_INNER + SSD_GROUPS * SSD_STATE
_Y_OFF = CONV_DIM - _XB_COLS


def _ssd_kernel(*refs, need_y, nsteps, cps):
    (xbc_ref, prev_ref, next_ref, dt_ref, pre_ref, suf_ref, z_ref, cw_ref, cb_ref,
     dskip_ref, gain_ref, exp_ref, h0f_ref, h0b_ref) = refs[:14]
    if need_y:
        y_ref, hf_out, hb_out = refs[14:17]
        rest = refs[17:]
    else:
        y_ref = None
        hf_out, hb_out = refs[14:16]
        rest = refs[16:]
    cwin, u_ref, hf, hb, hbs, ust = rest

    sw = pl.program_id(1)
    s = pl.program_id(2)
    bi = jnp.where(sw == 0, nsteps - 1 - s, s)
    rows = xbc_ref.shape[1]
    h = SSD_HEADS

    @pl.when((sw == 0) & (s == 0))
    def _():
        hb[...] = h0b_ref[0]

    @pl.when((sw == 1) & (s == 0))
    def _():
        hf[...] = h0f_ref[0]

    ri = lax.broadcasted_iota(jnp.int32, (CHUNK, CHUNK), 0)
    ci = lax.broadcasted_iota(jnp.int32, (CHUNK, CHUNK), 1)
    lane128 = lax.broadcasted_iota(jnp.int32, (CHUNK, LANES), 1)
    lo_mask = lane128 < SSD_HEAD_DIM

    def conv_silu(c, c0, c1, out_ref):
        r0 = pl.multiple_of(c * CHUNK, CHUNK)
        above = xbc_ref[0, pl.ds(pl.multiple_of(jnp.maximum(r0 - SUBLANES, 0), SUBLANES), SUBLANES), c0:c1]
        halo_a = jnp.where(bi > 0, prev_ref[0, :, c0:c1], 0.0)
        cwin[0:SUBLANES, c0:c1] = jnp.where(c > 0, above, halo_a)
        cwin[SUBLANES:SUBLANES + CHUNK, c0:c1] = xbc_ref[0, pl.ds(r0, CHUNK), c0:c1]
        below = xbc_ref[0, pl.ds(pl.multiple_of(jnp.minimum(r0 + CHUNK, rows - SUBLANES), SUBLANES), SUBLANES), c0:c1]
        halo_b = jnp.where(bi < nsteps - 1, next_ref[0, :, c0:c1], 0.0)
        cwin[SUBLANES + CHUNK:, c0:c1] = jnp.where(c < cps - 1, below, halo_b)
        for j0 in range(c0, c1, LANES):
            acc = jnp.broadcast_to(cb_ref[:, j0:j0 + LANES], (CHUNK, LANES))
            for k in range(CONV_WIDTH):
                off = SUBLANES - CONV_WIDTH // 2 + k
                acc = acc + cwin[off:off + CHUNK, j0:j0 + LANES] * cw_ref[k:k + 1, j0:j0 + LANES]
            out_ref[:, j0 - c0:j0 - c0 + LANES] = _silu(acc)

    def dt_rows(c):
        return dt_ref[c], pre_ref[c], suf_ref[c]

    def state_update(state_ref, xb, w_rows, tot_col):
        dec = jnp.exp(jnp.sum(tot_col * exp_ref[...], axis=0, keepdims=True))
        for g in range(SSD_GROUPS):
            bm = xb[:, SSD_INNER + g * SSD_STATE:SSD_INNER + (g + 1) * SSD_STATE]
            bt = bm.T
            for pr in range(2):
                h0 = g * 4 + pr * 2
                c0 = h0 * SSD_HEAD_DIM
                xp = xb[:, c0:c0 + LANES]
                rhs = jnp.concatenate([jnp.where(lo_mask, xp, 0.0), jnp.where(lo_mask, 0.0, xp)],
                                      axis=0).astype(BF16)
                lhs = jnp.concatenate([bt * w_rows[h0:h0 + 1, :], bt * w_rows[h0 + 1:h0 + 2, :]],
                                      axis=1).astype(BF16)
                sres = jnp.dot(lhs, rhs, preferred_element_type=F32)
                state_ref[:, c0:c0 + LANES] = state_ref[:, c0:c0 + LANES] * dec[:, c0:c0 + LANES] + sres

    def col_forms(rows16):
        padded = jnp.concatenate([rows16, jnp.zeros((CHUNK - 2 * h, LANES), F32)], axis=0)
        return padded.T

    def backward_chunk(k, carry):
        c = cps - 1 - k
        gc = bi * cps + c
        xb = ust.at[gc]
        conv_silu(c, 0, _XB_COLS, xb)
        dt, _, suf = dt_rows(c)
        hbs[gc] = hb[...].astype(BF16)
        s0 = suf[:, 0:1]
        w_rows = jnp.exp(s0 - suf) * dt
        state_update(hb, xb, w_rows[h:2 * h, :], s0[h:2 * h, :])
        return carry

    def forward_chunk(c, carry):
        gc = bi * cps + c
        r0 = pl.multiple_of(c * CHUNK, CHUNK)
        xb = ust.at[gc]
        conv_silu(c, _XB_COLS, CONV_DIM, u_ref)
        dt, pre, suf = dt_rows(c)
        rowsf = jnp.concatenate([pre[0:h, :], suf[h:2 * h, :]], axis=0)
        colsf = col_forms(rowsf)
        ecol = jnp.exp(colsf)
        for g in range(SSD_GROUPS):
            bm = xb[:, SSD_INNER + g * SSD_STATE:SSD_INNER + (g + 1) * SSD_STATE]
            cm = u_ref[:, g * SSD_STATE:(g + 1) * SSD_STATE]
            cmb = cm.astype(BF16)
            cbm = lax.dot_general(cmb, bm.astype(BF16), (((1,), (1,)), ((), ())),
                                  preferred_element_type=F32)
            for pr in range(2):
                h0 = g * 4 + pr * 2
                c0 = h0 * SSD_HEAD_DIM
                lhs_parts = []
                for hh in (h0, h0 + 1):
                    segf = colsf[:, hh:hh + 1] - rowsf[hh:hh + 1, :]
                    gf = jnp.exp(jnp.where(ci <= ri, segf, NEG)) * dt[hh:hh + 1, :]
                    segb = colsf[:, h + hh:h + hh + 1] - rowsf[h + hh:h + hh + 1, :]
                    gb = jnp.exp(jnp.where(ci >= ri, segb, NEG)) * dt[h + hh:h + hh + 1, :]
                    lhs_parts.append((cbm * (gf + gb)).astype(BF16))
                for hh in (h0, h0 + 1):
                    lhs_parts.append((cm * ecol[:, hh:hh + 1]).astype(BF16))
                for hh in (h0, h0 + 1):
                    lhs_parts.append((cm * ecol[:, h + hh:h + hh + 1]).astype(BF16))
                lhs = jnp.concatenate(lhs_parts, axis=1)
                xp = xb[:, c0:c0 + LANES]
                sf = hf[:, c0:c0 + LANES]
                sb = hbs[gc, :, c0:c0 + LANES].astype(F32)
                rhs = jnp.concatenate(
                    [jnp.where(lo_mask, xp, 0.0), jnp.where(lo_mask, 0.0, xp),
                     jnp.where(lo_mask, sf, 0.0), jnp.where(lo_mask, 0.0, sf),
                     jnp.where(lo_mask, sb, 0.0), jnp.where(lo_mask, 0.0, sb)], axis=0).astype(BF16)
                ypair = jnp.dot(lhs, rhs, preferred_element_type=F32)
                ypair = ypair + dskip_ref[:, c0:c0 + LANES] * xp
                zz = z_ref[0, pl.ds(r0, CHUNK), c0:c0 + LANES]
                u_ref[:, _Y_OFF + c0:_Y_OFF + c0 + LANES] = ypair * _silu(zz)
        last = pre[:, LANES - 1:LANES]
        w_rows = jnp.exp(last - pre) * dt
        state_update(hf, xb, w_rows[0:h, :], last[0:h, :])
        if need_y:
            yv = u_ref[:, _Y_OFF:_Y_OFF + SSD_INNER]
            ms = jnp.mean(yv * yv, axis=-1, keepdims=True)
            y_ref[0, pl.ds(r0, CHUNK), :] = (yv * lax.rsqrt(ms + EPS) * gain_ref[...]).astype(y_ref.dtype)
        return carry

    @pl.when(sw == 0)
    def _():
        lax.fori_loop(0, cps, backward_chunk, 0, unroll=2)

    @pl.when(sw == 1)
    def _():
        lax.fori_loop(0, cps, forward_chunk, 0, unroll=2)

    @pl.when((sw == 0) & (s == nsteps - 1))
    def _():
        hb_out[0] = hb[...]

    @pl.when((sw == 1) & (s == nsteps - 1))
    def _():
        hf_out[0] = hf[...]


def _ssd(xbc, dts, z, consts, h0f, h0b, *, rows, need_y):
    bt, seq, _ = xbc.shape
    nsteps = seq // rows
    cps = rows // CHUNK
    hb8 = rows // SUBLANES
    nb8 = seq // SUBLANES
    cw, cb, dskip, gain, expm = consts
    dt_spec = pl.BlockSpec((cps, 2 * SSD_HEADS, CHUNK), lambda b, sw, s: (b * nsteps + blk(b, sw, s), 0, 0))

    def blk(b, sw, s):
        return jnp.where(sw == 0, nsteps - 1 - s, s)

    def full(a):
        return pl.BlockSpec(a.shape, lambda b, sw, s: (0,) * a.ndim)

    in_specs = [
        pl.BlockSpec((1, rows, CONV_DIM), lambda b, sw, s: (b, blk(b, sw, s), 0)),
        pl.BlockSpec((1, SUBLANES, CONV_DIM), lambda b, sw, s: (b, jnp.maximum(blk(b, sw, s) * hb8 - 1, 0), 0)),
        pl.BlockSpec((1, SUBLANES, CONV_DIM),
                     lambda b, sw, s: (b, jnp.minimum((blk(b, sw, s) + 1) * hb8, nb8 - 1), 0)),
        dt_spec, dt_spec, dt_spec,
        pl.BlockSpec((1, rows, SSD_INNER), lambda b, sw, s: (b, jnp.where(sw == 0, 0, s), 0)),
        full(cw), full(cb), full(dskip), full(gain), full(expm),
        pl.BlockSpec((1, SSD_STATE, SSD_INNER), lambda b, sw, s: (b, 0, 0)),
        pl.BlockSpec((1, SSD_STATE, SSD_INNER), lambda b, sw, s: (b, 0, 0)),
    ]
    st_spec = pl.BlockSpec((1, SSD_STATE, SSD_INNER), lambda b, sw, s: (b, 0, 0))
    st_shape = jax.ShapeDtypeStruct((bt, SSD_STATE, SSD_INNER), F32)
    out_specs, out_shapes = [st_spec, st_spec], [st_shape, st_shape]
    if need_y:
        out_specs = [pl.BlockSpec((1, rows, SSD_INNER), lambda b, sw, s: (b, jnp.where(sw == 0, 0, s), 0))] + out_specs
        out_shapes = [jax.ShapeDtypeStruct((bt, seq, SSD_INNER), BF16)] + out_shapes
    scratch = [
        pltpu.VMEM((CHUNK + 2 * SUBLANES, CONV_DIM), F32),
        pltpu.VMEM((CHUNK, _Y_OFF + SSD_INNER), F32),
        pltpu.VMEM((SSD_STATE, SSD_INNER), F32),
        pltpu.VMEM((SSD_STATE, SSD_INNER), F32),
        pltpu.VMEM((seq // CHUNK, SSD_STATE, SSD_INNER), BF16),
        pltpu.VMEM((seq // CHUNK, CHUNK, _XB_COLS), F32),
    ]
    return pl.pallas_call(
        functools.partial(_ssd_kernel, need_y=need_y, nsteps=nsteps, cps=cps),
        grid=(bt, 2, nsteps),
        in_specs=in_specs,
        out_specs=out_specs,
        out_shape=out_shapes,
        scratch_shapes=scratch,
        compiler_params=pltpu.CompilerParams(dimension_semantics=("arbitrary", "arbitrary", "arbitrary")),
        name="ssd_y" if need_y else "ssd_ctx",
    )(xbc, xbc, xbc, *dts, z, cw, cb, dskip, gain, expm, h0f, h0b)


def _attn_kernel(sink_ref, q_ref, kp_ref, kc_ref, kn_ref, vp_ref, vc_ref, vn_ref, kx_ref, vx_ref, o_ref, *, nsteps):
    m = pl.program_id(1)
    lane = lax.broadcasted_iota(jnp.int32, (1, LANES), 1)
    lo = lane < HEAD_DIM
    rq = Q_PER_KV * WINDOW
    per = ROWS_ATT // WINDOW
    qi = lax.broadcasted_iota(jnp.int32, (rq, WINDOW), 0) & (WINDOW - 1)
    kj = lax.broadcasted_iota(jnp.int32, (rq, WINDOW), 1)
    rblk = lax.shift_right_logical(lax.broadcasted_iota(jnp.int32, (rq, 1), 0), int(math.log2(WINDOW)))
    zero = jnp.zeros((), BF16)
    band_prev = jnp.where(kj >= qi, 0.0, NEG)
    band_next = jnp.where(kj <= qi, 0.0, NEG)

    kpieces = [kp_ref[0]] + [kc_ref[0, w * WINDOW:(w + 1) * WINDOW] for w in range(per)] + [kn_ref[0]]
    vpieces = [vp_ref[0]] + [vc_ref[0, w * WINDOW:(w + 1) * WINDOW] for w in range(per)] + [vn_ref[0]]
    kx = kx_ref[0]
    vx = vx_ref[0]
    nt = (((1,), (1,)), ((), ()))

    for sb in range(per):
        qsb = q_ref[0, sb * WINDOW:(sb + 1) * WINDOW, :]
        mask_prev = band_prev if sb > 0 else jnp.where(m > 0, band_prev, NEG)
        mask_next = band_next if sb < per - 1 else jnp.where(m < nsteps - 1, band_next, NEG)
        outs = [None] * Q_PER_KV
        for kv in range(ATTN_KV_HEADS):
            sel = lo if kv == 0 else jnp.logical_not(lo)
            qs = jnp.concatenate(
                [jnp.where(sel, qsb[:, j * LANES:(j + 1) * LANES], zero) for j in range(Q_PER_KV)], axis=0)
            s_prev = lax.dot_general(qs, kpieces[sb], nt, preferred_element_type=F32) + mask_prev
            s_cur = lax.dot_general(qs, kpieces[sb + 1], nt, preferred_element_type=F32)
            s_next = lax.dot_general(qs, kpieces[sb + 2], nt, preferred_element_type=F32) + mask_next
            s_ctx = lax.dot_general(qs, kx, nt, preferred_element_type=F32)
            sink = jnp.zeros((rq, 1), F32)
            for j in range(Q_PER_KV):
                sink = jnp.where(rblk == j, sink_ref[kv * Q_PER_KV + j], sink)
            n_ct = s_ctx.shape[1] // LANES
            tile_max = jnp.maximum(jnp.maximum(s_prev, s_cur), s_next)
            for t in range(n_ct):
                tile_max = jnp.maximum(tile_max, s_ctx[:, t * LANES:(t + 1) * LANES])
            mx = jnp.maximum(jnp.max(tile_max, axis=-1, keepdims=True), sink)
            p_prev = jnp.exp2(s_prev - mx)
            p_cur = jnp.exp2(s_cur - mx)
            p_next = jnp.exp2(s_next - mx)
            p_ctx = jnp.exp2(s_ctx - mx)
            tile_sum = p_prev + p_cur + p_next
            for t in range(n_ct):
                tile_sum = tile_sum + p_ctx[:, t * LANES:(t + 1) * LANES]
            den = jnp.sum(tile_sum, axis=-1, keepdims=True) + jnp.exp2(sink - mx)
            acc = jnp.dot(p_prev.astype(BF16), jnp.where(sel, vpieces[sb], zero), preferred_element_type=F32)
            acc += jnp.dot(p_cur.astype(BF16), jnp.where(sel, vpieces[sb + 1], zero), preferred_element_type=F32)
            acc += jnp.dot(p_next.astype(BF16), jnp.where(sel, vpieces[sb + 2], zero), preferred_element_type=F32)
            acc += jnp.dot(p_ctx.astype(BF16), jnp.where(sel, vx, zero), preferred_element_type=F32)
            acc = acc * (1.0 / den)
            for j in range(Q_PER_KV):
                part = acc[j * WINDOW:(j + 1) * WINDOW, :]
                outs[j] = part if outs[j] is None else outs[j] + part
        for j in range(Q_PER_KV):
            o_ref[0, sb * WINDOW:(sb + 1) * WINDOW, j * LANES:(j + 1) * LANES] = outs[j].astype(o_ref.dtype)


def _attention(sinks, q, k, v, kx, vx):
    bt, seq, _ = q.shape
    nsteps = seq // ROWS_ATT
    per = ROWS_ATT // WINDOW
    nb = seq // WINDOW
    n_ctx = kx.shape[1]

    def prev(b, m, s):
        return (b, jnp.maximum(m * per - 1, 0), 0)

    def cur(b, m, s):
        return (b, m, 0)

    def nxt(b, m, s):
        return (b, jnp.minimum((m + 1) * per, nb - 1), 0)

    small = (1, WINDOW, KV_WIDTH)
    big = (1, ROWS_ATT, KV_WIDTH)
    grid_spec = pltpu.PrefetchScalarGridSpec(
        num_scalar_prefetch=1,
        grid=(bt, nsteps),
        in_specs=[
            pl.BlockSpec((1, ROWS_ATT, ATTN_WIDTH), cur),
            pl.BlockSpec(small, prev), pl.BlockSpec(big, cur), pl.BlockSpec(small, nxt),
            pl.BlockSpec(small, prev), pl.BlockSpec(big, cur), pl.BlockSpec(small, nxt),
            pl.BlockSpec((1, n_ctx, KV_WIDTH), lambda b, m, s: (b, 0, 0)),
            pl.BlockSpec((1, n_ctx, KV_WIDTH), lambda b, m, s: (b, 0, 0)),
        ],
        out_specs=pl.BlockSpec((1, ROWS_ATT, ATTN_WIDTH), cur),
    )
    return pl.pallas_call(
        functools.partial(_attn_kernel, nsteps=nsteps),
        grid_spec=grid_spec,
        out_shape=jax.ShapeDtypeStruct((bt, seq, ATTN_WIDTH), BF16),
        name="attn",
    )(sinks, q, k, k, k, v, v, v, kx, vx)


def _outproj_kernel(ssd_ref, att_ref, x_ref, g1_ref, sh_ref, sc_ref, n2_ref, wa_ref, wb_ref, wr_ref, br_ref,
                    tri_ref, x1_ref, h2_ref, ids_ref, wcol_ref, cnt_ref, carry, *, nsteps):
    i = pl.program_id(0)
    rows = x_ref.shape[0]

    @pl.when(i == 0)
    def _():
        carry[...] = jnp.zeros_like(carry)

    acc = jnp.dot(ssd_ref[...], wa_ref[...], preferred_element_type=F32)
    acc += jnp.dot(att_ref[...], wb_ref[...], preferred_element_type=F32)
    x1 = x_ref[...] + g1_ref[0] * acc
    x1_ref[...] = x1
    ms = jnp.mean(x1 * x1, axis=-1, keepdims=True)
    h2 = (x1 * lax.rsqrt(ms + EPS)) * (n2_ref[...] * (1.0 + sc_ref[0])) + sh_ref[0]
    h2b = h2.astype(BF16)
    half = h2.shape[1] // 2
    lo_bits = lax.shift_right_logical(pltpu.bitcast(h2b[:, :half].astype(F32), jnp.uint32), jnp.uint32(16))
    hi_bits = pltpu.bitcast(h2b[:, half:].astype(F32), jnp.uint32) & jnp.uint32(0xFFFF0000)
    h2_ref[...] = hi_bits | lo_bits
    lt = lax.dot_general(wr_ref[...], h2b, (((1,), (1,)), ((), ())),
                         preferred_element_type=F32) + br_ref[...]
    row = lax.broadcasted_iota(jnp.int32, lt.shape, 0).astype(F32)
    big_i = float(ROUTE_ROWS)
    gl = jnp.where(row < N_GROUPS, lt, NEG)
    gmax = jnp.max(gl, axis=0, keepdims=True)
    gidx = jnp.min(jnp.where(gl == gmax, row, big_i), axis=0, keepdims=True)
    g_w = 1.0 / jnp.sum(jnp.exp(gl - gmax), axis=0, keepdims=True)
    lo = N_GROUPS + EXPERTS_PER_GROUP * gidx
    el = jnp.where((row >= lo) & (row < lo + EXPERTS_PER_GROUP), lt, NEG)
    m1 = jnp.max(el, axis=0, keepdims=True)
    i1 = jnp.min(jnp.where(el == m1, row, big_i), axis=0, keepdims=True)
    el2 = jnp.where(row == i1, NEG, el)
    m2 = jnp.max(el2, axis=0, keepdims=True)
    i2 = jnp.min(jnp.where(el2 == m2, row, big_i), axis=0, keepdims=True)
    r = jnp.exp(m2 - m1)
    w1 = g_w / (1.0 + r)
    w2 = g_w * r / (1.0 + r)
    oh1 = row == i1
    oh2 = row == i2
    cnt = jnp.where(oh1 | oh2, 1.0, 0.0)
    prefix = jnp.dot(cnt.astype(BF16), tri_ref[...], preferred_element_type=F32)
    base = carry[...] + prefix
    rank1 = jnp.sum(jnp.where(oh1, base, 0.0), axis=0, keepdims=True)
    rank2 = jnp.sum(jnp.where(oh2, base, 0.0), axis=0, keepdims=True)
    carry[...] = carry[...] + jnp.sum(cnt, axis=1, keepdims=True)
    r8 = lax.broadcasted_iota(jnp.int32, (SUBLANES, rows), 0)
    code1 = (i1 - N_GROUPS) * float(1 << RANK_BITS) + rank1
    code2 = (i2 - N_GROUPS) * float(1 << RANK_BITS) + rank2
    ids_ref[...] = jnp.where(r8 == 0, code1, jnp.where(r8 == 1, code2, 0.0)).astype(jnp.int32)
    r128 = lax.broadcasted_iota(jnp.int32, (LANES, rows), 0)
    wfull = jnp.where(r128 == 0, w1, jnp.where(r128 == 1, w2, 0.0))
    for j in range(rows // LANES):
        wcol_ref[j * LANES:(j + 1) * LANES, :] = wfull[:, j * LANES:(j + 1) * LANES].T

    @pl.when(i == nsteps - 1)
    def _():
        cnt_ref[...] = carry[:, 0:LANES]


def _outproj(ssd, att, xf, mod3, norm2, wa, wb, wr, br, tri, *, rows, seq_blocks):
    t, d = xf.shape
    nsteps = t // rows

    def mod(col):
        return pl.BlockSpec((1, 1, d), lambda i: (i // seq_blocks, 0, col))

    def full(a):
        return pl.BlockSpec(a.shape, lambda i: (0,) * a.ndim)

    return pl.pallas_call(
        functools.partial(_outproj_kernel, nsteps=nsteps),
        grid=(nsteps,),
        in_specs=[pl.BlockSpec((rows, SSD_INNER), lambda i: (i, 0)),
                  pl.BlockSpec((rows, ATTN_WIDTH), lambda i: (i, 0)),
                  pl.BlockSpec((rows, d), lambda i: (i, 0)),
                  mod(2), mod(3), mod(4), full(norm2), full(wa), full(wb), full(wr), full(br), full(tri)],
        out_specs=[pl.BlockSpec((rows, d), lambda i: (i, 0)),
                   pl.BlockSpec((rows, d // 2), lambda i: (i, 0)),
                   pl.BlockSpec((SUBLANES, rows), lambda i: (0, i)),
                   pl.BlockSpec((rows, LANES), lambda i: (i, 0)),
                   pl.BlockSpec((ROUTE_ROWS, LANES), lambda i: (0, 0))],
        out_shape=[jax.ShapeDtypeStruct((t, d), F32),
                   jax.ShapeDtypeStruct((t, d // 2), jnp.uint32),
                   jax.ShapeDtypeStruct((SUBLANES, t), jnp.int32),
                   jax.ShapeDtypeStruct((t, LANES), F32),
                   jax.ShapeDtypeStruct((ROUTE_ROWS, LANES), F32)],
        scratch_shapes=[pltpu.VMEM((ROUTE_ROWS, rows), F32)],
        compiler_params=pltpu.CompilerParams(dimension_semantics=("arbitrary",)),
        name="outproj_router",
    )(ssd, att, xf, mod3, mod3, mod3, norm2, wa, wb, wr, br, tri)


def _expert_kernel(blk_e_ref, nused_ref, pstart_ref, counts_ref, eord_ref, enext_ref, dst_ref,
                   h2_hbm, wg_hbm, wu_hbm, wd_hbm, y_ref,
                   h2v, xg, wgs, wus, wds, wgb, wub, wdb, tok_ref, sem, wsem, *, n_tok, nblocks):
    b = pl.program_id(0)
    nused = nused_ref[0]
    slot = b % 2
    half = h2v.shape[1]

    def h2_copy():
        return pltpu.make_async_copy(h2_hbm, h2v, sem.at[0])

    def weight_copies(e, sl):
        return (pltpu.make_async_copy(wg_hbm.at[e], wgs.at[sl], wsem.at[sl, 0]),
                pltpu.make_async_copy(wu_hbm.at[e], wus.at[sl], wsem.at[sl, 1]),
                pltpu.make_async_copy(wd_hbm.at[e], wds.at[sl], wsem.at[sl, 2]))

    def gather_rows(blk, sl):
        for r in range(MOE_BLK):
            xg[sl, pl.ds(r, 1), :] = h2v[pl.ds(tok_ref[blk * MOE_BLK + r], 1), :]

    @pl.when(b == 0)
    def _():
        h2_copy().start()
        for cp in weight_copies(blk_e_ref[0], 0):
            cp.start()
        def pad_expert(e, carry):
            lo = pstart_ref[e] + counts_ref[e]
            hi = pstart_ref[e] + (counts_ref[e] + MOE_BLK - 1) // MOE_BLK * MOE_BLK

            def pad_slot(s, c):
                tok_ref[s] = 0
                return c
            return lax.fori_loop(lo, hi, pad_slot, carry)
        lax.fori_loop(0, N_EXPERTS, pad_expert, 0)

        for k in range(2):
            def claim(t, carry, k=k):
                tok_ref[dst_ref[k * n_tok + t]] = t
                return carry
            lax.fori_loop(0, n_tok, claim, 0, unroll=16)
        h2_copy().wait()
        gather_rows(0, 0)

    e = blk_e_ref[b]
    e_prev = blk_e_ref[jnp.maximum(b - 1, 0)]

    @pl.when((b == 0) | (e != e_prev))
    def _():
        sl = eord_ref[e] % 2
        for cp in weight_copies(e, sl):
            cp.wait()
        wgb[...] = wgs[sl].astype(BF16)
        wub[...] = wus[sl].astype(BF16)
        wdb[...] = wds[sl].astype(BF16)
        e_next = enext_ref[e]

        @pl.when(e_next >= 0)
        def _():
            for cp in weight_copies(e_next, 1 - sl):
                cp.start()

    @pl.when(b < nused)
    def _():
        words = xg[slot]
        x_lo = pltpu.bitcast(words << jnp.uint32(16), F32).astype(BF16)
        x_hi = pltpu.bitcast(words & jnp.uint32(0xFFFF0000), F32).astype(BF16)
        gate = (jnp.dot(x_lo, wgb[0:half, :], preferred_element_type=F32)
                + jnp.dot(x_hi, wgb[half:, :], preferred_element_type=F32))
        up = (jnp.dot(x_lo, wub[0:half, :], preferred_element_type=F32)
              + jnp.dot(x_hi, wub[half:, :], preferred_element_type=F32))
        hmid = (_silu(gate) * up).astype(BF16)
        y_ref[...] = jnp.dot(hmid, wdb[...], preferred_element_type=F32)
        gather_rows(jnp.minimum(b + 1, nused - 1), 1 - slot)

    @pl.when(b >= nused)
    def _():
        y_ref[...] = jnp.zeros_like(y_ref)


def _experts(blk_e, nused, pstart, counts, eord, enext, dst, h2p, w_gate, w_up, w_down, *, nblocks, n_tok):
    d = w_gate.shape[1]
    hbm = pl.BlockSpec(memory_space=pl.ANY)
    grid_spec = pltpu.PrefetchScalarGridSpec(
        num_scalar_prefetch=7,
        grid=(nblocks,),
        in_specs=[hbm, hbm, hbm, hbm],
        out_specs=pl.BlockSpec((MOE_BLK, d), lambda b, *_: (b, 0)),
        scratch_shapes=[
            pltpu.VMEM(h2p.shape, jnp.uint32),
            pltpu.VMEM((2, MOE_BLK, d // 2), jnp.uint32),
            pltpu.VMEM((2, d, EXPERT_DIM), F32),
            pltpu.VMEM((2, d, EXPERT_DIM), F32),
            pltpu.VMEM((2, EXPERT_DIM, d), F32),
            pltpu.VMEM((d, EXPERT_DIM), BF16),
            pltpu.VMEM((d, EXPERT_DIM), BF16),
            pltpu.VMEM((EXPERT_DIM, d), BF16),
            pltpu.SMEM((nblocks * MOE_BLK,), jnp.int32),
            pltpu.SemaphoreType.DMA((1,)),
            pltpu.SemaphoreType.DMA((2, 3)),
        ],
    )
    return pl.pallas_call(
        functools.partial(_expert_kernel, n_tok=n_tok, nblocks=nblocks),
        grid_spec=grid_spec,
        out_shape=jax.ShapeDtypeStruct((nblocks * MOE_BLK, d), F32),
        compiler_params=pltpu.CompilerParams(dimension_semantics=("arbitrary",)),
        name="experts",
    )(blk_e, nused, pstart, counts, eord, enext, dst, h2p, w_gate, w_up, w_down)


def _combine_kernel(dst_ref, y_hbm, x1_ref, wcol_ref, g2_ref, nf_ref, o_ref, ybuf, sem, *, nsteps, n_tok):
    i = pl.program_id(0)
    rows = x1_ref.shape[0]
    slot = i % 2

    def row_copy(src, r, k, sl):
        return pltpu.make_async_copy(y_hbm.at[pl.ds(src, 1)], ybuf.at[sl, k, pl.ds(r, 1)], sem.at[sl])

    def start_rows(step, sl):
        for r in range(rows):
            for k in range(2):
                row_copy(dst_ref[k * n_tok + step * rows + r], r, k, sl).start()

    def wait_rows(sl):
        for r in range(rows):
            for k in range(2):
                row_copy(0, r, k, sl).wait()

    @pl.when(i == 0)
    def _():
        start_rows(0, 0)

    @pl.when(i + 1 < nsteps)
    def _():
        start_rows(i + 1, 1 - slot)

    wait_rows(slot)
    moe = ybuf[slot, 0] * wcol_ref[:, 0:1] + ybuf[slot, 1] * wcol_ref[:, 1:2]
    x2 = x1_ref[...] + g2_ref[0] * moe
    ms = jnp.mean(x2 * x2, axis=-1, keepdims=True)
    o_ref[...] = x2 * lax.rsqrt(ms + EPS) * nf_ref[...]


def _combine(dst, y_buf, x1, wcol, mod3, norm_final, *, rows, seq_blocks):
    t, d = x1.shape
    nsteps = t // rows
    grid_spec = pltpu.PrefetchScalarGridSpec(
        num_scalar_prefetch=1,
        grid=(nsteps,),
        in_specs=[
            pl.BlockSpec(memory_space=pl.ANY),
            pl.BlockSpec((rows, d), lambda i, ds: (i, 0)),
            pl.BlockSpec((rows, LANES), lambda i, ds: (i, 0)),
            pl.BlockSpec((1, 1, d), lambda i, ds: (i // seq_blocks, 0, 5)),
            pl.BlockSpec((1, d), lambda i, ds: (0, 0)),
        ],
        out_specs=pl.BlockSpec((rows, d), lambda i, ds: (i, 0)),
        scratch_shapes=[pltpu.VMEM((2, 2, rows, d), F32), pltpu.SemaphoreType.DMA((2,))],
    )
    return pl.pallas_call(
        functools.partial(_combine_kernel, nsteps=nsteps, n_tok=t),
        grid_spec=grid_spec,
        out_shape=jax.ShapeDtypeStruct((t, d), F32),
        compiler_params=pltpu.CompilerParams(dimension_semantics=("arbitrary",)),
        name="combine",
    )(dst, y_buf, x1, wcol, mod3, norm_final)


def _rope_tables(seq):
    pos = np.arange(seq)
    n_freq = HEAD_DIM // 4
    inv = ROPE_BASE ** (-np.arange(n_freq, dtype=np.float32) / n_freq)
    ang = np.concatenate([(pos // GRID_W)[:, None] * inv, (pos % GRID_W)[:, None] * inv], axis=-1)
    ang = np.concatenate([ang, ang, ang, ang], axis=-1).astype(np.float32)
    return jnp.asarray(np.cos(ang), F32), jnp.asarray(np.sin(ang), F32)


def _head_expand_mat():
    m = np.zeros((SSD_HEADS, SSD_INNER), np.float32)
    for hh in range(SSD_HEADS):
        m[hh, hh * SSD_HEAD_DIM:(hh + 1) * SSD_HEAD_DIM] = 1.0
    return jnp.asarray(m)


def kernel(x, c, ctx, c_ctx, w_ada, b_ada, norm1, w_in, conv_w, conv_b, dt_bias, a_log, d_skip, ssd_norm,
           attn_sinks, w_out, norm2, w_group, b_group, w_expert, b_expert, w_gate, w_up, w_down, norm_final):
    return _pipeline(x, c, ctx, c_ctx, w_ada, b_ada, norm1, w_in, conv_w, conv_b, dt_bias, a_log, d_skip,
                     ssd_norm, attn_sinks, w_out, norm2, w_group, b_group, w_expert, b_expert, w_gate, w_up,
                     w_down, norm_final)["out"]


def _pipeline(x, c, ctx, c_ctx, w_ada, b_ada, norm1, w_in, conv_w, conv_b, dt_bias, a_log, d_skip, ssd_norm,
              attn_sinks, w_out, norm2, w_group, b_group, w_expert, b_expert, w_gate, w_up, w_down, norm_final):
    bt, seq, d = x.shape
    n_ctx = ctx.shape[1]
    t = bt * seq
    layer = 0
    assert w_ada.shape[0] == 1 and seq % ROWS_IN == 0 and n_ctx % CHUNK == 0

    cc = jnp.zeros((SUBLANES, d), F32).at[:bt].set(c).at[bt].set(c_ctx)
    mod = _ada(cc, w_ada[layer], b_ada[layer][None, :])
    mod3 = mod.reshape(SUBLANES, 1, 6 * d)

    w = w_in[layer]
    o_z, o_xbc, o_dt = 0, SSD_INNER, SSD_INNER + CONV_DIM
    o_q = o_dt + 2 * SSD_HEADS
    o_k, o_v = o_q + ATTN_WIDTH, o_q + ATTN_WIDTH + KV_WIDTH
    q_perm = np.concatenate([np.r_[j * HEAD_DIM:(j + 1) * HEAD_DIM, (j + 4) * HEAD_DIM:(j + 5) * HEAD_DIM]
                             for j in range(Q_PER_KV)])
    w_q = w[:, o_q:o_q + ATTN_WIDTH].reshape(d, ATTN_KV_HEADS, Q_PER_KV, HEAD_DIM).transpose(0, 2, 1, 3)
    w_q = w_q.reshape(d, ATTN_WIDTH)
    w_cat = jnp.concatenate([w[:, o_z:o_z + SSD_INNER], w[:, o_xbc:o_xbc + CONV_DIM], w_q,
                             w[:, o_k:o_k + KV_WIDTH], w[:, o_v:o_v + KV_WIDTH]], axis=1).astype(BF16)
    w_dtT = w[:, o_dt:o_dt + 2 * SSD_HEADS].T.astype(BF16)

    seq_blocks = seq // ROWS_IN
    cos, sin = _rope_tables(seq)
    bias_rows = jnp.broadcast_to(dt_bias[layer].reshape(2 * SSD_HEADS, 1), (2 * SSD_HEADS, LANES))
    alog_rows = jnp.broadcast_to(a_log[layer].reshape(2 * SSD_HEADS, 1), (2 * SSD_HEADS, LANES))
    segs_x = [("z", _COL_Z, SSD_INNER), ("xbc", _COL_XBC, CONV_DIM), ("dtT", 0, 0),
              ("q", _COL_Q, ATTN_WIDTH), ("k", _COL_K, KV_WIDTH), ("v", _COL_V, KV_WIDTH)]
    z_x, xbc_x, dt_x, pre_x, suf_x, q_x, k_x, v_x = _inproj(
        x.reshape(t, d), mod3, norm1[layer][None, :], w_cat, w_dtT, bias_rows, alog_rows, (cos, sin),
        rows=ROWS_IN, mod_row_fn=lambda i: i // seq_blocks, segs=segs_x, seq_blocks=seq_blocks)
    segs_c = [("xbc", _COL_XBC, CONV_DIM), ("dtT", 0, 0), ("k", _COL_K, KV_WIDTH), ("v", _COL_V, KV_WIDTH)]
    xbc_c, dt_c, pre_c, suf_c, k_c, v_c = _inproj(
        ctx.reshape(bt * n_ctx, d), mod3, norm1[layer][None, :], w_cat, w_dtT, bias_rows, alog_rows, None,
        rows=n_ctx, mod_row_fn=lambda i: bt, segs=segs_c, seq_blocks=1)

    cw = jnp.zeros((SUBLANES, CONV_DIM), F32).at[:CONV_WIDTH].set(conv_w[layer])
    cb = conv_b[layer][None, :]
    dskip_e = jnp.repeat(d_skip[layer], SSD_HEAD_DIM)[None, :]
    consts = (cw, cb, dskip_e, ssd_norm[layer][None, :], _head_expand_mat())

    zeros_state = jnp.zeros((bt, SSD_STATE, SSD_INNER), F32)
    z_dummy = jnp.zeros((bt, n_ctx, SSD_INNER), F32)
    h_cf, h_cb = _ssd(xbc_c.reshape(bt, n_ctx, CONV_DIM), (dt_c, pre_c, suf_c), z_dummy, consts,
                      zeros_state, zeros_state, rows=n_ctx, need_y=False)
    ssd_x, _, _ = _ssd(xbc_x.reshape(bt, seq, CONV_DIM), (dt_x, pre_x, suf_x), z_x.reshape(bt, seq, SSD_INNER),
                       consts, h_cf, h_cb, rows=ROWS_SSD, need_y=True)

    attn_x = _attention(attn_sinks[layer].astype(F32) * LOG2E,
                        q_x.reshape(bt, seq, ATTN_WIDTH), k_x.reshape(bt, seq, KV_WIDTH),
                        v_x.reshape(bt, seq, KV_WIDTH), k_c.reshape(bt, n_ctx, KV_WIDTH),
                        v_c.reshape(bt, n_ctx, KV_WIDTH))

    wo = w_out[layer]
    wa = wo[:SSD_INNER].astype(BF16)
    wb = wo[SSD_INNER:].reshape(ATTN_KV_HEADS, Q_PER_KV, HEAD_DIM, d).transpose(1, 0, 2, 3)
    wb = wb.reshape(ATTN_WIDTH, d).astype(BF16)
    wr = jnp.zeros((ROUTE_ROWS, d), F32).at[:N_GROUPS].set(w_group[layer].T)
    wr = wr.at[N_GROUPS:N_GROUPS + N_EXPERTS].set(w_expert[layer].T).astype(BF16)
    br = jnp.zeros((ROUTE_ROWS,), F32).at[:N_GROUPS].set(b_group[layer])
    br = br.at[N_GROUPS:N_GROUPS + N_EXPERTS].set(b_expert[layer])
    br = jnp.broadcast_to(br[:, None], (ROUTE_ROWS, ROWS_IN))
    tri = jnp.asarray(np.triu(np.ones((ROWS_IN, ROWS_IN), np.float32), 1), BF16)
    x1, h2, ids, wcol, cnt = _outproj(
        ssd_x.reshape(t, SSD_INNER), attn_x.reshape(t, ATTN_WIDTH), x.reshape(t, d), mod3, norm2[layer][None, :],
        wa, wb, wr, br, tri, rows=ROWS_IN, seq_blocks=seq_blocks)

    counts = cnt[N_GROUPS:N_GROUPS + N_EXPERTS, 0].astype(jnp.int32)
    padded = (counts + MOE_BLK - 1) // MOE_BLK * MOE_BLK
    ends = jnp.cumsum(padded)
    pstart = ends - padded
    nblocks = (2 * t) // MOE_BLK + N_EXPERTS
    blk_start = jnp.arange(nblocks, dtype=jnp.int32) * MOE_BLK
    eids = jnp.arange(N_EXPERTS, dtype=jnp.int32)
    active = counts > 0
    last_active = jnp.max(jnp.where(active, eids, 0))
    blk_e = jnp.minimum(jnp.sum((ends[None, :] <= blk_start[:, None]).astype(jnp.int32), axis=1), last_active)
    nused = (ends[-1] // MOE_BLK).astype(jnp.int32).reshape(1)
    eord = jnp.cumsum(active.astype(jnp.int32)) - active.astype(jnp.int32)
    later = active[None, :] & (eids[None, :] > eids[:, None])
    enext = jnp.min(jnp.where(later, eids[None, :], N_EXPERTS), axis=1)
    enext = jnp.where(enext == N_EXPERTS, -1, enext).astype(jnp.int32)
    code = ids[0:2].reshape(-1)
    e_id = lax.shift_right_logical(code, RANK_BITS)
    onehot = e_id[:, None] == jnp.arange(N_EXPERTS, dtype=jnp.int32)[None, :]
    dst = jnp.sum(jnp.where(onehot, pstart[None, :], 0), axis=1) + (code & ((1 << RANK_BITS) - 1))

    y_buf = _experts(blk_e, nused, pstart, counts, eord, enext, dst, h2, w_gate[layer], w_up[layer],
                     w_down[layer], nblocks=nblocks, n_tok=t)
    out = _combine(dst, y_buf, x1, wcol, mod3, norm_final[None, :], rows=ROWS_CMB, seq_blocks=seq // ROWS_CMB)
    return dict(out=out.reshape(bt, seq, d), mod=mod, z=z_x, xbc=xbc_x, dt=dt_x, q=q_x, k=k_x, v=v_x,
                k_c=k_c, v_c=v_c, h_cf=h_cf, h_cb=h_cb, ssd=ssd_x, attn=attn_x, x1=x1, h2=h2, ids=ids,
                wcol=wcol, cnt=cnt, y_buf=y_buf, pstart=pstart, q_perm=q_perm)
```

```python
import functools
import math

import numpy as np
import jax
import jax.numpy as jnp
from jax import lax
from jax.experimental import pallas as pl
from jax.experimental.pallas import tpu as pltpu

F32 = jnp.float32
BF16 = jnp.bfloat16

D_MODEL = 1024
GRID_W = 64
EPS = 1e-6
SSD_INNER = 512
SSD_HEAD_DIM = 64
SSD_HEADS = 8
SSD_GROUPS = 2
SSD_STATE = 128
CONV_WIDTH = 5
CONV_DIM = SSD_INNER + 2 * SSD_GROUPS * SSD_STATE
CHUNK = 128
ATTN_WIDTH = 512
HEAD_DIM = 64
ATTN_Q_HEADS = 8
ATTN_KV_HEADS = 2
Q_PER_KV = 4
KV_WIDTH = 128
WINDOW = 128
ROPE_BASE = 10000.0
N_GROUPS = 4
EXPERTS_PER_GROUP = 8
N_EXPERTS = 32
EXPERT_DIM = 512

LANES = 128
SUBLANES = 8
NEG = -1e30
LOG2E = math.log2(math.e)

ROWS_IN = 512
ROWS_SSD = 1024
ROWS_ATT = 512
ROUTE_ROWS = 48
MOE_BLK = 256
RANK_BITS = 16
D_TILES = D_MODEL // LANES
ROWS_CMB = 128


def _silu(v):
    half = 0.5 * v
    return half + half * jnp.tanh(half)


def _softplus(v):
    return jnp.maximum(v, 0.0) + jnp.log(1.0 + jnp.exp(-jnp.abs(v)))


def _ada_kernel(c_ref, w_ref, b_ref, o_ref):
    a = _silu(c_ref[...])
    o_ref[...] = jnp.dot(a.astype(BF16), w_ref[...].astype(BF16),
                         preferred_element_type=F32) + b_ref[...]


def _ada(cc, w_ada, b_ada):
    d, n = w_ada.shape
    tn = 1536
    return pl.pallas_call(
        _ada_kernel,
        grid=(n // tn,),
        in_specs=[pl.BlockSpec((SUBLANES, d), lambda j: (0, 0)),
                  pl.BlockSpec((d, tn), lambda j: (0, j)),
                  pl.BlockSpec((1, tn), lambda j: (0, j))],
        out_specs=pl.BlockSpec((SUBLANES, tn), lambda j: (0, j)),
        out_shape=jax.ShapeDtypeStruct((SUBLANES, n), F32),
        name="ada",
    )(cc, w_ada, b_ada)


def _rope(t, cos, sin_signed, first_half):
    rot = jnp.where(first_half, pltpu.roll(t, LANES - HEAD_DIM // 2, 1), pltpu.roll(t, HEAD_DIM // 2, 1))
    return t * cos + rot * sin_signed


def _lane_cumsum(v, lane, reverse):
    k = 1
    while k < LANES:
        if reverse:
            v = v + jnp.where(lane < LANES - k, pltpu.roll(v, LANES - k, 1), 0.0)
        else:
            v = v + jnp.where(lane >= k, pltpu.roll(v, k, 1), 0.0)
        k *= 2
    return v


def _inproj_kernel(*refs, segs, rope):
    x_ref, sh_ref, sc_ref, g_ref, w_ref, wdt_ref, bias_ref, alog_ref = refs[:8]
    pos = 8
    if rope:
        cos_ref, sin_ref = refs[8:10]
        pos = 10
    outs = iter(refs[pos:])
    x = x_ref[...]
    ms = jnp.mean(x * x, axis=-1, keepdims=True)
    h = (x * lax.rsqrt(ms + EPS)) * (g_ref[...] * (1.0 + sc_ref[0])) + sh_ref[0]
    hb = h.astype(BF16)
    if rope:
        cos = cos_ref[...]
        lane = lax.broadcasted_iota(jnp.int32, cos.shape, 1)
        first_half = (lane & (HEAD_DIM - 1)) < HEAD_DIM // 2
        sin_s = jnp.where(first_half, -sin_ref[...], sin_ref[...])
    for kind, c0, width in segs:
        if kind == "dtT":
            dt_ref, pre_ref, suf_ref = next(outs), next(outs), next(outs)
            dtt = lax.dot_general(wdt_ref[...], hb, (((1,), (1,)), ((), ())), preferred_element_type=F32)
            lane16 = lax.broadcasted_iota(jnp.int32, (2 * SSD_HEADS, CHUNK), 1)
            neg_a = -jnp.exp(alog_ref[...])
            for j in range(dt_ref.shape[0]):
                dt = _softplus(dtt[:, j * CHUNK:(j + 1) * CHUNK] + bias_ref[...])
                a = dt * neg_a
                dt_ref[j] = dt
                pre_ref[j] = _lane_cumsum(a, lane16, False)
                suf_ref[j] = _lane_cumsum(a, lane16, True)
            continue
        o_ref = next(outs)
        for j0 in range(0, width, 512):
            wj = min(512, width - j0)
            acc = jnp.dot(hb, w_ref[:, c0 + j0:c0 + j0 + wj], preferred_element_type=F32)
            if rope and kind in ("q", "k"):
                scale = HEAD_DIM ** -0.5 * LOG2E if kind == "q" else 1.0
                for l0 in range(0, wj, LANES):
                    t = _rope(acc[:, l0:l0 + LANES], cos, sin_s, first_half)
                    o_ref[:, j0 + l0:j0 + l0 + LANES] = (t * scale).astype(o_ref.dtype)
            else:
                o_ref[:, j0:j0 + wj] = acc.astype(o_ref.dtype)


_COL_Z, _COL_XBC, _COL_Q, _COL_K, _COL_V = 0, 512, 1536, 2048, 2176
_W_COLS = 2304


def _inproj(xf, mod3, norm1, w_cat, w_dtT, bias_rows, alog_rows, rope_tabs, *, rows, mod_row_fn, segs, seq_blocks):
    t, d = xf.shape
    rope = rope_tabs is not None
    in_specs = [
        pl.BlockSpec((rows, d), lambda i: (i, 0)),
        pl.BlockSpec((1, 1, d), lambda i: (mod_row_fn(i), 0, 0)),
        pl.BlockSpec((1, 1, d), lambda i: (mod_row_fn(i), 0, 1)),
        pl.BlockSpec((1, d), lambda i: (0, 0)),
        pl.BlockSpec(w_cat.shape, lambda i: (0, 0)),
        pl.BlockSpec(w_dtT.shape, lambda i: (0, 0)),
        pl.BlockSpec(bias_rows.shape, lambda i: (0, 0)),
        pl.BlockSpec(alog_rows.shape, lambda i: (0, 0)),
    ]
    args = [xf, mod3, mod3, norm1, w_cat, w_dtT, bias_rows, alog_rows]
    if rope:
        in_specs += [pl.BlockSpec((rows, LANES), lambda i: (i % seq_blocks, 0))] * 2
        args += list(rope_tabs)
    out_specs, out_shapes = [], []
    for kind, _, width in segs:
        if kind == "dtT":
            for _ in range(3):
                out_specs.append(pl.BlockSpec((rows // CHUNK, 2 * SSD_HEADS, CHUNK), lambda i: (i, 0, 0)))
                out_shapes.append(jax.ShapeDtypeStruct((t // CHUNK, 2 * SSD_HEADS, CHUNK), F32))
        else:
            dt = BF16 if kind in ("q", "k", "v") else F32
            out_specs.append(pl.BlockSpec((rows, width), lambda i: (i, 0)))
            out_shapes.append(jax.ShapeDtypeStruct((t, width), dt))
    return pl.pallas_call(
        functools.partial(_inproj_kernel, segs=tuple(segs), rope=rope),
        grid=(t // rows,),
        in_specs=in_specs,
        out_specs=out_specs,
        out_shape=out_shapes,
        name="inproj_rope" if rope else "inproj_ctx",
    )(*args)


_XB_COLS = SSD_INNER + SSD_GROUPS * SSD_STATE
_Y_OFF = CONV_DIM - _XB_COLS


def _ssd_kernel(*refs, need_y, nsteps, cps):
    (xbc_ref, prev_ref, next_ref, dt_ref, pre_ref, suf_ref, z_ref, cw_ref, cb_ref,
     dskip_ref, gain_ref, exp_ref, h0f_ref, h0b_ref) = refs[:14]
    if need_y:
        y_ref, hf_out, hb_out = refs[14:17]
        rest = refs[17:]
    else:
        y_ref = None
        hf_out, hb_out = refs[14:16]
        rest = refs[16:]
    cwin, u_ref, hf, hb, hbs, ust = rest

    sw = pl.program_id(1)
    s = pl.program_id(2)
    bi = jnp.where(sw == 0, nsteps - 1 - s, s)
    rows = xbc_ref.shape[1]
    h = SSD_HEADS

    @pl.when((sw == 0) & (s == 0))
    def _():
        hb[...] = h0b_ref[0]

    @pl.when((sw == 1) & (s == 0))
    def _():
        hf[...] = h0f_ref[0]

    ri = lax.broadcasted_iota(jnp.int32, (CHUNK, CHUNK), 0)
    ci = lax.broadcasted_iota(jnp.int32, (CHUNK, CHUNK), 1)
    lane128 = lax.broadcasted_iota(jnp.int32, (CHUNK, LANES), 1)
    lo_mask = lane128 < SSD_HEAD_DIM

    def conv_silu(c, c0, c1, out_ref):
        r0 = pl.multiple_of(c * CHUNK, CHUNK)
        above = xbc_ref[0, pl.ds(pl.multiple_of(jnp.maximum(r0 - SUBLANES, 0), SUBLANES), SUBLANES), c0:c1]
        halo_a = jnp.where(bi > 0, prev_ref[0, :, c0:c1], 0.0)
        cwin[0:SUBLANES, c0:c1] = jnp.where(c > 0, above, halo_a)
        cwin[SUBLANES:SUBLANES + CHUNK, c0:c1] = xbc_ref[0, pl.ds(r0, CHUNK), c0:c1]
        below = xbc_ref[0, pl.ds(pl.multiple_of(jnp.minimum(r0 + CHUNK, rows - SUBLANES), SUBLANES), SUBLANES), c0:c1]
        halo_b = jnp.where(bi < nsteps - 1, next_ref[0, :, c0:c1], 0.0)
        cwin[SUBLANES + CHUNK:, c0:c1] = jnp.where(c < cps - 1, below, halo_b)
        for j0 in range(c0, c1, LANES):
            acc = jnp.broadcast_to(cb_ref[:, j0:j0 + LANES], (CHUNK, LANES))
            for k in range(CONV_WIDTH):
                off = SUBLANES - CONV_WIDTH // 2 + k
                acc = acc + cwin[off:off + CHUNK, j0:j0 + LANES] * cw_ref[k:k + 1, j0:j0 + LANES]
            out_ref[:, j0 - c0:j0 - c0 + LANES] = _silu(acc)

    def dt_rows(c):
        return dt_ref[c], pre_ref[c], suf_ref[c]

    def state_update(state_ref, xb, w_rows, tot_col):
        dec = jnp.exp(jnp.sum(tot_col * exp_ref[...], axis=0, keepdims=True))
        for g in range(SSD_GROUPS):
            bm = xb[:, SSD_INNER + g * SSD_STATE:SSD_INNER + (g + 1) * SSD_STATE]
            bt = bm.T
            for pr in range(2):
                h0 = g * 4 + pr * 2
                c0 = h0 * SSD_HEAD_DIM
                xp = xb[:, c0:c0 + LANES]
                rhs = jnp.concatenate([jnp.where(lo_mask, xp, 0.0), jnp.where(lo_mask, 0.0, xp)],
                                      axis=0).astype(BF16)
                lhs = jnp.concatenate([bt * w_rows[h0:h0 + 1, :], bt * w_rows[h0 + 1:h0 + 2, :]],
                                      axis=1).astype(BF16)
                sres = jnp.dot(lhs, rhs, preferred_element_type=F32)
                state_ref[:, c0:c0 + LANES] = state_ref[:, c0:c0 + LANES] * dec[:, c0:c0 + LANES] + sres

    def col_forms(rows16):
        padded = jnp.concatenate([rows16, jnp.zeros((CHUNK - 2 * h, LANES), F32)], axis=0)
        return padded.T

    def backward_chunk(k, carry):
        c = cps - 1 - k
        gc = bi * cps + c
        xb = ust.at[gc]
        conv_silu(c, 0, _XB_COLS, xb)
        dt, _, suf = dt_rows(c)
        hbs[gc] = hb[...].astype(BF16)
        s0 = suf[:, 0:1]
        w_rows = jnp.exp(s0 - suf) * dt
        state_update(hb, xb, w_rows[h:2 * h, :], s0[h:2 * h, :])
        return carry

    def forward_chunk(c, carry):
        gc = bi * cps + c
        r0 = pl.multiple_of(c * CHUNK, CHUNK)
        xb = ust.at[gc]
        conv_silu(c, _XB_COLS, CONV_DIM, u_ref)
        dt, pre, suf = dt_rows(c)
        rowsf = jnp.concatenate([pre[0:h, :], suf[h:2 * h, :]], axis=0)
        colsf = col_forms(rowsf)
        ecol = jnp.exp(colsf)
        for g in range(SSD_GROUPS):
            bm = xb[:, SSD_INNER + g * SSD_STATE:SSD_INNER + (g + 1) * SSD_STATE]
            cm = u_ref[:, g * SSD_STATE:(g + 1) * SSD_STATE]
            cmb = cm.astype(BF16)
            cbm = lax.dot_general(cmb, bm.astype(BF16), (((1,), (1,)), ((), ())),
                                  preferred_element_type=F32)
            for pr in range(2):
                h0 = g * 4 + pr * 2
                c0 = h0 * SSD_HEAD_DIM
                lhs_parts = []
                for hh in (h0, h0 + 1):
                    segf = colsf[:, hh:hh + 1] - rowsf[hh:hh + 1, :]
                    gf = jnp.exp(jnp.where(ci <= ri, segf, NEG)) * dt[hh:hh + 1, :]
                    segb = colsf[:, h + hh:h + hh + 1] - rowsf[h + hh:h + hh + 1, :]
                    gb = jnp.exp(jnp.where(ci >= ri, segb, NEG)) * dt[h + hh:h + hh + 1, :]
                    lhs_parts.append((cbm * (gf + gb)).astype(BF16))
                for hh in (h0, h0 + 1):
                    lhs_parts.append((cm * ecol[:, hh:hh + 1]).astype(BF16))
                for hh in (h0, h0 + 1):
                    lhs_parts.append((cm * ecol[:, h + hh:h + hh + 1]).astype(BF16))
                lhs = jnp.concatenate(lhs_parts, axis=1)
                xp = xb[:, c0:c0 + LANES]
                sf = hf[:, c0:c0 + LANES]
                sb = hbs[gc, :, c0:c0 + LANES].astype(F32)
                rhs = jnp.concatenate(
                    [jnp.where(lo_mask, xp, 0.0), jnp.where(lo_mask, 0.0, xp),
                     jnp.where(lo_mask, sf, 0.0), jnp.where(lo_mask, 0.0, sf),
                     jnp.where(lo_mask, sb, 0.0), jnp.where(lo_mask, 0.0, sb)], axis=0).astype(BF16)
                ypair = jnp.dot(lhs, rhs, preferred_element_type=F32)
                ypair = ypair + dskip_ref[:, c0:c0 + LANES] * xp
                zz = z_ref[0, pl.ds(r0, CHUNK), c0:c0 + LANES]
                u_ref[:, _Y_OFF + c0:_Y_OFF + c0 + LANES] = ypair * _silu(zz)
        last = pre[:, LANES - 1:LANES]
        w_rows = jnp.exp(last - pre) * dt
        state_update(hf, xb, w_rows[0:h, :], last[0:h, :])
        if need_y:
            yv = u_ref[:, _Y_OFF:_Y_OFF + SSD_INNER]
            ms = jnp.mean(yv * yv, axis=-1, keepdims=True)
            y_ref[0, pl.ds(r0, CHUNK), :] = (yv * lax.rsqrt(ms + EPS) * gain_ref[...]).astype(y_ref.dtype)
        return carry

    @pl.when(sw == 0)
    def _():
        lax.fori_loop(0, cps, backward_chunk, 0, unroll=2)

    @pl.when(sw == 1)
    def _():
        lax.fori_loop(0, cps, forward_chunk, 0, unroll=2)

    @pl.when((sw == 0) & (s == nsteps - 1))
    def _():
        hb_out[0] = hb[...]

    @pl.when((sw == 1) & (s == nsteps - 1))
    def _():
        hf_out[0] = hf[...]


def _ssd(xbc, dts, z, consts, h0f, h0b, *, rows, need_y):
    bt, seq, _ = xbc.shape
    nsteps = seq // rows
    cps = rows // CHUNK
    hb8 = rows // SUBLANES
    nb8 = seq // SUBLANES
    cw, cb, dskip, gain, expm = consts
    dt_spec = pl.BlockSpec((cps, 2 * SSD_HEADS, CHUNK), lambda b, sw, s: (b * nsteps + blk(b, sw, s), 0, 0))

    def blk(b, sw, s):
        return jnp.where(sw == 0, nsteps - 1 - s, s)

    def full(a):
        return pl.BlockSpec(a.shape, lambda b, sw, s: (0,) * a.ndim)

    in_specs = [
        pl.BlockSpec((1, rows, CONV_DIM), lambda b, sw, s: (b, blk(b, sw, s), 0)),
        pl.BlockSpec((1, SUBLANES, CONV_DIM), lambda b, sw, s: (b, jnp.maximum(blk(b, sw, s) * hb8 - 1, 0), 0)),
        pl.BlockSpec((1, SUBLANES, CONV_DIM),
                     lambda b, sw, s: (b, jnp.minimum((blk(b, sw, s) + 1) * hb8, nb8 - 1), 0)),
        dt_spec, dt_spec, dt_spec,
        pl.BlockSpec((1, rows, SSD_INNER), lambda b, sw, s: (b, jnp.where(sw == 0, 0, s), 0)),
        full(cw), full(cb), full(dskip), full(gain), full(expm),
        pl.BlockSpec((1, SSD_STATE, SSD_INNER), lambda b, sw, s: (b, 0, 0)),
        pl.BlockSpec((1, SSD_STATE, SSD_INNER), lambda b, sw, s: (b, 0, 0)),
    ]
    st_spec = pl.BlockSpec((1, SSD_STATE, SSD_INNER), lambda b, sw, s: (b, 0, 0))
    st_shape = jax.ShapeDtypeStruct((bt, SSD_STATE, SSD_INNER), F32)
    out_specs, out_shapes = [st_spec, st_spec], [st_shape, st_shape]
    if need_y:
        out_specs = [pl.BlockSpec((1, rows, SSD_INNER), lambda b, sw, s: (b, jnp.where(sw == 0, 0, s), 0))] + out_specs
        out_shapes = [jax.ShapeDtypeStruct((bt, seq, SSD_INNER), BF16)] + out_shapes
    scratch = [
        pltpu.VMEM((CHUNK + 2 * SUBLANES, CONV_DIM), F32),
        pltpu.VMEM((CHUNK, _Y_OFF + SSD_INNER), F32),
        pltpu.VMEM((SSD_STATE, SSD_INNER), F32),
        pltpu.VMEM((SSD_STATE, SSD_INNER), F32),
        pltpu.VMEM((seq // CHUNK, SSD_STATE, SSD_INNER), BF16),
        pltpu.VMEM((seq // CHUNK, CHUNK, _XB_COLS), F32),
    ]
    return pl.pallas_call(
        functools.partial(_ssd_kernel, need_y=need_y, nsteps=nsteps, cps=cps),
        grid=(bt, 2, nsteps),
        in_specs=in_specs,
        out_specs=out_specs,
        out_shape=out_shapes,
        scratch_shapes=scratch,
        compiler_params=pltpu.CompilerParams(dimension_semantics=("arbitrary", "arbitrary", "arbitrary")),
        name="ssd_y" if need_y else "ssd_ctx",
    )(xbc, xbc, xbc, *dts, z, cw, cb, dskip, gain, expm, h0f, h0b)


def _attn_kernel(sink_ref, q_ref, kp_ref, kc_ref, kn_ref, vp_ref, vc_ref, vn_ref, kx_ref, vx_ref, o_ref, *, nsteps):
    m = pl.program_id(1)
    lane = lax.broadcasted_iota(jnp.int32, (1, LANES), 1)
    lo = lane < HEAD_DIM
    rq = Q_PER_KV * WINDOW
    per = ROWS_ATT // WINDOW
    qi = lax.broadcasted_iota(jnp.int32, (rq, WINDOW), 0) & (WINDOW - 1)
    kj = lax.broadcasted_iota(jnp.int32, (rq, WINDOW), 1)
    rblk = lax.shift_right_logical(lax.broadcasted_iota(jnp.int32, (rq, 1), 0), int(math.log2(WINDOW)))
    zero = jnp.zeros((), BF16)
    band_prev = jnp.where(kj >= qi, 0.0, NEG)
    band_next = jnp.where(kj <= qi, 0.0, NEG)

    kpieces = [kp_ref[0]] + [kc_ref[0, w * WINDOW:(w + 1) * WINDOW] for w in range(per)] + [kn_ref[0]]
    vpieces = [vp_ref[0]] + [vc_ref[0, w * WINDOW:(w + 1) * WINDOW] for w in range(per)] + [vn_ref[0]]
    kx = kx_ref[0]
    vx = vx_ref[0]
    nt = (((1,), (1,)), ((), ()))

    for sb in range(per):
        qsb = q_ref[0, sb * WINDOW:(sb + 1) * WINDOW, :]
        mask_prev = band_prev if sb > 0 else jnp.where(m > 0, band_prev, NEG)
        mask_next = band_next if sb < per - 1 else jnp.where(m < nsteps - 1, band_next, NEG)
        outs = [None] * Q_PER_KV
        for kv in range(ATTN_KV_HEADS):
            sel = lo if kv == 0 else jnp.logical_not(lo)
            qs = jnp.concatenate(
                [jnp.where(sel, qsb[:, j * LANES:(j + 1) * LANES], zero) for j in range(Q_PER_KV)], axis=0)
            s_prev = lax.dot_general(qs, kpieces[sb], nt, preferred_element_type=F32) + mask_prev
            s_cur = lax.dot_general(qs, kpieces[sb + 1], nt, preferred_element_type=F32)
            s_next = lax.dot_general(qs, kpieces[sb + 2], nt, preferred_element_type=F32) + mask_next
            s_ctx = lax.dot_general(qs, kx, nt, preferred_element_type=F32)
            sink = jnp.zeros((rq, 1), F32)
            for j in range(Q_PER_KV):
                sink = jnp.where(rblk == j, sink_ref[kv * Q_PER_KV + j], sink)
            n_ct = s_ctx.shape[1] // LANES
            tile_max = jnp.maximum(jnp.maximum(s_prev, s_cur), s_next)
            for t in range(n_ct):
                tile_max = jnp.maximum(tile_max, s_ctx[:, t * LANES:(t + 1) * LANES])
            mx = jnp.maximum(jnp.max(tile_max, axis=-1, keepdims=True), sink)
            p_prev = jnp.exp2(s_prev - mx)
            p_cur = jnp.exp2(s_cur - mx)
            p_next = jnp.exp2(s_next - mx)
            p_ctx = jnp.exp2(s_ctx - mx)
            tile_sum = p_prev + p_cur + p_next
            for t in range(n_ct):
                tile_sum = tile_sum + p_ctx[:, t * LANES:(t + 1) * LANES]
            den = jnp.sum(tile_sum, axis=-1, keepdims=True) + jnp.exp2(sink - mx)
            acc = jnp.dot(p_prev.astype(BF16), jnp.where(sel, vpieces[sb], zero), preferred_element_type=F32)
            acc += jnp.dot(p_cur.astype(BF16), jnp.where(sel, vpieces[sb + 1], zero), preferred_element_type=F32)
            acc += jnp.dot(p_next.astype(BF16), jnp.where(sel, vpieces[sb + 2], zero), preferred_element_type=F32)
            acc += jnp.dot(p_ctx.astype(BF16), jnp.where(sel, vx, zero), preferred_element_type=F32)
            acc = acc * (1.0 / den)
            for j in range(Q_PER_KV):
                part = acc[j * WINDOW:(j + 1) * WINDOW, :]
                outs[j] = part if outs[j] is None else outs[j] + part
        for j in range(Q_PER_KV):
            o_ref[0, sb * WINDOW:(sb + 1) * WINDOW, j * LANES:(j + 1) * LANES] = outs[j].astype(o_ref.dtype)


def _attention(sinks, q, k, v, kx, vx):
    bt, seq, _ = q.shape
    nsteps = seq // ROWS_ATT
    per = ROWS_ATT // WINDOW
    nb = seq // WINDOW
    n_ctx = kx.shape[1]

    def prev(b, m, s):
        return (b, jnp.maximum(m * per - 1, 0), 0)

    def cur(b, m, s):
        return (b, m, 0)

    def nxt(b, m, s):
        return (b, jnp.minimum((m + 1) * per, nb - 1), 0)

    small = (1, WINDOW, KV_WIDTH)
    big = (1, ROWS_ATT, KV_WIDTH)
    grid_spec = pltpu.PrefetchScalarGridSpec(
        num_scalar_prefetch=1,
        grid=(bt, nsteps),
        in_specs=[
            pl.BlockSpec((1, ROWS_ATT, ATTN_WIDTH), cur),
            pl.BlockSpec(small, prev), pl.BlockSpec(big, cur), pl.BlockSpec(small, nxt),
            pl.BlockSpec(small, prev), pl.BlockSpec(big, cur), pl.BlockSpec(small, nxt),
            pl.BlockSpec((1, n_ctx, KV_WIDTH), lambda b, m, s: (b, 0, 0)),
            pl.BlockSpec((1, n_ctx, KV_WIDTH), lambda b, m, s: (b, 0, 0)),
        ],
        out_specs=pl.BlockSpec((1, ROWS_ATT, ATTN_WIDTH), cur),
    )
    return pl.pallas_call(
        functools.partial(_attn_kernel, nsteps=nsteps),
        grid_spec=grid_spec,
        out_shape=jax.ShapeDtypeStruct((bt, seq, ATTN_WIDTH), BF16),
        name="attn",
    )(sinks, q, k, k, k, v, v, v, kx, vx)


def _outproj_kernel(ssd_ref, att_ref, x_ref, g1_ref, sh_ref, sc_ref, n2_ref, wa_ref, wb_ref, wr_ref, br_ref,
                    tri_ref, x1_ref, h2_ref, ids_ref, wcol_ref, cnt_ref, carry, *, nsteps):
    i = pl.program_id(0)
    rows = x_ref.shape[0]

    @pl.when(i == 0)
    def _():
        carry[...] = jnp.zeros_like(carry)

    acc = jnp.dot(ssd_ref[...], wa_ref[...], preferred_element_type=F32)
    acc += jnp.dot(att_ref[...], wb_ref[...], preferred_element_type=F32)
    x1 = x_ref[...] + g1_ref[0] * acc
    x1_ref[...] = x1
    ms = jnp.mean(x1 * x1, axis=-1, keepdims=True)
    h2 = (x1 * lax.rsqrt(ms + EPS)) * (n2_ref[...] * (1.0 + sc_ref[0])) + sh_ref[0]
    h2b = h2.astype(BF16)
    half = h2.shape[1] // 2
    lo_bits = lax.shift_right_logical(pltpu.bitcast(h2b[:, :half].astype(F32), jnp.uint32), jnp.uint32(16))
    hi_bits = pltpu.bitcast(h2b[:, half:].astype(F32), jnp.uint32) & jnp.uint32(0xFFFF0000)
    h2_ref[...] = hi_bits | lo_bits
    lt = lax.dot_general(wr_ref[...], h2b, (((1,), (1,)), ((), ())),
                         preferred_element_type=F32) + br_ref[...]
    row = lax.broadcasted_iota(jnp.int32, lt.shape, 0).astype(F32)
    big_i = float(ROUTE_ROWS)
    gl = jnp.where(row < N_GROUPS, lt, NEG)
    gmax = jnp.max(gl, axis=0, keepdims=True)
    gidx = jnp.min(jnp.where(gl == gmax, row, big_i), axis=0, keepdims=True)
    g_w = 1.0 / jnp.sum(jnp.exp(gl - gmax), axis=0, keepdims=True)
    lo = N_GROUPS + EXPERTS_PER_GROUP * gidx
    el = jnp.where((row >= lo) & (row < lo + EXPERTS_PER_GROUP), lt, NEG)
    m1 = jnp.max(el, axis=0, keepdims=True)
    i1 = jnp.min(jnp.where(el == m1, row, big_i), axis=0, keepdims=True)
    el2 = jnp.where(row == i1, NEG, el)
    m2 = jnp.max(el2, axis=0, keepdims=True)
    i2 = jnp.min(jnp.where(el2 == m2, row, big_i), axis=0, keepdims=True)
    r = jnp.exp(m2 - m1)
    w1 = g_w / (1.0 + r)
    w2 = g_w * r / (1.0 + r)
    oh1 = row == i1
    oh2 = row == i2
    cnt = jnp.where(oh1 | oh2, 1.0, 0.0)
    prefix = jnp.dot(cnt.astype(BF16), tri_ref[...], preferred_element_type=F32)
    base = carry[...] + prefix
    rank1 = jnp.sum(jnp.where(oh1, base, 0.0), axis=0, keepdims=True)
    rank2 = jnp.sum(jnp.where(oh2, base, 0.0), axis=0, keepdims=True)
    carry[...] = carry[...] + jnp.sum(cnt, axis=1, keepdims=True)
    r8 = lax.broadcasted_iota(jnp.int32, (SUBLANES, rows), 0)
    code1 = (i1 - N_GROUPS) * float(1 << RANK_BITS) + rank1
    code2 = (i2 - N_GROUPS) * float(1 << RANK_BITS) + rank2
    ids_ref[...] = jnp.where(r8 == 0, code1, jnp.where(r8 == 1, code2, 0.0)).astype(jnp.int32)
    r128 = lax.broadcasted_iota(jnp.int32, (LANES, rows), 0)
    wfull = jnp.where(r128 == 0, w1, jnp.where(r128 == 1, w2, 0.0))
    for j in range(rows // LANES):
        wcol_ref[j * LANES:(j + 1) * LANES, :] = wfull[:, j * LANES:(j + 1) * LANES].T

    @pl.when(i == nsteps - 1)
    def _():
        cnt_ref[...] = carry[:, 0:LANES]


def _outproj(ssd, att, xf, mod3, norm2, wa, wb, wr, br, tri, *, rows, seq_blocks):
    t, d = xf.shape
    nsteps = t // rows

    def mod(col):
        return pl.BlockSpec((1, 1, d), lambda i: (i // seq_blocks, 0, col))

    def full(a):
        return pl.BlockSpec(a.shape, lambda i: (0,) * a.ndim)

    return pl.pallas_call(
        functools.partial(_outproj_kernel, nsteps=nsteps),
        grid=(nsteps,),
        in_specs=[pl.BlockSpec((rows, SSD_INNER), lambda i: (i, 0)),
                  pl.BlockSpec((rows, ATTN_WIDTH), lambda i: (i, 0)),
                  pl.BlockSpec((rows, d), lambda i: (i, 0)),
                  mod(2), mod(3), mod(4), full(norm2), full(wa), full(wb), full(wr), full(br), full(tri)],
        out_specs=[pl.BlockSpec((rows, d), lambda i: (i, 0)),
                   pl.BlockSpec((rows, d // 2), lambda i: (i, 0)),
                   pl.BlockSpec((SUBLANES, rows), lambda i: (0, i)),
                   pl.BlockSpec((rows, LANES), lambda i: (i, 0)),
                   pl.BlockSpec((ROUTE_ROWS, LANES), lambda i: (0, 0))],
        out_shape=[jax.ShapeDtypeStruct((t, d), F32),
                   jax.ShapeDtypeStruct((t, d // 2), jnp.uint32),
                   jax.ShapeDtypeStruct((SUBLANES, t), jnp.int32),
                   jax.ShapeDtypeStruct((t, LANES), F32),
                   jax.ShapeDtypeStruct((ROUTE_ROWS, LANES), F32)],
        scratch_shapes=[pltpu.VMEM((ROUTE_ROWS, rows), F32)],
        compiler_params=pltpu.CompilerParams(dimension_semantics=("arbitrary",)),
        name="outproj_router",
    )(ssd, att, xf, mod3, mod3, mod3, norm2, wa, wb, wr, br, tri)


def _expert_kernel(blk_e_ref, nused_ref, pstart_ref, counts_ref, eord_ref, enext_ref, dst_ref,
                   h2_hbm, wg_hbm, wu_hbm, wd_hbm, y_hbm,
                   h2v, xg, ystage, wgs, wus, wds, wgb, wub, wdb, tok_ref, sem, wsem, ysem, *, n_tok, nblocks):
    nused = nused_ref[0]
    half = h2v.shape[1]

    def h2_copy():
        return pltpu.make_async_copy(h2_hbm, h2v, sem.at[0])

    def y_copy(blk, sl):
        row0 = blk * MOE_BLK if isinstance(blk, int) else pl.multiple_of(blk * MOE_BLK, MOE_BLK)
        return pltpu.make_async_copy(ystage.at[sl], y_hbm.at[pl.ds(row0, MOE_BLK)], ysem.at[sl])

    def weight_copies(e, sl):
        return (pltpu.make_async_copy(wg_hbm.at[e], wgs.at[sl], wsem.at[sl, 0]),
                pltpu.make_async_copy(wu_hbm.at[e], wus.at[sl], wsem.at[sl, 1]),
                pltpu.make_async_copy(wd_hbm.at[e], wds.at[sl], wsem.at[sl, 2]))

    def gather_rows(blk, sl):
        for r in range(MOE_BLK):
            xg[sl, pl.ds(r, 1), :] = h2v[pl.ds(tok_ref[blk * MOE_BLK + r], 1), :]

    h2_copy().start()
    for cp in weight_copies(blk_e_ref[0], 0):
        cp.start()

    def pad_expert(e, carry):
        lo = pstart_ref[e] + counts_ref[e]
        hi = pstart_ref[e] + (counts_ref[e] + MOE_BLK - 1) // MOE_BLK * MOE_BLK

        def pad_slot(s, c):
            tok_ref[s] = 0
            return c
        return lax.fori_loop(lo, hi, pad_slot, carry)
    lax.fori_loop(0, N_EXPERTS, pad_expert, 0)

    def claim(t, carry):
        tok_ref[dst_ref[t]] = t
        tok_ref[dst_ref[n_tok + t]] = t
        return carry
    lax.fori_loop(0, n_tok, claim, 0, unroll=16)
    h2_copy().wait()
    gather_rows(0, 0)

    def block(b, carry):
        slot = b % 2
        e = blk_e_ref[b]
        e_prev = blk_e_ref[jnp.maximum(b - 1, 0)]

        @pl.when(b >= 2)
        def _():
            y_copy(0, slot).wait()

        @pl.when((b == 0) | (e != e_prev))
        def _():
            sl = eord_ref[e] % 2
            for cp in weight_copies(e, sl):
                cp.wait()
            wgb[...] = wgs[sl].astype(BF16)
            wub[...] = wus[sl].astype(BF16)
            wdb[...] = wds[sl].astype(BF16)
            e_next = enext_ref[e]

            @pl.when(e_next >= 0)
            def _():
                for cp in weight_copies(e_next, 1 - sl):
                    cp.start()

        words = xg[slot]
        x_lo = pltpu.bitcast(words << jnp.uint32(16), F32).astype(BF16)
        x_hi = pltpu.bitcast(words & jnp.uint32(0xFFFF0000), F32).astype(BF16)
        gate = (jnp.dot(x_lo, wgb[0:half, :], preferred_element_type=F32)
                + jnp.dot(x_hi, wgb[half:, :], preferred_element_type=F32))
        up = (jnp.dot(x_lo, wub[0:half, :], preferred_element_type=F32)
              + jnp.dot(x_hi, wub[half:, :], preferred_element_type=F32))
        hmid = (_silu(gate) * up).astype(BF16)
        ystage[slot] = jnp.dot(hmid, wdb[...], preferred_element_type=F32)
        gather_rows(jnp.minimum(b + 1, nused - 1), 1 - slot)
        y_copy(b, slot).start()
        return carry
    lax.fori_loop(0, nused, block, 0)

    @pl.when(nused >= 2)
    def _():
        y_copy(0, nused % 2).wait()

    y_copy(0, (nused - 1) % 2).wait()
    ystage[0] = jnp.zeros(ystage.shape[1:], F32)

    def zero_block(b, carry):
        cp = y_copy(b, 0)
        cp.start()
        cp.wait()
        return carry
    lax.fori_loop(nused, nblocks, zero_block, 0)


def _experts(blk_e, nused, pstart, counts, eord, enext, dst, h2p, w_gate, w_up, w_down, *, nblocks, n_tok):
    d = w_gate.shape[1]
    hbm = pl.BlockSpec(memory_space=pl.ANY)
    grid_spec = pltpu.PrefetchScalarGridSpec(
        num_scalar_prefetch=7,
        grid=(1,),
        in_specs=[hbm, hbm, hbm, hbm],
        out_specs=hbm,
        scratch_shapes=[
            pltpu.VMEM(h2p.shape, jnp.uint32),
            pltpu.VMEM((2, MOE_BLK, d // 2), jnp.uint32),
            pltpu.VMEM((2, MOE_BLK, d), F32),
            pltpu.VMEM((2, d, EXPERT_DIM), F32),
            pltpu.VMEM((2, d, EXPERT_DIM), F32),
            pltpu.VMEM((2, EXPERT_DIM, d), F32),
            pltpu.VMEM((d, EXPERT_DIM), BF16),
            pltpu.VMEM((d, EXPERT_DIM), BF16),
            pltpu.VMEM((EXPERT_DIM, d), BF16),
            pltpu.SMEM((nblocks * MOE_BLK,), jnp.int32),
            pltpu.SemaphoreType.DMA((1,)),
            pltpu.SemaphoreType.DMA((2, 3)),
            pltpu.SemaphoreType.DMA((2,)),
        ],
    )
    return pl.pallas_call(
        functools.partial(_expert_kernel, n_tok=n_tok, nblocks=nblocks),
        grid_spec=grid_spec,
        out_shape=jax.ShapeDtypeStruct((nblocks * MOE_BLK, d), F32),
        compiler_params=pltpu.CompilerParams(dimension_semantics=("arbitrary",)),
        name="experts",
    )(blk_e, nused, pstart, counts, eord, enext, dst, h2p, w_gate, w_up, w_down)


def _combine_kernel(dst_ref, y_hbm, x1_ref, wcol_ref, g2_ref, nf_ref, o_ref, ybuf, sem, *, nsteps, n_tok):
    i = pl.program_id(0)
    rows = x1_ref.shape[0]
    slot = i % 2

    def row_copy(src, r, k, sl):
        return pltpu.make_async_copy(y_hbm.at[pl.ds(src, 1)], ybuf.at[sl, k, pl.ds(r, 1)], sem.at[sl])

    def start_rows(step, sl):
        for r in range(rows):
            for k in range(2):
                row_copy(dst_ref[k * n_tok + step * rows + r], r, k, sl).start()

    def wait_rows(sl):
        for r in range(rows):
            for k in range(2):
                row_copy(0, r, k, sl).wait()

    @pl.when(i == 0)
    def _():
        start_rows(0, 0)

    @pl.when(i + 1 < nsteps)
    def _():
        start_rows(i + 1, 1 - slot)

    wait_rows(slot)
    moe = ybuf[slot, 0] * wcol_ref[:, 0:1] + ybuf[slot, 1] * wcol_ref[:, 1:2]
    x2 = x1_ref[...] + g2_ref[0] * moe
    ms = jnp.mean(x2 * x2, axis=-1, keepdims=True)
    o_ref[...] = x2 * lax.rsqrt(ms + EPS) * nf_ref[...]


def _combine(dst, y_buf, x1, wcol, mod3, norm_final, *, rows, seq_blocks):
    t, d = x1.shape
    nsteps = t // rows
    grid_spec = pltpu.PrefetchScalarGridSpec(
        num_scalar_prefetch=1,
        grid=(nsteps,),
        in_specs=[
            pl.BlockSpec(memory_space=pl.ANY),
            pl.BlockSpec((rows, d), lambda i, ds: (i, 0)),
            pl.BlockSpec((rows, LANES), lambda i, ds: (i, 0)),
            pl.BlockSpec((1, 1, d), lambda i, ds: (i // seq_blocks, 0, 5)),
            pl.BlockSpec((1, d), lambda i, ds: (0, 0)),
        ],
        out_specs=pl.BlockSpec((rows, d), lambda i, ds: (i, 0)),
        scratch_shapes=[pltpu.VMEM((2, 2, rows, d), F32), pltpu.SemaphoreType.DMA((2,))],
    )
    return pl.pallas_call(
        functools.partial(_combine_kernel, nsteps=nsteps, n_tok=t),
        grid_spec=grid_spec,
        out_shape=jax.ShapeDtypeStruct((t, d), F32),
        compiler_params=pltpu.CompilerParams(dimension_semantics=("arbitrary",)),
        name="combine",
    )(dst, y_buf, x1, wcol, mod3, norm_final)


def _rope_tables(seq):
    pos = np.arange(seq)
    n_freq = HEAD_DIM // 4
    inv = ROPE_BASE ** (-np.arange(n_freq, dtype=np.float32) / n_freq)
    ang = np.concatenate([(pos // GRID_W)[:, None] * inv, (pos % GRID_W)[:, None] * inv], axis=-1)
    ang = np.concatenate([ang, ang, ang, ang], axis=-1).astype(np.float32)
    return jnp.asarray(np.cos(ang), F32), jnp.asarray(np.sin(ang), F32)


def _head_expand_mat():
    m = np.zeros((SSD_HEADS, SSD_INNER), np.float32)
    for hh in range(SSD_HEADS):
        m[hh, hh * SSD_HEAD_DIM:(hh + 1) * SSD_HEAD_DIM] = 1.0
    return jnp.asarray(m)


def kernel(x, c, ctx, c_ctx, w_ada, b_ada, norm1, w_in, conv_w, conv_b, dt_bias, a_log, d_skip, ssd_norm,
           attn_sinks, w_out, norm2, w_group, b_group, w_expert, b_expert, w_gate, w_up, w_down, norm_final):
    return _pipeline(x, c, ctx, c_ctx, w_ada, b_ada, norm1, w_in, conv_w, conv_b, dt_bias, a_log, d_skip,
                     ssd_norm, attn_sinks, w_out, norm2, w_group, b_group, w_expert, b_expert, w_gate, w_up,
                     w_down, norm_final)["out"]


def _pipeline(x, c, ctx, c_ctx, w_ada, b_ada, norm1, w_in, conv_w, conv_b, dt_bias, a_log, d_skip, ssd_norm,
              attn_sinks, w_out, norm2, w_group, b_group, w_expert, b_expert, w_gate, w_up, w_down, norm_final):
    bt, seq, d = x.shape
    n_ctx = ctx.shape[1]
    t = bt * seq
    layer = 0
    assert w_ada.shape[0] == 1 and seq % ROWS_IN == 0 and n_ctx % CHUNK == 0

    cc = jnp.zeros((SUBLANES, d), F32).at[:bt].set(c).at[bt].set(c_ctx)
    mod = _ada(cc, w_ada[layer], b_ada[layer][None, :])
    mod3 = mod.reshape(SUBLANES, 1, 6 * d)

    w = w_in[layer]
    o_z, o_xbc, o_dt = 0, SSD_INNER, SSD_INNER + CONV_DIM
    o_q = o_dt + 2 * SSD_HEADS
    o_k, o_v = o_q + ATTN_WIDTH, o_q + ATTN_WIDTH + KV_WIDTH
    q_perm = np.concatenate([np.r_[j * HEAD_DIM:(j + 1) * HEAD_DIM, (j + 4) * HEAD_DIM:(j + 5) * HEAD_DIM]
                             for j in range(Q_PER_KV)])
    w_q = w[:, o_q:o_q + ATTN_WIDTH].reshape(d, ATTN_KV_HEADS, Q_PER_KV, HEAD_DIM).transpose(0, 2, 1, 3)
    w_q = w_q.reshape(d, ATTN_WIDTH)
    w_cat = jnp.concatenate([w[:, o_z:o_z + SSD_INNER], w[:, o_xbc:o_xbc + CONV_DIM], w_q,
                             w[:, o_k:o_k + KV_WIDTH], w[:, o_v:o_v + KV_WIDTH]], axis=1).astype(BF16)
    w_dtT = w[:, o_dt:o_dt + 2 * SSD_HEADS].T.astype(BF16)

    seq_blocks = seq // ROWS_IN
    cos, sin = _rope_tables(seq)
    bias_rows = jnp.broadcast_to(dt_bias[layer].reshape(2 * SSD_HEADS, 1), (2 * SSD_HEADS, LANES))
    alog_rows = jnp.broadcast_to(a_log[layer].reshape(2 * SSD_HEADS, 1), (2 * SSD_HEADS, LANES))
    segs_x = [("z", _COL_Z, SSD_INNER), ("xbc", _COL_XBC, CONV_DIM), ("dtT", 0, 0),
              ("q", _COL_Q, ATTN_WIDTH), ("k", _COL_K, KV_WIDTH), ("v", _COL_V, KV_WIDTH)]
    z_x, xbc_x, dt_x, pre_x, suf_x, q_x, k_x, v_x = _inproj(
        x.reshape(t, d), mod3, norm1[layer][None, :], w_cat, w_dtT, bias_rows, alog_rows, (cos, sin),
        rows=ROWS_IN, mod_row_fn=lambda i: i // seq_blocks, segs=segs_x, seq_blocks=seq_blocks)
    segs_c = [("xbc", _COL_XBC, CONV_DIM), ("dtT", 0, 0), ("k", _COL_K, KV_WIDTH), ("v", _COL_V, KV_WIDTH)]
    xbc_c, dt_c, pre_c, suf_c, k_c, v_c = _inproj(
        ctx.reshape(bt * n_ctx, d), mod3, norm1[layer][None, :], w_cat, w_dtT, bias_rows, alog_rows, None,
        rows=n_ctx, mod_row_fn=lambda i: bt, segs=segs_c, seq_blocks=1)

    cw = jnp.zeros((SUBLANES, CONV_DIM), F32).at[:CONV_WIDTH].set(conv_w[layer])
    cb = conv_b[layer][None, :]
    dskip_e = jnp.repeat(d_skip[layer], SSD_HEAD_DIM)[None, :]
    consts = (cw, cb, dskip_e, ssd_norm[layer][None, :], _head_expand_mat())

    zeros_state = jnp.zeros((bt, SSD_STATE, SSD_INNER), F32)
    z_dummy = jnp.zeros((bt, n_ctx, SSD_INNER), F32)
    h_cf, h_cb = _ssd(xbc_c.reshape(bt, n_ctx, CONV_DIM), (dt_c, pre_c, suf_c), z_dummy, consts,
                      zeros_state, zeros_state, rows=n_ctx, need_y=False)
    ssd_x, _, _ = _ssd(xbc_x.reshape(bt, seq, CONV_DIM), (dt_x, pre_x, suf_x), z_x.reshape(bt, seq, SSD_INNER),
                       consts, h_cf, h_cb, rows=ROWS_SSD, need_y=True)

    attn_x = _attention(attn_sinks[layer].astype(F32) * LOG2E,
                        q_x.reshape(bt, seq, ATTN_WIDTH), k_x.reshape(bt, seq, KV_WIDTH),
                        v_x.reshape(bt, seq, KV_WIDTH), k_c.reshape(bt, n_ctx, KV_WIDTH),
                        v_c.reshape(bt, n_ctx, KV_WIDTH))

    wo = w_out[layer]
    wa = wo[:SSD_INNER].astype(BF16)
    wb = wo[SSD_INNER:].reshape(ATTN_KV_HEADS, Q_PER_KV, HEAD_DIM, d).transpose(1, 0, 2, 3)
    wb = wb.reshape(ATTN_WIDTH, d).astype(BF16)
    wr = jnp.zeros((ROUTE_ROWS, d), F32).at[:N_GROUPS].set(w_group[layer].T)
    wr = wr.at[N_GROUPS:N_GROUPS + N_EXPERTS].set(w_expert[layer].T).astype(BF16)
    br = jnp.zeros((ROUTE_ROWS,), F32).at[:N_GROUPS].set(b_group[layer])
    br = br.at[N_GROUPS:N_GROUPS + N_EXPERTS].set(b_expert[layer])
    br = jnp.broadcast_to(br[:, None], (ROUTE_ROWS, ROWS_IN))
    tri = jnp.asarray(np.triu(np.ones((ROWS_IN, ROWS_IN), np.float32), 1), BF16)
    x1, h2, ids, wcol, cnt = _outproj(
        ssd_x.reshape(t, SSD_INNER), attn_x.reshape(t, ATTN_WIDTH), x.reshape(t, d), mod3, norm2[layer][None, :],
        wa, wb, wr, br, tri, rows=ROWS_IN, seq_blocks=seq_blocks)

    counts = cnt[N_GROUPS:N_GROUPS + N_EXPERTS, 0].astype(jnp.int32)
    padded = (counts + MOE_BLK - 1) // MOE_BLK * MOE_BLK
    ends = jnp.cumsum(padded)
    pstart = ends - padded
    nblocks = (2 * t) // MOE_BLK + N_EXPERTS
    blk_start = jnp.arange(nblocks, dtype=jnp.int32) * MOE_BLK
    eids = jnp.arange(N_EXPERTS, dtype=jnp.int32)
    active = counts > 0
    last_active = jnp.max(jnp.where(active, eids, 0))
    blk_e = jnp.minimum(jnp.sum((ends[None, :] <= blk_start[:, None]).astype(jnp.int32), axis=1), last_active)
    nused = (ends[-1] // MOE_BLK).astype(jnp.int32).reshape(1)
    eord = jnp.cumsum(active.astype(jnp.int32)) - active.astype(jnp.int32)
    later = active[None, :] & (eids[None, :] > eids[:, None])
    enext = jnp.min(jnp.where(later, eids[None, :], N_EXPERTS), axis=1)
    enext = jnp.where(enext == N_EXPERTS, -1, enext).astype(jnp.int32)
    code = ids[0:2].reshape(-1)
    e_id = lax.shift_right_logical(code, RANK_BITS)
    onehot = e_id[:, None] == jnp.arange(N_EXPERTS, dtype=jnp.int32)[None, :]
    dst = jnp.sum(jnp.where(onehot, pstart[None, :], 0), axis=1) + (code & ((1 << RANK_BITS) - 1))

    y_buf = _experts(blk_e, nused, pstart, counts, eord, enext, dst, h2, w_gate[layer], w_up[layer],
                     w_down[layer], nblocks=nblocks, n_tok=t)
    out = _combine(dst, y_buf, x1, wcol, mod3, norm_final[None, :], rows=ROWS_CMB, seq_blocks=seq // ROWS_CMB)
    return dict(out=out.reshape(bt, seq, d), mod=mod, z=z_x, xbc=xbc_x, dt=dt_x, q=q_x, k=k_x, v=v_x,
                k_c=k_c, v_c=v_c, h_cf=h_cf, h_cb=h_cb, ssd=ssd_x, attn=attn_x, x1=x1, h2=h2, ids=ids,
                wcol=wcol, cnt=cnt, y_buf=y_buf, pstart=pstart, q_perm=q_perm)
```

```python
import functools
import math

import numpy as np
import jax
import jax.numpy as jnp
from jax import lax
from jax.experimental import pallas as pl
from jax.experimental.pallas import tpu as pltpu

F32 = jnp.float32
BF16 = jnp.bfloat16

D_MODEL = 1024
GRID_W = 64
EPS = 1e-6
SSD_INNER = 512
SSD_HEAD_DIM = 64
SSD_HEADS = 8
SSD_GROUPS = 2
SSD_STATE = 128
CONV_WIDTH = 5
CONV_DIM = SSD_INNER + 2 * SSD_GROUPS * SSD_STATE
CHUNK = 128
ATTN_WIDTH = 512
HEAD_DIM = 64
ATTN_Q_HEADS = 8
ATTN_KV_HEADS = 2
Q_PER_KV = 4
KV_WIDTH = 128
WINDOW = 128
ROPE_BASE = 10000.0
N_GROUPS = 4
EXPERTS_PER_GROUP = 8
N_EXPERTS = 32
EXPERT_DIM = 512

LANES = 128
SUBLANES = 8
NEG = -1e30
LOG2E = math.log2(math.e)

ROWS_IN = 1024
ROWS_OUT = 512
ROWS_SSD = 1024
ROWS_ATT = 512
ROUTE_ROWS = 48
MOE_BLK = 256
RANK_BITS = 16
D_TILES = D_MODEL // LANES
ROWS_CMB = 128


def _silu(v):
    half = 0.5 * v
    return half + half * jnp.tanh(half)


def _softplus(v):
    return jnp.maximum(v, 0.0) + jnp.log(1.0 + jnp.exp(-jnp.abs(v)))


def _ada_kernel(c_ref, w_ref, b_ref, o_ref):
    a = _silu(c_ref[...])
    o_ref[...] = jnp.dot(a.astype(BF16), w_ref[...].astype(BF16),
                         preferred_element_type=F32) + b_ref[...]


def _ada(cc, w_ada, b_ada):
    d, n = w_ada.shape
    tn = 1536
    return pl.pallas_call(
        _ada_kernel,
        grid=(n // tn,),
        in_specs=[pl.BlockSpec((SUBLANES, d), lambda j: (0, 0)),
                  pl.BlockSpec((d, tn), lambda j: (0, j)),
                  pl.BlockSpec((1, tn), lambda j: (0, j))],
        out_specs=pl.BlockSpec((SUBLANES, tn), lambda j: (0, j)),
        out_shape=jax.ShapeDtypeStruct((SUBLANES, n), F32),
        name="ada",
    )(cc, w_ada, b_ada)


def _rope(t, cos, sin_signed, first_half):
    rot = jnp.where(first_half, pltpu.roll(t, LANES - HEAD_DIM // 2, 1), pltpu.roll(t, HEAD_DIM // 2, 1))
    return t * cos + rot * sin_signed


def _lane_cumsum(v, lane, reverse):
    k = 1
    while k < LANES:
        if reverse:
            v = v + jnp.where(lane < LANES - k, pltpu.roll(v, LANES - k, 1), 0.0)
        else:
            v = v + jnp.where(lane >= k, pltpu.roll(v, k, 1), 0.0)
        k *= 2
    return v


def _inproj_kernel(*refs, segs, rope):
    x_ref, sh_ref, sc_ref, g_ref, w_ref, wdt_ref, bias_ref, alog_ref = refs[:8]
    pos = 8
    if rope:
        cos_ref, sin_ref = refs[8:10]
        pos = 10
    outs = iter(refs[pos:])
    x = x_ref[...]
    ms = jnp.mean(x * x, axis=-1, keepdims=True)
    h = (x * lax.rsqrt(ms + EPS)) * (g_ref[...] * (1.0 + sc_ref[0])) + sh_ref[0]
    hb = h.astype(BF16)
    if rope:
        cos = cos_ref[...]
        lane = lax.broadcasted_iota(jnp.int32, cos.shape, 1)
        first_half = (lane & (HEAD_DIM - 1)) < HEAD_DIM // 2
        sin_s = jnp.where(first_half, -sin_ref[...], sin_ref[...])
    for kind, c0, width in segs:
        if kind == "dtT":
            dt_ref, pre_ref, suf_ref = next(outs), next(outs), next(outs)
            dtt = lax.dot_general(wdt_ref[...], hb, (((1,), (1,)), ((), ())), preferred_element_type=F32)
            lane16 = lax.broadcasted_iota(jnp.int32, (2 * SSD_HEADS, CHUNK), 1)
            neg_a = -jnp.exp(alog_ref[...])
            for j in range(dt_ref.shape[0]):
                dt = _softplus(dtt[:, j * CHUNK:(j + 1) * CHUNK] + bias_ref[...])
                a = dt * neg_a
                dt_ref[j] = dt
                pre_ref[j] = _lane_cumsum(a, lane16, False)
                suf_ref[j] = _lane_cumsum(a, lane16, True)
            continue
        o_ref = next(outs)
        for j0 in range(0, width, 512):
            wj = min(512, width - j0)
            acc = jnp.dot(hb, w_ref[:, c0 + j0:c0 + j0 + wj], preferred_element_type=F32)
            if rope and kind in ("q", "k"):
                scale = HEAD_DIM ** -0.5 * LOG2E if kind == "q" else 1.0
                for l0 in range(0, wj, LANES):
                    t = _rope(acc[:, l0:l0 + LANES], cos, sin_s, first_half)
                    o_ref[:, j0 + l0:j0 + l0 + LANES] = (t * scale).astype(o_ref.dtype)
            else:
                o_ref[:, j0:j0 + wj] = acc.astype(o_ref.dtype)


_COL_Z, _COL_XBC, _COL_Q, _COL_K, _COL_V = 0, 512, 1536, 2048, 2176
_W_COLS = 2304


def _inproj(xf, mod3, norm1, w_cat, w_dtT, bias_rows, alog_rows, rope_tabs, *, rows, mod_row_fn, segs, seq_blocks):
    t, d = xf.shape
    rope = rope_tabs is not None
    in_specs = [
        pl.BlockSpec((rows, d), lambda i: (i, 0)),
        pl.BlockSpec((1, 1, d), lambda i: (mod_row_fn(i), 0, 0)),
        pl.BlockSpec((1, 1, d), lambda i: (mod_row_fn(i), 0, 1)),
        pl.BlockSpec((1, d), lambda i: (0, 0)),
        pl.BlockSpec(w_cat.shape, lambda i: (0, 0)),
        pl.BlockSpec(w_dtT.shape, lambda i: (0, 0)),
        pl.BlockSpec(bias_rows.shape, lambda i: (0, 0)),
        pl.BlockSpec(alog_rows.shape, lambda i: (0, 0)),
    ]
    args = [xf, mod3, mod3, norm1, w_cat, w_dtT, bias_rows, alog_rows]
    if rope:
        in_specs += [pl.BlockSpec((rows, LANES), lambda i: (i % seq_blocks, 0))] * 2
        args += list(rope_tabs)
    out_specs, out_shapes = [], []
    for kind, _, width in segs:
        if kind == "dtT":
            for _ in range(3):
                out_specs.append(pl.BlockSpec((rows // CHUNK, 2 * SSD_HEADS, CHUNK), lambda i: (i, 0, 0)))
                out_shapes.append(jax.ShapeDtypeStruct((t // CHUNK, 2 * SSD_HEADS, CHUNK), F32))
        else:
            dt = BF16 if kind in ("q", "k", "v") else F32
            out_specs.append(pl.BlockSpec((rows, width), lambda i: (i, 0)))
            out_shapes.append(jax.ShapeDtypeStruct((t, width), dt))
    return pl.pallas_call(
        functools.partial(_inproj_kernel, segs=tuple(segs), rope=rope),
        grid=(t // rows,),
        in_specs=in_specs,
        out_specs=out_specs,
        out_shape=out_shapes,
        name="inproj_rope" if rope else "inproj_ctx",
    )(*args)


_XB_COLS = SSD_INNER + SSD_GROUPS * SSD_STATE
_Y_OFF = CONV_DIM - _XB_COLS


def _ssd_kernel(*refs, need_y, nsteps, cps):
    (xbc_ref, prev_ref, next_ref, dt_ref, pre_ref, suf_ref, z_ref, cw_ref, cb_ref,
     dskip_ref, gain_ref, exp_ref, h0f_ref, h0b_ref) = refs[:14]
    if need_y:
        y_ref, hf_out, hb_out = refs[14:17]
        rest = refs[17:]
    else:
        y_ref = None
        hf_out, hb_out = refs[14:16]
        rest = refs[16:]
    cwin, u_ref, hf, hb, hbs, ust = rest

    sw = pl.program_id(1)
    s = pl.program_id(2)
    bi = jnp.where(sw == 0, nsteps - 1 - s, s)
    rows = xbc_ref.shape[1]
    h = SSD_HEADS

    @pl.when((sw == 0) & (s == 0))
    def _():
        hb[...] = h0b_ref[0]

    @pl.when((sw == 1) & (s == 0))
    def _():
        hf[...] = h0f_ref[0]

    ri = lax.broadcasted_iota(jnp.int32, (CHUNK, CHUNK), 0)
    ci = lax.broadcasted_iota(jnp.int32, (CHUNK, CHUNK), 1)
    lane128 = lax.broadcasted_iota(jnp.int32, (CHUNK, LANES), 1)
    lo_mask = lane128 < SSD_HEAD_DIM

    def conv_silu(c, c0, c1, out_ref):
        r0 = pl.multiple_of(c * CHUNK, CHUNK)
        above = xbc_ref[0, pl.ds(pl.multiple_of(jnp.maximum(r0 - SUBLANES, 0), SUBLANES), SUBLANES), c0:c1]
        halo_a = jnp.where(bi > 0, prev_ref[0, :, c0:c1], 0.0)
        cwin[0:SUBLANES, c0:c1] = jnp.where(c > 0, above, halo_a)
        cwin[SUBLANES:SUBLANES + CHUNK, c0:c1] = xbc_ref[0, pl.ds(r0, CHUNK), c0:c1]
        below = xbc_ref[0, pl.ds(pl.multiple_of(jnp.minimum(r0 + CHUNK, rows - SUBLANES), SUBLANES), SUBLANES), c0:c1]
        halo_b = jnp.where(bi < nsteps - 1, next_ref[0, :, c0:c1], 0.0)
        cwin[SUBLANES + CHUNK:, c0:c1] = jnp.where(c < cps - 1, below, halo_b)
        for j0 in range(c0, c1, LANES):
            acc = jnp.broadcast_to(cb_ref[:, j0:j0 + LANES], (CHUNK, LANES))
            for k in range(CONV_WIDTH):
                off = SUBLANES - CONV_WIDTH // 2 + k
                acc = acc + cwin[off:off + CHUNK, j0:j0 + LANES] * cw_ref[k:k + 1, j0:j0 + LANES]
            out_ref[:, j0 - c0:j0 - c0 + LANES] = _silu(acc)

    def dt_rows(c):
        return dt_ref[c], pre_ref[c], suf_ref[c]

    def state_update(state_ref, xb, w_rows, tot_col):
        dec = jnp.exp(jnp.sum(tot_col * exp_ref[...], axis=0, keepdims=True))
        for g in range(SSD_GROUPS):
            bm = xb[:, SSD_INNER + g * SSD_STATE:SSD_INNER + (g + 1) * SSD_STATE]
            bt = bm.T
            for pr in range(2):
                h0 = g * 4 + pr * 2
                c0 = h0 * SSD_HEAD_DIM
                xp = xb[:, c0:c0 + LANES]
                rhs = jnp.concatenate([jnp.where(lo_mask, xp, 0.0), jnp.where(lo_mask, 0.0, xp)],
                                      axis=0).astype(BF16)
                lhs = jnp.concatenate([bt * w_rows[h0:h0 + 1, :], bt * w_rows[h0 + 1:h0 + 2, :]],
                                      axis=1).astype(BF16)
                sres = jnp.dot(lhs, rhs, preferred_element_type=F32)
                state_ref[:, c0:c0 + LANES] = state_ref[:, c0:c0 + LANES] * dec[:, c0:c0 + LANES] + sres

    def col_forms(rows16):
        padded = jnp.concatenate([rows16, jnp.zeros((CHUNK - 2 * h, LANES), F32)], axis=0)
        return padded.T

    def backward_chunk(k, carry):
        c = cps - 1 - k
        gc = bi * cps + c
        xb = ust.at[gc]
        conv_silu(c, 0, _XB_COLS, xb)
        dt, _, suf = dt_rows(c)
        hbs[gc] = hb[...].astype(BF16)
        s0 = suf[:, 0:1]
        w_rows = jnp.exp(s0 - (suf - jnp.log(dt)))
        state_update(hb, xb, w_rows[h:2 * h, :], s0[h:2 * h, :])
        return carry

    def forward_chunk(c, carry):
        gc = bi * cps + c
        r0 = pl.multiple_of(c * CHUNK, CHUNK)
        xb = ust.at[gc]
        conv_silu(c, _XB_COLS, CONV_DIM, u_ref)
        dt, pre, suf = dt_rows(c)
        rowsf = jnp.concatenate([pre[0:h, :], suf[h:2 * h, :]], axis=0)
        colsf = col_forms(rowsf)
        ecol = jnp.exp(colsf)
        rowsd = rowsf - jnp.log(dt)
        for g in range(SSD_GROUPS):
            bm = xb[:, SSD_INNER + g * SSD_STATE:SSD_INNER + (g + 1) * SSD_STATE]
            cm = u_ref[:, g * SSD_STATE:(g + 1) * SSD_STATE]
            cmb = cm.astype(BF16)
            cbm = lax.dot_general(cmb, bm.astype(BF16), (((1,), (1,)), ((), ())),
                                  preferred_element_type=F32)
            for pr in range(2):
                h0 = g * 4 + pr * 2
                c0 = h0 * SSD_HEAD_DIM
                lhs_parts = []
                for hh in (h0, h0 + 1):
                    segf = colsf[:, hh:hh + 1] - rowsd[hh:hh + 1, :]
                    gf = jnp.exp(jnp.where(ci <= ri, segf, NEG))
                    segb = colsf[:, h + hh:h + hh + 1] - rowsd[h + hh:h + hh + 1, :]
                    gb = jnp.exp(jnp.where(ci >= ri, segb, NEG))
                    lhs_parts.append((cbm * (gf + gb)).astype(BF16))
                for hh in (h0, h0 + 1):
                    lhs_parts.append((cm * ecol[:, hh:hh + 1]).astype(BF16))
                for hh in (h0, h0 + 1):
                    lhs_parts.append((cm * ecol[:, h + hh:h + hh + 1]).astype(BF16))
                lhs = jnp.concatenate(lhs_parts, axis=1)
                xp = xb[:, c0:c0 + LANES]
                sf = hf[:, c0:c0 + LANES]
                sb = hbs[gc, :, c0:c0 + LANES].astype(F32)
                rhs = jnp.concatenate(
                    [jnp.where(lo_mask, xp, 0.0), jnp.where(lo_mask, 0.0, xp),
                     jnp.where(lo_mask, sf, 0.0), jnp.where(lo_mask, 0.0, sf),
                     jnp.where(lo_mask, sb, 0.0), jnp.where(lo_mask, 0.0, sb)], axis=0).astype(BF16)
                ypair = jnp.dot(lhs, rhs, preferred_element_type=F32)
                ypair = ypair + dskip_ref[:, c0:c0 + LANES] * xp
                zz = z_ref[0, pl.ds(r0, CHUNK), c0:c0 + LANES]
                u_ref[:, _Y_OFF + c0:_Y_OFF + c0 + LANES] = ypair * _silu(zz)
        last = pre[:, LANES - 1:LANES]
        w_rows = jnp.exp(last - rowsd)
        state_update(hf, xb, w_rows[0:h, :], last[0:h, :])
        if need_y:
            yv = u_ref[:, _Y_OFF:_Y_OFF + SSD_INNER]
            ms = jnp.mean(yv * yv, axis=-1, keepdims=True)
            y_ref[0, pl.ds(r0, CHUNK), :] = (yv * lax.rsqrt(ms + EPS) * gain_ref[...]).astype(y_ref.dtype)
        return carry

    @pl.when(sw == 0)
    def _():
        lax.fori_loop(0, cps, backward_chunk, 0, unroll=2)

    @pl.when(sw == 1)
    def _():
        lax.fori_loop(0, cps, forward_chunk, 0, unroll=2)

    @pl.when((sw == 0) & (s == nsteps - 1))
    def _():
        hb_out[0] = hb[...]

    @pl.when((sw == 1) & (s == nsteps - 1))
    def _():
        hf_out[0] = hf[...]


def _ssd(xbc, dts, z, consts, h0f, h0b, *, rows, need_y):
    bt, seq, _ = xbc.shape
    nsteps = seq // rows
    cps = rows // CHUNK
    hb8 = rows // SUBLANES
    nb8 = seq // SUBLANES
    cw, cb, dskip, gain, expm = consts
    dt_spec = pl.BlockSpec((cps, 2 * SSD_HEADS, CHUNK), lambda b, sw, s: (b * nsteps + blk(b, sw, s), 0, 0))

    def blk(b, sw, s):
        return jnp.where(sw == 0, nsteps - 1 - s, s)

    def full(a):
        return pl.BlockSpec(a.shape, lambda b, sw, s: (0,) * a.ndim)

    in_specs = [
        pl.BlockSpec((1, rows, CONV_DIM), lambda b, sw, s: (b, blk(b, sw, s), 0)),
        pl.BlockSpec((1, SUBLANES, CONV_DIM), lambda b, sw, s: (b, jnp.maximum(blk(b, sw, s) * hb8 - 1, 0), 0)),
        pl.BlockSpec((1, SUBLANES, CONV_DIM),
                     lambda b, sw, s: (b, jnp.minimum((blk(b, sw, s) + 1) * hb8, nb8 - 1), 0)),
        dt_spec, dt_spec, dt_spec,
        pl.BlockSpec((1, rows, SSD_INNER), lambda b, sw, s: (b, jnp.where(sw == 0, 0, s), 0)),
        full(cw), full(cb), full(dskip), full(gain), full(expm),
        pl.BlockSpec((1, SSD_STATE, SSD_INNER), lambda b, sw, s: (b, 0, 0)),
        pl.BlockSpec((1, SSD_STATE, SSD_INNER), lambda b, sw, s: (b, 0, 0)),
    ]
    st_spec = pl.BlockSpec((1, SSD_STATE, SSD_INNER), lambda b, sw, s: (b, 0, 0))
    st_shape = jax.ShapeDtypeStruct((bt, SSD_STATE, SSD_INNER), F32)
    out_specs, out_shapes = [st_spec, st_spec], [st_shape, st_shape]
    if need_y:
        out_specs = [pl.BlockSpec((1, rows, SSD_INNER), lambda b, sw, s: (b, jnp.where(sw == 0, 0, s), 0))] + out_specs
        out_shapes = [jax.ShapeDtypeStruct((bt, seq, SSD_INNER), BF16)] + out_shapes
    scratch = [
        pltpu.VMEM((CHUNK + 2 * SUBLANES, CONV_DIM), F32),
        pltpu.VMEM((CHUNK, _Y_OFF + SSD_INNER), F32),
        pltpu.VMEM((SSD_STATE, SSD_INNER), F32),
        pltpu.VMEM((SSD_STATE, SSD_INNER), F32),
        pltpu.VMEM((seq // CHUNK, SSD_STATE, SSD_INNER), BF16),
        pltpu.VMEM((seq // CHUNK, CHUNK, _XB_COLS), F32),
    ]
    return pl.pallas_call(
        functools.partial(_ssd_kernel, need_y=need_y, nsteps=nsteps, cps=cps),
        grid=(bt, 2, nsteps),
        in_specs=in_specs,
        out_specs=out_specs,
        out_shape=out_shapes,
        scratch_shapes=scratch,
        compiler_params=pltpu.CompilerParams(dimension_semantics=("arbitrary", "arbitrary", "arbitrary")),
        name="ssd_y" if need_y else "ssd_ctx",
    )(xbc, xbc, xbc, *dts, z, cw, cb, dskip, gain, expm, h0f, h0b)


def _attn_kernel(sink_ref, q_ref, kp_ref, kc_ref, kn_ref, vp_ref, vc_ref, vn_ref, kx_ref, vx_ref, o_ref, *, nsteps):
    m = pl.program_id(1)
    lane = lax.broadcasted_iota(jnp.int32, (1, LANES), 1)
    lo = lane < HEAD_DIM
    rq = Q_PER_KV * WINDOW
    per = ROWS_ATT // WINDOW
    qi = lax.broadcasted_iota(jnp.int32, (rq, WINDOW), 0) & (WINDOW - 1)
    kj = lax.broadcasted_iota(jnp.int32, (rq, WINDOW), 1)
    rblk = lax.shift_right_logical(lax.broadcasted_iota(jnp.int32, (rq, 1), 0), int(math.log2(WINDOW)))
    zero = jnp.zeros((), BF16)
    band_prev = jnp.where(kj >= qi, 0.0, NEG)
    band_next = jnp.where(kj <= qi, 0.0, NEG)

    kpieces = [kp_ref[0]] + [kc_ref[0, w * WINDOW:(w + 1) * WINDOW] for w in range(per)] + [kn_ref[0]]
    vpieces = [vp_ref[0]] + [vc_ref[0, w * WINDOW:(w + 1) * WINDOW] for w in range(per)] + [vn_ref[0]]
    kx = kx_ref[0]
    vx = vx_ref[0]
    nt = (((1,), (1,)), ((), ()))

    for sb in range(per):
        qsb = q_ref[0, sb * WINDOW:(sb + 1) * WINDOW, :]
        mask_prev = band_prev if sb > 0 else jnp.where(m > 0, band_prev, NEG)
        mask_next = band_next if sb < per - 1 else jnp.where(m < nsteps - 1, band_next, NEG)
        outs = [None] * Q_PER_KV
        for kv in range(ATTN_KV_HEADS):
            sel = lo if kv == 0 else jnp.logical_not(lo)
            qs = jnp.concatenate(
                [jnp.where(sel, qsb[:, j * LANES:(j + 1) * LANES], zero) for j in range(Q_PER_KV)], axis=0)
            s_prev = lax.dot_general(qs, kpieces[sb], nt, preferred_element_type=F32) + mask_prev
            s_cur = lax.dot_general(qs, kpieces[sb + 1], nt, preferred_element_type=F32)
            s_next = lax.dot_general(qs, kpieces[sb + 2], nt, preferred_element_type=F32) + mask_next
            s_ctx = lax.dot_general(qs, kx, nt, preferred_element_type=F32)
            sink = jnp.zeros((rq, 1), F32)
            for j in range(Q_PER_KV):
                sink = jnp.where(rblk == j, sink_ref[kv * Q_PER_KV + j], sink)
            n_ct = s_ctx.shape[1] // LANES
            tile_max = jnp.maximum(jnp.maximum(s_prev, s_cur), s_next)
            for t in range(n_ct):
                tile_max = jnp.maximum(tile_max, s_ctx[:, t * LANES:(t + 1) * LANES])
            mx = jnp.maximum(jnp.max(tile_max, axis=-1, keepdims=True), sink)
            p_prev = jnp.exp2(s_prev - mx)
            p_cur = jnp.exp2(s_cur - mx)
            p_next = jnp.exp2(s_next - mx)
            p_ctx = jnp.exp2(s_ctx - mx)
            tile_sum = p_prev + p_cur + p_next
            for t in range(n_ct):
                tile_sum = tile_sum + p_ctx[:, t * LANES:(t + 1) * LANES]
            den = jnp.sum(tile_sum, axis=-1, keepdims=True) + jnp.exp2(sink - mx)
            acc = jnp.dot(p_prev.astype(BF16), jnp.where(sel, vpieces[sb], zero), preferred_element_type=F32)
            acc += jnp.dot(p_cur.astype(BF16), jnp.where(sel, vpieces[sb + 1], zero), preferred_element_type=F32)
            acc += jnp.dot(p_next.astype(BF16), jnp.where(sel, vpieces[sb + 2], zero), preferred_element_type=F32)
            acc += jnp.dot(p_ctx.astype(BF16), jnp.where(sel, vx, zero), preferred_element_type=F32)
            acc = acc * (1.0 / den)
            for j in range(Q_PER_KV):
                part = acc[j * WINDOW:(j + 1) * WINDOW, :]
                outs[j] = part if outs[j] is None else outs[j] + part
        for j in range(Q_PER_KV):
            o_ref[0, sb * WINDOW:(sb + 1) * WINDOW, j * LANES:(j + 1) * LANES] = outs[j].astype(o_ref.dtype)


def _attention(sinks, q, k, v, kx, vx):
    bt, seq, _ = q.shape
    nsteps = seq // ROWS_ATT
    per = ROWS_ATT // WINDOW
    nb = seq // WINDOW
    n_ctx = kx.shape[1]

    def prev(b, m, s):
        return (b, jnp.maximum(m * per - 1, 0), 0)

    def cur(b, m, s):
        return (b, m, 0)

    def nxt(b, m, s):
        return (b, jnp.minimum((m + 1) * per, nb - 1), 0)

    small = (1, WINDOW, KV_WIDTH)
    big = (1, ROWS_ATT, KV_WIDTH)
    grid_spec = pltpu.PrefetchScalarGridSpec(
        num_scalar_prefetch=1,
        grid=(bt, nsteps),
        in_specs=[
            pl.BlockSpec((1, ROWS_ATT, ATTN_WIDTH), cur),
            pl.BlockSpec(small, prev), pl.BlockSpec(big, cur), pl.BlockSpec(small, nxt),
            pl.BlockSpec(small, prev), pl.BlockSpec(big, cur), pl.BlockSpec(small, nxt),
            pl.BlockSpec((1, n_ctx, KV_WIDTH), lambda b, m, s: (b, 0, 0)),
            pl.BlockSpec((1, n_ctx, KV_WIDTH), lambda b, m, s: (b, 0, 0)),
        ],
        out_specs=pl.BlockSpec((1, ROWS_ATT, ATTN_WIDTH), cur),
    )
    return pl.pallas_call(
        functools.partial(_attn_kernel, nsteps=nsteps),
        grid_spec=grid_spec,
        out_shape=jax.ShapeDtypeStruct((bt, seq, ATTN_WIDTH), BF16),
        name="attn",
    )(sinks, q, k, k, k, v, v, v, kx, vx)


def _outproj_kernel(ssd_ref, att_ref, x_ref, g1_ref, sh_ref, sc_ref, n2_ref, wa_ref, wb_ref, wr_ref, br_ref,
                    tri_ref, x1_ref, h2_ref, ids_ref, wcol_ref, cnt_ref, carry, *, nsteps):
    i = pl.program_id(0)
    rows = x_ref.shape[0]

    @pl.when(i == 0)
    def _():
        carry[...] = jnp.zeros_like(carry)

    acc = jnp.dot(ssd_ref[...], wa_ref[...], preferred_element_type=F32)
    acc += jnp.dot(att_ref[...], wb_ref[...], preferred_element_type=F32)
    x1 = x_ref[...] + g1_ref[0] * acc
    x1_ref[...] = x1
    ms = jnp.mean(x1 * x1, axis=-1, keepdims=True)
    h2 = (x1 * lax.rsqrt(ms + EPS)) * (n2_ref[...] * (1.0 + sc_ref[0])) + sh_ref[0]
    h2b = h2.astype(BF16)
    half = h2.shape[1] // 2
    lo_bits = lax.shift_right_logical(pltpu.bitcast(h2b[:, :half].astype(F32), jnp.uint32), jnp.uint32(16))
    hi_bits = pltpu.bitcast(h2b[:, half:].astype(F32), jnp.uint32) & jnp.uint32(0xFFFF0000)
    h2_ref[...] = hi_bits | lo_bits
    lt = lax.dot_general(wr_ref[...], h2b, (((1,), (1,)), ((), ())),
                         preferred_element_type=F32) + br_ref[...]
    row = lax.broadcasted_iota(jnp.int32, lt.shape, 0).astype(F32)
    big_i = float(ROUTE_ROWS)
    gl = jnp.where(row < N_GROUPS, lt, NEG)
    gmax = jnp.max(gl, axis=0, keepdims=True)
    gidx = jnp.min(jnp.where(gl == gmax, row, big_i), axis=0, keepdims=True)
    g_w = 1.0 / jnp.sum(jnp.exp(gl - gmax), axis=0, keepdims=True)
    lo = N_GROUPS + EXPERTS_PER_GROUP * gidx
    el = jnp.where((row >= lo) & (row < lo + EXPERTS_PER_GROUP), lt, NEG)
    m1 = jnp.max(el, axis=0, keepdims=True)
    i1 = jnp.min(jnp.where(el == m1, row, big_i), axis=0, keepdims=True)
    el2 = jnp.where(row == i1, NEG, el)
    m2 = jnp.max(el2, axis=0, keepdims=True)
    i2 = jnp.min(jnp.where(el2 == m2, row, big_i), axis=0, keepdims=True)
    r = jnp.exp(m2 - m1)
    w1 = g_w / (1.0 + r)
    w2 = g_w * r / (1.0 + r)
    oh1 = row == i1
    oh2 = row == i2
    cnt = jnp.where(oh1 | oh2, 1.0, 0.0)
    prefix = jnp.dot(cnt.astype(BF16), tri_ref[...], preferred_element_type=F32)
    base = carry[...] + prefix
    rank1 = jnp.sum(jnp.where(oh1, base, 0.0), axis=0, keepdims=True)
    rank2 = jnp.sum(jnp.where(oh2, base, 0.0), axis=0, keepdims=True)
    carry[...] = carry[...] + jnp.sum(cnt, axis=1, keepdims=True)
    r8 = lax.broadcasted_iota(jnp.int32, (SUBLANES, rows), 0)
    code1 = (i1 - N_GROUPS) * float(1 << RANK_BITS) + rank1
    code2 = (i2 - N_GROUPS) * float(1 << RANK_BITS) + rank2
    ids_ref[...] = jnp.where(r8 == 0, code1, jnp.where(r8 == 1, code2, 0.0)).astype(jnp.int32)
    r128 = lax.broadcasted_iota(jnp.int32, (LANES, rows), 0)
    wfull = jnp.where(r128 == 0, w1, jnp.where(r128 == 1, w2, 0.0))
    for j in range(rows // LANES):
        wcol_ref[j * LANES:(j + 1) * LANES, :] = wfull[:, j * LANES:(j + 1) * LANES].T

    @pl.when(i == nsteps - 1)
    def _():
        cnt_ref[...] = carry[:, 0:LANES]


def _outproj(ssd, att, xf, mod3, norm2, wa, wb, wr, br, tri, *, rows, seq_blocks):
    t, d = xf.shape
    nsteps = t // rows

    def mod(col):
        return pl.BlockSpec((1, 1, d), lambda i: (i // seq_blocks, 0, col))

    def full(a):
        return pl.BlockSpec(a.shape, lambda i: (0,) * a.ndim)

    return pl.pallas_call(
        functools.partial(_outproj_kernel, nsteps=nsteps),
        grid=(nsteps,),
        in_specs=[pl.BlockSpec((rows, SSD_INNER), lambda i: (i, 0)),
                  pl.BlockSpec((rows, ATTN_WIDTH), lambda i: (i, 0)),
                  pl.BlockSpec((rows, d), lambda i: (i, 0)),
                  mod(2), mod(3), mod(4), full(norm2), full(wa), full(wb), full(wr), full(br), full(tri)],
        out_specs=[pl.BlockSpec((rows, d), lambda i: (i, 0)),
                   pl.BlockSpec((rows, d // 2), lambda i: (i, 0)),
                   pl.BlockSpec((SUBLANES, rows), lambda i: (0, i)),
                   pl.BlockSpec((rows, LANES), lambda i: (i, 0)),
                   pl.BlockSpec((ROUTE_ROWS, LANES), lambda i: (0, 0))],
        out_shape=[jax.ShapeDtypeStruct((t, d), F32),
                   jax.ShapeDtypeStruct((t, d // 2), jnp.uint32),
                   jax.ShapeDtypeStruct((SUBLANES, t), jnp.int32),
                   jax.ShapeDtypeStruct((t, LANES), F32),
                   jax.ShapeDtypeStruct((ROUTE_ROWS, LANES), F32)],
        scratch_shapes=[pltpu.VMEM((ROUTE_ROWS, rows), F32)],
        compiler_params=pltpu.CompilerParams(dimension_semantics=("arbitrary",)),
        name="outproj_router",
    )(ssd, att, xf, mod3, mod3, mod3, norm2, wa, wb, wr, br, tri)


def _expert_kernel(blk_e_ref, nused_ref, pstart_ref, counts_ref, eord_ref, enext_ref, dst_ref,
                   h2_hbm, wg_hbm, wu_hbm, wd_hbm, y_hbm,
                   h2v, xg, ystage, wgs, wus, wds, wgb, wub, wdb, tok_ref, sem, wsem, ysem, *, n_tok, nblocks):
    nused = nused_ref[0]
    half = h2v.shape[1]

    def h2_copy():
        return pltpu.make_async_copy(h2_hbm, h2v, sem.at[0])

    def y_copy(blk, sl):
        row0 = blk * MOE_BLK if isinstance(blk, int) else pl.multiple_of(blk * MOE_BLK, MOE_BLK)
        return pltpu.make_async_copy(ystage.at[sl], y_hbm.at[pl.ds(row0, MOE_BLK)], ysem.at[sl])

    def weight_copies(e, sl):
        return (pltpu.make_async_copy(wg_hbm.at[e], wgs.at[sl], wsem.at[sl, 0]),
                pltpu.make_async_copy(wu_hbm.at[e], wus.at[sl], wsem.at[sl, 1]),
                pltpu.make_async_copy(wd_hbm.at[e], wds.at[sl], wsem.at[sl, 2]))

    def gather_rows(blk, sl):
        for r in range(MOE_BLK):
            xg[sl, pl.ds(r, 1), :] = h2v[pl.ds(tok_ref[blk * MOE_BLK + r], 1), :]

    h2_copy().start()
    for cp in weight_copies(blk_e_ref[0], 0):
        cp.start()

    def pad_expert(e, carry):
        lo = pstart_ref[e] + counts_ref[e]
        hi = pstart_ref[e] + (counts_ref[e] + MOE_BLK - 1) // MOE_BLK * MOE_BLK

        def pad_slot(s, c):
            tok_ref[s] = 0
            return c
        return lax.fori_loop(lo, hi, pad_slot, carry)
    lax.fori_loop(0, N_EXPERTS, pad_expert, 0)

    def claim(t, carry):
        tok_ref[dst_ref[t]] = t
        tok_ref[dst_ref[n_tok + t]] = t
        return carry
    lax.fori_loop(0, n_tok, claim, 0, unroll=16)
    h2_copy().wait()
    gather_rows(0, 0)

    def block(b, carry):
        slot = b % 2
        e = blk_e_ref[b]
        e_prev = blk_e_ref[jnp.maximum(b - 1, 0)]

        @pl.when(b >= 2)
        def _():
            y_copy(0, slot).wait()

        @pl.when((b == 0) | (e != e_prev))
        def _():
            sl = eord_ref[e] % 2
            for cp in weight_copies(e, sl):
                cp.wait()
            wgb[...] = wgs[sl].astype(BF16)
            wub[...] = wus[sl].astype(BF16)
            wdb[...] = wds[sl].astype(BF16)
            e_next = enext_ref[e]

            @pl.when(e_next >= 0)
            def _():
                for cp in weight_copies(e_next, 1 - sl):
                    cp.start()

        words = xg[slot]
        x_lo = pltpu.bitcast(words << jnp.uint32(16), F32).astype(BF16)
        x_hi = pltpu.bitcast(words & jnp.uint32(0xFFFF0000), F32).astype(BF16)
        gate = (jnp.dot(x_lo, wgb[0:half, :], preferred_element_type=F32)
                + jnp.dot(x_hi, wgb[half:, :], preferred_element_type=F32))
        up = (jnp.dot(x_lo, wub[0:half, :], preferred_element_type=F32)
              + jnp.dot(x_hi, wub[half:, :], preferred_element_type=F32))
        hmid = (_silu(gate) * up).astype(BF16)
        ystage[slot] = jnp.dot(hmid, wdb[...], preferred_element_type=F32)
        gather_rows(jnp.minimum(b + 1, nused - 1), 1 - slot)
        y_copy(b, slot).start()
        return carry
    lax.fori_loop(0, nused, block, 0)

    @pl.when(nused >= 2)
    def _():
        y_copy(0, nused % 2).wait()

    y_copy(0, (nused - 1) % 2).wait()
    ystage[0] = jnp.zeros(ystage.shape[1:], F32)

    def zero_block(b, carry):
        cp = y_copy(b, 0)
        cp.start()
        cp.wait()
        return carry
    lax.fori_loop(nused, nblocks, zero_block, 0)


def _experts(blk_e, nused, pstart, counts, eord, enext, dst, h2p, w_gate, w_up, w_down, *, nblocks, n_tok):
    d = w_gate.shape[1]
    hbm = pl.BlockSpec(memory_space=pl.ANY)
    grid_spec = pltpu.PrefetchScalarGridSpec(
        num_scalar_prefetch=7,
        grid=(1,),
        in_specs=[hbm, hbm, hbm, hbm],
        out_specs=hbm,
        scratch_shapes=[
            pltpu.VMEM(h2p.shape, jnp.uint32),
            pltpu.VMEM((2, MOE_BLK, d // 2), jnp.uint32),
            pltpu.VMEM((2, MOE_BLK, d), F32),
            pltpu.VMEM((2, d, EXPERT_DIM), F32),
            pltpu.VMEM((2, d, EXPERT_DIM), F32),
            pltpu.VMEM((2, EXPERT_DIM, d), F32),
            pltpu.VMEM((d, EXPERT_DIM), BF16),
            pltpu.VMEM((d, EXPERT_DIM), BF16),
            pltpu.VMEM((EXPERT_DIM, d), BF16),
            pltpu.SMEM((nblocks * MOE_BLK,), jnp.int32),
            pltpu.SemaphoreType.DMA((1,)),
            pltpu.SemaphoreType.DMA((2, 3)),
            pltpu.SemaphoreType.DMA((2,)),
        ],
    )
    return pl.pallas_call(
        functools.partial(_expert_kernel, n_tok=n_tok, nblocks=nblocks),
        grid_spec=grid_spec,
        out_shape=jax.ShapeDtypeStruct((nblocks * MOE_BLK, d), F32),
        compiler_params=pltpu.CompilerParams(dimension_semantics=("arbitrary",)),
        name="experts",
    )(blk_e, nused, pstart, counts, eord, enext, dst, h2p, w_gate, w_up, w_down)


def _combine_kernel(dst_ref, y_hbm, x1_ref, wcol_ref, g2_ref, nf_ref, o_ref, ybuf, sem, *, nsteps, n_tok):
    i = pl.program_id(0)
    rows = x1_ref.shape[0]
    slot = i % 2

    def row_copy(src, r, k, sl):
        return pltpu.make_async_copy(y_hbm.at[pl.ds(src, 1)], ybuf.at[sl, k, pl.ds(r, 1)], sem.at[sl])

    def start_rows(step, sl):
        for r in range(rows):
            for k in range(2):
                row_copy(dst_ref[k * n_tok + step * rows + r], r, k, sl).start()

    def wait_rows(sl):
        for r in range(rows):
            for k in range(2):
                row_copy(0, r, k, sl).wait()

    @pl.when(i == 0)
    def _():
        start_rows(0, 0)

    @pl.when(i + 1 < nsteps)
    def _():
        start_rows(i + 1, 1 - slot)

    wait_rows(slot)
    moe = ybuf[slot, 0] * wcol_ref[:, 0:1] + ybuf[slot, 1] * wcol_ref[:, 1:2]
    x2 = x1_ref[...] + g2_ref[0] * moe
    ms = jnp.mean(x2 * x2, axis=-1, keepdims=True)
    o_ref[...] = x2 * lax.rsqrt(ms + EPS) * nf_ref[...]


def _combine(dst, y_buf, x1, wcol, mod3, norm_final, *, rows, seq_blocks):
    t, d = x1.shape
    nsteps = t // rows
    grid_spec = pltpu.PrefetchScalarGridSpec(
        num_scalar_prefetch=1,
        grid=(nsteps,),
        in_specs=[
            pl.BlockSpec(memory_space=pl.ANY),
            pl.BlockSpec((rows, d), lambda i, ds: (i, 0)),
            pl.BlockSpec((rows, LANES), lambda i, ds: (i, 0)),
            pl.BlockSpec((1, 1, d), lambda i, ds: (i // seq_blocks, 0, 5)),
            pl.BlockSpec((1, d), lambda i, ds: (0, 0)),
        ],
        out_specs=pl.BlockSpec((rows, d), lambda i, ds: (i, 0)),
        scratch_shapes=[pltpu.VMEM((2, 2, rows, d), F32), pltpu.SemaphoreType.DMA((2,))],
    )
    return pl.pallas_call(
        functools.partial(_combine_kernel, nsteps=nsteps, n_tok=t),
        grid_spec=grid_spec,
        out_shape=jax.ShapeDtypeStruct((t, d), F32),
        compiler_params=pltpu.CompilerParams(dimension_semantics=("arbitrary",)),
        name="combine",
    )(dst, y_buf, x1, wcol, mod3, norm_final)


def _rope_tables(seq):
    pos = np.arange(seq)
    n_freq = HEAD_DIM // 4
    inv = ROPE_BASE ** (-np.arange(n_freq, dtype=np.float32) / n_freq)
    ang = np.concatenate([(pos // GRID_W)[:, None] * inv, (pos % GRID_W)[:, None] * inv], axis=-1)
    ang = np.concatenate([ang, ang, ang, ang], axis=-1).astype(np.float32)
    return jnp.asarray(np.cos(ang), F32), jnp.asarray(np.sin(ang), F32)


def _head_expand_mat():
    m = np.zeros((SSD_HEADS, SSD_INNER), np.float32)
    for hh in range(SSD_HEADS):
        m[hh, hh * SSD_HEAD_DIM:(hh + 1) * SSD_HEAD_DIM] = 1.0
    return jnp.asarray(m)


def kernel(x, c, ctx, c_ctx, w_ada, b_ada, norm1, w_in, conv_w, conv_b, dt_bias, a_log, d_skip, ssd_norm,
           attn_sinks, w_out, norm2, w_group, b_group, w_expert, b_expert, w_gate, w_up, w_down, norm_final):
    return _pipeline(x, c, ctx, c_ctx, w_ada, b_ada, norm1, w_in, conv_w, conv_b, dt_bias, a_log, d_skip,
                     ssd_norm, attn_sinks, w_out, norm2, w_group, b_group, w_expert, b_expert, w_gate, w_up,
                     w_down, norm_final)["out"]


def _pipeline(x, c, ctx, c_ctx, w_ada, b_ada, norm1, w_in, conv_w, conv_b, dt_bias, a_log, d_skip, ssd_norm,
              attn_sinks, w_out, norm2, w_group, b_group, w_expert, b_expert, w_gate, w_up, w_down, norm_final):
    bt, seq, d = x.shape
    n_ctx = ctx.shape[1]
    t = bt * seq
    layer = 0
    assert w_ada.shape[0] == 1 and seq % ROWS_IN == 0 and seq % ROWS_OUT == 0 and n_ctx % CHUNK == 0

    cc = jnp.concatenate([c, c_ctx[None, :], jnp.zeros((SUBLANES - bt - 1, d), F32)], axis=0)
    mod = _ada(cc, w_ada[layer], b_ada[layer][None, :])
    mod3 = mod.reshape(SUBLANES, 1, 6 * d)

    w = w_in[layer]
    o_z, o_xbc, o_dt = 0, SSD_INNER, SSD_INNER + CONV_DIM
    o_q = o_dt + 2 * SSD_HEADS
    o_k, o_v = o_q + ATTN_WIDTH, o_q + ATTN_WIDTH + KV_WIDTH
    q_perm = np.concatenate([np.r_[j * HEAD_DIM:(j + 1) * HEAD_DIM, (j + 4) * HEAD_DIM:(j + 5) * HEAD_DIM]
                             for j in range(Q_PER_KV)])
    w_q = w[:, o_q:o_q + ATTN_WIDTH].reshape(d, ATTN_KV_HEADS, Q_PER_KV, HEAD_DIM).transpose(0, 2, 1, 3)
    w_q = w_q.reshape(d, ATTN_WIDTH)
    w_cat = jnp.concatenate([w[:, o_z:o_z + SSD_INNER], w[:, o_xbc:o_xbc + CONV_DIM], w_q,
                             w[:, o_k:o_k + KV_WIDTH], w[:, o_v:o_v + KV_WIDTH]], axis=1).astype(BF16)
    w_dtT = w[:, o_dt:o_dt + 2 * SSD_HEADS].T.astype(BF16)

    seq_blocks = seq // ROWS_IN
    cos, sin = _rope_tables(seq)
    bias_rows = jnp.broadcast_to(dt_bias[layer].reshape(2 * SSD_HEADS, 1), (2 * SSD_HEADS, LANES))
    alog_rows = jnp.broadcast_to(a_log[layer].reshape(2 * SSD_HEADS, 1), (2 * SSD_HEADS, LANES))
    segs_x = [("z", _COL_Z, SSD_INNER), ("xbc", _COL_XBC, CONV_DIM), ("dtT", 0, 0),
              ("q", _COL_Q, ATTN_WIDTH), ("k", _COL_K, KV_WIDTH), ("v", _COL_V, KV_WIDTH)]
    z_x, xbc_x, dt_x, pre_x, suf_x, q_x, k_x, v_x = _inproj(
        x.reshape(t, d), mod3, norm1[layer][None, :], w_cat, w_dtT, bias_rows, alog_rows, (cos, sin),
        rows=ROWS_IN, mod_row_fn=lambda i: i // seq_blocks, segs=segs_x, seq_blocks=seq_blocks)
    segs_c = [("xbc", _COL_XBC, CONV_DIM), ("dtT", 0, 0), ("k", _COL_K, KV_WIDTH), ("v", _COL_V, KV_WIDTH)]
    xbc_c, dt_c, pre_c, suf_c, k_c, v_c = _inproj(
        ctx.reshape(bt * n_ctx, d), mod3, norm1[layer][None, :], w_cat, w_dtT, bias_rows, alog_rows, None,
        rows=n_ctx, mod_row_fn=lambda i: bt, segs=segs_c, seq_blocks=1)

    cw = jnp.zeros((SUBLANES, CONV_DIM), F32).at[:CONV_WIDTH].set(conv_w[layer])
    cb = conv_b[layer][None, :]
    dskip_e = jnp.repeat(d_skip[layer], SSD_HEAD_DIM)[None, :]
    consts = (cw, cb, dskip_e, ssd_norm[layer][None, :], _head_expand_mat())

    zeros_state = jnp.zeros((bt, SSD_STATE, SSD_INNER), F32)
    z_dummy = jnp.zeros((bt, n_ctx, SSD_INNER), F32)
    h_cf, h_cb = _ssd(xbc_c.reshape(bt, n_ctx, CONV_DIM), (dt_c, pre_c, suf_c), z_dummy, consts,
                      zeros_state, zeros_state, rows=n_ctx, need_y=False)
    ssd_x, _, _ = _ssd(xbc_x.reshape(bt, seq, CONV_DIM), (dt_x, pre_x, suf_x), z_x.reshape(bt, seq, SSD_INNER),
                       consts, h_cf, h_cb, rows=ROWS_SSD, need_y=True)

    attn_x = _attention(attn_sinks[layer].astype(F32) * LOG2E,
                        q_x.reshape(bt, seq, ATTN_WIDTH), k_x.reshape(bt, seq, KV_WIDTH),
                        v_x.reshape(bt, seq, KV_WIDTH), k_c.reshape(bt, n_ctx, KV_WIDTH),
                        v_c.reshape(bt, n_ctx, KV_WIDTH))

    wo = w_out[layer]
    wa = wo[:SSD_INNER].astype(BF16)
    wb = wo[SSD_INNER:].reshape(ATTN_KV_HEADS, Q_PER_KV, HEAD_DIM, d).transpose(1, 0, 2, 3)
    wb = wb.reshape(ATTN_WIDTH, d).astype(BF16)
    wr = jnp.zeros((ROUTE_ROWS, d), F32).at[:N_GROUPS].set(w_group[layer].T)
    wr = wr.at[N_GROUPS:N_GROUPS + N_EXPERTS].set(w_expert[layer].T).astype(BF16)
    br = jnp.zeros((ROUTE_ROWS,), F32).at[:N_GROUPS].set(b_group[layer])
    br = br.at[N_GROUPS:N_GROUPS + N_EXPERTS].set(b_expert[layer])
    br = jnp.broadcast_to(br[:, None], (ROUTE_ROWS, ROWS_OUT))
    tri = jnp.asarray(np.triu(np.ones((ROWS_OUT, ROWS_OUT), np.float32), 1), BF16)
    x1, h2, ids, wcol, cnt = _outproj(
        ssd_x.reshape(t, SSD_INNER), attn_x.reshape(t, ATTN_WIDTH), x.reshape(t, d), mod3, norm2[layer][None, :],
        wa, wb, wr, br, tri, rows=ROWS_OUT, seq_blocks=seq // ROWS_OUT)

    counts = cnt[N_GROUPS:N_GROUPS + N_EXPERTS, 0].astype(jnp.int32)
    padded = (counts + MOE_BLK - 1) // MOE_BLK * MOE_BLK
    ends = jnp.cumsum(padded)
    pstart = ends - padded
    nblocks = (2 * t) // MOE_BLK + N_EXPERTS
    blk_start = jnp.arange(nblocks, dtype=jnp.int32) * MOE_BLK
    eids = jnp.arange(N_EXPERTS, dtype=jnp.int32)
    active = counts > 0
    last_active = jnp.max(jnp.where(active, eids, 0))
    blk_e = jnp.minimum(jnp.sum((ends[None, :] <= blk_start[:, None]).astype(jnp.int32), axis=1), last_active)
    nused = (ends[-1] // MOE_BLK).astype(jnp.int32).reshape(1)
    eord = jnp.cumsum(active.astype(jnp.int32)) - active.astype(jnp.int32)
    later = active[None, :] & (eids[None, :] > eids[:, None])
    enext = jnp.min(jnp.where(later, eids[None, :], N_EXPERTS), axis=1)
    enext = jnp.where(enext == N_EXPERTS, -1, enext).astype(jnp.int32)
    code = ids[0:2].reshape(-1)
    e_id = lax.shift_right_logical(code, RANK_BITS)
    onehot = e_id[:, None] == jnp.arange(N_EXPERTS, dtype=jnp.int32)[None, :]
    dst = jnp.sum(jnp.where(onehot, pstart[None, :], 0), axis=1) + (code & ((1 << RANK_BITS) - 1))

    y_buf = _experts(blk_e, nused, pstart, counts, eord, enext, dst, h2, w_gate[layer], w_up[layer],
                     w_down[layer], nblocks=nblocks, n_tok=t)
    out = _combine(dst, y_buf, x1, wcol, mod3, norm_final[None, :], rows=ROWS_CMB, seq_blocks=seq // ROWS_CMB)
    return dict(out=out.reshape(bt, seq, d), mod=mod, z=z_x, xbc=xbc_x, dt=dt_x, q=q_x, k=k_x, v=v_x,
                k_c=k_c, v_c=v_c, h_cf=h_cf, h_cb=h_cb, ssd=ssd_x, attn=attn_x, x1=x1, h2=h2, ids=ids,
                wcol=wcol, cnt=cnt, y_buf=y_buf, pstart=pstart, q_perm=q_perm)
```

```python
import functools
import math

import numpy as np
import jax
import jax.numpy as jnp
from jax import lax
from jax.experimental import pallas as pl
from jax.experimental.pallas import tpu as pltpu

F32 = jnp.float32
BF16 = jnp.bfloat16

D_MODEL = 1024
GRID_W = 64
EPS = 1e-6
SSD_INNER = 512
SSD_HEAD_DIM = 64
SSD_HEADS = 8
SSD_GROUPS = 2
SSD_STATE = 128
CONV_WIDTH = 5
CONV_DIM = SSD_INNER + 2 * SSD_GROUPS * SSD_STATE
CHUNK = 128
ATTN_WIDTH = 512
HEAD_DIM = 64
ATTN_Q_HEADS = 8
ATTN_KV_HEADS = 2
Q_PER_KV = 4
KV_WIDTH = 128
WINDOW = 128
ROPE_BASE = 10000.0
N_GROUPS = 4
EXPERTS_PER_GROUP = 8
N_EXPERTS = 32
EXPERT_DIM = 512

LANES = 128
SUBLANES = 8
NEG = -1e30
LOG2E = math.log2(math.e)

ROWS_IN = 1024
ROWS_OUT = 1024
ROWS_SSD = 1024
ROWS_ATT = 1024
ROUTE_ROWS = 48
MOE_BLK = 256
RANK_BITS = 16
D_TILES = D_MODEL // LANES
ROWS_CMB = 128


def _silu(v):
    half = 0.5 * v
    return half + half * jnp.tanh(half)


def _softplus(v):
    return jnp.maximum(v, 0.0) + jnp.log(1.0 + jnp.exp(-jnp.abs(v)))


def _ada_kernel(c_ref, w_ref, b_ref, o_ref):
    a = _silu(c_ref[...])
    o_ref[...] = jnp.dot(a.astype(BF16), w_ref[...].astype(BF16),
                         preferred_element_type=F32) + b_ref[...]


def _ada(cc, w_ada, b_ada):
    d, n = w_ada.shape
    tn = 1536
    return pl.pallas_call(
        _ada_kernel,
        grid=(n // tn,),
        in_specs=[pl.BlockSpec((SUBLANES, d), lambda j: (0, 0)),
                  pl.BlockSpec((d, tn), lambda j: (0, j)),
                  pl.BlockSpec((1, tn), lambda j: (0, j))],
        out_specs=pl.BlockSpec((SUBLANES, tn), lambda j: (0, j)),
        out_shape=jax.ShapeDtypeStruct((SUBLANES, n), F32),
        name="ada",
    )(cc, w_ada, b_ada)


def _rope(t, cos, sin_signed, first_half):
    rot = jnp.where(first_half, pltpu.roll(t, LANES - HEAD_DIM // 2, 1), pltpu.roll(t, HEAD_DIM // 2, 1))
    return t * cos + rot * sin_signed


def _lane_cumsum(v, lane, reverse):
    k = 1
    while k < LANES:
        if reverse:
            v = v + jnp.where(lane < LANES - k, pltpu.roll(v, LANES - k, 1), 0.0)
        else:
            v = v + jnp.where(lane >= k, pltpu.roll(v, k, 1), 0.0)
        k *= 2
    return v


def _inproj_kernel(*refs, segs, rope):
    x_ref, sh_ref, sc_ref, g_ref, w_ref, wdt_ref, bias_ref, alog_ref = refs[:8]
    pos = 8
    if rope:
        cos_ref, sin_ref = refs[8:10]
        pos = 10
    outs = iter(refs[pos:])
    x = x_ref[...]
    ms = jnp.mean(x * x, axis=-1, keepdims=True)
    h = (x * lax.rsqrt(ms + EPS)) * (g_ref[...] * (1.0 + sc_ref[0])) + sh_ref[0]
    hb = h.astype(BF16)
    if rope:
        cos = cos_ref[...]
        lane = lax.broadcasted_iota(jnp.int32, cos.shape, 1)
        first_half = (lane & (HEAD_DIM - 1)) < HEAD_DIM // 2
        sin_s = jnp.where(first_half, -sin_ref[...], sin_ref[...])
    for kind, c0, width in segs:
        if kind == "dtT":
            dt_ref, pre_ref, suf_ref = next(outs), next(outs), next(outs)
            dtt = lax.dot_general(wdt_ref[...], hb, (((1,), (1,)), ((), ())), preferred_element_type=F32)
            lane16 = lax.broadcasted_iota(jnp.int32, (2 * SSD_HEADS, CHUNK), 1)
            neg_a = -jnp.exp(alog_ref[...])
            for j in range(dt_ref.shape[0]):
                dt = _softplus(dtt[:, j * CHUNK:(j + 1) * CHUNK] + bias_ref[...])
                a = dt * neg_a
                dt_ref[j] = dt
                pre_ref[j] = _lane_cumsum(a, lane16, False)
                suf_ref[j] = _lane_cumsum(a, lane16, True)
            continue
        o_ref = next(outs)
        for j0 in range(0, width, 512):
            wj = min(512, width - j0)
            acc = jnp.dot(hb, w_ref[:, c0 + j0:c0 + j0 + wj], preferred_element_type=F32)
            if rope and kind in ("q", "k"):
                scale = HEAD_DIM ** -0.5 * LOG2E if kind == "q" else 1.0
                for l0 in range(0, wj, LANES):
                    t = _rope(acc[:, l0:l0 + LANES], cos, sin_s, first_half)
                    o_ref[:, j0 + l0:j0 + l0 + LANES] = (t * scale).astype(o_ref.dtype)
            else:
                o_ref[:, j0:j0 + wj] = acc.astype(o_ref.dtype)


_COL_Z, _COL_XBC, _COL_Q, _COL_K, _COL_V = 0, 512, 1536, 2048, 2176
_W_COLS = 2304


def _inproj(xf, mod3, norm1, w_cat, w_dtT, bias_rows, alog_rows, rope_tabs, *, rows, mod_row_fn, segs, seq_blocks):
    t, d = xf.shape
    rope = rope_tabs is not None
    in_specs = [
        pl.BlockSpec((rows, d), lambda i: (i, 0)),
        pl.BlockSpec((1, 1, d), lambda i: (mod_row_fn(i), 0, 0)),
        pl.BlockSpec((1, 1, d), lambda i: (mod_row_fn(i), 0, 1)),
        pl.BlockSpec((1, d), lambda i: (0, 0)),
        pl.BlockSpec(w_cat.shape, lambda i: (0, 0)),
        pl.BlockSpec(w_dtT.shape, lambda i: (0, 0)),
        pl.BlockSpec(bias_rows.shape, lambda i: (0, 0)),
        pl.BlockSpec(alog_rows.shape, lambda i: (0, 0)),
    ]
    args = [xf, mod3, mod3, norm1, w_cat, w_dtT, bias_rows, alog_rows]
    if rope:
        in_specs += [pl.BlockSpec((rows, LANES), lambda i: (i % seq_blocks, 0))] * 2
        args += list(rope_tabs)
    out_specs, out_shapes = [], []
    for kind, _, width in segs:
        if kind == "dtT":
            for _ in range(3):
                out_specs.append(pl.BlockSpec((rows // CHUNK, 2 * SSD_HEADS, CHUNK), lambda i: (i, 0, 0)))
                out_shapes.append(jax.ShapeDtypeStruct((t // CHUNK, 2 * SSD_HEADS, CHUNK), F32))
        else:
            dt = BF16 if kind in ("q", "k", "v") else F32
            out_specs.append(pl.BlockSpec((rows, width), lambda i: (i, 0)))
            out_shapes.append(jax.ShapeDtypeStruct((t, width), dt))
    return pl.pallas_call(
        functools.partial(_inproj_kernel, segs=tuple(segs), rope=rope),
        grid=(t // rows,),
        in_specs=in_specs,
        out_specs=out_specs,
        out_shape=out_shapes,
        name="inproj_rope" if rope else "inproj_ctx",
    )(*args)


_XB_COLS = SSD_INNER + SSD_GROUPS * SSD_STATE
_Y_OFF = CONV_DIM - _XB_COLS


def _ssd_kernel(*refs, need_y, nsteps, cps):
    (xbc_ref, prev_ref, next_ref, dt_ref, pre_ref, suf_ref, z_ref, cw_ref, cb_ref,
     dskip_ref, gain_ref, exp_ref, h0f_ref, h0b_ref) = refs[:14]
    if need_y:
        y_ref, hf_out, hb_out = refs[14:17]
        rest = refs[17:]
    else:
        y_ref = None
        hf_out, hb_out = refs[14:16]
        rest = refs[16:]
    cwin, u_ref, hf, hb, hbs, ust = rest

    sw = pl.program_id(1)
    s = pl.program_id(2)
    bi = jnp.where(sw == 0, nsteps - 1 - s, s)
    rows = xbc_ref.shape[1]
    h = SSD_HEADS

    @pl.when((sw == 0) & (s == 0))
    def _():
        hb[...] = h0b_ref[0]

    @pl.when((sw == 1) & (s == 0))
    def _():
        hf[...] = h0f_ref[0]

    ri = lax.broadcasted_iota(jnp.int32, (CHUNK, CHUNK), 0)
    ci = lax.broadcasted_iota(jnp.int32, (CHUNK, CHUNK), 1)
    lane128 = lax.broadcasted_iota(jnp.int32, (CHUNK, LANES), 1)
    lo_mask = lane128 < SSD_HEAD_DIM

    def conv_silu(c, c0, c1, out_ref):
        r0 = pl.multiple_of(c * CHUNK, CHUNK)
        above = xbc_ref[0, pl.ds(pl.multiple_of(jnp.maximum(r0 - SUBLANES, 0), SUBLANES), SUBLANES), c0:c1]
        halo_a = jnp.where(bi > 0, prev_ref[0, :, c0:c1], 0.0)
        cwin[0:SUBLANES, c0:c1] = jnp.where(c > 0, above, halo_a)
        cwin[SUBLANES:SUBLANES + CHUNK, c0:c1] = xbc_ref[0, pl.ds(r0, CHUNK), c0:c1]
        below = xbc_ref[0, pl.ds(pl.multiple_of(jnp.minimum(r0 + CHUNK, rows - SUBLANES), SUBLANES), SUBLANES), c0:c1]
        halo_b = jnp.where(bi < nsteps - 1, next_ref[0, :, c0:c1], 0.0)
        cwin[SUBLANES + CHUNK:, c0:c1] = jnp.where(c < cps - 1, below, halo_b)
        for j0 in range(c0, c1, LANES):
            acc = jnp.broadcast_to(cb_ref[:, j0:j0 + LANES], (CHUNK, LANES))
            for k in range(CONV_WIDTH):
                off = SUBLANES - CONV_WIDTH // 2 + k
                acc = acc + cwin[off:off + CHUNK, j0:j0 + LANES] * cw_ref[k:k + 1, j0:j0 + LANES]
            out_ref[:, j0 - c0:j0 - c0 + LANES] = _silu(acc)

    def dt_rows(c):
        return dt_ref[c], pre_ref[c], suf_ref[c]

    def state_update(state_ref, xb, w_rows, tot_col):
        dec = jnp.exp(jnp.sum(tot_col * exp_ref[...], axis=0, keepdims=True))
        for g in range(SSD_GROUPS):
            bm = xb[:, SSD_INNER + g * SSD_STATE:SSD_INNER + (g + 1) * SSD_STATE]
            bt = bm.T
            for pr in range(2):
                h0 = g * 4 + pr * 2
                c0 = h0 * SSD_HEAD_DIM
                xp = xb[:, c0:c0 + LANES]
                rhs = jnp.concatenate([jnp.where(lo_mask, xp, 0.0), jnp.where(lo_mask, 0.0, xp)],
                                      axis=0).astype(BF16)
                lhs = jnp.concatenate([bt * w_rows[h0:h0 + 1, :], bt * w_rows[h0 + 1:h0 + 2, :]],
                                      axis=1).astype(BF16)
                sres = jnp.dot(lhs, rhs, preferred_element_type=F32)
                state_ref[:, c0:c0 + LANES] = state_ref[:, c0:c0 + LANES] * dec[:, c0:c0 + LANES] + sres

    def col_forms(rows16):
        padded = jnp.concatenate([rows16, jnp.zeros((CHUNK - 2 * h, LANES), F32)], axis=0)
        return padded.T

    def backward_chunk(k, carry):
        c = cps - 1 - k
        gc = bi * cps + c
        xb = ust.at[gc]
        conv_silu(c, 0, _XB_COLS, xb)
        dt, _, suf = dt_rows(c)
        hbs[gc] = hb[...].astype(BF16)
        s0 = suf[:, 0:1]
        w_rows = jnp.exp(s0 - (suf - jnp.log(dt)))
        state_update(hb, xb, w_rows[h:2 * h, :], s0[h:2 * h, :])
        return carry

    def forward_chunk(c, carry):
        gc = bi * cps + c
        r0 = pl.multiple_of(c * CHUNK, CHUNK)
        xb = ust.at[gc]
        conv_silu(c, _XB_COLS, CONV_DIM, u_ref)
        dt, pre, suf = dt_rows(c)
        rowsf = jnp.concatenate([pre[0:h, :], suf[h:2 * h, :]], axis=0)
        colsf = col_forms(rowsf)
        ecol = jnp.exp(colsf)
        rowsd = rowsf - jnp.log(dt)
        for g in range(SSD_GROUPS):
            bm = xb[:, SSD_INNER + g * SSD_STATE:SSD_INNER + (g + 1) * SSD_STATE]
            cm = u_ref[:, g * SSD_STATE:(g + 1) * SSD_STATE]
            cmb = cm.astype(BF16)
            cbm = lax.dot_general(cmb, bm.astype(BF16), (((1,), (1,)), ((), ())),
                                  preferred_element_type=F32)
            for pr in range(2):
                h0 = g * 4 + pr * 2
                c0 = h0 * SSD_HEAD_DIM
                lhs_parts = []
                for hh in (h0, h0 + 1):
                    segf = colsf[:, hh:hh + 1] - rowsd[hh:hh + 1, :]
                    gf = jnp.exp(jnp.where(ci <= ri, segf, NEG))
                    segb = colsf[:, h + hh:h + hh + 1] - rowsd[h + hh:h + hh + 1, :]
                    gb = jnp.exp(jnp.where(ci >= ri, segb, NEG))
                    lhs_parts.append((cbm * (gf + gb)).astype(BF16))
                for hh in (h0, h0 + 1):
                    lhs_parts.append((cm * ecol[:, hh:hh + 1]).astype(BF16))
                for hh in (h0, h0 + 1):
                    lhs_parts.append((cm * ecol[:, h + hh:h + hh + 1]).astype(BF16))
                lhs = jnp.concatenate(lhs_parts, axis=1)
                xp = xb[:, c0:c0 + LANES]
                sf = hf[:, c0:c0 + LANES]
                sb = hbs[gc, :, c0:c0 + LANES].astype(F32)
                rhs = jnp.concatenate(
                    [jnp.where(lo_mask, xp, 0.0), jnp.where(lo_mask, 0.0, xp),
                     jnp.where(lo_mask, sf, 0.0), jnp.where(lo_mask, 0.0, sf),
                     jnp.where(lo_mask, sb, 0.0), jnp.where(lo_mask, 0.0, sb)], axis=0).astype(BF16)
                ypair = jnp.dot(lhs, rhs, preferred_element_type=F32)
                ypair = ypair + dskip_ref[:, c0:c0 + LANES] * xp
                zz = z_ref[0, pl.ds(r0, CHUNK), c0:c0 + LANES]
                u_ref[:, _Y_OFF + c0:_Y_OFF + c0 + LANES] = ypair * _silu(zz)
        last = pre[:, LANES - 1:LANES]
        w_rows = jnp.exp(last - rowsd)
        state_update(hf, xb, w_rows[0:h, :], last[0:h, :])
        if need_y:
            yv = u_ref[:, _Y_OFF:_Y_OFF + SSD_INNER]
            ms = jnp.mean(yv * yv, axis=-1, keepdims=True)
            y_ref[0, pl.ds(r0, CHUNK), :] = (yv * lax.rsqrt(ms + EPS) * gain_ref[...]).astype(y_ref.dtype)
        return carry

    @pl.when(sw == 0)
    def _():
        lax.fori_loop(0, cps, backward_chunk, 0, unroll=2)

    @pl.when(sw == 1)
    def _():
        lax.fori_loop(0, cps, forward_chunk, 0, unroll=2)

    @pl.when((sw == 0) & (s == nsteps - 1))
    def _():
        hb_out[0] = hb[...]

    @pl.when((sw == 1) & (s == nsteps - 1))
    def _():
        hf_out[0] = hf[...]


def _ssd(xbc, dts, z, consts, h0f, h0b, *, rows, need_y):
    bt, seq, _ = xbc.shape
    nsteps = seq // rows
    cps = rows // CHUNK
    hb8 = rows // SUBLANES
    nb8 = seq // SUBLANES
    cw, cb, dskip, gain, expm = consts
    dt_spec = pl.BlockSpec((cps, 2 * SSD_HEADS, CHUNK), lambda b, sw, s: (b * nsteps + blk(b, sw, s), 0, 0))

    def blk(b, sw, s):
        return jnp.where(sw == 0, nsteps - 1 - s, s)

    def full(a):
        return pl.BlockSpec(a.shape, lambda b, sw, s: (0,) * a.ndim)

    in_specs = [
        pl.BlockSpec((1, rows, CONV_DIM), lambda b, sw, s: (b, blk(b, sw, s), 0)),
        pl.BlockSpec((1, SUBLANES, CONV_DIM), lambda b, sw, s: (b, jnp.maximum(blk(b, sw, s) * hb8 - 1, 0), 0)),
        pl.BlockSpec((1, SUBLANES, CONV_DIM),
                     lambda b, sw, s: (b, jnp.minimum((blk(b, sw, s) + 1) * hb8, nb8 - 1), 0)),
        dt_spec, dt_spec, dt_spec,
        pl.BlockSpec((1, rows, SSD_INNER), lambda b, sw, s: (b, jnp.where(sw == 0, 0, s), 0)),
        full(cw), full(cb), full(dskip), full(gain), full(expm),
        pl.BlockSpec((1, SSD_STATE, SSD_INNER), lambda b, sw, s: (b, 0, 0)),
        pl.BlockSpec((1, SSD_STATE, SSD_INNER), lambda b, sw, s: (b, 0, 0)),
    ]
    st_spec = pl.BlockSpec((1, SSD_STATE, SSD_INNER), lambda b, sw, s: (b, 0, 0))
    st_shape = jax.ShapeDtypeStruct((bt, SSD_STATE, SSD_INNER), F32)
    out_specs, out_shapes = [st_spec, st_spec], [st_shape, st_shape]
    if need_y:
        out_specs = [pl.BlockSpec((1, rows, SSD_INNER), lambda b, sw, s: (b, jnp.where(sw == 0, 0, s), 0))] + out_specs
        out_shapes = [jax.ShapeDtypeStruct((bt, seq, SSD_INNER), BF16)] + out_shapes
    scratch = [
        pltpu.VMEM((CHUNK + 2 * SUBLANES, CONV_DIM), F32),
        pltpu.VMEM((CHUNK, _Y_OFF + SSD_INNER), F32),
        pltpu.VMEM((SSD_STATE, SSD_INNER), F32),
        pltpu.VMEM((SSD_STATE, SSD_INNER), F32),
        pltpu.VMEM((seq // CHUNK, SSD_STATE, SSD_INNER), BF16),
        pltpu.VMEM((seq // CHUNK, CHUNK, _XB_COLS), F32),
    ]
    return pl.pallas_call(
        functools.partial(_ssd_kernel, need_y=need_y, nsteps=nsteps, cps=cps),
        grid=(bt, 2, nsteps),
        in_specs=in_specs,
        out_specs=out_specs,
        out_shape=out_shapes,
        scratch_shapes=scratch,
        compiler_params=pltpu.CompilerParams(dimension_semantics=("arbitrary", "arbitrary", "arbitrary")),
        name="ssd_y" if need_y else "ssd_ctx",
    )(xbc, xbc, xbc, *dts, z, cw, cb, dskip, gain, expm, h0f, h0b)


def _attn_kernel(sink_ref, q_ref, kp_ref, kc_ref, kn_ref, vp_ref, vc_ref, vn_ref, kx_ref, vx_ref, o_ref, *, nsteps):
    m = pl.program_id(1)
    lane = lax.broadcasted_iota(jnp.int32, (1, LANES), 1)
    lo = lane < HEAD_DIM
    rq = Q_PER_KV * WINDOW
    per = ROWS_ATT // WINDOW
    qi = lax.broadcasted_iota(jnp.int32, (rq, WINDOW), 0) & (WINDOW - 1)
    kj = lax.broadcasted_iota(jnp.int32, (rq, WINDOW), 1)
    rblk = lax.shift_right_logical(lax.broadcasted_iota(jnp.int32, (rq, 1), 0), int(math.log2(WINDOW)))
    zero = jnp.zeros((), BF16)
    band_prev = jnp.where(kj >= qi, 0.0, NEG)
    band_next = jnp.where(kj <= qi, 0.0, NEG)

    kpieces = [kp_ref[0]] + [kc_ref[0, w * WINDOW:(w + 1) * WINDOW] for w in range(per)] + [kn_ref[0]]
    vpieces = [vp_ref[0]] + [vc_ref[0, w * WINDOW:(w + 1) * WINDOW] for w in range(per)] + [vn_ref[0]]
    kx = kx_ref[0]
    vx = vx_ref[0]
    nt = (((1,), (1,)), ((), ()))

    for sb in range(per):
        qsb = q_ref[0, sb * WINDOW:(sb + 1) * WINDOW, :]
        mask_prev = band_prev if sb > 0 else jnp.where(m > 0, band_prev, NEG)
        mask_next = band_next if sb < per - 1 else jnp.where(m < nsteps - 1, band_next, NEG)
        outs = [None] * Q_PER_KV
        for kv in range(ATTN_KV_HEADS):
            sel = lo if kv == 0 else jnp.logical_not(lo)
            qs = jnp.concatenate(
                [jnp.where(sel, qsb[:, j * LANES:(j + 1) * LANES], zero) for j in range(Q_PER_KV)], axis=0)
            s_prev = lax.dot_general(qs, kpieces[sb], nt, preferred_element_type=F32) + mask_prev
            s_cur = lax.dot_general(qs, kpieces[sb + 1], nt, preferred_element_type=F32)
            s_next = lax.dot_general(qs, kpieces[sb + 2], nt, preferred_element_type=F32) + mask_next
            s_ctx = lax.dot_general(qs, kx, nt, preferred_element_type=F32)
            sink = jnp.zeros((rq, 1), F32)
            for j in range(Q_PER_KV):
                sink = jnp.where(rblk == j, sink_ref[kv * Q_PER_KV + j], sink)
            n_ct = s_ctx.shape[1] // LANES
            tile_max = jnp.maximum(jnp.maximum(s_prev, s_cur), s_next)
            for t in range(n_ct):
                tile_max = jnp.maximum(tile_max, s_ctx[:, t * LANES:(t + 1) * LANES])
            mx = jnp.maximum(jnp.max(tile_max, axis=-1, keepdims=True), sink)
            p_prev = jnp.exp2(s_prev - mx)
            p_cur = jnp.exp2(s_cur - mx)
            p_next = jnp.exp2(s_next - mx)
            p_ctx = jnp.exp2(s_ctx - mx)
            tile_sum = p_prev + p_cur + p_next
            for t in range(n_ct):
                tile_sum = tile_sum + p_ctx[:, t * LANES:(t + 1) * LANES]
            den = jnp.sum(tile_sum, axis=-1, keepdims=True) + jnp.exp2(sink - mx)
            acc = jnp.dot(p_prev.astype(BF16), jnp.where(sel, vpieces[sb], zero), preferred_element_type=F32)
            acc += jnp.dot(p_cur.astype(BF16), jnp.where(sel, vpieces[sb + 1], zero), preferred_element_type=F32)
            acc += jnp.dot(p_next.astype(BF16), jnp.where(sel, vpieces[sb + 2], zero), preferred_element_type=F32)
            acc += jnp.dot(p_ctx.astype(BF16), jnp.where(sel, vx, zero), preferred_element_type=F32)
            acc = acc * (1.0 / den)
            for j in range(Q_PER_KV):
                part = acc[j * WINDOW:(j + 1) * WINDOW, :]
                outs[j] = part if outs[j] is None else outs[j] + part
        for j in range(Q_PER_KV):
            o_ref[0, sb * WINDOW:(sb + 1) * WINDOW, j * LANES:(j + 1) * LANES] = outs[j].astype(o_ref.dtype)


def _attention(sinks, q, k, v, kx, vx):
    bt, seq, _ = q.shape
    nsteps = seq // ROWS_ATT
    per = ROWS_ATT // WINDOW
    nb = seq // WINDOW
    n_ctx = kx.shape[1]

    def prev(b, m, s):
        return (b, jnp.maximum(m * per - 1, 0), 0)

    def cur(b, m, s):
        return (b, m, 0)

    def nxt(b, m, s):
        return (b, jnp.minimum((m + 1) * per, nb - 1), 0)

    small = (1, WINDOW, KV_WIDTH)
    big = (1, ROWS_ATT, KV_WIDTH)
    grid_spec = pltpu.PrefetchScalarGridSpec(
        num_scalar_prefetch=1,
        grid=(bt, nsteps),
        in_specs=[
            pl.BlockSpec((1, ROWS_ATT, ATTN_WIDTH), cur),
            pl.BlockSpec(small, prev), pl.BlockSpec(big, cur), pl.BlockSpec(small, nxt),
            pl.BlockSpec(small, prev), pl.BlockSpec(big, cur), pl.BlockSpec(small, nxt),
            pl.BlockSpec((1, n_ctx, KV_WIDTH), lambda b, m, s: (b, 0, 0)),
            pl.BlockSpec((1, n_ctx, KV_WIDTH), lambda b, m, s: (b, 0, 0)),
        ],
        out_specs=pl.BlockSpec((1, ROWS_ATT, ATTN_WIDTH), cur),
    )
    return pl.pallas_call(
        functools.partial(_attn_kernel, nsteps=nsteps),
        grid_spec=grid_spec,
        out_shape=jax.ShapeDtypeStruct((bt, seq, ATTN_WIDTH), BF16),
        name="attn",
    )(sinks, q, k, k, k, v, v, v, kx, vx)


def _outproj_kernel(ssd_ref, att_ref, x_ref, g1_ref, sh_ref, sc_ref, n2_ref, wa_ref, wb_ref, wr_ref, br_ref,
                    tri_ref, x1_ref, h2_ref, ids_ref, wcol_ref, cnt_ref, carry, *, nsteps):
    i = pl.program_id(0)
    rows = x_ref.shape[0]

    @pl.when(i == 0)
    def _():
        carry[...] = jnp.zeros_like(carry)

    acc = jnp.dot(ssd_ref[...], wa_ref[...], preferred_element_type=F32)
    acc += jnp.dot(att_ref[...], wb_ref[...], preferred_element_type=F32)
    x1 = x_ref[...] + g1_ref[0] * acc
    x1_ref[...] = x1
    ms = jnp.mean(x1 * x1, axis=-1, keepdims=True)
    h2 = (x1 * lax.rsqrt(ms + EPS)) * (n2_ref[...] * (1.0 + sc_ref[0])) + sh_ref[0]
    h2b = h2.astype(BF16)
    half = h2.shape[1] // 2
    lo_bits = lax.shift_right_logical(pltpu.bitcast(h2b[:, :half].astype(F32), jnp.uint32), jnp.uint32(16))
    hi_bits = pltpu.bitcast(h2b[:, half:].astype(F32), jnp.uint32) & jnp.uint32(0xFFFF0000)
    h2_ref[...] = hi_bits | lo_bits
    lt = lax.dot_general(wr_ref[...], h2b, (((1,), (1,)), ((), ())),
                         preferred_element_type=F32) + br_ref[...]
    row = lax.broadcasted_iota(jnp.int32, lt.shape, 0).astype(F32)
    big_i = float(ROUTE_ROWS)
    gl = jnp.where(row < N_GROUPS, lt, NEG)
    gmax = jnp.max(gl, axis=0, keepdims=True)
    gidx = jnp.min(jnp.where(gl == gmax, row, big_i), axis=0, keepdims=True)
    g_w = 1.0 / jnp.sum(jnp.exp(gl - gmax), axis=0, keepdims=True)
    lo = N_GROUPS + EXPERTS_PER_GROUP * gidx
    el = jnp.where((row >= lo) & (row < lo + EXPERTS_PER_GROUP), lt, NEG)
    m1 = jnp.max(el, axis=0, keepdims=True)
    i1 = jnp.min(jnp.where(el == m1, row, big_i), axis=0, keepdims=True)
    el2 = jnp.where(row == i1, NEG, el)
    m2 = jnp.max(el2, axis=0, keepdims=True)
    i2 = jnp.min(jnp.where(el2 == m2, row, big_i), axis=0, keepdims=True)
    r = jnp.exp(m2 - m1)
    w1 = g_w / (1.0 + r)
    w2 = g_w * r / (1.0 + r)
    oh1 = row == i1
    oh2 = row == i2
    cnt = jnp.where(oh1 | oh2, 1.0, 0.0)
    prefix = jnp.dot(cnt.astype(BF16), tri_ref[...], preferred_element_type=F32)
    base = carry[...] + prefix
    rank1 = jnp.sum(jnp.where(oh1, base, 0.0), axis=0, keepdims=True)
    rank2 = jnp.sum(jnp.where(oh2, base, 0.0), axis=0, keepdims=True)
    carry[...] = carry[...] + jnp.sum(cnt, axis=1, keepdims=True)
    r8 = lax.broadcasted_iota(jnp.int32, (SUBLANES, rows), 0)
    code1 = (i1 - N_GROUPS) * float(1 << RANK_BITS) + rank1
    code2 = (i2 - N_GROUPS) * float(1 << RANK_BITS) + rank2
    ids_ref[...] = jnp.where(r8 == 0, code1, jnp.where(r8 == 1, code2, 0.0)).astype(jnp.int32)
    r128 = lax.broadcasted_iota(jnp.int32, (LANES, rows), 0)
    wfull = jnp.where(r128 == 0, w1, jnp.where(r128 == 1, w2, 0.0))
    for j in range(rows // LANES):
        wcol_ref[j * LANES:(j + 1) * LANES, :] = wfull[:, j * LANES:(j + 1) * LANES].T

    @pl.when(i == nsteps - 1)
    def _():
        cnt_ref[...] = carry[:, 0:LANES]


def _outproj(ssd, att, xf, mod3, norm2, wa, wb, wr, br, tri, *, rows, seq_blocks):
    t, d = xf.shape
    nsteps = t // rows

    def mod(col):
        return pl.BlockSpec((1, 1, d), lambda i: (i // seq_blocks, 0, col))

    def full(a):
        return pl.BlockSpec(a.shape, lambda i: (0,) * a.ndim)

    return pl.pallas_call(
        functools.partial(_outproj_kernel, nsteps=nsteps),
        grid=(nsteps,),
        in_specs=[pl.BlockSpec((rows, SSD_INNER), lambda i: (i, 0)),
                  pl.BlockSpec((rows, ATTN_WIDTH), lambda i: (i, 0)),
                  pl.BlockSpec((rows, d), lambda i: (i, 0)),
                  mod(2), mod(3), mod(4), full(norm2), full(wa), full(wb), full(wr), full(br), full(tri)],
        out_specs=[pl.BlockSpec((rows, d), lambda i: (i, 0)),
                   pl.BlockSpec((rows, d // 2), lambda i: (i, 0)),
                   pl.BlockSpec((SUBLANES, rows), lambda i: (0, i)),
                   pl.BlockSpec((rows, LANES), lambda i: (i, 0)),
                   pl.BlockSpec((ROUTE_ROWS, LANES), lambda i: (0, 0))],
        out_shape=[jax.ShapeDtypeStruct((t, d), F32),
                   jax.ShapeDtypeStruct((t, d // 2), jnp.uint32),
                   jax.ShapeDtypeStruct((SUBLANES, t), jnp.int32),
                   jax.ShapeDtypeStruct((t, LANES), F32),
                   jax.ShapeDtypeStruct((ROUTE_ROWS, LANES), F32)],
        scratch_shapes=[pltpu.VMEM((ROUTE_ROWS, rows), F32)],
        compiler_params=pltpu.CompilerParams(dimension_semantics=("arbitrary",)),
        name="outproj_router",
    )(ssd, att, xf, mod3, mod3, mod3, norm2, wa, wb, wr, br, tri)


def _expert_kernel(blk_e_ref, nused_ref, pstart_ref, counts_ref, eord_ref, enext_ref, dst_ref,
                   h2_hbm, wg_hbm, wu_hbm, wd_hbm, y_hbm,
                   h2v, xg, ystage, wgs, wus, wds, wgb, wub, wdb, tok_ref, sem, wsem, ysem, *, n_tok, nblocks):
    nused = nused_ref[0]
    half = h2v.shape[1]

    def h2_copy():
        return pltpu.make_async_copy(h2_hbm, h2v, sem.at[0])

    def y_copy(blk, sl):
        row0 = blk * MOE_BLK if isinstance(blk, int) else pl.multiple_of(blk * MOE_BLK, MOE_BLK)
        return pltpu.make_async_copy(ystage.at[sl], y_hbm.at[pl.ds(row0, MOE_BLK)], ysem.at[sl])

    def weight_copies(e, sl):
        return (pltpu.make_async_copy(wg_hbm.at[e], wgs.at[sl], wsem.at[sl, 0]),
                pltpu.make_async_copy(wu_hbm.at[e], wus.at[sl], wsem.at[sl, 1]),
                pltpu.make_async_copy(wd_hbm.at[e], wds.at[sl], wsem.at[sl, 2]))

    def gather_rows(blk, sl):
        for r in range(MOE_BLK):
            xg[sl, pl.ds(r, 1), :] = h2v[pl.ds(tok_ref[blk * MOE_BLK + r], 1), :]

    h2_copy().start()
    for cp in weight_copies(blk_e_ref[0], 0):
        cp.start()

    def pad_expert(e, carry):
        lo = pstart_ref[e] + counts_ref[e]
        hi = pstart_ref[e] + (counts_ref[e] + MOE_BLK - 1) // MOE_BLK * MOE_BLK

        def pad_slot(s, c):
            tok_ref[s] = 0
            return c
        return lax.fori_loop(lo, hi, pad_slot, carry)
    lax.fori_loop(0, N_EXPERTS, pad_expert, 0)

    def claim(t, carry):
        tok_ref[dst_ref[t]] = t
        tok_ref[dst_ref[n_tok + t]] = t
        return carry
    lax.fori_loop(0, n_tok, claim, 0, unroll=16)
    h2_copy().wait()
    gather_rows(0, 0)

    def block(b, carry):
        slot = b % 2
        e = blk_e_ref[b]
        e_prev = blk_e_ref[jnp.maximum(b - 1, 0)]

        @pl.when(b >= 2)
        def _():
            y_copy(0, slot).wait()

        @pl.when((b == 0) | (e != e_prev))
        def _():
            sl = eord_ref[e] % 2
            for cp in weight_copies(e, sl):
                cp.wait()
            wgb[...] = wgs[sl].astype(BF16)
            wub[...] = wus[sl].astype(BF16)
            wdb[...] = wds[sl].astype(BF16)
            e_next = enext_ref[e]

            @pl.when(e_next >= 0)
            def _():
                for cp in weight_copies(e_next, 1 - sl):
                    cp.start()

        words = xg[slot]
        x_lo = pltpu.bitcast(words << jnp.uint32(16), F32).astype(BF16)
        x_hi = pltpu.bitcast(words & jnp.uint32(0xFFFF0000), F32).astype(BF16)
        gate = (jnp.dot(x_lo, wgb[0:half, :], preferred_element_type=F32)
                + jnp.dot(x_hi, wgb[half:, :], preferred_element_type=F32))
        up = (jnp.dot(x_lo, wub[0:half, :], preferred_element_type=F32)
              + jnp.dot(x_hi, wub[half:, :], preferred_element_type=F32))
        hmid = (_silu(gate) * up).astype(BF16)
        ystage[slot] = jnp.dot(hmid, wdb[...], preferred_element_type=F32)
        gather_rows(jnp.minimum(b + 1, nused - 1), 1 - slot)
        y_copy(b, slot).start()
        return carry
    lax.fori_loop(0, nused, block, 0)

    @pl.when(nused >= 2)
    def _():
        y_copy(0, nused % 2).wait()

    y_copy(0, (nused - 1) % 2).wait()
    ystage[0] = jnp.zeros(ystage.shape[1:], F32)

    def zero_block(b, carry):
        cp = y_copy(b, 0)
        cp.start()
        cp.wait()
        return carry
    lax.fori_loop(nused, nblocks, zero_block, 0)


def _experts(blk_e, nused, pstart, counts, eord, enext, dst, h2p, w_gate, w_up, w_down, *, nblocks, n_tok):
    d = w_gate.shape[1]
    hbm = pl.BlockSpec(memory_space=pl.ANY)
    grid_spec = pltpu.PrefetchScalarGridSpec(
        num_scalar_prefetch=7,
        grid=(1,),
        in_specs=[hbm, hbm, hbm, hbm],
        out_specs=hbm,
        scratch_shapes=[
            pltpu.VMEM(h2p.shape, jnp.uint32),
            pltpu.VMEM((2, MOE_BLK, d // 2), jnp.uint32),
            pltpu.VMEM((2, MOE_BLK, d), F32),
            pltpu.VMEM((2, d, EXPERT_DIM), F32),
            pltpu.VMEM((2, d, EXPERT_DIM), F32),
            pltpu.VMEM((2, EXPERT_DIM, d), F32),
            pltpu.VMEM((d, EXPERT_DIM), BF16),
            pltpu.VMEM((d, EXPERT_DIM), BF16),
            pltpu.VMEM((EXPERT_DIM, d), BF16),
            pltpu.SMEM((nblocks * MOE_BLK,), jnp.int32),
            pltpu.SemaphoreType.DMA((1,)),
            pltpu.SemaphoreType.DMA((2, 3)),
            pltpu.SemaphoreType.DMA((2,)),
        ],
    )
    return pl.pallas_call(
        functools.partial(_expert_kernel, n_tok=n_tok, nblocks=nblocks),
        grid_spec=grid_spec,
        out_shape=jax.ShapeDtypeStruct((nblocks * MOE_BLK, d), F32),
        compiler_params=pltpu.CompilerParams(dimension_semantics=("arbitrary",)),
        name="experts",
    )(blk_e, nused, pstart, counts, eord, enext, dst, h2p, w_gate, w_up, w_down)


def _combine_kernel(dst_ref, y_hbm, x1_ref, wcol_ref, g2_ref, nf_ref, o_ref, ybuf, sem, *, nsteps, n_tok):
    i = pl.program_id(0)
    rows = x1_ref.shape[0]
    slot = i % 2

    def row_copy(src, r, k, sl):
        return pltpu.make_async_copy(y_hbm.at[pl.ds(src, 1)], ybuf.at[sl, k, pl.ds(r, 1)], sem.at[sl])

    def start_rows(step, sl):
        for r in range(rows):
            for k in range(2):
                row_copy(dst_ref[k * n_tok + step * rows + r], r, k, sl).start()

    def wait_rows(sl):
        for r in range(rows):
            for k in range(2):
                row_copy(0, r, k, sl).wait()

    @pl.when(i == 0)
    def _():
        start_rows(0, 0)

    @pl.when(i + 1 < nsteps)
    def _():
        start_rows(i + 1, 1 - slot)

    wait_rows(slot)
    moe = ybuf[slot, 0] * wcol_ref[:, 0:1] + ybuf[slot, 1] * wcol_ref[:, 1:2]
    x2 = x1_ref[...] + g2_ref[0] * moe
    ms = jnp.mean(x2 * x2, axis=-1, keepdims=True)
    o_ref[...] = x2 * lax.rsqrt(ms + EPS) * nf_ref[...]


def _combine(dst, y_buf, x1, wcol, mod3, norm_final, *, rows, seq_blocks):
    t, d = x1.shape
    nsteps = t // rows
    grid_spec = pltpu.PrefetchScalarGridSpec(
        num_scalar_prefetch=1,
        grid=(nsteps,),
        in_specs=[
            pl.BlockSpec(memory_space=pl.ANY),
            pl.BlockSpec((rows, d), lambda i, ds: (i, 0)),
            pl.BlockSpec((rows, LANES), lambda i, ds: (i, 0)),
            pl.BlockSpec((1, 1, d), lambda i, ds: (i // seq_blocks, 0, 5)),
            pl.BlockSpec((1, d), lambda i, ds: (0, 0)),
        ],
        out_specs=pl.BlockSpec((rows, d), lambda i, ds: (i, 0)),
        scratch_shapes=[pltpu.VMEM((2, 2, rows, d), F32), pltpu.SemaphoreType.DMA((2,))],
    )
    return pl.pallas_call(
        functools.partial(_combine_kernel, nsteps=nsteps, n_tok=t),
        grid_spec=grid_spec,
        out_shape=jax.ShapeDtypeStruct((t, d), F32),
        compiler_params=pltpu.CompilerParams(dimension_semantics=("arbitrary",)),
        name="combine",
    )(dst, y_buf, x1, wcol, mod3, norm_final)


def _rope_tables(seq):
    pos = np.arange(seq)
    n_freq = HEAD_DIM // 4
    inv = ROPE_BASE ** (-np.arange(n_freq, dtype=np.float32) / n_freq)
    ang = np.concatenate([(pos // GRID_W)[:, None] * inv, (pos % GRID_W)[:, None] * inv], axis=-1)
    ang = np.concatenate([ang, ang, ang, ang], axis=-1).astype(np.float32)
    return jnp.asarray(np.cos(ang), F32), jnp.asarray(np.sin(ang), F32)


def _head_expand_mat():
    m = np.zeros((SSD_HEADS, SSD_INNER), np.float32)
    for hh in range(SSD_HEADS):
        m[hh, hh * SSD_HEAD_DIM:(hh + 1) * SSD_HEAD_DIM] = 1.0
    return jnp.asarray(m)


def kernel(x, c, ctx, c_ctx, w_ada, b_ada, norm1, w_in, conv_w, conv_b, dt_bias, a_log, d_skip, ssd_norm,
           attn_sinks, w_out, norm2, w_group, b_group, w_expert, b_expert, w_gate, w_up, w_down, norm_final):
    return _pipeline(x, c, ctx, c_ctx, w_ada, b_ada, norm1, w_in, conv_w, conv_b, dt_bias, a_log, d_skip,
                     ssd_norm, attn_sinks, w_out, norm2, w_group, b_group, w_expert, b_expert, w_gate, w_up,
                     w_down, norm_final)["out"]


def _pipeline(x, c, ctx, c_ctx, w_ada, b_ada, norm1, w_in, conv_w, conv_b, dt_bias, a_log, d_skip, ssd_norm,
              attn_sinks, w_out, norm2, w_group, b_group, w_expert, b_expert, w_gate, w_up, w_down, norm_final):
    bt, seq, d = x.shape
    n_ctx = ctx.shape[1]
    t = bt * seq
    layer = 0
    assert w_ada.shape[0] == 1 and seq % ROWS_IN == 0 and seq % ROWS_OUT == 0 and n_ctx % CHUNK == 0

    cc = jnp.concatenate([c, c_ctx[None, :], jnp.zeros((SUBLANES - bt - 1, d), F32)], axis=0)
    mod = _ada(cc, w_ada[layer], b_ada[layer][None, :])
    mod3 = mod.reshape(SUBLANES, 1, 6 * d)

    w = w_in[layer]
    o_z, o_xbc, o_dt = 0, SSD_INNER, SSD_INNER + CONV_DIM
    o_q = o_dt + 2 * SSD_HEADS
    o_k, o_v = o_q + ATTN_WIDTH, o_q + ATTN_WIDTH + KV_WIDTH
    q_perm = np.concatenate([np.r_[j * HEAD_DIM:(j + 1) * HEAD_DIM, (j + 4) * HEAD_DIM:(j + 5) * HEAD_DIM]
                             for j in range(Q_PER_KV)])
    w_q = w[:, o_q:o_q + ATTN_WIDTH].reshape(d, ATTN_KV_HEADS, Q_PER_KV, HEAD_DIM).transpose(0, 2, 1, 3)
    w_q = w_q.reshape(d, ATTN_WIDTH)
    w_cat = jnp.concatenate([w[:, o_z:o_z + SSD_INNER], w[:, o_xbc:o_xbc + CONV_DIM], w_q,
                             w[:, o_k:o_k + KV_WIDTH], w[:, o_v:o_v + KV_WIDTH]], axis=1).astype(BF16)
    w_dtT = w[:, o_dt:o_dt + 2 * SSD_HEADS].T.astype(BF16)

    seq_blocks = seq // ROWS_IN
    cos, sin = _rope_tables(seq)
    bias_rows = jnp.broadcast_to(dt_bias[layer].reshape(2 * SSD_HEADS, 1), (2 * SSD_HEADS, LANES))
    alog_rows = jnp.broadcast_to(a_log[layer].reshape(2 * SSD_HEADS, 1), (2 * SSD_HEADS, LANES))
    segs_x = [("z", _COL_Z, SSD_INNER), ("xbc", _COL_XBC, CONV_DIM), ("dtT", 0, 0),
              ("q", _COL_Q, ATTN_WIDTH), ("k", _COL_K, KV_WIDTH), ("v", _COL_V, KV_WIDTH)]
    z_x, xbc_x, dt_x, pre_x, suf_x, q_x, k_x, v_x = _inproj(
        x.reshape(t, d), mod3, norm1[layer][None, :], w_cat, w_dtT, bias_rows, alog_rows, (cos, sin),
        rows=ROWS_IN, mod_row_fn=lambda i: i // seq_blocks, segs=segs_x, seq_blocks=seq_blocks)
    segs_c = [("xbc", _COL_XBC, CONV_DIM), ("dtT", 0, 0), ("k", _COL_K, KV_WIDTH), ("v", _COL_V, KV_WIDTH)]
    xbc_c, dt_c, pre_c, suf_c, k_c, v_c = _inproj(
        ctx.reshape(bt * n_ctx, d), mod3, norm1[layer][None, :], w_cat, w_dtT, bias_rows, alog_rows, None,
        rows=n_ctx, mod_row_fn=lambda i: bt, segs=segs_c, seq_blocks=1)

    cw = jnp.zeros((SUBLANES, CONV_DIM), F32).at[:CONV_WIDTH].set(conv_w[layer])
    cb = conv_b[layer][None, :]
    dskip_e = jnp.repeat(d_skip[layer], SSD_HEAD_DIM)[None, :]
    consts = (cw, cb, dskip_e, ssd_norm[layer][None, :], _head_expand_mat())

    zeros_state = jnp.zeros((bt, SSD_STATE, SSD_INNER), F32)
    z_dummy = jnp.zeros((bt, n_ctx, SSD_INNER), F32)
    h_cf, h_cb = _ssd(xbc_c.reshape(bt, n_ctx, CONV_DIM), (dt_c, pre_c, suf_c), z_dummy, consts,
                      zeros_state, zeros_state, rows=n_ctx, need_y=False)
    ssd_x, _, _ = _ssd(xbc_x.reshape(bt, seq, CONV_DIM), (dt_x, pre_x, suf_x), z_x.reshape(bt, seq, SSD_INNER),
                       consts, h_cf, h_cb, rows=ROWS_SSD, need_y=True)

    attn_x = _attention(attn_sinks[layer].astype(F32) * LOG2E,
                        q_x.reshape(bt, seq, ATTN_WIDTH), k_x.reshape(bt, seq, KV_WIDTH),
                        v_x.reshape(bt, seq, KV_WIDTH), k_c.reshape(bt, n_ctx, KV_WIDTH),
                        v_c.reshape(bt, n_ctx, KV_WIDTH))

    wo = w_out[layer]
    wa = wo[:SSD_INNER].astype(BF16)
    wb = wo[SSD_INNER:].reshape(ATTN_KV_HEADS, Q_PER_KV, HEAD_DIM, d).transpose(1, 0, 2, 3)
    wb = wb.reshape(ATTN_WIDTH, d).astype(BF16)
    wr = jnp.zeros((ROUTE_ROWS, d), F32).at[:N_GROUPS].set(w_group[layer].T)
    wr = wr.at[N_GROUPS:N_GROUPS + N_EXPERTS].set(w_expert[layer].T).astype(BF16)
    br = jnp.zeros((ROUTE_ROWS,), F32).at[:N_GROUPS].set(b_group[layer])
    br = br.at[N_GROUPS:N_GROUPS + N_EXPERTS].set(b_expert[layer])
    br = jnp.broadcast_to(br[:, None], (ROUTE_ROWS, ROWS_OUT))
    tri = jnp.asarray(np.triu(np.ones((ROWS_OUT, ROWS_OUT), np.float32), 1), BF16)
    x1, h2, ids, wcol, cnt = _outproj(
        ssd_x.reshape(t, SSD_INNER), attn_x.reshape(t, ATTN_WIDTH), x.reshape(t, d), mod3, norm2[layer][None, :],
        wa, wb, wr, br, tri, rows=ROWS_OUT, seq_blocks=seq // ROWS_OUT)

    counts = cnt[N_GROUPS:N_GROUPS + N_EXPERTS, 0].astype(jnp.int32)
    padded = (counts + MOE_BLK - 1) // MOE_BLK * MOE_BLK
    ends = jnp.cumsum(padded)
    pstart = ends - padded
    nblocks = (2 * t) // MOE_BLK + N_EXPERTS
    blk_start = jnp.arange(nblocks, dtype=jnp.int32) * MOE_BLK
    eids = jnp.arange(N_EXPERTS, dtype=jnp.int32)
    active = counts > 0
    last_active = jnp.max(jnp.where(active, eids, 0))
    blk_e = jnp.minimum(jnp.sum((ends[None, :] <= blk_start[:, None]).astype(jnp.int32), axis=1), last_active)
    nused = (ends[-1] // MOE_BLK).astype(jnp.int32).reshape(1)
    eord = jnp.cumsum(active.astype(jnp.int32)) - active.astype(jnp.int32)
    later = active[None, :] & (eids[None, :] > eids[:, None])
    enext = jnp.min(jnp.where(later, eids[None, :], N_EXPERTS), axis=1)
    enext = jnp.where(enext == N_EXPERTS, -1, enext).astype(jnp.int32)
    code = ids[0:2].reshape(-1)
    e_id = lax.shift_right_logical(code, RANK_BITS)
    onehot = e_id[:, None] == jnp.arange(N_EXPERTS, dtype=jnp.int32)[None, :]
    dst = jnp.sum(jnp.where(onehot, pstart[None, :], 0), axis=1) + (code & ((1 << RANK_BITS) - 1))

    y_buf = _experts(blk_e, nused, pstart, counts, eord, enext, dst, h2, w_gate[layer], w_up[layer],
                     w_down[layer], nblocks=nblocks, n_tok=t)
    out = _combine(dst, y_buf, x1, wcol, mod3, norm_final[None, :], rows=ROWS_CMB, seq_blocks=seq // ROWS_CMB)
    return dict(out=out.reshape(bt, seq, d), mod=mod, z=z_x, xbc=xbc_x, dt=dt_x, q=q_x, k=k_x, v=v_x,
                k_c=k_c, v_c=v_c, h_cf=h_cf, h_cb=h_cb, ssd=ssd_x, attn=attn_x, x1=x1, h2=h2, ids=ids,
                wcol=wcol, cnt=cnt, y_buf=y_buf, pstart=pstart, q_perm=q_perm)
```

```python
import functools
import math

import numpy as np
import jax
import jax.numpy as jnp
from jax import lax
from jax.experimental import pallas as pl
from jax.experimental.pallas import tpu as pltpu

F32 = jnp.float32
BF16 = jnp.bfloat16

D_MODEL = 1024
GRID_W = 64
EPS = 1e-6
SSD_INNER = 512
SSD_HEAD_DIM = 64
SSD_HEADS = 8
SSD_GROUPS = 2
SSD_STATE = 128
CONV_WIDTH = 5
CONV_DIM = SSD_INNER + 2 * SSD_GROUPS * SSD_STATE
CHUNK = 128
ATTN_WIDTH = 512
HEAD_DIM = 64
ATTN_Q_HEADS = 8
ATTN_KV_HEADS = 2
Q_PER_KV = 4
KV_WIDTH = 128
WINDOW = 128
ROPE_BASE = 10000.0
N_GROUPS = 4
EXPERTS_PER_GROUP = 8
N_EXPERTS = 32
EXPERT_DIM = 512

LANES = 128
SUBLANES = 8
NEG = -1e30
LOG2E = math.log2(math.e)

ROWS_IN = 1024
ROWS_OUT = 1024
ROWS_SSD = 1024
ROWS_ATT = 1024
ROUTE_ROWS = 48
MOE_BLK = 256
RANK_BITS = 16
D_TILES = D_MODEL // LANES
ROWS_CMB = 128


def _silu(v):
    half = 0.5 * v
    return half + half * jnp.tanh(half)


def _softplus(v):
    return jnp.maximum(v, 0.0) + jnp.log(1.0 + jnp.exp(-jnp.abs(v)))


def _ada_kernel(c_ref, w_ref, b_ref, o_ref):
    a = _silu(c_ref[...])
    o_ref[...] = jnp.dot(a.astype(BF16), w_ref[...].astype(BF16),
                         preferred_element_type=F32) + b_ref[...]


def _ada(cc, w_ada, b_ada):
    d, n = w_ada.shape
    tn = 1536
    return pl.pallas_call(
        _ada_kernel,
        grid=(n // tn,),
        in_specs=[pl.BlockSpec((SUBLANES, d), lambda j: (0, 0)),
                  pl.BlockSpec((d, tn), lambda j: (0, j)),
                  pl.BlockSpec((1, tn), lambda j: (0, j))],
        out_specs=pl.BlockSpec((SUBLANES, tn), lambda j: (0, j)),
        out_shape=jax.ShapeDtypeStruct((SUBLANES, n), F32),
        name="ada",
    )(cc, w_ada, b_ada)


def _rope(t, cos, sin_signed, first_half):
    rot = jnp.where(first_half, pltpu.roll(t, LANES - HEAD_DIM // 2, 1), pltpu.roll(t, HEAD_DIM // 2, 1))
    return t * cos + rot * sin_signed


def _lane_cumsum(v, lane, reverse):
    k = 1
    while k < LANES:
        if reverse:
            v = v + jnp.where(lane < LANES - k, pltpu.roll(v, LANES - k, 1), 0.0)
        else:
            v = v + jnp.where(lane >= k, pltpu.roll(v, k, 1), 0.0)
        k *= 2
    return v


def _inproj_kernel(*refs, segs, rope):
    x_ref, sh_ref, sc_ref, g_ref, w_ref, wdt_ref, bias_ref, alog_ref = refs[:8]
    pos = 8
    if rope:
        cos_ref, sin_ref = refs[8:10]
        pos = 10
    outs = iter(refs[pos:])
    x = x_ref[...]
    ms = jnp.mean(x * x, axis=-1, keepdims=True)
    h = (x * lax.rsqrt(ms + EPS)) * (g_ref[...] * (1.0 + sc_ref[0])) + sh_ref[0]
    hb = h.astype(BF16)
    if rope:
        cos = cos_ref[...]
        lane = lax.broadcasted_iota(jnp.int32, cos.shape, 1)
        first_half = (lane & (HEAD_DIM - 1)) < HEAD_DIM // 2
        sin_s = jnp.where(first_half, -sin_ref[...], sin_ref[...])
    for kind, c0, width in segs:
        if kind == "dtT":
            dt_ref, pre_ref, suf_ref = next(outs), next(outs), next(outs)
            dtt = lax.dot_general(wdt_ref[...], hb, (((1,), (1,)), ((), ())), preferred_element_type=F32)
            lane16 = lax.broadcasted_iota(jnp.int32, (2 * SSD_HEADS, CHUNK), 1)
            neg_a = -jnp.exp(alog_ref[...])
            for j in range(dt_ref.shape[0]):
                dt = _softplus(dtt[:, j * CHUNK:(j + 1) * CHUNK] + bias_ref[...])
                a = dt * neg_a
                dt_ref[j] = dt
                pre_ref[j] = _lane_cumsum(a, lane16, False)
                suf_ref[j] = _lane_cumsum(a, lane16, True)
            continue
        o_ref = next(outs)
        for j0 in range(0, width, 512):
            wj = min(512, width - j0)
            acc = jnp.dot(hb, w_ref[:, c0 + j0:c0 + j0 + wj], preferred_element_type=F32)
            if rope and kind in ("q", "k"):
                scale = HEAD_DIM ** -0.5 * LOG2E if kind == "q" else 1.0
                for l0 in range(0, wj, LANES):
                    t = _rope(acc[:, l0:l0 + LANES], cos, sin_s, first_half)
                    o_ref[:, j0 + l0:j0 + l0 + LANES] = (t * scale).astype(o_ref.dtype)
            else:
                o_ref[:, j0:j0 + wj] = acc.astype(o_ref.dtype)


_COL_Z, _COL_XBC, _COL_Q, _COL_K, _COL_V = 0, 512, 1536, 2048, 2176
_W_COLS = 2304


def _inproj(xf, mod3, norm1, w_cat, w_dtT, bias_rows, alog_rows, rope_tabs, *, rows, mod_row_fn, segs, seq_blocks):
    t, d = xf.shape
    rope = rope_tabs is not None
    in_specs = [
        pl.BlockSpec((rows, d), lambda i: (i, 0)),
        pl.BlockSpec((1, 1, d), lambda i: (mod_row_fn(i), 0, 0)),
        pl.BlockSpec((1, 1, d), lambda i: (mod_row_fn(i), 0, 1)),
        pl.BlockSpec((1, d), lambda i: (0, 0)),
        pl.BlockSpec(w_cat.shape, lambda i: (0, 0)),
        pl.BlockSpec(w_dtT.shape, lambda i: (0, 0)),
        pl.BlockSpec(bias_rows.shape, lambda i: (0, 0)),
        pl.BlockSpec(alog_rows.shape, lambda i: (0, 0)),
    ]
    args = [xf, mod3, mod3, norm1, w_cat, w_dtT, bias_rows, alog_rows]
    if rope:
        in_specs += [pl.BlockSpec((rows, LANES), lambda i: (i % seq_blocks, 0))] * 2
        args += list(rope_tabs)
    out_specs, out_shapes = [], []
    for kind, _, width in segs:
        if kind == "dtT":
            for _ in range(3):
                out_specs.append(pl.BlockSpec((rows // CHUNK, 2 * SSD_HEADS, CHUNK), lambda i: (i, 0, 0)))
                out_shapes.append(jax.ShapeDtypeStruct((t // CHUNK, 2 * SSD_HEADS, CHUNK), F32))
        else:
            dt = BF16 if kind in ("q", "k", "v") else F32
            out_specs.append(pl.BlockSpec((rows, width), lambda i: (i, 0)))
            out_shapes.append(jax.ShapeDtypeStruct((t, width), dt))
    return pl.pallas_call(
        functools.partial(_inproj_kernel, segs=tuple(segs), rope=rope),
        grid=(t // rows,),
        in_specs=in_specs,
        out_specs=out_specs,
        out_shape=out_shapes,
        name="inproj_rope" if rope else "inproj_ctx",
    )(*args)


_XB_COLS = SSD_INNER + SSD_GROUPS * SSD_STATE
_Y_OFF = CONV_DIM - _XB_COLS


def _ssd_kernel(*refs, need_y, nsteps, cps):
    (xbc_ref, prev_ref, next_ref, dt_ref, pre_ref, suf_ref, z_ref, cw_ref, cb_ref,
     dskip_ref, gain_ref, exp_ref, h0f_ref, h0b_ref) = refs[:14]
    if need_y:
        y_ref, hf_out, hb_out = refs[14:17]
        rest = refs[17:]
    else:
        y_ref = None
        hf_out, hb_out = refs[14:16]
        rest = refs[16:]
    cwin, u_ref, hf, hb, hbs, ust = rest

    sw = pl.program_id(1)
    s = pl.program_id(2)
    bi = jnp.where(sw == 0, nsteps - 1 - s, s)
    rows = xbc_ref.shape[1]
    h = SSD_HEADS

    @pl.when((sw == 0) & (s == 0))
    def _():
        hb[...] = h0b_ref[0]

    @pl.when((sw == 1) & (s == 0))
    def _():
        hf[...] = h0f_ref[0]

    ri = lax.broadcasted_iota(jnp.int32, (CHUNK, CHUNK), 0)
    ci = lax.broadcasted_iota(jnp.int32, (CHUNK, CHUNK), 1)
    lane128 = lax.broadcasted_iota(jnp.int32, (CHUNK, LANES), 1)
    lo_mask = lane128 < SSD_HEAD_DIM

    def conv_silu(c, c0, c1, out_ref):
        r0 = pl.multiple_of(c * CHUNK, CHUNK)
        above = xbc_ref[0, pl.ds(pl.multiple_of(jnp.maximum(r0 - SUBLANES, 0), SUBLANES), SUBLANES), c0:c1]
        halo_a = jnp.where(bi > 0, prev_ref[0, :, c0:c1], 0.0)
        cwin[0:SUBLANES, c0:c1] = jnp.where(c > 0, above, halo_a)
        cwin[SUBLANES:SUBLANES + CHUNK, c0:c1] = xbc_ref[0, pl.ds(r0, CHUNK), c0:c1]
        below = xbc_ref[0, pl.ds(pl.multiple_of(jnp.minimum(r0 + CHUNK, rows - SUBLANES), SUBLANES), SUBLANES), c0:c1]
        halo_b = jnp.where(bi < nsteps - 1, next_ref[0, :, c0:c1], 0.0)
        cwin[SUBLANES + CHUNK:, c0:c1] = jnp.where(c < cps - 1, below, halo_b)
        for j0 in range(c0, c1, LANES):
            acc = jnp.broadcast_to(cb_ref[:, j0:j0 + LANES], (CHUNK, LANES))
            for k in range(CONV_WIDTH):
                off = SUBLANES - CONV_WIDTH // 2 + k
                acc = acc + cwin[off:off + CHUNK, j0:j0 + LANES] * cw_ref[k:k + 1, j0:j0 + LANES]
            out_ref[:, j0 - c0:j0 - c0 + LANES] = _silu(acc)

    def dt_rows(c):
        return dt_ref[c], pre_ref[c], suf_ref[c]

    def state_update(state_ref, xb, w_rows, tot_col):
        dec = jnp.exp(jnp.sum(tot_col * exp_ref[...], axis=0, keepdims=True))
        for g in range(SSD_GROUPS):
            bm = xb[:, SSD_INNER + g * SSD_STATE:SSD_INNER + (g + 1) * SSD_STATE]
            bt = bm.T
            for pr in range(2):
                h0 = g * 4 + pr * 2
                c0 = h0 * SSD_HEAD_DIM
                xp = xb[:, c0:c0 + LANES]
                rhs = jnp.concatenate([jnp.where(lo_mask, xp, 0.0), jnp.where(lo_mask, 0.0, xp)],
                                      axis=0).astype(BF16)
                lhs = jnp.concatenate([bt * w_rows[h0:h0 + 1, :], bt * w_rows[h0 + 1:h0 + 2, :]],
                                      axis=1).astype(BF16)
                sres = jnp.dot(lhs, rhs, preferred_element_type=F32)
                state_ref[:, c0:c0 + LANES] = state_ref[:, c0:c0 + LANES] * dec[:, c0:c0 + LANES] + sres

    def col_forms(rows16):
        padded = jnp.concatenate([rows16, jnp.zeros((CHUNK - 2 * h, LANES), F32)], axis=0)
        return padded.T

    def backward_chunk(k, carry):
        c = cps - 1 - k
        gc = bi * cps + c
        xb = ust.at[gc]
        conv_silu(c, 0, _XB_COLS, xb)
        dt, _, suf = dt_rows(c)
        hbs[gc] = hb[...].astype(BF16)
        s0 = suf[:, 0:1]
        w_rows = jnp.exp(s0 - (suf - jnp.log(dt)))
        state_update(hb, xb, w_rows[h:2 * h, :], s0[h:2 * h, :])
        return carry

    def forward_chunk(c, carry):
        gc = bi * cps + c
        r0 = pl.multiple_of(c * CHUNK, CHUNK)
        xb = ust.at[gc]
        conv_silu(c, _XB_COLS, CONV_DIM, u_ref)
        dt, pre, suf = dt_rows(c)
        rowsf = jnp.concatenate([pre[0:h, :], suf[h:2 * h, :]], axis=0)
        colsf = col_forms(rowsf)
        ecol = jnp.exp(colsf)
        rowsd = rowsf - jnp.log(dt)
        for g in range(SSD_GROUPS):
            bm = xb[:, SSD_INNER + g * SSD_STATE:SSD_INNER + (g + 1) * SSD_STATE]
            cm = u_ref[:, g * SSD_STATE:(g + 1) * SSD_STATE]
            cmb = cm.astype(BF16)
            cbm = lax.dot_general(cmb, bm.astype(BF16), (((1,), (1,)), ((), ())),
                                  preferred_element_type=F32)
            for pr in range(2):
                h0 = g * 4 + pr * 2
                c0 = h0 * SSD_HEAD_DIM
                lhs_parts = []
                for hh in (h0, h0 + 1):
                    segf = colsf[:, hh:hh + 1] - rowsd[hh:hh + 1, :]
                    gf = jnp.exp(jnp.where(ci <= ri, segf, NEG))
                    segb = colsf[:, h + hh:h + hh + 1] - rowsd[h + hh:h + hh + 1, :]
                    gb = jnp.exp(jnp.where(ci >= ri, segb, NEG))
                    lhs_parts.append((cbm * (gf + gb)).astype(BF16))
                for hh in (h0, h0 + 1):
                    lhs_parts.append((cm * ecol[:, hh:hh + 1]).astype(BF16))
                for hh in (h0, h0 + 1):
                    lhs_parts.append((cm * ecol[:, h + hh:h + hh + 1]).astype(BF16))
                lhs = jnp.concatenate(lhs_parts, axis=1)
                xp = xb[:, c0:c0 + LANES]
                sf = hf[:, c0:c0 + LANES]
                sb = hbs[gc, :, c0:c0 + LANES].astype(F32)
                rhs = jnp.concatenate(
                    [jnp.where(lo_mask, xp, 0.0), jnp.where(lo_mask, 0.0, xp),
                     jnp.where(lo_mask, sf, 0.0), jnp.where(lo_mask, 0.0, sf),
                     jnp.where(lo_mask, sb, 0.0), jnp.where(lo_mask, 0.0, sb)], axis=0).astype(BF16)
                ypair = jnp.dot(lhs, rhs, preferred_element_type=F32)
                ypair = ypair + dskip_ref[:, c0:c0 + LANES] * xp
                zz = z_ref[0, pl.ds(r0, CHUNK), c0:c0 + LANES]
                u_ref[:, _Y_OFF + c0:_Y_OFF + c0 + LANES] = ypair * _silu(zz)
        last = pre[:, LANES - 1:LANES]
        w_rows = jnp.exp(last - rowsd)
        state_update(hf, xb, w_rows[0:h, :], last[0:h, :])
        if need_y:
            yv = u_ref[:, _Y_OFF:_Y_OFF + SSD_INNER]
            ms = jnp.mean(yv * yv, axis=-1, keepdims=True)
            y_ref[0, pl.ds(r0, CHUNK), :] = (yv * lax.rsqrt(ms + EPS) * gain_ref[...]).astype(y_ref.dtype)
        return carry

    @pl.when(sw == 0)
    def _():
        lax.fori_loop(0, cps, backward_chunk, 0, unroll=2)

    @pl.when(sw == 1)
    def _():
        lax.fori_loop(0, cps, forward_chunk, 0, unroll=2)

    @pl.when((sw == 0) & (s == nsteps - 1))
    def _():
        hb_out[0] = hb[...]

    @pl.when((sw == 1) & (s == nsteps - 1))
    def _():
        hf_out[0] = hf[...]


def _ssd(xbc, dts, z, consts, h0f, h0b, *, rows, need_y):
    bt, seq, _ = xbc.shape
    nsteps = seq // rows
    cps = rows // CHUNK
    hb8 = rows // SUBLANES
    nb8 = seq // SUBLANES
    cw, cb, dskip, gain, expm = consts
    dt_spec = pl.BlockSpec((cps, 2 * SSD_HEADS, CHUNK), lambda b, sw, s: (b * nsteps + blk(b, sw, s), 0, 0))

    def blk(b, sw, s):
        return jnp.where(sw == 0, nsteps - 1 - s, s)

    def full(a):
        return pl.BlockSpec(a.shape, lambda b, sw, s: (0,) * a.ndim)

    in_specs = [
        pl.BlockSpec((1, rows, CONV_DIM), lambda b, sw, s: (b, blk(b, sw, s), 0)),
        pl.BlockSpec((1, SUBLANES, CONV_DIM), lambda b, sw, s: (b, jnp.maximum(blk(b, sw, s) * hb8 - 1, 0), 0)),
        pl.BlockSpec((1, SUBLANES, CONV_DIM),
                     lambda b, sw, s: (b, jnp.minimum((blk(b, sw, s) + 1) * hb8, nb8 - 1), 0)),
        dt_spec, dt_spec, dt_spec,
        pl.BlockSpec((1, rows, SSD_INNER), lambda b, sw, s: (b, jnp.where(sw == 0, 0, s), 0)),
        full(cw), full(cb), full(dskip), full(gain), full(expm),
        pl.BlockSpec((1, SSD_STATE, SSD_INNER), lambda b, sw, s: (b, 0, 0)),
        pl.BlockSpec((1, SSD_STATE, SSD_INNER), lambda b, sw, s: (b, 0, 0)),
    ]
    st_spec = pl.BlockSpec((1, SSD_STATE, SSD_INNER), lambda b, sw, s: (b, 0, 0))
    st_shape = jax.ShapeDtypeStruct((bt, SSD_STATE, SSD_INNER), F32)
    out_specs, out_shapes = [st_spec, st_spec], [st_shape, st_shape]
    if need_y:
        out_specs = [pl.BlockSpec((1, rows, SSD_INNER), lambda b, sw, s: (b, jnp.where(sw == 0, 0, s), 0))] + out_specs
        out_shapes = [jax.ShapeDtypeStruct((bt, seq, SSD_INNER), BF16)] + out_shapes
    scratch = [
        pltpu.VMEM((CHUNK + 2 * SUBLANES, CONV_DIM), F32),
        pltpu.VMEM((CHUNK, _Y_OFF + SSD_INNER), F32),
        pltpu.VMEM((SSD_STATE, SSD_INNER), F32),
        pltpu.VMEM((SSD_STATE, SSD_INNER), F32),
        pltpu.VMEM((seq // CHUNK, SSD_STATE, SSD_INNER), BF16),
        pltpu.VMEM((seq // CHUNK, CHUNK, _XB_COLS), F32),
    ]
    return pl.pallas_call(
        functools.partial(_ssd_kernel, need_y=need_y, nsteps=nsteps, cps=cps),
        grid=(bt, 2, nsteps),
        in_specs=in_specs,
        out_specs=out_specs,
        out_shape=out_shapes,
        scratch_shapes=scratch,
        compiler_params=pltpu.CompilerParams(dimension_semantics=("arbitrary", "arbitrary", "arbitrary")),
        name="ssd_y" if need_y else "ssd_ctx",
    )(xbc, xbc, xbc, *dts, z, cw, cb, dskip, gain, expm, h0f, h0b)


def _attn_kernel(sink_ref, q_ref, kp_ref, kc_ref, kn_ref, vp_ref, vc_ref, vn_ref, kx_ref, vx_ref, o_ref, *, nsteps):
    m = pl.program_id(1)
    lane = lax.broadcasted_iota(jnp.int32, (1, LANES), 1)
    lo = lane < HEAD_DIM
    rq = Q_PER_KV * WINDOW
    per = ROWS_ATT // WINDOW
    qi = lax.broadcasted_iota(jnp.int32, (rq, WINDOW), 0) & (WINDOW - 1)
    kj = lax.broadcasted_iota(jnp.int32, (rq, WINDOW), 1)
    rblk = lax.shift_right_logical(lax.broadcasted_iota(jnp.int32, (rq, 1), 0), int(math.log2(WINDOW)))
    zero = jnp.zeros((), BF16)
    band_prev = jnp.where(kj >= qi, 0.0, NEG)
    band_next = jnp.where(kj <= qi, 0.0, NEG)

    kpieces = [kp_ref[0]] + [kc_ref[0, w * WINDOW:(w + 1) * WINDOW] for w in range(per)] + [kn_ref[0]]
    vpieces = [vp_ref[0]] + [vc_ref[0, w * WINDOW:(w + 1) * WINDOW] for w in range(per)] + [vn_ref[0]]
    kx = kx_ref[0]
    vx = vx_ref[0]
    nt = (((1,), (1,)), ((), ()))

    for sb in range(per):
        qsb = q_ref[0, sb * WINDOW:(sb + 1) * WINDOW, :]
        mask_prev = band_prev if sb > 0 else jnp.where(m > 0, band_prev, NEG)
        mask_next = band_next if sb < per - 1 else jnp.where(m < nsteps - 1, band_next, NEG)
        outs = [None] * Q_PER_KV
        for kv in range(ATTN_KV_HEADS):
            sel = lo if kv == 0 else jnp.logical_not(lo)
            qs = jnp.concatenate(
                [jnp.where(sel, qsb[:, j * LANES:(j + 1) * LANES], zero) for j in range(Q_PER_KV)], axis=0)
            s_prev = lax.dot_general(qs, kpieces[sb], nt, preferred_element_type=F32) + mask_prev
            s_cur = lax.dot_general(qs, kpieces[sb + 1], nt, preferred_element_type=F32)
            s_next = lax.dot_general(qs, kpieces[sb + 2], nt, preferred_element_type=F32) + mask_next
            s_ctx = lax.dot_general(qs, kx, nt, preferred_element_type=F32)
            sink = jnp.zeros((rq, 1), F32)
            for j in range(Q_PER_KV):
                sink = jnp.where(rblk == j, sink_ref[kv * Q_PER_KV + j], sink)
            n_ct = s_ctx.shape[1] // LANES
            tile_max = jnp.maximum(jnp.maximum(s_prev, s_cur), s_next)
            for t in range(n_ct):
                tile_max = jnp.maximum(tile_max, s_ctx[:, t * LANES:(t + 1) * LANES])
            mx = jnp.maximum(jnp.max(tile_max, axis=-1, keepdims=True), sink)
            p_prev = jnp.exp2(s_prev - mx)
            p_cur = jnp.exp2(s_cur - mx)
            p_next = jnp.exp2(s_next - mx)
            p_ctx = jnp.exp2(s_ctx - mx)
            tile_sum = p_prev + p_cur + p_next
            for t in range(n_ct):
                tile_sum = tile_sum + p_ctx[:, t * LANES:(t + 1) * LANES]
            den = jnp.sum(tile_sum, axis=-1, keepdims=True) + jnp.exp2(sink - mx)
            acc = jnp.dot(p_prev.astype(BF16), jnp.where(sel, vpieces[sb], zero), preferred_element_type=F32)
            acc += jnp.dot(p_cur.astype(BF16), jnp.where(sel, vpieces[sb + 1], zero), preferred_element_type=F32)
            acc += jnp.dot(p_next.astype(BF16), jnp.where(sel, vpieces[sb + 2], zero), preferred_element_type=F32)
            acc += jnp.dot(p_ctx.astype(BF16), jnp.where(sel, vx, zero), preferred_element_type=F32)
            acc = acc * (1.0 / den)
            for j in range(Q_PER_KV):
                part = acc[j * WINDOW:(j + 1) * WINDOW, :]
                outs[j] = part if outs[j] is None else outs[j] + part
        for j in range(Q_PER_KV):
            o_ref[0, sb * WINDOW:(sb + 1) * WINDOW, j * LANES:(j + 1) * LANES] = outs[j].astype(o_ref.dtype)


def _attention(sinks, q, k, v, kx, vx):
    bt, seq, _ = q.shape
    nsteps = seq // ROWS_ATT
    per = ROWS_ATT // WINDOW
    nb = seq // WINDOW
    n_ctx = kx.shape[1]

    def prev(b, m, s):
        return (b, jnp.maximum(m * per - 1, 0), 0)

    def cur(b, m, s):
        return (b, m, 0)

    def nxt(b, m, s):
        return (b, jnp.minimum((m + 1) * per, nb - 1), 0)

    small = (1, WINDOW, KV_WIDTH)
    big = (1, ROWS_ATT, KV_WIDTH)
    grid_spec = pltpu.PrefetchScalarGridSpec(
        num_scalar_prefetch=1,
        grid=(bt, nsteps),
        in_specs=[
            pl.BlockSpec((1, ROWS_ATT, ATTN_WIDTH), cur),
            pl.BlockSpec(small, prev), pl.BlockSpec(big, cur), pl.BlockSpec(small, nxt),
            pl.BlockSpec(small, prev), pl.BlockSpec(big, cur), pl.BlockSpec(small, nxt),
            pl.BlockSpec((1, n_ctx, KV_WIDTH), lambda b, m, s: (b, 0, 0)),
            pl.BlockSpec((1, n_ctx, KV_WIDTH), lambda b, m, s: (b, 0, 0)),
        ],
        out_specs=pl.BlockSpec((1, ROWS_ATT, ATTN_WIDTH), cur),
    )
    return pl.pallas_call(
        functools.partial(_attn_kernel, nsteps=nsteps),
        grid_spec=grid_spec,
        out_shape=jax.ShapeDtypeStruct((bt, seq, ATTN_WIDTH), BF16),
        name="attn",
    )(sinks, q, k, k, k, v, v, v, kx, vx)


def _outproj_kernel(ssd_ref, att_ref, x_ref, g1_ref, sh_ref, sc_ref, n2_ref, wa_ref, wb_ref, wr_ref, br_ref,
                    tri_ref, x1_ref, h2_ref, ids_ref, wcol_ref, cnt_ref, carry, *, nsteps):
    i = pl.program_id(0)
    rows = x_ref.shape[0]

    @pl.when(i == 0)
    def _():
        carry[...] = jnp.zeros_like(carry)

    acc = jnp.dot(ssd_ref[...], wa_ref[...], preferred_element_type=F32)
    acc += jnp.dot(att_ref[...], wb_ref[...], preferred_element_type=F32)
    x1 = x_ref[...] + g1_ref[0] * acc
    x1_ref[...] = x1
    ms = jnp.mean(x1 * x1, axis=-1, keepdims=True)
    h2 = (x1 * lax.rsqrt(ms + EPS)) * (n2_ref[...] * (1.0 + sc_ref[0])) + sh_ref[0]
    h2b = h2.astype(BF16)
    half = h2.shape[1] // 2
    lo_bits = lax.shift_right_logical(pltpu.bitcast(h2b[:, :half].astype(F32), jnp.uint32), jnp.uint32(16))
    hi_bits = pltpu.bitcast(h2b[:, half:].astype(F32), jnp.uint32) & jnp.uint32(0xFFFF0000)
    h2_ref[...] = hi_bits | lo_bits
    lt = lax.dot_general(wr_ref[...], h2b, (((1,), (1,)), ((), ())),
                         preferred_element_type=F32) + br_ref[...]
    row = lax.broadcasted_iota(jnp.int32, lt.shape, 0).astype(F32)
    big_i = float(ROUTE_ROWS)
    gl = jnp.where(row < N_GROUPS, lt, NEG)
    gmax = jnp.max(gl, axis=0, keepdims=True)
    gidx = jnp.min(jnp.where(gl == gmax, row, big_i), axis=0, keepdims=True)
    g_w = 1.0 / jnp.sum(jnp.exp(gl - gmax), axis=0, keepdims=True)
    lo = N_GROUPS + EXPERTS_PER_GROUP * gidx
    el = jnp.where((row >= lo) & (row < lo + EXPERTS_PER_GROUP), lt, NEG)
    m1 = jnp.max(el, axis=0, keepdims=True)
    i1 = jnp.min(jnp.where(el == m1, row, big_i), axis=0, keepdims=True)
    el2 = jnp.where(row == i1, NEG, el)
    m2 = jnp.max(el2, axis=0, keepdims=True)
    i2 = jnp.min(jnp.where(el2 == m2, row, big_i), axis=0, keepdims=True)
    r = jnp.exp(m2 - m1)
    w1 = g_w / (1.0 + r)
    w2 = g_w * r / (1.0 + r)
    oh1 = row == i1
    oh2 = row == i2
    cnt = jnp.where(oh1 | oh2, 1.0, 0.0)
    prefix = jnp.dot(cnt.astype(BF16), tri_ref[...], preferred_element_type=F32)
    base = carry[...] + prefix
    rank1 = jnp.sum(jnp.where(oh1, base, 0.0), axis=0, keepdims=True)
    rank2 = jnp.sum(jnp.where(oh2, base, 0.0), axis=0, keepdims=True)
    carry[...] = carry[...] + jnp.sum(cnt, axis=1, keepdims=True)
    r8 = lax.broadcasted_iota(jnp.int32, (SUBLANES, rows), 0)
    code1 = (i1 - N_GROUPS) * float(1 << RANK_BITS) + rank1
    code2 = (i2 - N_GROUPS) * float(1 << RANK_BITS) + rank2
    ids_ref[...] = jnp.where(r8 == 0, code1, jnp.where(r8 == 1, code2, 0.0)).astype(jnp.int32)
    r128 = lax.broadcasted_iota(jnp.int32, (LANES, rows), 0)
    wfull = jnp.where(r128 == 0, w1, jnp.where(r128 == 1, w2, 0.0))
    for j in range(rows // LANES):
        wcol_ref[j * LANES:(j + 1) * LANES, :] = wfull[:, j * LANES:(j + 1) * LANES].T

    @pl.when(i == nsteps - 1)
    def _():
        cnt_ref[...] = carry[:, 0:LANES]


def _outproj(ssd, att, xf, mod3, norm2, wa, wb, wr, br, tri, *, rows, seq_blocks):
    t, d = xf.shape
    nsteps = t // rows

    def mod(col):
        return pl.BlockSpec((1, 1, d), lambda i: (i // seq_blocks, 0, col))

    def full(a):
        return pl.BlockSpec(a.shape, lambda i: (0,) * a.ndim)

    return pl.pallas_call(
        functools.partial(_outproj_kernel, nsteps=nsteps),
        grid=(nsteps,),
        in_specs=[pl.BlockSpec((rows, SSD_INNER), lambda i: (i, 0)),
                  pl.BlockSpec((rows, ATTN_WIDTH), lambda i: (i, 0)),
                  pl.BlockSpec((rows, d), lambda i: (i, 0)),
                  mod(2), mod(3), mod(4), full(norm2), full(wa), full(wb), full(wr), full(br), full(tri)],
        out_specs=[pl.BlockSpec((rows, d), lambda i: (i, 0)),
                   pl.BlockSpec((rows, d // 2), lambda i: (i, 0)),
                   pl.BlockSpec((SUBLANES, rows), lambda i: (0, i)),
                   pl.BlockSpec((rows, LANES), lambda i: (i, 0)),
                   pl.BlockSpec((ROUTE_ROWS, LANES), lambda i: (0, 0))],
        out_shape=[jax.ShapeDtypeStruct((t, d), F32),
                   jax.ShapeDtypeStruct((t, d // 2), jnp.uint32),
                   jax.ShapeDtypeStruct((SUBLANES, t), jnp.int32),
                   jax.ShapeDtypeStruct((t, LANES), F32),
                   jax.ShapeDtypeStruct((ROUTE_ROWS, LANES), F32)],
        scratch_shapes=[pltpu.VMEM((ROUTE_ROWS, rows), F32)],
        compiler_params=pltpu.CompilerParams(dimension_semantics=("arbitrary",)),
        name="outproj_router",
    )(ssd, att, xf, mod3, mod3, mod3, norm2, wa, wb, wr, br, tri)


def _expert_kernel(blk_e_ref, nused_ref, pstart_ref, counts_ref, eord_ref, enext_ref, dst_ref,
                   h2_hbm, wg_hbm, wu_hbm, wd_hbm, y_hbm,
                   h2v, xg, ystage, wgs, wus, wds, wgub, wdb, tok_ref, sem, wsem, ysem, *, n_tok, nblocks):
    nused = nused_ref[0]

    def h2_copy():
        return pltpu.make_async_copy(h2_hbm, h2v, sem.at[0])

    def y_copy(blk, sl):
        row0 = blk * MOE_BLK if isinstance(blk, int) else pl.multiple_of(blk * MOE_BLK, MOE_BLK)
        return pltpu.make_async_copy(ystage.at[sl], y_hbm.at[pl.ds(row0, MOE_BLK)], ysem.at[sl])

    def weight_copies(e, sl):
        return (pltpu.make_async_copy(wg_hbm.at[e], wgs.at[sl], wsem.at[sl, 0]),
                pltpu.make_async_copy(wu_hbm.at[e], wus.at[sl], wsem.at[sl, 1]),
                pltpu.make_async_copy(wd_hbm.at[e], wds.at[sl], wsem.at[sl, 2]))

    def gather_rows(blk, sl):
        for r in range(MOE_BLK):
            xg[sl, pl.ds(r, 1), :] = h2v[pl.ds(tok_ref[blk * MOE_BLK + r], 1), :]

    h2_copy().start()
    for cp in weight_copies(blk_e_ref[0], 0):
        cp.start()

    def pad_expert(e, carry):
        lo = pstart_ref[e] + counts_ref[e]
        hi = pstart_ref[e] + (counts_ref[e] + MOE_BLK - 1) // MOE_BLK * MOE_BLK

        def pad_slot(s, c):
            tok_ref[s] = 0
            return c
        return lax.fori_loop(lo, hi, pad_slot, carry)
    lax.fori_loop(0, N_EXPERTS, pad_expert, 0)

    def claim(t, carry):
        tok_ref[dst_ref[t]] = t
        tok_ref[dst_ref[n_tok + t]] = t
        return carry
    lax.fori_loop(0, n_tok, claim, 0, unroll=16)
    h2_copy().wait()
    gather_rows(0, 0)

    def block(b, carry):
        slot = b % 2
        e = blk_e_ref[b]
        e_prev = blk_e_ref[jnp.maximum(b - 1, 0)]

        @pl.when(b >= 2)
        def _():
            y_copy(0, slot).wait()

        @pl.when((b == 0) | (e != e_prev))
        def _():
            sl = eord_ref[e] % 2
            for cp in weight_copies(e, sl):
                cp.wait()
            wgub[:, 0:EXPERT_DIM] = wgs[sl].astype(BF16)
            wgub[:, EXPERT_DIM:] = wus[sl].astype(BF16)
            wdb[...] = wds[sl].astype(BF16)
            e_next = enext_ref[e]

            @pl.when(e_next >= 0)
            def _():
                for cp in weight_copies(e_next, 1 - sl):
                    cp.start()

        words = xg[slot]
        x_lo = pltpu.bitcast(words << jnp.uint32(16), F32).astype(BF16)
        x_hi = pltpu.bitcast(words & jnp.uint32(0xFFFF0000), F32).astype(BF16)
        gu = jnp.dot(jnp.concatenate([x_lo, x_hi], axis=1), wgub[...], preferred_element_type=F32)
        hmid = (_silu(gu[:, 0:EXPERT_DIM]) * gu[:, EXPERT_DIM:]).astype(BF16)
        ystage[slot] = jnp.dot(hmid, wdb[...], preferred_element_type=F32)
        gather_rows(jnp.minimum(b + 1, nused - 1), 1 - slot)
        y_copy(b, slot).start()
        return carry
    lax.fori_loop(0, nused, block, 0)

    @pl.when(nused >= 2)
    def _():
        y_copy(0, nused % 2).wait()

    y_copy(0, (nused - 1) % 2).wait()
    ystage[0] = jnp.zeros(ystage.shape[1:], F32)

    def zero_block(b, carry):
        cp = y_copy(b, 0)
        cp.start()
        cp.wait()
        return carry
    lax.fori_loop(nused, nblocks, zero_block, 0)


def _experts(blk_e, nused, pstart, counts, eord, enext, dst, h2p, w_gate, w_up, w_down, *, nblocks, n_tok):
    d = w_gate.shape[1]
    hbm = pl.BlockSpec(memory_space=pl.ANY)
    grid_spec = pltpu.PrefetchScalarGridSpec(
        num_scalar_prefetch=7,
        grid=(1,),
        in_specs=[hbm, hbm, hbm, hbm],
        out_specs=hbm,
        scratch_shapes=[
            pltpu.VMEM(h2p.shape, jnp.uint32),
            pltpu.VMEM((2, MOE_BLK, d // 2), jnp.uint32),
            pltpu.VMEM((2, MOE_BLK, d), F32),
            pltpu.VMEM((2, d, EXPERT_DIM), F32),
            pltpu.VMEM((2, d, EXPERT_DIM), F32),
            pltpu.VMEM((2, EXPERT_DIM, d), F32),
            pltpu.VMEM((d, 2 * EXPERT_DIM), BF16),
            pltpu.VMEM((EXPERT_DIM, d), BF16),
            pltpu.SMEM((nblocks * MOE_BLK,), jnp.int32),
            pltpu.SemaphoreType.DMA((1,)),
            pltpu.SemaphoreType.DMA((2, 3)),
            pltpu.SemaphoreType.DMA((2,)),
        ],
    )
    return pl.pallas_call(
        functools.partial(_expert_kernel, n_tok=n_tok, nblocks=nblocks),
        grid_spec=grid_spec,
        out_shape=jax.ShapeDtypeStruct((nblocks * MOE_BLK, d), F32),
        compiler_params=pltpu.CompilerParams(dimension_semantics=("arbitrary",)),
        name="experts",
    )(blk_e, nused, pstart, counts, eord, enext, dst, h2p, w_gate, w_up, w_down)


def _combine_kernel(dst_ref, y_hbm, x1_ref, wcol_ref, g2_ref, nf_ref, o_ref, ybuf, sem, *, nsteps, n_tok):
    i = pl.program_id(0)
    rows = x1_ref.shape[0]
    slot = i % 2

    def row_copy(src, r, k, sl):
        return pltpu.make_async_copy(y_hbm.at[pl.ds(src, 1)], ybuf.at[sl, k, pl.ds(r, 1)], sem.at[sl])

    def start_rows(step, sl):
        for r in range(rows):
            for k in range(2):
                row_copy(dst_ref[k * n_tok + step * rows + r], r, k, sl).start()

    def wait_rows(sl):
        for r in range(rows):
            for k in range(2):
                row_copy(0, r, k, sl).wait()

    @pl.when(i == 0)
    def _():
        start_rows(0, 0)

    @pl.when(i + 1 < nsteps)
    def _():
        start_rows(i + 1, 1 - slot)

    wait_rows(slot)
    moe = ybuf[slot, 0] * wcol_ref[:, 0:1] + ybuf[slot, 1] * wcol_ref[:, 1:2]
    x2 = x1_ref[...] + g2_ref[0] * moe
    ms = jnp.mean(x2 * x2, axis=-1, keepdims=True)
    o_ref[...] = x2 * lax.rsqrt(ms + EPS) * nf_ref[...]


def _combine(dst, y_buf, x1, wcol, mod3, norm_final, *, rows, seq_blocks):
    t, d = x1.shape
    nsteps = t // rows
    grid_spec = pltpu.PrefetchScalarGridSpec(
        num_scalar_prefetch=1,
        grid=(nsteps,),
        in_specs=[
            pl.BlockSpec(memory_space=pl.ANY),
            pl.BlockSpec((rows, d), lambda i, ds: (i, 0)),
            pl.BlockSpec((rows, LANES), lambda i, ds: (i, 0)),
            pl.BlockSpec((1, 1, d), lambda i, ds: (i // seq_blocks, 0, 5)),
            pl.BlockSpec((1, d), lambda i, ds: (0, 0)),
        ],
        out_specs=pl.BlockSpec((rows, d), lambda i, ds: (i, 0)),
        scratch_shapes=[pltpu.VMEM((2, 2, rows, d), F32), pltpu.SemaphoreType.DMA((2,))],
    )
    return pl.pallas_call(
        functools.partial(_combine_kernel, nsteps=nsteps, n_tok=t),
        grid_spec=grid_spec,
        out_shape=jax.ShapeDtypeStruct((t, d), F32),
        compiler_params=pltpu.CompilerParams(dimension_semantics=("arbitrary",)),
        name="combine",
    )(dst, y_buf, x1, wcol, mod3, norm_final)


def _rope_tables(seq):
    pos = np.arange(seq)
    n_freq = HEAD_DIM // 4
    inv = ROPE_BASE ** (-np.arange(n_freq, dtype=np.float32) / n_freq)
    ang = np.concatenate([(pos // GRID_W)[:, None] * inv, (pos % GRID_W)[:, None] * inv], axis=-1)
    ang = np.concatenate([ang, ang, ang, ang], axis=-1).astype(np.float32)
    return jnp.asarray(np.cos(ang), F32), jnp.asarray(np.sin(ang), F32)


def _head_expand_mat():
    m = np.zeros((SSD_HEADS, SSD_INNER), np.float32)
    for hh in range(SSD_HEADS):
        m[hh, hh * SSD_HEAD_DIM:(hh + 1) * SSD_HEAD_DIM] = 1.0
    return jnp.asarray(m)


def kernel(x, c, ctx, c_ctx, w_ada, b_ada, norm1, w_in, conv_w, conv_b, dt_bias, a_log, d_skip, ssd_norm,
           attn_sinks, w_out, norm2, w_group, b_group, w_expert, b_expert, w_gate, w_up, w_down, norm_final):
    return _pipeline(x, c, ctx, c_ctx, w_ada, b_ada, norm1, w_in, conv_w, conv_b, dt_bias, a_log, d_skip,
                     ssd_norm, attn_sinks, w_out, norm2, w_group, b_group, w_expert, b_expert, w_gate, w_up,
                     w_down, norm_final)["out"]


def _pipeline(x, c, ctx, c_ctx, w_ada, b_ada, norm1, w_in, conv_w, conv_b, dt_bias, a_log, d_skip, ssd_norm,
              attn_sinks, w_out, norm2, w_group, b_group, w_expert, b_expert, w_gate, w_up, w_down, norm_final):
    bt, seq, d = x.shape
    n_ctx = ctx.shape[1]
    t = bt * seq
    layer = 0
    assert w_ada.shape[0] == 1 and seq % ROWS_IN == 0 and seq % ROWS_OUT == 0 and n_ctx % CHUNK == 0

    cc = jnp.concatenate([c, c_ctx[None, :], jnp.zeros((SUBLANES - bt - 1, d), F32)], axis=0)
    mod = _ada(cc, w_ada[layer], b_ada[layer][None, :])
    mod3 = mod.reshape(SUBLANES, 1, 6 * d)

    w = w_in[layer]
    o_z, o_xbc, o_dt = 0, SSD_INNER, SSD_INNER + CONV_DIM
    o_q = o_dt + 2 * SSD_HEADS
    o_k, o_v = o_q + ATTN_WIDTH, o_q + ATTN_WIDTH + KV_WIDTH
    q_perm = np.concatenate([np.r_[j * HEAD_DIM:(j + 1) * HEAD_DIM, (j + 4) * HEAD_DIM:(j + 5) * HEAD_DIM]
                             for j in range(Q_PER_KV)])
    w_q = w[:, o_q:o_q + ATTN_WIDTH].reshape(d, ATTN_KV_HEADS, Q_PER_KV, HEAD_DIM).transpose(0, 2, 1, 3)
    w_q = w_q.reshape(d, ATTN_WIDTH)
    w_cat = jnp.concatenate([w[:, o_z:o_z + SSD_INNER], w[:, o_xbc:o_xbc + CONV_DIM], w_q,
                             w[:, o_k:o_k + KV_WIDTH], w[:, o_v:o_v + KV_WIDTH]], axis=1).astype(BF16)
    w_dtT = w[:, o_dt:o_dt + 2 * SSD_HEADS].T.astype(BF16)

    seq_blocks = seq // ROWS_IN
    cos, sin = _rope_tables(seq)
    bias_rows = jnp.broadcast_to(dt_bias[layer].reshape(2 * SSD_HEADS, 1), (2 * SSD_HEADS, LANES))
    alog_rows = jnp.broadcast_to(a_log[layer].reshape(2 * SSD_HEADS, 1), (2 * SSD_HEADS, LANES))
    segs_x = [("z", _COL_Z, SSD_INNER), ("xbc", _COL_XBC, CONV_DIM), ("dtT", 0, 0),
              ("q", _COL_Q, ATTN_WIDTH), ("k", _COL_K, KV_WIDTH), ("v", _COL_V, KV_WIDTH)]
    z_x, xbc_x, dt_x, pre_x, suf_x, q_x, k_x, v_x = _inproj(
        x.reshape(t, d), mod3, norm1[layer][None, :], w_cat, w_dtT, bias_rows, alog_rows, (cos, sin),
        rows=ROWS_IN, mod_row_fn=lambda i: i // seq_blocks, segs=segs_x, seq_blocks=seq_blocks)
    segs_c = [("xbc", _COL_XBC, CONV_DIM), ("dtT", 0, 0), ("k", _COL_K, KV_WIDTH), ("v", _COL_V, KV_WIDTH)]
    xbc_c, dt_c, pre_c, suf_c, k_c, v_c = _inproj(
        ctx.reshape(bt * n_ctx, d), mod3, norm1[layer][None, :], w_cat, w_dtT, bias_rows, alog_rows, None,
        rows=n_ctx, mod_row_fn=lambda i: bt, segs=segs_c, seq_blocks=1)

    cw = jnp.concatenate([conv_w[layer], jnp.zeros((SUBLANES - CONV_WIDTH, CONV_DIM), F32)], axis=0)
    cb = conv_b[layer][None, :]
    dskip_e = jnp.repeat(d_skip[layer], SSD_HEAD_DIM)[None, :]
    consts = (cw, cb, dskip_e, ssd_norm[layer][None, :], _head_expand_mat())

    zeros_state = jnp.zeros((bt, SSD_STATE, SSD_INNER), F32)
    z_dummy = jnp.zeros((bt, n_ctx, SSD_INNER), F32)
    h_cf, h_cb = _ssd(xbc_c.reshape(bt, n_ctx, CONV_DIM), (dt_c, pre_c, suf_c), z_dummy, consts,
                      zeros_state, zeros_state, rows=n_ctx, need_y=False)
    ssd_x, _, _ = _ssd(xbc_x.reshape(bt, seq, CONV_DIM), (dt_x, pre_x, suf_x), z_x.reshape(bt, seq, SSD_INNER),
                       consts, h_cf, h_cb, rows=ROWS_SSD, need_y=True)

    attn_x = _attention(attn_sinks[layer].astype(F32) * LOG2E,
                        q_x.reshape(bt, seq, ATTN_WIDTH), k_x.reshape(bt, seq, KV_WIDTH),
                        v_x.reshape(bt, seq, KV_WIDTH), k_c.reshape(bt, n_ctx, KV_WIDTH),
                        v_c.reshape(bt, n_ctx, KV_WIDTH))

    wo = w_out[layer]
    wa = wo[:SSD_INNER].astype(BF16)
    wb = wo[SSD_INNER:].reshape(ATTN_KV_HEADS, Q_PER_KV, HEAD_DIM, d).transpose(1, 0, 2, 3)
    wb = wb.reshape(ATTN_WIDTH, d).astype(BF16)
    n_pad = ROUTE_ROWS - N_GROUPS - N_EXPERTS
    wr = jnp.concatenate([w_group[layer], w_expert[layer], jnp.zeros((d, n_pad), F32)], axis=1).T.astype(BF16)
    br = jnp.concatenate([b_group[layer], b_expert[layer], jnp.zeros((n_pad,), F32)])
    br = jnp.broadcast_to(br[:, None], (ROUTE_ROWS, ROWS_OUT))
    tri = jnp.asarray(np.triu(np.ones((ROWS_OUT, ROWS_OUT), np.float32), 1), BF16)
    x1, h2, ids, wcol, cnt = _outproj(
        ssd_x.reshape(t, SSD_INNER), attn_x.reshape(t, ATTN_WIDTH), x.reshape(t, d), mod3, norm2[layer][None, :],
        wa, wb, wr, br, tri, rows=ROWS_OUT, seq_blocks=seq // ROWS_OUT)

    counts = cnt[N_GROUPS:N_GROUPS + N_EXPERTS, 0].astype(jnp.int32)
    padded = (counts + MOE_BLK - 1) // MOE_BLK * MOE_BLK
    ends = jnp.cumsum(padded)
    pstart = ends - padded
    nblocks = (2 * t) // MOE_BLK + N_EXPERTS
    blk_start = jnp.arange(nblocks, dtype=jnp.int32) * MOE_BLK
    eids = jnp.arange(N_EXPERTS, dtype=jnp.int32)
    active = counts > 0
    last_active = jnp.max(jnp.where(active, eids, 0))
    blk_e = jnp.minimum(jnp.sum((ends[None, :] <= blk_start[:, None]).astype(jnp.int32), axis=1), last_active)
    nused = (ends[-1] // MOE_BLK).astype(jnp.int32).reshape(1)
    eord = jnp.cumsum(active.astype(jnp.int32)) - active.astype(jnp.int32)
    later = active[None, :] & (eids[None, :] > eids[:, None])
    enext = jnp.min(jnp.where(later, eids[None, :], N_EXPERTS), axis=1)
    enext = jnp.where(enext == N_EXPERTS, -1, enext).astype(jnp.int32)
    code = ids[0:2].reshape(-1)
    e_id = lax.shift_right_logical(code, RANK_BITS)
    onehot = e_id[:, None] == jnp.arange(N_EXPERTS, dtype=jnp.int32)[None, :]
    dst = jnp.sum(jnp.where(onehot, pstart[None, :], 0), axis=1) + (code & ((1 << RANK_BITS) - 1))

    y_buf = _experts(blk_e, nused, pstart, counts, eord, enext, dst, h2, w_gate[layer], w_up[layer],
                     w_down[layer], nblocks=nblocks, n_tok=t)
    out = _combine(dst, y_buf, x1, wcol, mod3, norm_final[None, :], rows=ROWS_CMB, seq_blocks=seq // ROWS_CMB)
    return dict(out=out.reshape(bt, seq, d), mod=mod, z=z_x, xbc=xbc_x, dt=dt_x, q=q_x, k=k_x, v=v_x,
                k_c=k_c, v_c=v_c, h_cf=h_cf, h_cb=h_cb, ssd=ssd_x, attn=attn_x, x1=x1, h2=h2, ids=ids,
                wcol=wcol, cnt=cnt, y_buf=y_buf, pstart=pstart, q_perm=q_perm)
```

```python
import functools
import math

import numpy as np
import jax
import jax.numpy as jnp
from jax import lax
from jax.experimental import pallas as pl
from jax.experimental.pallas import tpu as pltpu

F32 = jnp.float32
BF16 = jnp.bfloat16

D_MODEL = 1024
GRID_W = 64
EPS = 1e-6
SSD_INNER = 512
SSD_HEAD_DIM = 64
SSD_HEADS = 8
SSD_GROUPS = 2
SSD_STATE = 128
CONV_WIDTH = 5
CONV_DIM = SSD_INNER + 2 * SSD_GROUPS * SSD_STATE
CHUNK = 128
ATTN_WIDTH = 512
HEAD_DIM = 64
ATTN_Q_HEADS = 8
ATTN_KV_HEADS = 2
Q_PER_KV = 4
KV_WIDTH = 128
WINDOW = 128
ROPE_BASE = 10000.0
N_GROUPS = 4
EXPERTS_PER_GROUP = 8
N_EXPERTS = 32
EXPERT_DIM = 512

LANES = 128
SUBLANES = 8
NEG = -1e30
LOG2E = math.log2(math.e)

ROWS_IN = 1024
ROWS_OUT = 1024
ROWS_SSD = 1024
ROWS_ATT = 1024
ROUTE_ROWS = 48
MOE_BLK = 256
RANK_BITS = 16
D_TILES = D_MODEL // LANES
ROWS_CMB = 128


def _silu(v):
    half = 0.5 * v
    return half + half * jnp.tanh(half)


def _softplus(v):
    return jnp.maximum(v, 0.0) + jnp.log(1.0 + jnp.exp(-jnp.abs(v)))


def _ada_kernel(c_ref, w_ref, b_ref, o_ref):
    a = _silu(c_ref[...])
    o_ref[...] = jnp.dot(a.astype(BF16), w_ref[...].astype(BF16),
                         preferred_element_type=F32) + b_ref[...]


def _ada(cc, w_ada, b_ada):
    d, n = w_ada.shape
    tn = 1536
    return pl.pallas_call(
        _ada_kernel,
        grid=(n // tn,),
        in_specs=[pl.BlockSpec((SUBLANES, d), lambda j: (0, 0)),
                  pl.BlockSpec((d, tn), lambda j: (0, j)),
                  pl.BlockSpec((1, tn), lambda j: (0, j))],
        out_specs=pl.BlockSpec((SUBLANES, tn), lambda j: (0, j)),
        out_shape=jax.ShapeDtypeStruct((SUBLANES, n), F32),
        name="ada",
    )(cc, w_ada, b_ada)


def _rope(t, cos, sin_signed, first_half):
    rot = jnp.where(first_half, pltpu.roll(t, LANES - HEAD_DIM // 2, 1), pltpu.roll(t, HEAD_DIM // 2, 1))
    return t * cos + rot * sin_signed


def _lane_cumsum(v, lane, reverse):
    k = 1
    while k < LANES:
        if reverse:
            v = v + jnp.where(lane < LANES - k, pltpu.roll(v, LANES - k, 1), 0.0)
        else:
            v = v + jnp.where(lane >= k, pltpu.roll(v, k, 1), 0.0)
        k *= 2
    return v


def _inproj_kernel(*refs, segs, rope):
    x_ref, sh_ref, sc_ref, g_ref, w_ref, wdt_ref, bias_ref, alog_ref = refs[:8]
    pos = 8
    if rope:
        cos_ref, sin_ref = refs[8:10]
        pos = 10
    outs = iter(refs[pos:])
    x = x_ref[...]
    ms = jnp.mean(x * x, axis=-1, keepdims=True)
    h = (x * lax.rsqrt(ms + EPS)) * (g_ref[...] * (1.0 + sc_ref[0])) + sh_ref[0]
    hb = h.astype(BF16)
    if rope:
        cos = cos_ref[...]
        lane = lax.broadcasted_iota(jnp.int32, cos.shape, 1)
        first_half = (lane & (HEAD_DIM - 1)) < HEAD_DIM // 2
        sin_s = jnp.where(first_half, -sin_ref[...], sin_ref[...])
    for kind, c0, width in segs:
        if kind == "dtT":
            dt_ref, pre_ref, suf_ref = next(outs), next(outs), next(outs)
            dtt = lax.dot_general(wdt_ref[...], hb, (((1,), (1,)), ((), ())), preferred_element_type=F32)
            lane16 = lax.broadcasted_iota(jnp.int32, (2 * SSD_HEADS, CHUNK), 1)
            neg_a = -jnp.exp(alog_ref[...])
            for j in range(dt_ref.shape[0]):
                dt = _softplus(dtt[:, j * CHUNK:(j + 1) * CHUNK] + bias_ref[...])
                a = dt * neg_a
                dt_ref[j] = dt
                pre_ref[j] = _lane_cumsum(a, lane16, False)
                suf_ref[j] = _lane_cumsum(a, lane16, True)
            continue
        o_ref = next(outs)
        for j0 in range(0, width, 512):
            wj = min(512, width - j0)
            acc = jnp.dot(hb, w_ref[:, c0 + j0:c0 + j0 + wj], preferred_element_type=F32)
            if rope and kind in ("q", "k"):
                scale = HEAD_DIM ** -0.5 * LOG2E if kind == "q" else 1.0
                for l0 in range(0, wj, LANES):
                    t = _rope(acc[:, l0:l0 + LANES], cos, sin_s, first_half)
                    o_ref[:, j0 + l0:j0 + l0 + LANES] = (t * scale).astype(o_ref.dtype)
            else:
                o_ref[:, j0:j0 + wj] = acc.astype(o_ref.dtype)


_COL_Z, _COL_XBC, _COL_Q, _COL_K, _COL_V = 0, 512, 1536, 2048, 2176
_W_COLS = 2304


def _inproj(xf, mod3, norm1, w_cat, w_dtT, bias_rows, alog_rows, rope_tabs, *, rows, mod_row_fn, segs, seq_blocks):
    t, d = xf.shape
    rope = rope_tabs is not None
    in_specs = [
        pl.BlockSpec((rows, d), lambda i: (i, 0)),
        pl.BlockSpec((1, 1, d), lambda i: (mod_row_fn(i), 0, 0)),
        pl.BlockSpec((1, 1, d), lambda i: (mod_row_fn(i), 0, 1)),
        pl.BlockSpec((1, d), lambda i: (0, 0)),
        pl.BlockSpec(w_cat.shape, lambda i: (0, 0)),
        pl.BlockSpec(w_dtT.shape, lambda i: (0, 0)),
        pl.BlockSpec(bias_rows.shape, lambda i: (0, 0)),
        pl.BlockSpec(alog_rows.shape, lambda i: (0, 0)),
    ]
    args = [xf, mod3, mod3, norm1, w_cat, w_dtT, bias_rows, alog_rows]
    if rope:
        in_specs += [pl.BlockSpec((rows, LANES), lambda i: (i % seq_blocks, 0))] * 2
        args += list(rope_tabs)
    out_specs, out_shapes = [], []
    for kind, _, width in segs:
        if kind == "dtT":
            for _ in range(3):
                out_specs.append(pl.BlockSpec((rows // CHUNK, 2 * SSD_HEADS, CHUNK), lambda i: (i, 0, 0)))
                out_shapes.append(jax.ShapeDtypeStruct((t // CHUNK, 2 * SSD_HEADS, CHUNK), F32))
        else:
            dt = BF16 if kind in ("q", "k", "v") else F32
            out_specs.append(pl.BlockSpec((rows, width), lambda i: (i, 0)))
            out_shapes.append(jax.ShapeDtypeStruct((t, width), dt))
    return pl.pallas_call(
        functools.partial(_inproj_kernel, segs=tuple(segs), rope=rope),
        grid=(t // rows,),
        in_specs=in_specs,
        out_specs=out_specs,
        out_shape=out_shapes,
        name="inproj_rope" if rope else "inproj_ctx",
    )(*args)


_XB_COLS = SSD_INNER + SSD_GROUPS * SSD_STATE
_Y_OFF = CONV_DIM - _XB_COLS


def _ssd_kernel(*refs, need_y, nsteps, cps):
    (xbc_ref, prev_ref, next_ref, dt_ref, pre_ref, suf_ref, z_ref, cw_ref, cb_ref,
     dskip_ref, gain_ref, exp_ref, h0f_ref, h0b_ref) = refs[:14]
    if need_y:
        y_ref, hf_out, hb_out = refs[14:17]
        rest = refs[17:]
    else:
        y_ref = None
        hf_out, hb_out = refs[14:16]
        rest = refs[16:]
    cwin, u_ref, hf, hb, hbs, ust = rest

    sw = pl.program_id(1)
    s = pl.program_id(2)
    bi = jnp.where(sw == 0, nsteps - 1 - s, s)
    rows = xbc_ref.shape[1]
    h = SSD_HEADS

    @pl.when((sw == 0) & (s == 0))
    def _():
        hb[...] = h0b_ref[0]

    @pl.when((sw == 1) & (s == 0))
    def _():
        hf[...] = h0f_ref[0]

    ri = lax.broadcasted_iota(jnp.int32, (CHUNK, CHUNK), 0)
    ci = lax.broadcasted_iota(jnp.int32, (CHUNK, CHUNK), 1)
    lane128 = lax.broadcasted_iota(jnp.int32, (CHUNK, LANES), 1)
    lo_mask = lane128 < SSD_HEAD_DIM

    def conv_silu(c, c0, c1, out_ref):
        r0 = pl.multiple_of(c * CHUNK, CHUNK)
        above = xbc_ref[0, pl.ds(pl.multiple_of(jnp.maximum(r0 - SUBLANES, 0), SUBLANES), SUBLANES), c0:c1]
        halo_a = jnp.where(bi > 0, prev_ref[0, :, c0:c1], 0.0)
        cwin[0:SUBLANES, c0:c1] = jnp.where(c > 0, above, halo_a)
        cwin[SUBLANES:SUBLANES + CHUNK, c0:c1] = xbc_ref[0, pl.ds(r0, CHUNK), c0:c1]
        below = xbc_ref[0, pl.ds(pl.multiple_of(jnp.minimum(r0 + CHUNK, rows - SUBLANES), SUBLANES), SUBLANES), c0:c1]
        halo_b = jnp.where(bi < nsteps - 1, next_ref[0, :, c0:c1], 0.0)
        cwin[SUBLANES + CHUNK:, c0:c1] = jnp.where(c < cps - 1, below, halo_b)
        for j0 in range(c0, c1, LANES):
            acc = jnp.broadcast_to(cb_ref[:, j0:j0 + LANES], (CHUNK, LANES))
            for k in range(CONV_WIDTH):
                off = SUBLANES - CONV_WIDTH // 2 + k
                acc = acc + cwin[off:off + CHUNK, j0:j0 + LANES] * cw_ref[k:k + 1, j0:j0 + LANES]
            out_ref[:, j0 - c0:j0 - c0 + LANES] = _silu(acc)

    def dt_rows(c):
        return dt_ref[c], pre_ref[c], suf_ref[c]

    def state_update(state_ref, xb, w_rows, tot_col):
        dec = jnp.exp(jnp.sum(tot_col * exp_ref[...], axis=0, keepdims=True))
        for g in range(SSD_GROUPS):
            bm = xb[:, SSD_INNER + g * SSD_STATE:SSD_INNER + (g + 1) * SSD_STATE]
            bt = bm.T
            for pr in range(2):
                h0 = g * 4 + pr * 2
                c0 = h0 * SSD_HEAD_DIM
                xp = xb[:, c0:c0 + LANES]
                rhs = jnp.concatenate([jnp.where(lo_mask, xp, 0.0), jnp.where(lo_mask, 0.0, xp)],
                                      axis=0).astype(BF16)
                lhs = jnp.concatenate([bt * w_rows[h0:h0 + 1, :], bt * w_rows[h0 + 1:h0 + 2, :]],
                                      axis=1).astype(BF16)
                sres = jnp.dot(lhs, rhs, preferred_element_type=F32)
                state_ref[:, c0:c0 + LANES] = state_ref[:, c0:c0 + LANES] * dec[:, c0:c0 + LANES] + sres

    def col_forms(rows16):
        padded = jnp.concatenate([rows16, jnp.zeros((CHUNK - 2 * h, LANES), F32)], axis=0)
        return padded.T

    def backward_chunk(k, carry):
        c = cps - 1 - k
        gc = bi * cps + c
        xb = ust.at[gc]
        conv_silu(c, 0, _XB_COLS, xb)
        dt, _, suf = dt_rows(c)
        hbs[gc] = hb[...].astype(BF16)
        s0 = suf[:, 0:1]
        w_rows = jnp.exp(s0 - (suf - jnp.log(dt)))
        state_update(hb, xb, w_rows[h:2 * h, :], s0[h:2 * h, :])
        return carry

    def forward_chunk(c, carry):
        gc = bi * cps + c
        r0 = pl.multiple_of(c * CHUNK, CHUNK)
        xb = ust.at[gc]
        conv_silu(c, _XB_COLS, CONV_DIM, u_ref)
        dt, pre, suf = dt_rows(c)
        rowsf = jnp.concatenate([pre[0:h, :], suf[h:2 * h, :]], axis=0)
        colsf = col_forms(rowsf)
        ecol = jnp.exp(colsf)
        rowsd = rowsf - jnp.log(dt)
        for g in range(SSD_GROUPS):
            bm = xb[:, SSD_INNER + g * SSD_STATE:SSD_INNER + (g + 1) * SSD_STATE]
            cm = u_ref[:, g * SSD_STATE:(g + 1) * SSD_STATE]
            cmb = cm.astype(BF16)
            cbm = lax.dot_general(cmb, bm.astype(BF16), (((1,), (1,)), ((), ())),
                                  preferred_element_type=F32)
            for pr in range(2):
                h0 = g * 4 + pr * 2
                c0 = h0 * SSD_HEAD_DIM
                lhs_parts = []
                for hh in (h0, h0 + 1):
                    segf = colsf[:, hh:hh + 1] - rowsd[hh:hh + 1, :]
                    gf = jnp.exp(jnp.where(ci <= ri, segf, NEG))
                    segb = colsf[:, h + hh:h + hh + 1] - rowsd[h + hh:h + hh + 1, :]
                    gb = jnp.exp(jnp.where(ci >= ri, segb, NEG))
                    lhs_parts.append((cbm * (gf + gb)).astype(BF16))
                for hh in (h0, h0 + 1):
                    lhs_parts.append((cm * ecol[:, hh:hh + 1]).astype(BF16))
                for hh in (h0, h0 + 1):
                    lhs_parts.append((cm * ecol[:, h + hh:h + hh + 1]).astype(BF16))
                lhs = jnp.concatenate(lhs_parts, axis=1)
                xp = xb[:, c0:c0 + LANES]
                sf = hf[:, c0:c0 + LANES]
                sb = hbs[gc, :, c0:c0 + LANES].astype(F32)
                rhs = jnp.concatenate(
                    [jnp.where(lo_mask, xp, 0.0), jnp.where(lo_mask, 0.0, xp),
                     jnp.where(lo_mask, sf, 0.0), jnp.where(lo_mask, 0.0, sf),
                     jnp.where(lo_mask, sb, 0.0), jnp.where(lo_mask, 0.0, sb)], axis=0).astype(BF16)
                ypair = jnp.dot(lhs, rhs, preferred_element_type=F32)
                ypair = ypair + dskip_ref[:, c0:c0 + LANES] * xp
                zz = z_ref[0, pl.ds(r0, CHUNK), c0:c0 + LANES]
                u_ref[:, _Y_OFF + c0:_Y_OFF + c0 + LANES] = ypair * _silu(zz)
        last = pre[:, LANES - 1:LANES]
        w_rows = jnp.exp(last - rowsd)
        state_update(hf, xb, w_rows[0:h, :], last[0:h, :])
        if need_y:
            yv = u_ref[:, _Y_OFF:_Y_OFF + SSD_INNER]
            ms = jnp.mean(yv * yv, axis=-1, keepdims=True)
            y_ref[0, pl.ds(r0, CHUNK), :] = (yv * lax.rsqrt(ms + EPS) * gain_ref[...]).astype(y_ref.dtype)
        return carry

    @pl.when(sw == 0)
    def _():
        lax.fori_loop(0, cps, backward_chunk, 0, unroll=2)

    @pl.when(sw == 1)
    def _():
        lax.fori_loop(0, cps, forward_chunk, 0, unroll=2)

    @pl.when((sw == 0) & (s == nsteps - 1))
    def _():
        hb_out[0] = hb[...]

    @pl.when((sw == 1) & (s == nsteps - 1))
    def _():
        hf_out[0] = hf[...]


def _ssd(xbc, dts, z, consts, h0f, h0b, *, rows, need_y):
    bt, seq, _ = xbc.shape
    nsteps = seq // rows
    cps = rows // CHUNK
    hb8 = rows // SUBLANES
    nb8 = seq // SUBLANES
    cw, cb, dskip, gain, expm = consts
    dt_spec = pl.BlockSpec((cps, 2 * SSD_HEADS, CHUNK), lambda b, sw, s: (b * nsteps + blk(b, sw, s), 0, 0))

    def blk(b, sw, s):
        return jnp.where(sw == 0, nsteps - 1 - s, s)

    def full(a):
        return pl.BlockSpec(a.shape, lambda b, sw, s: (0,) * a.ndim)

    in_specs = [
        pl.BlockSpec((1, rows, CONV_DIM), lambda b, sw, s: (b, blk(b, sw, s), 0)),
        pl.BlockSpec((1, SUBLANES, CONV_DIM), lambda b, sw, s: (b, jnp.maximum(blk(b, sw, s) * hb8 - 1, 0), 0)),
        pl.BlockSpec((1, SUBLANES, CONV_DIM),
                     lambda b, sw, s: (b, jnp.minimum((blk(b, sw, s) + 1) * hb8, nb8 - 1), 0)),
        dt_spec, dt_spec, dt_spec,
        pl.BlockSpec((1, rows, SSD_INNER), lambda b, sw, s: (b, jnp.where(sw == 0, 0, s), 0)),
        full(cw), full(cb), full(dskip), full(gain), full(expm),
        pl.BlockSpec((1, SSD_STATE, SSD_INNER), lambda b, sw, s: (b, 0, 0)),
        pl.BlockSpec((1, SSD_STATE, SSD_INNER), lambda b, sw, s: (b, 0, 0)),
    ]
    st_spec = pl.BlockSpec((1, SSD_STATE, SSD_INNER), lambda b, sw, s: (b, 0, 0))
    st_shape = jax.ShapeDtypeStruct((bt, SSD_STATE, SSD_INNER), F32)
    out_specs, out_shapes = [st_spec, st_spec], [st_shape, st_shape]
    if need_y:
        out_specs = [pl.BlockSpec((1, rows, SSD_INNER), lambda b, sw, s: (b, jnp.where(sw == 0, 0, s), 0))] + out_specs
        out_shapes = [jax.ShapeDtypeStruct((bt, seq, SSD_INNER), BF16)] + out_shapes
    scratch = [
        pltpu.VMEM((CHUNK + 2 * SUBLANES, CONV_DIM), F32),
        pltpu.VMEM((CHUNK, _Y_OFF + SSD_INNER), F32),
        pltpu.VMEM((SSD_STATE, SSD_INNER), F32),
        pltpu.VMEM((SSD_STATE, SSD_INNER), F32),
        pltpu.VMEM((seq // CHUNK, SSD_STATE, SSD_INNER), BF16),
        pltpu.VMEM((seq // CHUNK, CHUNK, _XB_COLS), F32),
    ]
    return pl.pallas_call(
        functools.partial(_ssd_kernel, need_y=need_y, nsteps=nsteps, cps=cps),
        grid=(bt, 2, nsteps),
        in_specs=in_specs,
        out_specs=out_specs,
        out_shape=out_shapes,
        scratch_shapes=scratch,
        compiler_params=pltpu.CompilerParams(dimension_semantics=("arbitrary", "arbitrary", "arbitrary")),
        name="ssd_y" if need_y else "ssd_ctx",
    )(xbc, xbc, xbc, *dts, z, cw, cb, dskip, gain, expm, h0f, h0b)


def _attn_kernel(sink_ref, q_ref, kp_ref, kc_ref, kn_ref, vp_ref, vc_ref, vn_ref, kx_ref, vx_ref, o_ref, *, nsteps):
    m = pl.program_id(1)
    lane = lax.broadcasted_iota(jnp.int32, (1, LANES), 1)
    lo = lane < HEAD_DIM
    rq = Q_PER_KV * WINDOW
    per = ROWS_ATT // WINDOW
    qi = lax.broadcasted_iota(jnp.int32, (rq, WINDOW), 0) & (WINDOW - 1)
    kj = lax.broadcasted_iota(jnp.int32, (rq, WINDOW), 1)
    rblk = lax.shift_right_logical(lax.broadcasted_iota(jnp.int32, (rq, 1), 0), int(math.log2(WINDOW)))
    zero = jnp.zeros((), BF16)
    band_prev = jnp.where(kj >= qi, 0.0, NEG)
    band_next = jnp.where(kj <= qi, 0.0, NEG)

    kpieces = [kp_ref[0]] + [kc_ref[0, w * WINDOW:(w + 1) * WINDOW] for w in range(per)] + [kn_ref[0]]
    vpieces = [vp_ref[0]] + [vc_ref[0, w * WINDOW:(w + 1) * WINDOW] for w in range(per)] + [vn_ref[0]]
    kx = kx_ref[0]
    vx = vx_ref[0]
    nt = (((1,), (1,)), ((), ()))

    for sb in range(per):
        qsb = q_ref[0, sb * WINDOW:(sb + 1) * WINDOW, :]
        mask_prev = band_prev if sb > 0 else jnp.where(m > 0, band_prev, NEG)
        mask_next = band_next if sb < per - 1 else jnp.where(m < nsteps - 1, band_next, NEG)
        outs = [None] * Q_PER_KV
        kall = jnp.concatenate([kpieces[sb], kpieces[sb + 1], kpieces[sb + 2], kx], axis=0)
        vall = jnp.concatenate([vpieces[sb], vpieces[sb + 1], vpieces[sb + 2], vx], axis=0)
        n_t = kall.shape[0] // LANES
        for kv in range(ATTN_KV_HEADS):
            sel = lo if kv == 0 else jnp.logical_not(lo)
            qs = jnp.concatenate(
                [jnp.where(sel, qsb[:, j * LANES:(j + 1) * LANES], zero) for j in range(Q_PER_KV)], axis=0)
            s_all = lax.dot_general(qs, kall, nt, preferred_element_type=F32)
            tiles = [s_all[:, t * LANES:(t + 1) * LANES] for t in range(n_t)]
            tiles[0] = tiles[0] + mask_prev
            tiles[2] = tiles[2] + mask_next
            sink = jnp.zeros((rq, 1), F32)
            for j in range(Q_PER_KV):
                sink = jnp.where(rblk == j, sink_ref[kv * Q_PER_KV + j], sink)
            tile_max = tiles[0]
            for t in range(1, n_t):
                tile_max = jnp.maximum(tile_max, tiles[t])
            mx = jnp.maximum(jnp.max(tile_max, axis=-1, keepdims=True), sink)
            ps = [jnp.exp2(tl - mx) for tl in tiles]
            tile_sum = ps[0]
            for t in range(1, n_t):
                tile_sum = tile_sum + ps[t]
            den = jnp.sum(tile_sum, axis=-1, keepdims=True) + jnp.exp2(sink - mx)
            p_all = jnp.concatenate([p.astype(BF16) for p in ps], axis=1)
            acc = jnp.dot(p_all, jnp.where(sel, vall, zero), preferred_element_type=F32)
            acc = acc * (1.0 / den)
            for j in range(Q_PER_KV):
                part = acc[j * WINDOW:(j + 1) * WINDOW, :]
                outs[j] = part if outs[j] is None else outs[j] + part
        for j in range(Q_PER_KV):
            o_ref[0, sb * WINDOW:(sb + 1) * WINDOW, j * LANES:(j + 1) * LANES] = outs[j].astype(o_ref.dtype)


def _attention(sinks, q, k, v, kx, vx):
    bt, seq, _ = q.shape
    nsteps = seq // ROWS_ATT
    per = ROWS_ATT // WINDOW
    nb = seq // WINDOW
    n_ctx = kx.shape[1]

    def prev(b, m, s):
        return (b, jnp.maximum(m * per - 1, 0), 0)

    def cur(b, m, s):
        return (b, m, 0)

    def nxt(b, m, s):
        return (b, jnp.minimum((m + 1) * per, nb - 1), 0)

    small = (1, WINDOW, KV_WIDTH)
    big = (1, ROWS_ATT, KV_WIDTH)
    grid_spec = pltpu.PrefetchScalarGridSpec(
        num_scalar_prefetch=1,
        grid=(bt, nsteps),
        in_specs=[
            pl.BlockSpec((1, ROWS_ATT, ATTN_WIDTH), cur),
            pl.BlockSpec(small, prev), pl.BlockSpec(big, cur), pl.BlockSpec(small, nxt),
            pl.BlockSpec(small, prev), pl.BlockSpec(big, cur), pl.BlockSpec(small, nxt),
            pl.BlockSpec((1, n_ctx, KV_WIDTH), lambda b, m, s: (b, 0, 0)),
            pl.BlockSpec((1, n_ctx, KV_WIDTH), lambda b, m, s: (b, 0, 0)),
        ],
        out_specs=pl.BlockSpec((1, ROWS_ATT, ATTN_WIDTH), cur),
    )
    return pl.pallas_call(
        functools.partial(_attn_kernel, nsteps=nsteps),
        grid_spec=grid_spec,
        out_shape=jax.ShapeDtypeStruct((bt, seq, ATTN_WIDTH), BF16),
        name="attn",
    )(sinks, q, k, k, k, v, v, v, kx, vx)


def _outproj_kernel(ssd_ref, att_ref, x_ref, g1_ref, sh_ref, sc_ref, n2_ref, wa_ref, wb_ref, wr_ref, br_ref,
                    tri_ref, x1_ref, h2_ref, ids_ref, wcol_ref, cnt_ref, carry, *, nsteps):
    i = pl.program_id(0)
    rows = x_ref.shape[0]

    @pl.when(i == 0)
    def _():
        carry[...] = jnp.zeros_like(carry)

    acc = jnp.dot(ssd_ref[...], wa_ref[...], preferred_element_type=F32)
    acc += jnp.dot(att_ref[...], wb_ref[...], preferred_element_type=F32)
    x1 = x_ref[...] + g1_ref[0] * acc
    x1_ref[...] = x1
    ms = jnp.mean(x1 * x1, axis=-1, keepdims=True)
    h2 = (x1 * lax.rsqrt(ms + EPS)) * (n2_ref[...] * (1.0 + sc_ref[0])) + sh_ref[0]
    h2b = h2.astype(BF16)
    half = h2.shape[1] // 2
    lo_bits = lax.shift_right_logical(pltpu.bitcast(h2b[:, :half].astype(F32), jnp.uint32), jnp.uint32(16))
    hi_bits = pltpu.bitcast(h2b[:, half:].astype(F32), jnp.uint32) & jnp.uint32(0xFFFF0000)
    h2_ref[...] = hi_bits | lo_bits
    lt = lax.dot_general(wr_ref[...], h2b, (((1,), (1,)), ((), ())),
                         preferred_element_type=F32) + br_ref[...]
    row = lax.broadcasted_iota(jnp.int32, lt.shape, 0).astype(F32)
    big_i = float(ROUTE_ROWS)
    gl = jnp.where(row < N_GROUPS, lt, NEG)
    gmax = jnp.max(gl, axis=0, keepdims=True)
    gidx = jnp.min(jnp.where(gl == gmax, row, big_i), axis=0, keepdims=True)
    g_w = 1.0 / jnp.sum(jnp.exp(gl - gmax), axis=0, keepdims=True)
    lo = N_GROUPS + EXPERTS_PER_GROUP * gidx
    el = jnp.where((row >= lo) & (row < lo + EXPERTS_PER_GROUP), lt, NEG)
    m1 = jnp.max(el, axis=0, keepdims=True)
    i1 = jnp.min(jnp.where(el == m1, row, big_i), axis=0, keepdims=True)
    el2 = jnp.where(row == i1, NEG, el)
    m2 = jnp.max(el2, axis=0, keepdims=True)
    i2 = jnp.min(jnp.where(el2 == m2, row, big_i), axis=0, keepdims=True)
    r = jnp.exp(m2 - m1)
    w1 = g_w / (1.0 + r)
    w2 = g_w * r / (1.0 + r)
    oh1 = row == i1
    oh2 = row == i2
    cnt = jnp.where(oh1 | oh2, 1.0, 0.0)
    prefix = jnp.dot(cnt.astype(BF16), tri_ref[...], preferred_element_type=F32)
    base = carry[...] + prefix
    rank1 = jnp.sum(jnp.where(oh1, base, 0.0), axis=0, keepdims=True)
    rank2 = jnp.sum(jnp.where(oh2, base, 0.0), axis=0, keepdims=True)
    carry[...] = carry[...] + jnp.sum(cnt, axis=1, keepdims=True)
    r8 = lax.broadcasted_iota(jnp.int32, (SUBLANES, rows), 0)
    code1 = (i1 - N_GROUPS) * float(1 << RANK_BITS) + rank1
    code2 = (i2 - N_GROUPS) * float(1 << RANK_BITS) + rank2
    ids_ref[...] = jnp.where(r8 == 0, code1, jnp.where(r8 == 1, code2, 0.0)).astype(jnp.int32)
    r128 = lax.broadcasted_iota(jnp.int32, (LANES, rows), 0)
    wfull = jnp.where(r128 == 0, w1, jnp.where(r128 == 1, w2, 0.0))
    for j in range(rows // LANES):
        wcol_ref[j * LANES:(j + 1) * LANES, :] = wfull[:, j * LANES:(j + 1) * LANES].T

    @pl.when(i == nsteps - 1)
    def _():
        cnt_ref[...] = carry[:, 0:LANES]


def _outproj(ssd, att, xf, mod3, norm2, wa, wb, wr, br, tri, *, rows, seq_blocks):
    t, d = xf.shape
    nsteps = t // rows

    def mod(col):
        return pl.BlockSpec((1, 1, d), lambda i: (i // seq_blocks, 0, col))

    def full(a):
        return pl.BlockSpec(a.shape, lambda i: (0,) * a.ndim)

    return pl.pallas_call(
        functools.partial(_outproj_kernel, nsteps=nsteps),
        grid=(nsteps,),
        in_specs=[pl.BlockSpec((rows, SSD_INNER), lambda i: (i, 0)),
                  pl.BlockSpec((rows, ATTN_WIDTH), lambda i: (i, 0)),
                  pl.BlockSpec((rows, d), lambda i: (i, 0)),
                  mod(2), mod(3), mod(4), full(norm2), full(wa), full(wb), full(wr), full(br), full(tri)],
        out_specs=[pl.BlockSpec((rows, d), lambda i: (i, 0)),
                   pl.BlockSpec((rows, d // 2), lambda i: (i, 0)),
                   pl.BlockSpec((SUBLANES, rows), lambda i: (0, i)),
                   pl.BlockSpec((rows, LANES), lambda i: (i, 0)),
                   pl.BlockSpec((ROUTE_ROWS, LANES), lambda i: (0, 0))],
        out_shape=[jax.ShapeDtypeStruct((t, d), F32),
                   jax.ShapeDtypeStruct((t, d // 2), jnp.uint32),
                   jax.ShapeDtypeStruct((SUBLANES, t), jnp.int32),
                   jax.ShapeDtypeStruct((t, LANES), F32),
                   jax.ShapeDtypeStruct((ROUTE_ROWS, LANES), F32)],
        scratch_shapes=[pltpu.VMEM((ROUTE_ROWS, rows), F32)],
        compiler_params=pltpu.CompilerParams(dimension_semantics=("arbitrary",)),
        name="outproj_router",
    )(ssd, att, xf, mod3, mod3, mod3, norm2, wa, wb, wr, br, tri)


def _expert_kernel(blk_e_ref, nused_ref, pstart_ref, counts_ref, eord_ref, enext_ref, dst_ref,
                   h2_hbm, wg_hbm, wu_hbm, wd_hbm, y_hbm,
                   h2v, xg, ystage, wgs, wus, wds, wgub, wdb, tok_ref, sem, wsem, ysem, *, n_tok, nblocks):
    nused = nused_ref[0]

    def h2_copy():
        return pltpu.make_async_copy(h2_hbm, h2v, sem.at[0])

    def y_copy(blk, sl):
        row0 = blk * MOE_BLK if isinstance(blk, int) else pl.multiple_of(blk * MOE_BLK, MOE_BLK)
        return pltpu.make_async_copy(ystage.at[sl], y_hbm.at[pl.ds(row0, MOE_BLK)], ysem.at[sl])

    def weight_copies(e, sl):
        return (pltpu.make_async_copy(wg_hbm.at[e], wgs.at[sl], wsem.at[sl, 0]),
                pltpu.make_async_copy(wu_hbm.at[e], wus.at[sl], wsem.at[sl, 1]),
                pltpu.make_async_copy(wd_hbm.at[e], wds.at[sl], wsem.at[sl, 2]))

    def gather_rows(blk, sl):
        for r in range(MOE_BLK):
            xg[sl, pl.ds(r, 1), :] = h2v[pl.ds(tok_ref[blk * MOE_BLK + r], 1), :]

    h2_copy().start()
    for cp in weight_copies(blk_e_ref[0], 0):
        cp.start()

    def pad_expert(e, carry):
        lo = pstart_ref[e] + counts_ref[e]
        hi = pstart_ref[e] + (counts_ref[e] + MOE_BLK - 1) // MOE_BLK * MOE_BLK

        def pad_slot(s, c):
            tok_ref[s] = 0
            return c
        return lax.fori_loop(lo, hi, pad_slot, carry)
    lax.fori_loop(0, N_EXPERTS, pad_expert, 0)

    def claim(t, carry):
        tok_ref[dst_ref[t]] = t
        tok_ref[dst_ref[n_tok + t]] = t
        return carry
    lax.fori_loop(0, n_tok, claim, 0, unroll=16)
    h2_copy().wait()
    gather_rows(0, 0)

    def block(b, carry):
        slot = b % 2
        e = blk_e_ref[b]
        e_prev = blk_e_ref[jnp.maximum(b - 1, 0)]

        @pl.when(b >= 2)
        def _():
            y_copy(0, slot).wait()

        @pl.when((b == 0) | (e != e_prev))
        def _():
            sl = eord_ref[e] % 2
            for cp in weight_copies(e, sl):
                cp.wait()
            wgub[:, 0:EXPERT_DIM] = wgs[sl].astype(BF16)
            wgub[:, EXPERT_DIM:] = wus[sl].astype(BF16)
            wdb[...] = wds[sl].astype(BF16)
            e_next = enext_ref[e]

            @pl.when(e_next >= 0)
            def _():
                for cp in weight_copies(e_next, 1 - sl):
                    cp.start()

        words = xg[slot]
        x_lo = pltpu.bitcast(words << jnp.uint32(16), F32).astype(BF16)
        x_hi = pltpu.bitcast(words & jnp.uint32(0xFFFF0000), F32).astype(BF16)
        gu = jnp.dot(jnp.concatenate([x_lo, x_hi], axis=1), wgub[...], preferred_element_type=F32)
        hmid = (_silu(gu[:, 0:EXPERT_DIM]) * gu[:, EXPERT_DIM:]).astype(BF16)
        ystage[slot] = jnp.dot(hmid, wdb[...], preferred_element_type=F32)
        gather_rows(jnp.minimum(b + 1, nused - 1), 1 - slot)
        y_copy(b, slot).start()
        return carry
    lax.fori_loop(0, nused, block, 0)

    @pl.when(nused >= 2)
    def _():
        y_copy(0, nused % 2).wait()

    y_copy(0, (nused - 1) % 2).wait()
    ystage[0] = jnp.zeros(ystage.shape[1:], F32)

    def zero_block(b, carry):
        cp = y_copy(b, 0)
        cp.start()
        cp.wait()
        return carry
    lax.fori_loop(nused, nblocks, zero_block, 0)


def _experts(blk_e, nused, pstart, counts, eord, enext, dst, h2p, w_gate, w_up, w_down, *, nblocks, n_tok):
    d = w_gate.shape[1]
    hbm = pl.BlockSpec(memory_space=pl.ANY)
    grid_spec = pltpu.PrefetchScalarGridSpec(
        num_scalar_prefetch=7,
        grid=(1,),
        in_specs=[hbm, hbm, hbm, hbm],
        out_specs=hbm,
        scratch_shapes=[
            pltpu.VMEM(h2p.shape, jnp.uint32),
            pltpu.VMEM((2, MOE_BLK, d // 2), jnp.uint32),
            pltpu.VMEM((2, MOE_BLK, d), F32),
            pltpu.VMEM((2, d, EXPERT_DIM), F32),
            pltpu.VMEM((2, d, EXPERT_DIM), F32),
            pltpu.VMEM((2, EXPERT_DIM, d), F32),
            pltpu.VMEM((d, 2 * EXPERT_DIM), BF16),
            pltpu.VMEM((EXPERT_DIM, d), BF16),
            pltpu.SMEM((nblocks * MOE_BLK,), jnp.int32),
            pltpu.SemaphoreType.DMA((1,)),
            pltpu.SemaphoreType.DMA((2, 3)),
            pltpu.SemaphoreType.DMA((2,)),
        ],
    )
    return pl.pallas_call(
        functools.partial(_expert_kernel, n_tok=n_tok, nblocks=nblocks),
        grid_spec=grid_spec,
        out_shape=jax.ShapeDtypeStruct((nblocks * MOE_BLK, d), F32),
        compiler_params=pltpu.CompilerParams(dimension_semantics=("arbitrary",)),
        name="experts",
    )(blk_e, nused, pstart, counts, eord, enext, dst, h2p, w_gate, w_up, w_down)


def _combine_kernel(dst_ref, y_hbm, x1_ref, wcol_ref, g2_ref, nf_ref, o_ref, ybuf, sem, *, nsteps, n_tok):
    i = pl.program_id(0)
    rows = x1_ref.shape[0]
    slot = i % 2

    def row_copy(src, r, k, sl):
        return pltpu.make_async_copy(y_hbm.at[pl.ds(src, 1)], ybuf.at[sl, k, pl.ds(r, 1)], sem.at[sl])

    def start_rows(step, sl):
        for r in range(rows):
            for k in range(2):
                row_copy(dst_ref[k * n_tok + step * rows + r], r, k, sl).start()

    def wait_rows(sl):
        for r in range(rows):
            for k in range(2):
                row_copy(0, r, k, sl).wait()

    @pl.when(i == 0)
    def _():
        start_rows(0, 0)

    @pl.when(i + 1 < nsteps)
    def _():
        start_rows(i + 1, 1 - slot)

    wait_rows(slot)
    moe = ybuf[slot, 0] * wcol_ref[:, 0:1] + ybuf[slot, 1] * wcol_ref[:, 1:2]
    x2 = x1_ref[...] + g2_ref[0] * moe
    ms = jnp.mean(x2 * x2, axis=-1, keepdims=True)
    o_ref[...] = x2 * lax.rsqrt(ms + EPS) * nf_ref[...]


def _combine(dst, y_buf, x1, wcol, mod3, norm_final, *, rows, seq_blocks):
    t, d = x1.shape
    nsteps = t // rows
    grid_spec = pltpu.PrefetchScalarGridSpec(
        num_scalar_prefetch=1,
        grid=(nsteps,),
        in_specs=[
            pl.BlockSpec(memory_space=pl.ANY),
            pl.BlockSpec((rows, d), lambda i, ds: (i, 0)),
            pl.BlockSpec((rows, LANES), lambda i, ds: (i, 0)),
            pl.BlockSpec((1, 1, d), lambda i, ds: (i // seq_blocks, 0, 5)),
            pl.BlockSpec((1, d), lambda i, ds: (0, 0)),
        ],
        out_specs=pl.BlockSpec((rows, d), lambda i, ds: (i, 0)),
        scratch_shapes=[pltpu.VMEM((2, 2, rows, d), F32), pltpu.SemaphoreType.DMA((2,))],
    )
    return pl.pallas_call(
        functools.partial(_combine_kernel, nsteps=nsteps, n_tok=t),
        grid_spec=grid_spec,
        out_shape=jax.ShapeDtypeStruct((t, d), F32),
        compiler_params=pltpu.CompilerParams(dimension_semantics=("arbitrary",)),
        name="combine",
    )(dst, y_buf, x1, wcol, mod3, norm_final)


def _rope_tables(seq):
    pos = np.arange(seq)
    n_freq = HEAD_DIM // 4
    inv = ROPE_BASE ** (-np.arange(n_freq, dtype=np.float32) / n_freq)
    ang = np.concatenate([(pos // GRID_W)[:, None] * inv, (pos % GRID_W)[:, None] * inv], axis=-1)
    ang = np.concatenate([ang, ang, ang, ang], axis=-1).astype(np.float32)
    return jnp.asarray(np.cos(ang), F32), jnp.asarray(np.sin(ang), F32)


def _head_expand_mat():
    m = np.zeros((SSD_HEADS, SSD_INNER), np.float32)
    for hh in range(SSD_HEADS):
        m[hh, hh * SSD_HEAD_DIM:(hh + 1) * SSD_HEAD_DIM] = 1.0
    return jnp.asarray(m)


def kernel(x, c, ctx, c_ctx, w_ada, b_ada, norm1, w_in, conv_w, conv_b, dt_bias, a_log, d_skip, ssd_norm,
           attn_sinks, w_out, norm2, w_group, b_group, w_expert, b_expert, w_gate, w_up, w_down, norm_final):
    return _pipeline(x, c, ctx, c_ctx, w_ada, b_ada, norm1, w_in, conv_w, conv_b, dt_bias, a_log, d_skip,
                     ssd_norm, attn_sinks, w_out, norm2, w_group, b_group, w_expert, b_expert, w_gate, w_up,
                     w_down, norm_final)["out"]


def _pipeline(x, c, ctx, c_ctx, w_ada, b_ada, norm1, w_in, conv_w, conv_b, dt_bias, a_log, d_skip, ssd_norm,
              attn_sinks, w_out, norm2, w_group, b_group, w_expert, b_expert, w_gate, w_up, w_down, norm_final):
    bt, seq, d = x.shape
    n_ctx = ctx.shape[1]
    t = bt * seq
    layer = 0
    assert w_ada.shape[0] == 1 and seq % ROWS_IN == 0 and seq % ROWS_OUT == 0 and n_ctx % CHUNK == 0

    cc = jnp.concatenate([c, c_ctx[None, :], jnp.zeros((SUBLANES - bt - 1, d), F32)], axis=0)
    mod = _ada(cc, w_ada[layer], b_ada[layer][None, :])
    mod3 = mod.reshape(SUBLANES, 1, 6 * d)

    w = w_in[layer]
    o_z, o_xbc, o_dt = 0, SSD_INNER, SSD_INNER + CONV_DIM
    o_q = o_dt + 2 * SSD_HEADS
    o_k, o_v = o_q + ATTN_WIDTH, o_q + ATTN_WIDTH + KV_WIDTH
    q_perm = np.concatenate([np.r_[j * HEAD_DIM:(j + 1) * HEAD_DIM, (j + 4) * HEAD_DIM:(j + 5) * HEAD_DIM]
                             for j in range(Q_PER_KV)])
    w_q = w[:, o_q:o_q + ATTN_WIDTH].reshape(d, ATTN_KV_HEADS, Q_PER_KV, HEAD_DIM).transpose(0, 2, 1, 3)
    w_q = w_q.reshape(d, ATTN_WIDTH)
    w_cat = jnp.concatenate([w[:, o_z:o_z + SSD_INNER], w[:, o_xbc:o_xbc + CONV_DIM], w_q,
                             w[:, o_k:o_k + KV_WIDTH], w[:, o_v:o_v + KV_WIDTH]], axis=1).astype(BF16)
    w_dtT = w[:, o_dt:o_dt + 2 * SSD_HEADS].T.astype(BF16)

    seq_blocks = seq // ROWS_IN
    cos, sin = _rope_tables(seq)
    bias_rows = jnp.broadcast_to(dt_bias[layer].reshape(2 * SSD_HEADS, 1), (2 * SSD_HEADS, LANES))
    alog_rows = jnp.broadcast_to(a_log[layer].reshape(2 * SSD_HEADS, 1), (2 * SSD_HEADS, LANES))
    segs_x = [("z", _COL_Z, SSD_INNER), ("xbc", _COL_XBC, CONV_DIM), ("dtT", 0, 0),
              ("q", _COL_Q, ATTN_WIDTH), ("k", _COL_K, KV_WIDTH), ("v", _COL_V, KV_WIDTH)]
    z_x, xbc_x, dt_x, pre_x, suf_x, q_x, k_x, v_x = _inproj(
        x.reshape(t, d), mod3, norm1[layer][None, :], w_cat, w_dtT, bias_rows, alog_rows, (cos, sin),
        rows=ROWS_IN, mod_row_fn=lambda i: i // seq_blocks, segs=segs_x, seq_blocks=seq_blocks)
    segs_c = [("xbc", _COL_XBC, CONV_DIM), ("dtT", 0, 0), ("k", _COL_K, KV_WIDTH), ("v", _COL_V, KV_WIDTH)]
    xbc_c, dt_c, pre_c, suf_c, k_c, v_c = _inproj(
        ctx.reshape(bt * n_ctx, d), mod3, norm1[layer][None, :], w_cat, w_dtT, bias_rows, alog_rows, None,
        rows=n_ctx, mod_row_fn=lambda i: bt, segs=segs_c, seq_blocks=1)

    cw = jnp.concatenate([conv_w[layer], jnp.zeros((SUBLANES - CONV_WIDTH, CONV_DIM), F32)], axis=0)
    cb = conv_b[layer][None, :]
    dskip_e = jnp.repeat(d_skip[layer], SSD_HEAD_DIM)[None, :]
    consts = (cw, cb, dskip_e, ssd_norm[layer][None, :], _head_expand_mat())

    zeros_state = jnp.zeros((bt, SSD_STATE, SSD_INNER), F32)
    z_dummy = jnp.zeros((bt, n_ctx, SSD_INNER), F32)
    h_cf, h_cb = _ssd(xbc_c.reshape(bt, n_ctx, CONV_DIM), (dt_c, pre_c, suf_c), z_dummy, consts,
                      zeros_state, zeros_state, rows=n_ctx, need_y=False)
    ssd_x, _, _ = _ssd(xbc_x.reshape(bt, seq, CONV_DIM), (dt_x, pre_x, suf_x), z_x.reshape(bt, seq, SSD_INNER),
                       consts, h_cf, h_cb, rows=ROWS_SSD, need_y=True)

    attn_x = _attention(attn_sinks[layer].astype(F32) * LOG2E,
                        q_x.reshape(bt, seq, ATTN_WIDTH), k_x.reshape(bt, seq, KV_WIDTH),
                        v_x.reshape(bt, seq, KV_WIDTH), k_c.reshape(bt, n_ctx, KV_WIDTH),
                        v_c.reshape(bt, n_ctx, KV_WIDTH))

    wo = w_out[layer]
    wa = wo[:SSD_INNER].astype(BF16)
    wb = wo[SSD_INNER:].reshape(ATTN_KV_HEADS, Q_PER_KV, HEAD_DIM, d).transpose(1, 0, 2, 3)
    wb = wb.reshape(ATTN_WIDTH, d).astype(BF16)
    n_pad = ROUTE_ROWS - N_GROUPS - N_EXPERTS
    wr = jnp.concatenate([w_group[layer], w_expert[layer], jnp.zeros((d, n_pad), F32)], axis=1).T.astype(BF16)
    br = jnp.concatenate([b_group[layer], b_expert[layer], jnp.zeros((n_pad,), F32)])
    br = jnp.broadcast_to(br[:, None], (ROUTE_ROWS, ROWS_OUT))
    tri = jnp.asarray(np.triu(np.ones((ROWS_OUT, ROWS_OUT), np.float32), 1), BF16)
    x1, h2, ids, wcol, cnt = _outproj(
        ssd_x.reshape(t, SSD_INNER), attn_x.reshape(t, ATTN_WIDTH), x.reshape(t, d), mod3, norm2[layer][None, :],
        wa, wb, wr, br, tri, rows=ROWS_OUT, seq_blocks=seq // ROWS_OUT)

    counts = cnt[N_GROUPS:N_GROUPS + N_EXPERTS, 0].astype(jnp.int32)
    padded = (counts + MOE_BLK - 1) // MOE_BLK * MOE_BLK
    ends = jnp.cumsum(padded)
    pstart = ends - padded
    nblocks = (2 * t) // MOE_BLK + N_EXPERTS
    blk_start = jnp.arange(nblocks, dtype=jnp.int32) * MOE_BLK
    eids = jnp.arange(N_EXPERTS, dtype=jnp.int32)
    active = counts > 0
    last_active = jnp.max(jnp.where(active, eids, 0))
    blk_e = jnp.minimum(jnp.sum((ends[None, :] <= blk_start[:, None]).astype(jnp.int32), axis=1), last_active)
    nused = (ends[-1] // MOE_BLK).astype(jnp.int32).reshape(1)
    eord = jnp.cumsum(active.astype(jnp.int32)) - active.astype(jnp.int32)
    later = active[None, :] & (eids[None, :] > eids[:, None])
    enext = jnp.min(jnp.where(later, eids[None, :], N_EXPERTS), axis=1)
    enext = jnp.where(enext == N_EXPERTS, -1, enext).astype(jnp.int32)
    code = ids[0:2].reshape(-1)
    e_id = lax.shift_right_logical(code, RANK_BITS)
    onehot = e_id[:, None] == jnp.arange(N_EXPERTS, dtype=jnp.int32)[None, :]
    dst = jnp.sum(jnp.where(onehot, pstart[None, :], 0), axis=1) + (code & ((1 << RANK_BITS) - 1))

    y_buf = _experts(blk_e, nused, pstart, counts, eord, enext, dst, h2, w_gate[layer], w_up[layer],
                     w_down[layer], nblocks=nblocks, n_tok=t)
    out = _combine(dst, y_buf, x1, wcol, mod3, norm_final[None, :], rows=ROWS_CMB, seq_blocks=seq // ROWS_CMB)
    return dict(out=out.reshape(bt, seq, d), mod=mod, z=z_x, xbc=xbc_x, dt=dt_x, q=q_x, k=k_x, v=v_x,
                k_c=k_c, v_c=v_c, h_cf=h_cf, h_cb=h_cb, ssd=ssd_x, attn=attn_x, x1=x1, h2=h2, ids=ids,
                wcol=wcol, cnt=cnt, y_buf=y_buf, pstart=pstart, q_perm=q_perm)
```

```python
import functools
import math

import numpy as np
import jax
import jax.numpy as jnp
from jax import lax
from jax.experimental import pallas as pl
from jax.experimental.pallas import tpu as pltpu

F32 = jnp.float32
BF16 = jnp.bfloat16

D_MODEL = 1024
GRID_W = 64
EPS = 1e-6
SSD_INNER = 512
SSD_HEAD_DIM = 64
SSD_HEADS = 8
SSD_GROUPS = 2
SSD_STATE = 128
CONV_WIDTH = 5
CONV_DIM = SSD_INNER + 2 * SSD_GROUPS * SSD_STATE
CHUNK = 128
ATTN_WIDTH = 512
HEAD_DIM = 64
ATTN_Q_HEADS = 8
ATTN_KV_HEADS = 2
Q_PER_KV = 4
KV_WIDTH = 128
WINDOW = 128
ROPE_BASE = 10000.0
N_GROUPS = 4
EXPERTS_PER_GROUP = 8
N_EXPERTS = 32
EXPERT_DIM = 512

LANES = 128
SUBLANES = 8
NEG = -1e30
LOG2E = math.log2(math.e)

ROWS_IN = 1024
ROWS_OUT = 1024
ROWS_SSD = 1024
ROWS_ATT = 1024
ROUTE_ROWS = 48
MOE_BLK = 256
RANK_BITS = 16
D_TILES = D_MODEL // LANES
ROWS_CMB = 128


def _silu(v):
    half = 0.5 * v
    return half + half * jnp.tanh(half)


def _softplus(v):
    return jnp.maximum(v, 0.0) + jnp.log(1.0 + jnp.exp(-jnp.abs(v)))


def _ada_kernel(c_ref, w_ref, b_ref, o_ref):
    a = _silu(c_ref[...])
    o_ref[...] = jnp.dot(a.astype(BF16), w_ref[...].astype(BF16),
                         preferred_element_type=F32) + b_ref[...]


def _ada(cc, w_ada, b_ada):
    d, n = w_ada.shape
    tn = 1536
    return pl.pallas_call(
        _ada_kernel,
        grid=(n // tn,),
        in_specs=[pl.BlockSpec((SUBLANES, d), lambda j: (0, 0)),
                  pl.BlockSpec((d, tn), lambda j: (0, j)),
                  pl.BlockSpec((1, tn), lambda j: (0, j))],
        out_specs=pl.BlockSpec((SUBLANES, tn), lambda j: (0, j)),
        out_shape=jax.ShapeDtypeStruct((SUBLANES, n), F32),
        name="ada",
    )(cc, w_ada, b_ada)


def _rope(t, cos, sin_signed, first_half):
    rot = jnp.where(first_half, pltpu.roll(t, LANES - HEAD_DIM // 2, 1), pltpu.roll(t, HEAD_DIM // 2, 1))
    return t * cos + rot * sin_signed


def _lane_cumsum(v, lane, reverse):
    k = 1
    while k < LANES:
        if reverse:
            v = v + jnp.where(lane < LANES - k, pltpu.roll(v, LANES - k, 1), 0.0)
        else:
            v = v + jnp.where(lane >= k, pltpu.roll(v, k, 1), 0.0)
        k *= 2
    return v


def _inproj_kernel(*refs, segs, rope):
    x_ref, sh_ref, sc_ref, g_ref, w_ref, wdt_ref, bias_ref, alog_ref = refs[:8]
    pos = 8
    if rope:
        cos_ref, sin_ref = refs[8:10]
        pos = 10
    outs = iter(refs[pos:])
    x = x_ref[...]
    ms = jnp.mean(x * x, axis=-1, keepdims=True)
    h = (x * lax.rsqrt(ms + EPS)) * (g_ref[...] * (1.0 + sc_ref[0])) + sh_ref[0]
    hb = h.astype(BF16)
    if rope:
        cos = cos_ref[...]
        lane = lax.broadcasted_iota(jnp.int32, cos.shape, 1)
        first_half = (lane & (HEAD_DIM - 1)) < HEAD_DIM // 2
        sin_s = jnp.where(first_half, -sin_ref[...], sin_ref[...])
    for kind, c0, width in segs:
        if kind == "dtT":
            dt_ref, pre_ref, suf_ref = next(outs), next(outs), next(outs)
            dtt = lax.dot_general(wdt_ref[...], hb, (((1,), (1,)), ((), ())), preferred_element_type=F32)
            lane16 = lax.broadcasted_iota(jnp.int32, (2 * SSD_HEADS, CHUNK), 1)
            neg_a = -jnp.exp(alog_ref[...])
            for j in range(dt_ref.shape[0]):
                dt = _softplus(dtt[:, j * CHUNK:(j + 1) * CHUNK] + bias_ref[...])
                a = dt * neg_a
                dt_ref[j] = dt
                pre_ref[j] = _lane_cumsum(a, lane16, False)
                suf_ref[j] = _lane_cumsum(a, lane16, True)
            continue
        o_ref = next(outs)
        for j0 in range(0, width, 512):
            wj = min(512, width - j0)
            acc = jnp.dot(hb, w_ref[:, c0 + j0:c0 + j0 + wj], preferred_element_type=F32)
            if rope and kind in ("q", "k"):
                scale = HEAD_DIM ** -0.5 * LOG2E if kind == "q" else 1.0
                for l0 in range(0, wj, LANES):
                    t = _rope(acc[:, l0:l0 + LANES], cos, sin_s, first_half)
                    o_ref[:, j0 + l0:j0 + l0 + LANES] = (t * scale).astype(o_ref.dtype)
            else:
                o_ref[:, j0:j0 + wj] = acc.astype(o_ref.dtype)


_COL_Z, _COL_XBC, _COL_Q, _COL_K, _COL_V = 0, 512, 1536, 2048, 2176
_W_COLS = 2304


def _inproj(xf, mod3, norm1, w_cat, w_dtT, bias_rows, alog_rows, rope_tabs, *, rows, mod_row_fn, segs, seq_blocks):
    t, d = xf.shape
    rope = rope_tabs is not None
    in_specs = [
        pl.BlockSpec((rows, d), lambda i: (i, 0)),
        pl.BlockSpec((1, 1, d), lambda i: (mod_row_fn(i), 0, 0)),
        pl.BlockSpec((1, 1, d), lambda i: (mod_row_fn(i), 0, 1)),
        pl.BlockSpec((1, d), lambda i: (0, 0)),
        pl.BlockSpec(w_cat.shape, lambda i: (0, 0)),
        pl.BlockSpec(w_dtT.shape, lambda i: (0, 0)),
        pl.BlockSpec(bias_rows.shape, lambda i: (0, 0)),
        pl.BlockSpec(alog_rows.shape, lambda i: (0, 0)),
    ]
    args = [xf, mod3, mod3, norm1, w_cat, w_dtT, bias_rows, alog_rows]
    if rope:
        in_specs += [pl.BlockSpec((rows, LANES), lambda i: (i % seq_blocks, 0))] * 2
        args += list(rope_tabs)
    out_specs, out_shapes = [], []
    for kind, _, width in segs:
        if kind == "dtT":
            for _ in range(3):
                out_specs.append(pl.BlockSpec((rows // CHUNK, 2 * SSD_HEADS, CHUNK), lambda i: (i, 0, 0)))
                out_shapes.append(jax.ShapeDtypeStruct((t // CHUNK, 2 * SSD_HEADS, CHUNK), F32))
        else:
            dt = BF16 if kind in ("q", "k", "v") else F32
            out_specs.append(pl.BlockSpec((rows, width), lambda i: (i, 0)))
            out_shapes.append(jax.ShapeDtypeStruct((t, width), dt))
    return pl.pallas_call(
        functools.partial(_inproj_kernel, segs=tuple(segs), rope=rope),
        grid=(t // rows,),
        in_specs=in_specs,
        out_specs=out_specs,
        out_shape=out_shapes,
        name="inproj_rope" if rope else "inproj_ctx",
    )(*args)


_XB_COLS = SSD_INNER + SSD_GROUPS * SSD_STATE
_Y_OFF = CONV_DIM - _XB_COLS


def _ssd_kernel(*refs, need_y, nsteps, cps):
    (xbc_ref, prev_ref, next_ref, dt_ref, pre_ref, suf_ref, z_ref, cw_ref, cb_ref,
     dskip_ref, gain_ref, exp_ref, h0f_ref, h0b_ref) = refs[:14]
    if need_y:
        y_ref, hf_out, hb_out = refs[14:17]
        rest = refs[17:]
    else:
        y_ref = None
        hf_out, hb_out = refs[14:16]
        rest = refs[16:]
    cwin, u_ref, hf, hb, hbs, ust = rest

    sw = pl.program_id(1)
    s = pl.program_id(2)
    bi = jnp.where(sw == 0, nsteps - 1 - s, s)
    rows = xbc_ref.shape[1]
    h = SSD_HEADS

    @pl.when((sw == 0) & (s == 0))
    def _():
        hb[...] = h0b_ref[0]

    @pl.when((sw == 1) & (s == 0))
    def _():
        hf[...] = h0f_ref[0]

    ri = lax.broadcasted_iota(jnp.int32, (CHUNK, CHUNK), 0)
    ci = lax.broadcasted_iota(jnp.int32, (CHUNK, CHUNK), 1)
    lane128 = lax.broadcasted_iota(jnp.int32, (CHUNK, LANES), 1)
    lo_mask = lane128 < SSD_HEAD_DIM

    def conv_silu(c, c0, c1, out_ref):
        r0 = pl.multiple_of(c * CHUNK, CHUNK)
        above = xbc_ref[0, pl.ds(pl.multiple_of(jnp.maximum(r0 - SUBLANES, 0), SUBLANES), SUBLANES), c0:c1]
        halo_a = jnp.where(bi > 0, prev_ref[0, :, c0:c1], 0.0)
        cwin[0:SUBLANES, c0:c1] = jnp.where(c > 0, above, halo_a)
        cwin[SUBLANES:SUBLANES + CHUNK, c0:c1] = xbc_ref[0, pl.ds(r0, CHUNK), c0:c1]
        below = xbc_ref[0, pl.ds(pl.multiple_of(jnp.minimum(r0 + CHUNK, rows - SUBLANES), SUBLANES), SUBLANES), c0:c1]
        halo_b = jnp.where(bi < nsteps - 1, next_ref[0, :, c0:c1], 0.0)
        cwin[SUBLANES + CHUNK:, c0:c1] = jnp.where(c < cps - 1, below, halo_b)
        for j0 in range(c0, c1, LANES):
            acc = jnp.broadcast_to(cb_ref[:, j0:j0 + LANES], (CHUNK, LANES))
            for k in range(CONV_WIDTH):
                off = SUBLANES - CONV_WIDTH // 2 + k
                acc = acc + cwin[off:off + CHUNK, j0:j0 + LANES] * cw_ref[k:k + 1, j0:j0 + LANES]
            out_ref[:, j0 - c0:j0 - c0 + LANES] = _silu(acc)

    def dt_rows(c):
        return dt_ref[c], pre_ref[c], suf_ref[c]

    def state_update(state_ref, xb, w_rows, tot_col):
        dec = jnp.exp(jnp.sum(tot_col * exp_ref[...], axis=0, keepdims=True))
        for g in range(SSD_GROUPS):
            bm = xb[:, SSD_INNER + g * SSD_STATE:SSD_INNER + (g + 1) * SSD_STATE]
            bt = bm.T
            for pr in range(2):
                h0 = g * 4 + pr * 2
                c0 = h0 * SSD_HEAD_DIM
                xp = xb[:, c0:c0 + LANES]
                rhs = jnp.concatenate([jnp.where(lo_mask, xp, 0.0), jnp.where(lo_mask, 0.0, xp)],
                                      axis=0).astype(BF16)
                lhs = jnp.concatenate([bt * w_rows[h0:h0 + 1, :], bt * w_rows[h0 + 1:h0 + 2, :]],
                                      axis=1).astype(BF16)
                sres = jnp.dot(lhs, rhs, preferred_element_type=F32)
                state_ref[:, c0:c0 + LANES] = state_ref[:, c0:c0 + LANES] * dec[:, c0:c0 + LANES] + sres

    def col_forms(rows16):
        padded = jnp.concatenate([rows16, jnp.zeros((CHUNK - 2 * h, LANES), F32)], axis=0)
        return padded.T

    def backward_chunk(k, carry):
        c = cps - 1 - k
        gc = bi * cps + c
        xb = ust.at[gc]
        conv_silu(c, 0, _XB_COLS, xb)
        dt, _, suf = dt_rows(c)
        hbs[gc] = hb[...].astype(BF16)
        s0 = suf[:, 0:1]
        w_rows = jnp.exp(s0 - (suf - jnp.log(dt)))
        state_update(hb, xb, w_rows[h:2 * h, :], s0[h:2 * h, :])
        return carry

    def forward_chunk(c, carry):
        gc = bi * cps + c
        r0 = pl.multiple_of(c * CHUNK, CHUNK)
        xb = ust.at[gc]
        conv_silu(c, _XB_COLS, CONV_DIM, u_ref)
        dt, pre, suf = dt_rows(c)
        rowsf = jnp.concatenate([pre[0:h, :], suf[h:2 * h, :]], axis=0)
        colsf = col_forms(rowsf)
        ecol = jnp.exp(colsf)
        rowsd = rowsf - jnp.log(dt)
        for g in range(SSD_GROUPS):
            bm = xb[:, SSD_INNER + g * SSD_STATE:SSD_INNER + (g + 1) * SSD_STATE]
            cm = u_ref[:, g * SSD_STATE:(g + 1) * SSD_STATE]
            cmb = cm.astype(BF16)
            cbm = lax.dot_general(cmb, bm.astype(BF16), (((1,), (1,)), ((), ())),
                                  preferred_element_type=F32)
            for pr in range(2):
                h0 = g * 4 + pr * 2
                c0 = h0 * SSD_HEAD_DIM
                lhs_parts = []
                for hh in (h0, h0 + 1):
                    segf = colsf[:, hh:hh + 1] - rowsd[hh:hh + 1, :]
                    gf = jnp.exp(jnp.where(ci <= ri, segf, NEG))
                    segb = colsf[:, h + hh:h + hh + 1] - rowsd[h + hh:h + hh + 1, :]
                    gb = jnp.exp(jnp.where(ci >= ri, segb, NEG))
                    lhs_parts.append((cbm * (gf + gb)).astype(BF16))
                for hh in (h0, h0 + 1):
                    lhs_parts.append((cm * ecol[:, hh:hh + 1]).astype(BF16))
                for hh in (h0, h0 + 1):
                    lhs_parts.append((cm * ecol[:, h + hh:h + hh + 1]).astype(BF16))
                lhs = jnp.concatenate(lhs_parts, axis=1)
                xp = xb[:, c0:c0 + LANES]
                sf = hf[:, c0:c0 + LANES]
                sb = hbs[gc, :, c0:c0 + LANES].astype(F32)
                rhs = jnp.concatenate(
                    [jnp.where(lo_mask, xp, 0.0), jnp.where(lo_mask, 0.0, xp),
                     jnp.where(lo_mask, sf, 0.0), jnp.where(lo_mask, 0.0, sf),
                     jnp.where(lo_mask, sb, 0.0), jnp.where(lo_mask, 0.0, sb)], axis=0).astype(BF16)
                ypair = jnp.dot(lhs, rhs, preferred_element_type=F32)
                ypair = ypair + dskip_ref[:, c0:c0 + LANES] * xp
                zz = z_ref[0, pl.ds(r0, CHUNK), c0:c0 + LANES]
                u_ref[:, _Y_OFF + c0:_Y_OFF + c0 + LANES] = ypair * _silu(zz)
        last = pre[:, LANES - 1:LANES]
        w_rows = jnp.exp(last - rowsd)
        state_update(hf, xb, w_rows[0:h, :], last[0:h, :])
        if need_y:
            yv = u_ref[:, _Y_OFF:_Y_OFF + SSD_INNER]
            ms = jnp.mean(yv * yv, axis=-1, keepdims=True)
            y_ref[0, pl.ds(r0, CHUNK), :] = (yv * lax.rsqrt(ms + EPS) * gain_ref[...]).astype(y_ref.dtype)
        return carry

    @pl.when(sw == 0)
    def _():
        lax.fori_loop(0, cps, backward_chunk, 0, unroll=True)

    @pl.when(sw == 1)
    def _():
        lax.fori_loop(0, cps, forward_chunk, 0, unroll=min(4, cps))

    @pl.when((sw == 0) & (s == nsteps - 1))
    def _():
        hb_out[0] = hb[...]

    @pl.when((sw == 1) & (s == nsteps - 1))
    def _():
        hf_out[0] = hf[...]


def _ssd(xbc, dts, z, consts, h0f, h0b, *, rows, need_y):
    bt, seq, _ = xbc.shape
    nsteps = seq // rows
    cps = rows // CHUNK
    hb8 = rows // SUBLANES
    nb8 = seq // SUBLANES
    cw, cb, dskip, gain, expm = consts
    dt_spec = pl.BlockSpec((cps, 2 * SSD_HEADS, CHUNK), lambda b, sw, s: (b * nsteps + blk(b, sw, s), 0, 0))

    def blk(b, sw, s):
        return jnp.where(sw == 0, nsteps - 1 - s, s)

    def full(a):
        return pl.BlockSpec(a.shape, lambda b, sw, s: (0,) * a.ndim)

    in_specs = [
        pl.BlockSpec((1, rows, CONV_DIM), lambda b, sw, s: (b, blk(b, sw, s), 0)),
        pl.BlockSpec((1, SUBLANES, CONV_DIM), lambda b, sw, s: (b, jnp.maximum(blk(b, sw, s) * hb8 - 1, 0), 0)),
        pl.BlockSpec((1, SUBLANES, CONV_DIM),
                     lambda b, sw, s: (b, jnp.minimum((blk(b, sw, s) + 1) * hb8, nb8 - 1), 0)),
        dt_spec, dt_spec, dt_spec,
        pl.BlockSpec((1, rows, SSD_INNER), lambda b, sw, s: (b, jnp.where(sw == 0, 0, s), 0)),
        full(cw), full(cb), full(dskip), full(gain), full(expm),
        pl.BlockSpec((1, SSD_STATE, SSD_INNER), lambda b, sw, s: (b, 0, 0)),
        pl.BlockSpec((1, SSD_STATE, SSD_INNER), lambda b, sw, s: (b, 0, 0)),
    ]
    st_spec = pl.BlockSpec((1, SSD_STATE, SSD_INNER), lambda b, sw, s: (b, 0, 0))
    st_shape = jax.ShapeDtypeStruct((bt, SSD_STATE, SSD_INNER), F32)
    out_specs, out_shapes = [st_spec, st_spec], [st_shape, st_shape]
    if need_y:
        out_specs = [pl.BlockSpec((1, rows, SSD_INNER), lambda b, sw, s: (b, jnp.where(sw == 0, 0, s), 0))] + out_specs
        out_shapes = [jax.ShapeDtypeStruct((bt, seq, SSD_INNER), BF16)] + out_shapes
    scratch = [
        pltpu.VMEM((CHUNK + 2 * SUBLANES, CONV_DIM), F32),
        pltpu.VMEM((CHUNK, _Y_OFF + SSD_INNER), F32),
        pltpu.VMEM((SSD_STATE, SSD_INNER), F32),
        pltpu.VMEM((SSD_STATE, SSD_INNER), F32),
        pltpu.VMEM((seq // CHUNK, SSD_STATE, SSD_INNER), BF16),
        pltpu.VMEM((seq // CHUNK, CHUNK, _XB_COLS), F32),
    ]
    return pl.pallas_call(
        functools.partial(_ssd_kernel, need_y=need_y, nsteps=nsteps, cps=cps),
        grid=(bt, 2, nsteps),
        in_specs=in_specs,
        out_specs=out_specs,
        out_shape=out_shapes,
        scratch_shapes=scratch,
        compiler_params=pltpu.CompilerParams(dimension_semantics=("arbitrary", "arbitrary", "arbitrary")),
        name="ssd_y" if need_y else "ssd_ctx",
    )(xbc, xbc, xbc, *dts, z, cw, cb, dskip, gain, expm, h0f, h0b)


def _attn_kernel(sink_ref, q_ref, kp_ref, kc_ref, kn_ref, vp_ref, vc_ref, vn_ref, kx_ref, vx_ref, o_ref, *, nsteps):
    m = pl.program_id(1)
    lane = lax.broadcasted_iota(jnp.int32, (1, LANES), 1)
    lo = lane < HEAD_DIM
    rq = Q_PER_KV * WINDOW
    per = ROWS_ATT // WINDOW
    qi = lax.broadcasted_iota(jnp.int32, (rq, WINDOW), 0) & (WINDOW - 1)
    kj = lax.broadcasted_iota(jnp.int32, (rq, WINDOW), 1)
    rblk = lax.shift_right_logical(lax.broadcasted_iota(jnp.int32, (rq, 1), 0), int(math.log2(WINDOW)))
    zero = jnp.zeros((), BF16)
    band_prev = jnp.where(kj >= qi, 0.0, NEG)
    band_next = jnp.where(kj <= qi, 0.0, NEG)

    kpieces = [kp_ref[0]] + [kc_ref[0, w * WINDOW:(w + 1) * WINDOW] for w in range(per)] + [kn_ref[0]]
    vpieces = [vp_ref[0]] + [vc_ref[0, w * WINDOW:(w + 1) * WINDOW] for w in range(per)] + [vn_ref[0]]
    kx = kx_ref[0]
    vx = vx_ref[0]
    nt = (((1,), (1,)), ((), ()))

    for sb in range(per):
        qsb = q_ref[0, sb * WINDOW:(sb + 1) * WINDOW, :]
        mask_prev = band_prev if sb > 0 else jnp.where(m > 0, band_prev, NEG)
        mask_next = band_next if sb < per - 1 else jnp.where(m < nsteps - 1, band_next, NEG)
        outs = [None] * Q_PER_KV
        kall = jnp.concatenate([kpieces[sb], kpieces[sb + 1], kpieces[sb + 2], kx], axis=0)
        vall = jnp.concatenate([vpieces[sb], vpieces[sb + 1], vpieces[sb + 2], vx], axis=0)
        n_t = kall.shape[0] // LANES
        for kv in range(ATTN_KV_HEADS):
            sel = lo if kv == 0 else jnp.logical_not(lo)
            qs = jnp.concatenate(
                [jnp.where(sel, qsb[:, j * LANES:(j + 1) * LANES], zero) for j in range(Q_PER_KV)], axis=0)
            s_all = lax.dot_general(qs, kall, nt, preferred_element_type=F32)
            tiles = [s_all[:, t * LANES:(t + 1) * LANES] for t in range(n_t)]
            tiles[0] = tiles[0] + mask_prev
            tiles[2] = tiles[2] + mask_next
            sink = jnp.zeros((rq, 1), F32)
            for j in range(Q_PER_KV):
                sink = jnp.where(rblk == j, sink_ref[kv * Q_PER_KV + j], sink)
            tile_max = tiles[0]
            for t in range(1, n_t):
                tile_max = jnp.maximum(tile_max, tiles[t])
            mx = jnp.maximum(jnp.max(tile_max, axis=-1, keepdims=True), sink)
            ps = [jnp.exp2(tl - mx) for tl in tiles]
            tile_sum = ps[0]
            for t in range(1, n_t):
                tile_sum = tile_sum + ps[t]
            den = jnp.sum(tile_sum, axis=-1, keepdims=True) + jnp.exp2(sink - mx)
            p_all = jnp.concatenate([p.astype(BF16) for p in ps], axis=1)
            acc = jnp.dot(p_all, jnp.where(sel, vall, zero), preferred_element_type=F32)
            acc = acc * (1.0 / den)
            for j in range(Q_PER_KV):
                part = acc[j * WINDOW:(j + 1) * WINDOW, :]
                outs[j] = part if outs[j] is None else outs[j] + part
        for j in range(Q_PER_KV):
            o_ref[0, sb * WINDOW:(sb + 1) * WINDOW, j * LANES:(j + 1) * LANES] = outs[j].astype(o_ref.dtype)


def _attention(sinks, q, k, v, kx, vx):
    bt, seq, _ = q.shape
    nsteps = seq // ROWS_ATT
    per = ROWS_ATT // WINDOW
    nb = seq // WINDOW
    n_ctx = kx.shape[1]

    def prev(b, m, s):
        return (b, jnp.maximum(m * per - 1, 0), 0)

    def cur(b, m, s):
        return (b, m, 0)

    def nxt(b, m, s):
        return (b, jnp.minimum((m + 1) * per, nb - 1), 0)

    small = (1, WINDOW, KV_WIDTH)
    big = (1, ROWS_ATT, KV_WIDTH)
    grid_spec = pltpu.PrefetchScalarGridSpec(
        num_scalar_prefetch=1,
        grid=(bt, nsteps),
        in_specs=[
            pl.BlockSpec((1, ROWS_ATT, ATTN_WIDTH), cur),
            pl.BlockSpec(small, prev), pl.BlockSpec(big, cur), pl.BlockSpec(small, nxt),
            pl.BlockSpec(small, prev), pl.BlockSpec(big, cur), pl.BlockSpec(small, nxt),
            pl.BlockSpec((1, n_ctx, KV_WIDTH), lambda b, m, s: (b, 0, 0)),
            pl.BlockSpec((1, n_ctx, KV_WIDTH), lambda b, m, s: (b, 0, 0)),
        ],
        out_specs=pl.BlockSpec((1, ROWS_ATT, ATTN_WIDTH), cur),
    )
    return pl.pallas_call(
        functools.partial(_attn_kernel, nsteps=nsteps),
        grid_spec=grid_spec,
        out_shape=jax.ShapeDtypeStruct((bt, seq, ATTN_WIDTH), BF16),
        name="attn",
    )(sinks, q, k, k, k, v, v, v, kx, vx)


def _outproj_kernel(ssd_ref, att_ref, x_ref, g1_ref, sh_ref, sc_ref, n2_ref, wa_ref, wb_ref, wr_ref, br_ref,
                    tri_ref, x1_ref, h2_ref, ids_ref, wcol_ref, cnt_ref, carry, *, nsteps):
    i = pl.program_id(0)
    rows = x_ref.shape[0]

    @pl.when(i == 0)
    def _():
        carry[...] = jnp.zeros_like(carry)

    acc = jnp.dot(ssd_ref[...], wa_ref[...], preferred_element_type=F32)
    acc += jnp.dot(att_ref[...], wb_ref[...], preferred_element_type=F32)
    x1 = x_ref[...] + g1_ref[0] * acc
    x1_ref[...] = x1
    ms = jnp.mean(x1 * x1, axis=-1, keepdims=True)
    h2 = (x1 * lax.rsqrt(ms + EPS)) * (n2_ref[...] * (1.0 + sc_ref[0])) + sh_ref[0]
    h2b = h2.astype(BF16)
    half = h2.shape[1] // 2
    lo_bits = lax.shift_right_logical(pltpu.bitcast(h2b[:, :half].astype(F32), jnp.uint32), jnp.uint32(16))
    hi_bits = pltpu.bitcast(h2b[:, half:].astype(F32), jnp.uint32) & jnp.uint32(0xFFFF0000)
    h2_ref[...] = hi_bits | lo_bits
    lt = lax.dot_general(wr_ref[...], h2b, (((1,), (1,)), ((), ())),
                         preferred_element_type=F32) + br_ref[...]
    row = lax.broadcasted_iota(jnp.int32, lt.shape, 0).astype(F32)
    big_i = float(ROUTE_ROWS)
    gl = jnp.where(row < N_GROUPS, lt, NEG)
    gmax = jnp.max(gl, axis=0, keepdims=True)
    gidx = jnp.min(jnp.where(gl == gmax, row, big_i), axis=0, keepdims=True)
    g_w = 1.0 / jnp.sum(jnp.exp(gl - gmax), axis=0, keepdims=True)
    lo = N_GROUPS + EXPERTS_PER_GROUP * gidx
    el = jnp.where((row >= lo) & (row < lo + EXPERTS_PER_GROUP), lt, NEG)
    m1 = jnp.max(el, axis=0, keepdims=True)
    i1 = jnp.min(jnp.where(el == m1, row, big_i), axis=0, keepdims=True)
    el2 = jnp.where(row == i1, NEG, el)
    m2 = jnp.max(el2, axis=0, keepdims=True)
    i2 = jnp.min(jnp.where(el2 == m2, row, big_i), axis=0, keepdims=True)
    r = jnp.exp(m2 - m1)
    w1 = g_w / (1.0 + r)
    w2 = g_w * r / (1.0 + r)
    oh1 = row == i1
    oh2 = row == i2
    cnt = jnp.where(oh1 | oh2, 1.0, 0.0)
    prefix = jnp.dot(cnt.astype(BF16), tri_ref[...], preferred_element_type=F32)
    base = carry[...] + prefix
    rank1 = jnp.sum(jnp.where(oh1, base, 0.0), axis=0, keepdims=True)
    rank2 = jnp.sum(jnp.where(oh2, base, 0.0), axis=0, keepdims=True)
    carry[...] = carry[...] + jnp.sum(cnt, axis=1, keepdims=True)
    r8 = lax.broadcasted_iota(jnp.int32, (SUBLANES, rows), 0)
    code1 = (i1 - N_GROUPS) * float(1 << RANK_BITS) + rank1
    code2 = (i2 - N_GROUPS) * float(1 << RANK_BITS) + rank2
    ids_ref[...] = jnp.where(r8 == 0, code1, jnp.where(r8 == 1, code2, 0.0)).astype(jnp.int32)
    r128 = lax.broadcasted_iota(jnp.int32, (LANES, rows), 0)
    wfull = jnp.where(r128 == 0, w1, jnp.where(r128 == 1, w2, 0.0))
    for j in range(rows // LANES):
        wcol_ref[j * LANES:(j + 1) * LANES, :] = wfull[:, j * LANES:(j + 1) * LANES].T

    @pl.when(i == nsteps - 1)
    def _():
        cnt_ref[...] = carry[:, 0:LANES]


def _outproj(ssd, att, xf, mod3, norm2, wa, wb, wr, br, tri, *, rows, seq_blocks):
    t, d = xf.shape
    nsteps = t // rows

    def mod(col):
        return pl.BlockSpec((1, 1, d), lambda i: (i // seq_blocks, 0, col))

    def full(a):
        return pl.BlockSpec(a.shape, lambda i: (0,) * a.ndim)

    return pl.pallas_call(
        functools.partial(_outproj_kernel, nsteps=nsteps),
        grid=(nsteps,),
        in_specs=[pl.BlockSpec((rows, SSD_INNER), lambda i: (i, 0)),
                  pl.BlockSpec((rows, ATTN_WIDTH), lambda i: (i, 0)),
                  pl.BlockSpec((rows, d), lambda i: (i, 0)),
                  mod(2), mod(3), mod(4), full(norm2), full(wa), full(wb), full(wr), full(br), full(tri)],
        out_specs=[pl.BlockSpec((rows, d), lambda i: (i, 0)),
                   pl.BlockSpec((rows, d // 2), lambda i: (i, 0)),
                   pl.BlockSpec((SUBLANES, rows), lambda i: (0, i)),
                   pl.BlockSpec((rows, LANES), lambda i: (i, 0)),
                   pl.BlockSpec((ROUTE_ROWS, LANES), lambda i: (0, 0))],
        out_shape=[jax.ShapeDtypeStruct((t, d), F32),
                   jax.ShapeDtypeStruct((t, d // 2), jnp.uint32),
                   jax.ShapeDtypeStruct((SUBLANES, t), jnp.int32),
                   jax.ShapeDtypeStruct((t, LANES), F32),
                   jax.ShapeDtypeStruct((ROUTE_ROWS, LANES), F32)],
        scratch_shapes=[pltpu.VMEM((ROUTE_ROWS, rows), F32)],
        compiler_params=pltpu.CompilerParams(dimension_semantics=("arbitrary",)),
        name="outproj_router",
    )(ssd, att, xf, mod3, mod3, mod3, norm2, wa, wb, wr, br, tri)


def _expert_kernel(blk_e_ref, nused_ref, pstart_ref, counts_ref, eord_ref, enext_ref, dst_ref,
                   h2_hbm, wg_hbm, wu_hbm, wd_hbm, y_hbm,
                   h2v, xg, ystage, wgs, wus, wds, wgub, wdb, tok_ref, sem, wsem, ysem, *, n_tok, nblocks):
    nused = nused_ref[0]

    def h2_copy():
        return pltpu.make_async_copy(h2_hbm, h2v, sem.at[0])

    def y_copy(blk, sl):
        row0 = blk * MOE_BLK if isinstance(blk, int) else pl.multiple_of(blk * MOE_BLK, MOE_BLK)
        return pltpu.make_async_copy(ystage.at[sl], y_hbm.at[pl.ds(row0, MOE_BLK)], ysem.at[sl])

    def weight_copies(e, sl):
        return (pltpu.make_async_copy(wg_hbm.at[e], wgs.at[sl], wsem.at[sl, 0]),
                pltpu.make_async_copy(wu_hbm.at[e], wus.at[sl], wsem.at[sl, 1]),
                pltpu.make_async_copy(wd_hbm.at[e], wds.at[sl], wsem.at[sl, 2]))

    def gather_rows(blk, sl):
        for r in range(MOE_BLK):
            xg[sl, pl.ds(r, 1), :] = h2v[pl.ds(tok_ref[blk * MOE_BLK + r], 1), :]

    h2_copy().start()
    for cp in weight_copies(blk_e_ref[0], 0):
        cp.start()

    def pad_expert(e, carry):
        lo = pstart_ref[e] + counts_ref[e]
        hi = pstart_ref[e] + (counts_ref[e] + MOE_BLK - 1) // MOE_BLK * MOE_BLK

        def pad_slot(s, c):
            tok_ref[s] = 0
            return c
        return lax.fori_loop(lo, hi, pad_slot, carry)
    lax.fori_loop(0, N_EXPERTS, pad_expert, 0)

    def claim(t, carry):
        tok_ref[dst_ref[t]] = t
        tok_ref[dst_ref[n_tok + t]] = t
        return carry
    lax.fori_loop(0, n_tok, claim, 0, unroll=16)
    h2_copy().wait()
    gather_rows(0, 0)

    def block(b, carry):
        slot = b % 2
        e = blk_e_ref[b]
        e_prev = blk_e_ref[jnp.maximum(b - 1, 0)]

        @pl.when(b >= 2)
        def _():
            y_copy(0, slot).wait()

        @pl.when((b == 0) | (e != e_prev))
        def _():
            sl = eord_ref[e] % 2
            for cp in weight_copies(e, sl):
                cp.wait()
            wgub[:, 0:EXPERT_DIM] = wgs[sl].astype(BF16)
            wgub[:, EXPERT_DIM:] = wus[sl].astype(BF16)
            wdb[...] = wds[sl].astype(BF16)
            e_next = enext_ref[e]

            @pl.when(e_next >= 0)
            def _():
                for cp in weight_copies(e_next, 1 - sl):
                    cp.start()

        words = xg[slot]
        x_lo = pltpu.bitcast(words << jnp.uint32(16), F32).astype(BF16)
        x_hi = pltpu.bitcast(words & jnp.uint32(0xFFFF0000), F32).astype(BF16)
        gu = jnp.dot(jnp.concatenate([x_lo, x_hi], axis=1), wgub[...], preferred_element_type=F32)
        hmid = (_silu(gu[:, 0:EXPERT_DIM]) * gu[:, EXPERT_DIM:]).astype(BF16)
        ystage[slot] = jnp.dot(hmid, wdb[...], preferred_element_type=F32)
        gather_rows(jnp.minimum(b + 1, nused - 1), 1 - slot)
        y_copy(b, slot).start()
        return carry
    lax.fori_loop(0, nused, block, 0)

    @pl.when(nused >= 2)
    def _():
        y_copy(0, nused % 2).wait()

    y_copy(0, (nused - 1) % 2).wait()
    ystage[0] = jnp.zeros(ystage.shape[1:], F32)

    def zero_block(b, carry):
        cp = y_copy(b, 0)
        cp.start()
        cp.wait()
        return carry
    lax.fori_loop(nused, nblocks, zero_block, 0)


def _experts(blk_e, nused, pstart, counts, eord, enext, dst, h2p, w_gate, w_up, w_down, *, nblocks, n_tok):
    d = w_gate.shape[1]
    hbm = pl.BlockSpec(memory_space=pl.ANY)
    grid_spec = pltpu.PrefetchScalarGridSpec(
        num_scalar_prefetch=7,
        grid=(1,),
        in_specs=[hbm, hbm, hbm, hbm],
        out_specs=hbm,
        scratch_shapes=[
            pltpu.VMEM(h2p.shape, jnp.uint32),
            pltpu.VMEM((2, MOE_BLK, d // 2), jnp.uint32),
            pltpu.VMEM((2, MOE_BLK, d), F32),
            pltpu.VMEM((2, d, EXPERT_DIM), F32),
            pltpu.VMEM((2, d, EXPERT_DIM), F32),
            pltpu.VMEM((2, EXPERT_DIM, d), F32),
            pltpu.VMEM((d, 2 * EXPERT_DIM), BF16),
            pltpu.VMEM((EXPERT_DIM, d), BF16),
            pltpu.SMEM((nblocks * MOE_BLK,), jnp.int32),
            pltpu.SemaphoreType.DMA((1,)),
            pltpu.SemaphoreType.DMA((2, 3)),
            pltpu.SemaphoreType.DMA((2,)),
        ],
    )
    return pl.pallas_call(
        functools.partial(_expert_kernel, n_tok=n_tok, nblocks=nblocks),
        grid_spec=grid_spec,
        out_shape=jax.ShapeDtypeStruct((nblocks * MOE_BLK, d), F32),
        compiler_params=pltpu.CompilerParams(dimension_semantics=("arbitrary",)),
        name="experts",
    )(blk_e, nused, pstart, counts, eord, enext, dst, h2p, w_gate, w_up, w_down)


def _combine_kernel(dst_ref, y_hbm, x1_ref, wcol_ref, g2_ref, nf_ref, o_ref, ybuf, sem, *, nsteps, n_tok):
    i = pl.program_id(0)
    rows = x1_ref.shape[0]
    slot = i % 2

    def row_copy(src, r, k, sl):
        return pltpu.make_async_copy(y_hbm.at[pl.ds(src, 1)], ybuf.at[sl, k, pl.ds(r, 1)], sem.at[sl])

    def start_rows(step, sl):
        for r in range(rows):
            for k in range(2):
                row_copy(dst_ref[k * n_tok + step * rows + r], r, k, sl).start()

    def wait_rows(sl):
        for r in range(rows):
            for k in range(2):
                row_copy(0, r, k, sl).wait()

    @pl.when(i == 0)
    def _():
        start_rows(0, 0)

    @pl.when(i + 1 < nsteps)
    def _():
        start_rows(i + 1, 1 - slot)

    wait_rows(slot)
    moe = ybuf[slot, 0] * wcol_ref[:, 0:1] + ybuf[slot, 1] * wcol_ref[:, 1:2]
    x2 = x1_ref[...] + g2_ref[0] * moe
    ms = jnp.mean(x2 * x2, axis=-1, keepdims=True)
    o_ref[...] = x2 * lax.rsqrt(ms + EPS) * nf_ref[...]


def _combine(dst, y_buf, x1, wcol, mod3, norm_final, *, rows, seq_blocks):
    t, d = x1.shape
    nsteps = t // rows
    grid_spec = pltpu.PrefetchScalarGridSpec(
        num_scalar_prefetch=1,
        grid=(nsteps,),
        in_specs=[
            pl.BlockSpec(memory_space=pl.ANY),
            pl.BlockSpec((rows, d), lambda i, ds: (i, 0)),
            pl.BlockSpec((rows, LANES), lambda i, ds: (i, 0)),
            pl.BlockSpec((1, 1, d), lambda i, ds: (i // seq_blocks, 0, 5)),
            pl.BlockSpec((1, d), lambda i, ds: (0, 0)),
        ],
        out_specs=pl.BlockSpec((rows, d), lambda i, ds: (i, 0)),
        scratch_shapes=[pltpu.VMEM((2, 2, rows, d), F32), pltpu.SemaphoreType.DMA((2,))],
    )
    return pl.pallas_call(
        functools.partial(_combine_kernel, nsteps=nsteps, n_tok=t),
        grid_spec=grid_spec,
        out_shape=jax.ShapeDtypeStruct((t, d), F32),
        compiler_params=pltpu.CompilerParams(dimension_semantics=("arbitrary",)),
        name="combine",
    )(dst, y_buf, x1, wcol, mod3, norm_final)


def _rope_tables(seq):
    pos = np.arange(seq)
    n_freq = HEAD_DIM // 4
    inv = ROPE_BASE ** (-np.arange(n_freq, dtype=np.float32) / n_freq)
    ang = np.concatenate([(pos // GRID_W)[:, None] * inv, (pos % GRID_W)[:, None] * inv], axis=-1)
    ang = np.concatenate([ang, ang, ang, ang], axis=-1).astype(np.float32)
    return jnp.asarray(np.cos(ang), F32), jnp.asarray(np.sin(ang), F32)


def _head_expand_mat():
    m = np.zeros((SSD_HEADS, SSD_INNER), np.float32)
    for hh in range(SSD_HEADS):
        m[hh, hh * SSD_HEAD_DIM:(hh + 1) * SSD_HEAD_DIM] = 1.0
    return jnp.asarray(m)


def kernel(x, c, ctx, c_ctx, w_ada, b_ada, norm1, w_in, conv_w, conv_b, dt_bias, a_log, d_skip, ssd_norm,
           attn_sinks, w_out, norm2, w_group, b_group, w_expert, b_expert, w_gate, w_up, w_down, norm_final):
    return _pipeline(x, c, ctx, c_ctx, w_ada, b_ada, norm1, w_in, conv_w, conv_b, dt_bias, a_log, d_skip,
                     ssd_norm, attn_sinks, w_out, norm2, w_group, b_group, w_expert, b_expert, w_gate, w_up,
                     w_down, norm_final)["out"]


def _pipeline(x, c, ctx, c_ctx, w_ada, b_ada, norm1, w_in, conv_w, conv_b, dt_bias, a_log, d_skip, ssd_norm,
              attn_sinks, w_out, norm2, w_group, b_group, w_expert, b_expert, w_gate, w_up, w_down, norm_final):
    bt, seq, d = x.shape
    n_ctx = ctx.shape[1]
    t = bt * seq
    layer = 0
    assert w_ada.shape[0] == 1 and seq % ROWS_IN == 0 and seq % ROWS_OUT == 0 and n_ctx % CHUNK == 0

    cc = jnp.concatenate([c, c_ctx[None, :], jnp.zeros((SUBLANES - bt - 1, d), F32)], axis=0)
    mod = _ada(cc, w_ada[layer], b_ada[layer][None, :])
    mod3 = mod.reshape(SUBLANES, 1, 6 * d)

    w = w_in[layer]
    o_z, o_xbc, o_dt = 0, SSD_INNER, SSD_INNER + CONV_DIM
    o_q = o_dt + 2 * SSD_HEADS
    o_k, o_v = o_q + ATTN_WIDTH, o_q + ATTN_WIDTH + KV_WIDTH
    q_perm = np.concatenate([np.r_[j * HEAD_DIM:(j + 1) * HEAD_DIM, (j + 4) * HEAD_DIM:(j + 5) * HEAD_DIM]
                             for j in range(Q_PER_KV)])
    w_q = w[:, o_q:o_q + ATTN_WIDTH].reshape(d, ATTN_KV_HEADS, Q_PER_KV, HEAD_DIM).transpose(0, 2, 1, 3)
    w_q = w_q.reshape(d, ATTN_WIDTH)
    w_cat = jnp.concatenate([w[:, o_z:o_z + SSD_INNER], w[:, o_xbc:o_xbc + CONV_DIM], w_q,
                             w[:, o_k:o_k + KV_WIDTH], w[:, o_v:o_v + KV_WIDTH]], axis=1).astype(BF16)
    w_dtT = w[:, o_dt:o_dt + 2 * SSD_HEADS].T.astype(BF16)

    seq_blocks = seq // ROWS_IN
    cos, sin = _rope_tables(seq)
    bias_rows = jnp.broadcast_to(dt_bias[layer].reshape(2 * SSD_HEADS, 1), (2 * SSD_HEADS, LANES))
    alog_rows = jnp.broadcast_to(a_log[layer].reshape(2 * SSD_HEADS, 1), (2 * SSD_HEADS, LANES))
    segs_x = [("z", _COL_Z, SSD_INNER), ("xbc", _COL_XBC, CONV_DIM), ("dtT", 0, 0),
              ("q", _COL_Q, ATTN_WIDTH), ("k", _COL_K, KV_WIDTH), ("v", _COL_V, KV_WIDTH)]
    z_x, xbc_x, dt_x, pre_x, suf_x, q_x, k_x, v_x = _inproj(
        x.reshape(t, d), mod3, norm1[layer][None, :], w_cat, w_dtT, bias_rows, alog_rows, (cos, sin),
        rows=ROWS_IN, mod_row_fn=lambda i: i // seq_blocks, segs=segs_x, seq_blocks=seq_blocks)
    segs_c = [("xbc", _COL_XBC, CONV_DIM), ("dtT", 0, 0), ("k", _COL_K, KV_WIDTH), ("v", _COL_V, KV_WIDTH)]
    xbc_c, dt_c, pre_c, suf_c, k_c, v_c = _inproj(
        ctx.reshape(bt * n_ctx, d), mod3, norm1[layer][None, :], w_cat, w_dtT, bias_rows, alog_rows, None,
        rows=n_ctx, mod_row_fn=lambda i: bt, segs=segs_c, seq_blocks=1)

    cw = jnp.concatenate([conv_w[layer], jnp.zeros((SUBLANES - CONV_WIDTH, CONV_DIM), F32)], axis=0)
    cb = conv_b[layer][None, :]
    dskip_e = jnp.repeat(d_skip[layer], SSD_HEAD_DIM)[None, :]
    consts = (cw, cb, dskip_e, ssd_norm[layer][None, :], _head_expand_mat())

    zeros_state = jnp.zeros((bt, SSD_STATE, SSD_INNER), F32)
    z_dummy = jnp.zeros((bt, n_ctx, SSD_INNER), F32)
    h_cf, h_cb = _ssd(xbc_c.reshape(bt, n_ctx, CONV_DIM), (dt_c, pre_c, suf_c), z_dummy, consts,
                      zeros_state, zeros_state, rows=n_ctx, need_y=False)
    ssd_x, _, _ = _ssd(xbc_x.reshape(bt, seq, CONV_DIM), (dt_x, pre_x, suf_x), z_x.reshape(bt, seq, SSD_INNER),
                       consts, h_cf, h_cb, rows=ROWS_SSD, need_y=True)

    attn_x = _attention(attn_sinks[layer].astype(F32) * LOG2E,
                        q_x.reshape(bt, seq, ATTN_WIDTH), k_x.reshape(bt, seq, KV_WIDTH),
                        v_x.reshape(bt, seq, KV_WIDTH), k_c.reshape(bt, n_ctx, KV_WIDTH),
                        v_c.reshape(bt, n_ctx, KV_WIDTH))

    wo = w_out[layer]
    wa = wo[:SSD_INNER].astype(BF16)
    wb = wo[SSD_INNER:].reshape(ATTN_KV_HEADS, Q_PER_KV, HEAD_DIM, d).transpose(1, 0, 2, 3)
    wb = wb.reshape(ATTN_WIDTH, d).astype(BF16)
    n_pad = ROUTE_ROWS - N_GROUPS - N_EXPERTS
    wr = jnp.concatenate([w_group[layer], w_expert[layer], jnp.zeros((d, n_pad), F32)], axis=1).T.astype(BF16)
    br = jnp.concatenate([b_group[layer], b_expert[layer], jnp.zeros((n_pad,), F32)])
    br = jnp.broadcast_to(br[:, None], (ROUTE_ROWS, ROWS_OUT))
    tri = jnp.asarray(np.triu(np.ones((ROWS_OUT, ROWS_OUT), np.float32), 1), BF16)
    x1, h2, ids, wcol, cnt = _outproj(
        ssd_x.reshape(t, SSD_INNER), attn_x.reshape(t, ATTN_WIDTH), x.reshape(t, d), mod3, norm2[layer][None, :],
        wa, wb, wr, br, tri, rows=ROWS_OUT, seq_blocks=seq // ROWS_OUT)

    counts = cnt[N_GROUPS:N_GROUPS + N_EXPERTS, 0].astype(jnp.int32)
    padded = (counts + MOE_BLK - 1) // MOE_BLK * MOE_BLK
    ends = jnp.cumsum(padded)
    pstart = ends - padded
    nblocks = (2 * t) // MOE_BLK + N_EXPERTS
    blk_start = jnp.arange(nblocks, dtype=jnp.int32) * MOE_BLK
    eids = jnp.arange(N_EXPERTS, dtype=jnp.int32)
    active = counts > 0
    last_active = jnp.max(jnp.where(active, eids, 0))
    blk_e = jnp.minimum(jnp.sum((ends[None, :] <= blk_start[:, None]).astype(jnp.int32), axis=1), last_active)
    nused = (ends[-1] // MOE_BLK).astype(jnp.int32).reshape(1)
    eord = jnp.cumsum(active.astype(jnp.int32)) - active.astype(jnp.int32)
    later = active[None, :] & (eids[None, :] > eids[:, None])
    enext = jnp.min(jnp.where(later, eids[None, :], N_EXPERTS), axis=1)
    enext = jnp.where(enext == N_EXPERTS, -1, enext).astype(jnp.int32)
    code = ids[0:2].reshape(-1)
    e_id = lax.shift_right_logical(code, RANK_BITS)
    onehot = e_id[:, None] == jnp.arange(N_EXPERTS, dtype=jnp.int32)[None, :]
    dst = jnp.sum(jnp.where(onehot, pstart[None, :], 0), axis=1) + (code & ((1 << RANK_BITS) - 1))

    y_buf = _experts(blk_e, nused, pstart, counts, eord, enext, dst, h2, w_gate[layer], w_up[layer],
                     w_down[layer], nblocks=nblocks, n_tok=t)
    out = _combine(dst, y_buf, x1, wcol, mod3, norm_final[None, :], rows=ROWS_CMB, seq_blocks=seq // ROWS_CMB)
    return dict(out=out.reshape(bt, seq, d), mod=mod, z=z_x, xbc=xbc_x, dt=dt_x, q=q_x, k=k_x, v=v_x,
                k_c=k_c, v_c=v_c, h_cf=h_cf, h_cb=h_cb, ssd=ssd_x, attn=attn_x, x1=x1, h2=h2, ids=ids,
                wcol=wcol, cnt=cnt, y_buf=y_buf, pstart=pstart, q_perm=q_perm)
```
